```python
import jax, jax.numpy as jnp
from jax import lax
import numpy as np

D_MODEL = 1024
BATCH = 32
SEQ = 2048
DEPTH = 1

HEAD_DIM = 64
N_Q_HEADS = 8
N_KV_HEADS = 2
Q_PER_KV = N_Q_HEADS // N_KV_HEADS
ATTN_WIDTH = N_Q_HEADS * HEAD_DIM
KV_WIDTH = N_KV_HEADS * HEAD_DIM
WINDOW = 128
BLOCK = WINDOW
ROPE_THETA = 10000.0
POOL_WINDOWS = (2, 4, 8, 16)
N_POOL_GROUPS = len(POOL_WINDOWS)
POOL_WIDTH = D_MODEL - ATTN_WIDTH
POOL_GROUP_DIM = POOL_WIDTH // N_POOL_GROUPS
MIX_WIDTH = ATTN_WIDTH + POOL_WIDTH
IN_WIDTH = ATTN_WIDTH + 2 * KV_WIDTH + POOL_WIDTH
D_FF = 4 * D_MODEL
EPS = 1e-6

kernel_name = "hymba_swa_sink_multiscale_pool_block"


def _rmsnorm(x, g):
    xf = x.astype(jnp.float32)
    y = xf * lax.rsqrt(jnp.mean(xf * xf, axis=-1, keepdims=True) + EPS)
    return (y * g.astype(jnp.float32)).astype(x.dtype)


def _rope(x, pos):
    half = HEAD_DIM // 2
    inv_freq = ROPE_THETA ** (-jnp.arange(half, dtype=jnp.float32) / half)
    ang = pos.astype(jnp.float32)[:, None] * inv_freq[None, :]
    cos = jnp.cos(ang)[None, :, None, :]
    sin = jnp.sin(ang)[None, :, None, :]
    xf = x.astype(jnp.float32)
    x1, x2 = xf[..., :half], xf[..., half:]
    out = jnp.concatenate([x1 * cos - x2 * sin, x2 * cos + x1 * sin], axis=-1)
    return out.astype(x.dtype)


def _swa_with_sinks(q, k, v, sinks):
    B, S = q.shape[0], q.shape[1]
    nb = S // BLOCK
    qb = q.reshape(B, nb, BLOCK, N_KV_HEADS, Q_PER_KV, HEAD_DIM)
    kb = k.reshape(B, nb, BLOCK, N_KV_HEADS, HEAD_DIM)
    vb = v.reshape(B, nb, BLOCK, N_KV_HEADS, HEAD_DIM)

    def with_prev(t):
        prev = jnp.pad(t[:, :-1], ((0, 0), (1, 0), (0, 0), (0, 0), (0, 0)))
        return jnp.concatenate([prev, t], axis=2)

    kw, vw = with_prev(kb), with_prev(vb)
    scale = HEAD_DIM ** -0.5
    logits = jnp.einsum('bnqkgd,bnskd->bnkgqs', qb, kw,
                        preferred_element_type=jnp.float32) * scale
    blk = jnp.arange(nb)[:, None, None]
    qi = jnp.arange(BLOCK)[None, :, None]
    kj = jnp.arange(2 * BLOCK)[None, None, :]
    rel = BLOCK + qi - kj
    kpos = (blk - 1) * BLOCK + kj
    mask = (rel >= 0) & (rel < WINDOW) & (kpos >= 0)
    logits = jnp.where(mask[None, :, None, None], logits, -jnp.inf)
    sink = sinks.astype(jnp.float32).reshape(N_KV_HEADS, Q_PER_KV)[None, None, :, :, None, None]
    m = jnp.maximum(jnp.max(logits, axis=-1, keepdims=True), sink)
    p = jnp.exp(logits - m)
    denom = jnp.sum(p, axis=-1, keepdims=True) + jnp.exp(sink - m)
    probs = (p / denom).astype(v.dtype)
    out = jnp.einsum('bnkgqs,bnskd->bnqkgd', probs, vw)
    return out.reshape(B, S, ATTN_WIDTH)


def _multiscale_pool(u, w_pool, pool_scale):
    B, S = u.shape[0], u.shape[1]
    ug = u.reshape(B, S, N_POOL_GROUPS, POOL_GROUP_DIM).astype(jnp.float32)
    c = jnp.pad(jnp.cumsum(ug, axis=1), ((0, 0), (1, 0), (0, 0), (0, 0)))
    t = jnp.arange(S)
    means = []
    for g, w in enumerate(POOL_WINDOWS):
        cg = c[:, :, g]
        lagged = jnp.pad(cg[:, :S + 1 - w], ((0, 0), (w - 1, 0), (0, 0)))
        cnt = jnp.minimum(t + 1, w).astype(jnp.float32)[None, :, None]
        means.append((cg[:, 1:] - lagged) / cnt)
    mean = jnp.stack(means, axis=2)
    d = (mean - ug).astype(u.dtype)
    y = jnp.einsum('bsgc,gcd->bsgd', d, w_pool)
    return y.reshape(B, S, POOL_WIDTH) * pool_scale


def _fwd_setup_inputs(seed: int = 0) -> dict:
    key = jax.random.key(seed)
    ks = jax.random.split(key, 12)
    f32 = jnp.float32
    x = jax.random.normal(ks[0], (BATCH, SEQ, D_MODEL), f32)
    attn_norm_g = 1.0 + 0.02 * jax.random.normal(ks[1], (DEPTH, D_MODEL), f32)
    w_in = jax.random.normal(ks[2], (DEPTH, D_MODEL, IN_WIDTH), f32) * D_MODEL ** -0.5
    attn_sinks = 0.5 * jax.random.normal(ks[3], (DEPTH, N_Q_HEADS), f32)
    w_pool = jax.random.normal(ks[4], (DEPTH, N_POOL_GROUPS, POOL_GROUP_DIM, POOL_GROUP_DIM), f32) * POOL_GROUP_DIM ** -0.5
    pool_scale = 1.0 + 0.1 * jax.random.normal(ks[5], (DEPTH, POOL_WIDTH), f32)
    w_out = jax.random.normal(ks[6], (DEPTH, MIX_WIDTH, D_MODEL), f32) * MIX_WIDTH ** -0.5
    mlp_norm_g = 1.0 + 0.02 * jax.random.normal(ks[7], (DEPTH, D_MODEL), f32)
    w_up = jax.random.normal(ks[8], (DEPTH, D_MODEL, D_FF), f32) * D_MODEL ** -0.5
    w_down = jax.random.normal(ks[9], (DEPTH, D_FF, D_MODEL), f32) * D_FF ** -0.5
    final_norm_g = 1.0 + 0.02 * jax.random.normal(ks[10], (D_MODEL,), f32)
    return {"x": x, "attn_norm_g": attn_norm_g, "w_in": w_in, "attn_sinks": attn_sinks,
            "w_pool": w_pool, "pool_scale": pool_scale, "w_out": w_out,
            "mlp_norm_g": mlp_norm_g, "w_up": w_up, "w_down": w_down,
            "final_norm_g": final_norm_g}


def _fwd_reference(x, attn_norm_g, w_in, attn_sinks, w_pool, pool_scale, w_out,
              mlp_norm_g, w_up, w_down, final_norm_g):
    B, S = x.shape[0], x.shape[1]
    pos = jnp.arange(S)
    for l in range(DEPTH):
        h = _rmsnorm(x, attn_norm_g[l])
        proj = h @ w_in[l]
        q = proj[..., :ATTN_WIDTH].reshape(B, S, N_Q_HEADS, HEAD_DIM)
        k = proj[..., ATTN_WIDTH:ATTN_WIDTH + KV_WIDTH].reshape(B, S, N_KV_HEADS, HEAD_DIM)
        v = proj[..., ATTN_WIDTH + KV_WIDTH:ATTN_WIDTH + 2 * KV_WIDTH].reshape(B, S, N_KV_HEADS, HEAD_DIM)
        u = proj[..., ATTN_WIDTH + 2 * KV_WIDTH:]
        q, k = _rope(q, pos), _rope(k, pos)
        attn = _swa_with_sinks(q, k, v, attn_sinks[l])
        pool = _multiscale_pool(u, w_pool[l], pool_scale[l])
        x = x + jnp.concatenate([attn, pool], axis=-1) @ w_out[l]
        h = _rmsnorm(x, mlp_norm_g[l])
        x = x + jnp.square(jax.nn.relu(h @ w_up[l])) @ w_down[l]
    return _rmsnorm(x, final_norm_g)


import jax as _jax
import jax.numpy as _jnp

TWIN_FORMAT = 'train_step'
FWD_PARAMS = ['x', 'attn_norm_g', 'w_in', 'attn_sinks', 'w_pool', 'pool_scale', 'w_out', 'mlp_norm_g', 'w_up', 'w_down', 'final_norm_g']
TWIN_WEIGHTS = ['attn_norm_g', 'w_in', 'attn_sinks', 'w_pool', 'pool_scale', 'w_out', 'mlp_norm_g', 'w_up', 'w_down', 'final_norm_g']
TWIN_DIFF_INPUT = 'x'
TWIN_INPUTS = ['x', 'attn_norm_g', 'w_in', 'attn_sinks', 'w_pool', 'pool_scale', 'w_out', 'mlp_norm_g', 'w_up', 'w_down', 'final_norm_g', 'loss_target', 'm_attn_norm_g', 'm_w_in', 'm_attn_sinks', 'm_w_pool', 'm_pool_scale', 'm_w_out', 'm_mlp_norm_g', 'm_w_up', 'm_w_down', 'm_final_norm_g', 'v_attn_norm_g', 'v_w_in', 'v_attn_sinks', 'v_w_pool', 'v_pool_scale', 'v_w_out', 'v_mlp_norm_g', 'v_w_up', 'v_w_down', 'v_final_norm_g']
TWIN_OUTPUTS = ['loss', 'grad_x', 'grad_attn_norm_g', 'grad_w_in', 'grad_attn_sinks', 'grad_w_pool', 'grad_pool_scale', 'grad_w_out', 'grad_mlp_norm_g', 'grad_w_up', 'grad_w_down', 'grad_final_norm_g', 'delta_attn_norm_g', 'delta_w_in', 'delta_attn_sinks', 'delta_w_pool', 'delta_pool_scale', 'delta_w_out', 'delta_mlp_norm_g', 'delta_w_up', 'delta_w_down', 'delta_final_norm_g', 'new_m_attn_norm_g', 'new_m_w_in', 'new_m_attn_sinks', 'new_m_w_pool', 'new_m_pool_scale', 'new_m_w_out', 'new_m_mlp_norm_g', 'new_m_w_up', 'new_m_w_down', 'new_m_final_norm_g', 'new_v_attn_norm_g', 'new_v_w_in', 'new_v_attn_sinks', 'new_v_w_pool', 'new_v_pool_scale', 'new_v_w_out', 'new_v_mlp_norm_g', 'new_v_w_up', 'new_v_w_down', 'new_v_final_norm_g']
TWIN_LEAF_KINDS = {'loss': 'loss', 'grad_x': 'grad_x', 'grad_attn_norm_g': 'grad_w', 'grad_w_in': 'grad_w', 'grad_attn_sinks': 'grad_w', 'grad_w_pool': 'grad_w', 'grad_pool_scale': 'grad_w', 'grad_w_out': 'grad_w', 'grad_mlp_norm_g': 'grad_w', 'grad_w_up': 'grad_w', 'grad_w_down': 'grad_w', 'grad_final_norm_g': 'grad_w', 'delta_attn_norm_g': 'delta_w', 'delta_w_in': 'delta_w', 'delta_attn_sinks': 'delta_w', 'delta_w_pool': 'delta_w', 'delta_pool_scale': 'delta_w', 'delta_w_out': 'delta_w', 'delta_mlp_norm_g': 'delta_w', 'delta_w_up': 'delta_w', 'delta_w_down': 'delta_w', 'delta_final_norm_g': 'delta_w', 'new_m_attn_norm_g': 'new_m', 'new_m_w_in': 'new_m', 'new_m_attn_sinks': 'new_m', 'new_m_w_pool': 'new_m', 'new_m_pool_scale': 'new_m', 'new_m_w_out': 'new_m', 'new_m_mlp_norm_g': 'new_m', 'new_m_w_up': 'new_m', 'new_m_w_down': 'new_m', 'new_m_final_norm_g': 'new_m', 'new_v_attn_norm_g': 'new_v', 'new_v_w_in': 'new_v', 'new_v_attn_sinks': 'new_v', 'new_v_w_pool': 'new_v', 'new_v_pool_scale': 'new_v', 'new_v_w_out': 'new_v', 'new_v_mlp_norm_g': 'new_v', 'new_v_w_up': 'new_v', 'new_v_w_down': 'new_v', 'new_v_final_norm_g': 'new_v'}


def _forward(args):
    return _fwd_reference(*[args[k] for k in FWD_PARAMS])


def _output_shape():
    out = _jax.eval_shape(lambda: _forward(_fwd_setup_inputs(0)))
    return out.shape, out.dtype

N_MICROBATCH = 1
ADAM_LR = 0.001
ADAM_B1 = 0.9
ADAM_B2 = 0.999
ADAM_EPS = 1e-08
ADAM_WD = 0.01
ADAM_STEP = 10
PER_EXAMPLE_BATCH_AXIS = {'x': 0, 'loss_target': 0}
SHARED_INPUTS = []
_WEIGHT_DTYPES = {'attn_norm_g': _jnp.float32, 'w_in': _jnp.float32, 'attn_sinks': _jnp.float32, 'w_pool': _jnp.float32, 'pool_scale': _jnp.float32, 'w_out': _jnp.float32, 'mlp_norm_g': _jnp.float32, 'w_up': _jnp.float32, 'w_down': _jnp.float32, 'final_norm_g': _jnp.float32}
MOMENT_SCALE = {'attn_norm_g': 1.483639e-01, 'w_in': 1.353648e-01, 'attn_sinks': 2.551691e-02, 'w_pool': 2.027955e-01, 'pool_scale': 2.032771e-01, 'w_out': 1.457861e-01, 'mlp_norm_g': 2.026141e-01, 'w_up': 1.045981e-01, 'w_down': 1.995577e-01, 'final_norm_g': 6.445681e+01}


def _to_microbatches(a, axis):
    t = _jnp.moveaxis(a, axis, 0)
    t = t.reshape((N_MICROBATCH, t.shape[0] // N_MICROBATCH) + t.shape[1:])
    return _jnp.moveaxis(t, 1, axis + 1)


def setup_inputs(seed: int = 0) -> dict:
    inp = _fwd_setup_inputs(seed)
    key = _jax.random.fold_in(_jax.random.key(seed), 7919)
    shape, _ = _output_shape()
    out = dict(inp)
    out["loss_target"] = _jax.random.normal(_jax.random.fold_in(key, 0), shape, _jnp.float32)
    for i, name in enumerate(TWIN_WEIGHTS):
        w = inp[name].astype(_jnp.float32)
        if MOMENT_SCALE is None:
            s = _jnp.sqrt(_jnp.mean(_jnp.square(w)) + 1e-30)
        else:
            s = MOMENT_SCALE[name]
        km, kv = _jax.random.split(_jax.random.fold_in(key, i + 1))
        out[name] = w
        out["m_" + name] = s * _jax.random.normal(km, w.shape, _jnp.float32)
        out["v_" + name] = (s * s) * _jax.random.uniform(kv, w.shape, _jnp.float32, 0.5, 1.5)
    if N_MICROBATCH > 1:
        for name, axis in PER_EXAMPLE_BATCH_AXIS.items():
            out[name] = _to_microbatches(out[name], axis)
    return {'x': out['x'], 'attn_norm_g': out['attn_norm_g'], 'w_in': out['w_in'], 'attn_sinks': out['attn_sinks'], 'w_pool': out['w_pool'], 'pool_scale': out['pool_scale'], 'w_out': out['w_out'], 'mlp_norm_g': out['mlp_norm_g'], 'w_up': out['w_up'], 'w_down': out['w_down'], 'final_norm_g': out['final_norm_g'], 'loss_target': out['loss_target'], 'm_attn_norm_g': out['m_attn_norm_g'], 'm_w_in': out['m_w_in'], 'm_attn_sinks': out['m_attn_sinks'], 'm_w_pool': out['m_w_pool'], 'm_pool_scale': out['m_pool_scale'], 'm_w_out': out['m_w_out'], 'm_mlp_norm_g': out['m_mlp_norm_g'], 'm_w_up': out['m_w_up'], 'm_w_down': out['m_w_down'], 'm_final_norm_g': out['m_final_norm_g'], 'v_attn_norm_g': out['v_attn_norm_g'], 'v_w_in': out['v_w_in'], 'v_attn_sinks': out['v_attn_sinks'], 'v_w_pool': out['v_w_pool'], 'v_pool_scale': out['v_pool_scale'], 'v_w_out': out['v_w_out'], 'v_mlp_norm_g': out['v_mlp_norm_g'], 'v_w_up': out['v_w_up'], 'v_w_down': out['v_w_down'], 'v_final_norm_g': out['v_final_norm_g']}


def _loss(weights, diff, rest, loss_target):
    with _jax.named_scope("forward"):
        args = {**rest, TWIN_DIFF_INPUT: diff, **{k: w.astype(_WEIGHT_DTYPES[k]) for k, w in weights.items()}}
        y = _forward(args)
    with _jax.named_scope("loss_head"):
        err = _jnp.square(y.astype(_jnp.float32) - loss_target)
        return 0.5 * _jnp.sum(_jnp.mean(err, axis=-1)) if err.ndim else 0.5 * err


def _adamw(w, g, m, v):
    m = ADAM_B1 * m + (1.0 - ADAM_B1) * g
    v = ADAM_B2 * v + (1.0 - ADAM_B2) * _jnp.square(g)
    m_hat = m / (1.0 - ADAM_B1 ** ADAM_STEP)
    v_hat = v / (1.0 - ADAM_B2 ** ADAM_STEP)
    delta = -ADAM_LR * (m_hat / (_jnp.sqrt(v_hat) + ADAM_EPS) + ADAM_WD * w)
    return delta, m, v


def reference(x, attn_norm_g, w_in, attn_sinks, w_pool, pool_scale, w_out, mlp_norm_g, w_up, w_down, final_norm_g, loss_target, m_attn_norm_g, m_w_in, m_attn_sinks, m_w_pool, m_pool_scale, m_w_out, m_mlp_norm_g, m_w_up, m_w_down, m_final_norm_g, v_attn_norm_g, v_w_in, v_attn_sinks, v_w_pool, v_pool_scale, v_w_out, v_mlp_norm_g, v_w_up, v_w_down, v_final_norm_g):
    given = dict(x=x, attn_norm_g=attn_norm_g, w_in=w_in, attn_sinks=attn_sinks, w_pool=w_pool, pool_scale=pool_scale, w_out=w_out, mlp_norm_g=mlp_norm_g, w_up=w_up, w_down=w_down, final_norm_g=final_norm_g, loss_target=loss_target, m_attn_norm_g=m_attn_norm_g, m_w_in=m_w_in, m_attn_sinks=m_attn_sinks, m_w_pool=m_w_pool, m_pool_scale=m_pool_scale, m_w_out=m_w_out, m_mlp_norm_g=m_mlp_norm_g, m_w_up=m_w_up, m_w_down=m_w_down, m_final_norm_g=m_final_norm_g, v_attn_norm_g=v_attn_norm_g, v_w_in=v_w_in, v_attn_sinks=v_attn_sinks, v_w_pool=v_w_pool, v_pool_scale=v_pool_scale, v_w_out=v_w_out, v_mlp_norm_g=v_mlp_norm_g, v_w_up=v_w_up, v_w_down=v_w_down, v_final_norm_g=v_final_norm_g)
    weights = {n: given[n] for n in TWIN_WEIGHTS}
    shared = {n: given[n] for n in SHARED_INPUTS}
    per_example = {n: given[n] for n in ['x']}
    grad_fn = _jax.value_and_grad(_loss, argnums=(0, 1))

    def one_microbatch(ex, loss_target):
        ex = dict(ex)
        diff = ex.pop(TWIN_DIFF_INPUT)
        return grad_fn(weights, diff, {**shared, **ex}, loss_target)

    if N_MICROBATCH == 1:
        loss, (grad_w, grad_x) = one_microbatch(per_example, given["loss_target"])
    else:
        def body(carry, xs):
            loss_sum, grad_sum = carry
            l_k, (gw_k, gx_k) = one_microbatch(xs[0], xs[1])
            with _jax.named_scope("update"):
                return (loss_sum + l_k, _jax.tree.map(_jnp.add, grad_sum, gw_k)), gx_k

        init = (_jnp.zeros((), _jnp.float32), _jax.tree.map(_jnp.zeros_like, weights))
        (loss, grad_w), grad_x = _jax.lax.scan(body, init, (per_example, given["loss_target"]))
    with _jax.named_scope("update"):
        delta_w, new_m, new_v = {}, {}, {}
        for n in TWIN_WEIGHTS:
            delta_w[n], new_m[n], new_v[n] = _adamw(weights[n], grad_w[n], given["m_" + n], given["v_" + n])
    return (loss, grad_x, *[grad_w[n] for n in TWIN_WEIGHTS], *[delta_w[n] for n in TWIN_WEIGHTS],
            *[new_m[n] for n in TWIN_WEIGHTS], *[new_v[n] for n in TWIN_WEIGHTS])
```

```python
import jax
import jax.numpy as jnp
from jax import lax
from jax.experimental import pallas as pl
from jax.experimental.pallas import tpu as pltpu

F32 = jnp.float32
BF16 = jnp.bfloat16
MXU_DTYPE = jnp.bfloat16

D_MODEL = 1024
HEAD_DIM = 64
N_Q_HEADS = 8
N_KV_HEADS = 2
Q_PER_KV = N_Q_HEADS // N_KV_HEADS
ATTN_W = N_Q_HEADS * HEAD_DIM
KV_W = N_KV_HEADS * HEAD_DIM
BLK = 128
POOL_WINDOWS = (2, 4, 8, 16)
N_POOL = len(POOL_WINDOWS)
POOL_W = D_MODEL - ATTN_W
POOL_G = POOL_W // N_POOL
IN_W = ATTN_W + 2 * KV_W + POOL_W
D_FF = 4 * D_MODEL
EPS = 1e-6
ROPE_THETA = 10000.0
N_CHIPS = 4
IN_SHARD = IN_W // N_CHIPS
OUT_SHARD = D_MODEL // N_CHIPS
FF_SHARD = D_FF // N_CHIPS
LANES = 128

ADAM_LR = 0.001
ADAM_B1 = 0.9
ADAM_B2 = 0.999
ADAM_EPS = 1e-08
ADAM_WD = 0.01
ADAM_STEP = 10

VMEM_LIMIT = 56 * 1024 * 1024
MESH = pl.DeviceIdType.MESH


def _cp(**kw):
    return pltpu.CompilerParams(vmem_limit_bytes=VMEM_LIMIT, **kw)


def _mm(a, b):
    return jnp.dot(a.astype(MXU_DTYPE), b.astype(MXU_DTYPE), preferred_element_type=F32)


def _mm_nt(a, b):
    return lax.dot_general(a.astype(MXU_DTYPE), b.astype(MXU_DTYPE), (((1,), (1,)), ((), ())),
                           preferred_element_type=F32)


def _mm_tn(a, b):
    return lax.dot_general(a.astype(MXU_DTYPE), b.astype(MXU_DTYPE), (((0,), (0,)), ((), ())),
                           preferred_element_type=F32)


def _resident(shape):
    nd = len(shape)
    return pl.BlockSpec(shape, lambda *_: (0,) * nd, pipeline_mode=pl.Buffered(1))


def _const(shape):
    nd = len(shape)
    return pl.BlockSpec(shape, lambda *_: (0,) * nd)


def _rope_tables(seq):
    half = HEAD_DIM // 2
    inv_freq = ROPE_THETA ** (-jnp.arange(half, dtype=F32) / half)
    ang = jnp.arange(seq, dtype=F32)[:, None] * inv_freq[None, :]
    cos, sin = jnp.cos(ang), jnp.sin(ang)
    cos_t = jnp.concatenate([cos, cos, cos, cos], axis=1)
    sin_t = jnp.concatenate([-sin, sin, -sin, sin], axis=1)
    return cos_t, sin_t


def _swap_halves(xc):
    lane = lax.broadcasted_iota(jnp.int32, xc.shape, 1)
    return jnp.where((lane & 63) < 32, pltpu.roll(xc, 96, 1), pltpu.roll(xc, 32, 1))


def _gather_weights(w_in, w_out, w_up, w_down):
    shards = (w_in, w_out, w_up, w_down)
    nw = len(shards)

    def body(*refs):
        srcs = refs[0:nw]
        outs = refs[nw:2 * nw]
        stages = refs[2 * nw:3 * nw]
        send1, recv1, send2, recv2, lsem = refs[3 * nw:]
        x, y, c = lax.axis_index("x"), lax.axis_index("y"), lax.axis_index("c")
        j = 2 * x + y
        chips = [(1 - x, y), (x, 1 - y), (1 - x, 1 - y)]
        for k in range(nw):
            stages[k][...] = srcs[k][...].astype(BF16)
        local = [pltpu.make_async_copy(stages[k], outs[k].at[j], lsem.at[k]) for k in range(nw)]
        for cp in local:
            cp.start()

        def piece(k, chip, half):
            rows = shards[k].shape[0] // 2
            return outs[k].at[2 * chip[0] + chip[1], pl.ds(half * rows, rows)]

        first, passed = [], []
        for k in range(nw):
            rows = shards[k].shape[0] // 2
            for n, chip in enumerate(chips):
                cp = pltpu.make_async_remote_copy(
                    src_ref=stages[k].at[pl.ds(c * rows, rows)], dst_ref=piece(k, (x, y), c),
                    send_sem=send1.at[k * 3 + n], recv_sem=recv1.at[k * 3 + n],
                    device_id=(chip[0], chip[1], c), device_id_type=MESH)
                cp.start()
                first.append(cp)
        for k in range(nw):
            for n, chip in enumerate(chips):
                got = piece(k, chip, c)
                pltpu.make_async_remote_copy(
                    src_ref=got, dst_ref=got, send_sem=send1.at[k * 3 + n], recv_sem=recv1.at[k * 3 + n],
                    device_id=(chip[0], chip[1], c), device_id_type=MESH).wait_recv()
                cp = pltpu.make_async_remote_copy(
                    src_ref=got, dst_ref=got, send_sem=send2.at[k * 3 + n], recv_sem=recv2.at[k * 3 + n],
                    device_id=(x, y, 1 - c), device_id_type=MESH)
                cp.start()
                passed.append(cp)
        for k in range(nw):
            for n, chip in enumerate(chips):
                got = piece(k, chip, 1 - c)
                pltpu.make_async_remote_copy(
                    src_ref=got, dst_ref=got, send_sem=send2.at[k * 3 + n], recv_sem=recv2.at[k * 3 + n],
                    device_id=(x, y, 1 - c), device_id_type=MESH).wait_recv()
        for cp in first + passed:
            cp.wait_send()
        for cp in local:
            cp.wait()

    hbm = pl.BlockSpec(memory_space=pltpu.HBM)
    vmem = pl.BlockSpec(memory_space=pltpu.VMEM)
    return pl.pallas_call(
        body, name="gather_weights",
        out_shape=tuple(jax.ShapeDtypeStruct((N_CHIPS,) + s.shape, BF16) for s in shards),
        in_specs=[vmem] * nw, out_specs=tuple([hbm] * nw),
        scratch_shapes=[pltpu.VMEM(s.shape, BF16) for s in shards] + [
            pltpu.SemaphoreType.DMA((3 * nw,)), pltpu.SemaphoreType.DMA((3 * nw,)),
            pltpu.SemaphoreType.DMA((3 * nw,)), pltpu.SemaphoreType.DMA((3 * nw,)),
            pltpu.SemaphoreType.DMA((nw,))],
        compiler_params=_cp(),
    )(*shards)


def _concat_w_in(w_in4):
    def body(s_ref, o_ref):
        o_ref[...] = jnp.concatenate([s_ref[j] for j in range(N_CHIPS)], axis=1)

    return pl.pallas_call(body, name="concat_w_in", out_shape=jax.ShapeDtypeStruct((D_MODEL, IN_W), w_in4.dtype),
                          compiler_params=_cp())(w_in4)


def _fwd_inproj(x, g1, w_in, cos_t, sin_t, tm):
    t_tok = x.shape[0]
    seq = cos_t.shape[0]
    per_seq = seq // tm

    def body(x_ref, g_ref, w_ref, cos_ref, sin_ref, q_ref, k_ref, v_ref, u_ref):
        xv = x_ref[...]
        r = lax.rsqrt(jnp.mean(xv * xv, axis=-1, keepdims=True) + EPS)
        h = (xv * r) * g_ref[...]
        proj = _mm(h, w_ref[...])
        cos, sin = cos_ref[...], sin_ref[...]
        for cidx in range((ATTN_W + KV_W) // LANES):
            xc = proj[:, cidx * LANES:(cidx + 1) * LANES]
            rot = xc * cos + _swap_halves(xc) * sin
            if cidx < ATTN_W // LANES:
                q_ref[:, cidx * LANES:(cidx + 1) * LANES] = (rot * (HEAD_DIM ** -0.5)).astype(q_ref.dtype)
            else:
                k_ref[...] = rot.astype(k_ref.dtype)
        v_ref[...] = proj[:, ATTN_W + KV_W:ATTN_W + 2 * KV_W].astype(v_ref.dtype)
        u_ref[...] = proj[:, ATTN_W + 2 * KV_W:]

    row = lambda w: pl.BlockSpec((tm, w), lambda i: (i, 0))
    tab = pl.BlockSpec((tm, LANES), lambda i: (i % per_seq, 0))
    return pl.pallas_call(
        body, name="fwd_inproj", grid=(t_tok // tm,),
        in_specs=[row(D_MODEL), _const((1, D_MODEL)), _resident((D_MODEL, IN_W)), tab, tab],
        out_specs=(row(ATTN_W), row(KV_W), row(KV_W), row(POOL_W)),
        out_shape=(jax.ShapeDtypeStruct((t_tok, ATTN_W), MXU_DTYPE), jax.ShapeDtypeStruct((t_tok, KV_W), MXU_DTYPE),
                   jax.ShapeDtypeStruct((t_tok, KV_W), MXU_DTYPE), jax.ShapeDtypeStruct((t_tok, POOL_W), F32)),
        compiler_params=_cp(dimension_semantics=("parallel",)),
    )(x, g1, w_in, cos_t, sin_t)


def _attn_mask(i):
    a = lax.broadcasted_iota(jnp.int32, (Q_PER_KV * BLK, 2 * BLK), 0) & (BLK - 1)
    b = lax.broadcasted_iota(jnp.int32, (Q_PER_KV * BLK, 2 * BLK), 1)
    first_key = jnp.where(i > 0, 0, BLK)
    return (b > a) & (b <= a + BLK) & (b >= first_key)


def _stack_heads(ref, r0, g):
    return jnp.concatenate(
        [ref[pl.ds(r0, BLK), (Q_PER_KV * g + h) * HEAD_DIM:(Q_PER_KV * g + h + 1) * HEAD_DIM] for h in range(Q_PER_KV)],
        axis=0)


def _kv_window(ref, p0, r0, g):
    sl = slice(g * HEAD_DIM, (g + 1) * HEAD_DIM)
    return jnp.concatenate([ref[pl.ds(p0, BLK), sl], ref[pl.ds(r0, BLK), sl]], axis=0)


def _sink_col(s_ref, g):
    return jnp.concatenate([jnp.full((BLK, 1), s_ref[Q_PER_KV * g + h], F32) for h in range(Q_PER_KV)], axis=0)


def _softmax_with_sink(s, mask, sink):
    s = jnp.where(mask, s, -jnp.inf)
    m = jnp.maximum(jnp.max(s, axis=-1, keepdims=True), sink)
    p = jnp.exp(s - m)
    p_sink = jnp.exp(sink - m)
    inv = 1.0 / (jnp.sum(p, axis=-1, keepdims=True) + p_sink)
    return p * inv, p_sink * inv


def _shift_rows(x, k, seq):
    row = lax.broadcasted_iota(jnp.int32, x.shape, 0)
    if k > 0:
        return jnp.where(row >= k, pltpu.roll(x, k, 0), 0.0)
    return jnp.where(row < seq + k, pltpu.roll(x, seq + k, 0), 0.0)


def _window_sum(x, w, seq, forward):
    s, k = x, 1
    while k < w:
        s = s + _shift_rows(s, -k if forward else k, seq)
        k *= 2
    return s


def _inv_count(seq, w):
    pos = lax.broadcasted_iota(jnp.int32, (seq, 1), 0)
    return 1.0 / jnp.minimum(pos + 1, w).astype(F32)


def _fwd_mix(q, k, v, u, sinks, w_pool, pool_scale, seq):
    t_tok = q.shape[0]
    nblk = seq // BLK

    def body(s_ref, q_ref, k_ref, v_ref, u_ref, wp_ref, sc_ref, o_ref):
        def blk(i, carry):
            r0 = pl.multiple_of(i * BLK, BLK)
            p0 = pl.multiple_of(jnp.maximum(i - 1, 0) * BLK, BLK)
            mask = _attn_mask(i)
            for g in range(N_KV_HEADS):
                kk, vv = _kv_window(k_ref, p0, r0, g), _kv_window(v_ref, p0, r0, g)
                s = _mm_nt(_stack_heads(q_ref, r0, g), kk)
                probs, _ = _softmax_with_sink(s, mask, _sink_col(s_ref, g))
                o = _mm(probs, vv)
                for h in range(Q_PER_KV):
                    c0 = (Q_PER_KV * g + h) * HEAD_DIM
                    o_ref[pl.ds(r0, BLK), c0:c0 + HEAD_DIM] = o[h * BLK:(h + 1) * BLK].astype(o_ref.dtype)
            return carry

        lax.fori_loop(0, nblk, blk, 0)
        for gi, w in enumerate(POOL_WINDOWS):
            sl = slice(gi * POOL_G, (gi + 1) * POOL_G)
            ug = u_ref[:, sl]
            d = _window_sum(ug, w, seq, False) * _inv_count(seq, w) - ug
            y = _mm(d, wp_ref[gi])
            o_ref[:, ATTN_W + gi * POOL_G:ATTN_W + (gi + 1) * POOL_G] = (y * sc_ref[:, sl]).astype(o_ref.dtype)

    row = lambda w: pl.BlockSpec((seq, w), lambda i: (i, 0))
    return pl.pallas_call(
        body, name="fwd_mix", grid=(t_tok // seq,),
        in_specs=[pl.BlockSpec(memory_space=pltpu.SMEM), row(ATTN_W), row(KV_W), row(KV_W), row(POOL_W),
                  _const((N_POOL, POOL_G, POOL_G)), _const((1, POOL_W))],
        out_specs=row(D_MODEL),
        out_shape=jax.ShapeDtypeStruct((t_tok, D_MODEL), MXU_DTYPE),
        compiler_params=_cp(dimension_semantics=("parallel",)),
    )(sinks, q, k, v, u, w_pool, pool_scale)


def _rms_bwd(dy_g, xn, r):
    return r * (dy_g - xn * jnp.mean(dy_g * xn, axis=-1, keepdims=True))


def _fwd_mlp_loss(x, mix, target, w_out, w_up4, w_down4, g2, gf, tm):
    t_tok = x.shape[0]

    def body(x_ref, mix_ref, tgt_ref, wo_ref, wu_ref, wd_ref, g2_ref, gf_ref,
             x1_ref, h2_ref, a_ref, dx2_ref, dx2b_ref, loss_ref, dgf_ref):
        @pl.when(pl.program_id(0) == 0)
        def _():
            loss_ref[...] = jnp.zeros_like(loss_ref)
            dgf_ref[...] = jnp.zeros_like(dgf_ref)

        x1 = x_ref[...] + _mm(mix_ref[...], wo_ref[...])
        x1_ref[...] = x1
        r2 = lax.rsqrt(jnp.mean(x1 * x1, axis=-1, keepdims=True) + EPS)
        h2 = ((x1 * r2) * g2_ref[...]).astype(MXU_DTYPE)
        h2_ref[...] = h2
        acc = jnp.zeros((tm, D_MODEL), F32)
        for j in range(N_CHIPS):
            a = _mm(h2, wu_ref[j])
            a_ref[:, j * FF_SHARD:(j + 1) * FF_SHARD] = a.astype(a_ref.dtype)
            acc = acc + _mm(jnp.square(jnp.maximum(a, 0.0)), wd_ref[j])
        x2 = x1 + acc
        r3 = lax.rsqrt(jnp.mean(x2 * x2, axis=-1, keepdims=True) + EPS)
        xn = x2 * r3
        gf_v = gf_ref[...]
        err = xn * gf_v - tgt_ref[...]
        part = jnp.sum(err * err) * (0.5 / D_MODEL)
        first = (lax.broadcasted_iota(jnp.int32, loss_ref.shape, 0) == 0) & (lax.broadcasted_iota(jnp.int32, loss_ref.shape, 1) == 0)
        loss_ref[...] += jnp.where(first, part, 0.0)
        dy = err * (1.0 / D_MODEL)
        dgf_ref[...] += jnp.sum(dy * xn, axis=0, keepdims=True)
        dx2 = _rms_bwd(dy * gf_v, xn, r3)
        dx2_ref[...] = dx2
        dx2b_ref[...] = dx2.astype(dx2b_ref.dtype)

    row = lambda w: pl.BlockSpec((tm, w), lambda i: (i, 0))
    return pl.pallas_call(
        body, name="fwd_mlp_loss", grid=(t_tok // tm,),
        in_specs=[row(D_MODEL), row(D_MODEL), row(D_MODEL), _resident((D_MODEL, D_MODEL)),
                  _resident((N_CHIPS, D_MODEL, FF_SHARD)), _resident((N_CHIPS, FF_SHARD, D_MODEL)),
                  _const((1, D_MODEL)), _const((1, D_MODEL))],
        out_specs=(row(D_MODEL), row(D_MODEL), row(D_FF), row(D_MODEL), row(D_MODEL), _const((8, LANES)), _const((1, D_MODEL))),
        out_shape=(jax.ShapeDtypeStruct((t_tok, D_MODEL), F32), jax.ShapeDtypeStruct((t_tok, D_MODEL), MXU_DTYPE),
                   jax.ShapeDtypeStruct((t_tok, D_FF), MXU_DTYPE), jax.ShapeDtypeStruct((t_tok, D_MODEL), F32),
                   jax.ShapeDtypeStruct((t_tok, D_MODEL), MXU_DTYPE), jax.ShapeDtypeStruct((8, LANES), F32),
                   jax.ShapeDtypeStruct((1, D_MODEL), F32)),
        compiler_params=_cp(dimension_semantics=("arbitrary",)),
    )(x, mix, target, w_out, w_up4, w_down4, g2, gf)


def _bwd_mlp(dx2, dx2b, a, x1, w_up4, w_down4, g2, tm):
    t_tok = dx2.shape[0]

    def body(dx2_ref, dx2b_ref, a_ref, x1_ref, wu_ref, wd_ref, g2_ref, da_ref, dx1_ref, dx1b_ref, dg2_ref):
        @pl.when(pl.program_id(0) == 0)
        def _():
            dg2_ref[...] = jnp.zeros_like(dg2_ref)

        dx2b = dx2b_ref[...]
        dh2 = jnp.zeros((tm, D_MODEL), F32)
        for j in range(N_CHIPS):
            sl = slice(j * FF_SHARD, (j + 1) * FF_SHARD)
            dhid = _mm_nt(dx2b, wd_ref[j])
            da = (dhid * (2.0 * jnp.maximum(a_ref[:, sl].astype(F32), 0.0))).astype(MXU_DTYPE)
            da_ref[:, sl] = da
            dh2 = dh2 + _mm_nt(da, wu_ref[j])
        x1 = x1_ref[...]
        r2 = lax.rsqrt(jnp.mean(x1 * x1, axis=-1, keepdims=True) + EPS)
        xn = x1 * r2
        dg2_ref[...] += jnp.sum(dh2 * xn, axis=0, keepdims=True)
        dx1 = dx2_ref[...] + _rms_bwd(dh2 * g2_ref[...], xn, r2)
        dx1_ref[...] = dx1
        dx1b_ref[...] = dx1.astype(dx1b_ref.dtype)

    row = lambda w: pl.BlockSpec((tm, w), lambda i: (i, 0))
    return pl.pallas_call(
        body, name="bwd_mlp", grid=(t_tok // tm,),
        in_specs=[row(D_MODEL), row(D_MODEL), row(D_FF), row(D_MODEL),
                  _resident((N_CHIPS, D_MODEL, FF_SHARD)), _resident((N_CHIPS, FF_SHARD, D_MODEL)), _const((1, D_MODEL))],
        out_specs=(row(D_FF), row(D_MODEL), row(D_MODEL), _const((1, D_MODEL))),
        out_shape=(jax.ShapeDtypeStruct((t_tok, D_FF), MXU_DTYPE), jax.ShapeDtypeStruct((t_tok, D_MODEL), F32),
                   jax.ShapeDtypeStruct((t_tok, D_MODEL), MXU_DTYPE), jax.ShapeDtypeStruct((1, D_MODEL), F32)),
        compiler_params=_cp(dimension_semantics=("arbitrary",)),
    )(dx2, dx2b, a, x1, w_up4, w_down4, g2)


def _bwd_mlp_wgrads(h2, da, a, dx2b, tk):
    t_tok = h2.shape[0]

    def body(h2_ref, da_ref, a_ref, dx2b_ref, gup_ref, gdn_ref):
        @pl.when(pl.program_id(1) == 0)
        def _():
            gup_ref[...] = jnp.zeros_like(gup_ref)
            gdn_ref[...] = jnp.zeros_like(gdn_ref)

        gup_ref[0] += _mm_tn(h2_ref[...], da_ref[...])
        hid = jnp.square(jnp.maximum(a_ref[...].astype(F32), 0.0))
        gdn_ref[0] += _mm_tn(hid, dx2b_ref[...])

    tok = pl.BlockSpec((tk, D_MODEL), lambda j, t: (t, 0))
    ffb = pl.BlockSpec((tk, FF_SHARD), lambda j, t: (t, j))
    wblk = pl.BlockSpec((1, D_MODEL, D_MODEL), lambda j, t: (j, 0, 0))
    return pl.pallas_call(
        body, name="bwd_mlp_wgrads", grid=(N_CHIPS, t_tok // tk),
        in_specs=[tok, ffb, ffb, tok], out_specs=(wblk, wblk),
        out_shape=(jax.ShapeDtypeStruct((N_CHIPS, D_MODEL, FF_SHARD), F32), jax.ShapeDtypeStruct((N_CHIPS, FF_SHARD, D_MODEL), F32)),
        compiler_params=_cp(dimension_semantics=("parallel", "arbitrary")),
    )(h2, da, a, dx2b)


def _bwd_outproj(dx1b, mix, w_out, tm):
    t_tok = dx1b.shape[0]

    def body(dx_ref, mix_ref, wo_ref, dmix_ref, gwo_ref):
        @pl.when(pl.program_id(0) == 0)
        def _():
            gwo_ref[...] = jnp.zeros_like(gwo_ref)

        dx = dx_ref[...]
        dmix_ref[...] = _mm_nt(dx, wo_ref[...])
        gwo_ref[...] += _mm_tn(mix_ref[...], dx)

    row = lambda w: pl.BlockSpec((tm, w), lambda i: (i, 0))
    return pl.pallas_call(
        body, name="bwd_outproj", grid=(t_tok // tm,),
        in_specs=[row(D_MODEL), row(D_MODEL), _resident((D_MODEL, D_MODEL))],
        out_specs=(row(D_MODEL), _const((D_MODEL, D_MODEL))),
        out_shape=(jax.ShapeDtypeStruct((t_tok, D_MODEL), F32), jax.ShapeDtypeStruct((D_MODEL, D_MODEL), F32)),
        compiler_params=_cp(dimension_semantics=("arbitrary",)),
    )(dx1b, mix, w_out)


def _bwd_attn(q, k, v, mix, dmix, sinks, cos_t, sin_t, seq):
    t_tok = q.shape[0]
    nblk = seq // BLK
    qkv_w = ATTN_W + 2 * KV_W

    def unrope(d, cos, sin):
        return d * cos - _swap_halves(d) * sin

    def body(s_ref, q_ref, k_ref, v_ref, o_ref, do_ref, cos_ref, sin_ref, dqkv_ref, dsink_ref, dk_acc, dv_acc):
        @pl.when(pl.program_id(0) == 0)
        def _():
            dsink_ref[...] = jnp.zeros_like(dsink_ref)

        dk_acc[...] = jnp.zeros_like(dk_acc)
        dv_acc[...] = jnp.zeros_like(dv_acc)
        lane8 = lax.broadcasted_iota(jnp.int32, dsink_ref.shape, 1)
        row8 = lax.broadcasted_iota(jnp.int32, dsink_ref.shape, 0)

        def blk(i, dsink):
            r0 = pl.multiple_of(i * BLK, BLK)
            p0 = pl.multiple_of(jnp.maximum(i - 1, 0) * BLK, BLK)
            mask = _attn_mask(i)
            cos, sin = cos_ref[pl.ds(r0, BLK), :], sin_ref[pl.ds(r0, BLK), :]
            for g in range(N_KV_HEADS):
                sl = slice(g * HEAD_DIM, (g + 1) * HEAD_DIM)
                kk, vv = _kv_window(k_ref, p0, r0, g), _kv_window(v_ref, p0, r0, g)
                qs = _stack_heads(q_ref, r0, g)
                probs, p_sink = _softmax_with_sink(_mm_nt(qs, kk), mask, _sink_col(s_ref, g))
                do = _stack_heads(do_ref, r0, g)
                delta = jnp.sum(do * _stack_heads(o_ref, r0, g).astype(F32), axis=-1, keepdims=True)
                ds = probs * (_mm_nt(do, vv) - delta)
                ds_sink = -(p_sink * delta)
                for h in range(Q_PER_KV):
                    tot = jnp.sum(ds_sink[h * BLK:(h + 1) * BLK])
                    dsink = dsink + jnp.where((row8 == 0) & (lane8 == Q_PER_KV * g + h), tot, 0.0)
                dq = _mm(ds, kk) * (HEAD_DIM ** -0.5)
                for t in range(Q_PER_KV // 2):
                    pair = jnp.concatenate([dq[(2 * t) * BLK:(2 * t + 1) * BLK], dq[(2 * t + 1) * BLK:(2 * t + 2) * BLK]], axis=1)
                    c0 = (Q_PER_KV * g + 2 * t) * HEAD_DIM
                    dqkv_ref[pl.ds(r0, BLK), c0:c0 + LANES] = unrope(pair, cos, sin).astype(dqkv_ref.dtype)
                dkk = _mm_tn(ds, qs)
                dvv = _mm_tn(probs, do)
                dk_acc[pl.ds(p0, BLK), sl] += dkk[:BLK]
                dk_acc[pl.ds(r0, BLK), sl] += dkk[BLK:]
                dv_acc[pl.ds(p0, BLK), sl] += dvv[:BLK]
                dv_acc[pl.ds(r0, BLK), sl] += dvv[BLK:]
            return dsink

        dsink_ref[...] += lax.fori_loop(0, nblk, blk, jnp.zeros(dsink_ref.shape, F32))
        dqkv_ref[:, ATTN_W:ATTN_W + KV_W] = unrope(dk_acc[...], cos_ref[...], sin_ref[...]).astype(dqkv_ref.dtype)
        dqkv_ref[:, ATTN_W + KV_W:] = dv_acc[...].astype(dqkv_ref.dtype)

    row = lambda w: pl.BlockSpec((seq, w), lambda i: (i, 0))
    return pl.pallas_call(
        body, name="bwd_attn", grid=(t_tok // seq,),
        in_specs=[pl.BlockSpec(memory_space=pltpu.SMEM), row(ATTN_W), row(KV_W), row(KV_W), row(ATTN_W), row(ATTN_W),
                  _resident((seq, LANES)), _resident((seq, LANES))],
        out_specs=(row(qkv_w), _const((8, LANES))),
        out_shape=(jax.ShapeDtypeStruct((t_tok, qkv_w), MXU_DTYPE), jax.ShapeDtypeStruct((8, LANES), F32)),
        scratch_shapes=[pltpu.VMEM((seq, KV_W), F32), pltpu.VMEM((seq, KV_W), F32)],
        compiler_params=_cp(dimension_semantics=("arbitrary",)),
    )(sinks, q, k, v, mix, dmix, cos_t, sin_t)


def _bwd_pool(u, dmix, w_pool, pool_scale, seq):
    t_tok = u.shape[0]

    def body(u_ref, dp_ref, wp_ref, sc_ref, du_ref, dwp_ref, dsc_ref):
        @pl.when(pl.program_id(0) == 0)
        def _():
            dwp_ref[...] = jnp.zeros_like(dwp_ref)
            dsc_ref[...] = jnp.zeros_like(dsc_ref)

        for gi, w in enumerate(POOL_WINDOWS):
            sl = slice(gi * POOL_G, (gi + 1) * POOL_G)
            ug = u_ref[:, sl]
            inv = _inv_count(seq, w)
            d = (_window_sum(ug, w, seq, False) * inv - ug).astype(MXU_DTYPE)
            y = _mm(d, wp_ref[gi])
            dpool = dp_ref[:, sl]
            dsc_ref[:, sl] += jnp.sum(y * dpool, axis=0, keepdims=True)
            dy = (dpool * sc_ref[:, sl]).astype(MXU_DTYPE)
            dwp_ref[gi] += _mm_tn(d, dy)
            dd = _mm_nt(dy, wp_ref[gi])
            du_ref[:, sl] = (_window_sum(dd * inv, w, seq, True) - dd).astype(du_ref.dtype)

    row = lambda w, cb: pl.BlockSpec((seq, w), lambda i: (i, cb))
    return pl.pallas_call(
        body, name="bwd_pool", grid=(t_tok // seq,),
        in_specs=[row(POOL_W, 0), row(POOL_W, 1), _const((N_POOL, POOL_G, POOL_G)), _const((1, POOL_W))],
        out_specs=(row(POOL_W, 0), _const((N_POOL, POOL_G, POOL_G)), _const((1, POOL_W))),
        out_shape=(jax.ShapeDtypeStruct((t_tok, POOL_W), MXU_DTYPE), jax.ShapeDtypeStruct((N_POOL, POOL_G, POOL_G), F32),
                   jax.ShapeDtypeStruct((1, POOL_W), F32)),
        compiler_params=_cp(dimension_semantics=("arbitrary",)),
    )(u, dmix, w_pool, pool_scale)


def _bwd_inproj(dqkv, du, x, dx1, w_in, g1, tm):
    t_tok = x.shape[0]
    nsteps = t_tok // tm

    qkv_w = ATTN_W + 2 * KV_W

    def body(dqkv_ref, du_ref, x_ref, dx1_ref, w_ref, g_ref, gx_ref, gw_ref, dg_ref, acc_ref):
        @pl.when(pl.program_id(0) == 0)
        def _():
            acc_ref[...] = jnp.zeros_like(acc_ref)
            dg_ref[...] = jnp.zeros_like(dg_ref)

        dqkv, du = dqkv_ref[...], du_ref[...]
        xv = x_ref[...]
        r = lax.rsqrt(jnp.mean(xv * xv, axis=-1, keepdims=True) + EPS)
        xn = xv * r
        g = g_ref[...]
        dh = _mm_nt(dqkv, w_ref[:, :qkv_w]) + _mm_nt(du, w_ref[:, qkv_w:])
        dg_ref[...] += jnp.sum(dh * xn, axis=0, keepdims=True)
        gx_ref[...] = dx1_ref[...] + _rms_bwd(dh * g, xn, r)
        h = (xn * g).astype(MXU_DTYPE)
        acc_ref[:, :qkv_w] += _mm_tn(h, dqkv)
        acc_ref[:, qkv_w:] += _mm_tn(h, du)

        @pl.when(pl.program_id(0) == nsteps - 1)
        def _():
            acc = acc_ref[...]
            for j in range(N_CHIPS):
                gw_ref[j] = acc[:, j * IN_SHARD:(j + 1) * IN_SHARD]

    row = lambda w: pl.BlockSpec((tm, w), lambda i: (i, 0))
    return pl.pallas_call(
        body, name="bwd_inproj", grid=(nsteps,),
        in_specs=[row(qkv_w), row(POOL_W), row(D_MODEL), row(D_MODEL), _resident((D_MODEL, IN_W)), _const((1, D_MODEL))],
        out_specs=(row(D_MODEL), _const((N_CHIPS, D_MODEL, IN_SHARD)), _const((1, D_MODEL))),
        out_shape=(jax.ShapeDtypeStruct((t_tok, D_MODEL), F32), jax.ShapeDtypeStruct((N_CHIPS, D_MODEL, IN_SHARD), F32),
                   jax.ShapeDtypeStruct((1, D_MODEL), F32)),
        scratch_shapes=[pltpu.VMEM((D_MODEL, IN_W), F32)],
        compiler_params=_cp(dimension_semantics=("arbitrary",)),
    )(dqkv, du, x, dx1, w_in, g1)


def _local_step(x, target, w_in, w_out, w_up4, w_down4, g1, sinks, w_pool, pool_scale, g2, gf, seq):
    tm = min(512, seq)
    tm_mlp = min(256, seq)
    cos_t, sin_t = _rope_tables(seq)
    q, k, v, u = _fwd_inproj(x, g1, w_in, cos_t, sin_t, tm)
    mix = _fwd_mix(q, k, v, u, sinks, w_pool, pool_scale, seq)
    x1, h2, a, dx2, dx2b, loss, dgf = _fwd_mlp_loss(x, mix, target, w_out, w_up4, w_down4, g2, gf, tm_mlp)
    da, dx1, dx1b, dg2 = _bwd_mlp(dx2, dx2b, a, x1, w_up4, w_down4, g2, tm_mlp)
    gw_up4, gw_down4 = _bwd_mlp_wgrads(h2, da, a, dx2b, min(1024, x.shape[0]))
    dmix, gw_out = _bwd_outproj(dx1b, mix, w_out, tm)
    dqkv, dsinks = _bwd_attn(q, k, v, mix, dmix, sinks, cos_t, sin_t, seq)
    du, dwp, dsc = _bwd_pool(u, dmix, w_pool, pool_scale, seq)
    gx, gw_in4, dg1 = _bwd_inproj(dqkv, du, x, dx1, w_in, g1, tm)
    big = (gw_in4, gw_out.reshape(N_CHIPS, OUT_SHARD, D_MODEL), gw_up4, gw_down4)
    small = (dg1, dsinks, dwp.reshape(N_POOL * POOL_G, POOL_G), dsc, dg2, dgf, loss)
    return gx, big, small


def _sibling_exchange(arrs, pick_half, name):
    n = len(arrs)

    def out_shape(a):
        if pick_half:
            return (a.shape[0], a.shape[1] // 2) + a.shape[2:]
        return a.shape

    def body(*refs):
        srcs, dsts = refs[:n], refs[n:2 * n]
        send, recv = refs[2 * n:]
        x, y, c = lax.axis_index("x"), lax.axis_index("y"), lax.axis_index("c")
        cps = []
        for i in range(n):
            src = srcs[i]
            if pick_half:
                h = src.shape[1] // 2
                src = src.at[:, pl.ds((1 - c) * h, h)]
            cp = pltpu.make_async_remote_copy(src_ref=src, dst_ref=dsts[i], send_sem=send.at[i], recv_sem=recv.at[i],
                                              device_id=(x, y, 1 - c), device_id_type=MESH)
            cp.start()
            cps.append(cp)
        for cp in cps:
            cp.wait()

    hbm = pl.BlockSpec(memory_space=pltpu.HBM)
    return pl.pallas_call(
        body, name=name, out_shape=tuple(jax.ShapeDtypeStruct(out_shape(a), a.dtype) for a in arrs),
        in_specs=[hbm] * n, out_specs=tuple([hbm] * n),
        scratch_shapes=[pltpu.SemaphoreType.DMA((n,)), pltpu.SemaphoreType.DMA((n,))],
        compiler_params=_cp(),
    )(*arrs)


def _row_block(rows):
    for cand in (256, 128, 64, 32, 16, 8):
        if rows % cand == 0:
            return cand
    raise ValueError(rows)


def _chip_partial(g4, r4, c_arr, name):
    _, rows, cols = r4.shape
    rb = _row_block(rows)
    nb = rows // rb

    def body(c_ref, g_ref, r_ref, o_ref):
        o_ref[...] = (g_ref[...] + r_ref[...]).astype(o_ref.dtype)

    return pl.pallas_call(
        body, name=name,
        grid_spec=pltpu.PrefetchScalarGridSpec(
            num_scalar_prefetch=1, grid=(N_CHIPS, nb),
            in_specs=[pl.BlockSpec((1, rb, cols), lambda s, i, c: (s, c[0] * nb + i, 0)),
                      pl.BlockSpec((1, rb, cols), lambda s, i, c: (s, i, 0))],
            out_specs=pl.BlockSpec((1, rb, cols), lambda s, i, c: (s, i, 0))),
        out_shape=jax.ShapeDtypeStruct(r4.shape, BF16),
        compiler_params=_cp(dimension_semantics=("parallel", "parallel")),
    )(c_arr, g4, r4)


def _send_partials(parts):
    n = len(parts)

    def body(*refs):
        srcs, dsts = refs[:n], refs[n:2 * n]
        send, recv = refs[2 * n:]
        x, y, c = lax.axis_index("x"), lax.axis_index("y"), lax.axis_index("c")
        chips = [(1 - x, y), (x, 1 - y), (1 - x, 1 - y)]
        cps = []
        for i in range(n):
            for m, chip in enumerate(chips):
                cp = pltpu.make_async_remote_copy(
                    src_ref=srcs[i].at[2 * chip[0] + chip[1]], dst_ref=dsts[i].at[m],
                    send_sem=send.at[3 * i + m], recv_sem=recv.at[3 * i + m],
                    device_id=(chip[0], chip[1], c), device_id_type=MESH)
                cp.start()
                cps.append(cp)
        for cp in cps:
            cp.wait()

    hbm = pl.BlockSpec(memory_space=pltpu.HBM)
    return pl.pallas_call(
        body, name="send_partials",
        out_shape=tuple(jax.ShapeDtypeStruct((3,) + p.shape[1:], p.dtype) for p in parts),
        in_specs=[hbm] * n, out_specs=tuple([hbm] * n),
        scratch_shapes=[pltpu.SemaphoreType.DMA((3 * n,)), pltpu.SemaphoreType.DMA((3 * n,))],
        compiler_params=_cp(),
    )(*parts)


def _final_half(g4, r4, got3, jc_arr, name):
    _, rows, cols = r4.shape
    rb = _row_block(rows)
    nb = rows // rb

    def body(jc_ref, g_ref, r_ref, p_ref, o_ref):
        own = g_ref[0] + r_ref[0]
        o_ref[...] = ((own + p_ref[0].astype(F32)) + p_ref[1].astype(F32)) + p_ref[2].astype(F32)

    return pl.pallas_call(
        body, name=name,
        grid_spec=pltpu.PrefetchScalarGridSpec(
            num_scalar_prefetch=1, grid=(nb,),
            in_specs=[pl.BlockSpec((1, rb, cols), lambda i, jc: (jc[0], jc[1] * nb + i, 0)),
                      pl.BlockSpec((1, rb, cols), lambda i, jc: (jc[0], i, 0)),
                      pl.BlockSpec((3, rb, cols), lambda i, jc: (0, i, 0))],
            out_specs=pl.BlockSpec((rb, cols), lambda i, jc: (i, 0))),
        out_shape=jax.ShapeDtypeStruct((rows, cols), F32),
        compiler_params=_cp(dimension_semantics=("parallel",)),
    )(jc_arr, g4, r4, got3)


def _adamw_math(w, g, m, v):
    m2 = ADAM_B1 * m + (1.0 - ADAM_B1) * g
    v2 = ADAM_B2 * v + (1.0 - ADAM_B2) * (g * g)
    m_hat = m2 / (1.0 - ADAM_B1 ** ADAM_STEP)
    v_hat = v2 / (1.0 - ADAM_B2 ** ADAM_STEP)
    delta = -ADAM_LR * (m_hat / (jnp.sqrt(v_hat) + ADAM_EPS) + ADAM_WD * w)
    return delta, m2, v2


def _adamw_shard(mine, other, w, m, v, c_arr, name):
    rows, cols = w.shape
    half = rows // 2
    rb = _row_block(half)
    nb = half // rb

    def body(c_ref, a_ref, b_ref, w_ref, m_ref, v_ref, g_ref, d_ref, m2_ref, v2_ref):
        g = jnp.where(pl.program_id(0) == c_ref[0], a_ref[...], b_ref[...])
        delta, m2, v2 = _adamw_math(w_ref[...], g, m_ref[...], v_ref[...])
        g_ref[...] = g
        d_ref[...] = delta
        m2_ref[...] = m2
        v2_ref[...] = v2

    hb = pl.BlockSpec((rb, cols), lambda h, i, c: (i, 0))
    fb = pl.BlockSpec((rb, cols), lambda h, i, c: (h * nb + i, 0))
    shp = jax.ShapeDtypeStruct((rows, cols), F32)
    return pl.pallas_call(
        body, name=name,
        grid_spec=pltpu.PrefetchScalarGridSpec(num_scalar_prefetch=1, grid=(2, nb), in_specs=[hb, hb, fb, fb, fb],
                                               out_specs=(fb, fb, fb, fb)),
        out_shape=(shp, shp, shp, shp),
        compiler_params=_cp(dimension_semantics=("parallel", "parallel")),
    )(c_arr, mine, other, w, m, v)


def _small_allreduce_adamw(parts, params):
    n = len(parts)
    n_w = len(params)

    def body(*refs):
        p_refs = refs[:n]
        wmv = refs[n:n + 3 * n_w]
        outs = refs[n + 3 * n_w:n + 3 * n_w + 1 + 4 * n_w]
        rest = refs[n + 3 * n_w + 1 + 4 * n_w:]
        accs, bufs = rest[:n], rest[n:2 * n]
        send, recv = rest[2 * n:]
        x, y, c = lax.axis_index("x"), lax.axis_index("y"), lax.axis_index("c")
        partners = [(x, y, 1 - c), (1 - x, y, c), (x, 1 - y, c)]
        for i in range(n):
            accs[i][...] = p_refs[i][...]
        for s, partner in enumerate(partners):
            cps = []
            for i in range(n):
                cp = pltpu.make_async_remote_copy(src_ref=accs[i], dst_ref=bufs[i].at[s], send_sem=send.at[3 * i + s],
                                                  recv_sem=recv.at[3 * i + s], device_id=partner, device_id_type=MESH)
                cp.start()
                cps.append(cp)
            for cp in cps:
                cp.wait()
            for i in range(n):
                accs[i][...] = accs[i][...] + bufs[i][s]
        loss_ref = outs[0]
        loss_ref[...] = accs[n - 1][0:1, 0:1]
        grads = [accs[0][...], accs[1][0:1, 0:N_Q_HEADS], accs[2][...], accs[3][...], accs[4][...], accs[5][...]]
        for i in range(n_w):
            w_ref, m_ref, v_ref = wmv[3 * i:3 * i + 3]
            g_ref, d_ref, m2_ref, v2_ref = outs[1 + 4 * i:5 + 4 * i]
            delta, m2, v2 = _adamw_math(w_ref[...], grads[i], m_ref[...], v_ref[...])
            g_ref[...] = grads[i]
            d_ref[...] = delta
            m2_ref[...] = m2
            v2_ref[...] = v2

    flat = [a for p in params for a in p]
    vmem = pl.BlockSpec(memory_space=pltpu.VMEM)
    out_shape = [jax.ShapeDtypeStruct((1, 1), F32)]
    for p in params:
        out_shape += [jax.ShapeDtypeStruct(p[0].shape, F32)] * 4
    res = pl.pallas_call(
        body, name="small_allreduce_adamw", out_shape=tuple(out_shape),
        in_specs=[vmem] * (n + len(flat)), out_specs=tuple([vmem] * len(out_shape)),
        scratch_shapes=[pltpu.VMEM(p.shape, F32) for p in parts] + [pltpu.VMEM((3,) + p.shape, F32) for p in parts] + [
            pltpu.SemaphoreType.DMA((3 * n,)), pltpu.SemaphoreType.DMA((3 * n,))],
        compiler_params=_cp(),
    )(*parts, *flat)
    return res[0], [res[1 + 4 * i:5 + 4 * i] for i in range(n_w)]


def kernel(x, attn_norm_g, w_in, attn_sinks, w_pool, pool_scale, w_out, mlp_norm_g, w_up, w_down, final_norm_g, loss_target, m_attn_norm_g, m_w_in, m_attn_sinks, m_w_pool, m_pool_scale, m_w_out, m_mlp_norm_g, m_w_up, m_w_down, m_final_norm_g, v_attn_norm_g, v_w_in, v_attn_sinks, v_w_pool, v_pool_scale, v_w_out, v_mlp_norm_g, v_w_up, v_w_down, v_final_norm_g):
    nseq, seq, d = x.shape
    c_idx = lax.axis_index("c").astype(jnp.int32)
    j_idx = (2 * lax.axis_index("x") + lax.axis_index("y")).astype(jnp.int32)
    c_arr = jnp.reshape(c_idx, (1,))
    jc_arr = jnp.stack([j_idx, c_idx])

    big_w = (w_in[0], w_out[0], w_up[0], w_down[0])
    big_m = (m_w_in[0], m_w_out[0], m_w_up[0], m_w_down[0])
    big_v = (v_w_in[0], v_w_out[0], v_w_up[0], v_w_down[0])
    w_in4, w_out4, w_up4, w_down4 = _gather_weights(*big_w)
    w_in_full = _concat_w_in(w_in4)
    w_out_full = w_out4.reshape(D_MODEL, D_MODEL)

    gx, big_g, small_g = _local_step(
        x.reshape(nseq * seq, d), loss_target.reshape(nseq * seq, d), w_in_full, w_out_full, w_up4, w_down4,
        attn_norm_g, attn_sinks.reshape(N_Q_HEADS), w_pool[0], pool_scale, mlp_norm_g, final_norm_g.reshape(1, d), seq)

    from_sibling = _sibling_exchange(big_g, True, "grads_to_sibling")
    names = ("w_in", "w_out", "w_up", "w_down")
    partials = [_chip_partial(g, r, c_arr, "chip_partial_" + nm) for g, r, nm in zip(big_g, from_sibling, names)]
    received = _send_partials(partials)
    mine = [_final_half(g, r, p, jc_arr, "final_half_" + nm) for g, r, p, nm in zip(big_g, from_sibling, received, names)]
    other = _sibling_exchange(mine, False, "halves_to_sibling")
    big_out = [_adamw_shard(a, b, w, m, v, c_arr, "adamw_" + nm)
               for a, b, w, m, v, nm in zip(mine, other, big_w, big_m, big_v, names)]

    wp_flat = lambda a: a.reshape(N_POOL * POOL_G, POOL_G)
    small_params = [
        (attn_norm_g, m_attn_norm_g, v_attn_norm_g),
        (attn_sinks, m_attn_sinks, v_attn_sinks),
        (wp_flat(w_pool), wp_flat(m_w_pool), wp_flat(v_w_pool)),
        (pool_scale, m_pool_scale, v_pool_scale),
        (mlp_norm_g, m_mlp_norm_g, v_mlp_norm_g),
        (final_norm_g.reshape(1, d), m_final_norm_g.reshape(1, d), v_final_norm_g.reshape(1, d)),
    ]
    loss, small_out = _small_allreduce_adamw(small_g, small_params)

    def shaped(i, arr):
        return {1: w_in, 5: w_out, 7: w_up, 8: w_down, 0: attn_norm_g, 2: attn_sinks, 3: w_pool, 4: pool_scale,
                6: mlp_norm_g, 9: final_norm_g}[i].shape

    order = [small_out[0], big_out[0], small_out[1], small_out[2], small_out[3], big_out[1], small_out[4], big_out[2],
             big_out[3], small_out[5]]
    outs = [loss.reshape(()), gx.reshape(nseq, seq, d)]
    for kind in range(4):
        outs += [order[i][kind].reshape(shaped(i, None)) for i in range(10)]
    return tuple(outs)
```

```python
import jax
import jax.numpy as jnp
from jax import lax
from jax.experimental import pallas as pl
from jax.experimental.pallas import tpu as pltpu

F32 = jnp.float32
BF16 = jnp.bfloat16
MXU_DTYPE = jnp.bfloat16

D_MODEL = 1024
HEAD_DIM = 64
N_Q_HEADS = 8
N_KV_HEADS = 2
Q_PER_KV = N_Q_HEADS // N_KV_HEADS
ATTN_W = N_Q_HEADS * HEAD_DIM
KV_W = N_KV_HEADS * HEAD_DIM
BLK = 128
POOL_WINDOWS = (2, 4, 8, 16)
N_POOL = len(POOL_WINDOWS)
POOL_W = D_MODEL - ATTN_W
POOL_G = POOL_W // N_POOL
IN_W = ATTN_W + 2 * KV_W + POOL_W
D_FF = 4 * D_MODEL
EPS = 1e-6
ROPE_THETA = 10000.0
N_CHIPS = 4
IN_SHARD = IN_W // N_CHIPS
OUT_SHARD = D_MODEL // N_CHIPS
FF_SHARD = D_FF // N_CHIPS
LANES = 128

ADAM_LR = 0.001
ADAM_B1 = 0.9
ADAM_B2 = 0.999
ADAM_EPS = 1e-08
ADAM_WD = 0.01
ADAM_STEP = 10

VMEM_LIMIT = 56 * 1024 * 1024
MESH = pl.DeviceIdType.MESH


def _cp(**kw):
    return pltpu.CompilerParams(vmem_limit_bytes=VMEM_LIMIT, **kw)


def _mm(a, b):
    return jnp.dot(a.astype(MXU_DTYPE), b.astype(MXU_DTYPE), preferred_element_type=F32)


def _mm_nt(a, b):
    return lax.dot_general(a.astype(MXU_DTYPE), b.astype(MXU_DTYPE), (((1,), (1,)), ((), ())),
                           preferred_element_type=F32)


def _mm_tn(a, b):
    return lax.dot_general(a.astype(MXU_DTYPE), b.astype(MXU_DTYPE), (((0,), (0,)), ((), ())),
                           preferred_element_type=F32)


def _resident(shape):
    nd = len(shape)
    return pl.BlockSpec(shape, lambda *_: (0,) * nd, pipeline_mode=pl.Buffered(1))


def _const(shape):
    nd = len(shape)
    return pl.BlockSpec(shape, lambda *_: (0,) * nd)


def _rope_tables(seq):
    half = HEAD_DIM // 2
    inv_freq = ROPE_THETA ** (-jnp.arange(half, dtype=F32) / half)
    ang = jnp.arange(seq, dtype=F32)[:, None] * inv_freq[None, :]
    cos, sin = jnp.cos(ang), jnp.sin(ang)
    cos_t = jnp.concatenate([cos, cos, cos, cos], axis=1)
    sin_t = jnp.concatenate([-sin, sin, -sin, sin], axis=1)
    return cos_t, sin_t


def _swap_halves(xc):
    lane = lax.broadcasted_iota(jnp.int32, xc.shape, 1)
    return jnp.where((lane & 63) < 32, pltpu.roll(xc, 96, 1), pltpu.roll(xc, 32, 1))


def _gather_weights(w_in, w_out, w_up, w_down):
    shards = (w_in, w_out, w_up, w_down)
    nw = len(shards)

    def body(*refs):
        srcs = refs[0:nw]
        outs = refs[nw:2 * nw]
        stages = refs[2 * nw:3 * nw]
        send1, recv1, send2, recv2, lsem = refs[3 * nw:]
        x, y, c = lax.axis_index("x"), lax.axis_index("y"), lax.axis_index("c")
        j = 2 * x + y
        chips = [(1 - x, y), (x, 1 - y), (1 - x, 1 - y)]
        for k in range(nw):
            stages[k][...] = srcs[k][...].astype(BF16)
        local = [pltpu.make_async_copy(stages[k], outs[k].at[j], lsem.at[k]) for k in range(nw)]
        for cp in local:
            cp.start()

        def piece(k, chip, half):
            rows = shards[k].shape[0] // 2
            return outs[k].at[2 * chip[0] + chip[1], pl.ds(half * rows, rows)]

        first, passed = [], []
        for k in range(nw):
            rows = shards[k].shape[0] // 2
            for n, chip in enumerate(chips):
                cp = pltpu.make_async_remote_copy(
                    src_ref=stages[k].at[pl.ds(c * rows, rows)], dst_ref=piece(k, (x, y), c),
                    send_sem=send1.at[k * 3 + n], recv_sem=recv1.at[k * 3 + n],
                    device_id=(chip[0], chip[1], c), device_id_type=MESH)
                cp.start()
                first.append(cp)
        for k in range(nw):
            for n, chip in enumerate(chips):
                got = piece(k, chip, c)
                pltpu.make_async_remote_copy(
                    src_ref=got, dst_ref=got, send_sem=send1.at[k * 3 + n], recv_sem=recv1.at[k * 3 + n],
                    device_id=(chip[0], chip[1], c), device_id_type=MESH).wait_recv()
                cp = pltpu.make_async_remote_copy(
                    src_ref=got, dst_ref=got, send_sem=send2.at[k * 3 + n], recv_sem=recv2.at[k * 3 + n],
                    device_id=(x, y, 1 - c), device_id_type=MESH)
                cp.start()
                passed.append(cp)
        for k in range(nw):
            for n, chip in enumerate(chips):
                got = piece(k, chip, 1 - c)
                pltpu.make_async_remote_copy(
                    src_ref=got, dst_ref=got, send_sem=send2.at[k * 3 + n], recv_sem=recv2.at[k * 3 + n],
                    device_id=(x, y, 1 - c), device_id_type=MESH).wait_recv()
        for cp in first + passed:
            cp.wait_send()
        for cp in local:
            cp.wait()

    hbm = pl.BlockSpec(memory_space=pltpu.HBM)
    vmem = pl.BlockSpec(memory_space=pltpu.VMEM)
    return pl.pallas_call(
        body, name="gather_weights",
        out_shape=tuple(jax.ShapeDtypeStruct((N_CHIPS,) + s.shape, BF16) for s in shards),
        in_specs=[vmem] * nw, out_specs=tuple([hbm] * nw),
        scratch_shapes=[pltpu.VMEM(s.shape, BF16) for s in shards] + [
            pltpu.SemaphoreType.DMA((3 * nw,)), pltpu.SemaphoreType.DMA((3 * nw,)),
            pltpu.SemaphoreType.DMA((3 * nw,)), pltpu.SemaphoreType.DMA((3 * nw,)),
            pltpu.SemaphoreType.DMA((nw,))],
        compiler_params=_cp(),
    )(*shards)


def _concat_w_in(w_in4):
    def body(s_ref, o_ref):
        o_ref[...] = jnp.concatenate([s_ref[j] for j in range(N_CHIPS)], axis=1)

    return pl.pallas_call(body, name="concat_w_in", out_shape=jax.ShapeDtypeStruct((D_MODEL, IN_W), w_in4.dtype),
                          compiler_params=_cp())(w_in4)


def _fwd_inproj(x, g1, w_in, cos_t, sin_t, tm):
    t_tok = x.shape[0]
    seq = cos_t.shape[0]
    per_seq = seq // tm

    def body(x_ref, g_ref, w_ref, cos_ref, sin_ref, q_ref, k_ref, v_ref, u_ref):
        xv = x_ref[...]
        r = lax.rsqrt(jnp.mean(xv * xv, axis=-1, keepdims=True) + EPS)
        h = (xv * r) * g_ref[...]
        proj = _mm(h, w_ref[...])
        cos, sin = cos_ref[...], sin_ref[...]
        for cidx in range((ATTN_W + KV_W) // LANES):
            xc = proj[:, cidx * LANES:(cidx + 1) * LANES]
            rot = xc * cos + _swap_halves(xc) * sin
            if cidx < ATTN_W // LANES:
                q_ref[:, cidx * LANES:(cidx + 1) * LANES] = (rot * (HEAD_DIM ** -0.5)).astype(q_ref.dtype)
            else:
                for g in range(N_KV_HEADS):
                    k_ref[g] = rot[:, g * HEAD_DIM:(g + 1) * HEAD_DIM].astype(k_ref.dtype)
        for g in range(N_KV_HEADS):
            c0 = ATTN_W + KV_W + g * HEAD_DIM
            v_ref[g] = proj[:, c0:c0 + HEAD_DIM].astype(v_ref.dtype)
        u_ref[...] = proj[:, ATTN_W + 2 * KV_W:]

    row = lambda w: pl.BlockSpec((tm, w), lambda i: (i, 0))
    kv = pl.BlockSpec((N_KV_HEADS, tm, HEAD_DIM), lambda i: (0, i, 0))
    tab = pl.BlockSpec((tm, LANES), lambda i: (i % per_seq, 0))
    kv_shape = jax.ShapeDtypeStruct((N_KV_HEADS, t_tok, HEAD_DIM), MXU_DTYPE)
    return pl.pallas_call(
        body, name="fwd_inproj", grid=(t_tok // tm,),
        in_specs=[row(D_MODEL), _const((1, D_MODEL)), _resident((D_MODEL, IN_W)), tab, tab],
        out_specs=(row(ATTN_W), kv, kv, row(POOL_W)),
        out_shape=(jax.ShapeDtypeStruct((t_tok, ATTN_W), MXU_DTYPE), kv_shape, kv_shape,
                   jax.ShapeDtypeStruct((t_tok, POOL_W), F32)),
        compiler_params=_cp(dimension_semantics=("parallel",)),
    )(x, g1, w_in, cos_t, sin_t)


MASKED = -1e30


def _attn_bias():
    b = lax.broadcasted_iota(jnp.int32, (2 * BLK, BLK), 0)
    a = lax.broadcasted_iota(jnp.int32, (2 * BLK, BLK), 1)
    own = (b >= BLK) & (b - BLK <= a)
    prev = (b < BLK) & (b > a)
    return jnp.stack([jnp.where(own, 0.0, MASKED), jnp.where(own | prev, 0.0, MASKED)]).astype(F32)


def _sink_rows(sinks):
    return jnp.repeat(sinks.reshape(N_KV_HEADS, Q_PER_KV), BLK, axis=1).reshape(N_KV_HEADS, 1, Q_PER_KV * BLK)


def _stack_heads(ref, r0, g):
    return jnp.concatenate(
        [ref[pl.ds(r0, BLK), (Q_PER_KV * g + h) * HEAD_DIM:(Q_PER_KV * g + h + 1) * HEAD_DIM] for h in range(Q_PER_KV)],
        axis=0)


def _kv_window(ref, p0, r0, g):
    return jnp.concatenate([ref[g, pl.ds(p0, BLK), :], ref[g, pl.ds(r0, BLK), :]], axis=0)


def _scores_t(k_ref, q_ref, bias, p0, r0, g):
    kk = _kv_window(k_ref, p0, r0, g)
    qs = _stack_heads(q_ref, r0, g)
    st = _mm_nt(kk, qs) + jnp.concatenate([bias] * Q_PER_KV, axis=1)
    return st, kk, qs


def _head_rows(ref, r0, g):
    return jnp.concatenate([ref[pl.ds(Q_PER_KV * g + h, 1), pl.ds(r0, BLK)] for h in range(Q_PER_KV)], axis=1)


def _pairs_to_rows(xt):
    out = []
    for t in range(Q_PER_KV // 2):
        pair = jnp.concatenate([xt[:, (2 * t) * BLK:(2 * t + 1) * BLK], xt[:, (2 * t + 1) * BLK:(2 * t + 2) * BLK]], axis=0)
        out.append(pair.T)
    return out


def _shift_rows(x, k, seq):
    row = lax.broadcasted_iota(jnp.int32, x.shape, 0)
    if k > 0:
        return jnp.where(row >= k, pltpu.roll(x, k, 0), 0.0)
    return jnp.where(row < seq + k, pltpu.roll(x, seq + k, 0), 0.0)


def _window_sum(x, w, seq, forward):
    s, k = x, 1
    while k < w:
        s = s + _shift_rows(s, -k if forward else k, seq)
        k *= 2
    return s


def _inv_count(seq, w):
    pos = lax.broadcasted_iota(jnp.int32, (seq, 1), 0)
    return 1.0 / jnp.minimum(pos + 1, w).astype(F32)


def _fwd_attn(q, k, v, sink_rows, bias, seq):
    t_tok = q.shape[0]
    nblk = seq // BLK

    def body(q_ref, k_ref, v_ref, sink_ref, bias_ref, o_ref, lse_ref):
        def blk(i, carry):
            r0 = pl.multiple_of(i * BLK, BLK)
            p0 = pl.multiple_of(jnp.maximum(i - 1, 0) * BLK, BLK)
            bias = bias_ref[jnp.minimum(i, 1)]
            for g in range(N_KV_HEADS):
                st, _, _ = _scores_t(k_ref, q_ref, bias, p0, r0, g)
                sink = sink_ref[g]
                m = jnp.maximum(jnp.max(st, axis=0, keepdims=True), sink)
                p = jnp.exp(st - m)
                den = jnp.sum(p, axis=0, keepdims=True) + jnp.exp(sink - m)
                lse = m + jnp.log(den)
                for h in range(Q_PER_KV):
                    lse_ref[pl.ds(Q_PER_KV * g + h, 1), pl.ds(r0, BLK)] = lse[:, h * BLK:(h + 1) * BLK]
                ot = _mm_tn(_kv_window(v_ref, p0, r0, g), p) * (1.0 / den)
                for t, tile in enumerate(_pairs_to_rows(ot)):
                    c0 = (Q_PER_KV * g + 2 * t) * HEAD_DIM
                    o_ref[pl.ds(r0, BLK), c0:c0 + LANES] = tile.astype(o_ref.dtype)
            return carry

        lax.fori_loop(0, nblk, blk, 0)

    row = lambda w: pl.BlockSpec((seq, w), lambda i: (i, 0))
    kv = pl.BlockSpec((N_KV_HEADS, seq, HEAD_DIM), lambda i: (0, i, 0))
    return pl.pallas_call(
        body, name="fwd_attn", grid=(t_tok // seq,),
        in_specs=[row(ATTN_W), kv, kv, _const((N_KV_HEADS, 1, Q_PER_KV * BLK)), _const((2, 2 * BLK, BLK))],
        out_specs=(row(ATTN_W), pl.BlockSpec((N_Q_HEADS, seq), lambda i: (0, i))),
        out_shape=(jax.ShapeDtypeStruct((t_tok, ATTN_W), MXU_DTYPE), jax.ShapeDtypeStruct((N_Q_HEADS, t_tok), F32)),
        compiler_params=_cp(dimension_semantics=("parallel",)),
    )(q, k, v, sink_rows, bias)


def _fwd_pool(u, w_pool, pool_scale, seq):
    t_tok = u.shape[0]

    def body(u_ref, wp_ref, sc_ref, o_ref):
        for gi, w in enumerate(POOL_WINDOWS):
            sl = slice(gi * POOL_G, (gi + 1) * POOL_G)
            ug = u_ref[:, sl]
            d = _window_sum(ug, w, seq, False) * _inv_count(seq, w) - ug
            o_ref[:, sl] = (_mm(d, wp_ref[gi]) * sc_ref[:, sl]).astype(o_ref.dtype)

    row = pl.BlockSpec((seq, POOL_W), lambda i: (i, 0))
    return pl.pallas_call(
        body, name="fwd_pool", grid=(t_tok // seq,),
        in_specs=[row, _const((N_POOL, POOL_G, POOL_G)), _const((1, POOL_W))], out_specs=row,
        out_shape=jax.ShapeDtypeStruct((t_tok, POOL_W), MXU_DTYPE),
        compiler_params=_cp(dimension_semantics=("parallel",)),
    )(u, w_pool, pool_scale)


def _rms_bwd(dy_g, xn, r):
    return r * (dy_g - xn * jnp.mean(dy_g * xn, axis=-1, keepdims=True))


def _fwd_mlp_loss(x, attn, pool, target, w_out, w_up4, w_down4, g2, gf, tm):
    t_tok = x.shape[0]

    def body(x_ref, attn_ref, pool_ref, tgt_ref, wo_ref, wu_ref, wd_ref, g2_ref, gf_ref,
             x1_ref, h2_ref, a_ref, dx2_ref, dx2b_ref, loss_ref, dgf_ref):
        @pl.when(pl.program_id(0) == 0)
        def _():
            loss_ref[...] = jnp.zeros_like(loss_ref)
            dgf_ref[...] = jnp.zeros_like(dgf_ref)

        x1 = x_ref[...] + (_mm(attn_ref[...], wo_ref[:ATTN_W]) + _mm(pool_ref[...], wo_ref[ATTN_W:]))
        x1_ref[...] = x1
        r2 = lax.rsqrt(jnp.mean(x1 * x1, axis=-1, keepdims=True) + EPS)
        h2 = ((x1 * r2) * g2_ref[...]).astype(MXU_DTYPE)
        h2_ref[...] = h2
        acc = jnp.zeros((tm, D_MODEL), F32)
        for j in range(N_CHIPS):
            a = _mm(h2, wu_ref[j])
            a_ref[:, j * FF_SHARD:(j + 1) * FF_SHARD] = a.astype(a_ref.dtype)
            acc = acc + _mm(jnp.square(jnp.maximum(a, 0.0)), wd_ref[j])
        x2 = x1 + acc
        r3 = lax.rsqrt(jnp.mean(x2 * x2, axis=-1, keepdims=True) + EPS)
        xn = x2 * r3
        gf_v = gf_ref[...]
        err = xn * gf_v - tgt_ref[...]
        part = jnp.sum(err * err) * (0.5 / D_MODEL)
        first = (lax.broadcasted_iota(jnp.int32, loss_ref.shape, 0) == 0) & (lax.broadcasted_iota(jnp.int32, loss_ref.shape, 1) == 0)
        loss_ref[...] += jnp.where(first, part, 0.0)
        dy = err * (1.0 / D_MODEL)
        dgf_ref[...] += jnp.sum(dy * xn, axis=0, keepdims=True)
        dx2 = _rms_bwd(dy * gf_v, xn, r3)
        dx2_ref[...] = dx2
        dx2b_ref[...] = dx2.astype(dx2b_ref.dtype)

    row = lambda w: pl.BlockSpec((tm, w), lambda i: (i, 0))
    return pl.pallas_call(
        body, name="fwd_mlp_loss", grid=(t_tok // tm,),
        in_specs=[row(D_MODEL), row(ATTN_W), row(POOL_W), row(D_MODEL), _resident((D_MODEL, D_MODEL)),
                  _resident((N_CHIPS, D_MODEL, FF_SHARD)), _resident((N_CHIPS, FF_SHARD, D_MODEL)),
                  _const((1, D_MODEL)), _const((1, D_MODEL))],
        out_specs=(row(D_MODEL), row(D_MODEL), row(D_FF), row(D_MODEL), row(D_MODEL), _const((8, LANES)), _const((1, D_MODEL))),
        out_shape=(jax.ShapeDtypeStruct((t_tok, D_MODEL), F32), jax.ShapeDtypeStruct((t_tok, D_MODEL), MXU_DTYPE),
                   jax.ShapeDtypeStruct((t_tok, D_FF), MXU_DTYPE), jax.ShapeDtypeStruct((t_tok, D_MODEL), F32),
                   jax.ShapeDtypeStruct((t_tok, D_MODEL), MXU_DTYPE), jax.ShapeDtypeStruct((8, LANES), F32),
                   jax.ShapeDtypeStruct((1, D_MODEL), F32)),
        compiler_params=_cp(dimension_semantics=("arbitrary",)),
    )(x, attn, pool, target, w_out, w_up4, w_down4, g2, gf)


def _bwd_mlp(dx2, dx2b, a, x1, w_up4, w_down4, g2, tm):
    t_tok = dx2.shape[0]

    def body(dx2_ref, dx2b_ref, a_ref, x1_ref, wu_ref, wd_ref, g2_ref, da_ref, dx1_ref, dx1b_ref, dg2_ref):
        @pl.when(pl.program_id(0) == 0)
        def _():
            dg2_ref[...] = jnp.zeros_like(dg2_ref)

        dx2b = dx2b_ref[...]
        dh2 = jnp.zeros((tm, D_MODEL), F32)
        for j in range(N_CHIPS):
            sl = slice(j * FF_SHARD, (j + 1) * FF_SHARD)
            dhid = _mm_nt(dx2b, wd_ref[j])
            da = (dhid * (2.0 * jnp.maximum(a_ref[:, sl].astype(F32), 0.0))).astype(MXU_DTYPE)
            da_ref[:, sl] = da
            dh2 = dh2 + _mm_nt(da, wu_ref[j])
        x1 = x1_ref[...]
        r2 = lax.rsqrt(jnp.mean(x1 * x1, axis=-1, keepdims=True) + EPS)
        xn = x1 * r2
        dg2_ref[...] += jnp.sum(dh2 * xn, axis=0, keepdims=True)
        dx1 = dx2_ref[...] + _rms_bwd(dh2 * g2_ref[...], xn, r2)
        dx1_ref[...] = dx1
        dx1b_ref[...] = dx1.astype(dx1b_ref.dtype)

    row = lambda w: pl.BlockSpec((tm, w), lambda i: (i, 0))
    return pl.pallas_call(
        body, name="bwd_mlp", grid=(t_tok // tm,),
        in_specs=[row(D_MODEL), row(D_MODEL), row(D_FF), row(D_MODEL),
                  _resident((N_CHIPS, D_MODEL, FF_SHARD)), _resident((N_CHIPS, FF_SHARD, D_MODEL)), _const((1, D_MODEL))],
        out_specs=(row(D_FF), row(D_MODEL), row(D_MODEL), _const((1, D_MODEL))),
        out_shape=(jax.ShapeDtypeStruct((t_tok, D_FF), MXU_DTYPE), jax.ShapeDtypeStruct((t_tok, D_MODEL), F32),
                   jax.ShapeDtypeStruct((t_tok, D_MODEL), MXU_DTYPE), jax.ShapeDtypeStruct((1, D_MODEL), F32)),
        compiler_params=_cp(dimension_semantics=("arbitrary",)),
    )(dx2, dx2b, a, x1, w_up4, w_down4, g2)


def _bwd_mlp_wgrads(h2, da, a, dx2b, tk):
    t_tok = h2.shape[0]

    def body(h2_ref, da_ref, a_ref, dx2b_ref, gup_ref, gdn_ref):
        @pl.when(pl.program_id(1) == 0)
        def _():
            gup_ref[...] = jnp.zeros_like(gup_ref)
            gdn_ref[...] = jnp.zeros_like(gdn_ref)

        gup_ref[0] += _mm_tn(h2_ref[...], da_ref[...])
        hid = jnp.square(jnp.maximum(a_ref[...].astype(F32), 0.0))
        gdn_ref[0] += _mm_tn(hid, dx2b_ref[...])

    tok = pl.BlockSpec((tk, D_MODEL), lambda j, t: (t, 0))
    ffb = pl.BlockSpec((tk, FF_SHARD), lambda j, t: (t, j))
    wblk = pl.BlockSpec((1, D_MODEL, D_MODEL), lambda j, t: (j, 0, 0))
    return pl.pallas_call(
        body, name="bwd_mlp_wgrads", grid=(N_CHIPS, t_tok // tk),
        in_specs=[tok, ffb, ffb, tok], out_specs=(wblk, wblk),
        out_shape=(jax.ShapeDtypeStruct((N_CHIPS, D_MODEL, FF_SHARD), F32), jax.ShapeDtypeStruct((N_CHIPS, FF_SHARD, D_MODEL), F32)),
        compiler_params=_cp(dimension_semantics=("parallel", "arbitrary")),
    )(h2, da, a, dx2b)


def _head_selector():
    ch = lax.broadcasted_iota(jnp.int32, (ATTN_W, LANES), 0)
    col = lax.broadcasted_iota(jnp.int32, (ATTN_W, LANES), 1)
    return (ch // HEAD_DIM == col).astype(MXU_DTYPE)


def _bwd_outproj(dx1b, attn, pool, w_out, head_sel, tm):
    t_tok = dx1b.shape[0]

    def body(dx_ref, attn_ref, pool_ref, wo_ref, sel_ref, dattn_ref, dpool_ref, delta_ref, gwo_ref):
        @pl.when(pl.program_id(0) == 0)
        def _():
            gwo_ref[...] = jnp.zeros_like(gwo_ref)

        dx = dx_ref[...]
        attn = attn_ref[...]
        dattn = _mm_nt(dx, wo_ref[:ATTN_W])
        dattn_ref[...] = dattn.astype(dattn_ref.dtype)
        dpool_ref[...] = _mm_nt(dx, wo_ref[ATTN_W:])
        prod = dattn * attn.astype(F32)
        hi = prod.astype(MXU_DTYPE)
        lo = prod - hi.astype(F32)
        delta = _mm(hi, sel_ref[...]) + _mm(lo, sel_ref[...])
        delta_ref[...] = delta.T[:N_Q_HEADS]
        gwo_ref[:ATTN_W] += _mm_tn(attn, dx)
        gwo_ref[ATTN_W:] += _mm_tn(pool_ref[...], dx)

    row = lambda w: pl.BlockSpec((tm, w), lambda i: (i, 0))
    return pl.pallas_call(
        body, name="bwd_outproj", grid=(t_tok // tm,),
        in_specs=[row(D_MODEL), row(ATTN_W), row(POOL_W), _resident((D_MODEL, D_MODEL)), _const((ATTN_W, LANES))],
        out_specs=(row(ATTN_W), row(POOL_W), pl.BlockSpec((N_Q_HEADS, tm), lambda i: (0, i)), _const((D_MODEL, D_MODEL))),
        out_shape=(jax.ShapeDtypeStruct((t_tok, ATTN_W), MXU_DTYPE), jax.ShapeDtypeStruct((t_tok, POOL_W), F32),
                   jax.ShapeDtypeStruct((N_Q_HEADS, t_tok), F32), jax.ShapeDtypeStruct((D_MODEL, D_MODEL), F32)),
        compiler_params=_cp(dimension_semantics=("arbitrary",)),
    )(dx1b, attn, pool, w_out, head_sel)


def _bwd_attn(q, k, v, dattn, lse, delta, sink_rows, bias, cos_t, sin_t, seq):
    t_tok = q.shape[0]
    nblk = seq // BLK
    qkv_w = ATTN_W + 2 * KV_W

    def unrope(d, cos, sin):
        return d * cos - _swap_halves(d) * sin

    def body(q_ref, k_ref, v_ref, do_ref, lse_ref, delta_ref, sink_ref, bias_ref, cos_ref, sin_ref,
             dqkv_ref, dsink_ref, dk_acc, dv_acc):
        @pl.when(pl.program_id(0) == 0)
        def _():
            dsink_ref[...] = jnp.zeros_like(dsink_ref)

        dk_acc[...] = jnp.zeros_like(dk_acc)
        dv_acc[...] = jnp.zeros_like(dv_acc)

        def blk(i, dsink):
            r0 = pl.multiple_of(i * BLK, BLK)
            p0 = pl.multiple_of(jnp.maximum(i - 1, 0) * BLK, BLK)
            bias = bias_ref[jnp.minimum(i, 1)]
            cos, sin = cos_ref[pl.ds(r0, BLK), :], sin_ref[pl.ds(r0, BLK), :]
            new = []
            for g in range(N_KV_HEADS):
                st, kk, qs = _scores_t(k_ref, q_ref, bias, p0, r0, g)
                lse_g, delta_g = _head_rows(lse_ref, r0, g), _head_rows(delta_ref, r0, g)
                pn = jnp.exp(st - lse_g)
                dos = _stack_heads(do_ref, r0, g)
                dst = pn * (_mm_nt(_kv_window(v_ref, p0, r0, g), dos) - delta_g)
                new.append(dsink[g] - jnp.exp(sink_ref[g] - lse_g) * delta_g)
                dqt = _mm_tn(kk, dst) * (HEAD_DIM ** -0.5)
                for t, tile in enumerate(_pairs_to_rows(dqt)):
                    c0 = (Q_PER_KV * g + 2 * t) * HEAD_DIM
                    dqkv_ref[pl.ds(r0, BLK), c0:c0 + LANES] = unrope(tile, cos, sin).astype(dqkv_ref.dtype)
                dkk = _mm(dst, qs)
                dvv = _mm(pn, dos)
                dk_acc[g, pl.ds(p0, BLK), :] += dkk[:BLK]
                dk_acc[g, pl.ds(r0, BLK), :] += dkk[BLK:]
                dv_acc[g, pl.ds(p0, BLK), :] += dvv[:BLK]
                dv_acc[g, pl.ds(r0, BLK), :] += dvv[BLK:]
            return tuple(new)

        zero = jnp.zeros((1, Q_PER_KV * BLK), F32)
        dsink = lax.fori_loop(0, nblk, blk, (zero,) * N_KV_HEADS)
        lane = lax.broadcasted_iota(jnp.int32, dsink_ref.shape, 1)
        row = lax.broadcasted_iota(jnp.int32, dsink_ref.shape, 0)
        tile = jnp.zeros(dsink_ref.shape, F32)
        for g in range(N_KV_HEADS):
            for h in range(Q_PER_KV):
                tot = jnp.sum(dsink[g][:, h * BLK:(h + 1) * BLK])
                tile = tile + jnp.where((row == 0) & (lane == Q_PER_KV * g + h), tot, 0.0)
        dsink_ref[...] += tile
        dk = jnp.concatenate([dk_acc[g] for g in range(N_KV_HEADS)], axis=1)
        dqkv_ref[:, ATTN_W:ATTN_W + KV_W] = unrope(dk, cos_ref[...], sin_ref[...]).astype(dqkv_ref.dtype)
        dqkv_ref[:, ATTN_W + KV_W:] = jnp.concatenate([dv_acc[g] for g in range(N_KV_HEADS)], axis=1).astype(dqkv_ref.dtype)

    row = lambda w: pl.BlockSpec((seq, w), lambda i: (i, 0))
    kv = pl.BlockSpec((N_KV_HEADS, seq, HEAD_DIM), lambda i: (0, i, 0))
    per_head = pl.BlockSpec((N_Q_HEADS, seq), lambda i: (0, i))
    return pl.pallas_call(
        body, name="bwd_attn", grid=(t_tok // seq,),
        in_specs=[row(ATTN_W), kv, kv, row(ATTN_W), per_head, per_head, _const((N_KV_HEADS, 1, Q_PER_KV * BLK)),
                  _const((2, 2 * BLK, BLK)), _resident((seq, LANES)), _resident((seq, LANES))],
        out_specs=(row(qkv_w), _const((8, LANES))),
        out_shape=(jax.ShapeDtypeStruct((t_tok, qkv_w), MXU_DTYPE), jax.ShapeDtypeStruct((8, LANES), F32)),
        scratch_shapes=[pltpu.VMEM((N_KV_HEADS, seq, HEAD_DIM), F32), pltpu.VMEM((N_KV_HEADS, seq, HEAD_DIM), F32)],
        compiler_params=_cp(dimension_semantics=("arbitrary",)),
    )(q, k, v, dattn, lse, delta, sink_rows, bias, cos_t, sin_t)


def _bwd_pool(u, dpool, w_pool, pool_scale, seq):
    t_tok = u.shape[0]

    def body(u_ref, dp_ref, wp_ref, sc_ref, du_ref, dwp_ref, dsc_ref):
        @pl.when(pl.program_id(0) == 0)
        def _():
            dwp_ref[...] = jnp.zeros_like(dwp_ref)
            dsc_ref[...] = jnp.zeros_like(dsc_ref)

        for gi, w in enumerate(POOL_WINDOWS):
            sl = slice(gi * POOL_G, (gi + 1) * POOL_G)
            ug = u_ref[:, sl]
            inv = _inv_count(seq, w)
            d = (_window_sum(ug, w, seq, False) * inv - ug).astype(MXU_DTYPE)
            y = _mm(d, wp_ref[gi])
            dpool = dp_ref[:, sl]
            dsc_ref[:, sl] += jnp.sum(y * dpool, axis=0, keepdims=True)
            dy = (dpool * sc_ref[:, sl]).astype(MXU_DTYPE)
            dwp_ref[gi] += _mm_tn(d, dy)
            dd = _mm_nt(dy, wp_ref[gi])
            du_ref[:, sl] = (_window_sum(dd * inv, w, seq, True) - dd).astype(du_ref.dtype)

    row = pl.BlockSpec((seq, POOL_W), lambda i: (i, 0))
    return pl.pallas_call(
        body, name="bwd_pool", grid=(t_tok // seq,),
        in_specs=[row, row, _const((N_POOL, POOL_G, POOL_G)), _const((1, POOL_W))],
        out_specs=(row, _const((N_POOL, POOL_G, POOL_G)), _const((1, POOL_W))),
        out_shape=(jax.ShapeDtypeStruct((t_tok, POOL_W), MXU_DTYPE), jax.ShapeDtypeStruct((N_POOL, POOL_G, POOL_G), F32),
                   jax.ShapeDtypeStruct((1, POOL_W), F32)),
        compiler_params=_cp(dimension_semantics=("arbitrary",)),
    )(u, dpool, w_pool, pool_scale)


def _bwd_inproj(dqkv, du, x, dx1, w_in, g1, tm):
    t_tok = x.shape[0]
    nsteps = t_tok // tm

    qkv_w = ATTN_W + 2 * KV_W

    def body(dqkv_ref, du_ref, x_ref, dx1_ref, w_ref, g_ref, gx_ref, gw_ref, dg_ref, acc_ref):
        @pl.when(pl.program_id(0) == 0)
        def _():
            acc_ref[...] = jnp.zeros_like(acc_ref)
            dg_ref[...] = jnp.zeros_like(dg_ref)

        dqkv, du = dqkv_ref[...], du_ref[...]
        xv = x_ref[...]
        r = lax.rsqrt(jnp.mean(xv * xv, axis=-1, keepdims=True) + EPS)
        xn = xv * r
        g = g_ref[...]
        dh = _mm_nt(dqkv, w_ref[:, :qkv_w]) + _mm_nt(du, w_ref[:, qkv_w:])
        dg_ref[...] += jnp.sum(dh * xn, axis=0, keepdims=True)
        gx_ref[...] = dx1_ref[...] + _rms_bwd(dh * g, xn, r)
        h = (xn * g).astype(MXU_DTYPE)
        acc_ref[:, :qkv_w] += _mm_tn(h, dqkv)
        acc_ref[:, qkv_w:] += _mm_tn(h, du)

        @pl.when(pl.program_id(0) == nsteps - 1)
        def _():
            acc = acc_ref[...]
            for j in range(N_CHIPS):
                gw_ref[j] = acc[:, j * IN_SHARD:(j + 1) * IN_SHARD]

    row = lambda w: pl.BlockSpec((tm, w), lambda i: (i, 0))
    return pl.pallas_call(
        body, name="bwd_inproj", grid=(nsteps,),
        in_specs=[row(qkv_w), row(POOL_W), row(D_MODEL), row(D_MODEL), _resident((D_MODEL, IN_W)), _const((1, D_MODEL))],
        out_specs=(row(D_MODEL), _const((N_CHIPS, D_MODEL, IN_SHARD)), _const((1, D_MODEL))),
        out_shape=(jax.ShapeDtypeStruct((t_tok, D_MODEL), F32), jax.ShapeDtypeStruct((N_CHIPS, D_MODEL, IN_SHARD), F32),
                   jax.ShapeDtypeStruct((1, D_MODEL), F32)),
        scratch_shapes=[pltpu.VMEM((D_MODEL, IN_W), F32)],
        compiler_params=_cp(dimension_semantics=("arbitrary",)),
    )(dqkv, du, x, dx1, w_in, g1)


def _local_step(x, target, w_in, w_out, w_up4, w_down4, g1, sinks, w_pool, pool_scale, g2, gf, seq):
    tm = min(512, seq)
    tm_mlp = min(256, seq)
    cos_t, sin_t = _rope_tables(seq)
    sink_rows, bias = _sink_rows(sinks), _attn_bias()
    q, k, v, u = _fwd_inproj(x, g1, w_in, cos_t, sin_t, tm)
    attn, lse = _fwd_attn(q, k, v, sink_rows, bias, seq)
    pool = _fwd_pool(u, w_pool, pool_scale, seq)
    x1, h2, a, dx2, dx2b, loss, dgf = _fwd_mlp_loss(x, attn, pool, target, w_out, w_up4, w_down4, g2, gf, tm_mlp)
    da, dx1, dx1b, dg2 = _bwd_mlp(dx2, dx2b, a, x1, w_up4, w_down4, g2, tm_mlp)
    gw_up4, gw_down4 = _bwd_mlp_wgrads(h2, da, a, dx2b, min(1024, x.shape[0]))
    dattn, dpool, delta, gw_out = _bwd_outproj(dx1b, attn, pool, w_out, _head_selector(), tm)
    dqkv, dsinks = _bwd_attn(q, k, v, dattn, lse, delta, sink_rows, bias, cos_t, sin_t, seq)
    du, dwp, dsc = _bwd_pool(u, dpool, w_pool, pool_scale, seq)
    gx, gw_in4, dg1 = _bwd_inproj(dqkv, du, x, dx1, w_in, g1, tm)
    big = (gw_in4, gw_out.reshape(N_CHIPS, OUT_SHARD, D_MODEL), gw_up4, gw_down4)
    small = (dg1, dsinks, dwp.reshape(N_POOL * POOL_G, POOL_G), dsc, dg2, dgf, loss)
    return gx, big, small


def _sibling_exchange(arrs, pick_half, name):
    n = len(arrs)

    def out_shape(a):
        if pick_half:
            return (a.shape[0], a.shape[1] // 2) + a.shape[2:]
        return a.shape

    def body(*refs):
        srcs, dsts = refs[:n], refs[n:2 * n]
        send, recv = refs[2 * n:]
        x, y, c = lax.axis_index("x"), lax.axis_index("y"), lax.axis_index("c")
        cps = []
        for i in range(n):
            src = srcs[i]
            if pick_half:
                h = src.shape[1] // 2
                src = src.at[:, pl.ds((1 - c) * h, h)]
            cp = pltpu.make_async_remote_copy(src_ref=src, dst_ref=dsts[i], send_sem=send.at[i], recv_sem=recv.at[i],
                                              device_id=(x, y, 1 - c), device_id_type=MESH)
            cp.start()
            cps.append(cp)
        for cp in cps:
            cp.wait()

    hbm = pl.BlockSpec(memory_space=pltpu.HBM)
    return pl.pallas_call(
        body, name=name, out_shape=tuple(jax.ShapeDtypeStruct(out_shape(a), a.dtype) for a in arrs),
        in_specs=[hbm] * n, out_specs=tuple([hbm] * n),
        scratch_shapes=[pltpu.SemaphoreType.DMA((n,)), pltpu.SemaphoreType.DMA((n,))],
        compiler_params=_cp(),
    )(*arrs)


def _row_block(rows):
    for cand in (256, 128, 64, 32, 16, 8):
        if rows % cand == 0:
            return cand
    raise ValueError(rows)


def _chip_partial(g4, r4, c_arr, name):
    _, rows, cols = r4.shape
    rb = _row_block(rows)
    nb = rows // rb

    def body(c_ref, g_ref, r_ref, o_ref):
        o_ref[...] = (g_ref[...] + r_ref[...]).astype(o_ref.dtype)

    return pl.pallas_call(
        body, name=name,
        grid_spec=pltpu.PrefetchScalarGridSpec(
            num_scalar_prefetch=1, grid=(N_CHIPS, nb),
            in_specs=[pl.BlockSpec((1, rb, cols), lambda s, i, c: (s, c[0] * nb + i, 0)),
                      pl.BlockSpec((1, rb, cols), lambda s, i, c: (s, i, 0))],
            out_specs=pl.BlockSpec((1, rb, cols), lambda s, i, c: (s, i, 0))),
        out_shape=jax.ShapeDtypeStruct(r4.shape, BF16),
        compiler_params=_cp(dimension_semantics=("parallel", "parallel")),
    )(c_arr, g4, r4)


def _send_partials(parts):
    n = len(parts)

    def body(*refs):
        srcs, dsts = refs[:n], refs[n:2 * n]
        send, recv = refs[2 * n:]
        x, y, c = lax.axis_index("x"), lax.axis_index("y"), lax.axis_index("c")
        chips = [(1 - x, y), (x, 1 - y), (1 - x, 1 - y)]
        cps = []
        for i in range(n):
            for m, chip in enumerate(chips):
                cp = pltpu.make_async_remote_copy(
                    src_ref=srcs[i].at[2 * chip[0] + chip[1]], dst_ref=dsts[i].at[m],
                    send_sem=send.at[3 * i + m], recv_sem=recv.at[3 * i + m],
                    device_id=(chip[0], chip[1], c), device_id_type=MESH)
                cp.start()
                cps.append(cp)
        for cp in cps:
            cp.wait()

    hbm = pl.BlockSpec(memory_space=pltpu.HBM)
    return pl.pallas_call(
        body, name="send_partials",
        out_shape=tuple(jax.ShapeDtypeStruct((3,) + p.shape[1:], p.dtype) for p in parts),
        in_specs=[hbm] * n, out_specs=tuple([hbm] * n),
        scratch_shapes=[pltpu.SemaphoreType.DMA((3 * n,)), pltpu.SemaphoreType.DMA((3 * n,))],
        compiler_params=_cp(),
    )(*parts)


def _final_half(g4, r4, got3, jc_arr, name):
    _, rows, cols = r4.shape
    rb = _row_block(rows)
    nb = rows // rb

    def body(jc_ref, g_ref, r_ref, p_ref, o_ref):
        own = g_ref[0] + r_ref[0]
        o_ref[...] = ((own + p_ref[0].astype(F32)) + p_ref[1].astype(F32)) + p_ref[2].astype(F32)

    return pl.pallas_call(
        body, name=name,
        grid_spec=pltpu.PrefetchScalarGridSpec(
            num_scalar_prefetch=1, grid=(nb,),
            in_specs=[pl.BlockSpec((1, rb, cols), lambda i, jc: (jc[0], jc[1] * nb + i, 0)),
                      pl.BlockSpec((1, rb, cols), lambda i, jc: (jc[0], i, 0)),
                      pl.BlockSpec((3, rb, cols), lambda i, jc: (0, i, 0))],
            out_specs=pl.BlockSpec((rb, cols), lambda i, jc: (i, 0))),
        out_shape=jax.ShapeDtypeStruct((rows, cols), F32),
        compiler_params=_cp(dimension_semantics=("parallel",)),
    )(jc_arr, g4, r4, got3)


def _adamw_math(w, g, m, v):
    m2 = ADAM_B1 * m + (1.0 - ADAM_B1) * g
    v2 = ADAM_B2 * v + (1.0 - ADAM_B2) * (g * g)
    m_hat = m2 / (1.0 - ADAM_B1 ** ADAM_STEP)
    v_hat = v2 / (1.0 - ADAM_B2 ** ADAM_STEP)
    delta = -ADAM_LR * (m_hat / (jnp.sqrt(v_hat) + ADAM_EPS) + ADAM_WD * w)
    return delta, m2, v2


def _adamw_shard(mine, other, w, m, v, c_arr, name):
    rows, cols = w.shape
    half = rows // 2
    rb = _row_block(half)
    nb = half // rb

    def body(c_ref, a_ref, b_ref, w_ref, m_ref, v_ref, g_ref, d_ref, m2_ref, v2_ref):
        g = jnp.where(pl.program_id(0) == c_ref[0], a_ref[...], b_ref[...])
        delta, m2, v2 = _adamw_math(w_ref[...], g, m_ref[...], v_ref[...])
        g_ref[...] = g
        d_ref[...] = delta
        m2_ref[...] = m2
        v2_ref[...] = v2

    hb = pl.BlockSpec((rb, cols), lambda h, i, c: (i, 0))
    fb = pl.BlockSpec((rb, cols), lambda h, i, c: (h * nb + i, 0))
    shp = jax.ShapeDtypeStruct((rows, cols), F32)
    return pl.pallas_call(
        body, name=name,
        grid_spec=pltpu.PrefetchScalarGridSpec(num_scalar_prefetch=1, grid=(2, nb), in_specs=[hb, hb, fb, fb, fb],
                                               out_specs=(fb, fb, fb, fb)),
        out_shape=(shp, shp, shp, shp),
        compiler_params=_cp(dimension_semantics=("parallel", "parallel")),
    )(c_arr, mine, other, w, m, v)


def _small_allreduce_adamw(parts, params):
    n = len(parts)
    n_w = len(params)

    def body(*refs):
        p_refs = refs[:n]
        wmv = refs[n:n + 3 * n_w]
        outs = refs[n + 3 * n_w:n + 3 * n_w + 1 + 4 * n_w]
        rest = refs[n + 3 * n_w + 1 + 4 * n_w:]
        accs, bufs = rest[:n], rest[n:2 * n]
        send, recv = rest[2 * n:]
        x, y, c = lax.axis_index("x"), lax.axis_index("y"), lax.axis_index("c")
        partners = [(x, y, 1 - c), (1 - x, y, c), (x, 1 - y, c)]
        for i in range(n):
            accs[i][...] = p_refs[i][...]
        for s, partner in enumerate(partners):
            cps = []
            for i in range(n):
                cp = pltpu.make_async_remote_copy(src_ref=accs[i], dst_ref=bufs[i].at[s], send_sem=send.at[3 * i + s],
                                                  recv_sem=recv.at[3 * i + s], device_id=partner, device_id_type=MESH)
                cp.start()
                cps.append(cp)
            for cp in cps:
                cp.wait()
            for i in range(n):
                accs[i][...] = accs[i][...] + bufs[i][s]
        loss_ref = outs[0]
        loss_ref[...] = accs[n - 1][0:1, 0:1]
        grads = [accs[0][...], accs[1][0:1, 0:N_Q_HEADS], accs[2][...], accs[3][...], accs[4][...], accs[5][...]]
        for i in range(n_w):
            w_ref, m_ref, v_ref = wmv[3 * i:3 * i + 3]
            g_ref, d_ref, m2_ref, v2_ref = outs[1 + 4 * i:5 + 4 * i]
            delta, m2, v2 = _adamw_math(w_ref[...], grads[i], m_ref[...], v_ref[...])
            g_ref[...] = grads[i]
            d_ref[...] = delta
            m2_ref[...] = m2
            v2_ref[...] = v2

    flat = [a for p in params for a in p]
    vmem = pl.BlockSpec(memory_space=pltpu.VMEM)
    out_shape = [jax.ShapeDtypeStruct((1, 1), F32)]
    for p in params:
        out_shape += [jax.ShapeDtypeStruct(p[0].shape, F32)] * 4
    res = pl.pallas_call(
        body, name="small_allreduce_adamw", out_shape=tuple(out_shape),
        in_specs=[vmem] * (n + len(flat)), out_specs=tuple([vmem] * len(out_shape)),
        scratch_shapes=[pltpu.VMEM(p.shape, F32) for p in parts] + [pltpu.VMEM((3,) + p.shape, F32) for p in parts] + [
            pltpu.SemaphoreType.DMA((3 * n,)), pltpu.SemaphoreType.DMA((3 * n,))],
        compiler_params=_cp(),
    )(*parts, *flat)
    return res[0], [res[1 + 4 * i:5 + 4 * i] for i in range(n_w)]


def kernel(x, attn_norm_g, w_in, attn_sinks, w_pool, pool_scale, w_out, mlp_norm_g, w_up, w_down, final_norm_g, loss_target, m_attn_norm_g, m_w_in, m_attn_sinks, m_w_pool, m_pool_scale, m_w_out, m_mlp_norm_g, m_w_up, m_w_down, m_final_norm_g, v_attn_norm_g, v_w_in, v_attn_sinks, v_w_pool, v_pool_scale, v_w_out, v_mlp_norm_g, v_w_up, v_w_down, v_final_norm_g):
    nseq, seq, d = x.shape
    c_idx = lax.axis_index("c").astype(jnp.int32)
    j_idx = (2 * lax.axis_index("x") + lax.axis_index("y")).astype(jnp.int32)
    c_arr = jnp.reshape(c_idx, (1,))
    jc_arr = jnp.stack([j_idx, c_idx])

    big_w = (w_in[0], w_out[0], w_up[0], w_down[0])
    big_m = (m_w_in[0], m_w_out[0], m_w_up[0], m_w_down[0])
    big_v = (v_w_in[0], v_w_out[0], v_w_up[0], v_w_down[0])
    w_in4, w_out4, w_up4, w_down4 = _gather_weights(*big_w)
    w_in_full = _concat_w_in(w_in4)
    w_out_full = w_out4.reshape(D_MODEL, D_MODEL)

    gx, big_g, small_g = _local_step(
        x.reshape(nseq * seq, d), loss_target.reshape(nseq * seq, d), w_in_full, w_out_full, w_up4, w_down4,
        attn_norm_g, attn_sinks.reshape(N_Q_HEADS), w_pool[0], pool_scale, mlp_norm_g, final_norm_g.reshape(1, d), seq)

    from_sibling = _sibling_exchange(big_g, True, "grads_to_sibling")
    names = ("w_in", "w_out", "w_up", "w_down")
    partials = [_chip_partial(g, r, c_arr, "chip_partial_" + nm) for g, r, nm in zip(big_g, from_sibling, names)]
    received = _send_partials(partials)
    mine = [_final_half(g, r, p, jc_arr, "final_half_" + nm) for g, r, p, nm in zip(big_g, from_sibling, received, names)]
    other = _sibling_exchange(mine, False, "halves_to_sibling")
    big_out = [_adamw_shard(a, b, w, m, v, c_arr, "adamw_" + nm)
               for a, b, w, m, v, nm in zip(mine, other, big_w, big_m, big_v, names)]

    wp_flat = lambda a: a.reshape(N_POOL * POOL_G, POOL_G)
    small_params = [
        (attn_norm_g, m_attn_norm_g, v_attn_norm_g),
        (attn_sinks, m_attn_sinks, v_attn_sinks),
        (wp_flat(w_pool), wp_flat(m_w_pool), wp_flat(v_w_pool)),
        (pool_scale, m_pool_scale, v_pool_scale),
        (mlp_norm_g, m_mlp_norm_g, v_mlp_norm_g),
        (final_norm_g.reshape(1, d), m_final_norm_g.reshape(1, d), v_final_norm_g.reshape(1, d)),
    ]
    loss, small_out = _small_allreduce_adamw(small_g, small_params)

    def shaped(i, arr):
        return {1: w_in, 5: w_out, 7: w_up, 8: w_down, 0: attn_norm_g, 2: attn_sinks, 3: w_pool, 4: pool_scale,
                6: mlp_norm_g, 9: final_norm_g}[i].shape

    order = [small_out[0], big_out[0], small_out[1], small_out[2], small_out[3], big_out[1], small_out[4], big_out[2],
             big_out[3], small_out[5]]
    outs = [loss.reshape(()), gx.reshape(nseq, seq, d)]
    for kind in range(4):
        outs += [order[i][kind].reshape(shaped(i, None)) for i in range(10)]
    return tuple(outs)
```

```python
import jax
import jax.numpy as jnp
from jax import lax
from jax.experimental import pallas as pl
from jax.experimental.pallas import tpu as pltpu

F32 = jnp.float32
BF16 = jnp.bfloat16
MXU_DTYPE = jnp.bfloat16

D_MODEL = 1024
HEAD_DIM = 64
N_Q_HEADS = 8
N_KV_HEADS = 2
Q_PER_KV = N_Q_HEADS // N_KV_HEADS
ATTN_W = N_Q_HEADS * HEAD_DIM
KV_W = N_KV_HEADS * HEAD_DIM
BLK = 128
POOL_WINDOWS = (2, 4, 8, 16)
N_POOL = len(POOL_WINDOWS)
POOL_W = D_MODEL - ATTN_W
POOL_G = POOL_W // N_POOL
IN_W = ATTN_W + 2 * KV_W + POOL_W
D_FF = 4 * D_MODEL
EPS = 1e-6
ROPE_THETA = 10000.0
N_CHIPS = 4
IN_SHARD = IN_W // N_CHIPS
OUT_SHARD = D_MODEL // N_CHIPS
FF_SHARD = D_FF // N_CHIPS
LANES = 128

ADAM_LR = 0.001
ADAM_B1 = 0.9
ADAM_B2 = 0.999
ADAM_EPS = 1e-08
ADAM_WD = 0.01
ADAM_STEP = 10

VMEM_LIMIT = 56 * 1024 * 1024
MESH = pl.DeviceIdType.MESH


def _cp(**kw):
    return pltpu.CompilerParams(vmem_limit_bytes=VMEM_LIMIT, **kw)


def _mm(a, b):
    return jnp.dot(a.astype(MXU_DTYPE), b.astype(MXU_DTYPE), preferred_element_type=F32)


def _mm_nt(a, b):
    return lax.dot_general(a.astype(MXU_DTYPE), b.astype(MXU_DTYPE), (((1,), (1,)), ((), ())),
                           preferred_element_type=F32)


def _mm_tn(a, b):
    return lax.dot_general(a.astype(MXU_DTYPE), b.astype(MXU_DTYPE), (((0,), (0,)), ((), ())),
                           preferred_element_type=F32)


def _resident(shape):
    nd = len(shape)
    return pl.BlockSpec(shape, lambda *_: (0,) * nd, pipeline_mode=pl.Buffered(1))


def _const(shape):
    nd = len(shape)
    return pl.BlockSpec(shape, lambda *_: (0,) * nd)


def _rope_tables(seq):
    half = HEAD_DIM // 2
    inv_freq = ROPE_THETA ** (-jnp.arange(half, dtype=F32) / half)
    ang = jnp.arange(seq, dtype=F32)[:, None] * inv_freq[None, :]
    cos, sin = jnp.cos(ang), jnp.sin(ang)
    cos_t = jnp.concatenate([cos, cos, cos, cos], axis=1)
    sin_t = jnp.concatenate([-sin, sin, -sin, sin], axis=1)
    return cos_t, sin_t


def _swap_halves(xc):
    lane = lax.broadcasted_iota(jnp.int32, xc.shape, 1)
    return jnp.where((lane & 63) < 32, pltpu.roll(xc, 96, 1), pltpu.roll(xc, 32, 1))


def _gather_weights(*shards):
    nw = len(shards)

    def body(*refs):
        srcs = refs[0:nw]
        outs = refs[nw:2 * nw]
        stages = refs[2 * nw:3 * nw]
        send1, recv1, send2, recv2, lsem = refs[3 * nw:]
        x, y, c = lax.axis_index("x"), lax.axis_index("y"), lax.axis_index("c")
        j = 2 * x + y
        chips = [(1 - x, y), (x, 1 - y), (1 - x, 1 - y)]
        for k in range(nw):
            stages[k][...] = srcs[k][...].astype(BF16)
        local = [pltpu.make_async_copy(stages[k], outs[k].at[j], lsem.at[k]) for k in range(nw)]
        for cp in local:
            cp.start()

        def piece(k, chip, half):
            rows = shards[k].shape[0] // 2
            return outs[k].at[2 * chip[0] + chip[1], pl.ds(half * rows, rows)]

        first, passed = [], []
        for k in range(nw):
            rows = shards[k].shape[0] // 2
            for n, chip in enumerate(chips):
                cp = pltpu.make_async_remote_copy(
                    src_ref=stages[k].at[pl.ds(c * rows, rows)], dst_ref=piece(k, (x, y), c),
                    send_sem=send1.at[k * 3 + n], recv_sem=recv1.at[k * 3 + n],
                    device_id=(chip[0], chip[1], c), device_id_type=MESH)
                cp.start()
                first.append(cp)
        for k in range(nw):
            for n, chip in enumerate(chips):
                got = piece(k, chip, c)
                pltpu.make_async_remote_copy(
                    src_ref=got, dst_ref=got, send_sem=send1.at[k * 3 + n], recv_sem=recv1.at[k * 3 + n],
                    device_id=(chip[0], chip[1], c), device_id_type=MESH).wait_recv()
                cp = pltpu.make_async_remote_copy(
                    src_ref=got, dst_ref=got, send_sem=send2.at[k * 3 + n], recv_sem=recv2.at[k * 3 + n],
                    device_id=(x, y, 1 - c), device_id_type=MESH)
                cp.start()
                passed.append(cp)
        for k in range(nw):
            for n, chip in enumerate(chips):
                got = piece(k, chip, 1 - c)
                pltpu.make_async_remote_copy(
                    src_ref=got, dst_ref=got, send_sem=send2.at[k * 3 + n], recv_sem=recv2.at[k * 3 + n],
                    device_id=(x, y, 1 - c), device_id_type=MESH).wait_recv()
        for cp in first + passed:
            cp.wait_send()
        for cp in local:
            cp.wait()

    hbm = pl.BlockSpec(memory_space=pltpu.HBM)
    vmem = pl.BlockSpec(memory_space=pltpu.VMEM)
    return pl.pallas_call(
        body, name="gather_weights",
        out_shape=tuple(jax.ShapeDtypeStruct((N_CHIPS,) + s.shape, BF16) for s in shards),
        in_specs=[vmem] * nw, out_specs=tuple([hbm] * nw),
        scratch_shapes=[pltpu.VMEM(s.shape, BF16) for s in shards] + [
            pltpu.SemaphoreType.DMA((3 * nw,)), pltpu.SemaphoreType.DMA((3 * nw,)),
            pltpu.SemaphoreType.DMA((3 * nw,)), pltpu.SemaphoreType.DMA((3 * nw,)),
            pltpu.SemaphoreType.DMA((nw,))],
        compiler_params=_cp(),
    )(*shards)


_HBM = pl.BlockSpec(memory_space=pltpu.HBM)
_SEM = pl.BlockSpec(memory_space=pltpu.SEMAPHORE)
_ANY = pl.BlockSpec(memory_space=pl.ANY)
_EFFECT = pltpu.SideEffectType.DATAFLOW_SIDE_EFFECTING
TOKEN = jax.ShapeDtypeStruct((8, LANES), F32)


def _other_chips(x, y):
    return [(1 - x, y), (x, 1 - y), (1 - x, 1 - y)]


def _gather_start(*shards):
    nw = len(shards)

    def body(*refs):
        srcs, lands = refs[:nw], refs[nw:2 * nw]
        send, recv, token = refs[2 * nw:2 * nw + 3]
        stages = refs[2 * nw + 3:3 * nw + 3]
        lsem = refs[3 * nw + 3]
        x, y, c = lax.axis_index("x"), lax.axis_index("y"), lax.axis_index("c")
        j = 2 * x + y
        for k in range(nw):
            stages[k][...] = srcs[k][...].astype(BF16)
        local = [pltpu.make_async_copy(stages[k], lands[k].at[j], lsem.at[k]) for k in range(nw)]
        for cp in local:
            cp.start()
        for cp in local:
            cp.wait()
        for k in range(nw):
            rows = shards[k].shape[0] // 2
            mine = lands[k].at[j, pl.ds(c * rows, rows)]
            for n, chip in enumerate(_other_chips(x, y)):
                pltpu.make_async_remote_copy(
                    src_ref=mine, dst_ref=mine, send_sem=send.at[k * 3 + n], recv_sem=recv.at[k * 3 + n],
                    device_id=(chip[0], chip[1], c), device_id_type=MESH).start()
        token[...] = jnp.zeros_like(token)

    vmem = pl.BlockSpec(memory_space=pltpu.VMEM)
    res = pl.pallas_call(
        body, name="gather_start",
        out_shape=tuple(pltpu.HBM((N_CHIPS,) + s.shape, BF16) for s in shards) + (
            pltpu.SemaphoreType.DMA((3 * nw,)), pltpu.SemaphoreType.DMA((3 * nw,)), TOKEN),
        in_specs=[vmem] * nw, out_specs=tuple([_HBM] * nw) + (_SEM, _SEM, vmem),
        scratch_shapes=[pltpu.VMEM(s.shape, BF16) for s in shards] + [pltpu.SemaphoreType.DMA((nw,))],
        compiler_params=_cp(has_side_effects=_EFFECT),
    )(*shards)
    return res[:nw], res[nw], res[nw + 1], res[nw + 2]


def _gather_wait(lands, send, recv, after):
    nw = len(lands)

    def body(*refs):
        ins = refs[:nw]
        send_ref, recv_ref = refs[nw], refs[nw + 1]
        x, y, c = lax.axis_index("x"), lax.axis_index("y"), lax.axis_index("c")
        j = 2 * x + y
        for k in range(nw):
            rows = lands[k].shape[1] // 2
            mine = ins[k].at[j, pl.ds(c * rows, rows)]
            for n, chip in enumerate(_other_chips(x, y)):
                got = ins[k].at[2 * chip[0] + chip[1], pl.ds(c * rows, rows)]
                cp = pltpu.make_async_remote_copy(
                    src_ref=mine, dst_ref=got, send_sem=send_ref.at[k * 3 + n], recv_sem=recv_ref.at[k * 3 + n],
                    device_id=(chip[0], chip[1], c), device_id_type=MESH)
                cp.wait_send()
                cp.wait_recv()

    return pl.pallas_call(
        body, name="gather_wait", out_shape=tuple(pltpu.HBM(a.shape, a.dtype) for a in lands),
        in_specs=[_HBM] * nw + [_SEM, _SEM, _ANY], out_specs=tuple([_HBM] * nw),
        input_output_aliases={k: k for k in range(nw)},
        compiler_params=_cp(has_side_effects=_EFFECT),
    )(*lands, send, recv, after)


def _gather_forward(lands):
    nw = len(lands)

    def body(*refs):
        outs = refs[nw:2 * nw]
        send, recv = refs[2 * nw:]
        x, y, c = lax.axis_index("x"), lax.axis_index("y"), lax.axis_index("c")
        cps = []
        for k in range(nw):
            rows = lands[k].shape[1] // 2
            for n, chip in enumerate(_other_chips(x, y)):
                got = outs[k].at[2 * chip[0] + chip[1], pl.ds(c * rows, rows)]
                cp = pltpu.make_async_remote_copy(
                    src_ref=got, dst_ref=got, send_sem=send.at[k * 3 + n], recv_sem=recv.at[k * 3 + n],
                    device_id=(x, y, 1 - c), device_id_type=MESH)
                cp.start()
                cps.append(cp)
        for k in range(nw):
            rows = lands[k].shape[1] // 2
            for n, chip in enumerate(_other_chips(x, y)):
                theirs = outs[k].at[2 * chip[0] + chip[1], pl.ds((1 - c) * rows, rows)]
                pltpu.make_async_remote_copy(
                    src_ref=theirs, dst_ref=theirs, send_sem=send.at[k * 3 + n], recv_sem=recv.at[k * 3 + n],
                    device_id=(x, y, 1 - c), device_id_type=MESH).wait_recv()
        for cp in cps:
            cp.wait_send()

    return pl.pallas_call(
        body, name="gather_forward", out_shape=tuple(jax.ShapeDtypeStruct(a.shape, a.dtype) for a in lands),
        in_specs=[_HBM] * nw, out_specs=tuple([_HBM] * nw), input_output_aliases={k: k for k in range(nw)},
        scratch_shapes=[pltpu.SemaphoreType.DMA((3 * nw,)), pltpu.SemaphoreType.DMA((3 * nw,))],
        compiler_params=_cp(),
    )(*lands)


def _concat_w_in(w_in4):
    def body(s_ref, o_ref):
        o_ref[...] = jnp.concatenate([s_ref[j] for j in range(N_CHIPS)], axis=1)

    return pl.pallas_call(body, name="concat_w_in", out_shape=jax.ShapeDtypeStruct((D_MODEL, IN_W), w_in4.dtype),
                          compiler_params=_cp())(w_in4)


def _after(tokens):
    tokens = [t for t in tokens if t is not None]
    return tokens, [_ANY] * len(tokens)


def _fwd_inproj(x, g1, w_in, cos_t, sin_t, tm, after=()):
    t_tok = x.shape[0]
    seq = cos_t.shape[0]
    per_seq = seq // tm
    tokens, token_specs = _after(after)

    def body(x_ref, g_ref, w_ref, cos_ref, sin_ref, *rest):
        q_ref, k_ref, v_ref, u_ref = rest[len(tokens):]
        xv = x_ref[...]
        r = lax.rsqrt(jnp.mean(xv * xv, axis=-1, keepdims=True) + EPS)
        h = (xv * r) * g_ref[...]
        proj = _mm(h, w_ref[...])
        cos, sin = cos_ref[...], sin_ref[...]
        for cidx in range((ATTN_W + KV_W) // LANES):
            xc = proj[:, cidx * LANES:(cidx + 1) * LANES]
            rot = xc * cos + _swap_halves(xc) * sin
            if cidx < ATTN_W // LANES:
                q_ref[:, cidx * LANES:(cidx + 1) * LANES] = (rot * (HEAD_DIM ** -0.5)).astype(q_ref.dtype)
            else:
                for g in range(N_KV_HEADS):
                    k_ref[g] = rot[:, g * HEAD_DIM:(g + 1) * HEAD_DIM].astype(k_ref.dtype)
        for g in range(N_KV_HEADS):
            c0 = ATTN_W + KV_W + g * HEAD_DIM
            v_ref[g] = proj[:, c0:c0 + HEAD_DIM].astype(v_ref.dtype)
        u_ref[...] = proj[:, ATTN_W + 2 * KV_W:]

    row = lambda w: pl.BlockSpec((tm, w), lambda i: (i, 0))
    kv = pl.BlockSpec((N_KV_HEADS, tm, HEAD_DIM), lambda i: (0, i, 0))
    tab = pl.BlockSpec((tm, LANES), lambda i: (i % per_seq, 0))
    kv_shape = jax.ShapeDtypeStruct((N_KV_HEADS, t_tok, HEAD_DIM), MXU_DTYPE)
    return pl.pallas_call(
        body, name="fwd_inproj", grid=(t_tok // tm,),
        in_specs=[row(D_MODEL), _const((1, D_MODEL)), _resident((D_MODEL, IN_W)), tab, tab] + token_specs,
        out_specs=(row(ATTN_W), kv, kv, row(POOL_W)),
        out_shape=(jax.ShapeDtypeStruct((t_tok, ATTN_W), MXU_DTYPE), kv_shape, kv_shape,
                   jax.ShapeDtypeStruct((t_tok, POOL_W), F32)),
        compiler_params=_cp(dimension_semantics=("parallel",)),
    )(x, g1, w_in, cos_t, sin_t, *tokens)


MASKED = -1e30


def _attn_bias():
    b = lax.broadcasted_iota(jnp.int32, (2 * BLK, BLK), 0)
    a = lax.broadcasted_iota(jnp.int32, (2 * BLK, BLK), 1)
    own = (b >= BLK) & (b - BLK <= a)
    prev = (b < BLK) & (b > a)
    return jnp.stack([jnp.where(own, 0.0, MASKED), jnp.where(own | prev, 0.0, MASKED)]).astype(F32)


def _sink_rows(sinks):
    return jnp.repeat(sinks.reshape(N_KV_HEADS, Q_PER_KV), BLK, axis=1).reshape(N_KV_HEADS, 1, Q_PER_KV * BLK)


def _stack_heads(ref, r0, g):
    return jnp.concatenate(
        [ref[pl.ds(r0, BLK), (Q_PER_KV * g + h) * HEAD_DIM:(Q_PER_KV * g + h + 1) * HEAD_DIM] for h in range(Q_PER_KV)],
        axis=0)


def _kv_window(ref, p0, r0, g):
    return jnp.concatenate([ref[g, pl.ds(p0, BLK), :], ref[g, pl.ds(r0, BLK), :]], axis=0)


def _scores_t(k_ref, q_ref, bias, p0, r0, g):
    kk = _kv_window(k_ref, p0, r0, g)
    qs = _stack_heads(q_ref, r0, g)
    st = _mm_nt(kk, qs) + jnp.concatenate([bias] * Q_PER_KV, axis=1)
    return st, kk, qs


def _head_rows(ref, r0, g):
    return jnp.concatenate([ref[pl.ds(Q_PER_KV * g + h, 1), pl.ds(r0, BLK)] for h in range(Q_PER_KV)], axis=1)


def _pairs_to_rows(xt):
    out = []
    for t in range(Q_PER_KV // 2):
        pair = jnp.concatenate([xt[:, (2 * t) * BLK:(2 * t + 1) * BLK], xt[:, (2 * t + 1) * BLK:(2 * t + 2) * BLK]], axis=0)
        out.append(pair.T)
    return out


def _shift_rows(x, k, seq):
    row = lax.broadcasted_iota(jnp.int32, x.shape, 0)
    if k > 0:
        return jnp.where(row >= k, pltpu.roll(x, k, 0), 0.0)
    return jnp.where(row < seq + k, pltpu.roll(x, seq + k, 0), 0.0)


def _window_sum(x, w, seq, forward):
    s, k = x, 1
    while k < w:
        s = s + _shift_rows(s, -k if forward else k, seq)
        k *= 2
    return s


def _inv_count(seq, w):
    pos = lax.broadcasted_iota(jnp.int32, (seq, 1), 0)
    return 1.0 / jnp.minimum(pos + 1, w).astype(F32)


def _fwd_attn(q, k, v, sink_rows, bias, seq):
    t_tok = q.shape[0]
    nblk = seq // BLK

    def body(q_ref, k_ref, v_ref, sink_ref, bias_ref, o_ref, lse_ref):
        def blk(i, carry):
            r0 = pl.multiple_of(i * BLK, BLK)
            p0 = pl.multiple_of(jnp.maximum(i - 1, 0) * BLK, BLK)
            bias = bias_ref[jnp.minimum(i, 1)]
            for g in range(N_KV_HEADS):
                st, _, _ = _scores_t(k_ref, q_ref, bias, p0, r0, g)
                sink = sink_ref[g]
                m = jnp.maximum(jnp.max(st, axis=0, keepdims=True), sink)
                p = jnp.exp(st - m)
                den = jnp.sum(p, axis=0, keepdims=True) + jnp.exp(sink - m)
                lse = m + jnp.log(den)
                for h in range(Q_PER_KV):
                    lse_ref[pl.ds(Q_PER_KV * g + h, 1), pl.ds(r0, BLK)] = lse[:, h * BLK:(h + 1) * BLK]
                ot = _mm_tn(_kv_window(v_ref, p0, r0, g), p) * (1.0 / den)
                for t, tile in enumerate(_pairs_to_rows(ot)):
                    c0 = (Q_PER_KV * g + 2 * t) * HEAD_DIM
                    o_ref[pl.ds(r0, BLK), c0:c0 + LANES] = tile.astype(o_ref.dtype)
            return carry

        lax.fori_loop(0, nblk, blk, 0)

    row = lambda w: pl.BlockSpec((seq, w), lambda i: (i, 0))
    kv = pl.BlockSpec((N_KV_HEADS, seq, HEAD_DIM), lambda i: (0, i, 0))
    return pl.pallas_call(
        body, name="fwd_attn", grid=(t_tok // seq,),
        in_specs=[row(ATTN_W), kv, kv, _const((N_KV_HEADS, 1, Q_PER_KV * BLK)), _const((2, 2 * BLK, BLK))],
        out_specs=(row(ATTN_W), pl.BlockSpec((N_Q_HEADS, seq), lambda i: (0, i))),
        out_shape=(jax.ShapeDtypeStruct((t_tok, ATTN_W), MXU_DTYPE), jax.ShapeDtypeStruct((N_Q_HEADS, t_tok), F32)),
        compiler_params=_cp(dimension_semantics=("parallel",)),
    )(q, k, v, sink_rows, bias)


def _fwd_pool(u, w_pool, pool_scale, seq):
    t_tok = u.shape[0]

    def body(u_ref, wp_ref, sc_ref, o_ref):
        for gi, w in enumerate(POOL_WINDOWS):
            sl = slice(gi * POOL_G, (gi + 1) * POOL_G)
            ug = u_ref[:, sl]
            d = _window_sum(ug, w, seq, False) * _inv_count(seq, w) - ug
            o_ref[:, sl] = (_mm(d, wp_ref[gi]) * sc_ref[:, sl]).astype(o_ref.dtype)

    row = pl.BlockSpec((seq, POOL_W), lambda i: (i, 0))
    return pl.pallas_call(
        body, name="fwd_pool", grid=(t_tok // seq,),
        in_specs=[row, _const((N_POOL, POOL_G, POOL_G)), _const((1, POOL_W))], out_specs=row,
        out_shape=jax.ShapeDtypeStruct((t_tok, POOL_W), MXU_DTYPE),
        compiler_params=_cp(dimension_semantics=("parallel",)),
    )(u, w_pool, pool_scale)


def _rms_bwd(dy_g, xn, r):
    return r * (dy_g - xn * jnp.mean(dy_g * xn, axis=-1, keepdims=True))


def _fwd_mlp_loss(x, attn, pool, target, w_out, w_up4, w_down4, g2, gf, tm):
    t_tok = x.shape[0]

    def body(x_ref, attn_ref, pool_ref, tgt_ref, wo_ref, wu_ref, wd_ref, g2_ref, gf_ref,
             x1_ref, h2_ref, a_ref, dx2_ref, dx2b_ref, loss_ref, dgf_ref):
        @pl.when(pl.program_id(0) == 0)
        def _():
            loss_ref[...] = jnp.zeros_like(loss_ref)
            dgf_ref[...] = jnp.zeros_like(dgf_ref)

        x1 = x_ref[...] + (_mm(attn_ref[...], wo_ref[:ATTN_W]) + _mm(pool_ref[...], wo_ref[ATTN_W:]))
        x1_ref[...] = x1
        r2 = lax.rsqrt(jnp.mean(x1 * x1, axis=-1, keepdims=True) + EPS)
        h2 = ((x1 * r2) * g2_ref[...]).astype(MXU_DTYPE)
        h2_ref[...] = h2
        acc = jnp.zeros((tm, D_MODEL), F32)
        for j in range(N_CHIPS):
            a = _mm(h2, wu_ref[j])
            a_ref[:, j * FF_SHARD:(j + 1) * FF_SHARD] = a.astype(a_ref.dtype)
            acc = acc + _mm(jnp.square(jnp.maximum(a, 0.0)), wd_ref[j])
        x2 = x1 + acc
        r3 = lax.rsqrt(jnp.mean(x2 * x2, axis=-1, keepdims=True) + EPS)
        xn = x2 * r3
        gf_v = gf_ref[...]
        err = xn * gf_v - tgt_ref[...]
        part = jnp.sum(err * err) * (0.5 / D_MODEL)
        first = (lax.broadcasted_iota(jnp.int32, loss_ref.shape, 0) == 0) & (lax.broadcasted_iota(jnp.int32, loss_ref.shape, 1) == 0)
        loss_ref[...] += jnp.where(first, part, 0.0)
        dy = err * (1.0 / D_MODEL)
        dgf_ref[...] += jnp.sum(dy * xn, axis=0, keepdims=True)
        dx2 = _rms_bwd(dy * gf_v, xn, r3)
        dx2_ref[...] = dx2
        dx2b_ref[...] = dx2.astype(dx2b_ref.dtype)

    row = lambda w: pl.BlockSpec((tm, w), lambda i: (i, 0))
    return pl.pallas_call(
        body, name="fwd_mlp_loss", grid=(t_tok // tm,),
        in_specs=[row(D_MODEL), row(ATTN_W), row(POOL_W), row(D_MODEL), _resident((D_MODEL, D_MODEL)),
                  _resident((N_CHIPS, D_MODEL, FF_SHARD)), _resident((N_CHIPS, FF_SHARD, D_MODEL)),
                  _const((1, D_MODEL)), _const((1, D_MODEL))],
        out_specs=(row(D_MODEL), row(D_MODEL), row(D_FF), row(D_MODEL), row(D_MODEL), _const((8, LANES)), _const((1, D_MODEL))),
        out_shape=(jax.ShapeDtypeStruct((t_tok, D_MODEL), F32), jax.ShapeDtypeStruct((t_tok, D_MODEL), MXU_DTYPE),
                   jax.ShapeDtypeStruct((t_tok, D_FF), MXU_DTYPE), jax.ShapeDtypeStruct((t_tok, D_MODEL), F32),
                   jax.ShapeDtypeStruct((t_tok, D_MODEL), MXU_DTYPE), jax.ShapeDtypeStruct((8, LANES), F32),
                   jax.ShapeDtypeStruct((1, D_MODEL), F32)),
        compiler_params=_cp(dimension_semantics=("arbitrary",)),
    )(x, attn, pool, target, w_out, w_up4, w_down4, g2, gf)


def _bwd_mlp(dx2, dx2b, a, x1, w_up4, w_down4, g2, tm):
    t_tok = dx2.shape[0]

    def body(dx2_ref, dx2b_ref, a_ref, x1_ref, wu_ref, wd_ref, g2_ref, da_ref, dx1_ref, dx1b_ref, dg2_ref):
        @pl.when(pl.program_id(0) == 0)
        def _():
            dg2_ref[...] = jnp.zeros_like(dg2_ref)

        dx2b = dx2b_ref[...]
        dh2 = jnp.zeros((tm, D_MODEL), F32)
        for j in range(N_CHIPS):
            sl = slice(j * FF_SHARD, (j + 1) * FF_SHARD)
            dhid = _mm_nt(dx2b, wd_ref[j])
            da = (dhid * (2.0 * jnp.maximum(a_ref[:, sl].astype(F32), 0.0))).astype(MXU_DTYPE)
            da_ref[:, sl] = da
            dh2 = dh2 + _mm_nt(da, wu_ref[j])
        x1 = x1_ref[...]
        r2 = lax.rsqrt(jnp.mean(x1 * x1, axis=-1, keepdims=True) + EPS)
        xn = x1 * r2
        dg2_ref[...] += jnp.sum(dh2 * xn, axis=0, keepdims=True)
        dx1 = dx2_ref[...] + _rms_bwd(dh2 * g2_ref[...], xn, r2)
        dx1_ref[...] = dx1
        dx1b_ref[...] = dx1.astype(dx1b_ref.dtype)

    row = lambda w: pl.BlockSpec((tm, w), lambda i: (i, 0))
    return pl.pallas_call(
        body, name="bwd_mlp", grid=(t_tok // tm,),
        in_specs=[row(D_MODEL), row(D_MODEL), row(D_FF), row(D_MODEL),
                  _resident((N_CHIPS, D_MODEL, FF_SHARD)), _resident((N_CHIPS, FF_SHARD, D_MODEL)), _const((1, D_MODEL))],
        out_specs=(row(D_FF), row(D_MODEL), row(D_MODEL), _const((1, D_MODEL))),
        out_shape=(jax.ShapeDtypeStruct((t_tok, D_FF), MXU_DTYPE), jax.ShapeDtypeStruct((t_tok, D_MODEL), F32),
                   jax.ShapeDtypeStruct((t_tok, D_MODEL), MXU_DTYPE), jax.ShapeDtypeStruct((1, D_MODEL), F32)),
        compiler_params=_cp(dimension_semantics=("arbitrary",)),
    )(dx2, dx2b, a, x1, w_up4, w_down4, g2)


def _bwd_mlp_wgrads(h2, da, a, dx2b, tk):
    t_tok = h2.shape[0]

    def body(h2_ref, da_ref, a_ref, dx2b_ref, gup_ref, gdn_ref):
        @pl.when(pl.program_id(1) == 0)
        def _():
            gup_ref[...] = jnp.zeros_like(gup_ref)
            gdn_ref[...] = jnp.zeros_like(gdn_ref)

        gup_ref[0] += _mm_tn(h2_ref[...], da_ref[...])
        hid = jnp.square(jnp.maximum(a_ref[...].astype(F32), 0.0))
        gdn_ref[0] += _mm_tn(hid, dx2b_ref[...])

    tok = pl.BlockSpec((tk, D_MODEL), lambda j, t: (t, 0))
    ffb = pl.BlockSpec((tk, FF_SHARD), lambda j, t: (t, j))
    wblk = pl.BlockSpec((1, D_MODEL, D_MODEL), lambda j, t: (j, 0, 0))
    return pl.pallas_call(
        body, name="bwd_mlp_wgrads", grid=(N_CHIPS, t_tok // tk),
        in_specs=[tok, ffb, ffb, tok], out_specs=(wblk, wblk),
        out_shape=(jax.ShapeDtypeStruct((N_CHIPS, D_MODEL, FF_SHARD), F32), jax.ShapeDtypeStruct((N_CHIPS, FF_SHARD, D_MODEL), F32)),
        compiler_params=_cp(dimension_semantics=("parallel", "arbitrary")),
    )(h2, da, a, dx2b)


def _head_selector():
    ch = lax.broadcasted_iota(jnp.int32, (ATTN_W, LANES), 0)
    col = lax.broadcasted_iota(jnp.int32, (ATTN_W, LANES), 1)
    return (ch // HEAD_DIM == col).astype(MXU_DTYPE)


def _bwd_outproj(dx1b, attn, pool, w_out, head_sel, tm, after=()):
    t_tok = dx1b.shape[0]
    tokens, token_specs = _after(after)

    def body(dx_ref, attn_ref, pool_ref, wo_ref, sel_ref, *rest):
        dattn_ref, dpool_ref, delta_ref, gwo_ref = rest[len(tokens):]

        @pl.when(pl.program_id(0) == 0)
        def _():
            gwo_ref[...] = jnp.zeros_like(gwo_ref)

        dx = dx_ref[...]
        attn = attn_ref[...]
        dattn = _mm_nt(dx, wo_ref[:ATTN_W])
        dattn_ref[...] = dattn.astype(dattn_ref.dtype)
        dpool_ref[...] = _mm_nt(dx, wo_ref[ATTN_W:])
        prod = dattn * attn.astype(F32)
        hi = prod.astype(MXU_DTYPE)
        lo = prod - hi.astype(F32)
        delta = _mm(hi, sel_ref[...]) + _mm(lo, sel_ref[...])
        delta_ref[...] = delta.T[:N_Q_HEADS]
        gwo_ref[:ATTN_W] += _mm_tn(attn, dx)
        gwo_ref[ATTN_W:] += _mm_tn(pool_ref[...], dx)

    row = lambda w: pl.BlockSpec((tm, w), lambda i: (i, 0))
    return pl.pallas_call(
        body, name="bwd_outproj", grid=(t_tok // tm,),
        in_specs=[row(D_MODEL), row(ATTN_W), row(POOL_W), _resident((D_MODEL, D_MODEL)), _const((ATTN_W, LANES))] + token_specs,
        out_specs=(row(ATTN_W), row(POOL_W), pl.BlockSpec((N_Q_HEADS, tm), lambda i: (0, i)), _const((D_MODEL, D_MODEL))),
        out_shape=(jax.ShapeDtypeStruct((t_tok, ATTN_W), MXU_DTYPE), jax.ShapeDtypeStruct((t_tok, POOL_W), F32),
                   jax.ShapeDtypeStruct((N_Q_HEADS, t_tok), F32), jax.ShapeDtypeStruct((D_MODEL, D_MODEL), F32)),
        compiler_params=_cp(dimension_semantics=("arbitrary",)),
    )(dx1b, attn, pool, w_out, head_sel, *tokens)


def _bwd_attn(q, k, v, dattn, lse, delta, sink_rows, bias, cos_t, sin_t, seq):
    t_tok = q.shape[0]
    nblk = seq // BLK
    qkv_w = ATTN_W + 2 * KV_W

    def unrope(d, cos, sin):
        return d * cos - _swap_halves(d) * sin

    def body(q_ref, k_ref, v_ref, do_ref, lse_ref, delta_ref, sink_ref, bias_ref, cos_ref, sin_ref,
             dqkv_ref, dsink_ref, dk_acc, dv_acc):
        @pl.when(pl.program_id(0) == 0)
        def _():
            dsink_ref[...] = jnp.zeros_like(dsink_ref)

        dk_acc[...] = jnp.zeros_like(dk_acc)
        dv_acc[...] = jnp.zeros_like(dv_acc)

        def blk(i, dsink):
            r0 = pl.multiple_of(i * BLK, BLK)
            p0 = pl.multiple_of(jnp.maximum(i - 1, 0) * BLK, BLK)
            bias = bias_ref[jnp.minimum(i, 1)]
            cos, sin = cos_ref[pl.ds(r0, BLK), :], sin_ref[pl.ds(r0, BLK), :]
            new = []
            for g in range(N_KV_HEADS):
                st, kk, qs = _scores_t(k_ref, q_ref, bias, p0, r0, g)
                lse_g, delta_g = _head_rows(lse_ref, r0, g), _head_rows(delta_ref, r0, g)
                pn = jnp.exp(st - lse_g)
                dos = _stack_heads(do_ref, r0, g)
                dst = pn * (_mm_nt(_kv_window(v_ref, p0, r0, g), dos) - delta_g)
                new.append(dsink[g] - jnp.exp(sink_ref[g] - lse_g) * delta_g)
                dqt = _mm_tn(kk, dst) * (HEAD_DIM ** -0.5)
                for t, tile in enumerate(_pairs_to_rows(dqt)):
                    c0 = (Q_PER_KV * g + 2 * t) * HEAD_DIM
                    dqkv_ref[pl.ds(r0, BLK), c0:c0 + LANES] = unrope(tile, cos, sin).astype(dqkv_ref.dtype)
                dkk = _mm(dst, qs)
                dvv = _mm(pn, dos)
                dk_acc[g, pl.ds(p0, BLK), :] += dkk[:BLK]
                dk_acc[g, pl.ds(r0, BLK), :] += dkk[BLK:]
                dv_acc[g, pl.ds(p0, BLK), :] += dvv[:BLK]
                dv_acc[g, pl.ds(r0, BLK), :] += dvv[BLK:]
            return tuple(new)

        zero = jnp.zeros((1, Q_PER_KV * BLK), F32)
        dsink = lax.fori_loop(0, nblk, blk, (zero,) * N_KV_HEADS)
        lane = lax.broadcasted_iota(jnp.int32, dsink_ref.shape, 1)
        row = lax.broadcasted_iota(jnp.int32, dsink_ref.shape, 0)
        tile = jnp.zeros(dsink_ref.shape, F32)
        for g in range(N_KV_HEADS):
            for h in range(Q_PER_KV):
                tot = jnp.sum(dsink[g][:, h * BLK:(h + 1) * BLK])
                tile = tile + jnp.where((row == 0) & (lane == Q_PER_KV * g + h), tot, 0.0)
        dsink_ref[...] += tile
        dk = jnp.concatenate([dk_acc[g] for g in range(N_KV_HEADS)], axis=1)
        dqkv_ref[:, ATTN_W:ATTN_W + KV_W] = unrope(dk, cos_ref[...], sin_ref[...]).astype(dqkv_ref.dtype)
        dqkv_ref[:, ATTN_W + KV_W:] = jnp.concatenate([dv_acc[g] for g in range(N_KV_HEADS)], axis=1).astype(dqkv_ref.dtype)

    row = lambda w: pl.BlockSpec((seq, w), lambda i: (i, 0))
    kv = pl.BlockSpec((N_KV_HEADS, seq, HEAD_DIM), lambda i: (0, i, 0))
    per_head = pl.BlockSpec((N_Q_HEADS, seq), lambda i: (0, i))
    return pl.pallas_call(
        body, name="bwd_attn", grid=(t_tok // seq,),
        in_specs=[row(ATTN_W), kv, kv, row(ATTN_W), per_head, per_head, _const((N_KV_HEADS, 1, Q_PER_KV * BLK)),
                  _const((2, 2 * BLK, BLK)), _resident((seq, LANES)), _resident((seq, LANES))],
        out_specs=(row(qkv_w), _const((8, LANES))),
        out_shape=(jax.ShapeDtypeStruct((t_tok, qkv_w), MXU_DTYPE), jax.ShapeDtypeStruct((8, LANES), F32)),
        scratch_shapes=[pltpu.VMEM((N_KV_HEADS, seq, HEAD_DIM), F32), pltpu.VMEM((N_KV_HEADS, seq, HEAD_DIM), F32)],
        compiler_params=_cp(dimension_semantics=("arbitrary",)),
    )(q, k, v, dattn, lse, delta, sink_rows, bias, cos_t, sin_t)


def _bwd_pool(u, dpool, w_pool, pool_scale, seq):
    t_tok = u.shape[0]

    def body(u_ref, dp_ref, wp_ref, sc_ref, du_ref, dwp_ref, dsc_ref):
        @pl.when(pl.program_id(0) == 0)
        def _():
            dwp_ref[...] = jnp.zeros_like(dwp_ref)
            dsc_ref[...] = jnp.zeros_like(dsc_ref)

        for gi, w in enumerate(POOL_WINDOWS):
            sl = slice(gi * POOL_G, (gi + 1) * POOL_G)
            ug = u_ref[:, sl]
            inv = _inv_count(seq, w)
            d = (_window_sum(ug, w, seq, False) * inv - ug).astype(MXU_DTYPE)
            y = _mm(d, wp_ref[gi])
            dpool = dp_ref[:, sl]
            dsc_ref[:, sl] += jnp.sum(y * dpool, axis=0, keepdims=True)
            dy = (dpool * sc_ref[:, sl]).astype(MXU_DTYPE)
            dwp_ref[gi] += _mm_tn(d, dy)
            dd = _mm_nt(dy, wp_ref[gi])
            du_ref[:, sl] = (_window_sum(dd * inv, w, seq, True) - dd).astype(du_ref.dtype)

    row = pl.BlockSpec((seq, POOL_W), lambda i: (i, 0))
    return pl.pallas_call(
        body, name="bwd_pool", grid=(t_tok // seq,),
        in_specs=[row, row, _const((N_POOL, POOL_G, POOL_G)), _const((1, POOL_W))],
        out_specs=(row, _const((N_POOL, POOL_G, POOL_G)), _const((1, POOL_W))),
        out_shape=(jax.ShapeDtypeStruct((t_tok, POOL_W), MXU_DTYPE), jax.ShapeDtypeStruct((N_POOL, POOL_G, POOL_G), F32),
                   jax.ShapeDtypeStruct((1, POOL_W), F32)),
        compiler_params=_cp(dimension_semantics=("arbitrary",)),
    )(u, dpool, w_pool, pool_scale)


def _bwd_inproj(dqkv, du, x, dx1, w_in, g1, tm):
    t_tok = x.shape[0]
    nsteps = t_tok // tm

    qkv_w = ATTN_W + 2 * KV_W

    def body(dqkv_ref, du_ref, x_ref, dx1_ref, w_ref, g_ref, gx_ref, gw_ref, dg_ref, acc_ref):
        @pl.when(pl.program_id(0) == 0)
        def _():
            acc_ref[...] = jnp.zeros_like(acc_ref)
            dg_ref[...] = jnp.zeros_like(dg_ref)

        dqkv, du = dqkv_ref[...], du_ref[...]
        xv = x_ref[...]
        r = lax.rsqrt(jnp.mean(xv * xv, axis=-1, keepdims=True) + EPS)
        xn = xv * r
        g = g_ref[...]
        dh = _mm_nt(dqkv, w_ref[:, :qkv_w]) + _mm_nt(du, w_ref[:, qkv_w:])
        dg_ref[...] += jnp.sum(dh * xn, axis=0, keepdims=True)
        gx_ref[...] = dx1_ref[...] + _rms_bwd(dh * g, xn, r)
        h = (xn * g).astype(MXU_DTYPE)
        acc_ref[:, :qkv_w] += _mm_tn(h, dqkv)
        acc_ref[:, qkv_w:] += _mm_tn(h, du)

        @pl.when(pl.program_id(0) == nsteps - 1)
        def _():
            acc = acc_ref[...]
            for j in range(N_CHIPS):
                gw_ref[j] = acc[:, j * IN_SHARD:(j + 1) * IN_SHARD]

    row = lambda w: pl.BlockSpec((tm, w), lambda i: (i, 0))
    return pl.pallas_call(
        body, name="bwd_inproj", grid=(nsteps,),
        in_specs=[row(qkv_w), row(POOL_W), row(D_MODEL), row(D_MODEL), _resident((D_MODEL, IN_W)), _const((1, D_MODEL))],
        out_specs=(row(D_MODEL), _const((N_CHIPS, D_MODEL, IN_SHARD)), _const((1, D_MODEL))),
        out_shape=(jax.ShapeDtypeStruct((t_tok, D_MODEL), F32), jax.ShapeDtypeStruct((N_CHIPS, D_MODEL, IN_SHARD), F32),
                   jax.ShapeDtypeStruct((1, D_MODEL), F32)),
        scratch_shapes=[pltpu.VMEM((D_MODEL, IN_W), F32)],
        compiler_params=_cp(dimension_semantics=("arbitrary",)),
    )(dqkv, du, x, dx1, w_in, g1)


def _local_step(x, target, w_in, rest_of_weights, mlp_grads_ready, g1, sinks, w_pool, pool_scale, g2, gf, seq, after=()):
    tm = min(512, seq)
    tm_mlp = min(256, seq)
    cos_t, sin_t = _rope_tables(seq)
    sink_rows, bias = _sink_rows(sinks), _attn_bias()
    q, k, v, u = _fwd_inproj(x, g1, w_in, cos_t, sin_t, tm, after)
    attn, lse = _fwd_attn(q, k, v, sink_rows, bias, seq)
    pool = _fwd_pool(u, w_pool, pool_scale, seq)
    w_out, w_up4, w_down4 = rest_of_weights(pool)
    x1, h2, a, dx2, dx2b, loss, dgf = _fwd_mlp_loss(x, attn, pool, target, w_out, w_up4, w_down4, g2, gf, tm_mlp)
    da, dx1, dx1b, dg2 = _bwd_mlp(dx2, dx2b, a, x1, w_up4, w_down4, g2, tm_mlp)
    gw_up4, gw_down4 = _bwd_mlp_wgrads(h2, da, a, dx2b, min(1024, x.shape[0]))
    dattn, dpool, delta, gw_out = _bwd_outproj(dx1b, attn, pool, w_out, _head_selector(), tm,
                                               mlp_grads_ready(gw_up4, gw_down4))
    dqkv, dsinks = _bwd_attn(q, k, v, dattn, lse, delta, sink_rows, bias, cos_t, sin_t, seq)
    du, dwp, dsc = _bwd_pool(u, dpool, w_pool, pool_scale, seq)
    gx, gw_in4, dg1 = _bwd_inproj(dqkv, du, x, dx1, w_in, g1, tm)
    big = (gw_in4, gw_out.reshape(N_CHIPS, OUT_SHARD, D_MODEL), gw_up4, gw_down4)
    small = (dg1, dsinks, dwp.reshape(N_POOL * POOL_G, POOL_G), dsc, dg2, dgf, loss)
    return gx, big, small


def _sibling_exchange(arrs, pick_half, name):
    n = len(arrs)

    def out_shape(a):
        if pick_half:
            return (a.shape[0], a.shape[1] // 2) + a.shape[2:]
        return a.shape

    def body(*refs):
        srcs, dsts = refs[:n], refs[n:2 * n]
        send, recv = refs[2 * n:]
        x, y, c = lax.axis_index("x"), lax.axis_index("y"), lax.axis_index("c")
        cps = []
        for i in range(n):
            src = srcs[i]
            if pick_half:
                h = src.shape[1] // 2
                src = src.at[:, pl.ds((1 - c) * h, h)]
            cp = pltpu.make_async_remote_copy(src_ref=src, dst_ref=dsts[i], send_sem=send.at[i], recv_sem=recv.at[i],
                                              device_id=(x, y, 1 - c), device_id_type=MESH)
            cp.start()
            cps.append(cp)
        for cp in cps:
            cp.wait()

    hbm = pl.BlockSpec(memory_space=pltpu.HBM)
    return pl.pallas_call(
        body, name=name, out_shape=tuple(jax.ShapeDtypeStruct(out_shape(a), a.dtype) for a in arrs),
        in_specs=[hbm] * n, out_specs=tuple([hbm] * n),
        scratch_shapes=[pltpu.SemaphoreType.DMA((n,)), pltpu.SemaphoreType.DMA((n,))],
        compiler_params=_cp(),
    )(*arrs)


def _row_block(rows):
    for cand in (256, 128, 64, 32, 16, 8):
        if rows % cand == 0:
            return cand
    raise ValueError(rows)


def _chip_partial(g4, r4, c_arr, name):
    _, rows, cols = r4.shape
    rb = _row_block(rows)
    nb = rows // rb

    def body(c_ref, g_ref, r_ref, o_ref):
        o_ref[...] = (g_ref[...] + r_ref[...]).astype(o_ref.dtype)

    return pl.pallas_call(
        body, name=name,
        grid_spec=pltpu.PrefetchScalarGridSpec(
            num_scalar_prefetch=1, grid=(N_CHIPS, nb),
            in_specs=[pl.BlockSpec((1, rb, cols), lambda s, i, c: (s, c[0] * nb + i, 0)),
                      pl.BlockSpec((1, rb, cols), lambda s, i, c: (s, i, 0))],
            out_specs=pl.BlockSpec((1, rb, cols), lambda s, i, c: (s, i, 0))),
        out_shape=jax.ShapeDtypeStruct(r4.shape, BF16),
        compiler_params=_cp(dimension_semantics=("parallel", "parallel")),
    )(c_arr, g4, r4)


def _send_partials(parts):
    n = len(parts)

    def body(*refs):
        srcs, dsts = refs[:n], refs[n:2 * n]
        send, recv = refs[2 * n:]
        x, y, c = lax.axis_index("x"), lax.axis_index("y"), lax.axis_index("c")
        chips = [(1 - x, y), (x, 1 - y), (1 - x, 1 - y)]
        cps = []
        for i in range(n):
            for m, chip in enumerate(chips):
                cp = pltpu.make_async_remote_copy(
                    src_ref=srcs[i].at[2 * chip[0] + chip[1]], dst_ref=dsts[i].at[m],
                    send_sem=send.at[3 * i + m], recv_sem=recv.at[3 * i + m],
                    device_id=(chip[0], chip[1], c), device_id_type=MESH)
                cp.start()
                cps.append(cp)
        for cp in cps:
            cp.wait()

    hbm = pl.BlockSpec(memory_space=pltpu.HBM)
    return pl.pallas_call(
        body, name="send_partials",
        out_shape=tuple(jax.ShapeDtypeStruct((3,) + p.shape[1:], p.dtype) for p in parts),
        in_specs=[hbm] * n, out_specs=tuple([hbm] * n),
        scratch_shapes=[pltpu.SemaphoreType.DMA((3 * n,)), pltpu.SemaphoreType.DMA((3 * n,))],
        compiler_params=_cp(),
    )(*parts)


def _send_start(parts):
    n = len(parts)

    def body(*refs):
        srcs, lands = refs[:n], refs[2 * n:3 * n]
        send, recv, token = refs[3 * n:3 * n + 3]
        x, y, c = lax.axis_index("x"), lax.axis_index("y"), lax.axis_index("c")
        for i in range(n):
            for m, chip in enumerate(_other_chips(x, y)):
                pltpu.make_async_remote_copy(
                    src_ref=srcs[i].at[2 * chip[0] + chip[1]], dst_ref=lands[i].at[m],
                    send_sem=send.at[3 * i + m], recv_sem=recv.at[3 * i + m],
                    device_id=(chip[0], chip[1], c), device_id_type=MESH).start()
        token[...] = jnp.zeros_like(token)

    res = pl.pallas_call(
        body, name="send_start",
        out_shape=tuple(pltpu.HBM(p.shape, p.dtype) for p in parts) + tuple(pltpu.HBM((3,) + p.shape[1:], p.dtype) for p in parts) + (
            pltpu.SemaphoreType.DMA((3 * n,)), pltpu.SemaphoreType.DMA((3 * n,)), TOKEN),
        in_specs=[_HBM] * n, out_specs=tuple([_HBM] * (2 * n)) + (_SEM, _SEM, pl.BlockSpec(memory_space=pltpu.VMEM)),
        input_output_aliases={i: i for i in range(n)},
        compiler_params=_cp(has_side_effects=_EFFECT),
    )(*[pltpu.with_memory_space_constraint(p, pltpu.HBM) for p in parts])
    return res[:n], res[n:2 * n], res[2 * n], res[2 * n + 1], res[2 * n + 2]


def _send_wait(parts, lands, send, recv, after):
    n = len(parts)

    def body(*refs):
        srcs, ins = refs[:n], refs[n:2 * n]
        send_ref, recv_ref = refs[2 * n], refs[2 * n + 1]
        x, y, c = lax.axis_index("x"), lax.axis_index("y"), lax.axis_index("c")
        for i in range(n):
            for m, chip in enumerate(_other_chips(x, y)):
                cp = pltpu.make_async_remote_copy(
                    src_ref=srcs[i].at[2 * chip[0] + chip[1]], dst_ref=ins[i].at[m],
                    send_sem=send_ref.at[3 * i + m], recv_sem=recv_ref.at[3 * i + m],
                    device_id=(chip[0], chip[1], c), device_id_type=MESH)
                cp.wait_send()
                cp.wait_recv()

    res = pl.pallas_call(
        body, name="send_wait", out_shape=tuple(pltpu.HBM(a.shape, a.dtype) for a in list(parts) + list(lands)),
        in_specs=[_HBM] * (2 * n) + [_SEM, _SEM, _ANY], out_specs=tuple([_HBM] * (2 * n)),
        input_output_aliases={i: i for i in range(2 * n)},
        compiler_params=_cp(has_side_effects=_EFFECT),
    )(*parts, *lands, send, recv, after)
    return res[n:]


def _final_half(g4, r4, got3, jc_arr, name):
    _, rows, cols = r4.shape
    rb = _row_block(rows)
    nb = rows // rb

    def body(jc_ref, g_ref, r_ref, p_ref, o_ref):
        own = g_ref[0] + r_ref[0]
        o_ref[...] = ((own + p_ref[0].astype(F32)) + p_ref[1].astype(F32)) + p_ref[2].astype(F32)

    return pl.pallas_call(
        body, name=name,
        grid_spec=pltpu.PrefetchScalarGridSpec(
            num_scalar_prefetch=1, grid=(nb,),
            in_specs=[pl.BlockSpec((1, rb, cols), lambda i, jc: (jc[0], jc[1] * nb + i, 0)),
                      pl.BlockSpec((1, rb, cols), lambda i, jc: (jc[0], i, 0)),
                      pl.BlockSpec((3, rb, cols), lambda i, jc: (0, i, 0))],
            out_specs=pl.BlockSpec((rb, cols), lambda i, jc: (i, 0))),
        out_shape=jax.ShapeDtypeStruct((rows, cols), F32),
        compiler_params=_cp(dimension_semantics=("parallel",)),
    )(jc_arr, g4, r4, got3)


def _adamw_math(w, g, m, v):
    m2 = ADAM_B1 * m + (1.0 - ADAM_B1) * g
    v2 = ADAM_B2 * v + (1.0 - ADAM_B2) * (g * g)
    m_hat = m2 / (1.0 - ADAM_B1 ** ADAM_STEP)
    v_hat = v2 / (1.0 - ADAM_B2 ** ADAM_STEP)
    delta = -ADAM_LR * (m_hat / (jnp.sqrt(v_hat) + ADAM_EPS) + ADAM_WD * w)
    return delta, m2, v2


def _adamw_shard(mine, other, w, m, v, c_arr, name):
    rows, cols = w.shape
    half = rows // 2
    rb = _row_block(half)
    nb = half // rb

    def body(c_ref, a_ref, b_ref, w_ref, m_ref, v_ref, g_ref, d_ref, m2_ref, v2_ref):
        g = jnp.where(pl.program_id(0) == c_ref[0], a_ref[...], b_ref[...])
        delta, m2, v2 = _adamw_math(w_ref[...], g, m_ref[...], v_ref[...])
        g_ref[...] = g
        d_ref[...] = delta
        m2_ref[...] = m2
        v2_ref[...] = v2

    hb = pl.BlockSpec((rb, cols), lambda h, i, c: (i, 0))
    fb = pl.BlockSpec((rb, cols), lambda h, i, c: (h * nb + i, 0))
    shp = jax.ShapeDtypeStruct((rows, cols), F32)
    return pl.pallas_call(
        body, name=name,
        grid_spec=pltpu.PrefetchScalarGridSpec(num_scalar_prefetch=1, grid=(2, nb), in_specs=[hb, hb, fb, fb, fb],
                                               out_specs=(fb, fb, fb, fb)),
        out_shape=(shp, shp, shp, shp),
        compiler_params=_cp(dimension_semantics=("parallel", "parallel")),
    )(c_arr, mine, other, w, m, v)


def _small_allreduce(parts):
    n = len(parts)

    def body(*refs):
        p_refs, accs = refs[:n], refs[n:2 * n]
        bufs = refs[2 * n:3 * n]
        send, recv = refs[3 * n:]
        x, y, c = lax.axis_index("x"), lax.axis_index("y"), lax.axis_index("c")
        partners = [(x, y, 1 - c), (1 - x, y, c), (x, 1 - y, c)]
        for i in range(n):
            accs[i][...] = p_refs[i][...]
        for s, partner in enumerate(partners):
            cps = []
            for i in range(n):
                cp = pltpu.make_async_remote_copy(src_ref=accs[i], dst_ref=bufs[i].at[s], send_sem=send.at[3 * i + s],
                                                  recv_sem=recv.at[3 * i + s], device_id=partner, device_id_type=MESH)
                cp.start()
                cps.append(cp)
            for cp in cps:
                cp.wait()
            for i in range(n):
                accs[i][...] = accs[i][...] + bufs[i][s]

    vmem = pl.BlockSpec(memory_space=pltpu.VMEM)
    return pl.pallas_call(
        body, name="small_allreduce", out_shape=tuple(jax.ShapeDtypeStruct(p.shape, F32) for p in parts),
        in_specs=[vmem] * n, out_specs=tuple([vmem] * n),
        scratch_shapes=[pltpu.VMEM((3,) + p.shape, F32) for p in parts] + [
            pltpu.SemaphoreType.DMA((3 * n,)), pltpu.SemaphoreType.DMA((3 * n,))],
        compiler_params=_cp(),
    )(*parts)


def _small_adamw(reduced, params):
    n = len(reduced)
    n_w = len(params)

    def body(*refs):
        r_refs = refs[:n]
        wmv = refs[n:n + 3 * n_w]
        outs = refs[n + 3 * n_w:]
        outs[0][...] = r_refs[n - 1][0:1, 0:1]
        grads = [r_refs[0][...], r_refs[1][0:1, 0:N_Q_HEADS]] + [r_refs[i][...] for i in range(2, n_w)]
        for i in range(n_w):
            w_ref, m_ref, v_ref = wmv[3 * i:3 * i + 3]
            g_ref, d_ref, m2_ref, v2_ref = outs[1 + 4 * i:5 + 4 * i]
            delta, m2, v2 = _adamw_math(w_ref[...], grads[i], m_ref[...], v_ref[...])
            g_ref[...] = grads[i]
            d_ref[...] = delta
            m2_ref[...] = m2
            v2_ref[...] = v2

    flat = [a for p in params for a in p]
    vmem = pl.BlockSpec(memory_space=pltpu.VMEM)
    out_shape = [jax.ShapeDtypeStruct((1, 1), F32)]
    for p in params:
        out_shape += [jax.ShapeDtypeStruct(p[0].shape, F32)] * 4
    res = pl.pallas_call(
        body, name="small_adamw", out_shape=tuple(out_shape),
        in_specs=[vmem] * (n + len(flat)), out_specs=tuple([vmem] * len(out_shape)),
        compiler_params=_cp(),
    )(*reduced, *flat)
    return res[0], [res[1 + 4 * i:5 + 4 * i] for i in range(n_w)]


def kernel(x, attn_norm_g, w_in, attn_sinks, w_pool, pool_scale, w_out, mlp_norm_g, w_up, w_down, final_norm_g, loss_target, m_attn_norm_g, m_w_in, m_attn_sinks, m_w_pool, m_pool_scale, m_w_out, m_mlp_norm_g, m_w_up, m_w_down, m_final_norm_g, v_attn_norm_g, v_w_in, v_attn_sinks, v_w_pool, v_pool_scale, v_w_out, v_mlp_norm_g, v_w_up, v_w_down, v_final_norm_g):
    nseq, seq, d = x.shape
    c_idx = lax.axis_index("c").astype(jnp.int32)
    j_idx = (2 * lax.axis_index("x") + lax.axis_index("y")).astype(jnp.int32)
    c_arr = jnp.reshape(c_idx, (1,))
    jc_arr = jnp.stack([j_idx, c_idx])

    big_w = (w_in[0], w_out[0], w_up[0], w_down[0])
    big_m = (m_w_in[0], m_w_out[0], m_w_up[0], m_w_down[0])
    big_v = (v_w_in[0], v_w_out[0], v_w_up[0], v_w_down[0])
    (w_in4,) = _gather_weights(big_w[0])
    w_in_full = _concat_w_in(w_in4)
    rest_landing, rest_send, rest_recv, gather_token = _gather_start(*big_w[1:])

    def rest_of_weights(behind):
        w_out4, w_up4, w_down4 = _gather_forward(_gather_wait(rest_landing, rest_send, rest_recv, behind))
        return w_out4.reshape(D_MODEL, D_MODEL), w_up4, w_down4

    mlp_state = {}

    def mlp_grads_ready(gw_up4, gw_down4):
        grads = (gw_up4, gw_down4)
        from_sib = _sibling_exchange(grads, True, "mlp_grads_to_sibling")
        partials = [_chip_partial(g, r, c_arr, "chip_partial_" + nm) for g, r, nm in zip(grads, from_sib, ("w_up", "w_down"))]
        parts, lands, send, recv, token = _send_start(partials)
        mlp_state.update(from_sib=from_sib, parts=parts, lands=lands, send=send, recv=recv)
        return (token,)

    gx, big_g, small_g = _local_step(
        x.reshape(nseq * seq, d), loss_target.reshape(nseq * seq, d), w_in_full, rest_of_weights, mlp_grads_ready,
        attn_norm_g, attn_sinks.reshape(N_Q_HEADS), w_pool[0], pool_scale, mlp_norm_g, final_norm_g.reshape(1, d), seq,
        (gather_token,))

    names = ("w_in", "w_out", "w_up", "w_down")
    from_sibling = list(_sibling_exchange(big_g[:2], True, "grads_to_sibling")) + list(mlp_state["from_sib"])
    partials = [_chip_partial(g, r, c_arr, "chip_partial_" + nm) for g, r, nm in zip(big_g[:2], from_sibling[:2], names[:2])]
    received = list(_send_partials(partials))
    received += list(_send_wait(mlp_state["parts"], mlp_state["lands"], mlp_state["send"], mlp_state["recv"], received[0]))
    mine = [_final_half(g, r, p, jc_arr, "final_half_" + nm) for g, r, p, nm in zip(big_g, from_sibling, received, names)]
    other = _sibling_exchange(mine, False, "halves_to_sibling")
    big_out = [_adamw_shard(a, b, w, m, v, c_arr, "adamw_" + nm)
               for a, b, w, m, v, nm in zip(mine, other, big_w, big_m, big_v, names)]

    wp_flat = lambda a: a.reshape(N_POOL * POOL_G, POOL_G)
    small_params = [
        (attn_norm_g, m_attn_norm_g, v_attn_norm_g),
        (attn_sinks, m_attn_sinks, v_attn_sinks),
        (wp_flat(w_pool), wp_flat(m_w_pool), wp_flat(v_w_pool)),
        (pool_scale, m_pool_scale, v_pool_scale),
        (mlp_norm_g, m_mlp_norm_g, v_mlp_norm_g),
        (final_norm_g.reshape(1, d), m_final_norm_g.reshape(1, d), v_final_norm_g.reshape(1, d)),
    ]
    loss, small_out = _small_adamw(_small_allreduce(small_g), small_params)

    def shaped(i, arr):
        return {1: w_in, 5: w_out, 7: w_up, 8: w_down, 0: attn_norm_g, 2: attn_sinks, 3: w_pool, 4: pool_scale,
                6: mlp_norm_g, 9: final_norm_g}[i].shape

    order = [small_out[0], big_out[0], small_out[1], small_out[2], small_out[3], big_out[1], small_out[4], big_out[2],
             big_out[3], small_out[5]]
    outs = [loss.reshape(()), gx.reshape(nseq, seq, d)]
    for kind in range(4):
        outs += [order[i][kind].reshape(shaped(i, None)) for i in range(10)]
    return tuple(outs)
```

```python
import jax
import jax.numpy as jnp
from jax import lax
from jax.experimental import pallas as pl
from jax.experimental.pallas import tpu as pltpu

F32 = jnp.float32
BF16 = jnp.bfloat16
MXU_DTYPE = jnp.bfloat16

D_MODEL = 1024
HEAD_DIM = 64
N_Q_HEADS = 8
N_KV_HEADS = 2
Q_PER_KV = N_Q_HEADS // N_KV_HEADS
ATTN_W = N_Q_HEADS * HEAD_DIM
KV_W = N_KV_HEADS * HEAD_DIM
BLK = 128
POOL_WINDOWS = (2, 4, 8, 16)
N_POOL = len(POOL_WINDOWS)
POOL_W = D_MODEL - ATTN_W
POOL_G = POOL_W // N_POOL
IN_W = ATTN_W + 2 * KV_W + POOL_W
D_FF = 4 * D_MODEL
EPS = 1e-6
ROPE_THETA = 10000.0
N_CHIPS = 4
IN_SHARD = IN_W // N_CHIPS
OUT_SHARD = D_MODEL // N_CHIPS
FF_SHARD = D_FF // N_CHIPS
LANES = 128

ADAM_LR = 0.001
ADAM_B1 = 0.9
ADAM_B2 = 0.999
ADAM_EPS = 1e-08
ADAM_WD = 0.01
ADAM_STEP = 10

VMEM_LIMIT = 56 * 1024 * 1024
MESH = pl.DeviceIdType.MESH


def _cp(**kw):
    return pltpu.CompilerParams(vmem_limit_bytes=VMEM_LIMIT, **kw)


def _mm(a, b):
    return jnp.dot(a.astype(MXU_DTYPE), b.astype(MXU_DTYPE), preferred_element_type=F32)


def _mm_nt(a, b):
    return lax.dot_general(a.astype(MXU_DTYPE), b.astype(MXU_DTYPE), (((1,), (1,)), ((), ())),
                           preferred_element_type=F32)


def _mm_tn(a, b):
    return lax.dot_general(a.astype(MXU_DTYPE), b.astype(MXU_DTYPE), (((0,), (0,)), ((), ())),
                           preferred_element_type=F32)


def _resident(shape):
    nd = len(shape)
    return pl.BlockSpec(shape, lambda *_: (0,) * nd, pipeline_mode=pl.Buffered(1))


def _const(shape):
    nd = len(shape)
    return pl.BlockSpec(shape, lambda *_: (0,) * nd)


def _rope_tables(seq):
    half = HEAD_DIM // 2
    inv_freq = ROPE_THETA ** (-jnp.arange(half, dtype=F32) / half)
    ang = jnp.arange(seq, dtype=F32)[:, None] * inv_freq[None, :]
    cos, sin = jnp.cos(ang), jnp.sin(ang)
    cos_t = jnp.concatenate([cos, cos, cos, cos], axis=1)
    sin_t = jnp.concatenate([-sin, sin, -sin, sin], axis=1)
    return cos_t, sin_t


def _swap_halves(xc):
    lane = lax.broadcasted_iota(jnp.int32, xc.shape, 1)
    return jnp.where((lane & 63) < 32, pltpu.roll(xc, 96, 1), pltpu.roll(xc, 32, 1))


def _gather_weights(*shards):
    nw = len(shards)

    def body(*refs):
        srcs = refs[0:nw]
        outs = refs[nw:2 * nw]
        stages = refs[2 * nw:3 * nw]
        send1, recv1, send2, recv2, lsem = refs[3 * nw:]
        x, y, c = lax.axis_index("x"), lax.axis_index("y"), lax.axis_index("c")
        j = 2 * x + y
        chips = [(1 - x, y), (x, 1 - y), (1 - x, 1 - y)]
        for k in range(nw):
            stages[k][...] = srcs[k][...].astype(BF16)
        local = [pltpu.make_async_copy(stages[k], outs[k].at[j], lsem.at[k]) for k in range(nw)]
        for cp in local:
            cp.start()

        def piece(k, chip, half):
            rows = shards[k].shape[0] // 2
            return outs[k].at[2 * chip[0] + chip[1], pl.ds(half * rows, rows)]

        first, passed = [], []
        for k in range(nw):
            rows = shards[k].shape[0] // 2
            for n, chip in enumerate(chips):
                cp = pltpu.make_async_remote_copy(
                    src_ref=stages[k].at[pl.ds(c * rows, rows)], dst_ref=piece(k, (x, y), c),
                    send_sem=send1.at[k * 3 + n], recv_sem=recv1.at[k * 3 + n],
                    device_id=(chip[0], chip[1], c), device_id_type=MESH)
                cp.start()
                first.append(cp)
        for k in range(nw):
            for n, chip in enumerate(chips):
                got = piece(k, chip, c)
                pltpu.make_async_remote_copy(
                    src_ref=got, dst_ref=got, send_sem=send1.at[k * 3 + n], recv_sem=recv1.at[k * 3 + n],
                    device_id=(chip[0], chip[1], c), device_id_type=MESH).wait_recv()
                cp = pltpu.make_async_remote_copy(
                    src_ref=got, dst_ref=got, send_sem=send2.at[k * 3 + n], recv_sem=recv2.at[k * 3 + n],
                    device_id=(x, y, 1 - c), device_id_type=MESH)
                cp.start()
                passed.append(cp)
        for k in range(nw):
            for n, chip in enumerate(chips):
                got = piece(k, chip, 1 - c)
                pltpu.make_async_remote_copy(
                    src_ref=got, dst_ref=got, send_sem=send2.at[k * 3 + n], recv_sem=recv2.at[k * 3 + n],
                    device_id=(x, y, 1 - c), device_id_type=MESH).wait_recv()
        for cp in first + passed:
            cp.wait_send()
        for cp in local:
            cp.wait()

    hbm = pl.BlockSpec(memory_space=pltpu.HBM)
    vmem = pl.BlockSpec(memory_space=pltpu.VMEM)
    return pl.pallas_call(
        body, name="gather_weights",
        out_shape=tuple(jax.ShapeDtypeStruct((N_CHIPS,) + s.shape, BF16) for s in shards),
        in_specs=[vmem] * nw, out_specs=tuple([hbm] * nw),
        scratch_shapes=[pltpu.VMEM(s.shape, BF16) for s in shards] + [
            pltpu.SemaphoreType.DMA((3 * nw,)), pltpu.SemaphoreType.DMA((3 * nw,)),
            pltpu.SemaphoreType.DMA((3 * nw,)), pltpu.SemaphoreType.DMA((3 * nw,)),
            pltpu.SemaphoreType.DMA((nw,))],
        compiler_params=_cp(),
    )(*shards)


_HBM = pl.BlockSpec(memory_space=pltpu.HBM)
_SEM = pl.BlockSpec(memory_space=pltpu.SEMAPHORE)
_ANY = pl.BlockSpec(memory_space=pl.ANY)
_EFFECT = pltpu.SideEffectType.DATAFLOW_SIDE_EFFECTING
TOKEN = jax.ShapeDtypeStruct((8, LANES), F32)


def _other_chips(x, y):
    return [(1 - x, y), (x, 1 - y), (1 - x, 1 - y)]


def _gather_start(shards, after):
    nw = len(shards)

    def body(*refs):
        srcs, lands = refs[:nw], refs[nw + 1:2 * nw + 1]
        send, recv, token = refs[2 * nw + 1:2 * nw + 4]
        stages = refs[2 * nw + 4:3 * nw + 4]
        lsem = refs[3 * nw + 4]
        x, y, c = lax.axis_index("x"), lax.axis_index("y"), lax.axis_index("c")
        j = 2 * x + y
        for k in range(nw):
            stages[k][...] = srcs[k][...].astype(BF16)
        local = [pltpu.make_async_copy(stages[k], lands[k].at[j], lsem.at[k]) for k in range(nw)]
        for cp in local:
            cp.start()
        for cp in local:
            cp.wait()
        for k in range(nw):
            rows = shards[k].shape[0] // 2
            mine = lands[k].at[j, pl.ds(c * rows, rows)]
            for n, chip in enumerate(_other_chips(x, y)):
                pltpu.make_async_remote_copy(
                    src_ref=mine, dst_ref=mine, send_sem=send.at[k * 3 + n], recv_sem=recv.at[k * 3 + n],
                    device_id=(chip[0], chip[1], c), device_id_type=MESH).start()
        token[...] = jnp.zeros_like(token)

    vmem = pl.BlockSpec(memory_space=pltpu.VMEM)
    res = pl.pallas_call(
        body, name="gather_start",
        out_shape=tuple(pltpu.HBM((N_CHIPS,) + s.shape, BF16) for s in shards) + (
            pltpu.SemaphoreType.DMA((3 * nw,)), pltpu.SemaphoreType.DMA((3 * nw,)), TOKEN),
        in_specs=[vmem] * nw + [_ANY], out_specs=tuple([_HBM] * nw) + (_SEM, _SEM, vmem),
        scratch_shapes=[pltpu.VMEM(s.shape, BF16) for s in shards] + [pltpu.SemaphoreType.DMA((nw,))],
        compiler_params=_cp(has_side_effects=_EFFECT),
    )(*shards, after)
    return res[:nw], res[nw], res[nw + 1], res[nw + 2]


def _gather_wait(lands, send, recv, after):
    nw = len(lands)

    def body(*refs):
        ins = refs[:nw]
        send_ref, recv_ref = refs[nw], refs[nw + 1]
        x, y, c = lax.axis_index("x"), lax.axis_index("y"), lax.axis_index("c")
        j = 2 * x + y
        for k in range(nw):
            rows = lands[k].shape[1] // 2
            mine = ins[k].at[j, pl.ds(c * rows, rows)]
            for n, chip in enumerate(_other_chips(x, y)):
                got = ins[k].at[2 * chip[0] + chip[1], pl.ds(c * rows, rows)]
                cp = pltpu.make_async_remote_copy(
                    src_ref=mine, dst_ref=got, send_sem=send_ref.at[k * 3 + n], recv_sem=recv_ref.at[k * 3 + n],
                    device_id=(chip[0], chip[1], c), device_id_type=MESH)
                cp.wait_send()
                cp.wait_recv()

    return pl.pallas_call(
        body, name="gather_wait", out_shape=tuple(pltpu.HBM(a.shape, a.dtype) for a in lands),
        in_specs=[_HBM] * nw + [_SEM, _SEM, _ANY], out_specs=tuple([_HBM] * nw),
        input_output_aliases={k: k for k in range(nw)},
        compiler_params=_cp(has_side_effects=_EFFECT),
    )(*lands, send, recv, after)


def _forward_copies(refs, nw, rows_of, send, recv):
    x, y, c = lax.axis_index("x"), lax.axis_index("y"), lax.axis_index("c")
    out = []
    for k in range(nw):
        rows = rows_of[k]
        for n, chip in enumerate(_other_chips(x, y)):
            got = refs[k].at[2 * chip[0] + chip[1], pl.ds(c * rows, rows)]
            theirs = refs[k].at[2 * chip[0] + chip[1], pl.ds((1 - c) * rows, rows)]
            out.append(pltpu.make_async_remote_copy(
                src_ref=got, dst_ref=got, send_sem=send.at[k * 3 + n], recv_sem=recv.at[k * 3 + n],
                device_id=(x, y, 1 - c), device_id_type=MESH))
            out.append(pltpu.make_async_remote_copy(
                src_ref=theirs, dst_ref=theirs, send_sem=send.at[k * 3 + n], recv_sem=recv.at[k * 3 + n],
                device_id=(x, y, 1 - c), device_id_type=MESH))
    return out[0::2], out[1::2]


def _forward_start(lands):
    nw = len(lands)
    rows_of = [a.shape[1] // 2 for a in lands]

    def body(*refs):
        send, recv, token = refs[2 * nw:2 * nw + 3]
        mine, _ = _forward_copies(refs[:nw], nw, rows_of, send, recv)
        for cp in mine:
            cp.start()
        token[...] = jnp.zeros_like(token)

    res = pl.pallas_call(
        body, name="forward_start",
        out_shape=tuple(pltpu.HBM(a.shape, a.dtype) for a in lands) + (
            pltpu.SemaphoreType.DMA((3 * nw,)), pltpu.SemaphoreType.DMA((3 * nw,)), TOKEN),
        in_specs=[_HBM] * nw, out_specs=tuple([_HBM] * nw) + (_SEM, _SEM, pl.BlockSpec(memory_space=pltpu.VMEM)),
        input_output_aliases={k: k for k in range(nw)},
        compiler_params=_cp(has_side_effects=_EFFECT),
    )(*lands)
    return res[:nw], res[nw], res[nw + 1], res[nw + 2]


def _forward_wait(lands, send, recv, after):
    nw = len(lands)
    rows_of = [a.shape[1] // 2 for a in lands]

    def body(*refs):
        mine, theirs = _forward_copies(refs[:nw], nw, rows_of, refs[nw], refs[nw + 1])
        for cp in mine:
            cp.wait_send()
        for cp in theirs:
            cp.wait_recv()

    return pl.pallas_call(
        body, name="forward_wait", out_shape=tuple(pltpu.HBM(a.shape, a.dtype) for a in lands),
        in_specs=[_HBM] * nw + [_SEM, _SEM, _ANY], out_specs=tuple([_HBM] * nw),
        input_output_aliases={k: k for k in range(nw)},
        compiler_params=_cp(has_side_effects=_EFFECT),
    )(*lands, send, recv, after)


def _concat_w_in(w_in4):
    def body(s_ref, o_ref):
        o_ref[...] = jnp.concatenate([s_ref[j] for j in range(N_CHIPS)], axis=1)

    return pl.pallas_call(body, name="concat_w_in", out_shape=jax.ShapeDtypeStruct((D_MODEL, IN_W), w_in4.dtype),
                          compiler_params=_cp())(w_in4)


def _after(tokens):
    tokens = [t for t in tokens if t is not None]
    return tokens, [_ANY] * len(tokens)


def _fwd_inproj(x, g1, w_in, cos_t, sin_t, tm, after=()):
    t_tok = x.shape[0]
    seq = cos_t.shape[0]
    per_seq = seq // tm
    tokens, token_specs = _after(after)

    def body(x_ref, g_ref, w_ref, cos_ref, sin_ref, *rest):
        q_ref, k_ref, v_ref, u_ref = rest[len(tokens):]
        xv = x_ref[...]
        r = lax.rsqrt(jnp.mean(xv * xv, axis=-1, keepdims=True) + EPS)
        h = (xv * r) * g_ref[...]
        proj = _mm(h, w_ref[...])
        cos, sin = cos_ref[...], sin_ref[...]
        for cidx in range((ATTN_W + KV_W) // LANES):
            xc = proj[:, cidx * LANES:(cidx + 1) * LANES]
            rot = xc * cos + _swap_halves(xc) * sin
            if cidx < ATTN_W // LANES:
                q_ref[:, cidx * LANES:(cidx + 1) * LANES] = (rot * (HEAD_DIM ** -0.5)).astype(q_ref.dtype)
            else:
                for g in range(N_KV_HEADS):
                    k_ref[g] = rot[:, g * HEAD_DIM:(g + 1) * HEAD_DIM].astype(k_ref.dtype)
        for g in range(N_KV_HEADS):
            c0 = ATTN_W + KV_W + g * HEAD_DIM
            v_ref[g] = proj[:, c0:c0 + HEAD_DIM].astype(v_ref.dtype)
        u_ref[...] = proj[:, ATTN_W + 2 * KV_W:]

    row = lambda w: pl.BlockSpec((tm, w), lambda i: (i, 0))
    kv = pl.BlockSpec((N_KV_HEADS, tm, HEAD_DIM), lambda i: (0, i, 0))
    tab = pl.BlockSpec((tm, LANES), lambda i: (i % per_seq, 0))
    kv_shape = jax.ShapeDtypeStruct((N_KV_HEADS, t_tok, HEAD_DIM), MXU_DTYPE)
    return pl.pallas_call(
        body, name="fwd_inproj", grid=(t_tok // tm,),
        in_specs=[row(D_MODEL), _const((1, D_MODEL)), _resident((D_MODEL, IN_W)), tab, tab] + token_specs,
        out_specs=(row(ATTN_W), kv, kv, row(POOL_W)),
        out_shape=(jax.ShapeDtypeStruct((t_tok, ATTN_W), MXU_DTYPE), kv_shape, kv_shape,
                   jax.ShapeDtypeStruct((t_tok, POOL_W), F32)),
        compiler_params=_cp(dimension_semantics=("parallel",)),
    )(x, g1, w_in, cos_t, sin_t, *tokens)


MASKED = -1e30


def _attn_bias():
    b = lax.broadcasted_iota(jnp.int32, (2 * BLK, BLK), 0)
    a = lax.broadcasted_iota(jnp.int32, (2 * BLK, BLK), 1)
    own = (b >= BLK) & (b - BLK <= a)
    prev = (b < BLK) & (b > a)
    return jnp.stack([jnp.where(own, 0.0, MASKED), jnp.where(own | prev, 0.0, MASKED)]).astype(F32)


def _sink_rows(sinks):
    return jnp.repeat(sinks.reshape(N_KV_HEADS, Q_PER_KV), BLK, axis=1).reshape(N_KV_HEADS, 1, Q_PER_KV * BLK)


def _stack_heads(ref, r0, g):
    return jnp.concatenate(
        [ref[pl.ds(r0, BLK), (Q_PER_KV * g + h) * HEAD_DIM:(Q_PER_KV * g + h + 1) * HEAD_DIM] for h in range(Q_PER_KV)],
        axis=0)


def _kv_window(ref, p0, r0, g):
    return jnp.concatenate([ref[g, pl.ds(p0, BLK), :], ref[g, pl.ds(r0, BLK), :]], axis=0)


def _scores_t(k_ref, q_ref, bias, p0, r0, g):
    kk = _kv_window(k_ref, p0, r0, g)
    qs = _stack_heads(q_ref, r0, g)
    st = _mm_nt(kk, qs) + jnp.concatenate([bias] * Q_PER_KV, axis=1)
    return st, kk, qs


def _head_rows(ref, r0, g):
    return jnp.concatenate([ref[pl.ds(Q_PER_KV * g + h, 1), pl.ds(r0, BLK)] for h in range(Q_PER_KV)], axis=1)


def _pairs_to_rows(xt):
    out = []
    for t in range(Q_PER_KV // 2):
        pair = jnp.concatenate([xt[:, (2 * t) * BLK:(2 * t + 1) * BLK], xt[:, (2 * t + 1) * BLK:(2 * t + 2) * BLK]], axis=0)
        out.append(pair.T)
    return out


def _shift_rows(x, k, seq):
    row = lax.broadcasted_iota(jnp.int32, x.shape, 0)
    if k > 0:
        return jnp.where(row >= k, pltpu.roll(x, k, 0), 0.0)
    return jnp.where(row < seq + k, pltpu.roll(x, seq + k, 0), 0.0)


def _window_sum(x, w, seq, forward):
    s, k = x, 1
    while k < w:
        s = s + _shift_rows(s, -k if forward else k, seq)
        k *= 2
    return s


def _inv_count(seq, w):
    pos = lax.broadcasted_iota(jnp.int32, (seq, 1), 0)
    return 1.0 / jnp.minimum(pos + 1, w).astype(F32)


def _fwd_attn(q, k, v, sink_rows, bias, seq):
    t_tok = q.shape[0]
    nblk = seq // BLK

    def body(q_ref, k_ref, v_ref, sink_ref, bias_ref, o_ref, lse_ref):
        def blk(i, carry):
            r0 = pl.multiple_of(i * BLK, BLK)
            p0 = pl.multiple_of(jnp.maximum(i - 1, 0) * BLK, BLK)
            bias = bias_ref[jnp.minimum(i, 1)]
            for g in range(N_KV_HEADS):
                st, _, _ = _scores_t(k_ref, q_ref, bias, p0, r0, g)
                sink = sink_ref[g]
                m = jnp.maximum(jnp.max(st, axis=0, keepdims=True), sink)
                p = jnp.exp(st - m)
                den = jnp.sum(p, axis=0, keepdims=True) + jnp.exp(sink - m)
                lse = m + jnp.log(den)
                for h in range(Q_PER_KV):
                    lse_ref[pl.ds(Q_PER_KV * g + h, 1), pl.ds(r0, BLK)] = lse[:, h * BLK:(h + 1) * BLK]
                ot = _mm_tn(_kv_window(v_ref, p0, r0, g), p) * (1.0 / den)
                for t, tile in enumerate(_pairs_to_rows(ot)):
                    c0 = (Q_PER_KV * g + 2 * t) * HEAD_DIM
                    o_ref[pl.ds(r0, BLK), c0:c0 + LANES] = tile.astype(o_ref.dtype)
            return carry

        lax.fori_loop(0, nblk, blk, 0)

    row = lambda w: pl.BlockSpec((seq, w), lambda i: (i, 0))
    kv = pl.BlockSpec((N_KV_HEADS, seq, HEAD_DIM), lambda i: (0, i, 0))
    return pl.pallas_call(
        body, name="fwd_attn", grid=(t_tok // seq,),
        in_specs=[row(ATTN_W), kv, kv, _const((N_KV_HEADS, 1, Q_PER_KV * BLK)), _const((2, 2 * BLK, BLK))],
        out_specs=(row(ATTN_W), pl.BlockSpec((N_Q_HEADS, seq), lambda i: (0, i))),
        out_shape=(jax.ShapeDtypeStruct((t_tok, ATTN_W), MXU_DTYPE), jax.ShapeDtypeStruct((N_Q_HEADS, t_tok), F32)),
        compiler_params=_cp(dimension_semantics=("parallel",)),
    )(q, k, v, sink_rows, bias)


def _fwd_pool(u, w_pool, pool_scale, seq, after=()):
    t_tok = u.shape[0]
    tokens, token_specs = _after(after)

    def body(u_ref, wp_ref, sc_ref, *rest):
        o_ref = rest[-1]
        for gi, w in enumerate(POOL_WINDOWS):
            sl = slice(gi * POOL_G, (gi + 1) * POOL_G)
            ug = u_ref[:, sl]
            d = _window_sum(ug, w, seq, False) * _inv_count(seq, w) - ug
            o_ref[:, sl] = (_mm(d, wp_ref[gi]) * sc_ref[:, sl]).astype(o_ref.dtype)

    row = pl.BlockSpec((seq, POOL_W), lambda i: (i, 0))
    return pl.pallas_call(
        body, name="fwd_pool", grid=(t_tok // seq,),
        in_specs=[row, _const((N_POOL, POOL_G, POOL_G)), _const((1, POOL_W))] + token_specs, out_specs=row,
        out_shape=jax.ShapeDtypeStruct((t_tok, POOL_W), MXU_DTYPE),
        compiler_params=_cp(dimension_semantics=("parallel",)),
    )(u, w_pool, pool_scale, *tokens)


def _rms_bwd(dy_g, xn, r):
    return r * (dy_g - xn * jnp.mean(dy_g * xn, axis=-1, keepdims=True))


def _fwd_mlp_loss(x, attn, pool, target, w_out, w_up4, w_down4, g2, gf, tm):
    t_tok = x.shape[0]

    def body(x_ref, attn_ref, pool_ref, tgt_ref, wo_ref, wu_ref, wd_ref, g2_ref, gf_ref,
             x1_ref, h2_ref, a_ref, dx2_ref, dx2b_ref, loss_ref, dgf_ref):
        @pl.when(pl.program_id(0) == 0)
        def _():
            loss_ref[...] = jnp.zeros_like(loss_ref)
            dgf_ref[...] = jnp.zeros_like(dgf_ref)

        x1 = x_ref[...] + (_mm(attn_ref[...], wo_ref[:ATTN_W]) + _mm(pool_ref[...], wo_ref[ATTN_W:]))
        x1_ref[...] = x1
        r2 = lax.rsqrt(jnp.mean(x1 * x1, axis=-1, keepdims=True) + EPS)
        h2 = ((x1 * r2) * g2_ref[...]).astype(MXU_DTYPE)
        h2_ref[...] = h2
        acc = jnp.zeros((tm, D_MODEL), F32)
        for j in range(N_CHIPS):
            a = _mm(h2, wu_ref[j])
            a_ref[:, j * FF_SHARD:(j + 1) * FF_SHARD] = a.astype(a_ref.dtype)
            acc = acc + _mm(jnp.square(jnp.maximum(a, 0.0)), wd_ref[j])
        x2 = x1 + acc
        r3 = lax.rsqrt(jnp.mean(x2 * x2, axis=-1, keepdims=True) + EPS)
        xn = x2 * r3
        gf_v = gf_ref[...]
        err = xn * gf_v - tgt_ref[...]
        part = jnp.sum(err * err) * (0.5 / D_MODEL)
        first = (lax.broadcasted_iota(jnp.int32, loss_ref.shape, 0) == 0) & (lax.broadcasted_iota(jnp.int32, loss_ref.shape, 1) == 0)
        loss_ref[...] += jnp.where(first, part, 0.0)
        dy = err * (1.0 / D_MODEL)
        dgf_ref[...] += jnp.sum(dy * xn, axis=0, keepdims=True)
        dx2 = _rms_bwd(dy * gf_v, xn, r3)
        dx2_ref[...] = dx2
        dx2b_ref[...] = dx2.astype(dx2b_ref.dtype)

    row = lambda w: pl.BlockSpec((tm, w), lambda i: (i, 0))
    return pl.pallas_call(
        body, name="fwd_mlp_loss", grid=(t_tok // tm,),
        in_specs=[row(D_MODEL), row(ATTN_W), row(POOL_W), row(D_MODEL), _resident((D_MODEL, D_MODEL)),
                  _resident((N_CHIPS, D_MODEL, FF_SHARD)), _resident((N_CHIPS, FF_SHARD, D_MODEL)),
                  _const((1, D_MODEL)), _const((1, D_MODEL))],
        out_specs=(row(D_MODEL), row(D_MODEL), row(D_FF), row(D_MODEL), row(D_MODEL), _const((8, LANES)), _const((1, D_MODEL))),
        out_shape=(jax.ShapeDtypeStruct((t_tok, D_MODEL), F32), jax.ShapeDtypeStruct((t_tok, D_MODEL), MXU_DTYPE),
                   jax.ShapeDtypeStruct((t_tok, D_FF), MXU_DTYPE), jax.ShapeDtypeStruct((t_tok, D_MODEL), F32),
                   jax.ShapeDtypeStruct((t_tok, D_MODEL), MXU_DTYPE), jax.ShapeDtypeStruct((8, LANES), F32),
                   jax.ShapeDtypeStruct((1, D_MODEL), F32)),
        compiler_params=_cp(dimension_semantics=("arbitrary",)),
    )(x, attn, pool, target, w_out, w_up4, w_down4, g2, gf)


def _bwd_mlp(dx2, dx2b, a, x1, w_up4, w_down4, g2, tm):
    t_tok = dx2.shape[0]

    def body(dx2_ref, dx2b_ref, a_ref, x1_ref, wu_ref, wd_ref, g2_ref, da_ref, dx1_ref, dx1b_ref, dg2_ref):
        @pl.when(pl.program_id(0) == 0)
        def _():
            dg2_ref[...] = jnp.zeros_like(dg2_ref)

        dx2b = dx2b_ref[...]
        dh2 = jnp.zeros((tm, D_MODEL), F32)
        for j in range(N_CHIPS):
            sl = slice(j * FF_SHARD, (j + 1) * FF_SHARD)
            dhid = _mm_nt(dx2b, wd_ref[j])
            da = (dhid * (2.0 * jnp.maximum(a_ref[:, sl].astype(F32), 0.0))).astype(MXU_DTYPE)
            da_ref[:, sl] = da
            dh2 = dh2 + _mm_nt(da, wu_ref[j])
        x1 = x1_ref[...]
        r2 = lax.rsqrt(jnp.mean(x1 * x1, axis=-1, keepdims=True) + EPS)
        xn = x1 * r2
        dg2_ref[...] += jnp.sum(dh2 * xn, axis=0, keepdims=True)
        dx1 = dx2_ref[...] + _rms_bwd(dh2 * g2_ref[...], xn, r2)
        dx1_ref[...] = dx1
        dx1b_ref[...] = dx1.astype(dx1b_ref.dtype)

    row = lambda w: pl.BlockSpec((tm, w), lambda i: (i, 0))
    return pl.pallas_call(
        body, name="bwd_mlp", grid=(t_tok // tm,),
        in_specs=[row(D_MODEL), row(D_MODEL), row(D_FF), row(D_MODEL),
                  _resident((N_CHIPS, D_MODEL, FF_SHARD)), _resident((N_CHIPS, FF_SHARD, D_MODEL)), _const((1, D_MODEL))],
        out_specs=(row(D_FF), row(D_MODEL), row(D_MODEL), _const((1, D_MODEL))),
        out_shape=(jax.ShapeDtypeStruct((t_tok, D_FF), MXU_DTYPE), jax.ShapeDtypeStruct((t_tok, D_MODEL), F32),
                   jax.ShapeDtypeStruct((t_tok, D_MODEL), MXU_DTYPE), jax.ShapeDtypeStruct((1, D_MODEL), F32)),
        compiler_params=_cp(dimension_semantics=("arbitrary",)),
    )(dx2, dx2b, a, x1, w_up4, w_down4, g2)


def _bwd_mlp_wgrads(h2, da, a, dx2b, tk):
    t_tok = h2.shape[0]

    def body(h2_ref, da_ref, a_ref, dx2b_ref, gup_ref, gdn_ref):
        @pl.when(pl.program_id(1) == 0)
        def _():
            gup_ref[...] = jnp.zeros_like(gup_ref)
            gdn_ref[...] = jnp.zeros_like(gdn_ref)

        gup_ref[0] += _mm_tn(h2_ref[...], da_ref[...])
        hid = jnp.square(jnp.maximum(a_ref[...].astype(F32), 0.0))
        gdn_ref[0] += _mm_tn(hid, dx2b_ref[...])

    tok = pl.BlockSpec((tk, D_MODEL), lambda j, t: (t, 0))
    ffb = pl.BlockSpec((tk, FF_SHARD), lambda j, t: (t, j))
    wblk = pl.BlockSpec((1, D_MODEL, D_MODEL), lambda j, t: (j, 0, 0))
    return pl.pallas_call(
        body, name="bwd_mlp_wgrads", grid=(N_CHIPS, t_tok // tk),
        in_specs=[tok, ffb, ffb, tok], out_specs=(wblk, wblk),
        out_shape=(jax.ShapeDtypeStruct((N_CHIPS, D_MODEL, FF_SHARD), F32), jax.ShapeDtypeStruct((N_CHIPS, FF_SHARD, D_MODEL), F32)),
        compiler_params=_cp(dimension_semantics=("parallel", "arbitrary")),
    )(h2, da, a, dx2b)


def _head_selector():
    ch = lax.broadcasted_iota(jnp.int32, (ATTN_W, LANES), 0)
    col = lax.broadcasted_iota(jnp.int32, (ATTN_W, LANES), 1)
    return (ch // HEAD_DIM == col).astype(MXU_DTYPE)


def _bwd_outproj(dx1b, attn, pool, w_out, head_sel, tm, after=()):
    t_tok = dx1b.shape[0]
    tokens, token_specs = _after(after)

    def body(dx_ref, attn_ref, pool_ref, wo_ref, sel_ref, *rest):
        dattn_ref, dpool_ref, delta_ref, gwo_ref = rest[len(tokens):]

        @pl.when(pl.program_id(0) == 0)
        def _():
            gwo_ref[...] = jnp.zeros_like(gwo_ref)

        dx = dx_ref[...]
        attn = attn_ref[...]
        dattn = _mm_nt(dx, wo_ref[:ATTN_W])
        dattn_ref[...] = dattn.astype(dattn_ref.dtype)
        dpool_ref[...] = _mm_nt(dx, wo_ref[ATTN_W:])
        prod = dattn * attn.astype(F32)
        hi = prod.astype(MXU_DTYPE)
        lo = prod - hi.astype(F32)
        delta = _mm(hi, sel_ref[...]) + _mm(lo, sel_ref[...])
        delta_ref[...] = delta.T[:N_Q_HEADS]
        gwo_ref[:ATTN_W] += _mm_tn(attn, dx)
        gwo_ref[ATTN_W:] += _mm_tn(pool_ref[...], dx)

    row = lambda w: pl.BlockSpec((tm, w), lambda i: (i, 0))
    return pl.pallas_call(
        body, name="bwd_outproj", grid=(t_tok // tm,),
        in_specs=[row(D_MODEL), row(ATTN_W), row(POOL_W), _resident((D_MODEL, D_MODEL)), _const((ATTN_W, LANES))] + token_specs,
        out_specs=(row(ATTN_W), row(POOL_W), pl.BlockSpec((N_Q_HEADS, tm), lambda i: (0, i)), _const((D_MODEL, D_MODEL))),
        out_shape=(jax.ShapeDtypeStruct((t_tok, ATTN_W), MXU_DTYPE), jax.ShapeDtypeStruct((t_tok, POOL_W), F32),
                   jax.ShapeDtypeStruct((N_Q_HEADS, t_tok), F32), jax.ShapeDtypeStruct((D_MODEL, D_MODEL), F32)),
        compiler_params=_cp(dimension_semantics=("arbitrary",)),
    )(dx1b, attn, pool, w_out, head_sel, *tokens)


def _bwd_attn(q, k, v, dattn, lse, delta, sink_rows, bias, cos_t, sin_t, seq, after=()):
    t_tok = q.shape[0]
    nblk = seq // BLK
    qkv_w = ATTN_W + 2 * KV_W
    tokens, token_specs = _after(after)

    def unrope(d, cos, sin):
        return d * cos - _swap_halves(d) * sin

    def body(q_ref, k_ref, v_ref, do_ref, lse_ref, delta_ref, sink_ref, bias_ref, cos_ref, sin_ref, *rest):
        dqkv_ref, dsink_ref, dk_acc, dv_acc = rest[len(tokens):]

        @pl.when(pl.program_id(0) == 0)
        def _():
            dsink_ref[...] = jnp.zeros_like(dsink_ref)

        dk_acc[...] = jnp.zeros_like(dk_acc)
        dv_acc[...] = jnp.zeros_like(dv_acc)

        def blk(i, dsink):
            r0 = pl.multiple_of(i * BLK, BLK)
            p0 = pl.multiple_of(jnp.maximum(i - 1, 0) * BLK, BLK)
            bias = bias_ref[jnp.minimum(i, 1)]
            cos, sin = cos_ref[pl.ds(r0, BLK), :], sin_ref[pl.ds(r0, BLK), :]
            new = []
            for g in range(N_KV_HEADS):
                st, kk, qs = _scores_t(k_ref, q_ref, bias, p0, r0, g)
                lse_g, delta_g = _head_rows(lse_ref, r0, g), _head_rows(delta_ref, r0, g)
                pn = jnp.exp(st - lse_g)
                dos = _stack_heads(do_ref, r0, g)
                dst = pn * (_mm_nt(_kv_window(v_ref, p0, r0, g), dos) - delta_g)
                new.append(dsink[g] - jnp.exp(sink_ref[g] - lse_g) * delta_g)
                dqt = _mm_tn(kk, dst) * (HEAD_DIM ** -0.5)
                for t, tile in enumerate(_pairs_to_rows(dqt)):
                    c0 = (Q_PER_KV * g + 2 * t) * HEAD_DIM
                    dqkv_ref[pl.ds(r0, BLK), c0:c0 + LANES] = unrope(tile, cos, sin).astype(dqkv_ref.dtype)
                dkk = _mm(dst, qs)
                dvv = _mm(pn, dos)
                dk_acc[g, pl.ds(p0, BLK), :] += dkk[:BLK]
                dk_acc[g, pl.ds(r0, BLK), :] += dkk[BLK:]
                dv_acc[g, pl.ds(p0, BLK), :] += dvv[:BLK]
                dv_acc[g, pl.ds(r0, BLK), :] += dvv[BLK:]
            return tuple(new)

        zero = jnp.zeros((1, Q_PER_KV * BLK), F32)
        dsink = lax.fori_loop(0, nblk, blk, (zero,) * N_KV_HEADS)
        lane = lax.broadcasted_iota(jnp.int32, dsink_ref.shape, 1)
        row = lax.broadcasted_iota(jnp.int32, dsink_ref.shape, 0)
        tile = jnp.zeros(dsink_ref.shape, F32)
        for g in range(N_KV_HEADS):
            for h in range(Q_PER_KV):
                tot = jnp.sum(dsink[g][:, h * BLK:(h + 1) * BLK])
                tile = tile + jnp.where((row == 0) & (lane == Q_PER_KV * g + h), tot, 0.0)
        dsink_ref[...] += tile
        dk = jnp.concatenate([dk_acc[g] for g in range(N_KV_HEADS)], axis=1)
        dqkv_ref[:, ATTN_W:ATTN_W + KV_W] = unrope(dk, cos_ref[...], sin_ref[...]).astype(dqkv_ref.dtype)
        dqkv_ref[:, ATTN_W + KV_W:] = jnp.concatenate([dv_acc[g] for g in range(N_KV_HEADS)], axis=1).astype(dqkv_ref.dtype)

    row = lambda w: pl.BlockSpec((seq, w), lambda i: (i, 0))
    kv = pl.BlockSpec((N_KV_HEADS, seq, HEAD_DIM), lambda i: (0, i, 0))
    per_head = pl.BlockSpec((N_Q_HEADS, seq), lambda i: (0, i))
    return pl.pallas_call(
        body, name="bwd_attn", grid=(t_tok // seq,),
        in_specs=[row(ATTN_W), kv, kv, row(ATTN_W), per_head, per_head, _const((N_KV_HEADS, 1, Q_PER_KV * BLK)),
                  _const((2, 2 * BLK, BLK)), _resident((seq, LANES)), _resident((seq, LANES))] + token_specs,
        out_specs=(row(qkv_w), _const((8, LANES))),
        out_shape=(jax.ShapeDtypeStruct((t_tok, qkv_w), MXU_DTYPE), jax.ShapeDtypeStruct((8, LANES), F32)),
        scratch_shapes=[pltpu.VMEM((N_KV_HEADS, seq, HEAD_DIM), F32), pltpu.VMEM((N_KV_HEADS, seq, HEAD_DIM), F32)],
        compiler_params=_cp(dimension_semantics=("arbitrary",)),
    )(q, k, v, dattn, lse, delta, sink_rows, bias, cos_t, sin_t, *tokens)


def _bwd_pool(u, dpool, w_pool, pool_scale, seq):
    t_tok = u.shape[0]

    def body(u_ref, dp_ref, wp_ref, sc_ref, du_ref, dwp_ref, dsc_ref):
        @pl.when(pl.program_id(0) == 0)
        def _():
            dwp_ref[...] = jnp.zeros_like(dwp_ref)
            dsc_ref[...] = jnp.zeros_like(dsc_ref)

        for gi, w in enumerate(POOL_WINDOWS):
            sl = slice(gi * POOL_G, (gi + 1) * POOL_G)
            ug = u_ref[:, sl]
            inv = _inv_count(seq, w)
            d = (_window_sum(ug, w, seq, False) * inv - ug).astype(MXU_DTYPE)
            y = _mm(d, wp_ref[gi])
            dpool = dp_ref[:, sl]
            dsc_ref[:, sl] += jnp.sum(y * dpool, axis=0, keepdims=True)
            dy = (dpool * sc_ref[:, sl]).astype(MXU_DTYPE)
            dwp_ref[gi] += _mm_tn(d, dy)
            dd = _mm_nt(dy, wp_ref[gi])
            du_ref[:, sl] = (_window_sum(dd * inv, w, seq, True) - dd).astype(du_ref.dtype)

    row = pl.BlockSpec((seq, POOL_W), lambda i: (i, 0))
    return pl.pallas_call(
        body, name="bwd_pool", grid=(t_tok // seq,),
        in_specs=[row, row, _const((N_POOL, POOL_G, POOL_G)), _const((1, POOL_W))],
        out_specs=(row, _const((N_POOL, POOL_G, POOL_G)), _const((1, POOL_W))),
        out_shape=(jax.ShapeDtypeStruct((t_tok, POOL_W), MXU_DTYPE), jax.ShapeDtypeStruct((N_POOL, POOL_G, POOL_G), F32),
                   jax.ShapeDtypeStruct((1, POOL_W), F32)),
        compiler_params=_cp(dimension_semantics=("arbitrary",)),
    )(u, dpool, w_pool, pool_scale)


def _bwd_inproj(dqkv, du, x, dx1, w_in, g1, tm):
    t_tok = x.shape[0]
    nsteps = t_tok // tm

    qkv_w = ATTN_W + 2 * KV_W

    def body(dqkv_ref, du_ref, x_ref, dx1_ref, w_ref, g_ref, gx_ref, gw_ref, dg_ref, acc_ref):
        @pl.when(pl.program_id(0) == 0)
        def _():
            acc_ref[...] = jnp.zeros_like(acc_ref)
            dg_ref[...] = jnp.zeros_like(dg_ref)

        dqkv, du = dqkv_ref[...], du_ref[...]
        xv = x_ref[...]
        r = lax.rsqrt(jnp.mean(xv * xv, axis=-1, keepdims=True) + EPS)
        xn = xv * r
        g = g_ref[...]
        dh = _mm_nt(dqkv, w_ref[:, :qkv_w]) + _mm_nt(du, w_ref[:, qkv_w:])
        dg_ref[...] += jnp.sum(dh * xn, axis=0, keepdims=True)
        gx_ref[...] = dx1_ref[...] + _rms_bwd(dh * g, xn, r)
        h = (xn * g).astype(MXU_DTYPE)
        acc_ref[:, :qkv_w] += _mm_tn(h, dqkv)
        acc_ref[:, qkv_w:] += _mm_tn(h, du)

        @pl.when(pl.program_id(0) == nsteps - 1)
        def _():
            acc = acc_ref[...]
            for j in range(N_CHIPS):
                gw_ref[j] = acc[:, j * IN_SHARD:(j + 1) * IN_SHARD]

    row = lambda w: pl.BlockSpec((tm, w), lambda i: (i, 0))
    return pl.pallas_call(
        body, name="bwd_inproj", grid=(nsteps,),
        in_specs=[row(qkv_w), row(POOL_W), row(D_MODEL), row(D_MODEL), _resident((D_MODEL, IN_W)), _const((1, D_MODEL))],
        out_specs=(row(D_MODEL), _const((N_CHIPS, D_MODEL, IN_SHARD)), _const((1, D_MODEL))),
        out_shape=(jax.ShapeDtypeStruct((t_tok, D_MODEL), F32), jax.ShapeDtypeStruct((N_CHIPS, D_MODEL, IN_SHARD), F32),
                   jax.ShapeDtypeStruct((1, D_MODEL), F32)),
        scratch_shapes=[pltpu.VMEM((D_MODEL, IN_W), F32)],
        compiler_params=_cp(dimension_semantics=("arbitrary",)),
    )(dqkv, du, x, dx1, w_in, g1)


class _NoComm:
    def start(self):
        return ()

    def attn_done(self, attn):
        return ()

    def rest_of_weights(self, pool):
        raise NotImplementedError

    def mlp_grads_ready(self, gw_up4, gw_down4):
        return ()

    def outproj_done(self, gw_out):
        return ()


def _local_step(x, target, w_in, comm, g1, sinks, w_pool, pool_scale, g2, gf, seq):
    tm = min(512, seq)
    tm_mlp = min(256, seq)
    cos_t, sin_t = _rope_tables(seq)
    sink_rows, bias = _sink_rows(sinks), _attn_bias()
    q, k, v, u = _fwd_inproj(x, g1, w_in, cos_t, sin_t, tm, comm.start())
    attn, lse = _fwd_attn(q, k, v, sink_rows, bias, seq)
    pool = _fwd_pool(u, w_pool, pool_scale, seq, comm.attn_done(attn))
    w_out, w_up4, w_down4 = comm.rest_of_weights(pool)
    x1, h2, a, dx2, dx2b, loss, dgf = _fwd_mlp_loss(x, attn, pool, target, w_out, w_up4, w_down4, g2, gf, tm_mlp)
    da, dx1, dx1b, dg2 = _bwd_mlp(dx2, dx2b, a, x1, w_up4, w_down4, g2, tm_mlp)
    gw_up4, gw_down4 = _bwd_mlp_wgrads(h2, da, a, dx2b, min(1024, x.shape[0]))
    dattn, dpool, delta, gw_out = _bwd_outproj(dx1b, attn, pool, w_out, _head_selector(), tm,
                                               comm.mlp_grads_ready(gw_up4, gw_down4))
    dqkv, dsinks = _bwd_attn(q, k, v, dattn, lse, delta, sink_rows, bias, cos_t, sin_t, seq, comm.outproj_done(gw_out))
    du, dwp, dsc = _bwd_pool(u, dpool, w_pool, pool_scale, seq)
    gx, gw_in4, dg1 = _bwd_inproj(dqkv, du, x, dx1, w_in, g1, tm)
    big = (gw_in4, gw_out.reshape(N_CHIPS, OUT_SHARD, D_MODEL), gw_up4, gw_down4)
    small = (dg1, dsinks, dwp.reshape(N_POOL * POOL_G, POOL_G), dsc, dg2, dgf, loss)
    return gx, big, small


def _sibling_exchange(arrs, pick_half, name):
    n = len(arrs)

    def out_shape(a):
        if pick_half:
            return (a.shape[0], a.shape[1] // 2) + a.shape[2:]
        return a.shape

    def body(*refs):
        srcs, dsts = refs[:n], refs[n:2 * n]
        send, recv = refs[2 * n:]
        x, y, c = lax.axis_index("x"), lax.axis_index("y"), lax.axis_index("c")
        cps = []
        for i in range(n):
            src = srcs[i]
            if pick_half:
                h = src.shape[1] // 2
                src = src.at[:, pl.ds((1 - c) * h, h)]
            cp = pltpu.make_async_remote_copy(src_ref=src, dst_ref=dsts[i], send_sem=send.at[i], recv_sem=recv.at[i],
                                              device_id=(x, y, 1 - c), device_id_type=MESH)
            cp.start()
            cps.append(cp)
        for cp in cps:
            cp.wait()

    hbm = pl.BlockSpec(memory_space=pltpu.HBM)
    return pl.pallas_call(
        body, name=name, out_shape=tuple(jax.ShapeDtypeStruct(out_shape(a), a.dtype) for a in arrs),
        in_specs=[hbm] * n, out_specs=tuple([hbm] * n),
        scratch_shapes=[pltpu.SemaphoreType.DMA((n,)), pltpu.SemaphoreType.DMA((n,))],
        compiler_params=_cp(),
    )(*arrs)


def _sibling_copies(srcs, lands, send, recv):
    x, y, c = lax.axis_index("x"), lax.axis_index("y"), lax.axis_index("c")
    cps = []
    for i in range(len(srcs)):
        h = srcs[i].shape[1] // 2
        cps.append(pltpu.make_async_remote_copy(
            src_ref=srcs[i].at[:, pl.ds((1 - c) * h, h)], dst_ref=lands[i], send_sem=send.at[i], recv_sem=recv.at[i],
            device_id=(x, y, 1 - c), device_id_type=MESH))
    return cps


def _sibling_start(arrs, name):
    n = len(arrs)

    def body(*refs):
        send, recv, token = refs[3 * n:3 * n + 3]
        for cp in _sibling_copies(refs[:n], refs[2 * n:3 * n], send, recv):
            cp.start()
        token[...] = jnp.zeros_like(token)

    half = lambda a: (a.shape[0], a.shape[1] // 2) + a.shape[2:]
    res = pl.pallas_call(
        body, name=name + "_start",
        out_shape=tuple(pltpu.HBM(a.shape, a.dtype) for a in arrs) + tuple(pltpu.HBM(half(a), a.dtype) for a in arrs) + (
            pltpu.SemaphoreType.DMA((n,)), pltpu.SemaphoreType.DMA((n,)), TOKEN),
        in_specs=[_HBM] * n, out_specs=tuple([_HBM] * (2 * n)) + (_SEM, _SEM, pl.BlockSpec(memory_space=pltpu.VMEM)),
        input_output_aliases={i: i for i in range(n)},
        compiler_params=_cp(has_side_effects=_EFFECT),
    )(*[pltpu.with_memory_space_constraint(a, pltpu.HBM) for a in arrs])
    return res[:n], res[n:2 * n], res[2 * n], res[2 * n + 1], res[2 * n + 2]


def _sibling_wait(arrs, lands, send, recv, after, name):
    n = len(arrs)

    def body(*refs):
        for cp in _sibling_copies(refs[:n], refs[n:2 * n], refs[2 * n], refs[2 * n + 1]):
            cp.wait_send()
            cp.wait_recv()

    res = pl.pallas_call(
        body, name=name + "_wait", out_shape=tuple(pltpu.HBM(a.shape, a.dtype) for a in list(arrs) + list(lands)),
        in_specs=[_HBM] * (2 * n) + [_SEM, _SEM, _ANY], out_specs=tuple([_HBM] * (2 * n)),
        input_output_aliases={i: i for i in range(2 * n)},
        compiler_params=_cp(has_side_effects=_EFFECT),
    )(*arrs, *lands, send, recv, after)
    return res[:n], res[n:]


def _row_block(rows):
    for cand in (256, 128, 64, 32, 16, 8):
        if rows % cand == 0:
            return cand
    raise ValueError(rows)


def _chip_partial(g4, r4, c_arr, name):
    _, rows, cols = r4.shape
    rb = _row_block(rows)
    nb = rows // rb

    def body(c_ref, g_ref, r_ref, o_ref):
        o_ref[...] = (g_ref[...] + r_ref[...]).astype(o_ref.dtype)

    return pl.pallas_call(
        body, name=name,
        grid_spec=pltpu.PrefetchScalarGridSpec(
            num_scalar_prefetch=1, grid=(N_CHIPS, nb),
            in_specs=[pl.BlockSpec((1, rb, cols), lambda s, i, c: (s, c[0] * nb + i, 0)),
                      pl.BlockSpec((1, rb, cols), lambda s, i, c: (s, i, 0))],
            out_specs=pl.BlockSpec((1, rb, cols), lambda s, i, c: (s, i, 0))),
        out_shape=jax.ShapeDtypeStruct(r4.shape, BF16),
        compiler_params=_cp(dimension_semantics=("parallel", "parallel")),
    )(c_arr, g4, r4)


def _send_partials(parts):
    n = len(parts)

    def body(*refs):
        srcs, dsts = refs[:n], refs[n:2 * n]
        send, recv = refs[2 * n:]
        x, y, c = lax.axis_index("x"), lax.axis_index("y"), lax.axis_index("c")
        chips = [(1 - x, y), (x, 1 - y), (1 - x, 1 - y)]
        cps = []
        for i in range(n):
            for m, chip in enumerate(chips):
                cp = pltpu.make_async_remote_copy(
                    src_ref=srcs[i].at[2 * chip[0] + chip[1]], dst_ref=dsts[i].at[m],
                    send_sem=send.at[3 * i + m], recv_sem=recv.at[3 * i + m],
                    device_id=(chip[0], chip[1], c), device_id_type=MESH)
                cp.start()
                cps.append(cp)
        for cp in cps:
            cp.wait()

    hbm = pl.BlockSpec(memory_space=pltpu.HBM)
    return pl.pallas_call(
        body, name="send_partials",
        out_shape=tuple(jax.ShapeDtypeStruct((3,) + p.shape[1:], p.dtype) for p in parts),
        in_specs=[hbm] * n, out_specs=tuple([hbm] * n),
        scratch_shapes=[pltpu.SemaphoreType.DMA((3 * n,)), pltpu.SemaphoreType.DMA((3 * n,))],
        compiler_params=_cp(),
    )(*parts)


def _send_start(parts):
    n = len(parts)

    def body(*refs):
        srcs, lands = refs[:n], refs[2 * n:3 * n]
        send, recv, token = refs[3 * n:3 * n + 3]
        x, y, c = lax.axis_index("x"), lax.axis_index("y"), lax.axis_index("c")
        for i in range(n):
            for m, chip in enumerate(_other_chips(x, y)):
                pltpu.make_async_remote_copy(
                    src_ref=srcs[i].at[2 * chip[0] + chip[1]], dst_ref=lands[i].at[m],
                    send_sem=send.at[3 * i + m], recv_sem=recv.at[3 * i + m],
                    device_id=(chip[0], chip[1], c), device_id_type=MESH).start()
        token[...] = jnp.zeros_like(token)

    res = pl.pallas_call(
        body, name="send_start",
        out_shape=tuple(pltpu.HBM(p.shape, p.dtype) for p in parts) + tuple(pltpu.HBM((3,) + p.shape[1:], p.dtype) for p in parts) + (
            pltpu.SemaphoreType.DMA((3 * n,)), pltpu.SemaphoreType.DMA((3 * n,)), TOKEN),
        in_specs=[_HBM] * n, out_specs=tuple([_HBM] * (2 * n)) + (_SEM, _SEM, pl.BlockSpec(memory_space=pltpu.VMEM)),
        input_output_aliases={i: i for i in range(n)},
        compiler_params=_cp(has_side_effects=_EFFECT),
    )(*[pltpu.with_memory_space_constraint(p, pltpu.HBM) for p in parts])
    return res[:n], res[n:2 * n], res[2 * n], res[2 * n + 1], res[2 * n + 2]


def _send_wait(parts, lands, send, recv, after):
    n = len(parts)

    def body(*refs):
        srcs, ins = refs[:n], refs[n:2 * n]
        send_ref, recv_ref = refs[2 * n], refs[2 * n + 1]
        x, y, c = lax.axis_index("x"), lax.axis_index("y"), lax.axis_index("c")
        for i in range(n):
            for m, chip in enumerate(_other_chips(x, y)):
                cp = pltpu.make_async_remote_copy(
                    src_ref=srcs[i].at[2 * chip[0] + chip[1]], dst_ref=ins[i].at[m],
                    send_sem=send_ref.at[3 * i + m], recv_sem=recv_ref.at[3 * i + m],
                    device_id=(chip[0], chip[1], c), device_id_type=MESH)
                cp.wait_send()
                cp.wait_recv()

    res = pl.pallas_call(
        body, name="send_wait", out_shape=tuple(pltpu.HBM(a.shape, a.dtype) for a in list(parts) + list(lands)),
        in_specs=[_HBM] * (2 * n) + [_SEM, _SEM, _ANY], out_specs=tuple([_HBM] * (2 * n)),
        input_output_aliases={i: i for i in range(2 * n)},
        compiler_params=_cp(has_side_effects=_EFFECT),
    )(*parts, *lands, send, recv, after)
    return res[n:]


def _final_half(g4, r4, got3, jc_arr, name):
    _, rows, cols = r4.shape
    rb = _row_block(rows)
    nb = rows // rb

    def body(jc_ref, g_ref, r_ref, p_ref, o_ref):
        own = g_ref[0] + r_ref[0]
        o_ref[...] = ((own + p_ref[0].astype(F32)) + p_ref[1].astype(F32)) + p_ref[2].astype(F32)

    return pl.pallas_call(
        body, name=name,
        grid_spec=pltpu.PrefetchScalarGridSpec(
            num_scalar_prefetch=1, grid=(nb,),
            in_specs=[pl.BlockSpec((1, rb, cols), lambda i, jc: (jc[0], jc[1] * nb + i, 0)),
                      pl.BlockSpec((1, rb, cols), lambda i, jc: (jc[0], i, 0)),
                      pl.BlockSpec((3, rb, cols), lambda i, jc: (0, i, 0))],
            out_specs=pl.BlockSpec((rb, cols), lambda i, jc: (i, 0))),
        out_shape=jax.ShapeDtypeStruct((rows, cols), F32),
        compiler_params=_cp(dimension_semantics=("parallel",)),
    )(jc_arr, g4, r4, got3)


def _adamw_math(w, g, m, v):
    m2 = ADAM_B1 * m + (1.0 - ADAM_B1) * g
    v2 = ADAM_B2 * v + (1.0 - ADAM_B2) * (g * g)
    m_hat = m2 / (1.0 - ADAM_B1 ** ADAM_STEP)
    v_hat = v2 / (1.0 - ADAM_B2 ** ADAM_STEP)
    delta = -ADAM_LR * (m_hat / (jnp.sqrt(v_hat) + ADAM_EPS) + ADAM_WD * w)
    return delta, m2, v2


def _adamw_shard(mine, other, w, m, v, c_arr, name):
    rows, cols = w.shape
    half = rows // 2
    rb = _row_block(half)
    nb = half // rb

    def body(c_ref, a_ref, b_ref, w_ref, m_ref, v_ref, g_ref, d_ref, m2_ref, v2_ref):
        g = jnp.where(pl.program_id(0) == c_ref[0], a_ref[...], b_ref[...])
        delta, m2, v2 = _adamw_math(w_ref[...], g, m_ref[...], v_ref[...])
        g_ref[...] = g
        d_ref[...] = delta
        m2_ref[...] = m2
        v2_ref[...] = v2

    hb = pl.BlockSpec((rb, cols), lambda h, i, c: (i, 0))
    fb = pl.BlockSpec((rb, cols), lambda h, i, c: (h * nb + i, 0))
    shp = jax.ShapeDtypeStruct((rows, cols), F32)
    return pl.pallas_call(
        body, name=name,
        grid_spec=pltpu.PrefetchScalarGridSpec(num_scalar_prefetch=1, grid=(2, nb), in_specs=[hb, hb, fb, fb, fb],
                                               out_specs=(fb, fb, fb, fb)),
        out_shape=(shp, shp, shp, shp),
        compiler_params=_cp(dimension_semantics=("parallel", "parallel")),
    )(c_arr, mine, other, w, m, v)


def _small_allreduce(parts):
    n = len(parts)

    def body(*refs):
        p_refs, accs = refs[:n], refs[n:2 * n]
        bufs = refs[2 * n:3 * n]
        send, recv = refs[3 * n:]
        x, y, c = lax.axis_index("x"), lax.axis_index("y"), lax.axis_index("c")
        partners = [(x, y, 1 - c), (1 - x, y, c), (x, 1 - y, c)]
        for i in range(n):
            accs[i][...] = p_refs[i][...]
        for s, partner in enumerate(partners):
            cps = []
            for i in range(n):
                cp = pltpu.make_async_remote_copy(src_ref=accs[i], dst_ref=bufs[i].at[s], send_sem=send.at[3 * i + s],
                                                  recv_sem=recv.at[3 * i + s], device_id=partner, device_id_type=MESH)
                cp.start()
                cps.append(cp)
            for cp in cps:
                cp.wait()
            for i in range(n):
                accs[i][...] = accs[i][...] + bufs[i][s]

    vmem = pl.BlockSpec(memory_space=pltpu.VMEM)
    return pl.pallas_call(
        body, name="small_allreduce", out_shape=tuple(jax.ShapeDtypeStruct(p.shape, F32) for p in parts),
        in_specs=[vmem] * n, out_specs=tuple([vmem] * n),
        scratch_shapes=[pltpu.VMEM((3,) + p.shape, F32) for p in parts] + [
            pltpu.SemaphoreType.DMA((3 * n,)), pltpu.SemaphoreType.DMA((3 * n,))],
        compiler_params=_cp(),
    )(*parts)


def _small_adamw(reduced, params):
    n = len(reduced)
    n_w = len(params)

    def body(*refs):
        r_refs = refs[:n]
        wmv = refs[n:n + 3 * n_w]
        outs = refs[n + 3 * n_w:]
        outs[0][...] = r_refs[n - 1][0:1, 0:1]
        grads = [r_refs[0][...], r_refs[1][0:1, 0:N_Q_HEADS]] + [r_refs[i][...] for i in range(2, n_w)]
        for i in range(n_w):
            w_ref, m_ref, v_ref = wmv[3 * i:3 * i + 3]
            g_ref, d_ref, m2_ref, v2_ref = outs[1 + 4 * i:5 + 4 * i]
            delta, m2, v2 = _adamw_math(w_ref[...], grads[i], m_ref[...], v_ref[...])
            g_ref[...] = grads[i]
            d_ref[...] = delta
            m2_ref[...] = m2
            v2_ref[...] = v2

    flat = [a for p in params for a in p]
    vmem = pl.BlockSpec(memory_space=pltpu.VMEM)
    out_shape = [jax.ShapeDtypeStruct((1, 1), F32)]
    for p in params:
        out_shape += [jax.ShapeDtypeStruct(p[0].shape, F32)] * 4
    res = pl.pallas_call(
        body, name="small_adamw", out_shape=tuple(out_shape),
        in_specs=[vmem] * (n + len(flat)), out_specs=tuple([vmem] * len(out_shape)),
        compiler_params=_cp(),
    )(*reduced, *flat)
    return res[0], [res[1 + 4 * i:5 + 4 * i] for i in range(n_w)]


def kernel(x, attn_norm_g, w_in, attn_sinks, w_pool, pool_scale, w_out, mlp_norm_g, w_up, w_down, final_norm_g, loss_target, m_attn_norm_g, m_w_in, m_attn_sinks, m_w_pool, m_pool_scale, m_w_out, m_mlp_norm_g, m_w_up, m_w_down, m_final_norm_g, v_attn_norm_g, v_w_in, v_attn_sinks, v_w_pool, v_pool_scale, v_w_out, v_mlp_norm_g, v_w_up, v_w_down, v_final_norm_g):
    nseq, seq, d = x.shape
    c_idx = lax.axis_index("c").astype(jnp.int32)
    j_idx = (2 * lax.axis_index("x") + lax.axis_index("y")).astype(jnp.int32)
    c_arr = jnp.reshape(c_idx, (1,))
    jc_arr = jnp.stack([j_idx, c_idx])

    big_w = (w_in[0], w_out[0], w_up[0], w_down[0])
    big_m = (m_w_in[0], m_w_out[0], m_w_up[0], m_w_down[0])
    big_v = (v_w_in[0], v_w_out[0], v_w_up[0], v_w_down[0])
    (w_in4,) = _gather_weights(big_w[0])
    w_in_full = _concat_w_in(w_in4)
    class Comm(_NoComm):
        def start(self):
            self.lands, self.send, self.recv, token = _gather_start(big_w[1:], w_in4)
            return (token,)

        def attn_done(self, attn):
            arrived = _gather_wait(self.lands, self.send, self.recv, attn)
            self.lands, self.send, self.recv, token = _forward_start(arrived)
            return (token,)

        def rest_of_weights(self, pool):
            w_out4, w_up4, w_down4 = _forward_wait(self.lands, self.send, self.recv, pool)
            return w_out4.reshape(D_MODEL, D_MODEL), w_up4, w_down4

        def mlp_grads_ready(self, gw_up4, gw_down4):
            self.mlp = _sibling_start((gw_up4, gw_down4), "mlp_grads_to_sibling")
            return (self.mlp[4],)

        def outproj_done(self, gw_out):
            grads, lands, send, recv, _ = self.mlp
            self.mlp_grads, self.mlp_from_sib = _sibling_wait(grads, lands, send, recv, gw_out, "mlp_grads_to_sibling")
            partials = [_chip_partial(g, r, c_arr, "chip_partial_" + nm)
                        for g, r, nm in zip(self.mlp_grads, self.mlp_from_sib, ("w_up", "w_down"))]
            self.parts, self.part_lands, self.part_send, self.part_recv, token = _send_start(partials)
            return (token,)

    comm = Comm()
    gx, big_g, small_g = _local_step(
        x.reshape(nseq * seq, d), loss_target.reshape(nseq * seq, d), w_in_full, comm,
        attn_norm_g, attn_sinks.reshape(N_Q_HEADS), w_pool[0], pool_scale, mlp_norm_g, final_norm_g.reshape(1, d), seq)
    big_g = tuple(big_g[:2]) + tuple(comm.mlp_grads)

    names = ("w_in", "w_out", "w_up", "w_down")
    from_sibling = list(_sibling_exchange(big_g[:2], True, "grads_to_sibling")) + list(comm.mlp_from_sib)
    partials = [_chip_partial(g, r, c_arr, "chip_partial_" + nm) for g, r, nm in zip(big_g[:2], from_sibling[:2], names[:2])]
    received = list(_send_partials(partials))
    received += list(_send_wait(comm.parts, comm.part_lands, comm.part_send, comm.part_recv, received[0]))
    mine = [_final_half(g, r, p, jc_arr, "final_half_" + nm) for g, r, p, nm in zip(big_g, from_sibling, received, names)]
    other = _sibling_exchange(mine, False, "halves_to_sibling")
    big_out = [_adamw_shard(a, b, w, m, v, c_arr, "adamw_" + nm)
               for a, b, w, m, v, nm in zip(mine, other, big_w, big_m, big_v, names)]

    wp_flat = lambda a: a.reshape(N_POOL * POOL_G, POOL_G)
    small_params = [
        (attn_norm_g, m_attn_norm_g, v_attn_norm_g),
        (attn_sinks, m_attn_sinks, v_attn_sinks),
        (wp_flat(w_pool), wp_flat(m_w_pool), wp_flat(v_w_pool)),
        (pool_scale, m_pool_scale, v_pool_scale),
        (mlp_norm_g, m_mlp_norm_g, v_mlp_norm_g),
        (final_norm_g.reshape(1, d), m_final_norm_g.reshape(1, d), v_final_norm_g.reshape(1, d)),
    ]
    loss, small_out = _small_adamw(_small_allreduce(small_g), small_params)

    def shaped(i, arr):
        return {1: w_in, 5: w_out, 7: w_up, 8: w_down, 0: attn_norm_g, 2: attn_sinks, 3: w_pool, 4: pool_scale,
                6: mlp_norm_g, 9: final_norm_g}[i].shape

    order = [small_out[0], big_out[0], small_out[1], small_out[2], small_out[3], big_out[1], small_out[4], big_out[2],
             big_out[3], small_out[5]]
    outs = [loss.reshape(()), gx.reshape(nseq, seq, d)]
    for kind in range(4):
        outs += [order[i][kind].reshape(shaped(i, None)) for i in range(10)]
    return tuple(outs)
```

```python
import jax
import jax.numpy as jnp
from jax import lax
from jax.experimental import pallas as pl
from jax.experimental.pallas import tpu as pltpu

F32 = jnp.float32
BF16 = jnp.bfloat16
MXU_DTYPE = jnp.bfloat16

D_MODEL = 1024
HEAD_DIM = 64
N_Q_HEADS = 8
N_KV_HEADS = 2
Q_PER_KV = N_Q_HEADS // N_KV_HEADS
ATTN_W = N_Q_HEADS * HEAD_DIM
KV_W = N_KV_HEADS * HEAD_DIM
BLK = 128
POOL_WINDOWS = (2, 4, 8, 16)
N_POOL = len(POOL_WINDOWS)
POOL_W = D_MODEL - ATTN_W
POOL_G = POOL_W // N_POOL
IN_W = ATTN_W + 2 * KV_W + POOL_W
D_FF = 4 * D_MODEL
EPS = 1e-6
ROPE_THETA = 10000.0
N_CHIPS = 4
IN_SHARD = IN_W // N_CHIPS
OUT_SHARD = D_MODEL // N_CHIPS
FF_SHARD = D_FF // N_CHIPS
LANES = 128

ADAM_LR = 0.001
ADAM_B1 = 0.9
ADAM_B2 = 0.999
ADAM_EPS = 1e-08
ADAM_WD = 0.01
ADAM_STEP = 10

VMEM_LIMIT = 56 * 1024 * 1024
MESH = pl.DeviceIdType.MESH


def _cp(**kw):
    return pltpu.CompilerParams(vmem_limit_bytes=VMEM_LIMIT, **kw)


def _mm(a, b):
    return jnp.dot(a.astype(MXU_DTYPE), b.astype(MXU_DTYPE), preferred_element_type=F32)


def _mm_nt(a, b):
    return lax.dot_general(a.astype(MXU_DTYPE), b.astype(MXU_DTYPE), (((1,), (1,)), ((), ())),
                           preferred_element_type=F32)


def _mm_tn(a, b):
    return lax.dot_general(a.astype(MXU_DTYPE), b.astype(MXU_DTYPE), (((0,), (0,)), ((), ())),
                           preferred_element_type=F32)


def _resident(shape):
    nd = len(shape)
    return pl.BlockSpec(shape, lambda *_: (0,) * nd, pipeline_mode=pl.Buffered(1))


def _const(shape):
    nd = len(shape)
    return pl.BlockSpec(shape, lambda *_: (0,) * nd)


def _rope_tables(seq):
    half = HEAD_DIM // 2
    inv_freq = ROPE_THETA ** (-jnp.arange(half, dtype=F32) / half)
    ang = jnp.arange(seq, dtype=F32)[:, None] * inv_freq[None, :]
    cos, sin = jnp.cos(ang), jnp.sin(ang)
    cos_t = jnp.concatenate([cos, cos, cos, cos], axis=1)
    sin_t = jnp.concatenate([-sin, sin, -sin, sin], axis=1)
    return cos_t, sin_t


def _swap_halves(xc):
    lane = lax.broadcasted_iota(jnp.int32, xc.shape, 1)
    return jnp.where((lane & 63) < 32, pltpu.roll(xc, 96, 1), pltpu.roll(xc, 32, 1))


def _gather_weights(*shards):
    nw = len(shards)

    def body(*refs):
        srcs = refs[0:nw]
        outs = refs[nw:2 * nw]
        stages = refs[2 * nw:3 * nw]
        send1, recv1, send2, recv2, lsem = refs[3 * nw:]
        x, y, c = lax.axis_index("x"), lax.axis_index("y"), lax.axis_index("c")
        j = 2 * x + y
        chips = [(1 - x, y), (x, 1 - y), (1 - x, 1 - y)]
        for k in range(nw):
            stages[k][...] = srcs[k][...].astype(BF16)
        local = [pltpu.make_async_copy(stages[k], outs[k].at[j], lsem.at[k]) for k in range(nw)]
        for cp in local:
            cp.start()

        def piece(k, chip, half):
            rows = shards[k].shape[0] // 2
            return outs[k].at[2 * chip[0] + chip[1], pl.ds(half * rows, rows)]

        first, passed = [], []
        for k in range(nw):
            rows = shards[k].shape[0] // 2
            for n, chip in enumerate(chips):
                cp = pltpu.make_async_remote_copy(
                    src_ref=stages[k].at[pl.ds(c * rows, rows)], dst_ref=piece(k, (x, y), c),
                    send_sem=send1.at[k * 3 + n], recv_sem=recv1.at[k * 3 + n],
                    device_id=(chip[0], chip[1], c), device_id_type=MESH)
                cp.start()
                first.append(cp)
        for k in range(nw):
            for n, chip in enumerate(chips):
                got = piece(k, chip, c)
                pltpu.make_async_remote_copy(
                    src_ref=got, dst_ref=got, send_sem=send1.at[k * 3 + n], recv_sem=recv1.at[k * 3 + n],
                    device_id=(chip[0], chip[1], c), device_id_type=MESH).wait_recv()
                cp = pltpu.make_async_remote_copy(
                    src_ref=got, dst_ref=got, send_sem=send2.at[k * 3 + n], recv_sem=recv2.at[k * 3 + n],
                    device_id=(x, y, 1 - c), device_id_type=MESH)
                cp.start()
                passed.append(cp)
        for k in range(nw):
            for n, chip in enumerate(chips):
                got = piece(k, chip, 1 - c)
                pltpu.make_async_remote_copy(
                    src_ref=got, dst_ref=got, send_sem=send2.at[k * 3 + n], recv_sem=recv2.at[k * 3 + n],
                    device_id=(x, y, 1 - c), device_id_type=MESH).wait_recv()
        for cp in first + passed:
            cp.wait_send()
        for cp in local:
            cp.wait()

    hbm = pl.BlockSpec(memory_space=pltpu.HBM)
    vmem = pl.BlockSpec(memory_space=pltpu.VMEM)
    return pl.pallas_call(
        body, name="gather_weights",
        out_shape=tuple(jax.ShapeDtypeStruct((N_CHIPS,) + s.shape, BF16) for s in shards),
        in_specs=[vmem] * nw, out_specs=tuple([hbm] * nw),
        scratch_shapes=[pltpu.VMEM(s.shape, BF16) for s in shards] + [
            pltpu.SemaphoreType.DMA((3 * nw,)), pltpu.SemaphoreType.DMA((3 * nw,)),
            pltpu.SemaphoreType.DMA((3 * nw,)), pltpu.SemaphoreType.DMA((3 * nw,)),
            pltpu.SemaphoreType.DMA((nw,))],
        compiler_params=_cp(),
    )(*shards)


_HBM = pl.BlockSpec(memory_space=pltpu.HBM)
_SEM = pl.BlockSpec(memory_space=pltpu.SEMAPHORE)
_ANY = pl.BlockSpec(memory_space=pl.ANY)
_EFFECT = pltpu.SideEffectType.DATAFLOW_SIDE_EFFECTING
TOKEN = jax.ShapeDtypeStruct((8, LANES), F32)


def _other_chips(x, y):
    return [(1 - x, y), (x, 1 - y), (1 - x, 1 - y)]


def _gather_start(shards, after):
    nw = len(shards)

    def body(*refs):
        srcs, lands = refs[:nw], refs[nw + 1:2 * nw + 1]
        send, recv, token = refs[2 * nw + 1:2 * nw + 4]
        stages = refs[2 * nw + 4:3 * nw + 4]
        lsem = refs[3 * nw + 4]
        x, y, c = lax.axis_index("x"), lax.axis_index("y"), lax.axis_index("c")
        j = 2 * x + y
        for k in range(nw):
            stages[k][...] = srcs[k][...].astype(BF16)
        local = [pltpu.make_async_copy(stages[k], lands[k].at[j], lsem.at[k]) for k in range(nw)]
        for cp in local:
            cp.start()
        for cp in local:
            cp.wait()
        for k in range(nw):
            rows = shards[k].shape[0] // 2
            mine = lands[k].at[j, pl.ds(c * rows, rows)]
            for n, chip in enumerate(_other_chips(x, y)):
                pltpu.make_async_remote_copy(
                    src_ref=mine, dst_ref=mine, send_sem=send.at[k * 3 + n], recv_sem=recv.at[k * 3 + n],
                    device_id=(chip[0], chip[1], c), device_id_type=MESH).start()
        token[...] = jnp.zeros_like(token)

    vmem = pl.BlockSpec(memory_space=pltpu.VMEM)
    res = pl.pallas_call(
        body, name="gather_start",
        out_shape=tuple(pltpu.HBM((N_CHIPS,) + s.shape, BF16) for s in shards) + (
            pltpu.SemaphoreType.DMA((3 * nw,)), pltpu.SemaphoreType.DMA((3 * nw,)), TOKEN),
        in_specs=[vmem] * nw + [_ANY], out_specs=tuple([_HBM] * nw) + (_SEM, _SEM, vmem),
        scratch_shapes=[pltpu.VMEM(s.shape, BF16) for s in shards] + [pltpu.SemaphoreType.DMA((nw,))],
        compiler_params=_cp(has_side_effects=_EFFECT),
    )(*shards, after)
    return res[:nw], res[nw], res[nw + 1], res[nw + 2]


def _gather_wait(lands, send, recv, after):
    nw = len(lands)

    def body(*refs):
        ins = refs[:nw]
        send_ref, recv_ref = refs[nw], refs[nw + 1]
        x, y, c = lax.axis_index("x"), lax.axis_index("y"), lax.axis_index("c")
        j = 2 * x + y
        for k in range(nw):
            rows = lands[k].shape[1] // 2
            mine = ins[k].at[j, pl.ds(c * rows, rows)]
            for n, chip in enumerate(_other_chips(x, y)):
                got = ins[k].at[2 * chip[0] + chip[1], pl.ds(c * rows, rows)]
                cp = pltpu.make_async_remote_copy(
                    src_ref=mine, dst_ref=got, send_sem=send_ref.at[k * 3 + n], recv_sem=recv_ref.at[k * 3 + n],
                    device_id=(chip[0], chip[1], c), device_id_type=MESH)
                cp.wait_send()
                cp.wait_recv()

    return pl.pallas_call(
        body, name="gather_wait", out_shape=tuple(pltpu.HBM(a.shape, a.dtype) for a in lands),
        in_specs=[_HBM] * nw + [_SEM, _SEM, _ANY], out_specs=tuple([_HBM] * nw),
        input_output_aliases={k: k for k in range(nw)},
        compiler_params=_cp(has_side_effects=_EFFECT),
    )(*lands, send, recv, after)


def _forward_copies(refs, nw, rows_of, send, recv):
    x, y, c = lax.axis_index("x"), lax.axis_index("y"), lax.axis_index("c")
    out = []
    for k in range(nw):
        rows = rows_of[k]
        for n, chip in enumerate(_other_chips(x, y)):
            got = refs[k].at[2 * chip[0] + chip[1], pl.ds(c * rows, rows)]
            theirs = refs[k].at[2 * chip[0] + chip[1], pl.ds((1 - c) * rows, rows)]
            out.append(pltpu.make_async_remote_copy(
                src_ref=got, dst_ref=got, send_sem=send.at[k * 3 + n], recv_sem=recv.at[k * 3 + n],
                device_id=(x, y, 1 - c), device_id_type=MESH))
            out.append(pltpu.make_async_remote_copy(
                src_ref=theirs, dst_ref=theirs, send_sem=send.at[k * 3 + n], recv_sem=recv.at[k * 3 + n],
                device_id=(x, y, 1 - c), device_id_type=MESH))
    return out[0::2], out[1::2]


def _forward_start(lands):
    nw = len(lands)
    rows_of = [a.shape[1] // 2 for a in lands]

    def body(*refs):
        send, recv, token = refs[2 * nw:2 * nw + 3]
        mine, _ = _forward_copies(refs[:nw], nw, rows_of, send, recv)
        for cp in mine:
            cp.start()
        token[...] = jnp.zeros_like(token)

    res = pl.pallas_call(
        body, name="forward_start",
        out_shape=tuple(pltpu.HBM(a.shape, a.dtype) for a in lands) + (
            pltpu.SemaphoreType.DMA((3 * nw,)), pltpu.SemaphoreType.DMA((3 * nw,)), TOKEN),
        in_specs=[_HBM] * nw, out_specs=tuple([_HBM] * nw) + (_SEM, _SEM, pl.BlockSpec(memory_space=pltpu.VMEM)),
        input_output_aliases={k: k for k in range(nw)},
        compiler_params=_cp(has_side_effects=_EFFECT),
    )(*lands)
    return res[:nw], res[nw], res[nw + 1], res[nw + 2]


def _forward_wait(lands, send, recv, after):
    nw = len(lands)
    rows_of = [a.shape[1] // 2 for a in lands]

    def body(*refs):
        mine, theirs = _forward_copies(refs[:nw], nw, rows_of, refs[nw], refs[nw + 1])
        for cp in mine:
            cp.wait_send()
        for cp in theirs:
            cp.wait_recv()

    return pl.pallas_call(
        body, name="forward_wait", out_shape=tuple(pltpu.HBM(a.shape, a.dtype) for a in lands),
        in_specs=[_HBM] * nw + [_SEM, _SEM, _ANY], out_specs=tuple([_HBM] * nw),
        input_output_aliases={k: k for k in range(nw)},
        compiler_params=_cp(has_side_effects=_EFFECT),
    )(*lands, send, recv, after)


def _concat_w_in(w_in4):
    def body(s_ref, o_ref):
        o_ref[...] = jnp.concatenate([s_ref[j] for j in range(N_CHIPS)], axis=1)

    return pl.pallas_call(body, name="concat_w_in", out_shape=jax.ShapeDtypeStruct((D_MODEL, IN_W), w_in4.dtype),
                          compiler_params=_cp())(w_in4)


def _after(tokens):
    tokens = [t for t in tokens if t is not None]
    return tokens, [_ANY] * len(tokens)


def _fwd_inproj(x, g1, w_in, cos_t, sin_t, tm, after=()):
    t_tok = x.shape[0]
    seq = cos_t.shape[0]
    per_seq = seq // tm
    tokens, token_specs = _after(after)

    def body(x_ref, g_ref, w_ref, cos_ref, sin_ref, *rest):
        q_ref, k_ref, v_ref, u_ref = rest[len(tokens):]
        xv = x_ref[...]
        r = lax.rsqrt(jnp.mean(xv * xv, axis=-1, keepdims=True) + EPS)
        h = (xv * r) * g_ref[...]
        proj = _mm(h, w_ref[...])
        cos, sin = cos_ref[...], sin_ref[...]
        for cidx in range((ATTN_W + KV_W) // LANES):
            xc = proj[:, cidx * LANES:(cidx + 1) * LANES]
            rot = xc * cos + _swap_halves(xc) * sin
            if cidx < ATTN_W // LANES:
                q_ref[:, cidx * LANES:(cidx + 1) * LANES] = (rot * (HEAD_DIM ** -0.5)).astype(q_ref.dtype)
            else:
                for g in range(N_KV_HEADS):
                    k_ref[g] = rot[:, g * HEAD_DIM:(g + 1) * HEAD_DIM].astype(k_ref.dtype)
        for g in range(N_KV_HEADS):
            c0 = ATTN_W + KV_W + g * HEAD_DIM
            v_ref[g] = proj[:, c0:c0 + HEAD_DIM].astype(v_ref.dtype)
        u_ref[...] = proj[:, ATTN_W + 2 * KV_W:]

    row = lambda w: pl.BlockSpec((tm, w), lambda i: (i, 0))
    kv = pl.BlockSpec((N_KV_HEADS, tm, HEAD_DIM), lambda i: (0, i, 0))
    tab = pl.BlockSpec((tm, LANES), lambda i: (i % per_seq, 0))
    kv_shape = jax.ShapeDtypeStruct((N_KV_HEADS, t_tok, HEAD_DIM), MXU_DTYPE)
    return pl.pallas_call(
        body, name="fwd_inproj", grid=(t_tok // tm,),
        in_specs=[row(D_MODEL), _const((1, D_MODEL)), _resident((D_MODEL, IN_W)), tab, tab] + token_specs,
        out_specs=(row(ATTN_W), kv, kv, row(POOL_W)),
        out_shape=(jax.ShapeDtypeStruct((t_tok, ATTN_W), MXU_DTYPE), kv_shape, kv_shape,
                   jax.ShapeDtypeStruct((t_tok, POOL_W), F32)),
        compiler_params=_cp(dimension_semantics=("parallel",)),
    )(x, g1, w_in, cos_t, sin_t, *tokens)


MASKED = -1e30


def _attn_bias():
    b = lax.broadcasted_iota(jnp.int32, (2 * BLK, BLK), 0)
    a = lax.broadcasted_iota(jnp.int32, (2 * BLK, BLK), 1)
    own = (b >= BLK) & (b - BLK <= a)
    prev = (b < BLK) & (b > a)
    return jnp.stack([jnp.where(own, 0.0, MASKED), jnp.where(own | prev, 0.0, MASKED)]).astype(F32)


def _sink_rows(sinks):
    return jnp.repeat(sinks.reshape(N_KV_HEADS, Q_PER_KV), BLK, axis=1).reshape(N_KV_HEADS, 1, Q_PER_KV * BLK)


def _stack_heads(ref, r0, g):
    return jnp.concatenate(
        [ref[pl.ds(r0, BLK), (Q_PER_KV * g + h) * HEAD_DIM:(Q_PER_KV * g + h + 1) * HEAD_DIM] for h in range(Q_PER_KV)],
        axis=0)


def _kv_window(ref, p0, r0, g):
    return jnp.concatenate([ref[g, pl.ds(p0, BLK), :], ref[g, pl.ds(r0, BLK), :]], axis=0)


def _scores_t(k_ref, q_ref, bias, p0, r0, g):
    kk = _kv_window(k_ref, p0, r0, g)
    qs = _stack_heads(q_ref, r0, g)
    st = _mm_nt(kk, qs) + jnp.concatenate([bias] * Q_PER_KV, axis=1)
    return st, kk, qs


def _head_rows(ref, r0, g):
    return jnp.concatenate([ref[pl.ds(Q_PER_KV * g + h, 1), pl.ds(r0, BLK)] for h in range(Q_PER_KV)], axis=1)


def _pairs_to_rows(xt):
    out = []
    for t in range(Q_PER_KV // 2):
        pair = jnp.concatenate([xt[:, (2 * t) * BLK:(2 * t + 1) * BLK], xt[:, (2 * t + 1) * BLK:(2 * t + 2) * BLK]], axis=0)
        out.append(pair.T)
    return out


def _shift_rows(x, k, seq):
    row = lax.broadcasted_iota(jnp.int32, x.shape, 0)
    if k > 0:
        return jnp.where(row >= k, pltpu.roll(x, k, 0), 0.0)
    return jnp.where(row < seq + k, pltpu.roll(x, seq + k, 0), 0.0)


def _window_sum(x, w, seq, forward):
    s, k = x, 1
    while k < w:
        s = s + _shift_rows(s, -k if forward else k, seq)
        k *= 2
    return s


def _inv_count(seq, w):
    pos = lax.broadcasted_iota(jnp.int32, (seq, 1), 0)
    return 1.0 / jnp.minimum(pos + 1, w).astype(F32)


def _fwd_attn(q, k, v, sink_rows, bias, seq):
    t_tok = q.shape[0]
    nblk = seq // BLK

    def body(q_ref, k_ref, v_ref, sink_ref, bias_ref, o_ref, lse_ref):
        def blk(i, carry):
            r0 = pl.multiple_of(i * BLK, BLK)
            p0 = pl.multiple_of(jnp.maximum(i - 1, 0) * BLK, BLK)
            bias = bias_ref[jnp.minimum(i, 1)]
            for g in range(N_KV_HEADS):
                st, _, _ = _scores_t(k_ref, q_ref, bias, p0, r0, g)
                sink = sink_ref[g]
                m = jnp.maximum(jnp.max(st, axis=0, keepdims=True), sink)
                p = jnp.exp(st - m).astype(MXU_DTYPE)
                vv = _kv_window(v_ref, p0, r0, g)
                ot = _mm_tn(jnp.concatenate([vv, jnp.ones_like(vv)], axis=1), p)
                den = ot[HEAD_DIM:HEAD_DIM + 1] + jnp.exp(sink - m)
                lse = m + jnp.log(den)
                for h in range(Q_PER_KV):
                    lse_ref[pl.ds(Q_PER_KV * g + h, 1), pl.ds(r0, BLK)] = lse[:, h * BLK:(h + 1) * BLK]
                ot = ot[:HEAD_DIM] * (1.0 / den)
                for t, tile in enumerate(_pairs_to_rows(ot)):
                    c0 = (Q_PER_KV * g + 2 * t) * HEAD_DIM
                    o_ref[pl.ds(r0, BLK), c0:c0 + LANES] = tile.astype(o_ref.dtype)
            return carry

        lax.fori_loop(0, nblk, blk, 0, unroll=4)

    row = lambda w: pl.BlockSpec((seq, w), lambda i: (i, 0))
    kv = pl.BlockSpec((N_KV_HEADS, seq, HEAD_DIM), lambda i: (0, i, 0))
    return pl.pallas_call(
        body, name="fwd_attn", grid=(t_tok // seq,),
        in_specs=[row(ATTN_W), kv, kv, _const((N_KV_HEADS, 1, Q_PER_KV * BLK)), _const((2, 2 * BLK, BLK))],
        out_specs=(row(ATTN_W), pl.BlockSpec((N_Q_HEADS, seq), lambda i: (0, i))),
        out_shape=(jax.ShapeDtypeStruct((t_tok, ATTN_W), MXU_DTYPE), jax.ShapeDtypeStruct((N_Q_HEADS, t_tok), F32)),
        compiler_params=_cp(dimension_semantics=("parallel",)),
    )(q, k, v, sink_rows, bias)


def _fwd_pool(u, w_pool, pool_scale, seq, after=()):
    t_tok = u.shape[0]
    tokens, token_specs = _after(after)

    def body(u_ref, wp_ref, sc_ref, *rest):
        o_ref = rest[-1]
        for gi, w in enumerate(POOL_WINDOWS):
            sl = slice(gi * POOL_G, (gi + 1) * POOL_G)
            ug = u_ref[:, sl]
            d = _window_sum(ug, w, seq, False) * _inv_count(seq, w) - ug
            o_ref[:, sl] = (_mm(d, wp_ref[gi]) * sc_ref[:, sl]).astype(o_ref.dtype)

    row = pl.BlockSpec((seq, POOL_W), lambda i: (i, 0))
    return pl.pallas_call(
        body, name="fwd_pool", grid=(t_tok // seq,),
        in_specs=[row, _const((N_POOL, POOL_G, POOL_G)), _const((1, POOL_W))] + token_specs, out_specs=row,
        out_shape=jax.ShapeDtypeStruct((t_tok, POOL_W), MXU_DTYPE),
        compiler_params=_cp(dimension_semantics=("parallel",)),
    )(u, w_pool, pool_scale, *tokens)


def _rms_bwd(dy_g, xn, r):
    return r * (dy_g - xn * jnp.mean(dy_g * xn, axis=-1, keepdims=True))


def _fwd_mlp_loss(x, attn, pool, target, w_out, w_up4, w_down4, g2, gf, tm):
    t_tok = x.shape[0]

    def body(x_ref, attn_ref, pool_ref, tgt_ref, wo_ref, wu_ref, wd_ref, g2_ref, gf_ref,
             x1_ref, h2_ref, a_ref, dx2_ref, dx2b_ref, loss_ref, dgf_ref):
        @pl.when(pl.program_id(0) == 0)
        def _():
            loss_ref[...] = jnp.zeros_like(loss_ref)
            dgf_ref[...] = jnp.zeros_like(dgf_ref)

        x1 = x_ref[...] + (_mm(attn_ref[...], wo_ref[:ATTN_W]) + _mm(pool_ref[...], wo_ref[ATTN_W:]))
        x1_ref[...] = x1
        r2 = lax.rsqrt(jnp.mean(x1 * x1, axis=-1, keepdims=True) + EPS)
        h2 = ((x1 * r2) * g2_ref[...]).astype(MXU_DTYPE)
        h2_ref[...] = h2
        acc = jnp.zeros((tm, D_MODEL), F32)
        for j in range(N_CHIPS):
            a = _mm(h2, wu_ref[j])
            a_ref[:, j * FF_SHARD:(j + 1) * FF_SHARD] = a.astype(a_ref.dtype)
            acc = acc + _mm(jnp.square(jnp.maximum(a, 0.0)), wd_ref[j])
        x2 = x1 + acc
        r3 = lax.rsqrt(jnp.mean(x2 * x2, axis=-1, keepdims=True) + EPS)
        xn = x2 * r3
        gf_v = gf_ref[...]
        err = xn * gf_v - tgt_ref[...]
        part = jnp.sum(err * err) * (0.5 / D_MODEL)
        first = (lax.broadcasted_iota(jnp.int32, loss_ref.shape, 0) == 0) & (lax.broadcasted_iota(jnp.int32, loss_ref.shape, 1) == 0)
        loss_ref[...] += jnp.where(first, part, 0.0)
        dy = err * (1.0 / D_MODEL)
        dgf_ref[...] += jnp.sum(dy * xn, axis=0, keepdims=True)
        dx2 = _rms_bwd(dy * gf_v, xn, r3)
        dx2_ref[...] = dx2
        dx2b_ref[...] = dx2.astype(dx2b_ref.dtype)

    row = lambda w: pl.BlockSpec((tm, w), lambda i: (i, 0))
    return pl.pallas_call(
        body, name="fwd_mlp_loss", grid=(t_tok // tm,),
        in_specs=[row(D_MODEL), row(ATTN_W), row(POOL_W), row(D_MODEL), _resident((D_MODEL, D_MODEL)),
                  _resident((N_CHIPS, D_MODEL, FF_SHARD)), _resident((N_CHIPS, FF_SHARD, D_MODEL)),
                  _const((1, D_MODEL)), _const((1, D_MODEL))],
        out_specs=(row(D_MODEL), row(D_MODEL), row(D_FF), row(D_MODEL), row(D_MODEL), _const((8, LANES)), _const((1, D_MODEL))),
        out_shape=(jax.ShapeDtypeStruct((t_tok, D_MODEL), F32), jax.ShapeDtypeStruct((t_tok, D_MODEL), MXU_DTYPE),
                   jax.ShapeDtypeStruct((t_tok, D_FF), MXU_DTYPE), jax.ShapeDtypeStruct((t_tok, D_MODEL), F32),
                   jax.ShapeDtypeStruct((t_tok, D_MODEL), MXU_DTYPE), jax.ShapeDtypeStruct((8, LANES), F32),
                   jax.ShapeDtypeStruct((1, D_MODEL), F32)),
        compiler_params=_cp(dimension_semantics=("arbitrary",)),
    )(x, attn, pool, target, w_out, w_up4, w_down4, g2, gf)


def _bwd_mlp(dx2, dx2b, a, x1, w_up4, w_down4, g2, tm):
    t_tok = dx2.shape[0]

    def body(dx2_ref, dx2b_ref, a_ref, x1_ref, wu_ref, wd_ref, g2_ref, da_ref, dx1_ref, dx1b_ref, dg2_ref):
        @pl.when(pl.program_id(0) == 0)
        def _():
            dg2_ref[...] = jnp.zeros_like(dg2_ref)

        dx2b = dx2b_ref[...]
        dh2 = jnp.zeros((tm, D_MODEL), F32)
        for j in range(N_CHIPS):
            sl = slice(j * FF_SHARD, (j + 1) * FF_SHARD)
            dhid = _mm_nt(dx2b, wd_ref[j])
            da = (dhid * (2.0 * jnp.maximum(a_ref[:, sl].astype(F32), 0.0))).astype(MXU_DTYPE)
            da_ref[:, sl] = da
            dh2 = dh2 + _mm_nt(da, wu_ref[j])
        x1 = x1_ref[...]
        r2 = lax.rsqrt(jnp.mean(x1 * x1, axis=-1, keepdims=True) + EPS)
        xn = x1 * r2
        dg2_ref[...] += jnp.sum(dh2 * xn, axis=0, keepdims=True)
        dx1 = dx2_ref[...] + _rms_bwd(dh2 * g2_ref[...], xn, r2)
        dx1_ref[...] = dx1
        dx1b_ref[...] = dx1.astype(dx1b_ref.dtype)

    row = lambda w: pl.BlockSpec((tm, w), lambda i: (i, 0))
    return pl.pallas_call(
        body, name="bwd_mlp", grid=(t_tok // tm,),
        in_specs=[row(D_MODEL), row(D_MODEL), row(D_FF), row(D_MODEL),
                  _resident((N_CHIPS, D_MODEL, FF_SHARD)), _resident((N_CHIPS, FF_SHARD, D_MODEL)), _const((1, D_MODEL))],
        out_specs=(row(D_FF), row(D_MODEL), row(D_MODEL), _const((1, D_MODEL))),
        out_shape=(jax.ShapeDtypeStruct((t_tok, D_FF), MXU_DTYPE), jax.ShapeDtypeStruct((t_tok, D_MODEL), F32),
                   jax.ShapeDtypeStruct((t_tok, D_MODEL), MXU_DTYPE), jax.ShapeDtypeStruct((1, D_MODEL), F32)),
        compiler_params=_cp(dimension_semantics=("arbitrary",)),
    )(dx2, dx2b, a, x1, w_up4, w_down4, g2)


def _bwd_mlp_wgrads(h2, da, a, dx2b, tk):
    t_tok = h2.shape[0]

    def body(h2_ref, da_ref, a_ref, dx2b_ref, gup_ref, gdn_ref):
        @pl.when(pl.program_id(1) == 0)
        def _():
            gup_ref[...] = jnp.zeros_like(gup_ref)
            gdn_ref[...] = jnp.zeros_like(gdn_ref)

        gup_ref[0] += _mm_tn(h2_ref[...], da_ref[...])
        hid = jnp.square(jnp.maximum(a_ref[...].astype(F32), 0.0))
        gdn_ref[0] += _mm_tn(hid, dx2b_ref[...])

    tok = pl.BlockSpec((tk, D_MODEL), lambda j, t: (t, 0))
    ffb = pl.BlockSpec((tk, FF_SHARD), lambda j, t: (t, j))
    wblk = pl.BlockSpec((1, D_MODEL, D_MODEL), lambda j, t: (j, 0, 0))
    return pl.pallas_call(
        body, name="bwd_mlp_wgrads", grid=(N_CHIPS, t_tok // tk),
        in_specs=[tok, ffb, ffb, tok], out_specs=(wblk, wblk),
        out_shape=(jax.ShapeDtypeStruct((N_CHIPS, D_MODEL, FF_SHARD), F32), jax.ShapeDtypeStruct((N_CHIPS, FF_SHARD, D_MODEL), F32)),
        compiler_params=_cp(dimension_semantics=("parallel", "arbitrary")),
    )(h2, da, a, dx2b)


def _head_selector():
    ch = lax.broadcasted_iota(jnp.int32, (ATTN_W, LANES), 0)
    col = lax.broadcasted_iota(jnp.int32, (ATTN_W, LANES), 1)
    return (ch // HEAD_DIM == col).astype(MXU_DTYPE)


def _bwd_outproj(dx1b, attn, pool, w_out, head_sel, tm, after=()):
    t_tok = dx1b.shape[0]
    tokens, token_specs = _after(after)

    def body(dx_ref, attn_ref, pool_ref, wo_ref, sel_ref, *rest):
        dattn_ref, dpool_ref, delta_ref, gwo_ref = rest[len(tokens):]

        @pl.when(pl.program_id(0) == 0)
        def _():
            gwo_ref[...] = jnp.zeros_like(gwo_ref)

        dx = dx_ref[...]
        attn = attn_ref[...]
        dattn = _mm_nt(dx, wo_ref[:ATTN_W])
        dattn_ref[...] = dattn.astype(dattn_ref.dtype)
        dpool_ref[...] = _mm_nt(dx, wo_ref[ATTN_W:])
        prod = dattn * attn.astype(F32)
        hi = prod.astype(MXU_DTYPE)
        lo = prod - hi.astype(F32)
        delta = _mm(hi, sel_ref[...]) + _mm(lo, sel_ref[...])
        delta_ref[...] = delta.T[:N_Q_HEADS]
        gwo_ref[:ATTN_W] += _mm_tn(attn, dx)
        gwo_ref[ATTN_W:] += _mm_tn(pool_ref[...], dx)

    row = lambda w: pl.BlockSpec((tm, w), lambda i: (i, 0))
    return pl.pallas_call(
        body, name="bwd_outproj", grid=(t_tok // tm,),
        in_specs=[row(D_MODEL), row(ATTN_W), row(POOL_W), _resident((D_MODEL, D_MODEL)), _const((ATTN_W, LANES))] + token_specs,
        out_specs=(row(ATTN_W), row(POOL_W), pl.BlockSpec((N_Q_HEADS, tm), lambda i: (0, i)), _const((D_MODEL, D_MODEL))),
        out_shape=(jax.ShapeDtypeStruct((t_tok, ATTN_W), MXU_DTYPE), jax.ShapeDtypeStruct((t_tok, POOL_W), F32),
                   jax.ShapeDtypeStruct((N_Q_HEADS, t_tok), F32), jax.ShapeDtypeStruct((D_MODEL, D_MODEL), F32)),
        compiler_params=_cp(dimension_semantics=("arbitrary",)),
    )(dx1b, attn, pool, w_out, head_sel, *tokens)


def _bwd_attn(q, k, v, dattn, lse, delta, sink_rows, bias, cos_t, sin_t, seq, after=()):
    t_tok = q.shape[0]
    nblk = seq // BLK
    qkv_w = ATTN_W + 2 * KV_W
    tokens, token_specs = _after(after)

    def unrope(d, cos, sin):
        return d * cos - _swap_halves(d) * sin

    def body(q_ref, k_ref, v_ref, do_ref, lse_ref, delta_ref, sink_ref, bias_ref, cos_ref, sin_ref, *rest):
        dqkv_ref, dsink_ref, dk_acc, dv_acc = rest[len(tokens):]

        @pl.when(pl.program_id(0) == 0)
        def _():
            dsink_ref[...] = jnp.zeros_like(dsink_ref)

        dk_acc[...] = jnp.zeros_like(dk_acc)
        dv_acc[...] = jnp.zeros_like(dv_acc)

        def blk(i, dsink):
            r0 = pl.multiple_of(i * BLK, BLK)
            p0 = pl.multiple_of(jnp.maximum(i - 1, 0) * BLK, BLK)
            bias = bias_ref[jnp.minimum(i, 1)]
            cos, sin = cos_ref[pl.ds(r0, BLK), :], sin_ref[pl.ds(r0, BLK), :]
            new = []
            for g in range(N_KV_HEADS):
                st, kk, qs = _scores_t(k_ref, q_ref, bias, p0, r0, g)
                lse_g, delta_g = _head_rows(lse_ref, r0, g), _head_rows(delta_ref, r0, g)
                pn = jnp.exp(st - lse_g)
                dos = _stack_heads(do_ref, r0, g)
                dst = pn * (_mm_nt(_kv_window(v_ref, p0, r0, g), dos) - delta_g)
                new.append(dsink[g] - jnp.exp(sink_ref[g] - lse_g) * delta_g)
                dqt = _mm_tn(kk, dst) * (HEAD_DIM ** -0.5)
                for t, tile in enumerate(_pairs_to_rows(dqt)):
                    c0 = (Q_PER_KV * g + 2 * t) * HEAD_DIM
                    dqkv_ref[pl.ds(r0, BLK), c0:c0 + LANES] = unrope(tile, cos, sin).astype(dqkv_ref.dtype)
                dkk = _mm(dst, qs)
                dvv = _mm(pn, dos)
                dk_acc[g, pl.ds(p0, BLK), :] += dkk[:BLK]
                dk_acc[g, pl.ds(r0, BLK), :] += dkk[BLK:]
                dv_acc[g, pl.ds(p0, BLK), :] += dvv[:BLK]
                dv_acc[g, pl.ds(r0, BLK), :] += dvv[BLK:]
            return tuple(new)

        zero = jnp.zeros((1, Q_PER_KV * BLK), F32)
        dsink = lax.fori_loop(0, nblk // 2, lambda i2, acc: blk(2 * i2 + 1, blk(2 * i2, acc)), (zero,) * N_KV_HEADS)
        lane = lax.broadcasted_iota(jnp.int32, dsink_ref.shape, 1)
        row = lax.broadcasted_iota(jnp.int32, dsink_ref.shape, 0)
        tile = jnp.zeros(dsink_ref.shape, F32)
        for g in range(N_KV_HEADS):
            for h in range(Q_PER_KV):
                tot = jnp.sum(dsink[g][:, h * BLK:(h + 1) * BLK])
                tile = tile + jnp.where((row == 0) & (lane == Q_PER_KV * g + h), tot, 0.0)
        dsink_ref[...] += tile
        dk = jnp.concatenate([dk_acc[g] for g in range(N_KV_HEADS)], axis=1)
        dqkv_ref[:, ATTN_W:ATTN_W + KV_W] = unrope(dk, cos_ref[...], sin_ref[...]).astype(dqkv_ref.dtype)
        dqkv_ref[:, ATTN_W + KV_W:] = jnp.concatenate([dv_acc[g] for g in range(N_KV_HEADS)], axis=1).astype(dqkv_ref.dtype)

    row = lambda w: pl.BlockSpec((seq, w), lambda i: (i, 0))
    kv = pl.BlockSpec((N_KV_HEADS, seq, HEAD_DIM), lambda i: (0, i, 0))
    per_head = pl.BlockSpec((N_Q_HEADS, seq), lambda i: (0, i))
    return pl.pallas_call(
        body, name="bwd_attn", grid=(t_tok // seq,),
        in_specs=[row(ATTN_W), kv, kv, row(ATTN_W), per_head, per_head, _const((N_KV_HEADS, 1, Q_PER_KV * BLK)),
                  _const((2, 2 * BLK, BLK)), _resident((seq, LANES)), _resident((seq, LANES))] + token_specs,
        out_specs=(row(qkv_w), _const((8, LANES))),
        out_shape=(jax.ShapeDtypeStruct((t_tok, qkv_w), MXU_DTYPE), jax.ShapeDtypeStruct((8, LANES), F32)),
        scratch_shapes=[pltpu.VMEM((N_KV_HEADS, seq, HEAD_DIM), F32), pltpu.VMEM((N_KV_HEADS, seq, HEAD_DIM), F32)],
        compiler_params=_cp(dimension_semantics=("arbitrary",)),
    )(q, k, v, dattn, lse, delta, sink_rows, bias, cos_t, sin_t, *tokens)


def _bwd_pool(u, dpool, w_pool, pool_scale, seq):
    t_tok = u.shape[0]

    def body(u_ref, dp_ref, wp_ref, sc_ref, du_ref, dwp_ref, dsc_ref):
        @pl.when(pl.program_id(0) == 0)
        def _():
            dwp_ref[...] = jnp.zeros_like(dwp_ref)
            dsc_ref[...] = jnp.zeros_like(dsc_ref)

        for gi, w in enumerate(POOL_WINDOWS):
            sl = slice(gi * POOL_G, (gi + 1) * POOL_G)
            ug = u_ref[:, sl]
            inv = _inv_count(seq, w)
            d = (_window_sum(ug, w, seq, False) * inv - ug).astype(MXU_DTYPE)
            y = _mm(d, wp_ref[gi])
            dpool = dp_ref[:, sl]
            dsc_ref[:, sl] += jnp.sum(y * dpool, axis=0, keepdims=True)
            dy = (dpool * sc_ref[:, sl]).astype(MXU_DTYPE)
            dwp_ref[gi] += _mm_tn(d, dy)
            dd = _mm_nt(dy, wp_ref[gi])
            du_ref[:, sl] = (_window_sum(dd * inv, w, seq, True) - dd).astype(du_ref.dtype)

    row = pl.BlockSpec((seq, POOL_W), lambda i: (i, 0))
    return pl.pallas_call(
        body, name="bwd_pool", grid=(t_tok // seq,),
        in_specs=[row, row, _const((N_POOL, POOL_G, POOL_G)), _const((1, POOL_W))],
        out_specs=(row, _const((N_POOL, POOL_G, POOL_G)), _const((1, POOL_W))),
        out_shape=(jax.ShapeDtypeStruct((t_tok, POOL_W), MXU_DTYPE), jax.ShapeDtypeStruct((N_POOL, POOL_G, POOL_G), F32),
                   jax.ShapeDtypeStruct((1, POOL_W), F32)),
        compiler_params=_cp(dimension_semantics=("arbitrary",)),
    )(u, dpool, w_pool, pool_scale)


def _bwd_inproj(dqkv, du, x, dx1, w_in, g1, tm):
    t_tok = x.shape[0]
    nsteps = t_tok // tm

    qkv_w = ATTN_W + 2 * KV_W

    def body(dqkv_ref, du_ref, x_ref, dx1_ref, w_ref, g_ref, gx_ref, gw_ref, dg_ref, acc_ref):
        @pl.when(pl.program_id(0) == 0)
        def _():
            acc_ref[...] = jnp.zeros_like(acc_ref)
            dg_ref[...] = jnp.zeros_like(dg_ref)

        dqkv, du = dqkv_ref[...], du_ref[...]
        xv = x_ref[...]
        r = lax.rsqrt(jnp.mean(xv * xv, axis=-1, keepdims=True) + EPS)
        xn = xv * r
        g = g_ref[...]
        dh = _mm_nt(dqkv, w_ref[:, :qkv_w]) + _mm_nt(du, w_ref[:, qkv_w:])
        dg_ref[...] += jnp.sum(dh * xn, axis=0, keepdims=True)
        gx_ref[...] = dx1_ref[...] + _rms_bwd(dh * g, xn, r)
        h = (xn * g).astype(MXU_DTYPE)
        acc_ref[:, :qkv_w] += _mm_tn(h, dqkv)
        acc_ref[:, qkv_w:] += _mm_tn(h, du)

        @pl.when(pl.program_id(0) == nsteps - 1)
        def _():
            acc = acc_ref[...]
            for j in range(N_CHIPS):
                gw_ref[j] = acc[:, j * IN_SHARD:(j + 1) * IN_SHARD]

    row = lambda w: pl.BlockSpec((tm, w), lambda i: (i, 0))
    return pl.pallas_call(
        body, name="bwd_inproj", grid=(nsteps,),
        in_specs=[row(qkv_w), row(POOL_W), row(D_MODEL), row(D_MODEL), _resident((D_MODEL, IN_W)), _const((1, D_MODEL))],
        out_specs=(row(D_MODEL), _const((N_CHIPS, D_MODEL, IN_SHARD)), _const((1, D_MODEL))),
        out_shape=(jax.ShapeDtypeStruct((t_tok, D_MODEL), F32), jax.ShapeDtypeStruct((N_CHIPS, D_MODEL, IN_SHARD), F32),
                   jax.ShapeDtypeStruct((1, D_MODEL), F32)),
        scratch_shapes=[pltpu.VMEM((D_MODEL, IN_W), F32)],
        compiler_params=_cp(dimension_semantics=("arbitrary",)),
    )(dqkv, du, x, dx1, w_in, g1)


class _NoComm:
    def start(self):
        return ()

    def attn_done(self, attn):
        return ()

    def rest_of_weights(self, pool):
        raise NotImplementedError

    def mlp_grads_ready(self, gw_up4, gw_down4):
        return ()

    def outproj_done(self, gw_out):
        return ()


def _local_step(x, target, w_in, comm, g1, sinks, w_pool, pool_scale, g2, gf, seq):
    tm = min(512, seq)
    tm_mlp = min(256, seq)
    cos_t, sin_t = _rope_tables(seq)
    sink_rows, bias = _sink_rows(sinks), _attn_bias()
    q, k, v, u = _fwd_inproj(x, g1, w_in, cos_t, sin_t, tm, comm.start())
    attn, lse = _fwd_attn(q, k, v, sink_rows, bias, seq)
    pool = _fwd_pool(u, w_pool, pool_scale, seq, comm.attn_done(attn))
    w_out, w_up4, w_down4 = comm.rest_of_weights(pool)
    x1, h2, a, dx2, dx2b, loss, dgf = _fwd_mlp_loss(x, attn, pool, target, w_out, w_up4, w_down4, g2, gf, tm_mlp)
    da, dx1, dx1b, dg2 = _bwd_mlp(dx2, dx2b, a, x1, w_up4, w_down4, g2, tm_mlp)
    gw_up4, gw_down4 = _bwd_mlp_wgrads(h2, da, a, dx2b, min(1024, x.shape[0]))
    dattn, dpool, delta, gw_out = _bwd_outproj(dx1b, attn, pool, w_out, _head_selector(), tm,
                                               comm.mlp_grads_ready(gw_up4, gw_down4))
    dqkv, dsinks = _bwd_attn(q, k, v, dattn, lse, delta, sink_rows, bias, cos_t, sin_t, seq, comm.outproj_done(gw_out))
    du, dwp, dsc = _bwd_pool(u, dpool, w_pool, pool_scale, seq)
    gx, gw_in4, dg1 = _bwd_inproj(dqkv, du, x, dx1, w_in, g1, tm)
    big = (gw_in4, gw_out.reshape(N_CHIPS, OUT_SHARD, D_MODEL), gw_up4, gw_down4)
    small = (dg1, dsinks, dwp.reshape(N_POOL * POOL_G, POOL_G), dsc, dg2, dgf, loss)
    return gx, big, small


def _sibling_exchange(arrs, pick_half, name):
    n = len(arrs)

    def out_shape(a):
        if pick_half:
            return (a.shape[0], a.shape[1] // 2) + a.shape[2:]
        return a.shape

    def body(*refs):
        srcs, dsts = refs[:n], refs[n:2 * n]
        send, recv = refs[2 * n:]
        x, y, c = lax.axis_index("x"), lax.axis_index("y"), lax.axis_index("c")
        cps = []
        for i in range(n):
            src = srcs[i]
            if pick_half:
                h = src.shape[1] // 2
                src = src.at[:, pl.ds((1 - c) * h, h)]
            cp = pltpu.make_async_remote_copy(src_ref=src, dst_ref=dsts[i], send_sem=send.at[i], recv_sem=recv.at[i],
                                              device_id=(x, y, 1 - c), device_id_type=MESH)
            cp.start()
            cps.append(cp)
        for cp in cps:
            cp.wait()

    hbm = pl.BlockSpec(memory_space=pltpu.HBM)
    return pl.pallas_call(
        body, name=name, out_shape=tuple(jax.ShapeDtypeStruct(out_shape(a), a.dtype) for a in arrs),
        in_specs=[hbm] * n, out_specs=tuple([hbm] * n),
        scratch_shapes=[pltpu.SemaphoreType.DMA((n,)), pltpu.SemaphoreType.DMA((n,))],
        compiler_params=_cp(),
    )(*arrs)


def _sibling_copies(srcs, lands, send, recv):
    x, y, c = lax.axis_index("x"), lax.axis_index("y"), lax.axis_index("c")
    cps = []
    for i in range(len(srcs)):
        h = srcs[i].shape[1] // 2
        cps.append(pltpu.make_async_remote_copy(
            src_ref=srcs[i].at[:, pl.ds((1 - c) * h, h)], dst_ref=lands[i], send_sem=send.at[i], recv_sem=recv.at[i],
            device_id=(x, y, 1 - c), device_id_type=MESH))
    return cps


def _sibling_start(arrs, name):
    n = len(arrs)

    def body(*refs):
        send, recv, token = refs[3 * n:3 * n + 3]
        for cp in _sibling_copies(refs[:n], refs[2 * n:3 * n], send, recv):
            cp.start()
        token[...] = jnp.zeros_like(token)

    half = lambda a: (a.shape[0], a.shape[1] // 2) + a.shape[2:]
    res = pl.pallas_call(
        body, name=name + "_start",
        out_shape=tuple(pltpu.HBM(a.shape, a.dtype) for a in arrs) + tuple(pltpu.HBM(half(a), a.dtype) for a in arrs) + (
            pltpu.SemaphoreType.DMA((n,)), pltpu.SemaphoreType.DMA((n,)), TOKEN),
        in_specs=[_HBM] * n, out_specs=tuple([_HBM] * (2 * n)) + (_SEM, _SEM, pl.BlockSpec(memory_space=pltpu.VMEM)),
        input_output_aliases={i: i for i in range(n)},
        compiler_params=_cp(has_side_effects=_EFFECT),
    )(*[pltpu.with_memory_space_constraint(a, pltpu.HBM) for a in arrs])
    return res[:n], res[n:2 * n], res[2 * n], res[2 * n + 1], res[2 * n + 2]


def _sibling_wait(arrs, lands, send, recv, after, name):
    n = len(arrs)

    def body(*refs):
        for cp in _sibling_copies(refs[:n], refs[n:2 * n], refs[2 * n], refs[2 * n + 1]):
            cp.wait_send()
            cp.wait_recv()

    res = pl.pallas_call(
        body, name=name + "_wait", out_shape=tuple(pltpu.HBM(a.shape, a.dtype) for a in list(arrs) + list(lands)),
        in_specs=[_HBM] * (2 * n) + [_SEM, _SEM, _ANY], out_specs=tuple([_HBM] * (2 * n)),
        input_output_aliases={i: i for i in range(2 * n)},
        compiler_params=_cp(has_side_effects=_EFFECT),
    )(*arrs, *lands, send, recv, after)
    return res[:n], res[n:]


def _row_block(rows):
    for cand in (256, 128, 64, 32, 16, 8):
        if rows % cand == 0:
            return cand
    raise ValueError(rows)


def _chip_partial(g4, r4, c_arr, name):
    _, rows, cols = r4.shape
    rb = _row_block(rows)
    nb = rows // rb

    def body(c_ref, g_ref, r_ref, o_ref):
        o_ref[...] = (g_ref[...] + r_ref[...]).astype(o_ref.dtype)

    return pl.pallas_call(
        body, name=name,
        grid_spec=pltpu.PrefetchScalarGridSpec(
            num_scalar_prefetch=1, grid=(N_CHIPS, nb),
            in_specs=[pl.BlockSpec((1, rb, cols), lambda s, i, c: (s, c[0] * nb + i, 0)),
                      pl.BlockSpec((1, rb, cols), lambda s, i, c: (s, i, 0))],
            out_specs=pl.BlockSpec((1, rb, cols), lambda s, i, c: (s, i, 0))),
        out_shape=jax.ShapeDtypeStruct(r4.shape, BF16),
        compiler_params=_cp(dimension_semantics=("parallel", "parallel")),
    )(c_arr, g4, r4)


def _send_partials(parts):
    n = len(parts)

    def body(*refs):
        srcs, dsts = refs[:n], refs[n:2 * n]
        send, recv = refs[2 * n:]
        x, y, c = lax.axis_index("x"), lax.axis_index("y"), lax.axis_index("c")
        chips = [(1 - x, y), (x, 1 - y), (1 - x, 1 - y)]
        cps = []
        for i in range(n):
            for m, chip in enumerate(chips):
                cp = pltpu.make_async_remote_copy(
                    src_ref=srcs[i].at[2 * chip[0] + chip[1]], dst_ref=dsts[i].at[m],
                    send_sem=send.at[3 * i + m], recv_sem=recv.at[3 * i + m],
                    device_id=(chip[0], chip[1], c), device_id_type=MESH)
                cp.start()
                cps.append(cp)
        for cp in cps:
            cp.wait()

    hbm = pl.BlockSpec(memory_space=pltpu.HBM)
    return pl.pallas_call(
        body, name="send_partials",
        out_shape=tuple(jax.ShapeDtypeStruct((3,) + p.shape[1:], p.dtype) for p in parts),
        in_specs=[hbm] * n, out_specs=tuple([hbm] * n),
        scratch_shapes=[pltpu.SemaphoreType.DMA((3 * n,)), pltpu.SemaphoreType.DMA((3 * n,))],
        compiler_params=_cp(),
    )(*parts)


def _send_start(parts):
    n = len(parts)

    def body(*refs):
        srcs, lands = refs[:n], refs[2 * n:3 * n]
        send, recv, token = refs[3 * n:3 * n + 3]
        x, y, c = lax.axis_index("x"), lax.axis_index("y"), lax.axis_index("c")
        for i in range(n):
            for m, chip in enumerate(_other_chips(x, y)):
                pltpu.make_async_remote_copy(
                    src_ref=srcs[i].at[2 * chip[0] + chip[1]], dst_ref=lands[i].at[m],
                    send_sem=send.at[3 * i + m], recv_sem=recv.at[3 * i + m],
                    device_id=(chip[0], chip[1], c), device_id_type=MESH).start()
        token[...] = jnp.zeros_like(token)

    res = pl.pallas_call(
        body, name="send_start",
        out_shape=tuple(pltpu.HBM(p.shape, p.dtype) for p in parts) + tuple(pltpu.HBM((3,) + p.shape[1:], p.dtype) for p in parts) + (
            pltpu.SemaphoreType.DMA((3 * n,)), pltpu.SemaphoreType.DMA((3 * n,)), TOKEN),
        in_specs=[_HBM] * n, out_specs=tuple([_HBM] * (2 * n)) + (_SEM, _SEM, pl.BlockSpec(memory_space=pltpu.VMEM)),
        input_output_aliases={i: i for i in range(n)},
        compiler_params=_cp(has_side_effects=_EFFECT),
    )(*[pltpu.with_memory_space_constraint(p, pltpu.HBM) for p in parts])
    return res[:n], res[n:2 * n], res[2 * n], res[2 * n + 1], res[2 * n + 2]


def _send_wait(parts, lands, send, recv, after):
    n = len(parts)

    def body(*refs):
        srcs, ins = refs[:n], refs[n:2 * n]
        send_ref, recv_ref = refs[2 * n], refs[2 * n + 1]
        x, y, c = lax.axis_index("x"), lax.axis_index("y"), lax.axis_index("c")
        for i in range(n):
            for m, chip in enumerate(_other_chips(x, y)):
                cp = pltpu.make_async_remote_copy(
                    src_ref=srcs[i].at[2 * chip[0] + chip[1]], dst_ref=ins[i].at[m],
                    send_sem=send_ref.at[3 * i + m], recv_sem=recv_ref.at[3 * i + m],
                    device_id=(chip[0], chip[1], c), device_id_type=MESH)
                cp.wait_send()
                cp.wait_recv()

    res = pl.pallas_call(
        body, name="send_wait", out_shape=tuple(pltpu.HBM(a.shape, a.dtype) for a in list(parts) + list(lands)),
        in_specs=[_HBM] * (2 * n) + [_SEM, _SEM, _ANY], out_specs=tuple([_HBM] * (2 * n)),
        input_output_aliases={i: i for i in range(2 * n)},
        compiler_params=_cp(has_side_effects=_EFFECT),
    )(*parts, *lands, send, recv, after)
    return res[n:]


def _final_half(g4, r4, got3, jc_arr, name):
    _, rows, cols = r4.shape
    rb = _row_block(rows)
    nb = rows // rb

    def body(jc_ref, g_ref, r_ref, p_ref, o_ref):
        own = g_ref[0] + r_ref[0]
        o_ref[...] = ((own + p_ref[0].astype(F32)) + p_ref[1].astype(F32)) + p_ref[2].astype(F32)

    return pl.pallas_call(
        body, name=name,
        grid_spec=pltpu.PrefetchScalarGridSpec(
            num_scalar_prefetch=1, grid=(nb,),
            in_specs=[pl.BlockSpec((1, rb, cols), lambda i, jc: (jc[0], jc[1] * nb + i, 0)),
                      pl.BlockSpec((1, rb, cols), lambda i, jc: (jc[0], i, 0)),
                      pl.BlockSpec((3, rb, cols), lambda i, jc: (0, i, 0))],
            out_specs=pl.BlockSpec((rb, cols), lambda i, jc: (i, 0))),
        out_shape=jax.ShapeDtypeStruct((rows, cols), F32),
        compiler_params=_cp(dimension_semantics=("parallel",)),
    )(jc_arr, g4, r4, got3)


def _adamw_math(w, g, m, v):
    m2 = ADAM_B1 * m + (1.0 - ADAM_B1) * g
    v2 = ADAM_B2 * v + (1.0 - ADAM_B2) * (g * g)
    m_hat = m2 / (1.0 - ADAM_B1 ** ADAM_STEP)
    v_hat = v2 / (1.0 - ADAM_B2 ** ADAM_STEP)
    delta = -ADAM_LR * (m_hat / (jnp.sqrt(v_hat) + ADAM_EPS) + ADAM_WD * w)
    return delta, m2, v2


def _adamw_shard(mine, other, w, m, v, c_arr, name):
    rows, cols = w.shape
    half = rows // 2
    rb = _row_block(half)
    nb = half // rb

    def body(c_ref, a_ref, b_ref, w_ref, m_ref, v_ref, g_ref, d_ref, m2_ref, v2_ref):
        g = jnp.where(pl.program_id(0) == c_ref[0], a_ref[...], b_ref[...])
        delta, m2, v2 = _adamw_math(w_ref[...], g, m_ref[...], v_ref[...])
        g_ref[...] = g
        d_ref[...] = delta
        m2_ref[...] = m2
        v2_ref[...] = v2

    hb = pl.BlockSpec((rb, cols), lambda h, i, c: (i, 0))
    fb = pl.BlockSpec((rb, cols), lambda h, i, c: (h * nb + i, 0))
    shp = jax.ShapeDtypeStruct((rows, cols), F32)
    return pl.pallas_call(
        body, name=name,
        grid_spec=pltpu.PrefetchScalarGridSpec(num_scalar_prefetch=1, grid=(2, nb), in_specs=[hb, hb, fb, fb, fb],
                                               out_specs=(fb, fb, fb, fb)),
        out_shape=(shp, shp, shp, shp),
        compiler_params=_cp(dimension_semantics=("parallel", "parallel")),
    )(c_arr, mine, other, w, m, v)


def _small_allreduce(parts):
    n = len(parts)

    def body(*refs):
        p_refs, accs = refs[:n], refs[n:2 * n]
        bufs = refs[2 * n:3 * n]
        send, recv = refs[3 * n:]
        x, y, c = lax.axis_index("x"), lax.axis_index("y"), lax.axis_index("c")
        partners = [(x, y, 1 - c), (1 - x, y, c), (x, 1 - y, c)]
        for i in range(n):
            accs[i][...] = p_refs[i][...]
        for s, partner in enumerate(partners):
            cps = []
            for i in range(n):
                cp = pltpu.make_async_remote_copy(src_ref=accs[i], dst_ref=bufs[i].at[s], send_sem=send.at[3 * i + s],
                                                  recv_sem=recv.at[3 * i + s], device_id=partner, device_id_type=MESH)
                cp.start()
                cps.append(cp)
            for cp in cps:
                cp.wait()
            for i in range(n):
                accs[i][...] = accs[i][...] + bufs[i][s]

    vmem = pl.BlockSpec(memory_space=pltpu.VMEM)
    return pl.pallas_call(
        body, name="small_allreduce", out_shape=tuple(jax.ShapeDtypeStruct(p.shape, F32) for p in parts),
        in_specs=[vmem] * n, out_specs=tuple([vmem] * n),
        scratch_shapes=[pltpu.VMEM((3,) + p.shape, F32) for p in parts] + [
            pltpu.SemaphoreType.DMA((3 * n,)), pltpu.SemaphoreType.DMA((3 * n,))],
        compiler_params=_cp(),
    )(*parts)


def _small_adamw(reduced, params):
    n = len(reduced)
    n_w = len(params)

    def body(*refs):
        r_refs = refs[:n]
        wmv = refs[n:n + 3 * n_w]
        outs = refs[n + 3 * n_w:]
        outs[0][...] = r_refs[n - 1][0:1, 0:1]
        grads = [r_refs[0][...], r_refs[1][0:1, 0:N_Q_HEADS]] + [r_refs[i][...] for i in range(2, n_w)]
        for i in range(n_w):
            w_ref, m_ref, v_ref = wmv[3 * i:3 * i + 3]
            g_ref, d_ref, m2_ref, v2_ref = outs[1 + 4 * i:5 + 4 * i]
            delta, m2, v2 = _adamw_math(w_ref[...], grads[i], m_ref[...], v_ref[...])
            g_ref[...] = grads[i]
            d_ref[...] = delta
            m2_ref[...] = m2
            v2_ref[...] = v2

    flat = [a for p in params for a in p]
    vmem = pl.BlockSpec(memory_space=pltpu.VMEM)
    out_shape = [jax.ShapeDtypeStruct((1, 1), F32)]
    for p in params:
        out_shape += [jax.ShapeDtypeStruct(p[0].shape, F32)] * 4
    res = pl.pallas_call(
        body, name="small_adamw", out_shape=tuple(out_shape),
        in_specs=[vmem] * (n + len(flat)), out_specs=tuple([vmem] * len(out_shape)),
        compiler_params=_cp(),
    )(*reduced, *flat)
    return res[0], [res[1 + 4 * i:5 + 4 * i] for i in range(n_w)]


def kernel(x, attn_norm_g, w_in, attn_sinks, w_pool, pool_scale, w_out, mlp_norm_g, w_up, w_down, final_norm_g, loss_target, m_attn_norm_g, m_w_in, m_attn_sinks, m_w_pool, m_pool_scale, m_w_out, m_mlp_norm_g, m_w_up, m_w_down, m_final_norm_g, v_attn_norm_g, v_w_in, v_attn_sinks, v_w_pool, v_pool_scale, v_w_out, v_mlp_norm_g, v_w_up, v_w_down, v_final_norm_g):
    nseq, seq, d = x.shape
    c_idx = lax.axis_index("c").astype(jnp.int32)
    j_idx = (2 * lax.axis_index("x") + lax.axis_index("y")).astype(jnp.int32)
    c_arr = jnp.reshape(c_idx, (1,))
    jc_arr = jnp.stack([j_idx, c_idx])

    big_w = (w_in[0], w_out[0], w_up[0], w_down[0])
    big_m = (m_w_in[0], m_w_out[0], m_w_up[0], m_w_down[0])
    big_v = (v_w_in[0], v_w_out[0], v_w_up[0], v_w_down[0])
    (w_in4,) = _gather_weights(big_w[0])
    w_in_full = _concat_w_in(w_in4)
    class Comm(_NoComm):
        def start(self):
            self.lands, self.send, self.recv, token = _gather_start(big_w[1:], w_in4)
            return (token,)

        def attn_done(self, attn):
            arrived = _gather_wait(self.lands, self.send, self.recv, attn)
            self.lands, self.send, self.recv, token = _forward_start(arrived)
            return (token,)

        def rest_of_weights(self, pool):
            w_out4, w_up4, w_down4 = _forward_wait(self.lands, self.send, self.recv, pool)
            return w_out4.reshape(D_MODEL, D_MODEL), w_up4, w_down4

        def mlp_grads_ready(self, gw_up4, gw_down4):
            self.mlp = _sibling_start((gw_up4, gw_down4), "mlp_grads_to_sibling")
            return (self.mlp[4],)

        def outproj_done(self, gw_out):
            grads, lands, send, recv, _ = self.mlp
            gw_out4 = gw_out.reshape(N_CHIPS, OUT_SHARD, D_MODEL)
            mlp_grads, mlp_from_sib = _sibling_wait(grads, lands, send, recv, gw_out, "mlp_grads_to_sibling")
            self.grads = (gw_out4,) + tuple(mlp_grads)
            self.from_sib = tuple(_sibling_exchange((gw_out4,), True, "w_out_grad_to_sibling")) + tuple(mlp_from_sib)
            partials = [_chip_partial(g, r, c_arr, "chip_partial_" + nm)
                        for g, r, nm in zip(self.grads, self.from_sib, names[1:])]
            self.parts, self.part_lands, self.part_send, self.part_recv, token = _send_start(partials)
            return (token,)

    names = ("w_in", "w_out", "w_up", "w_down")
    comm = Comm()
    gx, big_g, small_g = _local_step(
        x.reshape(nseq * seq, d), loss_target.reshape(nseq * seq, d), w_in_full, comm,
        attn_norm_g, attn_sinks.reshape(N_Q_HEADS), w_pool[0], pool_scale, mlp_norm_g, final_norm_g.reshape(1, d), seq)
    big_g = tuple(big_g[:1]) + comm.grads

    from_sibling = list(_sibling_exchange(big_g[:1], True, "w_in_grad_to_sibling")) + list(comm.from_sib)
    received = list(_send_partials([_chip_partial(big_g[0], from_sibling[0], c_arr, "chip_partial_w_in")]))
    received += list(_send_wait(comm.parts, comm.part_lands, comm.part_send, comm.part_recv, received[0]))
    mine = [_final_half(g, r, p, jc_arr, "final_half_" + nm) for g, r, p, nm in zip(big_g, from_sibling, received, names)]
    other = _sibling_exchange(mine, False, "halves_to_sibling")
    big_out = [_adamw_shard(a, b, w, m, v, c_arr, "adamw_" + nm)
               for a, b, w, m, v, nm in zip(mine, other, big_w, big_m, big_v, names)]

    wp_flat = lambda a: a.reshape(N_POOL * POOL_G, POOL_G)
    small_params = [
        (attn_norm_g, m_attn_norm_g, v_attn_norm_g),
        (attn_sinks, m_attn_sinks, v_attn_sinks),
        (wp_flat(w_pool), wp_flat(m_w_pool), wp_flat(v_w_pool)),
        (pool_scale, m_pool_scale, v_pool_scale),
        (mlp_norm_g, m_mlp_norm_g, v_mlp_norm_g),
        (final_norm_g.reshape(1, d), m_final_norm_g.reshape(1, d), v_final_norm_g.reshape(1, d)),
    ]
    loss, small_out = _small_adamw(_small_allreduce(small_g), small_params)

    def shaped(i, arr):
        return {1: w_in, 5: w_out, 7: w_up, 8: w_down, 0: attn_norm_g, 2: attn_sinks, 3: w_pool, 4: pool_scale,
                6: mlp_norm_g, 9: final_norm_g}[i].shape

    order = [small_out[0], big_out[0], small_out[1], small_out[2], small_out[3], big_out[1], small_out[4], big_out[2],
             big_out[3], small_out[5]]
    outs = [loss.reshape(()), gx.reshape(nseq, seq, d)]
    for kind in range(4):
        outs += [order[i][kind].reshape(shaped(i, None)) for i in range(10)]
    return tuple(outs)
```

```python
import jax
import jax.numpy as jnp
import numpy as np
from jax import lax
from jax.experimental import pallas as pl
from jax.experimental.pallas import tpu as pltpu

F32 = jnp.float32
BF16 = jnp.bfloat16
MXU_DTYPE = jnp.bfloat16

D_MODEL = 1024
HEAD_DIM = 64
N_Q_HEADS = 8
N_KV_HEADS = 2
Q_PER_KV = N_Q_HEADS // N_KV_HEADS
ATTN_W = N_Q_HEADS * HEAD_DIM
KV_W = N_KV_HEADS * HEAD_DIM
BLK = 128
POOL_WINDOWS = (2, 4, 8, 16)
N_POOL = len(POOL_WINDOWS)
POOL_W = D_MODEL - ATTN_W
POOL_G = POOL_W // N_POOL
IN_W = ATTN_W + 2 * KV_W + POOL_W
D_FF = 4 * D_MODEL
EPS = 1e-6
ROPE_THETA = 10000.0
N_CHIPS = 4
IN_SHARD = IN_W // N_CHIPS
OUT_SHARD = D_MODEL // N_CHIPS
FF_SHARD = D_FF // N_CHIPS
LANES = 128

ADAM_LR = 0.001
ADAM_B1 = 0.9
ADAM_B2 = 0.999
ADAM_EPS = 1e-08
ADAM_WD = 0.01
ADAM_STEP = 10

VMEM_LIMIT = 56 * 1024 * 1024
MESH = pl.DeviceIdType.MESH


def _cp(**kw):
    return pltpu.CompilerParams(vmem_limit_bytes=VMEM_LIMIT, **kw)


def _mm(a, b):
    return jnp.dot(a.astype(MXU_DTYPE), b.astype(MXU_DTYPE), preferred_element_type=F32)


def _mm_nt(a, b):
    return lax.dot_general(a.astype(MXU_DTYPE), b.astype(MXU_DTYPE), (((1,), (1,)), ((), ())),
                           preferred_element_type=F32)


def _mm_tn(a, b):
    return lax.dot_general(a.astype(MXU_DTYPE), b.astype(MXU_DTYPE), (((0,), (0,)), ((), ())),
                           preferred_element_type=F32)


def _resident(shape):
    nd = len(shape)
    return pl.BlockSpec(shape, lambda *_: (0,) * nd, pipeline_mode=pl.Buffered(1))


def _const(shape):
    nd = len(shape)
    return pl.BlockSpec(shape, lambda *_: (0,) * nd)


def _rope_tables(seq):
    half = HEAD_DIM // 2
    inv_freq = (ROPE_THETA ** (-np.arange(half, dtype=np.float32) / half)).astype(np.float32)
    ang = np.arange(seq, dtype=np.float32)[:, None] * inv_freq[None, :]
    cos, sin = np.cos(ang).astype(np.float32), np.sin(ang).astype(np.float32)
    cos_t = np.concatenate([cos, cos, cos, cos], axis=1)
    sin_t = np.concatenate([-sin, sin, -sin, sin], axis=1)
    return jnp.asarray(cos_t), jnp.asarray(sin_t)


def _swap_halves(xc):
    lane = lax.broadcasted_iota(jnp.int32, xc.shape, 1)
    return jnp.where((lane & 63) < 32, pltpu.roll(xc, 96, 1), pltpu.roll(xc, 32, 1))


def _gather_weights(*shards):
    nw = len(shards)

    def body(*refs):
        srcs = refs[0:nw]
        outs = refs[nw:2 * nw]
        stages = refs[2 * nw:3 * nw]
        send1, recv1, send2, recv2, lsem = refs[3 * nw:]
        x, y, c = lax.axis_index("x"), lax.axis_index("y"), lax.axis_index("c")
        j = 2 * x + y
        chips = [(1 - x, y), (x, 1 - y), (1 - x, 1 - y)]
        for k in range(nw):
            stages[k][...] = srcs[k][...].astype(BF16)
        local = [pltpu.make_async_copy(stages[k], outs[k].at[j], lsem.at[k]) for k in range(nw)]
        for cp in local:
            cp.start()

        def piece(k, chip, half):
            rows = shards[k].shape[0] // 2
            return outs[k].at[2 * chip[0] + chip[1], pl.ds(half * rows, rows)]

        first, passed = [], []
        for k in range(nw):
            rows = shards[k].shape[0] // 2
            for n, chip in enumerate(chips):
                cp = pltpu.make_async_remote_copy(
                    src_ref=stages[k].at[pl.ds(c * rows, rows)], dst_ref=piece(k, (x, y), c),
                    send_sem=send1.at[k * 3 + n], recv_sem=recv1.at[k * 3 + n],
                    device_id=(chip[0], chip[1], c), device_id_type=MESH)
                cp.start()
                first.append(cp)
        for k in range(nw):
            for n, chip in enumerate(chips):
                got = piece(k, chip, c)
                pltpu.make_async_remote_copy(
                    src_ref=got, dst_ref=got, send_sem=send1.at[k * 3 + n], recv_sem=recv1.at[k * 3 + n],
                    device_id=(chip[0], chip[1], c), device_id_type=MESH).wait_recv()
                cp = pltpu.make_async_remote_copy(
                    src_ref=got, dst_ref=got, send_sem=send2.at[k * 3 + n], recv_sem=recv2.at[k * 3 + n],
                    device_id=(x, y, 1 - c), device_id_type=MESH)
                cp.start()
                passed.append(cp)
        for k in range(nw):
            for n, chip in enumerate(chips):
                got = piece(k, chip, 1 - c)
                pltpu.make_async_remote_copy(
                    src_ref=got, dst_ref=got, send_sem=send2.at[k * 3 + n], recv_sem=recv2.at[k * 3 + n],
                    device_id=(x, y, 1 - c), device_id_type=MESH).wait_recv()
        for cp in first + passed:
            cp.wait_send()
        for cp in local:
            cp.wait()

    hbm = pl.BlockSpec(memory_space=pltpu.HBM)
    vmem = pl.BlockSpec(memory_space=pltpu.VMEM)
    return pl.pallas_call(
        body, name="gather_weights",
        out_shape=tuple(jax.ShapeDtypeStruct((N_CHIPS,) + s.shape, BF16) for s in shards),
        in_specs=[vmem] * nw, out_specs=tuple([hbm] * nw),
        scratch_shapes=[pltpu.VMEM(s.shape, BF16) for s in shards] + [
            pltpu.SemaphoreType.DMA((3 * nw,)), pltpu.SemaphoreType.DMA((3 * nw,)),
            pltpu.SemaphoreType.DMA((3 * nw,)), pltpu.SemaphoreType.DMA((3 * nw,)),
            pltpu.SemaphoreType.DMA((nw,))],
        compiler_params=_cp(),
    )(*shards)


_HBM = pl.BlockSpec(memory_space=pltpu.HBM)
_SEM = pl.BlockSpec(memory_space=pltpu.SEMAPHORE)
_ANY = pl.BlockSpec(memory_space=pl.ANY)
_EFFECT = pltpu.SideEffectType.DATAFLOW_SIDE_EFFECTING
TOKEN = jax.ShapeDtypeStruct((8, LANES), F32)


def _other_chips(x, y):
    return [(1 - x, y), (x, 1 - y), (1 - x, 1 - y)]


def _gather_start(shards, after):
    nw = len(shards)

    def body(*refs):
        srcs, lands = refs[:nw], refs[nw + 1:2 * nw + 1]
        send, recv, token = refs[2 * nw + 1:2 * nw + 4]
        stages = refs[2 * nw + 4:3 * nw + 4]
        lsem = refs[3 * nw + 4]
        x, y, c = lax.axis_index("x"), lax.axis_index("y"), lax.axis_index("c")
        j = 2 * x + y
        for k in range(nw):
            stages[k][...] = srcs[k][...].astype(BF16)
        local = [pltpu.make_async_copy(stages[k], lands[k].at[j], lsem.at[k]) for k in range(nw)]
        for cp in local:
            cp.start()
        for cp in local:
            cp.wait()
        for k in range(nw):
            rows = shards[k].shape[0] // 2
            mine = lands[k].at[j, pl.ds(c * rows, rows)]
            for n, chip in enumerate(_other_chips(x, y)):
                pltpu.make_async_remote_copy(
                    src_ref=mine, dst_ref=mine, send_sem=send.at[k * 3 + n], recv_sem=recv.at[k * 3 + n],
                    device_id=(chip[0], chip[1], c), device_id_type=MESH).start()
        token[...] = jnp.zeros_like(token)

    vmem = pl.BlockSpec(memory_space=pltpu.VMEM)
    res = pl.pallas_call(
        body, name="gather_start",
        out_shape=tuple(pltpu.HBM((N_CHIPS,) + s.shape, BF16) for s in shards) + (
            pltpu.SemaphoreType.DMA((3 * nw,)), pltpu.SemaphoreType.DMA((3 * nw,)), TOKEN),
        in_specs=[vmem] * nw + [_ANY], out_specs=tuple([_HBM] * nw) + (_SEM, _SEM, vmem),
        scratch_shapes=[pltpu.VMEM(s.shape, BF16) for s in shards] + [pltpu.SemaphoreType.DMA((nw,))],
        compiler_params=_cp(has_side_effects=_EFFECT),
    )(*shards, after)
    return res[:nw], res[nw], res[nw + 1], res[nw + 2]


def _gather_wait(lands, send, recv, after):
    nw = len(lands)

    def body(*refs):
        ins = refs[:nw]
        send_ref, recv_ref = refs[nw], refs[nw + 1]
        x, y, c = lax.axis_index("x"), lax.axis_index("y"), lax.axis_index("c")
        j = 2 * x + y
        for k in range(nw):
            rows = lands[k].shape[1] // 2
            mine = ins[k].at[j, pl.ds(c * rows, rows)]
            for n, chip in enumerate(_other_chips(x, y)):
                got = ins[k].at[2 * chip[0] + chip[1], pl.ds(c * rows, rows)]
                cp = pltpu.make_async_remote_copy(
                    src_ref=mine, dst_ref=got, send_sem=send_ref.at[k * 3 + n], recv_sem=recv_ref.at[k * 3 + n],
                    device_id=(chip[0], chip[1], c), device_id_type=MESH)
                cp.wait_send()
                cp.wait_recv()

    return pl.pallas_call(
        body, name="gather_wait", out_shape=tuple(pltpu.HBM(a.shape, a.dtype) for a in lands),
        in_specs=[_HBM] * nw + [_SEM, _SEM, _ANY], out_specs=tuple([_HBM] * nw),
        input_output_aliases={k: k for k in range(nw)},
        compiler_params=_cp(has_side_effects=_EFFECT),
    )(*lands, send, recv, after)


def _forward_copies(refs, nw, rows_of, send, recv):
    x, y, c = lax.axis_index("x"), lax.axis_index("y"), lax.axis_index("c")
    out = []
    for k in range(nw):
        rows = rows_of[k]
        for n, chip in enumerate(_other_chips(x, y)):
            got = refs[k].at[2 * chip[0] + chip[1], pl.ds(c * rows, rows)]
            theirs = refs[k].at[2 * chip[0] + chip[1], pl.ds((1 - c) * rows, rows)]
            out.append(pltpu.make_async_remote_copy(
                src_ref=got, dst_ref=got, send_sem=send.at[k * 3 + n], recv_sem=recv.at[k * 3 + n],
                device_id=(x, y, 1 - c), device_id_type=MESH))
            out.append(pltpu.make_async_remote_copy(
                src_ref=theirs, dst_ref=theirs, send_sem=send.at[k * 3 + n], recv_sem=recv.at[k * 3 + n],
                device_id=(x, y, 1 - c), device_id_type=MESH))
    return out[0::2], out[1::2]


def _forward_start(lands):
    nw = len(lands)
    rows_of = [a.shape[1] // 2 for a in lands]

    def body(*refs):
        send, recv, token = refs[2 * nw:2 * nw + 3]
        mine, _ = _forward_copies(refs[:nw], nw, rows_of, send, recv)
        for cp in mine:
            cp.start()
        token[...] = jnp.zeros_like(token)

    res = pl.pallas_call(
        body, name="forward_start",
        out_shape=tuple(pltpu.HBM(a.shape, a.dtype) for a in lands) + (
            pltpu.SemaphoreType.DMA((3 * nw,)), pltpu.SemaphoreType.DMA((3 * nw,)), TOKEN),
        in_specs=[_HBM] * nw, out_specs=tuple([_HBM] * nw) + (_SEM, _SEM, pl.BlockSpec(memory_space=pltpu.VMEM)),
        input_output_aliases={k: k for k in range(nw)},
        compiler_params=_cp(has_side_effects=_EFFECT),
    )(*lands)
    return res[:nw], res[nw], res[nw + 1], res[nw + 2]


def _forward_wait(lands, send, recv, after):
    nw = len(lands)
    rows_of = [a.shape[1] // 2 for a in lands]

    def body(*refs):
        mine, theirs = _forward_copies(refs[:nw], nw, rows_of, refs[nw], refs[nw + 1])
        for cp in mine:
            cp.wait_send()
        for cp in theirs:
            cp.wait_recv()

    return pl.pallas_call(
        body, name="forward_wait", out_shape=tuple(pltpu.HBM(a.shape, a.dtype) for a in lands),
        in_specs=[_HBM] * nw + [_SEM, _SEM, _ANY], out_specs=tuple([_HBM] * nw),
        input_output_aliases={k: k for k in range(nw)},
        compiler_params=_cp(has_side_effects=_EFFECT),
    )(*lands, send, recv, after)


def _after(tokens):
    tokens = [t for t in tokens if t is not None]
    return tokens, [_ANY] * len(tokens)


def _fwd_inproj(x, g1, w_in, cos_t, sin_t, tm, after=()):
    t_tok = x.shape[0]
    seq = cos_t.shape[0]
    per_seq = seq // tm
    tokens, token_specs = _after(after)

    def body(x_ref, g_ref, w_ref, cos_ref, sin_ref, *rest):
        q_ref, k_ref, v_ref, u_ref = rest[len(tokens):]
        xv = x_ref[...]
        r = lax.rsqrt(jnp.mean(xv * xv, axis=-1, keepdims=True) + EPS)
        h = (xv * r) * g_ref[...]
        proj = _mm_nt(h, w_ref[...])
        cos, sin = cos_ref[...], sin_ref[...]
        for cidx in range((ATTN_W + KV_W) // LANES):
            xc = proj[:, cidx * LANES:(cidx + 1) * LANES]
            rot = xc * cos + _swap_halves(xc) * sin
            if cidx < ATTN_W // LANES:
                q_ref[:, cidx * LANES:(cidx + 1) * LANES] = (rot * (HEAD_DIM ** -0.5)).astype(q_ref.dtype)
            else:
                for g in range(N_KV_HEADS):
                    k_ref[g] = rot[:, g * HEAD_DIM:(g + 1) * HEAD_DIM].astype(k_ref.dtype)
        for g in range(N_KV_HEADS):
            c0 = ATTN_W + KV_W + g * HEAD_DIM
            v_ref[g] = proj[:, c0:c0 + HEAD_DIM].astype(v_ref.dtype)
        u_ref[...] = proj[:, ATTN_W + 2 * KV_W:]

    row = lambda w: pl.BlockSpec((tm, w), lambda i: (i, 0))
    kv = pl.BlockSpec((N_KV_HEADS, tm, HEAD_DIM), lambda i: (0, i, 0))
    tab = pl.BlockSpec((tm, LANES), lambda i: (i % per_seq, 0))
    kv_shape = jax.ShapeDtypeStruct((N_KV_HEADS, t_tok, HEAD_DIM), MXU_DTYPE)
    return pl.pallas_call(
        body, name="fwd_inproj", grid=(t_tok // tm,),
        in_specs=[row(D_MODEL), _const((1, D_MODEL)), _resident((IN_W, D_MODEL)), tab, tab] + token_specs,
        out_specs=(row(ATTN_W), kv, kv, row(POOL_W)),
        out_shape=(jax.ShapeDtypeStruct((t_tok, ATTN_W), MXU_DTYPE), kv_shape, kv_shape,
                   jax.ShapeDtypeStruct((t_tok, POOL_W), F32)),
        compiler_params=_cp(dimension_semantics=("parallel",)),
    )(x, g1, w_in, cos_t, sin_t, *tokens)


MASKED = -1e30


def _attn_bias():
    b = lax.broadcasted_iota(jnp.int32, (2 * BLK, BLK), 0)
    a = lax.broadcasted_iota(jnp.int32, (2 * BLK, BLK), 1)
    own = (b >= BLK) & (b - BLK <= a)
    prev = (b < BLK) & (b > a)
    return jnp.stack([jnp.where(own, 0.0, MASKED), jnp.where(own | prev, 0.0, MASKED)]).astype(F32)


def _sink_rows(sinks):
    return jnp.repeat(sinks.reshape(N_KV_HEADS, Q_PER_KV), BLK, axis=1).reshape(N_KV_HEADS, 1, Q_PER_KV * BLK)


def _stack_heads(ref, r0, g):
    return jnp.concatenate(
        [ref[pl.ds(r0, BLK), (Q_PER_KV * g + h) * HEAD_DIM:(Q_PER_KV * g + h + 1) * HEAD_DIM] for h in range(Q_PER_KV)],
        axis=0)


def _kv_window(ref, p0, r0, g):
    return jnp.concatenate([ref[g, pl.ds(p0, BLK), :], ref[g, pl.ds(r0, BLK), :]], axis=0)


def _scores_t(k_ref, q_ref, bias, p0, r0, g):
    kk = _kv_window(k_ref, p0, r0, g)
    qs = _stack_heads(q_ref, r0, g)
    st = _mm_nt(kk, qs) + jnp.concatenate([bias] * Q_PER_KV, axis=1)
    return st, kk, qs


def _head_rows(ref, r0, g):
    return jnp.concatenate([ref[pl.ds(Q_PER_KV * g + h, 1), pl.ds(r0, BLK)] for h in range(Q_PER_KV)], axis=1)


def _pairs_to_rows(xt):
    out = []
    for t in range(Q_PER_KV // 2):
        pair = jnp.concatenate([xt[:, (2 * t) * BLK:(2 * t + 1) * BLK], xt[:, (2 * t + 1) * BLK:(2 * t + 2) * BLK]], axis=0)
        out.append(pair.T)
    return out


def _shift_rows(x, k, seq):
    row = lax.broadcasted_iota(jnp.int32, x.shape, 0)
    if k > 0:
        return jnp.where(row >= k, pltpu.roll(x, k, 0), 0.0)
    return jnp.where(row < seq + k, pltpu.roll(x, seq + k, 0), 0.0)


def _window_sum(x, w, seq, forward):
    s, k = x, 1
    while k < w:
        s = s + _shift_rows(s, -k if forward else k, seq)
        k *= 2
    return s


def _inv_count(seq, w):
    pos = lax.broadcasted_iota(jnp.int32, (seq, 1), 0)
    return 1.0 / jnp.minimum(pos + 1, w).astype(F32)


def _fwd_attn(q, k, v, sink_rows, bias, seq):
    t_tok = q.shape[0]
    nblk = seq // BLK

    def body(q_ref, k_ref, v_ref, sink_ref, bias_ref, o_ref, lse_ref):
        def blk(i, carry):
            r0 = pl.multiple_of(i * BLK, BLK)
            p0 = pl.multiple_of(jnp.maximum(i - 1, 0) * BLK, BLK)
            bias = bias_ref[jnp.minimum(i, 1)]
            for g in range(N_KV_HEADS):
                st, _, _ = _scores_t(k_ref, q_ref, bias, p0, r0, g)
                sink = sink_ref[g]
                m = jnp.maximum(jnp.max(st, axis=0, keepdims=True), sink)
                p = jnp.exp(st - m).astype(MXU_DTYPE)
                vv = _kv_window(v_ref, p0, r0, g)
                ot = _mm_tn(jnp.concatenate([vv, jnp.ones_like(vv)], axis=1), p)
                den = ot[HEAD_DIM:HEAD_DIM + 1] + jnp.exp(sink - m)
                lse = m + jnp.log(den)
                for h in range(Q_PER_KV):
                    lse_ref[pl.ds(Q_PER_KV * g + h, 1), pl.ds(r0, BLK)] = lse[:, h * BLK:(h + 1) * BLK]
                ot = ot[:HEAD_DIM] * (1.0 / den)
                for t, tile in enumerate(_pairs_to_rows(ot)):
                    c0 = (Q_PER_KV * g + 2 * t) * HEAD_DIM
                    o_ref[pl.ds(r0, BLK), c0:c0 + LANES] = tile.astype(o_ref.dtype)
            return carry

        lax.fori_loop(0, nblk, blk, 0, unroll=4)

    row = lambda w: pl.BlockSpec((seq, w), lambda i: (i, 0))
    kv = pl.BlockSpec((N_KV_HEADS, seq, HEAD_DIM), lambda i: (0, i, 0))
    return pl.pallas_call(
        body, name="fwd_attn", grid=(t_tok // seq,),
        in_specs=[row(ATTN_W), kv, kv, _const((N_KV_HEADS, 1, Q_PER_KV * BLK)), _const((2, 2 * BLK, BLK))],
        out_specs=(row(ATTN_W), pl.BlockSpec((N_Q_HEADS, seq), lambda i: (0, i))),
        out_shape=(jax.ShapeDtypeStruct((t_tok, ATTN_W), MXU_DTYPE), jax.ShapeDtypeStruct((N_Q_HEADS, t_tok), F32)),
        compiler_params=_cp(dimension_semantics=("parallel",)),
    )(q, k, v, sink_rows, bias)


def _fwd_pool(u, w_pool, pool_scale, seq, after=()):
    t_tok = u.shape[0]
    tokens, token_specs = _after(after)

    def body(u_ref, wp_ref, sc_ref, *rest):
        o_ref = rest[-1]
        for gi, w in enumerate(POOL_WINDOWS):
            sl = slice(gi * POOL_G, (gi + 1) * POOL_G)
            ug = u_ref[:, sl]
            d = _window_sum(ug, w, seq, False) * _inv_count(seq, w) - ug
            o_ref[:, sl] = (_mm(d, wp_ref[gi]) * sc_ref[:, sl]).astype(o_ref.dtype)

    row = pl.BlockSpec((seq, POOL_W), lambda i: (i, 0))
    return pl.pallas_call(
        body, name="fwd_pool", grid=(t_tok // seq,),
        in_specs=[row, _const((N_POOL, POOL_G, POOL_G)), _const((1, POOL_W))] + token_specs, out_specs=row,
        out_shape=jax.ShapeDtypeStruct((t_tok, POOL_W), MXU_DTYPE),
        compiler_params=_cp(dimension_semantics=("parallel",)),
    )(u, w_pool, pool_scale, *tokens)


def _rms_bwd(dy_g, xn, r):
    return r * (dy_g - xn * jnp.mean(dy_g * xn, axis=-1, keepdims=True))


def _fwd_mlp_loss(x, attn, pool, target, w_out, w_up4, w_down4, g2, gf, tm):
    t_tok = x.shape[0]

    def body(x_ref, attn_ref, pool_ref, tgt_ref, wo_ref, wu_ref, wd_ref, g2_ref, gf_ref,
             x1_ref, h2_ref, a_ref, dx2_ref, dx2b_ref, loss_ref, dgf_ref, x2_ref):
        i = pl.program_id(0)

        @pl.when(i == 0)
        def _():
            loss_ref[...] = jnp.zeros_like(loss_ref)
            dgf_ref[...] = jnp.zeros_like(dgf_ref)
            x2_ref[...] = jnp.zeros_like(x2_ref)

        def tail(valid):
            x2 = x2_ref[...]
            r3 = lax.rsqrt(jnp.mean(x2 * x2, axis=-1, keepdims=True) + EPS)
            xn = x2 * r3
            gf_v = gf_ref[...]
            err = xn * gf_v - tgt_ref[...]
            part = jnp.sum(err * err) * (0.5 / D_MODEL)
            first = (lax.broadcasted_iota(jnp.int32, loss_ref.shape, 0) == 0) & (lax.broadcasted_iota(jnp.int32, loss_ref.shape, 1) == 0)
            keep = jnp.where(valid, 1.0, 0.0)
            loss_ref[...] += jnp.where(first, part * keep, 0.0)
            dy = err * (1.0 / D_MODEL)
            dgf_ref[...] += jnp.sum(dy * xn, axis=0, keepdims=True) * keep
            dx2 = _rms_bwd(dy * gf_v, xn, r3)
            dx2_ref[...] = dx2
            dx2b_ref[...] = dx2.astype(dx2b_ref.dtype)

        @pl.when(i < nsteps)
        def _():
            tail(i > 0)
            x1 = x_ref[...] + (_mm(attn_ref[...], wo_ref[:ATTN_W]) + _mm(pool_ref[...], wo_ref[ATTN_W:]))
            x1_ref[...] = x1
            r2 = lax.rsqrt(jnp.mean(x1 * x1, axis=-1, keepdims=True) + EPS)
            h2 = ((x1 * r2) * g2_ref[...]).astype(MXU_DTYPE)
            h2_ref[...] = h2
            acc = jnp.zeros((tm, D_MODEL), F32)
            for j in range(N_CHIPS):
                a = _mm(h2, wu_ref[j])
                a_ref[:, j * FF_SHARD:(j + 1) * FF_SHARD] = a.astype(a_ref.dtype)
                acc = acc + _mm(jnp.square(jnp.maximum(a, 0.0)), wd_ref[j])
            x2_ref[...] = x1 + acc

        @pl.when(i == nsteps)
        def _():
            tail(True)

    nsteps = t_tok // tm
    row = lambda w: pl.BlockSpec((tm, w), lambda i: (jnp.minimum(i, nsteps - 1), 0))
    lag = lambda w: pl.BlockSpec((tm, w), lambda i: (jnp.maximum(i - 1, 0), 0))
    return pl.pallas_call(
        body, name="fwd_mlp_loss", grid=(nsteps + 1,),
        in_specs=[row(D_MODEL), row(ATTN_W), row(POOL_W), lag(D_MODEL), _resident((D_MODEL, D_MODEL)),
                  _resident((N_CHIPS, D_MODEL, FF_SHARD)), _resident((N_CHIPS, FF_SHARD, D_MODEL)),
                  _const((1, D_MODEL)), _const((1, D_MODEL))],
        out_specs=(row(D_MODEL), row(D_MODEL), row(D_FF), lag(D_MODEL), lag(D_MODEL), _const((8, LANES)), _const((1, D_MODEL))),
        out_shape=(jax.ShapeDtypeStruct((t_tok, D_MODEL), F32), jax.ShapeDtypeStruct((t_tok, D_MODEL), MXU_DTYPE),
                   jax.ShapeDtypeStruct((t_tok, D_FF), MXU_DTYPE), jax.ShapeDtypeStruct((t_tok, D_MODEL), F32),
                   jax.ShapeDtypeStruct((t_tok, D_MODEL), MXU_DTYPE), jax.ShapeDtypeStruct((8, LANES), F32),
                   jax.ShapeDtypeStruct((1, D_MODEL), F32)),
        scratch_shapes=[pltpu.VMEM((tm, D_MODEL), F32)],
        compiler_params=_cp(dimension_semantics=("arbitrary",)),
    )(x, attn, pool, target, w_out, w_up4, w_down4, g2, gf)


def _bwd_mlp(dx2, dx2b, a, x1, w_up4, w_down4, g2, tm):
    t_tok = dx2.shape[0]

    def body(dx2_ref, dx2b_ref, a_ref, x1_ref, wu_ref, wd_ref, g2_ref, da_ref, dx1_ref, dx1b_ref, dg2_ref):
        @pl.when(pl.program_id(0) == 0)
        def _():
            dg2_ref[...] = jnp.zeros_like(dg2_ref)

        dx2b = dx2b_ref[...]
        dh2 = jnp.zeros((tm, D_MODEL), F32)
        for j in range(N_CHIPS):
            sl = slice(j * FF_SHARD, (j + 1) * FF_SHARD)
            dhid = _mm_nt(dx2b, wd_ref[j])
            da = (dhid * (2.0 * jnp.maximum(a_ref[:, sl].astype(F32), 0.0))).astype(MXU_DTYPE)
            da_ref[:, sl] = da
            dh2 = dh2 + _mm_nt(da, wu_ref[j])
        x1 = x1_ref[...]
        r2 = lax.rsqrt(jnp.mean(x1 * x1, axis=-1, keepdims=True) + EPS)
        xn = x1 * r2
        dg2_ref[...] += jnp.sum(dh2 * xn, axis=0, keepdims=True)
        dx1 = dx2_ref[...] + _rms_bwd(dh2 * g2_ref[...], xn, r2)
        dx1_ref[...] = dx1
        dx1b_ref[...] = dx1.astype(dx1b_ref.dtype)

    row = lambda w: pl.BlockSpec((tm, w), lambda i: (i, 0))
    return pl.pallas_call(
        body, name="bwd_mlp", grid=(t_tok // tm,),
        in_specs=[row(D_MODEL), row(D_MODEL), row(D_FF), row(D_MODEL),
                  _resident((N_CHIPS, D_MODEL, FF_SHARD)), _resident((N_CHIPS, FF_SHARD, D_MODEL)), _const((1, D_MODEL))],
        out_specs=(row(D_FF), row(D_MODEL), row(D_MODEL), _const((1, D_MODEL))),
        out_shape=(jax.ShapeDtypeStruct((t_tok, D_FF), MXU_DTYPE), jax.ShapeDtypeStruct((t_tok, D_MODEL), F32),
                   jax.ShapeDtypeStruct((t_tok, D_MODEL), MXU_DTYPE), jax.ShapeDtypeStruct((1, D_MODEL), F32)),
        compiler_params=_cp(dimension_semantics=("arbitrary",)),
    )(dx2, dx2b, a, x1, w_up4, w_down4, g2)


def _bwd_mlp_wgrads(h2, da, a, dx2b, tk):
    t_tok = h2.shape[0]

    def body(h2_ref, da_ref, a_ref, dx2b_ref, gup_ref, gdn_ref):
        @pl.when(pl.program_id(1) == 0)
        def _():
            gup_ref[...] = jnp.zeros_like(gup_ref)
            gdn_ref[...] = jnp.zeros_like(gdn_ref)

        gup_ref[0] += _mm_tn(h2_ref[...], da_ref[...])
        hid = jnp.square(jnp.maximum(a_ref[...].astype(F32), 0.0))
        gdn_ref[0] += _mm_tn(hid, dx2b_ref[...])

    tok = pl.BlockSpec((tk, D_MODEL), lambda j, t: (t, 0))
    ffb = pl.BlockSpec((tk, FF_SHARD), lambda j, t: (t, j))
    wblk = pl.BlockSpec((1, D_MODEL, D_MODEL), lambda j, t: (j, 0, 0))
    return pl.pallas_call(
        body, name="bwd_mlp_wgrads", grid=(N_CHIPS, t_tok // tk),
        in_specs=[tok, ffb, ffb, tok], out_specs=(wblk, wblk),
        out_shape=(jax.ShapeDtypeStruct((N_CHIPS, D_MODEL, FF_SHARD), F32), jax.ShapeDtypeStruct((N_CHIPS, FF_SHARD, D_MODEL), F32)),
        compiler_params=_cp(dimension_semantics=("parallel", "arbitrary")),
    )(h2, da, a, dx2b)


def _head_selector():
    ch = lax.broadcasted_iota(jnp.int32, (ATTN_W, LANES), 0)
    col = lax.broadcasted_iota(jnp.int32, (ATTN_W, LANES), 1)
    return (ch // HEAD_DIM == col).astype(MXU_DTYPE)


def _bwd_outproj(dx1b, attn, pool, w_out, head_sel, tm, after=()):
    t_tok = dx1b.shape[0]
    tokens, token_specs = _after(after)

    def body(dx_ref, attn_ref, pool_ref, wo_ref, sel_ref, *rest):
        dattn_ref, dpool_ref, delta_ref, gwo_ref = rest[len(tokens):]

        @pl.when(pl.program_id(0) == 0)
        def _():
            gwo_ref[...] = jnp.zeros_like(gwo_ref)

        dx = dx_ref[...]
        attn = attn_ref[...]
        dattn = _mm_nt(dx, wo_ref[:ATTN_W])
        dattn_ref[...] = dattn.astype(dattn_ref.dtype)
        dpool_ref[...] = _mm_nt(dx, wo_ref[ATTN_W:])
        prod = dattn * attn.astype(F32)
        hi = prod.astype(MXU_DTYPE)
        lo = prod - hi.astype(F32)
        delta = _mm(hi, sel_ref[...]) + _mm(lo, sel_ref[...])
        delta_ref[...] = delta.T[:N_Q_HEADS]
        gwo_ref[:ATTN_W] += _mm_tn(attn, dx)
        gwo_ref[ATTN_W:] += _mm_tn(pool_ref[...], dx)

    row = lambda w: pl.BlockSpec((tm, w), lambda i: (i, 0))
    return pl.pallas_call(
        body, name="bwd_outproj", grid=(t_tok // tm,),
        in_specs=[row(D_MODEL), row(ATTN_W), row(POOL_W), _resident((D_MODEL, D_MODEL)), _const((ATTN_W, LANES))] + token_specs,
        out_specs=(row(ATTN_W), row(POOL_W), pl.BlockSpec((N_Q_HEADS, tm), lambda i: (0, i)), _const((D_MODEL, D_MODEL))),
        out_shape=(jax.ShapeDtypeStruct((t_tok, ATTN_W), MXU_DTYPE), jax.ShapeDtypeStruct((t_tok, POOL_W), F32),
                   jax.ShapeDtypeStruct((N_Q_HEADS, t_tok), F32), jax.ShapeDtypeStruct((D_MODEL, D_MODEL), F32)),
        compiler_params=_cp(dimension_semantics=("arbitrary",)),
    )(dx1b, attn, pool, w_out, head_sel, *tokens)


def _bwd_attn(q, k, v, dattn, lse, delta, sink_rows, bias, cos_t, sin_t, seq, after=()):
    t_tok = q.shape[0]
    nblk = seq // BLK
    qkv_w = ATTN_W + 2 * KV_W
    tokens, token_specs = _after(after)

    def unrope(d, cos, sin):
        return d * cos - _swap_halves(d) * sin

    def body(q_ref, k_ref, v_ref, do_ref, lse_ref, delta_ref, sink_ref, bias_ref, cos_ref, sin_ref, *rest):
        dqkv_ref, dsink_ref, dk_acc, dv_acc = rest[len(tokens):]

        @pl.when(pl.program_id(0) == 0)
        def _():
            dsink_ref[...] = jnp.zeros_like(dsink_ref)

        dk_acc[...] = jnp.zeros_like(dk_acc)
        dv_acc[...] = jnp.zeros_like(dv_acc)

        def blk(i, dsink):
            r0 = pl.multiple_of(i * BLK, BLK)
            p0 = pl.multiple_of(jnp.maximum(i - 1, 0) * BLK, BLK)
            bias = bias_ref[jnp.minimum(i, 1)]
            cos, sin = cos_ref[pl.ds(r0, BLK), :], sin_ref[pl.ds(r0, BLK), :]
            new = []
            for g in range(N_KV_HEADS):
                st, kk, qs = _scores_t(k_ref, q_ref, bias, p0, r0, g)
                lse_g, delta_g = _head_rows(lse_ref, r0, g), _head_rows(delta_ref, r0, g)
                pn = jnp.exp(st - lse_g)
                dos = _stack_heads(do_ref, r0, g)
                dst = pn * (_mm_nt(_kv_window(v_ref, p0, r0, g), dos) - delta_g)
                new.append(dsink[g] - jnp.exp(sink_ref[g] - lse_g) * delta_g)
                dqt = _mm_tn(kk, dst) * (HEAD_DIM ** -0.5)
                for t, tile in enumerate(_pairs_to_rows(dqt)):
                    c0 = (Q_PER_KV * g + 2 * t) * HEAD_DIM
                    dqkv_ref[pl.ds(r0, BLK), c0:c0 + LANES] = unrope(tile, cos, sin).astype(dqkv_ref.dtype)
                dkk = _mm(dst, qs)
                dvv = _mm(pn, dos)
                dk_acc[g, pl.ds(p0, BLK), :] += dkk[:BLK]
                dk_acc[g, pl.ds(r0, BLK), :] += dkk[BLK:]
                dv_acc[g, pl.ds(p0, BLK), :] += dvv[:BLK]
                dv_acc[g, pl.ds(r0, BLK), :] += dvv[BLK:]
            return tuple(new)

        zero = jnp.zeros((1, Q_PER_KV * BLK), F32)
        dsink = lax.fori_loop(0, nblk // 2, lambda i2, acc: blk(2 * i2 + 1, blk(2 * i2, acc)), (zero,) * N_KV_HEADS)
        lane = lax.broadcasted_iota(jnp.int32, dsink_ref.shape, 1)
        row = lax.broadcasted_iota(jnp.int32, dsink_ref.shape, 0)
        tile = jnp.zeros(dsink_ref.shape, F32)
        for g in range(N_KV_HEADS):
            for h in range(Q_PER_KV):
                tot = jnp.sum(dsink[g][:, h * BLK:(h + 1) * BLK])
                tile = tile + jnp.where((row == 0) & (lane == Q_PER_KV * g + h), tot, 0.0)
        dsink_ref[...] += tile
        dk = jnp.concatenate([dk_acc[g] for g in range(N_KV_HEADS)], axis=1)
        dqkv_ref[:, ATTN_W:ATTN_W + KV_W] = unrope(dk, cos_ref[...], sin_ref[...]).astype(dqkv_ref.dtype)
        dqkv_ref[:, ATTN_W + KV_W:] = jnp.concatenate([dv_acc[g] for g in range(N_KV_HEADS)], axis=1).astype(dqkv_ref.dtype)

    row = lambda w: pl.BlockSpec((seq, w), lambda i: (i, 0))
    kv = pl.BlockSpec((N_KV_HEADS, seq, HEAD_DIM), lambda i: (0, i, 0))
    per_head = pl.BlockSpec((N_Q_HEADS, seq), lambda i: (0, i))
    return pl.pallas_call(
        body, name="bwd_attn", grid=(t_tok // seq,),
        in_specs=[row(ATTN_W), kv, kv, row(ATTN_W), per_head, per_head, _const((N_KV_HEADS, 1, Q_PER_KV * BLK)),
                  _const((2, 2 * BLK, BLK)), _resident((seq, LANES)), _resident((seq, LANES))] + token_specs,
        out_specs=(row(qkv_w), _const((8, LANES))),
        out_shape=(jax.ShapeDtypeStruct((t_tok, qkv_w), MXU_DTYPE), jax.ShapeDtypeStruct((8, LANES), F32)),
        scratch_shapes=[pltpu.VMEM((N_KV_HEADS, seq, HEAD_DIM), F32), pltpu.VMEM((N_KV_HEADS, seq, HEAD_DIM), F32)],
        compiler_params=_cp(dimension_semantics=("arbitrary",)),
    )(q, k, v, dattn, lse, delta, sink_rows, bias, cos_t, sin_t, *tokens)


def _bwd_pool(u, dpool, w_pool, pool_scale, seq):
    t_tok = u.shape[0]

    def body(u_ref, dp_ref, wp_ref, sc_ref, du_ref, dwp_ref, dsc_ref):
        @pl.when(pl.program_id(0) == 0)
        def _():
            dwp_ref[...] = jnp.zeros_like(dwp_ref)
            dsc_ref[...] = jnp.zeros_like(dsc_ref)

        for gi, w in enumerate(POOL_WINDOWS):
            sl = slice(gi * POOL_G, (gi + 1) * POOL_G)
            ug = u_ref[:, sl]
            inv = _inv_count(seq, w)
            d = (_window_sum(ug, w, seq, False) * inv - ug).astype(MXU_DTYPE)
            y = _mm(d, wp_ref[gi])
            dpool = dp_ref[:, sl]
            dsc_ref[:, sl] += jnp.sum(y * dpool, axis=0, keepdims=True)
            dy = (dpool * sc_ref[:, sl]).astype(MXU_DTYPE)
            dwp_ref[gi] += _mm_tn(d, dy)
            dd = _mm_nt(dy, wp_ref[gi])
            du_ref[:, sl] = (_window_sum(dd * inv, w, seq, True) - dd).astype(du_ref.dtype)

    row = pl.BlockSpec((seq, POOL_W), lambda i: (i, 0))
    return pl.pallas_call(
        body, name="bwd_pool", grid=(t_tok // seq,),
        in_specs=[row, row, _const((N_POOL, POOL_G, POOL_G)), _const((1, POOL_W))],
        out_specs=(row, _const((N_POOL, POOL_G, POOL_G)), _const((1, POOL_W))),
        out_shape=(jax.ShapeDtypeStruct((t_tok, POOL_W), MXU_DTYPE), jax.ShapeDtypeStruct((N_POOL, POOL_G, POOL_G), F32),
                   jax.ShapeDtypeStruct((1, POOL_W), F32)),
        compiler_params=_cp(dimension_semantics=("arbitrary",)),
    )(u, dpool, w_pool, pool_scale)


def _bwd_inproj(dqkv, du, x, dx1, w_in, g1, tm):
    t_tok = x.shape[0]
    nsteps = t_tok // tm

    qkv_w = ATTN_W + 2 * KV_W

    def body(dqkv_ref, du_ref, x_ref, dx1_ref, w_ref, g_ref, gx_ref, gw_ref, dg_ref):
        @pl.when(pl.program_id(0) == 0)
        def _():
            gw_ref[...] = jnp.zeros_like(gw_ref)
            dg_ref[...] = jnp.zeros_like(dg_ref)

        dqkv, du = dqkv_ref[...], du_ref[...]
        xv = x_ref[...]
        r = lax.rsqrt(jnp.mean(xv * xv, axis=-1, keepdims=True) + EPS)
        xn = xv * r
        g = g_ref[...]
        dh = _mm(dqkv, w_ref[:qkv_w]) + _mm(du, w_ref[qkv_w:])
        dg_ref[...] += jnp.sum(dh * xn, axis=0, keepdims=True)
        gx_ref[...] = dx1_ref[...] + _rms_bwd(dh * g, xn, r)
        h = (xn * g).astype(MXU_DTYPE)
        gw_ref[:qkv_w] += _mm_tn(dqkv, h)
        gw_ref[qkv_w:] += _mm_tn(du, h)

    row = lambda w: pl.BlockSpec((tm, w), lambda i: (i, 0))
    return pl.pallas_call(
        body, name="bwd_inproj", grid=(nsteps,),
        in_specs=[row(qkv_w), row(POOL_W), row(D_MODEL), row(D_MODEL), _resident((IN_W, D_MODEL)), _const((1, D_MODEL))],
        out_specs=(row(D_MODEL), _const((IN_W, D_MODEL)), _const((1, D_MODEL))),
        out_shape=(jax.ShapeDtypeStruct((t_tok, D_MODEL), F32), jax.ShapeDtypeStruct((IN_W, D_MODEL), F32),
                   jax.ShapeDtypeStruct((1, D_MODEL), F32)),
        compiler_params=_cp(dimension_semantics=("arbitrary",)),
    )(dqkv, du, x, dx1, w_in, g1)


class _NoComm:
    def start(self):
        return ()

    def attn_done(self, attn):
        return ()

    def rest_of_weights(self, pool):
        raise NotImplementedError

    def mlp_grads_ready(self, gw_up4, gw_down4):
        return ()

    def outproj_done(self, gw_out):
        return ()


def _local_step(x, target, w_in, comm, g1, sinks, w_pool, pool_scale, g2, gf, seq):
    tm = min(512, seq)
    tm_mlp = min(256, seq)
    cos_t, sin_t = _rope_tables(seq)
    sink_rows, bias = _sink_rows(sinks), _attn_bias()
    q, k, v, u = _fwd_inproj(x, g1, w_in, cos_t, sin_t, tm, comm.start())
    attn, lse = _fwd_attn(q, k, v, sink_rows, bias, seq)
    pool = _fwd_pool(u, w_pool, pool_scale, seq, comm.attn_done(attn))
    w_out, w_up4, w_down4 = comm.rest_of_weights(pool)
    x1, h2, a, dx2, dx2b, loss, dgf = _fwd_mlp_loss(x, attn, pool, target, w_out, w_up4, w_down4, g2, gf, tm_mlp)
    da, dx1, dx1b, dg2 = _bwd_mlp(dx2, dx2b, a, x1, w_up4, w_down4, g2, tm_mlp)
    gw_up4, gw_down4 = _bwd_mlp_wgrads(h2, da, a, dx2b, min(1024, x.shape[0]))
    dattn, dpool, delta, gw_out = _bwd_outproj(dx1b, attn, pool, w_out, _head_selector(), tm,
                                               comm.mlp_grads_ready(gw_up4, gw_down4))
    dqkv, dsinks = _bwd_attn(q, k, v, dattn, lse, delta, sink_rows, bias, cos_t, sin_t, seq, comm.outproj_done(gw_out))
    du, dwp, dsc = _bwd_pool(u, dpool, w_pool, pool_scale, seq)
    gx, gw_in_t, dg1 = _bwd_inproj(dqkv, du, x, dx1, w_in, g1, tm)
    big = (gw_in_t.reshape(N_CHIPS, IN_SHARD, D_MODEL), gw_out.reshape(N_CHIPS, OUT_SHARD, D_MODEL), gw_up4, gw_down4)
    small = (dg1, dsinks, dwp.reshape(N_POOL * POOL_G, POOL_G), dsc, dg2, dgf, loss)
    return gx, big, small


def _sibling_exchange(arrs, pick_half, name):
    n = len(arrs)

    def out_shape(a):
        if pick_half:
            return (a.shape[0], a.shape[1] // 2) + a.shape[2:]
        return a.shape

    def body(*refs):
        srcs, dsts = refs[:n], refs[n:2 * n]
        send, recv = refs[2 * n:]
        x, y, c = lax.axis_index("x"), lax.axis_index("y"), lax.axis_index("c")
        cps = []
        for i in range(n):
            src = srcs[i]
            if pick_half:
                h = src.shape[1] // 2
                src = src.at[:, pl.ds((1 - c) * h, h)]
            cp = pltpu.make_async_remote_copy(src_ref=src, dst_ref=dsts[i], send_sem=send.at[i], recv_sem=recv.at[i],
                                              device_id=(x, y, 1 - c), device_id_type=MESH)
            cp.start()
            cps.append(cp)
        for cp in cps:
            cp.wait()

    hbm = pl.BlockSpec(memory_space=pltpu.HBM)
    return pl.pallas_call(
        body, name=name, out_shape=tuple(jax.ShapeDtypeStruct(out_shape(a), a.dtype) for a in arrs),
        in_specs=[hbm] * n, out_specs=tuple([hbm] * n),
        scratch_shapes=[pltpu.SemaphoreType.DMA((n,)), pltpu.SemaphoreType.DMA((n,))],
        compiler_params=_cp(),
    )(*arrs)


def _sibling_copies(srcs, lands, send, recv):
    x, y, c = lax.axis_index("x"), lax.axis_index("y"), lax.axis_index("c")
    cps = []
    for i in range(len(srcs)):
        h = srcs[i].shape[1] // 2
        cps.append(pltpu.make_async_remote_copy(
            src_ref=srcs[i].at[:, pl.ds((1 - c) * h, h)], dst_ref=lands[i], send_sem=send.at[i], recv_sem=recv.at[i],
            device_id=(x, y, 1 - c), device_id_type=MESH))
    return cps


def _sibling_start(arrs, name):
    n = len(arrs)

    def body(*refs):
        send, recv, token = refs[3 * n:3 * n + 3]
        for cp in _sibling_copies(refs[:n], refs[2 * n:3 * n], send, recv):
            cp.start()
        token[...] = jnp.zeros_like(token)

    half = lambda a: (a.shape[0], a.shape[1] // 2) + a.shape[2:]
    res = pl.pallas_call(
        body, name=name + "_start",
        out_shape=tuple(pltpu.HBM(a.shape, a.dtype) for a in arrs) + tuple(pltpu.HBM(half(a), a.dtype) for a in arrs) + (
            pltpu.SemaphoreType.DMA((n,)), pltpu.SemaphoreType.DMA((n,)), TOKEN),
        in_specs=[_HBM] * n, out_specs=tuple([_HBM] * (2 * n)) + (_SEM, _SEM, pl.BlockSpec(memory_space=pltpu.VMEM)),
        input_output_aliases={i: i for i in range(n)},
        compiler_params=_cp(has_side_effects=_EFFECT),
    )(*[pltpu.with_memory_space_constraint(a, pltpu.HBM) for a in arrs])
    return res[:n], res[n:2 * n], res[2 * n], res[2 * n + 1], res[2 * n + 2]


def _sibling_wait(arrs, lands, send, recv, after, name):
    n = len(arrs)

    def body(*refs):
        for cp in _sibling_copies(refs[:n], refs[n:2 * n], refs[2 * n], refs[2 * n + 1]):
            cp.wait_send()
            cp.wait_recv()

    res = pl.pallas_call(
        body, name=name + "_wait", out_shape=tuple(pltpu.HBM(a.shape, a.dtype) for a in list(arrs) + list(lands)),
        in_specs=[_HBM] * (2 * n) + [_SEM, _SEM, _ANY], out_specs=tuple([_HBM] * (2 * n)),
        input_output_aliases={i: i for i in range(2 * n)},
        compiler_params=_cp(has_side_effects=_EFFECT),
    )(*arrs, *lands, send, recv, after)
    return res[:n], res[n:]


def _row_block(rows):
    for cand in (256, 128, 64, 32, 16, 8):
        if rows % cand == 0:
            return cand
    raise ValueError(rows)


def _chip_partial(g4, r4, c_arr, name):
    _, rows, cols = r4.shape
    rb = _row_block(rows)
    nb = rows // rb

    def body(c_ref, g_ref, r_ref, o_ref):
        o_ref[...] = (g_ref[...] + r_ref[...]).astype(o_ref.dtype)

    return pl.pallas_call(
        body, name=name,
        grid_spec=pltpu.PrefetchScalarGridSpec(
            num_scalar_prefetch=1, grid=(N_CHIPS, nb),
            in_specs=[pl.BlockSpec((1, rb, cols), lambda s, i, c: (s, c[0] * nb + i, 0)),
                      pl.BlockSpec((1, rb, cols), lambda s, i, c: (s, i, 0))],
            out_specs=pl.BlockSpec((1, rb, cols), lambda s, i, c: (s, i, 0))),
        out_shape=jax.ShapeDtypeStruct(r4.shape, BF16),
        compiler_params=_cp(dimension_semantics=("parallel", "parallel")),
    )(c_arr, g4, r4)


def _send_partials(parts):
    n = len(parts)

    def body(*refs):
        srcs, dsts = refs[:n], refs[n:2 * n]
        send, recv = refs[2 * n:]
        x, y, c = lax.axis_index("x"), lax.axis_index("y"), lax.axis_index("c")
        chips = [(1 - x, y), (x, 1 - y), (1 - x, 1 - y)]
        cps = []
        for i in range(n):
            for m, chip in enumerate(chips):
                cp = pltpu.make_async_remote_copy(
                    src_ref=srcs[i].at[2 * chip[0] + chip[1]], dst_ref=dsts[i].at[m],
                    send_sem=send.at[3 * i + m], recv_sem=recv.at[3 * i + m],
                    device_id=(chip[0], chip[1], c), device_id_type=MESH)
                cp.start()
                cps.append(cp)
        for cp in cps:
            cp.wait()

    hbm = pl.BlockSpec(memory_space=pltpu.HBM)
    return pl.pallas_call(
        body, name="send_partials",
        out_shape=tuple(jax.ShapeDtypeStruct((3,) + p.shape[1:], p.dtype) for p in parts),
        in_specs=[hbm] * n, out_specs=tuple([hbm] * n),
        scratch_shapes=[pltpu.SemaphoreType.DMA((3 * n,)), pltpu.SemaphoreType.DMA((3 * n,))],
        compiler_params=_cp(),
    )(*parts)


def _send_start(parts):
    n = len(parts)

    def body(*refs):
        srcs, lands = refs[:n], refs[2 * n:3 * n]
        send, recv, token = refs[3 * n:3 * n + 3]
        x, y, c = lax.axis_index("x"), lax.axis_index("y"), lax.axis_index("c")
        for i in range(n):
            for m, chip in enumerate(_other_chips(x, y)):
                pltpu.make_async_remote_copy(
                    src_ref=srcs[i].at[2 * chip[0] + chip[1]], dst_ref=lands[i].at[m],
                    send_sem=send.at[3 * i + m], recv_sem=recv.at[3 * i + m],
                    device_id=(chip[0], chip[1], c), device_id_type=MESH).start()
        token[...] = jnp.zeros_like(token)

    res = pl.pallas_call(
        body, name="send_start",
        out_shape=tuple(pltpu.HBM(p.shape, p.dtype) for p in parts) + tuple(pltpu.HBM((3,) + p.shape[1:], p.dtype) for p in parts) + (
            pltpu.SemaphoreType.DMA((3 * n,)), pltpu.SemaphoreType.DMA((3 * n,)), TOKEN),
        in_specs=[_HBM] * n, out_specs=tuple([_HBM] * (2 * n)) + (_SEM, _SEM, pl.BlockSpec(memory_space=pltpu.VMEM)),
        input_output_aliases={i: i for i in range(n)},
        compiler_params=_cp(has_side_effects=_EFFECT),
    )(*[pltpu.with_memory_space_constraint(p, pltpu.HBM) for p in parts])
    return res[:n], res[n:2 * n], res[2 * n], res[2 * n + 1], res[2 * n + 2]


def _send_wait(parts, lands, send, recv, after):
    n = len(parts)

    def body(*refs):
        srcs, ins = refs[:n], refs[n:2 * n]
        send_ref, recv_ref = refs[2 * n], refs[2 * n + 1]
        x, y, c = lax.axis_index("x"), lax.axis_index("y"), lax.axis_index("c")
        for i in range(n):
            for m, chip in enumerate(_other_chips(x, y)):
                cp = pltpu.make_async_remote_copy(
                    src_ref=srcs[i].at[2 * chip[0] + chip[1]], dst_ref=ins[i].at[m],
                    send_sem=send_ref.at[3 * i + m], recv_sem=recv_ref.at[3 * i + m],
                    device_id=(chip[0], chip[1], c), device_id_type=MESH)
                cp.wait_send()
                cp.wait_recv()

    res = pl.pallas_call(
        body, name="send_wait", out_shape=tuple(pltpu.HBM(a.shape, a.dtype) for a in list(parts) + list(lands)),
        in_specs=[_HBM] * (2 * n) + [_SEM, _SEM, _ANY], out_specs=tuple([_HBM] * (2 * n)),
        input_output_aliases={i: i for i in range(2 * n)},
        compiler_params=_cp(has_side_effects=_EFFECT),
    )(*parts, *lands, send, recv, after)
    return res[n:]


def _final_half(g4, r4, got3, jc_arr, name):
    _, rows, cols = r4.shape
    rb = _row_block(rows)
    nb = rows // rb

    def body(jc_ref, g_ref, r_ref, p_ref, o_ref):
        own = g_ref[0] + r_ref[0]
        o_ref[...] = ((own + p_ref[0].astype(F32)) + p_ref[1].astype(F32)) + p_ref[2].astype(F32)

    return pl.pallas_call(
        body, name=name,
        grid_spec=pltpu.PrefetchScalarGridSpec(
            num_scalar_prefetch=1, grid=(nb,),
            in_specs=[pl.BlockSpec((1, rb, cols), lambda i, jc: (jc[0], jc[1] * nb + i, 0)),
                      pl.BlockSpec((1, rb, cols), lambda i, jc: (jc[0], i, 0)),
                      pl.BlockSpec((3, rb, cols), lambda i, jc: (0, i, 0))],
            out_specs=pl.BlockSpec((rb, cols), lambda i, jc: (i, 0))),
        out_shape=jax.ShapeDtypeStruct((rows, cols), F32),
        compiler_params=_cp(dimension_semantics=("parallel",)),
    )(jc_arr, g4, r4, got3)


def _adamw_math(w, g, m, v):
    m2 = ADAM_B1 * m + (1.0 - ADAM_B1) * g
    v2 = ADAM_B2 * v + (1.0 - ADAM_B2) * (g * g)
    m_hat = m2 / (1.0 - ADAM_B1 ** ADAM_STEP)
    v_hat = v2 / (1.0 - ADAM_B2 ** ADAM_STEP)
    delta = -ADAM_LR * (m_hat / (jnp.sqrt(v_hat) + ADAM_EPS) + ADAM_WD * w)
    return delta, m2, v2


def _adamw_shard(mine, other, w, m, v, c_arr, name):
    rows, cols = w.shape
    half = rows // 2
    rb = _row_block(half)
    nb = half // rb

    def body(c_ref, a_ref, b_ref, w_ref, m_ref, v_ref, g_ref, d_ref, m2_ref, v2_ref):
        g = jnp.where(pl.program_id(0) == c_ref[0], a_ref[...], b_ref[...])
        delta, m2, v2 = _adamw_math(w_ref[...], g, m_ref[...], v_ref[...])
        g_ref[...] = g
        d_ref[...] = delta
        m2_ref[...] = m2
        v2_ref[...] = v2

    hb = pl.BlockSpec((rb, cols), lambda h, i, c: (i, 0))
    fb = pl.BlockSpec((rb, cols), lambda h, i, c: (h * nb + i, 0))
    shp = jax.ShapeDtypeStruct((rows, cols), F32)
    return pl.pallas_call(
        body, name=name,
        grid_spec=pltpu.PrefetchScalarGridSpec(num_scalar_prefetch=1, grid=(2, nb), in_specs=[hb, hb, fb, fb, fb],
                                               out_specs=(fb, fb, fb, fb)),
        out_shape=(shp, shp, shp, shp),
        compiler_params=_cp(dimension_semantics=("parallel", "parallel")),
    )(c_arr, mine, other, w, m, v)


def _small_allreduce(parts):
    n = len(parts)

    def body(*refs):
        p_refs, accs = refs[:n], refs[n:2 * n]
        bufs = refs[2 * n:3 * n]
        send, recv = refs[3 * n:]
        x, y, c = lax.axis_index("x"), lax.axis_index("y"), lax.axis_index("c")
        partners = [(x, y, 1 - c), (1 - x, y, c), (x, 1 - y, c)]
        for i in range(n):
            accs[i][...] = p_refs[i][...]
        for s, partner in enumerate(partners):
            cps = []
            for i in range(n):
                cp = pltpu.make_async_remote_copy(src_ref=accs[i], dst_ref=bufs[i].at[s], send_sem=send.at[3 * i + s],
                                                  recv_sem=recv.at[3 * i + s], device_id=partner, device_id_type=MESH)
                cp.start()
                cps.append(cp)
            for cp in cps:
                cp.wait()
            for i in range(n):
                accs[i][...] = accs[i][...] + bufs[i][s]

    vmem = pl.BlockSpec(memory_space=pltpu.VMEM)
    return pl.pallas_call(
        body, name="small_allreduce", out_shape=tuple(jax.ShapeDtypeStruct(p.shape, F32) for p in parts),
        in_specs=[vmem] * n, out_specs=tuple([vmem] * n),
        scratch_shapes=[pltpu.VMEM((3,) + p.shape, F32) for p in parts] + [
            pltpu.SemaphoreType.DMA((3 * n,)), pltpu.SemaphoreType.DMA((3 * n,))],
        compiler_params=_cp(),
    )(*parts)


def _small_adamw(reduced, params):
    n = len(reduced)
    n_w = len(params)

    def body(*refs):
        r_refs = refs[:n]
        wmv = refs[n:n + 3 * n_w]
        outs = refs[n + 3 * n_w:]
        outs[0][...] = r_refs[n - 1][0:1, 0:1]
        grads = [r_refs[0][...], r_refs[1][0:1, 0:N_Q_HEADS]] + [r_refs[i][...] for i in range(2, n_w)]
        for i in range(n_w):
            w_ref, m_ref, v_ref = wmv[3 * i:3 * i + 3]
            g_ref, d_ref, m2_ref, v2_ref = outs[1 + 4 * i:5 + 4 * i]
            delta, m2, v2 = _adamw_math(w_ref[...], grads[i], m_ref[...], v_ref[...])
            g_ref[...] = grads[i]
            d_ref[...] = delta
            m2_ref[...] = m2
            v2_ref[...] = v2

    flat = [a for p in params for a in p]
    vmem = pl.BlockSpec(memory_space=pltpu.VMEM)
    out_shape = [jax.ShapeDtypeStruct((1, 1), F32)]
    for p in params:
        out_shape += [jax.ShapeDtypeStruct(p[0].shape, F32)] * 4
    res = pl.pallas_call(
        body, name="small_adamw", out_shape=tuple(out_shape),
        in_specs=[vmem] * (n + len(flat)), out_specs=tuple([vmem] * len(out_shape)),
        compiler_params=_cp(),
    )(*reduced, *flat)
    return res[0], [res[1 + 4 * i:5 + 4 * i] for i in range(n_w)]


def kernel(x, attn_norm_g, w_in, attn_sinks, w_pool, pool_scale, w_out, mlp_norm_g, w_up, w_down, final_norm_g, loss_target, m_attn_norm_g, m_w_in, m_attn_sinks, m_w_pool, m_pool_scale, m_w_out, m_mlp_norm_g, m_w_up, m_w_down, m_final_norm_g, v_attn_norm_g, v_w_in, v_attn_sinks, v_w_pool, v_pool_scale, v_w_out, v_mlp_norm_g, v_w_up, v_w_down, v_final_norm_g):
    nseq, seq, d = x.shape
    c_idx = lax.axis_index("c").astype(jnp.int32)
    j_idx = (2 * lax.axis_index("x") + lax.axis_index("y")).astype(jnp.int32)
    c_arr = jnp.reshape(c_idx, (1,))
    jc_arr = jnp.stack([j_idx, c_idx])

    big_w = (w_in[0].T, w_out[0], w_up[0], w_down[0])
    big_m = (m_w_in[0].T, m_w_out[0], m_w_up[0], m_w_down[0])
    big_v = (v_w_in[0].T, v_w_out[0], v_w_up[0], v_w_down[0])
    (w_in4,) = _gather_weights(big_w[0])
    w_in_full = w_in4.reshape(IN_W, D_MODEL)
    class Comm(_NoComm):
        def start(self):
            self.lands, self.send, self.recv, token = _gather_start(big_w[1:], w_in4)
            return (token,)

        def attn_done(self, attn):
            arrived = _gather_wait(self.lands, self.send, self.recv, attn)
            self.lands, self.send, self.recv, token = _forward_start(arrived)
            return (token,)

        def rest_of_weights(self, pool):
            w_out4, w_up4, w_down4 = _forward_wait(self.lands, self.send, self.recv, pool)
            return w_out4.reshape(D_MODEL, D_MODEL), w_up4, w_down4

        def mlp_grads_ready(self, gw_up4, gw_down4):
            self.mlp = _sibling_start((gw_up4, gw_down4), "mlp_grads_to_sibling")
            return (self.mlp[4],)

        def outproj_done(self, gw_out):
            grads, lands, send, recv, _ = self.mlp
            gw_out4 = gw_out.reshape(N_CHIPS, OUT_SHARD, D_MODEL)
            mlp_grads, mlp_from_sib = _sibling_wait(grads, lands, send, recv, gw_out, "mlp_grads_to_sibling")
            self.grads = (gw_out4,) + tuple(mlp_grads)
            self.from_sib = tuple(_sibling_exchange((gw_out4,), True, "w_out_grad_to_sibling")) + tuple(mlp_from_sib)
            partials = [_chip_partial(g, r, c_arr, "chip_partial_" + nm)
                        for g, r, nm in zip(self.grads, self.from_sib, names[1:])]
            self.parts, self.part_lands, self.part_send, self.part_recv, token = _send_start(partials)
            return (token,)

    names = ("w_in", "w_out", "w_up", "w_down")
    comm = Comm()
    gx, big_g, small_g = _local_step(
        x.reshape(nseq * seq, d), loss_target.reshape(nseq * seq, d), w_in_full, comm,
        attn_norm_g, attn_sinks.reshape(N_Q_HEADS), w_pool[0], pool_scale, mlp_norm_g, final_norm_g.reshape(1, d), seq)
    big_g = tuple(big_g[:1]) + comm.grads

    from_sibling = list(_sibling_exchange(big_g[:1], True, "w_in_grad_to_sibling")) + list(comm.from_sib)
    received = list(_send_partials([_chip_partial(big_g[0], from_sibling[0], c_arr, "chip_partial_w_in")]))
    received += list(_send_wait(comm.parts, comm.part_lands, comm.part_send, comm.part_recv, received[0]))
    mine = [_final_half(g, r, p, jc_arr, "final_half_" + nm) for g, r, p, nm in zip(big_g, from_sibling, received, names)]
    other = _sibling_exchange(mine, False, "halves_to_sibling")
    big_out = [_adamw_shard(a, b, w, m, v, c_arr, "adamw_" + nm)
               for a, b, w, m, v, nm in zip(mine, other, big_w, big_m, big_v, names)]
    big_out[0] = [t.T for t in big_out[0]]

    wp_flat = lambda a: a.reshape(N_POOL * POOL_G, POOL_G)
    small_params = [
        (attn_norm_g, m_attn_norm_g, v_attn_norm_g),
        (attn_sinks, m_attn_sinks, v_attn_sinks),
        (wp_flat(w_pool), wp_flat(m_w_pool), wp_flat(v_w_pool)),
        (pool_scale, m_pool_scale, v_pool_scale),
        (mlp_norm_g, m_mlp_norm_g, v_mlp_norm_g),
        (final_norm_g.reshape(1, d), m_final_norm_g.reshape(1, d), v_final_norm_g.reshape(1, d)),
    ]
    loss, small_out = _small_adamw(_small_allreduce(small_g), small_params)

    def shaped(i, arr):
        return {1: w_in, 5: w_out, 7: w_up, 8: w_down, 0: attn_norm_g, 2: attn_sinks, 3: w_pool, 4: pool_scale,
                6: mlp_norm_g, 9: final_norm_g}[i].shape

    order = [small_out[0], big_out[0], small_out[1], small_out[2], small_out[3], big_out[1], small_out[4], big_out[2],
             big_out[3], small_out[5]]
    outs = [loss.reshape(()), gx.reshape(nseq, seq, d)]
    for kind in range(4):
        outs += [order[i][kind].reshape(shaped(i, None)) for i in range(10)]
    return tuple(outs)
```

```python
import jax
import jax.numpy as jnp
import numpy as np
from jax import lax
from jax.experimental import pallas as pl
from jax.experimental.pallas import tpu as pltpu

F32 = jnp.float32
BF16 = jnp.bfloat16
MXU_DTYPE = jnp.bfloat16

D_MODEL = 1024
HEAD_DIM = 64
N_Q_HEADS = 8
N_KV_HEADS = 2
Q_PER_KV = N_Q_HEADS // N_KV_HEADS
ATTN_W = N_Q_HEADS * HEAD_DIM
KV_W = N_KV_HEADS * HEAD_DIM
BLK = 128
POOL_WINDOWS = (2, 4, 8, 16)
N_POOL = len(POOL_WINDOWS)
POOL_W = D_MODEL - ATTN_W
POOL_G = POOL_W // N_POOL
IN_W = ATTN_W + 2 * KV_W + POOL_W
D_FF = 4 * D_MODEL
EPS = 1e-6
ROPE_THETA = 10000.0
N_CHIPS = 4
IN_SHARD = IN_W // N_CHIPS
OUT_SHARD = D_MODEL // N_CHIPS
FF_SHARD = D_FF // N_CHIPS
LANES = 128

ADAM_LR = 0.001
ADAM_B1 = 0.9
ADAM_B2 = 0.999
ADAM_EPS = 1e-08
ADAM_WD = 0.01
ADAM_STEP = 10

VMEM_LIMIT = 56 * 1024 * 1024
MESH = pl.DeviceIdType.MESH


def _cp(**kw):
    return pltpu.CompilerParams(vmem_limit_bytes=VMEM_LIMIT, **kw)


def _mm(a, b):
    return jnp.dot(a.astype(MXU_DTYPE), b.astype(MXU_DTYPE), preferred_element_type=F32)


def _mm_nt(a, b):
    return lax.dot_general(a.astype(MXU_DTYPE), b.astype(MXU_DTYPE), (((1,), (1,)), ((), ())),
                           preferred_element_type=F32)


def _mm_tn(a, b):
    return lax.dot_general(a.astype(MXU_DTYPE), b.astype(MXU_DTYPE), (((0,), (0,)), ((), ())),
                           preferred_element_type=F32)


def _resident(shape):
    nd = len(shape)
    return pl.BlockSpec(shape, lambda *_: (0,) * nd, pipeline_mode=pl.Buffered(1))


def _const(shape):
    nd = len(shape)
    return pl.BlockSpec(shape, lambda *_: (0,) * nd)


def _rope_tables(seq):
    half = HEAD_DIM // 2
    inv_freq = (ROPE_THETA ** (-np.arange(half, dtype=np.float32) / half)).astype(np.float32)
    ang = np.arange(seq, dtype=np.float32)[:, None] * inv_freq[None, :]
    cos, sin = np.cos(ang).astype(np.float32), np.sin(ang).astype(np.float32)
    cos_t = np.concatenate([cos, cos, cos, cos], axis=1)
    sin_t = np.concatenate([-sin, sin, -sin, sin], axis=1)
    return jnp.asarray(cos_t), jnp.asarray(sin_t)


def _swap_halves(xc):
    lane = lax.broadcasted_iota(jnp.int32, xc.shape, 1)
    return jnp.where((lane & 63) < 32, pltpu.roll(xc, 96, 1), pltpu.roll(xc, 32, 1))


def _gather_weights(*shards):
    nw = len(shards)

    def body(*refs):
        srcs = refs[0:nw]
        outs = refs[nw:2 * nw]
        stages = refs[2 * nw:3 * nw]
        send1, recv1, send2, recv2, lsem = refs[3 * nw:]
        x, y, c = lax.axis_index("x"), lax.axis_index("y"), lax.axis_index("c")
        j = 2 * x + y
        chips = [(1 - x, y), (x, 1 - y), (1 - x, 1 - y)]
        for k in range(nw):
            stages[k][...] = srcs[k][...].astype(BF16)
        local = [pltpu.make_async_copy(stages[k], outs[k].at[j], lsem.at[k]) for k in range(nw)]
        for cp in local:
            cp.start()

        def piece(k, chip, half):
            rows = shards[k].shape[0] // 2
            return outs[k].at[2 * chip[0] + chip[1], pl.ds(half * rows, rows)]

        first, passed = [], []
        for k in range(nw):
            rows = shards[k].shape[0] // 2
            for n, chip in enumerate(chips):
                cp = pltpu.make_async_remote_copy(
                    src_ref=stages[k].at[pl.ds(c * rows, rows)], dst_ref=piece(k, (x, y), c),
                    send_sem=send1.at[k * 3 + n], recv_sem=recv1.at[k * 3 + n],
                    device_id=(chip[0], chip[1], c), device_id_type=MESH)
                cp.start()
                first.append(cp)
        for k in range(nw):
            for n, chip in enumerate(chips):
                got = piece(k, chip, c)
                pltpu.make_async_remote_copy(
                    src_ref=got, dst_ref=got, send_sem=send1.at[k * 3 + n], recv_sem=recv1.at[k * 3 + n],
                    device_id=(chip[0], chip[1], c), device_id_type=MESH).wait_recv()
                cp = pltpu.make_async_remote_copy(
                    src_ref=got, dst_ref=got, send_sem=send2.at[k * 3 + n], recv_sem=recv2.at[k * 3 + n],
                    device_id=(x, y, 1 - c), device_id_type=MESH)
                cp.start()
                passed.append(cp)
        for k in range(nw):
            for n, chip in enumerate(chips):
                got = piece(k, chip, 1 - c)
                pltpu.make_async_remote_copy(
                    src_ref=got, dst_ref=got, send_sem=send2.at[k * 3 + n], recv_sem=recv2.at[k * 3 + n],
                    device_id=(x, y, 1 - c), device_id_type=MESH).wait_recv()
        for cp in first + passed:
            cp.wait_send()
        for cp in local:
            cp.wait()

    hbm = pl.BlockSpec(memory_space=pltpu.HBM)
    vmem = pl.BlockSpec(memory_space=pltpu.VMEM)
    return pl.pallas_call(
        body, name="gather_weights",
        out_shape=tuple(jax.ShapeDtypeStruct((N_CHIPS,) + s.shape, BF16) for s in shards),
        in_specs=[vmem] * nw, out_specs=tuple([hbm] * nw),
        scratch_shapes=[pltpu.VMEM(s.shape, BF16) for s in shards] + [
            pltpu.SemaphoreType.DMA((3 * nw,)), pltpu.SemaphoreType.DMA((3 * nw,)),
            pltpu.SemaphoreType.DMA((3 * nw,)), pltpu.SemaphoreType.DMA((3 * nw,)),
            pltpu.SemaphoreType.DMA((nw,))],
        compiler_params=_cp(),
    )(*shards)


_HBM = pl.BlockSpec(memory_space=pltpu.HBM)
_SEM = pl.BlockSpec(memory_space=pltpu.SEMAPHORE)
_ANY = pl.BlockSpec(memory_space=pl.ANY)
_EFFECT = pltpu.SideEffectType.DATAFLOW_SIDE_EFFECTING
TOKEN = jax.ShapeDtypeStruct((8, LANES), F32)


def _other_chips(x, y):
    return [(1 - x, y), (x, 1 - y), (1 - x, 1 - y)]


def _gather_start(shards, after):
    nw = len(shards)

    def body(*refs):
        srcs, lands = refs[:nw], refs[nw + 1:2 * nw + 1]
        send, recv, token = refs[2 * nw + 1:2 * nw + 4]
        stages = refs[2 * nw + 4:3 * nw + 4]
        lsem = refs[3 * nw + 4]
        x, y, c = lax.axis_index("x"), lax.axis_index("y"), lax.axis_index("c")
        j = 2 * x + y
        for k in range(nw):
            stages[k][...] = srcs[k][...].astype(BF16)
        local = [pltpu.make_async_copy(stages[k], lands[k].at[j], lsem.at[k]) for k in range(nw)]
        for cp in local:
            cp.start()
        for cp in local:
            cp.wait()
        for k in range(nw):
            rows = shards[k].shape[0] // 2
            mine = lands[k].at[j, pl.ds(c * rows, rows)]
            for n, chip in enumerate(_other_chips(x, y)):
                pltpu.make_async_remote_copy(
                    src_ref=mine, dst_ref=mine, send_sem=send.at[k * 3 + n], recv_sem=recv.at[k * 3 + n],
                    device_id=(chip[0], chip[1], c), device_id_type=MESH).start()
        token[...] = jnp.zeros_like(token)

    vmem = pl.BlockSpec(memory_space=pltpu.VMEM)
    res = pl.pallas_call(
        body, name="gather_start",
        out_shape=tuple(pltpu.HBM((N_CHIPS,) + s.shape, BF16) for s in shards) + (
            pltpu.SemaphoreType.DMA((3 * nw,)), pltpu.SemaphoreType.DMA((3 * nw,)), TOKEN),
        in_specs=[vmem] * nw + [_ANY], out_specs=tuple([_HBM] * nw) + (_SEM, _SEM, vmem),
        scratch_shapes=[pltpu.VMEM(s.shape, BF16) for s in shards] + [pltpu.SemaphoreType.DMA((nw,))],
        compiler_params=_cp(has_side_effects=_EFFECT),
    )(*shards, after)
    return res[:nw], res[nw], res[nw + 1], res[nw + 2]


def _gather_wait(lands, send, recv, after):
    nw = len(lands)

    def body(*refs):
        ins = refs[:nw]
        send_ref, recv_ref = refs[nw], refs[nw + 1]
        x, y, c = lax.axis_index("x"), lax.axis_index("y"), lax.axis_index("c")
        j = 2 * x + y
        for k in range(nw):
            rows = lands[k].shape[1] // 2
            mine = ins[k].at[j, pl.ds(c * rows, rows)]
            for n, chip in enumerate(_other_chips(x, y)):
                got = ins[k].at[2 * chip[0] + chip[1], pl.ds(c * rows, rows)]
                cp = pltpu.make_async_remote_copy(
                    src_ref=mine, dst_ref=got, send_sem=send_ref.at[k * 3 + n], recv_sem=recv_ref.at[k * 3 + n],
                    device_id=(chip[0], chip[1], c), device_id_type=MESH)
                cp.wait_send()
                cp.wait_recv()

    return pl.pallas_call(
        body, name="gather_wait", out_shape=tuple(pltpu.HBM(a.shape, a.dtype) for a in lands),
        in_specs=[_HBM] * nw + [_SEM, _SEM, _ANY], out_specs=tuple([_HBM] * nw),
        input_output_aliases={k: k for k in range(nw)},
        compiler_params=_cp(has_side_effects=_EFFECT),
    )(*lands, send, recv, after)


def _forward_copies(refs, nw, rows_of, send, recv):
    x, y, c = lax.axis_index("x"), lax.axis_index("y"), lax.axis_index("c")
    out = []
    for k in range(nw):
        rows = rows_of[k]
        for n, chip in enumerate(_other_chips(x, y)):
            got = refs[k].at[2 * chip[0] + chip[1], pl.ds(c * rows, rows)]
            theirs = refs[k].at[2 * chip[0] + chip[1], pl.ds((1 - c) * rows, rows)]
            out.append(pltpu.make_async_remote_copy(
                src_ref=got, dst_ref=got, send_sem=send.at[k * 3 + n], recv_sem=recv.at[k * 3 + n],
                device_id=(x, y, 1 - c), device_id_type=MESH))
            out.append(pltpu.make_async_remote_copy(
                src_ref=theirs, dst_ref=theirs, send_sem=send.at[k * 3 + n], recv_sem=recv.at[k * 3 + n],
                device_id=(x, y, 1 - c), device_id_type=MESH))
    return out[0::2], out[1::2]


def _forward_start(lands):
    nw = len(lands)
    rows_of = [a.shape[1] // 2 for a in lands]

    def body(*refs):
        send, recv, token = refs[2 * nw:2 * nw + 3]
        mine, _ = _forward_copies(refs[:nw], nw, rows_of, send, recv)
        for cp in mine:
            cp.start()
        token[...] = jnp.zeros_like(token)

    res = pl.pallas_call(
        body, name="forward_start",
        out_shape=tuple(pltpu.HBM(a.shape, a.dtype) for a in lands) + (
            pltpu.SemaphoreType.DMA((3 * nw,)), pltpu.SemaphoreType.DMA((3 * nw,)), TOKEN),
        in_specs=[_HBM] * nw, out_specs=tuple([_HBM] * nw) + (_SEM, _SEM, pl.BlockSpec(memory_space=pltpu.VMEM)),
        input_output_aliases={k: k for k in range(nw)},
        compiler_params=_cp(has_side_effects=_EFFECT),
    )(*lands)
    return res[:nw], res[nw], res[nw + 1], res[nw + 2]


def _forward_wait(lands, send, recv, after):
    nw = len(lands)
    rows_of = [a.shape[1] // 2 for a in lands]

    def body(*refs):
        mine, theirs = _forward_copies(refs[:nw], nw, rows_of, refs[nw], refs[nw + 1])
        for cp in mine:
            cp.wait_send()
        for cp in theirs:
            cp.wait_recv()

    return pl.pallas_call(
        body, name="forward_wait", out_shape=tuple(pltpu.HBM(a.shape, a.dtype) for a in lands),
        in_specs=[_HBM] * nw + [_SEM, _SEM, _ANY], out_specs=tuple([_HBM] * nw),
        input_output_aliases={k: k for k in range(nw)},
        compiler_params=_cp(has_side_effects=_EFFECT),
    )(*lands, send, recv, after)


def _after(tokens):
    tokens = [t for t in tokens if t is not None]
    return tokens, [_ANY] * len(tokens)


def _fwd_inproj(x, g1, w_in, cos_t, sin_t, tm, after=()):
    t_tok = x.shape[0]
    seq = cos_t.shape[0]
    per_seq = seq // tm
    tokens, token_specs = _after(after)

    def body(x_ref, g_ref, w_ref, cos_ref, sin_ref, *rest):
        q_ref, k_ref, v_ref, u_ref = rest[len(tokens):]
        xv = x_ref[...]
        r = lax.rsqrt(jnp.mean(xv * xv, axis=-1, keepdims=True) + EPS)
        h = (xv * r) * g_ref[...]
        proj = _mm_nt(h, w_ref[...])
        cos, sin = cos_ref[...], sin_ref[...]
        for cidx in range((ATTN_W + KV_W) // LANES):
            xc = proj[:, cidx * LANES:(cidx + 1) * LANES]
            rot = xc * cos + _swap_halves(xc) * sin
            if cidx < ATTN_W // LANES:
                q_ref[:, cidx * LANES:(cidx + 1) * LANES] = (rot * (HEAD_DIM ** -0.5)).astype(q_ref.dtype)
            else:
                for g in range(N_KV_HEADS):
                    k_ref[g] = rot[:, g * HEAD_DIM:(g + 1) * HEAD_DIM].astype(k_ref.dtype)
        for g in range(N_KV_HEADS):
            c0 = ATTN_W + KV_W + g * HEAD_DIM
            v_ref[g] = proj[:, c0:c0 + HEAD_DIM].astype(v_ref.dtype)
        u_ref[...] = proj[:, ATTN_W + 2 * KV_W:]

    row = lambda w: pl.BlockSpec((tm, w), lambda i: (i, 0))
    kv = pl.BlockSpec((N_KV_HEADS, tm, HEAD_DIM), lambda i: (0, i, 0))
    tab = pl.BlockSpec((tm, LANES), lambda i: (i % per_seq, 0))
    kv_shape = jax.ShapeDtypeStruct((N_KV_HEADS, t_tok, HEAD_DIM), MXU_DTYPE)
    return pl.pallas_call(
        body, name="fwd_inproj", grid=(t_tok // tm,),
        in_specs=[row(D_MODEL), _const((1, D_MODEL)), _resident((IN_W, D_MODEL)), tab, tab] + token_specs,
        out_specs=(row(ATTN_W), kv, kv, row(POOL_W)),
        out_shape=(jax.ShapeDtypeStruct((t_tok, ATTN_W), MXU_DTYPE), kv_shape, kv_shape,
                   jax.ShapeDtypeStruct((t_tok, POOL_W), F32)),
        compiler_params=_cp(dimension_semantics=("parallel",)),
    )(x, g1, w_in, cos_t, sin_t, *tokens)


MASKED = -1e30


def _attn_bias():
    b = lax.broadcasted_iota(jnp.int32, (2 * BLK, BLK), 0)
    a = lax.broadcasted_iota(jnp.int32, (2 * BLK, BLK), 1)
    own = (b >= BLK) & (b - BLK <= a)
    prev = (b < BLK) & (b > a)
    return jnp.stack([jnp.where(own, 0.0, MASKED), jnp.where(own | prev, 0.0, MASKED)]).astype(F32)


def _sink_rows(sinks):
    return jnp.repeat(sinks.reshape(N_KV_HEADS, Q_PER_KV), BLK, axis=1).reshape(N_KV_HEADS, 1, Q_PER_KV * BLK)


def _stack_heads(ref, r0, g):
    return jnp.concatenate(
        [ref[pl.ds(r0, BLK), (Q_PER_KV * g + h) * HEAD_DIM:(Q_PER_KV * g + h + 1) * HEAD_DIM] for h in range(Q_PER_KV)],
        axis=0)


def _kv_window(ref, p0, r0, g):
    return jnp.concatenate([ref[g, pl.ds(p0, BLK), :], ref[g, pl.ds(r0, BLK), :]], axis=0)


def _scores_t(k_ref, q_ref, bias, p0, r0, g):
    kk = _kv_window(k_ref, p0, r0, g)
    qs = _stack_heads(q_ref, r0, g)
    st = _mm_nt(kk, qs) + jnp.concatenate([bias] * Q_PER_KV, axis=1)
    return st, kk, qs


def _head_rows(ref, r0, g):
    return jnp.concatenate([ref[pl.ds(Q_PER_KV * g + h, 1), pl.ds(r0, BLK)] for h in range(Q_PER_KV)], axis=1)


def _pairs_to_rows(xt):
    out = []
    for t in range(Q_PER_KV // 2):
        pair = jnp.concatenate([xt[:, (2 * t) * BLK:(2 * t + 1) * BLK], xt[:, (2 * t + 1) * BLK:(2 * t + 2) * BLK]], axis=0)
        out.append(pair.T)
    return out


def _shift_rows(x, k, seq):
    row = lax.broadcasted_iota(jnp.int32, x.shape, 0)
    if k > 0:
        return jnp.where(row >= k, pltpu.roll(x, k, 0), 0.0)
    return jnp.where(row < seq + k, pltpu.roll(x, seq + k, 0), 0.0)


def _window_sum(x, w, seq, forward):
    s, k = x, 1
    while k < w:
        s = s + _shift_rows(s, -k if forward else k, seq)
        k *= 2
    return s


def _inv_count(seq, w):
    pos = lax.broadcasted_iota(jnp.int32, (seq, 1), 0)
    return 1.0 / jnp.minimum(pos + 1, w).astype(F32)


def _fwd_attn(q, k, v, sink_rows, bias, seq):
    t_tok = q.shape[0]
    nblk = seq // BLK

    def body(q_ref, k_ref, v_ref, sink_ref, bias_ref, o_ref, lse_ref):
        def blk(i, carry):
            r0 = pl.multiple_of(i * BLK, BLK)
            p0 = pl.multiple_of(jnp.maximum(i - 1, 0) * BLK, BLK)
            bias = bias_ref[jnp.minimum(i, 1)]
            for g in range(N_KV_HEADS):
                st, _, _ = _scores_t(k_ref, q_ref, bias, p0, r0, g)
                sink = sink_ref[g]
                m = jnp.maximum(jnp.max(st, axis=0, keepdims=True), sink)
                p = jnp.exp(st - m).astype(MXU_DTYPE)
                vv = _kv_window(v_ref, p0, r0, g)
                ot = _mm_tn(jnp.concatenate([vv, jnp.ones_like(vv)], axis=1), p)
                den = ot[HEAD_DIM:HEAD_DIM + 1] + jnp.exp(sink - m)
                lse = m + jnp.log(den)
                for h in range(Q_PER_KV):
                    lse_ref[pl.ds(Q_PER_KV * g + h, 1), pl.ds(r0, BLK)] = lse[:, h * BLK:(h + 1) * BLK]
                ot = ot[:HEAD_DIM] * (1.0 / den)
                for t, tile in enumerate(_pairs_to_rows(ot)):
                    c0 = (Q_PER_KV * g + 2 * t) * HEAD_DIM
                    o_ref[pl.ds(r0, BLK), c0:c0 + LANES] = tile.astype(o_ref.dtype)
            return carry

        lax.fori_loop(0, nblk, blk, 0, unroll=4)

    row = lambda w: pl.BlockSpec((seq, w), lambda i: (i, 0))
    kv = pl.BlockSpec((N_KV_HEADS, seq, HEAD_DIM), lambda i: (0, i, 0))
    return pl.pallas_call(
        body, name="fwd_attn", grid=(t_tok // seq,),
        in_specs=[row(ATTN_W), kv, kv, _const((N_KV_HEADS, 1, Q_PER_KV * BLK)), _const((2, 2 * BLK, BLK))],
        out_specs=(row(ATTN_W), pl.BlockSpec((N_Q_HEADS, seq), lambda i: (0, i))),
        out_shape=(jax.ShapeDtypeStruct((t_tok, ATTN_W), MXU_DTYPE), jax.ShapeDtypeStruct((N_Q_HEADS, t_tok), F32)),
        compiler_params=_cp(dimension_semantics=("parallel",)),
    )(q, k, v, sink_rows, bias)


def _fwd_pool(u, w_pool, pool_scale, seq, after=()):
    t_tok = u.shape[0]
    tokens, token_specs = _after(after)

    def body(u_ref, wp_ref, sc_ref, *rest):
        o_ref = rest[-1]
        for gi, w in enumerate(POOL_WINDOWS):
            @pl.when(pl.program_id(0) == gi)
            def _():
                ug = u_ref[...]
                d = _window_sum(ug, w, seq, False) * _inv_count(seq, w) - ug
                o_ref[...] = (_mm(d, wp_ref[0]) * sc_ref[...]).astype(o_ref.dtype)

    col = pl.BlockSpec((seq, POOL_G), lambda g, s: (s, g))
    return pl.pallas_call(
        body, name="fwd_pool", grid=(N_POOL, t_tok // seq),
        in_specs=[col, pl.BlockSpec((1, POOL_G, POOL_G), lambda g, s: (g, 0, 0)),
                  pl.BlockSpec((1, POOL_G), lambda g, s: (0, g))] + token_specs, out_specs=col,
        out_shape=jax.ShapeDtypeStruct((t_tok, POOL_W), MXU_DTYPE),
        compiler_params=_cp(dimension_semantics=("parallel", "parallel")),
    )(u, w_pool, pool_scale, *tokens)


def _rms_bwd(dy_g, xn, r):
    return r * (dy_g - xn * jnp.mean(dy_g * xn, axis=-1, keepdims=True))


def _mlp_fwd_bwd(x, attn, pool, target, w_out, w_up4, w_down4, g2, gf, tm):
    t_tok = x.shape[0]

    def body(x_ref, attn_ref, pool_ref, tgt_ref, wo_ref, wu_ref, wd_ref, g2_ref, gf_ref,
             h2_ref, a_ref, da_ref, dx2b_ref, dx1_ref, dx1b_ref, loss_ref, dgf_ref, dg2_ref):
        @pl.when(pl.program_id(0) == 0)
        def _():
            loss_ref[...] = jnp.zeros_like(loss_ref)
            dgf_ref[...] = jnp.zeros_like(dgf_ref)
            dg2_ref[...] = jnp.zeros_like(dg2_ref)

        x1 = x_ref[...] + (_mm(attn_ref[...], wo_ref[:ATTN_W]) + _mm(pool_ref[...], wo_ref[ATTN_W:]))
        r2 = lax.rsqrt(jnp.mean(x1 * x1, axis=-1, keepdims=True) + EPS)
        xn1 = x1 * r2
        g2 = g2_ref[...]
        h2 = (xn1 * g2).astype(MXU_DTYPE)
        h2_ref[...] = h2
        acc = jnp.zeros((tm, D_MODEL), F32)
        for j in range(N_CHIPS):
            a = _mm(h2, wu_ref[j])
            a_ref[:, j * FF_SHARD:(j + 1) * FF_SHARD] = a.astype(a_ref.dtype)
            acc = acc + _mm(jnp.square(jnp.maximum(a, 0.0)), wd_ref[j])
        x2 = x1 + acc
        r3 = lax.rsqrt(jnp.mean(x2 * x2, axis=-1, keepdims=True) + EPS)
        xn2 = x2 * r3
        gf_v = gf_ref[...]
        err = xn2 * gf_v - tgt_ref[...]
        part = jnp.sum(err * err) * (0.5 / D_MODEL)
        first = (lax.broadcasted_iota(jnp.int32, loss_ref.shape, 0) == 0) & (lax.broadcasted_iota(jnp.int32, loss_ref.shape, 1) == 0)
        loss_ref[...] += jnp.where(first, part, 0.0)
        dy = err * (1.0 / D_MODEL)
        dgf_ref[...] += jnp.sum(dy * xn2, axis=0, keepdims=True)
        dx2 = _rms_bwd(dy * gf_v, xn2, r3)
        dx2b = dx2.astype(MXU_DTYPE)
        dx2b_ref[...] = dx2b
        dh2 = jnp.zeros((tm, D_MODEL), F32)
        for j in range(N_CHIPS):
            sl = slice(j * FF_SHARD, (j + 1) * FF_SHARD)
            dhid = _mm_nt(dx2b, wd_ref[j])
            da = (dhid * (2.0 * jnp.maximum(a_ref[:, sl].astype(F32), 0.0))).astype(MXU_DTYPE)
            da_ref[:, sl] = da
            dh2 = dh2 + _mm_nt(da, wu_ref[j])
        dg2_ref[...] += jnp.sum(dh2 * xn1, axis=0, keepdims=True)
        dx1 = dx2 + _rms_bwd(dh2 * g2, xn1, r2)
        dx1_ref[...] = dx1
        dx1b_ref[...] = dx1.astype(dx1b_ref.dtype)

    row = lambda w: pl.BlockSpec((tm, w), lambda i: (i, 0))
    vec = jax.ShapeDtypeStruct((1, D_MODEL), F32)
    return pl.pallas_call(
        body, name="mlp_fwd_bwd", grid=(t_tok // tm,),
        in_specs=[row(D_MODEL), row(ATTN_W), row(POOL_W), row(D_MODEL), _resident((D_MODEL, D_MODEL)),
                  _resident((N_CHIPS, D_MODEL, FF_SHARD)), _resident((N_CHIPS, FF_SHARD, D_MODEL)),
                  _const((1, D_MODEL)), _const((1, D_MODEL))],
        out_specs=(row(D_MODEL), row(D_FF), row(D_FF), row(D_MODEL), row(D_MODEL), row(D_MODEL),
                   _const((8, LANES)), _const((1, D_MODEL)), _const((1, D_MODEL))),
        out_shape=(jax.ShapeDtypeStruct((t_tok, D_MODEL), MXU_DTYPE), jax.ShapeDtypeStruct((t_tok, D_FF), MXU_DTYPE),
                   jax.ShapeDtypeStruct((t_tok, D_FF), MXU_DTYPE), jax.ShapeDtypeStruct((t_tok, D_MODEL), MXU_DTYPE),
                   jax.ShapeDtypeStruct((t_tok, D_MODEL), F32), jax.ShapeDtypeStruct((t_tok, D_MODEL), MXU_DTYPE),
                   jax.ShapeDtypeStruct((8, LANES), F32), vec, vec),
        compiler_params=_cp(dimension_semantics=("arbitrary",)),
    )(x, attn, pool, target, w_out, w_up4, w_down4, g2, gf)


def _bwd_mlp_wgrads(h2, da, a, dx2b, tk):
    t_tok = h2.shape[0]

    def body(h2_ref, da_ref, a_ref, dx2b_ref, gup_ref, gdn_ref):
        @pl.when(pl.program_id(1) == 0)
        def _():
            gup_ref[...] = jnp.zeros_like(gup_ref)
            gdn_ref[...] = jnp.zeros_like(gdn_ref)

        gup_ref[0] += _mm_tn(h2_ref[...], da_ref[...])
        hid = jnp.square(jnp.maximum(a_ref[...].astype(F32), 0.0))
        gdn_ref[0] += _mm_tn(hid, dx2b_ref[...])

    tok = pl.BlockSpec((tk, D_MODEL), lambda j, t: (t, 0))
    ffb = pl.BlockSpec((tk, FF_SHARD), lambda j, t: (t, j))
    wblk = pl.BlockSpec((1, D_MODEL, D_MODEL), lambda j, t: (j, 0, 0))
    return pl.pallas_call(
        body, name="bwd_mlp_wgrads", grid=(N_CHIPS, t_tok // tk),
        in_specs=[tok, ffb, ffb, tok], out_specs=(wblk, wblk),
        out_shape=(jax.ShapeDtypeStruct((N_CHIPS, D_MODEL, FF_SHARD), F32), jax.ShapeDtypeStruct((N_CHIPS, FF_SHARD, D_MODEL), F32)),
        compiler_params=_cp(dimension_semantics=("parallel", "arbitrary")),
    )(h2, da, a, dx2b)


def _head_selector():
    ch = lax.broadcasted_iota(jnp.int32, (ATTN_W, LANES), 0)
    col = lax.broadcasted_iota(jnp.int32, (ATTN_W, LANES), 1)
    return (ch // HEAD_DIM == col).astype(MXU_DTYPE)


def _bwd_outproj(dx1b, attn, pool, w_out, head_sel, tm, after=()):
    t_tok = dx1b.shape[0]
    tokens, token_specs = _after(after)

    def body(dx_ref, attn_ref, pool_ref, wo_ref, sel_ref, *rest):
        dattn_ref, dpool_ref, delta_ref, gwo_ref = rest[len(tokens):]

        @pl.when(pl.program_id(0) == 0)
        def _():
            gwo_ref[...] = jnp.zeros_like(gwo_ref)

        dx = dx_ref[...]
        attn = attn_ref[...]
        dattn = _mm_nt(dx, wo_ref[:ATTN_W])
        dattn_ref[...] = dattn.astype(dattn_ref.dtype)
        dpool_ref[...] = _mm_nt(dx, wo_ref[ATTN_W:])
        prod = dattn * attn.astype(F32)
        hi = prod.astype(MXU_DTYPE)
        lo = prod - hi.astype(F32)
        delta = _mm(hi, sel_ref[...]) + _mm(lo, sel_ref[...])
        delta_ref[...] = delta.T[:N_Q_HEADS]
        gwo_ref[:ATTN_W] += _mm_tn(attn, dx)
        gwo_ref[ATTN_W:] += _mm_tn(pool_ref[...], dx)

    row = lambda w: pl.BlockSpec((tm, w), lambda i: (i, 0))
    return pl.pallas_call(
        body, name="bwd_outproj", grid=(t_tok // tm,),
        in_specs=[row(D_MODEL), row(ATTN_W), row(POOL_W), _resident((D_MODEL, D_MODEL)), _const((ATTN_W, LANES))] + token_specs,
        out_specs=(row(ATTN_W), row(POOL_W), pl.BlockSpec((N_Q_HEADS, tm), lambda i: (0, i)), _const((D_MODEL, D_MODEL))),
        out_shape=(jax.ShapeDtypeStruct((t_tok, ATTN_W), MXU_DTYPE), jax.ShapeDtypeStruct((t_tok, POOL_W), F32),
                   jax.ShapeDtypeStruct((N_Q_HEADS, t_tok), F32), jax.ShapeDtypeStruct((D_MODEL, D_MODEL), F32)),
        compiler_params=_cp(dimension_semantics=("arbitrary",)),
    )(dx1b, attn, pool, w_out, head_sel, *tokens)


def _bwd_attn(q, k, v, dattn, lse, delta, sink_rows, bias, cos_t, sin_t, seq, after=()):
    t_tok = q.shape[0]
    nblk = seq // BLK
    qkv_w = ATTN_W + 2 * KV_W
    tokens, token_specs = _after(after)

    def unrope(d, cos, sin):
        return d * cos - _swap_halves(d) * sin

    def body(q_ref, k_ref, v_ref, do_ref, lse_ref, delta_ref, sink_ref, bias_ref, cos_ref, sin_ref, *rest):
        dqkv_ref, dsink_ref, dk_acc, dv_acc = rest[len(tokens):]

        @pl.when(pl.program_id(0) == 0)
        def _():
            dsink_ref[...] = jnp.zeros_like(dsink_ref)

        dk_acc[...] = jnp.zeros_like(dk_acc)
        dv_acc[...] = jnp.zeros_like(dv_acc)

        def blk(i, dsink):
            r0 = pl.multiple_of(i * BLK, BLK)
            p0 = pl.multiple_of(jnp.maximum(i - 1, 0) * BLK, BLK)
            bias = bias_ref[jnp.minimum(i, 1)]
            cos, sin = cos_ref[pl.ds(r0, BLK), :], sin_ref[pl.ds(r0, BLK), :]
            new = []
            for g in range(N_KV_HEADS):
                st, kk, qs = _scores_t(k_ref, q_ref, bias, p0, r0, g)
                lse_g, delta_g = _head_rows(lse_ref, r0, g), _head_rows(delta_ref, r0, g)
                pn = jnp.exp(st - lse_g)
                dos = _stack_heads(do_ref, r0, g)
                dst = pn * (_mm_nt(_kv_window(v_ref, p0, r0, g), dos) - delta_g)
                new.append(dsink[g] - jnp.exp(sink_ref[g] - lse_g) * delta_g)
                dqt = _mm_tn(kk, dst) * (HEAD_DIM ** -0.5)
                for t, tile in enumerate(_pairs_to_rows(dqt)):
                    c0 = (Q_PER_KV * g + 2 * t) * HEAD_DIM
                    dqkv_ref[pl.ds(r0, BLK), c0:c0 + LANES] = unrope(tile, cos, sin).astype(dqkv_ref.dtype)
                dkk = _mm(dst, qs)
                dvv = _mm(pn, dos)
                dk_acc[g, pl.ds(p0, BLK), :] += dkk[:BLK]
                dk_acc[g, pl.ds(r0, BLK), :] += dkk[BLK:]
                dv_acc[g, pl.ds(p0, BLK), :] += dvv[:BLK]
                dv_acc[g, pl.ds(r0, BLK), :] += dvv[BLK:]
            return tuple(new)

        zero = jnp.zeros((1, Q_PER_KV * BLK), F32)
        dsink = lax.fori_loop(0, nblk // 2, lambda i2, acc: blk(2 * i2 + 1, blk(2 * i2, acc)), (zero,) * N_KV_HEADS)
        lane = lax.broadcasted_iota(jnp.int32, dsink_ref.shape, 1)
        row = lax.broadcasted_iota(jnp.int32, dsink_ref.shape, 0)
        tile = jnp.zeros(dsink_ref.shape, F32)
        for g in range(N_KV_HEADS):
            for h in range(Q_PER_KV):
                tot = jnp.sum(dsink[g][:, h * BLK:(h + 1) * BLK])
                tile = tile + jnp.where((row == 0) & (lane == Q_PER_KV * g + h), tot, 0.0)
        dsink_ref[...] += tile
        dk = jnp.concatenate([dk_acc[g] for g in range(N_KV_HEADS)], axis=1)
        dqkv_ref[:, ATTN_W:ATTN_W + KV_W] = unrope(dk, cos_ref[...], sin_ref[...]).astype(dqkv_ref.dtype)
        dqkv_ref[:, ATTN_W + KV_W:] = jnp.concatenate([dv_acc[g] for g in range(N_KV_HEADS)], axis=1).astype(dqkv_ref.dtype)

    row = lambda w: pl.BlockSpec((seq, w), lambda i: (i, 0))
    kv = pl.BlockSpec((N_KV_HEADS, seq, HEAD_DIM), lambda i: (0, i, 0))
    per_head = pl.BlockSpec((N_Q_HEADS, seq), lambda i: (0, i))
    return pl.pallas_call(
        body, name="bwd_attn", grid=(t_tok // seq,),
        in_specs=[row(ATTN_W), kv, kv, row(ATTN_W), per_head, per_head, _const((N_KV_HEADS, 1, Q_PER_KV * BLK)),
                  _const((2, 2 * BLK, BLK)), _resident((seq, LANES)), _resident((seq, LANES))] + token_specs,
        out_specs=(row(qkv_w), _const((8, LANES))),
        out_shape=(jax.ShapeDtypeStruct((t_tok, qkv_w), MXU_DTYPE), jax.ShapeDtypeStruct((8, LANES), F32)),
        scratch_shapes=[pltpu.VMEM((N_KV_HEADS, seq, HEAD_DIM), F32), pltpu.VMEM((N_KV_HEADS, seq, HEAD_DIM), F32)],
        compiler_params=_cp(dimension_semantics=("arbitrary",)),
    )(q, k, v, dattn, lse, delta, sink_rows, bias, cos_t, sin_t, *tokens)


def _bwd_pool(u, dpool, w_pool, pool_scale, seq):
    t_tok = u.shape[0]

    def body(u_ref, dp_ref, wp_ref, sc_ref, du_ref, dwp_ref, dsc_ref):
        @pl.when(pl.program_id(1) == 0)
        def _():
            dwp_ref[...] = jnp.zeros_like(dwp_ref)
            dsc_ref[...] = jnp.zeros_like(dsc_ref)

        for gi, w in enumerate(POOL_WINDOWS):
            @pl.when(pl.program_id(0) == gi)
            def _():
                ug = u_ref[...]
                inv = _inv_count(seq, w)
                d = (_window_sum(ug, w, seq, False) * inv - ug).astype(MXU_DTYPE)
                y = _mm(d, wp_ref[0])
                dpool = dp_ref[...]
                dsc_ref[...] += jnp.sum(y * dpool, axis=0, keepdims=True)
                dy = (dpool * sc_ref[...]).astype(MXU_DTYPE)
                dwp_ref[0] += _mm_tn(d, dy)
                dd = _mm_nt(dy, wp_ref[0])
                du_ref[...] = (_window_sum(dd * inv, w, seq, True) - dd).astype(du_ref.dtype)

    col = pl.BlockSpec((seq, POOL_G), lambda g, s: (s, g))
    wblk = pl.BlockSpec((1, POOL_G, POOL_G), lambda g, s: (g, 0, 0))
    sblk = pl.BlockSpec((1, POOL_G), lambda g, s: (0, g))
    return pl.pallas_call(
        body, name="bwd_pool", grid=(N_POOL, t_tok // seq),
        in_specs=[col, col, wblk, sblk], out_specs=(col, wblk, sblk),
        out_shape=(jax.ShapeDtypeStruct((t_tok, POOL_W), MXU_DTYPE), jax.ShapeDtypeStruct((N_POOL, POOL_G, POOL_G), F32),
                   jax.ShapeDtypeStruct((1, POOL_W), F32)),
        compiler_params=_cp(dimension_semantics=("arbitrary", "arbitrary")),
    )(u, dpool, w_pool, pool_scale)


def _bwd_inproj(dqkv, du, x, dx1, w_in, g1, tm):
    t_tok = x.shape[0]
    nsteps = t_tok // tm

    qkv_w = ATTN_W + 2 * KV_W

    def body(dqkv_ref, du_ref, x_ref, dx1_ref, w_ref, g_ref, gx_ref, gw_ref, dg_ref):
        @pl.when(pl.program_id(0) == 0)
        def _():
            gw_ref[...] = jnp.zeros_like(gw_ref)
            dg_ref[...] = jnp.zeros_like(dg_ref)

        dqkv, du = dqkv_ref[...], du_ref[...]
        xv = x_ref[...]
        r = lax.rsqrt(jnp.mean(xv * xv, axis=-1, keepdims=True) + EPS)
        xn = xv * r
        g = g_ref[...]
        dh = _mm(dqkv, w_ref[:qkv_w]) + _mm(du, w_ref[qkv_w:])
        dg_ref[...] += jnp.sum(dh * xn, axis=0, keepdims=True)
        gx_ref[...] = dx1_ref[...] + _rms_bwd(dh * g, xn, r)
        h = (xn * g).astype(MXU_DTYPE)
        gw_ref[:qkv_w] += _mm_tn(dqkv, h)
        gw_ref[qkv_w:] += _mm_tn(du, h)

    row = lambda w: pl.BlockSpec((tm, w), lambda i: (i, 0))
    return pl.pallas_call(
        body, name="bwd_inproj", grid=(nsteps,),
        in_specs=[row(qkv_w), row(POOL_W), row(D_MODEL), row(D_MODEL), _resident((IN_W, D_MODEL)), _const((1, D_MODEL))],
        out_specs=(row(D_MODEL), _const((IN_W, D_MODEL)), _const((1, D_MODEL))),
        out_shape=(jax.ShapeDtypeStruct((t_tok, D_MODEL), F32), jax.ShapeDtypeStruct((IN_W, D_MODEL), F32),
                   jax.ShapeDtypeStruct((1, D_MODEL), F32)),
        compiler_params=_cp(dimension_semantics=("arbitrary",)),
    )(dqkv, du, x, dx1, w_in, g1)


class _NoComm:
    def start(self):
        return ()

    def attn_done(self, attn):
        return ()

    def rest_of_weights(self, pool):
        raise NotImplementedError

    def mlp_grads_ready(self, gw_up4, gw_down4):
        return ()

    def outproj_done(self, gw_out):
        return ()


def _local_step(x, target, w_in, comm, g1, sinks, w_pool, pool_scale, g2, gf, seq):
    tm = min(512, seq)
    tm_mlp = min(256, seq)
    cos_t, sin_t = _rope_tables(seq)
    sink_rows, bias = _sink_rows(sinks), _attn_bias()
    q, k, v, u = _fwd_inproj(x, g1, w_in, cos_t, sin_t, tm, comm.start())
    attn, lse = _fwd_attn(q, k, v, sink_rows, bias, seq)
    pool = _fwd_pool(u, w_pool, pool_scale, seq, comm.attn_done(attn))
    w_out, w_up4, w_down4 = comm.rest_of_weights(pool)
    h2, a, da, dx2b, dx1, dx1b, loss, dgf, dg2 = _mlp_fwd_bwd(x, attn, pool, target, w_out, w_up4, w_down4, g2, gf, tm_mlp)
    gw_up4, gw_down4 = _bwd_mlp_wgrads(h2, da, a, dx2b, min(1024, x.shape[0]))
    dattn, dpool, delta, gw_out = _bwd_outproj(dx1b, attn, pool, w_out, _head_selector(), tm,
                                               comm.mlp_grads_ready(gw_up4, gw_down4))
    dqkv, dsinks = _bwd_attn(q, k, v, dattn, lse, delta, sink_rows, bias, cos_t, sin_t, seq, comm.outproj_done(gw_out))
    du, dwp, dsc = _bwd_pool(u, dpool, w_pool, pool_scale, seq)
    gx, gw_in_t, dg1 = _bwd_inproj(dqkv, du, x, dx1, w_in, g1, tm)
    big = (gw_in_t.reshape(N_CHIPS, IN_SHARD, D_MODEL), gw_out.reshape(N_CHIPS, OUT_SHARD, D_MODEL), gw_up4, gw_down4)
    small = (dg1, dsinks, dwp.reshape(N_POOL * POOL_G, POOL_G), dsc, dg2, dgf, loss)
    return gx, big, small


def _sibling_exchange(arrs, pick_half, name):
    n = len(arrs)

    def out_shape(a):
        if pick_half:
            return (a.shape[0], a.shape[1] // 2) + a.shape[2:]
        return a.shape

    def body(*refs):
        srcs, dsts = refs[:n], refs[n:2 * n]
        send, recv = refs[2 * n:]
        x, y, c = lax.axis_index("x"), lax.axis_index("y"), lax.axis_index("c")
        cps = []
        for i in range(n):
            src = srcs[i]
            if pick_half:
                h = src.shape[1] // 2
                src = src.at[:, pl.ds((1 - c) * h, h)]
            cp = pltpu.make_async_remote_copy(src_ref=src, dst_ref=dsts[i], send_sem=send.at[i], recv_sem=recv.at[i],
                                              device_id=(x, y, 1 - c), device_id_type=MESH)
            cp.start()
            cps.append(cp)
        for cp in cps:
            cp.wait()

    hbm = pl.BlockSpec(memory_space=pltpu.HBM)
    return pl.pallas_call(
        body, name=name, out_shape=tuple(jax.ShapeDtypeStruct(out_shape(a), a.dtype) for a in arrs),
        in_specs=[hbm] * n, out_specs=tuple([hbm] * n),
        scratch_shapes=[pltpu.SemaphoreType.DMA((n,)), pltpu.SemaphoreType.DMA((n,))],
        compiler_params=_cp(),
    )(*arrs)


def _sibling_copies(srcs, lands, send, recv):
    x, y, c = lax.axis_index("x"), lax.axis_index("y"), lax.axis_index("c")
    cps = []
    for i in range(len(srcs)):
        h = srcs[i].shape[1] // 2
        cps.append(pltpu.make_async_remote_copy(
            src_ref=srcs[i].at[:, pl.ds((1 - c) * h, h)], dst_ref=lands[i], send_sem=send.at[i], recv_sem=recv.at[i],
            device_id=(x, y, 1 - c), device_id_type=MESH))
    return cps


def _sibling_start(arrs, name):
    n = len(arrs)

    def body(*refs):
        send, recv, token = refs[3 * n:3 * n + 3]
        for cp in _sibling_copies(refs[:n], refs[2 * n:3 * n], send, recv):
            cp.start()
        token[...] = jnp.zeros_like(token)

    half = lambda a: (a.shape[0], a.shape[1] // 2) + a.shape[2:]
    res = pl.pallas_call(
        body, name=name + "_start",
        out_shape=tuple(pltpu.HBM(a.shape, a.dtype) for a in arrs) + tuple(pltpu.HBM(half(a), a.dtype) for a in arrs) + (
            pltpu.SemaphoreType.DMA((n,)), pltpu.SemaphoreType.DMA((n,)), TOKEN),
        in_specs=[_HBM] * n, out_specs=tuple([_HBM] * (2 * n)) + (_SEM, _SEM, pl.BlockSpec(memory_space=pltpu.VMEM)),
        input_output_aliases={i: i for i in range(n)},
        compiler_params=_cp(has_side_effects=_EFFECT),
    )(*[pltpu.with_memory_space_constraint(a, pltpu.HBM) for a in arrs])
    return res[:n], res[n:2 * n], res[2 * n], res[2 * n + 1], res[2 * n + 2]


def _sibling_wait(arrs, lands, send, recv, after, name):
    n = len(arrs)

    def body(*refs):
        for cp in _sibling_copies(refs[:n], refs[n:2 * n], refs[2 * n], refs[2 * n + 1]):
            cp.wait_send()
            cp.wait_recv()

    res = pl.pallas_call(
        body, name=name + "_wait", out_shape=tuple(pltpu.HBM(a.shape, a.dtype) for a in list(arrs) + list(lands)),
        in_specs=[_HBM] * (2 * n) + [_SEM, _SEM, _ANY], out_specs=tuple([_HBM] * (2 * n)),
        input_output_aliases={i: i for i in range(2 * n)},
        compiler_params=_cp(has_side_effects=_EFFECT),
    )(*arrs, *lands, send, recv, after)
    return res[:n], res[n:]


def _row_block(rows):
    if rows <= 256:
        return rows
    for cand in (256, 128, 64, 32, 16, 8):
        if rows % cand == 0:
            return cand
    raise ValueError(rows)


def _chip_partial(g4, r4, c_arr, name):
    _, rows, cols = r4.shape
    rb = _row_block(rows)
    nb = rows // rb

    def body(c_ref, g_ref, r_ref, o_ref):
        o_ref[...] = (g_ref[...] + r_ref[...]).astype(o_ref.dtype)

    return pl.pallas_call(
        body, name=name,
        grid_spec=pltpu.PrefetchScalarGridSpec(
            num_scalar_prefetch=1, grid=(N_CHIPS, nb),
            in_specs=[pl.BlockSpec((1, rb, cols), lambda s, i, c: (s, c[0] * nb + i, 0)),
                      pl.BlockSpec((1, rb, cols), lambda s, i, c: (s, i, 0))],
            out_specs=pl.BlockSpec((1, rb, cols), lambda s, i, c: (s, i, 0))),
        out_shape=jax.ShapeDtypeStruct(r4.shape, BF16),
        compiler_params=_cp(dimension_semantics=("parallel", "parallel")),
    )(c_arr, g4, r4)


def _send_partials(parts):
    n = len(parts)

    def body(*refs):
        srcs, dsts = refs[:n], refs[n:2 * n]
        send, recv = refs[2 * n:]
        x, y, c = lax.axis_index("x"), lax.axis_index("y"), lax.axis_index("c")
        chips = [(1 - x, y), (x, 1 - y), (1 - x, 1 - y)]
        cps = []
        for i in range(n):
            for m, chip in enumerate(chips):
                cp = pltpu.make_async_remote_copy(
                    src_ref=srcs[i].at[2 * chip[0] + chip[1]], dst_ref=dsts[i].at[m],
                    send_sem=send.at[3 * i + m], recv_sem=recv.at[3 * i + m],
                    device_id=(chip[0], chip[1], c), device_id_type=MESH)
                cp.start()
                cps.append(cp)
        for cp in cps:
            cp.wait()

    hbm = pl.BlockSpec(memory_space=pltpu.HBM)
    return pl.pallas_call(
        body, name="send_partials",
        out_shape=tuple(jax.ShapeDtypeStruct((3,) + p.shape[1:], p.dtype) for p in parts),
        in_specs=[hbm] * n, out_specs=tuple([hbm] * n),
        scratch_shapes=[pltpu.SemaphoreType.DMA((3 * n,)), pltpu.SemaphoreType.DMA((3 * n,))],
        compiler_params=_cp(),
    )(*parts)


def _send_start(parts):
    n = len(parts)

    def body(*refs):
        srcs, lands = refs[:n], refs[2 * n:3 * n]
        send, recv, token = refs[3 * n:3 * n + 3]
        x, y, c = lax.axis_index("x"), lax.axis_index("y"), lax.axis_index("c")
        for i in range(n):
            for m, chip in enumerate(_other_chips(x, y)):
                pltpu.make_async_remote_copy(
                    src_ref=srcs[i].at[2 * chip[0] + chip[1]], dst_ref=lands[i].at[m],
                    send_sem=send.at[3 * i + m], recv_sem=recv.at[3 * i + m],
                    device_id=(chip[0], chip[1], c), device_id_type=MESH).start()
        token[...] = jnp.zeros_like(token)

    res = pl.pallas_call(
        body, name="send_start",
        out_shape=tuple(pltpu.HBM(p.shape, p.dtype) for p in parts) + tuple(pltpu.HBM((3,) + p.shape[1:], p.dtype) for p in parts) + (
            pltpu.SemaphoreType.DMA((3 * n,)), pltpu.SemaphoreType.DMA((3 * n,)), TOKEN),
        in_specs=[_HBM] * n, out_specs=tuple([_HBM] * (2 * n)) + (_SEM, _SEM, pl.BlockSpec(memory_space=pltpu.VMEM)),
        input_output_aliases={i: i for i in range(n)},
        compiler_params=_cp(has_side_effects=_EFFECT),
    )(*[pltpu.with_memory_space_constraint(p, pltpu.HBM) for p in parts])
    return res[:n], res[n:2 * n], res[2 * n], res[2 * n + 1], res[2 * n + 2]


def _send_wait(parts, lands, send, recv, after):
    n = len(parts)

    def body(*refs):
        srcs, ins = refs[:n], refs[n:2 * n]
        send_ref, recv_ref = refs[2 * n], refs[2 * n + 1]
        x, y, c = lax.axis_index("x"), lax.axis_index("y"), lax.axis_index("c")
        for i in range(n):
            for m, chip in enumerate(_other_chips(x, y)):
                cp = pltpu.make_async_remote_copy(
                    src_ref=srcs[i].at[2 * chip[0] + chip[1]], dst_ref=ins[i].at[m],
                    send_sem=send_ref.at[3 * i + m], recv_sem=recv_ref.at[3 * i + m],
                    device_id=(chip[0], chip[1], c), device_id_type=MESH)
                cp.wait_send()
                cp.wait_recv()

    res = pl.pallas_call(
        body, name="send_wait", out_shape=tuple(pltpu.HBM(a.shape, a.dtype) for a in list(parts) + list(lands)),
        in_specs=[_HBM] * (2 * n) + [_SEM, _SEM, _ANY], out_specs=tuple([_HBM] * (2 * n)),
        input_output_aliases={i: i for i in range(2 * n)},
        compiler_params=_cp(has_side_effects=_EFFECT),
    )(*parts, *lands, send, recv, after)
    return res[n:]


def _final_half(g4, r4, got3, jc_arr, name):
    _, rows, cols = r4.shape
    rb = _row_block(rows)
    nb = rows // rb

    def body(jc_ref, g_ref, r_ref, p_ref, o_ref):
        own = g_ref[0] + r_ref[0]
        o_ref[...] = ((own + p_ref[0].astype(F32)) + p_ref[1].astype(F32)) + p_ref[2].astype(F32)

    return pl.pallas_call(
        body, name=name,
        grid_spec=pltpu.PrefetchScalarGridSpec(
            num_scalar_prefetch=1, grid=(nb,),
            in_specs=[pl.BlockSpec((1, rb, cols), lambda i, jc: (jc[0], jc[1] * nb + i, 0)),
                      pl.BlockSpec((1, rb, cols), lambda i, jc: (jc[0], i, 0)),
                      pl.BlockSpec((3, rb, cols), lambda i, jc: (0, i, 0))],
            out_specs=pl.BlockSpec((rb, cols), lambda i, jc: (i, 0))),
        out_shape=jax.ShapeDtypeStruct((rows, cols), F32),
        compiler_params=_cp(dimension_semantics=("parallel",)),
    )(jc_arr, g4, r4, got3)


def _adamw_math(w, g, m, v):
    m2 = ADAM_B1 * m + (1.0 - ADAM_B1) * g
    v2 = ADAM_B2 * v + (1.0 - ADAM_B2) * (g * g)
    m_hat = m2 / (1.0 - ADAM_B1 ** ADAM_STEP)
    v_hat = v2 / (1.0 - ADAM_B2 ** ADAM_STEP)
    delta = -ADAM_LR * (m_hat / (jnp.sqrt(v_hat) + ADAM_EPS) + ADAM_WD * w)
    return delta, m2, v2


def _adamw_shard(mine, other, w, m, v, c_arr, name):
    rows, cols = w.shape
    half = rows // 2
    rb = _row_block(half)
    nb = half // rb

    def body(c_ref, a_ref, b_ref, w_ref, m_ref, v_ref, g_ref, d_ref, m2_ref, v2_ref):
        g = jnp.where(pl.program_id(0) == c_ref[0], a_ref[...], b_ref[...])
        delta, m2, v2 = _adamw_math(w_ref[...], g, m_ref[...], v_ref[...])
        g_ref[...] = g
        d_ref[...] = delta
        m2_ref[...] = m2
        v2_ref[...] = v2

    hb = pl.BlockSpec((rb, cols), lambda h, i, c: (i, 0))
    fb = pl.BlockSpec((rb, cols), lambda h, i, c: (h * nb + i, 0))
    shp = jax.ShapeDtypeStruct((rows, cols), F32)
    return pl.pallas_call(
        body, name=name,
        grid_spec=pltpu.PrefetchScalarGridSpec(num_scalar_prefetch=1, grid=(2, nb), in_specs=[hb, hb, fb, fb, fb],
                                               out_specs=(fb, fb, fb, fb)),
        out_shape=(shp, shp, shp, shp),
        compiler_params=_cp(dimension_semantics=("parallel", "parallel")),
    )(c_arr, mine, other, w, m, v)


def _small_allreduce(parts):
    n = len(parts)

    def body(*refs):
        p_refs, accs = refs[:n], refs[n:2 * n]
        bufs = refs[2 * n:3 * n]
        send, recv = refs[3 * n:]
        x, y, c = lax.axis_index("x"), lax.axis_index("y"), lax.axis_index("c")
        partners = [(x, y, 1 - c), (1 - x, y, c), (x, 1 - y, c)]
        for i in range(n):
            accs[i][...] = p_refs[i][...]
        for s, partner in enumerate(partners):
            cps = []
            for i in range(n):
                cp = pltpu.make_async_remote_copy(src_ref=accs[i], dst_ref=bufs[i].at[s], send_sem=send.at[3 * i + s],
                                                  recv_sem=recv.at[3 * i + s], device_id=partner, device_id_type=MESH)
                cp.start()
                cps.append(cp)
            for cp in cps:
                cp.wait()
            for i in range(n):
                accs[i][...] = accs[i][...] + bufs[i][s]

    vmem = pl.BlockSpec(memory_space=pltpu.VMEM)
    return pl.pallas_call(
        body, name="small_allreduce", out_shape=tuple(jax.ShapeDtypeStruct(p.shape, F32) for p in parts),
        in_specs=[vmem] * n, out_specs=tuple([vmem] * n),
        scratch_shapes=[pltpu.VMEM((3,) + p.shape, F32) for p in parts] + [
            pltpu.SemaphoreType.DMA((3 * n,)), pltpu.SemaphoreType.DMA((3 * n,))],
        compiler_params=_cp(),
    )(*parts)


def _small_adamw(reduced, params):
    n = len(reduced)
    n_w = len(params)

    def body(*refs):
        r_refs = refs[:n]
        wmv = refs[n:n + 3 * n_w]
        outs = refs[n + 3 * n_w:]
        outs[0][...] = r_refs[n - 1][0:1, 0:1]
        grads = [r_refs[0][...], r_refs[1][0:1, 0:N_Q_HEADS]] + [r_refs[i][...] for i in range(2, n_w)]
        for i in range(n_w):
            w_ref, m_ref, v_ref = wmv[3 * i:3 * i + 3]
            g_ref, d_ref, m2_ref, v2_ref = outs[1 + 4 * i:5 + 4 * i]
            delta, m2, v2 = _adamw_math(w_ref[...], grads[i], m_ref[...], v_ref[...])
            g_ref[...] = grads[i]
            d_ref[...] = delta
            m2_ref[...] = m2
            v2_ref[...] = v2

    flat = [a for p in params for a in p]
    vmem = pl.BlockSpec(memory_space=pltpu.VMEM)
    out_shape = [jax.ShapeDtypeStruct((1, 1), F32)]
    for p in params:
        out_shape += [jax.ShapeDtypeStruct(p[0].shape, F32)] * 4
    res = pl.pallas_call(
        body, name="small_adamw", out_shape=tuple(out_shape),
        in_specs=[vmem] * (n + len(flat)), out_specs=tuple([vmem] * len(out_shape)),
        compiler_params=_cp(),
    )(*reduced, *flat)
    return res[0], [res[1 + 4 * i:5 + 4 * i] for i in range(n_w)]


def kernel(x, attn_norm_g, w_in, attn_sinks, w_pool, pool_scale, w_out, mlp_norm_g, w_up, w_down, final_norm_g, loss_target, m_attn_norm_g, m_w_in, m_attn_sinks, m_w_pool, m_pool_scale, m_w_out, m_mlp_norm_g, m_w_up, m_w_down, m_final_norm_g, v_attn_norm_g, v_w_in, v_attn_sinks, v_w_pool, v_pool_scale, v_w_out, v_mlp_norm_g, v_w_up, v_w_down, v_final_norm_g):
    nseq, seq, d = x.shape
    c_idx = lax.axis_index("c").astype(jnp.int32)
    j_idx = (2 * lax.axis_index("x") + lax.axis_index("y")).astype(jnp.int32)
    c_arr = jnp.reshape(c_idx, (1,))
    jc_arr = jnp.stack([j_idx, c_idx])

    big_w = (w_in[0].T, w_out[0], w_up[0], w_down[0])
    big_m = (m_w_in[0].T, m_w_out[0], m_w_up[0], m_w_down[0])
    big_v = (v_w_in[0].T, v_w_out[0], v_w_up[0], v_w_down[0])
    (w_in4,) = _gather_weights(big_w[0])
    w_in_full = w_in4.reshape(IN_W, D_MODEL)
    class Comm(_NoComm):
        def start(self):
            self.lands, self.send, self.recv, token = _gather_start(big_w[1:], w_in4)
            return (token,)

        def attn_done(self, attn):
            arrived = _gather_wait(self.lands, self.send, self.recv, attn)
            self.lands, self.send, self.recv, token = _forward_start(arrived)
            return (token,)

        def rest_of_weights(self, pool):
            w_out4, w_up4, w_down4 = _forward_wait(self.lands, self.send, self.recv, pool)
            return w_out4.reshape(D_MODEL, D_MODEL), w_up4, w_down4

        def mlp_grads_ready(self, gw_up4, gw_down4):
            self.mlp = _sibling_start((gw_up4, gw_down4), "mlp_grads_to_sibling")
            return (self.mlp[4],)

        def outproj_done(self, gw_out):
            grads, lands, send, recv, _ = self.mlp
            gw_out4 = gw_out.reshape(N_CHIPS, OUT_SHARD, D_MODEL)
            mlp_grads, mlp_from_sib = _sibling_wait(grads, lands, send, recv, gw_out, "mlp_grads_to_sibling")
            self.grads = (gw_out4,) + tuple(mlp_grads)
            self.from_sib = tuple(_sibling_exchange((gw_out4,), True, "w_out_grad_to_sibling")) + tuple(mlp_from_sib)
            partials = [_chip_partial(g, r, c_arr, "chip_partial_" + nm)
                        for g, r, nm in zip(self.grads, self.from_sib, names[1:])]
            self.parts, self.part_lands, self.part_send, self.part_recv, token = _send_start(partials)
            return (token,)

    names = ("w_in", "w_out", "w_up", "w_down")
    comm = Comm()
    gx, big_g, small_g = _local_step(
        x.reshape(nseq * seq, d), loss_target.reshape(nseq * seq, d), w_in_full, comm,
        attn_norm_g, attn_sinks.reshape(N_Q_HEADS), w_pool[0], pool_scale, mlp_norm_g, final_norm_g.reshape(1, d), seq)
    big_g = tuple(big_g[:1]) + comm.grads

    from_sibling = list(_sibling_exchange(big_g[:1], True, "w_in_grad_to_sibling")) + list(comm.from_sib)
    received = list(_send_partials([_chip_partial(big_g[0], from_sibling[0], c_arr, "chip_partial_w_in")]))
    received += list(_send_wait(comm.parts, comm.part_lands, comm.part_send, comm.part_recv, received[0]))
    mine = [_final_half(g, r, p, jc_arr, "final_half_" + nm) for g, r, p, nm in zip(big_g, from_sibling, received, names)]
    other = _sibling_exchange(mine, False, "halves_to_sibling")
    big_out = [_adamw_shard(a, b, w, m, v, c_arr, "adamw_" + nm)
               for a, b, w, m, v, nm in zip(mine, other, big_w, big_m, big_v, names)]
    big_out[0] = [t.T for t in big_out[0]]

    wp_flat = lambda a: a.reshape(N_POOL * POOL_G, POOL_G)
    small_params = [
        (attn_norm_g, m_attn_norm_g, v_attn_norm_g),
        (attn_sinks, m_attn_sinks, v_attn_sinks),
        (wp_flat(w_pool), wp_flat(m_w_pool), wp_flat(v_w_pool)),
        (pool_scale, m_pool_scale, v_pool_scale),
        (mlp_norm_g, m_mlp_norm_g, v_mlp_norm_g),
        (final_norm_g.reshape(1, d), m_final_norm_g.reshape(1, d), v_final_norm_g.reshape(1, d)),
    ]
    loss, small_out = _small_adamw(_small_allreduce(small_g), small_params)

    def shaped(i, arr):
        return {1: w_in, 5: w_out, 7: w_up, 8: w_down, 0: attn_norm_g, 2: attn_sinks, 3: w_pool, 4: pool_scale,
                6: mlp_norm_g, 9: final_norm_g}[i].shape

    order = [small_out[0], big_out[0], small_out[1], small_out[2], small_out[3], big_out[1], small_out[4], big_out[2],
             big_out[3], small_out[5]]
    outs = [loss.reshape(()), gx.reshape(nseq, seq, d)]
    for kind in range(4):
        outs += [order[i][kind].reshape(shaped(i, None)) for i in range(10)]
    return tuple(outs)
```

```python
import jax
import jax.numpy as jnp
import numpy as np
from jax import lax
from jax.experimental import pallas as pl
from jax.experimental.pallas import tpu as pltpu

F32 = jnp.float32
BF16 = jnp.bfloat16
MXU_DTYPE = jnp.bfloat16

D_MODEL = 1024
HEAD_DIM = 64
N_Q_HEADS = 8
N_KV_HEADS = 2
Q_PER_KV = N_Q_HEADS // N_KV_HEADS
ATTN_W = N_Q_HEADS * HEAD_DIM
KV_W = N_KV_HEADS * HEAD_DIM
BLK = 128
POOL_WINDOWS = (2, 4, 8, 16)
N_POOL = len(POOL_WINDOWS)
POOL_W = D_MODEL - ATTN_W
POOL_G = POOL_W // N_POOL
IN_W = ATTN_W + 2 * KV_W + POOL_W
D_FF = 4 * D_MODEL
EPS = 1e-6
ROPE_THETA = 10000.0
N_CHIPS = 4
IN_SHARD = IN_W // N_CHIPS
OUT_SHARD = D_MODEL // N_CHIPS
FF_SHARD = D_FF // N_CHIPS
LANES = 128

ADAM_LR = 0.001
ADAM_B1 = 0.9
ADAM_B2 = 0.999
ADAM_EPS = 1e-08
ADAM_WD = 0.01
ADAM_STEP = 10

VMEM_LIMIT = 56 * 1024 * 1024
MESH = pl.DeviceIdType.MESH


def _cp(**kw):
    return pltpu.CompilerParams(vmem_limit_bytes=VMEM_LIMIT, **kw)


def _mm(a, b):
    return jnp.dot(a.astype(MXU_DTYPE), b.astype(MXU_DTYPE), preferred_element_type=F32)


def _mm_nt(a, b):
    return lax.dot_general(a.astype(MXU_DTYPE), b.astype(MXU_DTYPE), (((1,), (1,)), ((), ())),
                           preferred_element_type=F32)


def _mm_tn(a, b):
    return lax.dot_general(a.astype(MXU_DTYPE), b.astype(MXU_DTYPE), (((0,), (0,)), ((), ())),
                           preferred_element_type=F32)


def _resident(shape):
    nd = len(shape)
    return pl.BlockSpec(shape, lambda *_: (0,) * nd, pipeline_mode=pl.Buffered(1))


def _const(shape):
    nd = len(shape)
    return pl.BlockSpec(shape, lambda *_: (0,) * nd)


def _rope_tables(seq):
    half = HEAD_DIM // 2
    inv_freq = (ROPE_THETA ** (-np.arange(half, dtype=np.float32) / half)).astype(np.float32)
    ang = np.arange(seq, dtype=np.float32)[:, None] * inv_freq[None, :]
    cos, sin = np.cos(ang).astype(np.float32), np.sin(ang).astype(np.float32)
    cos_t = np.concatenate([cos, cos, cos, cos], axis=1)
    sin_t = np.concatenate([-sin, sin, -sin, sin], axis=1)
    return jnp.asarray(cos_t), jnp.asarray(sin_t)


def _swap_halves(xc):
    lane = lax.broadcasted_iota(jnp.int32, xc.shape, 1)
    return jnp.where((lane & 63) < 32, pltpu.roll(xc, 96, 1), pltpu.roll(xc, 32, 1))


def _gather_weights(*shards):
    nw = len(shards)

    def body(*refs):
        srcs = refs[0:nw]
        outs = refs[nw:2 * nw]
        stages = refs[2 * nw:3 * nw]
        send1, recv1, send2, recv2, lsem = refs[3 * nw:]
        x, y, c = lax.axis_index("x"), lax.axis_index("y"), lax.axis_index("c")
        j = 2 * x + y
        chips = [(1 - x, y), (x, 1 - y), (1 - x, 1 - y)]
        for k in range(nw):
            stages[k][...] = srcs[k][...].astype(BF16)
        local = [pltpu.make_async_copy(stages[k], outs[k].at[j], lsem.at[k]) for k in range(nw)]
        for cp in local:
            cp.start()

        def piece(k, chip, half):
            rows = shards[k].shape[0] // 2
            return outs[k].at[2 * chip[0] + chip[1], pl.ds(half * rows, rows)]

        first, passed = [], []
        for k in range(nw):
            rows = shards[k].shape[0] // 2
            for n, chip in enumerate(chips):
                cp = pltpu.make_async_remote_copy(
                    src_ref=stages[k].at[pl.ds(c * rows, rows)], dst_ref=piece(k, (x, y), c),
                    send_sem=send1.at[k * 3 + n], recv_sem=recv1.at[k * 3 + n],
                    device_id=(chip[0], chip[1], c), device_id_type=MESH)
                cp.start()
                first.append(cp)
        for k in range(nw):
            for n, chip in enumerate(chips):
                got = piece(k, chip, c)
                pltpu.make_async_remote_copy(
                    src_ref=got, dst_ref=got, send_sem=send1.at[k * 3 + n], recv_sem=recv1.at[k * 3 + n],
                    device_id=(chip[0], chip[1], c), device_id_type=MESH).wait_recv()
                cp = pltpu.make_async_remote_copy(
                    src_ref=got, dst_ref=got, send_sem=send2.at[k * 3 + n], recv_sem=recv2.at[k * 3 + n],
                    device_id=(x, y, 1 - c), device_id_type=MESH)
                cp.start()
                passed.append(cp)
        for k in range(nw):
            for n, chip in enumerate(chips):
                got = piece(k, chip, 1 - c)
                pltpu.make_async_remote_copy(
                    src_ref=got, dst_ref=got, send_sem=send2.at[k * 3 + n], recv_sem=recv2.at[k * 3 + n],
                    device_id=(x, y, 1 - c), device_id_type=MESH).wait_recv()
        for cp in first + passed:
            cp.wait_send()
        for cp in local:
            cp.wait()

    hbm = pl.BlockSpec(memory_space=pltpu.HBM)
    vmem = pl.BlockSpec(memory_space=pltpu.VMEM)
    return pl.pallas_call(
        body, name="gather_weights",
        out_shape=tuple(jax.ShapeDtypeStruct((N_CHIPS,) + s.shape, BF16) for s in shards),
        in_specs=[vmem] * nw, out_specs=tuple([hbm] * nw),
        scratch_shapes=[pltpu.VMEM(s.shape, BF16) for s in shards] + [
            pltpu.SemaphoreType.DMA((3 * nw,)), pltpu.SemaphoreType.DMA((3 * nw,)),
            pltpu.SemaphoreType.DMA((3 * nw,)), pltpu.SemaphoreType.DMA((3 * nw,)),
            pltpu.SemaphoreType.DMA((nw,))],
        compiler_params=_cp(),
    )(*shards)


_HBM = pl.BlockSpec(memory_space=pltpu.HBM)
_SEM = pl.BlockSpec(memory_space=pltpu.SEMAPHORE)
_ANY = pl.BlockSpec(memory_space=pl.ANY)
_EFFECT = pltpu.SideEffectType.DATAFLOW_SIDE_EFFECTING
TOKEN = jax.ShapeDtypeStruct((8, LANES), F32)


def _other_chips(x, y):
    return [(1 - x, y), (x, 1 - y), (1 - x, 1 - y)]


def _gather_start(shards, after):
    nw = len(shards)

    def body(*refs):
        srcs, lands = refs[:nw], refs[nw + 1:2 * nw + 1]
        send, recv, token = refs[2 * nw + 1:2 * nw + 4]
        stages = refs[2 * nw + 4:3 * nw + 4]
        lsem = refs[3 * nw + 4]
        x, y, c = lax.axis_index("x"), lax.axis_index("y"), lax.axis_index("c")
        j = 2 * x + y
        for k in range(nw):
            stages[k][...] = srcs[k][...].astype(BF16)
        local = [pltpu.make_async_copy(stages[k], lands[k].at[j], lsem.at[k]) for k in range(nw)]
        for cp in local:
            cp.start()
        for cp in local:
            cp.wait()
        for k in range(nw):
            rows = shards[k].shape[0] // 2
            mine = lands[k].at[j, pl.ds(c * rows, rows)]
            for n, chip in enumerate(_other_chips(x, y)):
                pltpu.make_async_remote_copy(
                    src_ref=mine, dst_ref=mine, send_sem=send.at[k * 3 + n], recv_sem=recv.at[k * 3 + n],
                    device_id=(chip[0], chip[1], c), device_id_type=MESH).start()
        token[...] = jnp.zeros_like(token)

    vmem = pl.BlockSpec(memory_space=pltpu.VMEM)
    res = pl.pallas_call(
        body, name="gather_start",
        out_shape=tuple(pltpu.HBM((N_CHIPS,) + s.shape, BF16) for s in shards) + (
            pltpu.SemaphoreType.DMA((3 * nw,)), pltpu.SemaphoreType.DMA((3 * nw,)), TOKEN),
        in_specs=[vmem] * nw + [_ANY], out_specs=tuple([_HBM] * nw) + (_SEM, _SEM, vmem),
        scratch_shapes=[pltpu.VMEM(s.shape, BF16) for s in shards] + [pltpu.SemaphoreType.DMA((nw,))],
        compiler_params=_cp(has_side_effects=_EFFECT),
    )(*shards, after)
    return res[:nw], res[nw], res[nw + 1], res[nw + 2]


def _gather_wait(lands, send, recv, after):
    nw = len(lands)

    def body(*refs):
        ins = refs[:nw]
        send_ref, recv_ref = refs[nw], refs[nw + 1]
        x, y, c = lax.axis_index("x"), lax.axis_index("y"), lax.axis_index("c")
        j = 2 * x + y
        for k in range(nw):
            rows = lands[k].shape[1] // 2
            mine = ins[k].at[j, pl.ds(c * rows, rows)]
            for n, chip in enumerate(_other_chips(x, y)):
                got = ins[k].at[2 * chip[0] + chip[1], pl.ds(c * rows, rows)]
                cp = pltpu.make_async_remote_copy(
                    src_ref=mine, dst_ref=got, send_sem=send_ref.at[k * 3 + n], recv_sem=recv_ref.at[k * 3 + n],
                    device_id=(chip[0], chip[1], c), device_id_type=MESH)
                cp.wait_send()
                cp.wait_recv()

    return pl.pallas_call(
        body, name="gather_wait", out_shape=tuple(pltpu.HBM(a.shape, a.dtype) for a in lands),
        in_specs=[_HBM] * nw + [_SEM, _SEM, _ANY], out_specs=tuple([_HBM] * nw),
        input_output_aliases={k: k for k in range(nw)},
        compiler_params=_cp(has_side_effects=_EFFECT),
    )(*lands, send, recv, after)


def _forward_copies(refs, nw, rows_of, send, recv):
    x, y, c = lax.axis_index("x"), lax.axis_index("y"), lax.axis_index("c")
    out = []
    for k in range(nw):
        rows = rows_of[k]
        for n, chip in enumerate(_other_chips(x, y)):
            got = refs[k].at[2 * chip[0] + chip[1], pl.ds(c * rows, rows)]
            theirs = refs[k].at[2 * chip[0] + chip[1], pl.ds((1 - c) * rows, rows)]
            out.append(pltpu.make_async_remote_copy(
                src_ref=got, dst_ref=got, send_sem=send.at[k * 3 + n], recv_sem=recv.at[k * 3 + n],
                device_id=(x, y, 1 - c), device_id_type=MESH))
            out.append(pltpu.make_async_remote_copy(
                src_ref=theirs, dst_ref=theirs, send_sem=send.at[k * 3 + n], recv_sem=recv.at[k * 3 + n],
                device_id=(x, y, 1 - c), device_id_type=MESH))
    return out[0::2], out[1::2]


def _forward_start(lands):
    nw = len(lands)
    rows_of = [a.shape[1] // 2 for a in lands]

    def body(*refs):
        send, recv, token = refs[2 * nw:2 * nw + 3]
        mine, _ = _forward_copies(refs[:nw], nw, rows_of, send, recv)
        for cp in mine:
            cp.start()
        token[...] = jnp.zeros_like(token)

    res = pl.pallas_call(
        body, name="forward_start",
        out_shape=tuple(pltpu.HBM(a.shape, a.dtype) for a in lands) + (
            pltpu.SemaphoreType.DMA((3 * nw,)), pltpu.SemaphoreType.DMA((3 * nw,)), TOKEN),
        in_specs=[_HBM] * nw, out_specs=tuple([_HBM] * nw) + (_SEM, _SEM, pl.BlockSpec(memory_space=pltpu.VMEM)),
        input_output_aliases={k: k for k in range(nw)},
        compiler_params=_cp(has_side_effects=_EFFECT),
    )(*lands)
    return res[:nw], res[nw], res[nw + 1], res[nw + 2]


def _forward_wait(lands, send, recv, after):
    nw = len(lands)
    rows_of = [a.shape[1] // 2 for a in lands]

    def body(*refs):
        mine, theirs = _forward_copies(refs[:nw], nw, rows_of, refs[nw], refs[nw + 1])
        for cp in mine:
            cp.wait_send()
        for cp in theirs:
            cp.wait_recv()

    return pl.pallas_call(
        body, name="forward_wait", out_shape=tuple(pltpu.HBM(a.shape, a.dtype) for a in lands),
        in_specs=[_HBM] * nw + [_SEM, _SEM, _ANY], out_specs=tuple([_HBM] * nw),
        input_output_aliases={k: k for k in range(nw)},
        compiler_params=_cp(has_side_effects=_EFFECT),
    )(*lands, send, recv, after)


def _after(tokens):
    tokens = [t for t in tokens if t is not None]
    return tokens, [_ANY] * len(tokens)


def _fwd_inproj(x, g1, w_in, cos_t, sin_t, tm, after=()):
    t_tok = x.shape[0]
    seq = cos_t.shape[0]
    per_seq = seq // tm
    tokens, token_specs = _after(after)

    def body(x_ref, g_ref, w_ref, cos_ref, sin_ref, *rest):
        q_ref, k_ref, v_ref, u_ref = rest[len(tokens):]
        xv = x_ref[...]
        r = lax.rsqrt(jnp.mean(xv * xv, axis=-1, keepdims=True) + EPS)
        h = (xv * r) * g_ref[...]
        proj = _mm_nt(h, w_ref[...])
        cos, sin = cos_ref[...], sin_ref[...]
        for cidx in range((ATTN_W + KV_W) // LANES):
            xc = proj[:, cidx * LANES:(cidx + 1) * LANES]
            rot = xc * cos + _swap_halves(xc) * sin
            if cidx < ATTN_W // LANES:
                q_ref[:, cidx * LANES:(cidx + 1) * LANES] = (rot * (HEAD_DIM ** -0.5)).astype(q_ref.dtype)
            else:
                k_ref[...] = rot.astype(k_ref.dtype)
        v_ref[...] = proj[:, ATTN_W + KV_W:ATTN_W + 2 * KV_W].astype(v_ref.dtype)
        u_ref[...] = proj[:, ATTN_W + 2 * KV_W:]

    row = lambda w: pl.BlockSpec((tm, w), lambda i: (i, 0))
    kv = row(KV_W)
    tab = pl.BlockSpec((tm, LANES), lambda i: (i % per_seq, 0))
    kv_shape = jax.ShapeDtypeStruct((t_tok, KV_W), MXU_DTYPE)
    return pl.pallas_call(
        body, name="fwd_inproj", grid=(t_tok // tm,),
        in_specs=[row(D_MODEL), _const((1, D_MODEL)), _resident((IN_W, D_MODEL)), tab, tab] + token_specs,
        out_specs=(row(ATTN_W), kv, kv, row(POOL_W)),
        out_shape=(jax.ShapeDtypeStruct((t_tok, ATTN_W), MXU_DTYPE), kv_shape, kv_shape,
                   jax.ShapeDtypeStruct((t_tok, POOL_W), F32)),
        compiler_params=_cp(dimension_semantics=("parallel",)),
    )(x, g1, w_in, cos_t, sin_t, *tokens)


MASKED = -1e30


def _attn_bias():
    b = lax.broadcasted_iota(jnp.int32, (2 * BLK, BLK), 0)
    a = lax.broadcasted_iota(jnp.int32, (2 * BLK, BLK), 1)
    own = (b >= BLK) & (b - BLK <= a)
    prev = (b < BLK) & (b > a)
    return jnp.stack([jnp.where(own, 0.0, MASKED), jnp.where(own | prev, 0.0, MASKED)]).astype(F32)


def _sink_rows(sinks):
    return jnp.repeat(sinks.reshape(N_KV_HEADS, Q_PER_KV), BLK, axis=1).reshape(N_KV_HEADS, 1, Q_PER_KV * BLK)


def _stack_heads(ref, r0, g):
    return jnp.concatenate(
        [ref[pl.ds(r0, BLK), (Q_PER_KV * g + h) * HEAD_DIM:(Q_PER_KV * g + h + 1) * HEAD_DIM] for h in range(Q_PER_KV)],
        axis=0)


def _kv_window(ref, p0, r0, g):
    sl = slice(g * HEAD_DIM, (g + 1) * HEAD_DIM)
    return jnp.concatenate([ref[pl.ds(p0, BLK), sl], ref[pl.ds(r0, BLK), sl]], axis=0)


def _scores_t(k_ref, q_ref, bias, p0, r0, g):
    kk = _kv_window(k_ref, p0, r0, g)
    qs = _stack_heads(q_ref, r0, g)
    st = _mm_nt(kk, qs) + jnp.concatenate([bias] * Q_PER_KV, axis=1)
    return st, kk, qs


def _head_rows(ref, r0, g):
    return jnp.concatenate([ref[pl.ds(Q_PER_KV * g + h, 1), pl.ds(r0, BLK)] for h in range(Q_PER_KV)], axis=1)


def _pairs_to_rows(xt):
    out = []
    for t in range(Q_PER_KV // 2):
        pair = jnp.concatenate([xt[:, (2 * t) * BLK:(2 * t + 1) * BLK], xt[:, (2 * t + 1) * BLK:(2 * t + 2) * BLK]], axis=0)
        out.append(pair.T)
    return out


def _shift_rows(x, k, seq):
    row = lax.broadcasted_iota(jnp.int32, x.shape, 0)
    if k > 0:
        return jnp.where(row >= k, pltpu.roll(x, k, 0), 0.0)
    return jnp.where(row < seq + k, pltpu.roll(x, seq + k, 0), 0.0)


def _window_sum(x, w, seq, forward):
    s, k = x, 1
    while k < w:
        s = s + _shift_rows(s, -k if forward else k, seq)
        k *= 2
    return s


def _inv_count(seq, w):
    pos = lax.broadcasted_iota(jnp.int32, (seq, 1), 0)
    return 1.0 / jnp.minimum(pos + 1, w).astype(F32)


def _fwd_attn(q, k, v, sink_rows, bias, seq):
    t_tok = q.shape[0]
    nblk = seq // BLK

    def body(q_ref, k_ref, v_ref, sink_ref, bias_ref, o_ref, lse_ref):
        def blk(i, carry):
            r0 = pl.multiple_of(i * BLK, BLK)
            p0 = pl.multiple_of(jnp.maximum(i - 1, 0) * BLK, BLK)
            bias = bias_ref[jnp.minimum(i, 1)]
            for g in range(N_KV_HEADS):
                st, _, _ = _scores_t(k_ref, q_ref, bias, p0, r0, g)
                sink = sink_ref[g]
                m = jnp.maximum(jnp.max(st, axis=0, keepdims=True), sink)
                p = jnp.exp(st - m).astype(MXU_DTYPE)
                vv = _kv_window(v_ref, p0, r0, g)
                ot = _mm_tn(jnp.concatenate([vv, jnp.ones_like(vv)], axis=1), p)
                den = ot[HEAD_DIM:HEAD_DIM + 1] + jnp.exp(sink - m)
                lse = m + jnp.log(den)
                for h in range(Q_PER_KV):
                    lse_ref[pl.ds(Q_PER_KV * g + h, 1), pl.ds(r0, BLK)] = lse[:, h * BLK:(h + 1) * BLK]
                ot = ot[:HEAD_DIM] * (1.0 / den)
                for t, tile in enumerate(_pairs_to_rows(ot)):
                    c0 = (Q_PER_KV * g + 2 * t) * HEAD_DIM
                    o_ref[pl.ds(r0, BLK), c0:c0 + LANES] = tile.astype(o_ref.dtype)
            return carry

        lax.fori_loop(0, nblk, blk, 0, unroll=4)

    row = lambda w: pl.BlockSpec((seq, w), lambda i: (i, 0))
    kv = row(KV_W)
    return pl.pallas_call(
        body, name="fwd_attn", grid=(t_tok // seq,),
        in_specs=[row(ATTN_W), kv, kv, _const((N_KV_HEADS, 1, Q_PER_KV * BLK)), _const((2, 2 * BLK, BLK))],
        out_specs=(row(ATTN_W), pl.BlockSpec((N_Q_HEADS, seq), lambda i: (0, i))),
        out_shape=(jax.ShapeDtypeStruct((t_tok, ATTN_W), MXU_DTYPE), jax.ShapeDtypeStruct((N_Q_HEADS, t_tok), F32)),
        compiler_params=_cp(dimension_semantics=("parallel",)),
    )(q, k, v, sink_rows, bias)


def _fwd_pool(u, w_pool, pool_scale, seq, after=()):
    t_tok = u.shape[0]
    tokens, token_specs = _after(after)

    def body(u_ref, wp_ref, sc_ref, *rest):
        o_ref = rest[-1]
        for gi, w in enumerate(POOL_WINDOWS):
            sl = slice(gi * POOL_G, (gi + 1) * POOL_G)
            ug = u_ref[:, sl]
            d = _window_sum(ug, w, seq, False) * _inv_count(seq, w) - ug
            o_ref[:, sl] = (_mm(d, wp_ref[gi]) * sc_ref[:, sl]).astype(o_ref.dtype)

    row = pl.BlockSpec((seq, POOL_W), lambda i: (i, 0))
    return pl.pallas_call(
        body, name="fwd_pool", grid=(t_tok // seq,),
        in_specs=[row, _const((N_POOL, POOL_G, POOL_G)), _const((1, POOL_W))] + token_specs, out_specs=row,
        out_shape=jax.ShapeDtypeStruct((t_tok, POOL_W), MXU_DTYPE),
        compiler_params=_cp(dimension_semantics=("parallel",)),
    )(u, w_pool, pool_scale, *tokens)


def _rms_bwd(dy_g, xn, r):
    return r * (dy_g - xn * jnp.mean(dy_g * xn, axis=-1, keepdims=True))


def _mlp_fwd_bwd(x, attn, pool, target, w_out, w_up4, w_down4, g2, gf, tm):
    t_tok = x.shape[0]

    def body(x_ref, attn_ref, pool_ref, tgt_ref, wo_ref, wu_ref, wd_ref, g2_ref, gf_ref,
             h2_ref, a_ref, da_ref, dx2b_ref, dx1_ref, dx1b_ref, loss_ref, dgf_ref, dg2_ref):
        @pl.when(pl.program_id(0) == 0)
        def _():
            loss_ref[...] = jnp.zeros_like(loss_ref)
            dgf_ref[...] = jnp.zeros_like(dgf_ref)
            dg2_ref[...] = jnp.zeros_like(dg2_ref)

        x1 = x_ref[...] + (_mm(attn_ref[...], wo_ref[:ATTN_W]) + _mm(pool_ref[...], wo_ref[ATTN_W:]))
        r2 = lax.rsqrt(jnp.mean(x1 * x1, axis=-1, keepdims=True) + EPS)
        xn1 = x1 * r2
        g2 = g2_ref[...]
        h2 = (xn1 * g2).astype(MXU_DTYPE)
        h2_ref[...] = h2
        acc = jnp.zeros((tm, D_MODEL), F32)
        for j in range(N_CHIPS):
            a = _mm(h2, wu_ref[j])
            a_ref[:, j * FF_SHARD:(j + 1) * FF_SHARD] = a.astype(a_ref.dtype)
            acc = acc + _mm(jnp.square(jnp.maximum(a, 0.0)), wd_ref[j])
        x2 = x1 + acc
        r3 = lax.rsqrt(jnp.mean(x2 * x2, axis=-1, keepdims=True) + EPS)
        xn2 = x2 * r3
        gf_v = gf_ref[...]
        err = xn2 * gf_v - tgt_ref[...]
        part = jnp.sum(err * err) * (0.5 / D_MODEL)
        first = (lax.broadcasted_iota(jnp.int32, loss_ref.shape, 0) == 0) & (lax.broadcasted_iota(jnp.int32, loss_ref.shape, 1) == 0)
        loss_ref[...] += jnp.where(first, part, 0.0)
        dy = err * (1.0 / D_MODEL)
        dgf_ref[...] += jnp.sum(dy * xn2, axis=0, keepdims=True)
        dx2 = _rms_bwd(dy * gf_v, xn2, r3)
        dx2b = dx2.astype(MXU_DTYPE)
        dx2b_ref[...] = dx2b
        dh2 = jnp.zeros((tm, D_MODEL), F32)
        for j in range(N_CHIPS):
            sl = slice(j * FF_SHARD, (j + 1) * FF_SHARD)
            dhid = _mm_nt(dx2b, wd_ref[j])
            da = (dhid * (2.0 * jnp.maximum(a_ref[:, sl].astype(F32), 0.0))).astype(MXU_DTYPE)
            da_ref[:, sl] = da
            dh2 = dh2 + _mm_nt(da, wu_ref[j])
        dg2_ref[...] += jnp.sum(dh2 * xn1, axis=0, keepdims=True)
        dx1 = dx2 + _rms_bwd(dh2 * g2, xn1, r2)
        dx1_ref[...] = dx1
        dx1b_ref[...] = dx1.astype(dx1b_ref.dtype)

    row = lambda w: pl.BlockSpec((tm, w), lambda i: (i, 0))
    vec = jax.ShapeDtypeStruct((1, D_MODEL), F32)
    return pl.pallas_call(
        body, name="mlp_fwd_bwd", grid=(t_tok // tm,),
        in_specs=[row(D_MODEL), row(ATTN_W), row(POOL_W), row(D_MODEL), _resident((D_MODEL, D_MODEL)),
                  _resident((N_CHIPS, D_MODEL, FF_SHARD)), _resident((N_CHIPS, FF_SHARD, D_MODEL)),
                  _const((1, D_MODEL)), _const((1, D_MODEL))],
        out_specs=(row(D_MODEL), row(D_FF), row(D_FF), row(D_MODEL), row(D_MODEL), row(D_MODEL),
                   _const((8, LANES)), _const((1, D_MODEL)), _const((1, D_MODEL))),
        out_shape=(jax.ShapeDtypeStruct((t_tok, D_MODEL), MXU_DTYPE), jax.ShapeDtypeStruct((t_tok, D_FF), MXU_DTYPE),
                   jax.ShapeDtypeStruct((t_tok, D_FF), MXU_DTYPE), jax.ShapeDtypeStruct((t_tok, D_MODEL), MXU_DTYPE),
                   jax.ShapeDtypeStruct((t_tok, D_MODEL), F32), jax.ShapeDtypeStruct((t_tok, D_MODEL), MXU_DTYPE),
                   jax.ShapeDtypeStruct((8, LANES), F32), vec, vec),
        compiler_params=_cp(dimension_semantics=("arbitrary",)),
    )(x, attn, pool, target, w_out, w_up4, w_down4, g2, gf)


def _bwd_mlp_wgrads(h2, da, a, dx2b, tk):
    t_tok = h2.shape[0]

    def body(h2_ref, da_ref, a_ref, dx2b_ref, gup_ref, gdn_ref):
        @pl.when(pl.program_id(1) == 0)
        def _():
            gup_ref[...] = jnp.zeros_like(gup_ref)
            gdn_ref[...] = jnp.zeros_like(gdn_ref)

        gup_ref[0] += _mm_tn(h2_ref[...], da_ref[...])
        hid = jnp.square(jnp.maximum(a_ref[...].astype(F32), 0.0))
        gdn_ref[0] += _mm_tn(hid, dx2b_ref[...])

    tok = pl.BlockSpec((tk, D_MODEL), lambda j, t: (t, 0))
    ffb = pl.BlockSpec((tk, FF_SHARD), lambda j, t: (t, j))
    wblk = pl.BlockSpec((1, D_MODEL, D_MODEL), lambda j, t: (j, 0, 0))
    return pl.pallas_call(
        body, name="bwd_mlp_wgrads", grid=(N_CHIPS, t_tok // tk),
        in_specs=[tok, ffb, ffb, tok], out_specs=(wblk, wblk),
        out_shape=(jax.ShapeDtypeStruct((N_CHIPS, D_MODEL, FF_SHARD), F32), jax.ShapeDtypeStruct((N_CHIPS, FF_SHARD, D_MODEL), F32)),
        compiler_params=_cp(dimension_semantics=("parallel", "arbitrary")),
    )(h2, da, a, dx2b)


def _head_selector():
    ch = lax.broadcasted_iota(jnp.int32, (ATTN_W, LANES), 0)
    col = lax.broadcasted_iota(jnp.int32, (ATTN_W, LANES), 1)
    return (ch // HEAD_DIM == col).astype(MXU_DTYPE)


def _bwd_outproj(dx1b, attn, pool, w_out, head_sel, tm, after=()):
    t_tok = dx1b.shape[0]
    tokens, token_specs = _after(after)

    def body(dx_ref, attn_ref, pool_ref, wo_ref, sel_ref, *rest):
        dattn_ref, dpool_ref, delta_ref, gwo_ref = rest[len(tokens):]

        @pl.when(pl.program_id(0) == 0)
        def _():
            gwo_ref[...] = jnp.zeros_like(gwo_ref)

        dx = dx_ref[...]
        attn = attn_ref[...]
        dattn = _mm_nt(dx, wo_ref[:ATTN_W])
        dattn_ref[...] = dattn.astype(dattn_ref.dtype)
        dpool_ref[...] = _mm_nt(dx, wo_ref[ATTN_W:])
        prod = dattn * attn.astype(F32)
        hi = prod.astype(MXU_DTYPE)
        lo = prod - hi.astype(F32)
        delta = _mm(hi, sel_ref[...]) + _mm(lo, sel_ref[...])
        delta_ref[...] = delta.T[:N_Q_HEADS]
        gwo_ref[:ATTN_W] += _mm_tn(attn, dx)
        gwo_ref[ATTN_W:] += _mm_tn(pool_ref[...], dx)

    row = lambda w: pl.BlockSpec((tm, w), lambda i: (i, 0))
    return pl.pallas_call(
        body, name="bwd_outproj", grid=(t_tok // tm,),
        in_specs=[row(D_MODEL), row(ATTN_W), row(POOL_W), _resident((D_MODEL, D_MODEL)), _const((ATTN_W, LANES))] + token_specs,
        out_specs=(row(ATTN_W), row(POOL_W), pl.BlockSpec((N_Q_HEADS, tm), lambda i: (0, i)), _const((D_MODEL, D_MODEL))),
        out_shape=(jax.ShapeDtypeStruct((t_tok, ATTN_W), MXU_DTYPE), jax.ShapeDtypeStruct((t_tok, POOL_W), F32),
                   jax.ShapeDtypeStruct((N_Q_HEADS, t_tok), F32), jax.ShapeDtypeStruct((D_MODEL, D_MODEL), F32)),
        compiler_params=_cp(dimension_semantics=("arbitrary",)),
    )(dx1b, attn, pool, w_out, head_sel, *tokens)


def _bwd_attn(q, k, v, dattn, lse, delta, sink_rows, bias, cos_t, sin_t, seq, after=()):
    t_tok = q.shape[0]
    nblk = seq // BLK
    qkv_w = ATTN_W + 2 * KV_W
    tokens, token_specs = _after(after)

    def unrope(d, cos, sin):
        return d * cos - _swap_halves(d) * sin

    def body(q_ref, k_ref, v_ref, do_ref, lse_ref, delta_ref, sink_ref, bias_ref, cos_ref, sin_ref, *rest):
        dqkv_ref, dsink_ref, dk_acc, dv_acc = rest[len(tokens):]

        @pl.when(pl.program_id(0) == 0)
        def _():
            dsink_ref[...] = jnp.zeros_like(dsink_ref)

        dk_acc[...] = jnp.zeros_like(dk_acc)
        dv_acc[...] = jnp.zeros_like(dv_acc)

        def blk(i, dsink):
            r0 = pl.multiple_of(i * BLK, BLK)
            p0 = pl.multiple_of(jnp.maximum(i - 1, 0) * BLK, BLK)
            bias = bias_ref[jnp.minimum(i, 1)]
            cos, sin = cos_ref[pl.ds(r0, BLK), :], sin_ref[pl.ds(r0, BLK), :]
            new = []
            for g in range(N_KV_HEADS):
                st, kk, qs = _scores_t(k_ref, q_ref, bias, p0, r0, g)
                lse_g, delta_g = _head_rows(lse_ref, r0, g), _head_rows(delta_ref, r0, g)
                pn = jnp.exp(st - lse_g)
                dos = _stack_heads(do_ref, r0, g)
                dst = pn * (_mm_nt(_kv_window(v_ref, p0, r0, g), dos) - delta_g)
                new.append(dsink[g] - jnp.exp(sink_ref[g] - lse_g) * delta_g)
                dqt = _mm_tn(kk, dst) * (HEAD_DIM ** -0.5)
                for t, tile in enumerate(_pairs_to_rows(dqt)):
                    c0 = (Q_PER_KV * g + 2 * t) * HEAD_DIM
                    dqkv_ref[pl.ds(r0, BLK), c0:c0 + LANES] = unrope(tile, cos, sin).astype(dqkv_ref.dtype)
                dkk = _mm(dst, qs)
                dvv = _mm(pn, dos)
                dk_acc[g, pl.ds(p0, BLK), :] += dkk[:BLK]
                dk_acc[g, pl.ds(r0, BLK), :] += dkk[BLK:]
                dv_acc[g, pl.ds(p0, BLK), :] += dvv[:BLK]
                dv_acc[g, pl.ds(r0, BLK), :] += dvv[BLK:]
            return tuple(new)

        zero = jnp.zeros((1, Q_PER_KV * BLK), F32)
        dsink = lax.fori_loop(0, nblk // 2, lambda i2, acc: blk(2 * i2 + 1, blk(2 * i2, acc)), (zero,) * N_KV_HEADS)
        lane = lax.broadcasted_iota(jnp.int32, dsink_ref.shape, 1)
        row = lax.broadcasted_iota(jnp.int32, dsink_ref.shape, 0)
        tile = jnp.zeros(dsink_ref.shape, F32)
        for g in range(N_KV_HEADS):
            for h in range(Q_PER_KV):
                tot = jnp.sum(dsink[g][:, h * BLK:(h + 1) * BLK])
                tile = tile + jnp.where((row == 0) & (lane == Q_PER_KV * g + h), tot, 0.0)
        dsink_ref[...] += tile
        dk = jnp.concatenate([dk_acc[g] for g in range(N_KV_HEADS)], axis=1)
        dqkv_ref[:, ATTN_W:ATTN_W + KV_W] = unrope(dk, cos_ref[...], sin_ref[...]).astype(dqkv_ref.dtype)
        dqkv_ref[:, ATTN_W + KV_W:] = jnp.concatenate([dv_acc[g] for g in range(N_KV_HEADS)], axis=1).astype(dqkv_ref.dtype)

    row = lambda w: pl.BlockSpec((seq, w), lambda i: (i, 0))
    kv = row(KV_W)
    per_head = pl.BlockSpec((N_Q_HEADS, seq), lambda i: (0, i))
    return pl.pallas_call(
        body, name="bwd_attn", grid=(t_tok // seq,),
        in_specs=[row(ATTN_W), kv, kv, row(ATTN_W), per_head, per_head, _const((N_KV_HEADS, 1, Q_PER_KV * BLK)),
                  _const((2, 2 * BLK, BLK)), _resident((seq, LANES)), _resident((seq, LANES))] + token_specs,
        out_specs=(row(qkv_w), _const((8, LANES))),
        out_shape=(jax.ShapeDtypeStruct((t_tok, qkv_w), MXU_DTYPE), jax.ShapeDtypeStruct((8, LANES), F32)),
        scratch_shapes=[pltpu.VMEM((N_KV_HEADS, seq, HEAD_DIM), F32), pltpu.VMEM((N_KV_HEADS, seq, HEAD_DIM), F32)],
        compiler_params=_cp(dimension_semantics=("arbitrary",)),
    )(q, k, v, dattn, lse, delta, sink_rows, bias, cos_t, sin_t, *tokens)


def _bwd_pool(u, dpool, w_pool, pool_scale, seq):
    t_tok = u.shape[0]

    def body(u_ref, dp_ref, wp_ref, sc_ref, du_ref, dwp_ref, dsc_ref):
        @pl.when(pl.program_id(0) == 0)
        def _():
            dwp_ref[...] = jnp.zeros_like(dwp_ref)
            dsc_ref[...] = jnp.zeros_like(dsc_ref)

        for gi, w in enumerate(POOL_WINDOWS):
            sl = slice(gi * POOL_G, (gi + 1) * POOL_G)
            ug = u_ref[:, sl]
            inv = _inv_count(seq, w)
            d = (_window_sum(ug, w, seq, False) * inv - ug).astype(MXU_DTYPE)
            y = _mm(d, wp_ref[gi])
            dpool = dp_ref[:, sl]
            dsc_ref[:, sl] += jnp.sum(y * dpool, axis=0, keepdims=True)
            dy = (dpool * sc_ref[:, sl]).astype(MXU_DTYPE)
            dwp_ref[gi] += _mm_tn(d, dy)
            dd = _mm_nt(dy, wp_ref[gi])
            du_ref[:, sl] = (_window_sum(dd * inv, w, seq, True) - dd).astype(du_ref.dtype)

    row = pl.BlockSpec((seq, POOL_W), lambda i: (i, 0))
    return pl.pallas_call(
        body, name="bwd_pool", grid=(t_tok // seq,),
        in_specs=[row, row, _const((N_POOL, POOL_G, POOL_G)), _const((1, POOL_W))],
        out_specs=(row, _const((N_POOL, POOL_G, POOL_G)), _const((1, POOL_W))),
        out_shape=(jax.ShapeDtypeStruct((t_tok, POOL_W), MXU_DTYPE), jax.ShapeDtypeStruct((N_POOL, POOL_G, POOL_G), F32),
                   jax.ShapeDtypeStruct((1, POOL_W), F32)),
        compiler_params=_cp(dimension_semantics=("arbitrary",)),
    )(u, dpool, w_pool, pool_scale)


def _bwd_inproj(dqkv, du, x, dx1, w_in, g1, tm):
    t_tok = x.shape[0]
    nsteps = t_tok // tm

    qkv_w = ATTN_W + 2 * KV_W

    def body(dqkv_ref, du_ref, x_ref, dx1_ref, w_ref, g_ref, gx_ref, gw_ref, dg_ref):
        @pl.when(pl.program_id(0) == 0)
        def _():
            gw_ref[...] = jnp.zeros_like(gw_ref)
            dg_ref[...] = jnp.zeros_like(dg_ref)

        dqkv, du = dqkv_ref[...], du_ref[...]
        xv = x_ref[...]
        r = lax.rsqrt(jnp.mean(xv * xv, axis=-1, keepdims=True) + EPS)
        xn = xv * r
        g = g_ref[...]
        dh = _mm(dqkv, w_ref[:qkv_w]) + _mm(du, w_ref[qkv_w:])
        dg_ref[...] += jnp.sum(dh * xn, axis=0, keepdims=True)
        gx_ref[...] = dx1_ref[...] + _rms_bwd(dh * g, xn, r)
        h = (xn * g).astype(MXU_DTYPE)
        gw_ref[:qkv_w] += _mm_tn(dqkv, h)
        gw_ref[qkv_w:] += _mm_tn(du, h)

    row = lambda w: pl.BlockSpec((tm, w), lambda i: (i, 0))
    return pl.pallas_call(
        body, name="bwd_inproj", grid=(nsteps,),
        in_specs=[row(qkv_w), row(POOL_W), row(D_MODEL), row(D_MODEL), _resident((IN_W, D_MODEL)), _const((1, D_MODEL))],
        out_specs=(row(D_MODEL), _const((IN_W, D_MODEL)), _const((1, D_MODEL))),
        out_shape=(jax.ShapeDtypeStruct((t_tok, D_MODEL), F32), jax.ShapeDtypeStruct((IN_W, D_MODEL), F32),
                   jax.ShapeDtypeStruct((1, D_MODEL), F32)),
        compiler_params=_cp(dimension_semantics=("arbitrary",)),
    )(dqkv, du, x, dx1, w_in, g1)


class _NoComm:
    def start(self):
        return ()

    def attn_done(self, attn):
        return ()

    def rest_of_weights(self, pool):
        raise NotImplementedError

    def mlp_grads_ready(self, gw_up4, gw_down4):
        return ()

    def outproj_done(self, gw_out):
        return ()


def _local_step(x, target, w_in, comm, g1, sinks, w_pool, pool_scale, g2, gf, seq):
    tm = min(512, seq)
    tm_big = min(1024, seq)
    tm_mlp = min(256, seq)
    cos_t, sin_t = _rope_tables(seq)
    sink_rows, bias = _sink_rows(sinks), _attn_bias()
    q, k, v, u = _fwd_inproj(x, g1, w_in, cos_t, sin_t, tm_big, comm.start())
    attn, lse = _fwd_attn(q, k, v, sink_rows, bias, seq)
    pool = _fwd_pool(u, w_pool, pool_scale, seq, comm.attn_done(attn))
    w_out, w_up4, w_down4 = comm.rest_of_weights(pool)
    h2, a, da, dx2b, dx1, dx1b, loss, dgf, dg2 = _mlp_fwd_bwd(x, attn, pool, target, w_out, w_up4, w_down4, g2, gf, tm_mlp)
    gw_up4, gw_down4 = _bwd_mlp_wgrads(h2, da, a, dx2b, min(1024, x.shape[0]))
    dattn, dpool, delta, gw_out = _bwd_outproj(dx1b, attn, pool, w_out, _head_selector(), tm_big,
                                               comm.mlp_grads_ready(gw_up4, gw_down4))
    dqkv, dsinks = _bwd_attn(q, k, v, dattn, lse, delta, sink_rows, bias, cos_t, sin_t, seq, comm.outproj_done(gw_out))
    du, dwp, dsc = _bwd_pool(u, dpool, w_pool, pool_scale, seq)
    gx, gw_in_t, dg1 = _bwd_inproj(dqkv, du, x, dx1, w_in, g1, tm)
    big = (gw_in_t.reshape(N_CHIPS, IN_SHARD, D_MODEL), gw_out.reshape(N_CHIPS, OUT_SHARD, D_MODEL), gw_up4, gw_down4)
    small = (dg1, dsinks, dwp.reshape(N_POOL * POOL_G, POOL_G), dsc, dg2, dgf, loss)
    return gx, big, small


def _sibling_exchange(arrs, pick_half, name):
    n = len(arrs)

    def out_shape(a):
        if pick_half:
            return (a.shape[0], a.shape[1] // 2) + a.shape[2:]
        return a.shape

    def body(*refs):
        srcs, dsts = refs[:n], refs[n:2 * n]
        send, recv = refs[2 * n:]
        x, y, c = lax.axis_index("x"), lax.axis_index("y"), lax.axis_index("c")
        cps = []
        for i in range(n):
            src = srcs[i]
            if pick_half:
                h = src.shape[1] // 2
                src = src.at[:, pl.ds((1 - c) * h, h)]
            cp = pltpu.make_async_remote_copy(src_ref=src, dst_ref=dsts[i], send_sem=send.at[i], recv_sem=recv.at[i],
                                              device_id=(x, y, 1 - c), device_id_type=MESH)
            cp.start()
            cps.append(cp)
        for cp in cps:
            cp.wait()

    hbm = pl.BlockSpec(memory_space=pltpu.HBM)
    return pl.pallas_call(
        body, name=name, out_shape=tuple(jax.ShapeDtypeStruct(out_shape(a), a.dtype) for a in arrs),
        in_specs=[hbm] * n, out_specs=tuple([hbm] * n),
        scratch_shapes=[pltpu.SemaphoreType.DMA((n,)), pltpu.SemaphoreType.DMA((n,))],
        compiler_params=_cp(),
    )(*arrs)


def _sibling_copies(srcs, lands, send, recv):
    x, y, c = lax.axis_index("x"), lax.axis_index("y"), lax.axis_index("c")
    cps = []
    for i in range(len(srcs)):
        h = srcs[i].shape[1] // 2
        cps.append(pltpu.make_async_remote_copy(
            src_ref=srcs[i].at[:, pl.ds((1 - c) * h, h)], dst_ref=lands[i], send_sem=send.at[i], recv_sem=recv.at[i],
            device_id=(x, y, 1 - c), device_id_type=MESH))
    return cps


def _sibling_start(arrs, name):
    n = len(arrs)

    def body(*refs):
        send, recv, token = refs[3 * n:3 * n + 3]
        for cp in _sibling_copies(refs[:n], refs[2 * n:3 * n], send, recv):
            cp.start()
        token[...] = jnp.zeros_like(token)

    half = lambda a: (a.shape[0], a.shape[1] // 2) + a.shape[2:]
    res = pl.pallas_call(
        body, name=name + "_start",
        out_shape=tuple(pltpu.HBM(a.shape, a.dtype) for a in arrs) + tuple(pltpu.HBM(half(a), a.dtype) for a in arrs) + (
            pltpu.SemaphoreType.DMA((n,)), pltpu.SemaphoreType.DMA((n,)), TOKEN),
        in_specs=[_HBM] * n, out_specs=tuple([_HBM] * (2 * n)) + (_SEM, _SEM, pl.BlockSpec(memory_space=pltpu.VMEM)),
        input_output_aliases={i: i for i in range(n)},
        compiler_params=_cp(has_side_effects=_EFFECT),
    )(*[pltpu.with_memory_space_constraint(a, pltpu.HBM) for a in arrs])
    return res[:n], res[n:2 * n], res[2 * n], res[2 * n + 1], res[2 * n + 2]


def _sibling_wait(arrs, lands, send, recv, after, name):
    n = len(arrs)

    def body(*refs):
        for cp in _sibling_copies(refs[:n], refs[n:2 * n], refs[2 * n], refs[2 * n + 1]):
            cp.wait_send()
            cp.wait_recv()

    res = pl.pallas_call(
        body, name=name + "_wait", out_shape=tuple(pltpu.HBM(a.shape, a.dtype) for a in list(arrs) + list(lands)),
        in_specs=[_HBM] * (2 * n) + [_SEM, _SEM, _ANY], out_specs=tuple([_HBM] * (2 * n)),
        input_output_aliases={i: i for i in range(2 * n)},
        compiler_params=_cp(has_side_effects=_EFFECT),
    )(*arrs, *lands, send, recv, after)
    return res[:n], res[n:]


def _row_block(rows):
    if rows <= 256:
        return rows
    for cand in (256, 128, 64, 32, 16, 8):
        if rows % cand == 0:
            return cand
    raise ValueError(rows)


def _chip_partial(g4, r4, c_arr, name):
    _, rows, cols = r4.shape
    rb = _row_block(rows)
    nb = rows // rb

    def body(c_ref, g_ref, r_ref, o_ref):
        o_ref[...] = (g_ref[...] + r_ref[...]).astype(o_ref.dtype)

    return pl.pallas_call(
        body, name=name,
        grid_spec=pltpu.PrefetchScalarGridSpec(
            num_scalar_prefetch=1, grid=(N_CHIPS, nb),
            in_specs=[pl.BlockSpec((1, rb, cols), lambda s, i, c: (s, c[0] * nb + i, 0)),
                      pl.BlockSpec((1, rb, cols), lambda s, i, c: (s, i, 0))],
            out_specs=pl.BlockSpec((1, rb, cols), lambda s, i, c: (s, i, 0))),
        out_shape=jax.ShapeDtypeStruct(r4.shape, BF16),
        compiler_params=_cp(dimension_semantics=("parallel", "parallel")),
    )(c_arr, g4, r4)


def _send_partials(parts):
    n = len(parts)

    def body(*refs):
        srcs, dsts = refs[:n], refs[n:2 * n]
        send, recv = refs[2 * n:]
        x, y, c = lax.axis_index("x"), lax.axis_index("y"), lax.axis_index("c")
        chips = [(1 - x, y), (x, 1 - y), (1 - x, 1 - y)]
        cps = []
        for i in range(n):
            for m, chip in enumerate(chips):
                cp = pltpu.make_async_remote_copy(
                    src_ref=srcs[i].at[2 * chip[0] + chip[1]], dst_ref=dsts[i].at[m],
                    send_sem=send.at[3 * i + m], recv_sem=recv.at[3 * i + m],
                    device_id=(chip[0], chip[1], c), device_id_type=MESH)
                cp.start()
                cps.append(cp)
        for cp in cps:
            cp.wait()

    hbm = pl.BlockSpec(memory_space=pltpu.HBM)
    return pl.pallas_call(
        body, name="send_partials",
        out_shape=tuple(jax.ShapeDtypeStruct((3,) + p.shape[1:], p.dtype) for p in parts),
        in_specs=[hbm] * n, out_specs=tuple([hbm] * n),
        scratch_shapes=[pltpu.SemaphoreType.DMA((3 * n,)), pltpu.SemaphoreType.DMA((3 * n,))],
        compiler_params=_cp(),
    )(*parts)


def _send_start(parts):
    n = len(parts)

    def body(*refs):
        srcs, lands = refs[:n], refs[2 * n:3 * n]
        send, recv, token = refs[3 * n:3 * n + 3]
        x, y, c = lax.axis_index("x"), lax.axis_index("y"), lax.axis_index("c")
        for i in range(n):
            for m, chip in enumerate(_other_chips(x, y)):
                pltpu.make_async_remote_copy(
                    src_ref=srcs[i].at[2 * chip[0] + chip[1]], dst_ref=lands[i].at[m],
                    send_sem=send.at[3 * i + m], recv_sem=recv.at[3 * i + m],
                    device_id=(chip[0], chip[1], c), device_id_type=MESH).start()
        token[...] = jnp.zeros_like(token)

    res = pl.pallas_call(
        body, name="send_start",
        out_shape=tuple(pltpu.HBM(p.shape, p.dtype) for p in parts) + tuple(pltpu.HBM((3,) + p.shape[1:], p.dtype) for p in parts) + (
            pltpu.SemaphoreType.DMA((3 * n,)), pltpu.SemaphoreType.DMA((3 * n,)), TOKEN),
        in_specs=[_HBM] * n, out_specs=tuple([_HBM] * (2 * n)) + (_SEM, _SEM, pl.BlockSpec(memory_space=pltpu.VMEM)),
        input_output_aliases={i: i for i in range(n)},
        compiler_params=_cp(has_side_effects=_EFFECT),
    )(*[pltpu.with_memory_space_constraint(p, pltpu.HBM) for p in parts])
    return res[:n], res[n:2 * n], res[2 * n], res[2 * n + 1], res[2 * n + 2]


def _send_wait(parts, lands, send, recv, after):
    n = len(parts)

    def body(*refs):
        srcs, ins = refs[:n], refs[n:2 * n]
        send_ref, recv_ref = refs[2 * n], refs[2 * n + 1]
        x, y, c = lax.axis_index("x"), lax.axis_index("y"), lax.axis_index("c")
        for i in range(n):
            for m, chip in enumerate(_other_chips(x, y)):
                cp = pltpu.make_async_remote_copy(
                    src_ref=srcs[i].at[2 * chip[0] + chip[1]], dst_ref=ins[i].at[m],
                    send_sem=send_ref.at[3 * i + m], recv_sem=recv_ref.at[3 * i + m],
                    device_id=(chip[0], chip[1], c), device_id_type=MESH)
                cp.wait_send()
                cp.wait_recv()

    res = pl.pallas_call(
        body, name="send_wait", out_shape=tuple(pltpu.HBM(a.shape, a.dtype) for a in list(parts) + list(lands)),
        in_specs=[_HBM] * (2 * n) + [_SEM, _SEM, _ANY], out_specs=tuple([_HBM] * (2 * n)),
        input_output_aliases={i: i for i in range(2 * n)},
        compiler_params=_cp(has_side_effects=_EFFECT),
    )(*parts, *lands, send, recv, after)
    return res[n:]


def _final_half(g4, r4, got3, jc_arr, name):
    _, rows, cols = r4.shape
    rb = _row_block(rows)
    nb = rows // rb

    def body(jc_ref, g_ref, r_ref, p_ref, o_ref):
        own = g_ref[0] + r_ref[0]
        o_ref[...] = ((own + p_ref[0].astype(F32)) + p_ref[1].astype(F32)) + p_ref[2].astype(F32)

    return pl.pallas_call(
        body, name=name,
        grid_spec=pltpu.PrefetchScalarGridSpec(
            num_scalar_prefetch=1, grid=(nb,),
            in_specs=[pl.BlockSpec((1, rb, cols), lambda i, jc: (jc[0], jc[1] * nb + i, 0)),
                      pl.BlockSpec((1, rb, cols), lambda i, jc: (jc[0], i, 0)),
                      pl.BlockSpec((3, rb, cols), lambda i, jc: (0, i, 0))],
            out_specs=pl.BlockSpec((rb, cols), lambda i, jc: (i, 0))),
        out_shape=jax.ShapeDtypeStruct((rows, cols), F32),
        compiler_params=_cp(dimension_semantics=("parallel",)),
    )(jc_arr, g4, r4, got3)


def _adamw_math(w, g, m, v):
    m2 = ADAM_B1 * m + (1.0 - ADAM_B1) * g
    v2 = ADAM_B2 * v + (1.0 - ADAM_B2) * (g * g)
    m_hat = m2 / (1.0 - ADAM_B1 ** ADAM_STEP)
    v_hat = v2 / (1.0 - ADAM_B2 ** ADAM_STEP)
    delta = -ADAM_LR * (m_hat / (jnp.sqrt(v_hat) + ADAM_EPS) + ADAM_WD * w)
    return delta, m2, v2


def _adamw_shard(mine, other, w, m, v, c_arr, name):
    rows, cols = w.shape
    half = rows // 2
    rb = _row_block(half)
    nb = half // rb

    def body(c_ref, a_ref, b_ref, w_ref, m_ref, v_ref, g_ref, d_ref, m2_ref, v2_ref):
        g = jnp.where(pl.program_id(0) == c_ref[0], a_ref[...], b_ref[...])
        delta, m2, v2 = _adamw_math(w_ref[...], g, m_ref[...], v_ref[...])
        g_ref[...] = g
        d_ref[...] = delta
        m2_ref[...] = m2
        v2_ref[...] = v2

    hb = pl.BlockSpec((rb, cols), lambda h, i, c: (i, 0))
    fb = pl.BlockSpec((rb, cols), lambda h, i, c: (h * nb + i, 0))
    shp = jax.ShapeDtypeStruct((rows, cols), F32)
    return pl.pallas_call(
        body, name=name,
        grid_spec=pltpu.PrefetchScalarGridSpec(num_scalar_prefetch=1, grid=(2, nb), in_specs=[hb, hb, fb, fb, fb],
                                               out_specs=(fb, fb, fb, fb)),
        out_shape=(shp, shp, shp, shp),
        compiler_params=_cp(dimension_semantics=("parallel", "parallel")),
    )(c_arr, mine, other, w, m, v)


def _small_allreduce(parts):
    n = len(parts)

    def body(*refs):
        p_refs, accs = refs[:n], refs[n:2 * n]
        bufs = refs[2 * n:3 * n]
        send, recv = refs[3 * n:]
        x, y, c = lax.axis_index("x"), lax.axis_index("y"), lax.axis_index("c")
        partners = [(x, y, 1 - c), (1 - x, y, c), (x, 1 - y, c)]
        for i in range(n):
            accs[i][...] = p_refs[i][...]
        for s, partner in enumerate(partners):
            cps = []
            for i in range(n):
                cp = pltpu.make_async_remote_copy(src_ref=accs[i], dst_ref=bufs[i].at[s], send_sem=send.at[3 * i + s],
                                                  recv_sem=recv.at[3 * i + s], device_id=partner, device_id_type=MESH)
                cp.start()
                cps.append(cp)
            for cp in cps:
                cp.wait()
            for i in range(n):
                accs[i][...] = accs[i][...] + bufs[i][s]

    vmem = pl.BlockSpec(memory_space=pltpu.VMEM)
    return pl.pallas_call(
        body, name="small_allreduce", out_shape=tuple(jax.ShapeDtypeStruct(p.shape, F32) for p in parts),
        in_specs=[vmem] * n, out_specs=tuple([vmem] * n),
        scratch_shapes=[pltpu.VMEM((3,) + p.shape, F32) for p in parts] + [
            pltpu.SemaphoreType.DMA((3 * n,)), pltpu.SemaphoreType.DMA((3 * n,))],
        compiler_params=_cp(),
    )(*parts)


def _small_adamw(reduced, params):
    n = len(reduced)
    n_w = len(params)

    def body(*refs):
        r_refs = refs[:n]
        wmv = refs[n:n + 3 * n_w]
        outs = refs[n + 3 * n_w:]
        outs[0][...] = r_refs[n - 1][0:1, 0:1]
        grads = [r_refs[0][...], r_refs[1][0:1, 0:N_Q_HEADS]] + [r_refs[i][...] for i in range(2, n_w)]
        for i in range(n_w):
            w_ref, m_ref, v_ref = wmv[3 * i:3 * i + 3]
            g_ref, d_ref, m2_ref, v2_ref = outs[1 + 4 * i:5 + 4 * i]
            delta, m2, v2 = _adamw_math(w_ref[...], grads[i], m_ref[...], v_ref[...])
            g_ref[...] = grads[i]
            d_ref[...] = delta
            m2_ref[...] = m2
            v2_ref[...] = v2

    flat = [a for p in params for a in p]
    vmem = pl.BlockSpec(memory_space=pltpu.VMEM)
    out_shape = [jax.ShapeDtypeStruct((1, 1), F32)]
    for p in params:
        out_shape += [jax.ShapeDtypeStruct(p[0].shape, F32)] * 4
    res = pl.pallas_call(
        body, name="small_adamw", out_shape=tuple(out_shape),
        in_specs=[vmem] * (n + len(flat)), out_specs=tuple([vmem] * len(out_shape)),
        compiler_params=_cp(),
    )(*reduced, *flat)
    return res[0], [res[1 + 4 * i:5 + 4 * i] for i in range(n_w)]


def kernel(x, attn_norm_g, w_in, attn_sinks, w_pool, pool_scale, w_out, mlp_norm_g, w_up, w_down, final_norm_g, loss_target, m_attn_norm_g, m_w_in, m_attn_sinks, m_w_pool, m_pool_scale, m_w_out, m_mlp_norm_g, m_w_up, m_w_down, m_final_norm_g, v_attn_norm_g, v_w_in, v_attn_sinks, v_w_pool, v_pool_scale, v_w_out, v_mlp_norm_g, v_w_up, v_w_down, v_final_norm_g):
    nseq, seq, d = x.shape
    c_idx = lax.axis_index("c").astype(jnp.int32)
    j_idx = (2 * lax.axis_index("x") + lax.axis_index("y")).astype(jnp.int32)
    c_arr = jnp.reshape(c_idx, (1,))
    jc_arr = jnp.stack([j_idx, c_idx])

    big_w = (w_in[0].T, w_out[0], w_up[0], w_down[0])
    big_m = (m_w_in[0].T, m_w_out[0], m_w_up[0], m_w_down[0])
    big_v = (v_w_in[0].T, v_w_out[0], v_w_up[0], v_w_down[0])
    (w_in4,) = _gather_weights(big_w[0])
    w_in_full = w_in4.reshape(IN_W, D_MODEL)
    class Comm(_NoComm):
        def start(self):
            self.lands, self.send, self.recv, token = _gather_start(big_w[1:], w_in4)
            return (token,)

        def attn_done(self, attn):
            arrived = _gather_wait(self.lands, self.send, self.recv, attn)
            self.lands, self.send, self.recv, token = _forward_start(arrived)
            return (token,)

        def rest_of_weights(self, pool):
            w_out4, w_up4, w_down4 = _forward_wait(self.lands, self.send, self.recv, pool)
            return w_out4.reshape(D_MODEL, D_MODEL), w_up4, w_down4

        def mlp_grads_ready(self, gw_up4, gw_down4):
            self.mlp = _sibling_start((gw_up4, gw_down4), "mlp_grads_to_sibling")
            return (self.mlp[4],)

        def outproj_done(self, gw_out):
            grads, lands, send, recv, _ = self.mlp
            gw_out4 = gw_out.reshape(N_CHIPS, OUT_SHARD, D_MODEL)
            mlp_grads, mlp_from_sib = _sibling_wait(grads, lands, send, recv, gw_out, "mlp_grads_to_sibling")
            self.grads = (gw_out4,) + tuple(mlp_grads)
            self.from_sib = tuple(_sibling_exchange((gw_out4,), True, "w_out_grad_to_sibling")) + tuple(mlp_from_sib)
            partials = [_chip_partial(g, r, c_arr, "chip_partial_" + nm)
                        for g, r, nm in zip(self.grads, self.from_sib, names[1:])]
            self.parts, self.part_lands, self.part_send, self.part_recv, token = _send_start(partials)
            return (token,)

    names = ("w_in", "w_out", "w_up", "w_down")
    comm = Comm()
    gx, big_g, small_g = _local_step(
        x.reshape(nseq * seq, d), loss_target.reshape(nseq * seq, d), w_in_full, comm,
        attn_norm_g, attn_sinks.reshape(N_Q_HEADS), w_pool[0], pool_scale, mlp_norm_g, final_norm_g.reshape(1, d), seq)
    big_g = tuple(big_g[:1]) + comm.grads

    from_sibling = list(_sibling_exchange(big_g[:1], True, "w_in_grad_to_sibling")) + list(comm.from_sib)
    received = list(_send_partials([_chip_partial(big_g[0], from_sibling[0], c_arr, "chip_partial_w_in")]))
    received += list(_send_wait(comm.parts, comm.part_lands, comm.part_send, comm.part_recv, received[0]))
    mine = [_final_half(g, r, p, jc_arr, "final_half_" + nm) for g, r, p, nm in zip(big_g, from_sibling, received, names)]
    other = _sibling_exchange(mine, False, "halves_to_sibling")
    big_out = [_adamw_shard(a, b, w, m, v, c_arr, "adamw_" + nm)
               for a, b, w, m, v, nm in zip(mine, other, big_w, big_m, big_v, names)]
    big_out[0] = [t.T for t in big_out[0]]

    wp_flat = lambda a: a.reshape(N_POOL * POOL_G, POOL_G)
    small_params = [
        (attn_norm_g, m_attn_norm_g, v_attn_norm_g),
        (attn_sinks, m_attn_sinks, v_attn_sinks),
        (wp_flat(w_pool), wp_flat(m_w_pool), wp_flat(v_w_pool)),
        (pool_scale, m_pool_scale, v_pool_scale),
        (mlp_norm_g, m_mlp_norm_g, v_mlp_norm_g),
        (final_norm_g.reshape(1, d), m_final_norm_g.reshape(1, d), v_final_norm_g.reshape(1, d)),
    ]
    loss, small_out = _small_adamw(_small_allreduce(small_g), small_params)

    def shaped(i, arr):
        return {1: w_in, 5: w_out, 7: w_up, 8: w_down, 0: attn_norm_g, 2: attn_sinks, 3: w_pool, 4: pool_scale,
                6: mlp_norm_g, 9: final_norm_g}[i].shape

    order = [small_out[0], big_out[0], small_out[1], small_out[2], small_out[3], big_out[1], small_out[4], big_out[2],
             big_out[3], small_out[5]]
    outs = [loss.reshape(()), gx.reshape(nseq, seq, d)]
    for kind in range(4):
        outs += [order[i][kind].reshape(shaped(i, None)) for i in range(10)]
    return tuple(outs)
```

```python
import jax
import jax.numpy as jnp
import numpy as np
from jax import lax
from jax.experimental import pallas as pl
from jax.experimental.pallas import tpu as pltpu

F32 = jnp.float32
BF16 = jnp.bfloat16
MXU_DTYPE = jnp.bfloat16

D_MODEL = 1024
HEAD_DIM = 64
N_Q_HEADS = 8
N_KV_HEADS = 2
Q_PER_KV = N_Q_HEADS // N_KV_HEADS
ATTN_W = N_Q_HEADS * HEAD_DIM
KV_W = N_KV_HEADS * HEAD_DIM
BLK = 128
POOL_WINDOWS = (2, 4, 8, 16)
N_POOL = len(POOL_WINDOWS)
POOL_W = D_MODEL - ATTN_W
POOL_G = POOL_W // N_POOL
IN_W = ATTN_W + 2 * KV_W + POOL_W
D_FF = 4 * D_MODEL
EPS = 1e-6
ROPE_THETA = 10000.0
N_CHIPS = 4
IN_SHARD = IN_W // N_CHIPS
OUT_SHARD = D_MODEL // N_CHIPS
FF_SHARD = D_FF // N_CHIPS
LANES = 128

ADAM_LR = 0.001
ADAM_B1 = 0.9
ADAM_B2 = 0.999
ADAM_EPS = 1e-08
ADAM_WD = 0.01
ADAM_STEP = 10

VMEM_LIMIT = 56 * 1024 * 1024
MESH = pl.DeviceIdType.MESH


def _cp(**kw):
    return pltpu.CompilerParams(vmem_limit_bytes=VMEM_LIMIT, **kw)


def _mm(a, b):
    return jnp.dot(a.astype(MXU_DTYPE), b.astype(MXU_DTYPE), preferred_element_type=F32)


def _mm_nt(a, b):
    return lax.dot_general(a.astype(MXU_DTYPE), b.astype(MXU_DTYPE), (((1,), (1,)), ((), ())),
                           preferred_element_type=F32)


def _mm_tn(a, b):
    return lax.dot_general(a.astype(MXU_DTYPE), b.astype(MXU_DTYPE), (((0,), (0,)), ((), ())),
                           preferred_element_type=F32)


def _resident(shape):
    nd = len(shape)
    return pl.BlockSpec(shape, lambda *_: (0,) * nd, pipeline_mode=pl.Buffered(1))


def _const(shape):
    nd = len(shape)
    return pl.BlockSpec(shape, lambda *_: (0,) * nd)


def _rope_tables(seq):
    half = HEAD_DIM // 2
    inv_freq = (ROPE_THETA ** (-np.arange(half, dtype=np.float32) / half)).astype(np.float32)
    ang = np.arange(seq, dtype=np.float32)[:, None] * inv_freq[None, :]
    cos, sin = np.cos(ang).astype(np.float32), np.sin(ang).astype(np.float32)
    cos_t = np.concatenate([cos, cos, cos, cos], axis=1)
    sin_t = np.concatenate([-sin, sin, -sin, sin], axis=1)
    return jnp.asarray(cos_t), jnp.asarray(sin_t)


def _swap_halves(xc):
    lane = lax.broadcasted_iota(jnp.int32, xc.shape, 1)
    return jnp.where((lane & 63) < 32, pltpu.roll(xc, 96, 1), pltpu.roll(xc, 32, 1))


_HBM = pl.BlockSpec(memory_space=pltpu.HBM)
_SEM = pl.BlockSpec(memory_space=pltpu.SEMAPHORE)
_ANY = pl.BlockSpec(memory_space=pl.ANY)
_EFFECT = pltpu.SideEffectType.DATAFLOW_SIDE_EFFECTING
TOKEN = jax.ShapeDtypeStruct((8, LANES), F32)


def _other_chips(x, y):
    return [(1 - x, y), (x, 1 - y), (1 - x, 1 - y)]


def _gather_start(first, rest):
    shards = tuple(first) + tuple(rest)
    nw, nf = len(shards), len(first)

    def body(*refs):
        srcs, lands = refs[:nw], refs[nw:2 * nw]
        sems = refs[2 * nw:2 * nw + 4]
        stages = refs[2 * nw + 4:3 * nw + 4]
        lsem = refs[3 * nw + 4]
        x, y, c = lax.axis_index("x"), lax.axis_index("y"), lax.axis_index("c")
        j = 2 * x + y
        for k in range(nw):
            send, recv = sems[0:2] if k < nf else sems[2:4]
            slot = k if k < nf else k - nf
            stages[k][...] = srcs[k][...].astype(BF16)
            local = pltpu.make_async_copy(stages[k], lands[k].at[j], lsem.at[k])
            local.start()
            local.wait()
            rows = shards[k].shape[0] // 2
            mine = lands[k].at[j, pl.ds(c * rows, rows)]
            for n, chip in enumerate(_other_chips(x, y)):
                pltpu.make_async_remote_copy(
                    src_ref=mine, dst_ref=mine, send_sem=send.at[slot * 3 + n], recv_sem=recv.at[slot * 3 + n],
                    device_id=(chip[0], chip[1], c), device_id_type=MESH).start()

    vmem = pl.BlockSpec(memory_space=pltpu.VMEM)
    nr = nw - nf
    res = pl.pallas_call(
        body, name="gather_start",
        out_shape=tuple(pltpu.HBM((N_CHIPS,) + s.shape, BF16) for s in shards) + (
            pltpu.SemaphoreType.DMA((3 * nf,)), pltpu.SemaphoreType.DMA((3 * nf,)),
            pltpu.SemaphoreType.DMA((3 * nr,)), pltpu.SemaphoreType.DMA((3 * nr,))),
        in_specs=[vmem] * nw, out_specs=tuple([_HBM] * nw) + (_SEM, _SEM, _SEM, _SEM),
        scratch_shapes=[pltpu.VMEM(s.shape, BF16) for s in shards] + [pltpu.SemaphoreType.DMA((nw,))],
        compiler_params=_cp(has_side_effects=_EFFECT),
    )(*shards)
    return (res[:nf], res[nw], res[nw + 1]), (res[nf:nw], res[nw + 2], res[nw + 3])


def _gather_wait(lands, send, recv, name, after=()):
    nw = len(lands)
    tokens, token_specs = _after(after)

    def body(*refs):
        ins = refs[:nw]
        send_ref, recv_ref = refs[nw], refs[nw + 1]
        x, y, c = lax.axis_index("x"), lax.axis_index("y"), lax.axis_index("c")
        j = 2 * x + y
        for k in range(nw):
            rows = lands[k].shape[1] // 2
            mine = ins[k].at[j, pl.ds(c * rows, rows)]
            for n, chip in enumerate(_other_chips(x, y)):
                got = ins[k].at[2 * chip[0] + chip[1], pl.ds(c * rows, rows)]
                cp = pltpu.make_async_remote_copy(
                    src_ref=mine, dst_ref=got, send_sem=send_ref.at[k * 3 + n], recv_sem=recv_ref.at[k * 3 + n],
                    device_id=(chip[0], chip[1], c), device_id_type=MESH)
                cp.wait_send()
                cp.wait_recv()

    return pl.pallas_call(
        body, name=name, out_shape=tuple(pltpu.HBM(a.shape, a.dtype) for a in lands),
        in_specs=[_HBM] * nw + [_SEM, _SEM] + token_specs, out_specs=tuple([_HBM] * nw),
        input_output_aliases={k: k for k in range(nw)},
        compiler_params=_cp(has_side_effects=_EFFECT),
    )(*lands, send, recv, *tokens)


def _forward_now(lands, name):
    nw = len(lands)
    rows_of = [a.shape[1] // 2 for a in lands]

    def body(*refs):
        mine, theirs = _forward_copies(refs[nw:2 * nw], nw, rows_of, refs[2 * nw], refs[2 * nw + 1])
        for cp in mine:
            cp.start()
        for cp in theirs:
            cp.wait_recv()
        for cp in mine:
            cp.wait_send()

    return pl.pallas_call(
        body, name=name, out_shape=tuple(jax.ShapeDtypeStruct(a.shape, a.dtype) for a in lands),
        in_specs=[_HBM] * nw, out_specs=tuple([_HBM] * nw), input_output_aliases={k: k for k in range(nw)},
        scratch_shapes=[pltpu.SemaphoreType.DMA((3 * nw,)), pltpu.SemaphoreType.DMA((3 * nw,))],
        compiler_params=_cp(),
    )(*lands)


def _forward_copies(refs, nw, rows_of, send, recv):
    x, y, c = lax.axis_index("x"), lax.axis_index("y"), lax.axis_index("c")
    out = []
    for k in range(nw):
        rows = rows_of[k]
        for n, chip in enumerate(_other_chips(x, y)):
            got = refs[k].at[2 * chip[0] + chip[1], pl.ds(c * rows, rows)]
            theirs = refs[k].at[2 * chip[0] + chip[1], pl.ds((1 - c) * rows, rows)]
            out.append(pltpu.make_async_remote_copy(
                src_ref=got, dst_ref=got, send_sem=send.at[k * 3 + n], recv_sem=recv.at[k * 3 + n],
                device_id=(x, y, 1 - c), device_id_type=MESH))
            out.append(pltpu.make_async_remote_copy(
                src_ref=theirs, dst_ref=theirs, send_sem=send.at[k * 3 + n], recv_sem=recv.at[k * 3 + n],
                device_id=(x, y, 1 - c), device_id_type=MESH))
    return out[0::2], out[1::2]


def _forward_start(lands):
    nw = len(lands)
    rows_of = [a.shape[1] // 2 for a in lands]

    def body(*refs):
        send, recv, token = refs[2 * nw:2 * nw + 3]
        mine, _ = _forward_copies(refs[:nw], nw, rows_of, send, recv)
        for cp in mine:
            cp.start()
        token[...] = jnp.zeros_like(token)

    res = pl.pallas_call(
        body, name="forward_start",
        out_shape=tuple(pltpu.HBM(a.shape, a.dtype) for a in lands) + (
            pltpu.SemaphoreType.DMA((3 * nw,)), pltpu.SemaphoreType.DMA((3 * nw,)), TOKEN),
        in_specs=[_HBM] * nw, out_specs=tuple([_HBM] * nw) + (_SEM, _SEM, pl.BlockSpec(memory_space=pltpu.VMEM)),
        input_output_aliases={k: k for k in range(nw)},
        compiler_params=_cp(has_side_effects=_EFFECT),
    )(*lands)
    return res[:nw], res[nw], res[nw + 1], res[nw + 2]


def _forward_wait(lands, send, recv, after):
    nw = len(lands)
    rows_of = [a.shape[1] // 2 for a in lands]

    def body(*refs):
        mine, theirs = _forward_copies(refs[:nw], nw, rows_of, refs[nw], refs[nw + 1])
        for cp in mine:
            cp.wait_send()
        for cp in theirs:
            cp.wait_recv()

    return pl.pallas_call(
        body, name="forward_wait", out_shape=tuple(pltpu.HBM(a.shape, a.dtype) for a in lands),
        in_specs=[_HBM] * nw + [_SEM, _SEM, _ANY], out_specs=tuple([_HBM] * nw),
        input_output_aliases={k: k for k in range(nw)},
        compiler_params=_cp(has_side_effects=_EFFECT),
    )(*lands, send, recv, after)


def _after(tokens):
    tokens = [t for t in tokens if t is not None]
    return tokens, [_ANY] * len(tokens)


def _fwd_inproj(x, g1, w_in, cos_t, sin_t, tm, after=()):
    t_tok = x.shape[0]
    seq = cos_t.shape[0]
    per_seq = seq // tm
    tokens, token_specs = _after(after)

    def body(x_ref, g_ref, w_ref, cos_ref, sin_ref, *rest):
        q_ref, k_ref, v_ref, u_ref = rest[len(tokens):]
        xv = x_ref[...]
        r = lax.rsqrt(jnp.mean(xv * xv, axis=-1, keepdims=True) + EPS)
        h = (xv * r) * g_ref[...]
        proj = _mm_nt(h, w_ref[...])
        cos, sin = cos_ref[...], sin_ref[...]
        for cidx in range((ATTN_W + KV_W) // LANES):
            xc = proj[:, cidx * LANES:(cidx + 1) * LANES]
            rot = xc * cos + _swap_halves(xc) * sin
            if cidx < ATTN_W // LANES:
                q_ref[:, cidx * LANES:(cidx + 1) * LANES] = (rot * (HEAD_DIM ** -0.5)).astype(q_ref.dtype)
            else:
                k_ref[...] = rot.astype(k_ref.dtype)
        v_ref[...] = proj[:, ATTN_W + KV_W:ATTN_W + 2 * KV_W].astype(v_ref.dtype)
        u_ref[...] = proj[:, ATTN_W + 2 * KV_W:]

    row = lambda w: pl.BlockSpec((tm, w), lambda i: (i, 0))
    kv = row(KV_W)
    tab = pl.BlockSpec((tm, LANES), lambda i: (i % per_seq, 0))
    kv_shape = jax.ShapeDtypeStruct((t_tok, KV_W), MXU_DTYPE)
    return pl.pallas_call(
        body, name="fwd_inproj", grid=(t_tok // tm,),
        in_specs=[row(D_MODEL), _const((1, D_MODEL)), _resident((IN_W, D_MODEL)), tab, tab] + token_specs,
        out_specs=(row(ATTN_W), kv, kv, row(POOL_W)),
        out_shape=(jax.ShapeDtypeStruct((t_tok, ATTN_W), MXU_DTYPE), kv_shape, kv_shape,
                   jax.ShapeDtypeStruct((t_tok, POOL_W), F32)),
        compiler_params=_cp(dimension_semantics=("parallel",)),
    )(x, g1, w_in, cos_t, sin_t, *tokens)


MASKED = -1e30


def _attn_bias():
    b = lax.broadcasted_iota(jnp.int32, (2 * BLK, BLK), 0)
    a = lax.broadcasted_iota(jnp.int32, (2 * BLK, BLK), 1)
    own = (b >= BLK) & (b - BLK <= a)
    prev = (b < BLK) & (b > a)
    return jnp.stack([jnp.where(own, 0.0, MASKED), jnp.where(own | prev, 0.0, MASKED)]).astype(F32)


def _sink_rows(sinks):
    return jnp.repeat(sinks.reshape(N_KV_HEADS, Q_PER_KV), BLK, axis=1).reshape(N_KV_HEADS, 1, Q_PER_KV * BLK)


def _stack_heads(ref, r0, g):
    return jnp.concatenate(
        [ref[pl.ds(r0, BLK), (Q_PER_KV * g + h) * HEAD_DIM:(Q_PER_KV * g + h + 1) * HEAD_DIM] for h in range(Q_PER_KV)],
        axis=0)


def _kv_window(ref, p0, r0, g):
    sl = slice(g * HEAD_DIM, (g + 1) * HEAD_DIM)
    return jnp.concatenate([ref[pl.ds(p0, BLK), sl], ref[pl.ds(r0, BLK), sl]], axis=0)


def _scores_t(k_ref, q_ref, bias, p0, r0, g):
    kk = _kv_window(k_ref, p0, r0, g)
    qs = _stack_heads(q_ref, r0, g)
    st = _mm_nt(kk, qs) + jnp.concatenate([bias] * Q_PER_KV, axis=1)
    return st, kk, qs


def _head_rows(ref, r0, g):
    return jnp.concatenate([ref[pl.ds(Q_PER_KV * g + h, 1), pl.ds(r0, BLK)] for h in range(Q_PER_KV)], axis=1)


def _pairs_to_rows(xt):
    out = []
    for t in range(Q_PER_KV // 2):
        pair = jnp.concatenate([xt[:, (2 * t) * BLK:(2 * t + 1) * BLK], xt[:, (2 * t + 1) * BLK:(2 * t + 2) * BLK]], axis=0)
        out.append(pair.T)
    return out


def _shift_rows(x, k, seq):
    row = lax.broadcasted_iota(jnp.int32, x.shape, 0)
    if k > 0:
        return jnp.where(row >= k, pltpu.roll(x, k, 0), 0.0)
    return jnp.where(row < seq + k, pltpu.roll(x, seq + k, 0), 0.0)


def _window_sum(x, w, seq, forward):
    s, k = x, 1
    while k < w:
        s = s + _shift_rows(s, -k if forward else k, seq)
        k *= 2
    return s


def _inv_count(seq, w):
    pos = lax.broadcasted_iota(jnp.int32, (seq, 1), 0)
    return 1.0 / jnp.minimum(pos + 1, w).astype(F32)


def _fwd_attn(q, k, v, sink_rows, bias, seq):
    t_tok = q.shape[0]
    nblk = seq // BLK

    def body(q_ref, k_ref, v_ref, sink_ref, bias_ref, o_ref, lse_ref):
        def blk(i, carry):
            r0 = pl.multiple_of(i * BLK, BLK)
            p0 = pl.multiple_of(jnp.maximum(i - 1, 0) * BLK, BLK)
            bias = bias_ref[jnp.minimum(i, 1)]
            for g in range(N_KV_HEADS):
                st, _, _ = _scores_t(k_ref, q_ref, bias, p0, r0, g)
                sink = sink_ref[g]
                m = jnp.maximum(jnp.max(st, axis=0, keepdims=True), sink)
                p = jnp.exp(st - m).astype(MXU_DTYPE)
                vv = _kv_window(v_ref, p0, r0, g)
                ot = _mm_tn(jnp.concatenate([vv, jnp.ones_like(vv)], axis=1), p)
                den = ot[HEAD_DIM:HEAD_DIM + 1] + jnp.exp(sink - m)
                lse = m + jnp.log(den)
                for h in range(Q_PER_KV):
                    lse_ref[pl.ds(Q_PER_KV * g + h, 1), pl.ds(r0, BLK)] = lse[:, h * BLK:(h + 1) * BLK]
                ot = ot[:HEAD_DIM] * (1.0 / den)
                for t, tile in enumerate(_pairs_to_rows(ot)):
                    c0 = (Q_PER_KV * g + 2 * t) * HEAD_DIM
                    o_ref[pl.ds(r0, BLK), c0:c0 + LANES] = tile.astype(o_ref.dtype)
            return carry

        lax.fori_loop(0, nblk, blk, 0, unroll=4)

    row = lambda w: pl.BlockSpec((seq, w), lambda i: (i, 0))
    kv = row(KV_W)
    return pl.pallas_call(
        body, name="fwd_attn", grid=(t_tok // seq,),
        in_specs=[row(ATTN_W), kv, kv, _const((N_KV_HEADS, 1, Q_PER_KV * BLK)), _const((2, 2 * BLK, BLK))],
        out_specs=(row(ATTN_W), pl.BlockSpec((N_Q_HEADS, seq), lambda i: (0, i))),
        out_shape=(jax.ShapeDtypeStruct((t_tok, ATTN_W), MXU_DTYPE), jax.ShapeDtypeStruct((N_Q_HEADS, t_tok), F32)),
        compiler_params=_cp(dimension_semantics=("parallel",)),
    )(q, k, v, sink_rows, bias)


def _fwd_pool(u, w_pool, pool_scale, seq, after=()):
    t_tok = u.shape[0]
    tokens, token_specs = _after(after)

    def body(u_ref, wp_ref, sc_ref, *rest):
        o_ref = rest[-1]
        for gi, w in enumerate(POOL_WINDOWS):
            sl = slice(gi * POOL_G, (gi + 1) * POOL_G)
            ug = u_ref[:, sl]
            d = _window_sum(ug, w, seq, False) * _inv_count(seq, w) - ug
            o_ref[:, sl] = (_mm(d, wp_ref[gi]) * sc_ref[:, sl]).astype(o_ref.dtype)

    row = pl.BlockSpec((seq, POOL_W), lambda i: (i, 0))
    return pl.pallas_call(
        body, name="fwd_pool", grid=(t_tok // seq,),
        in_specs=[row, _const((N_POOL, POOL_G, POOL_G)), _const((1, POOL_W))] + token_specs, out_specs=row,
        out_shape=jax.ShapeDtypeStruct((t_tok, POOL_W), MXU_DTYPE),
        compiler_params=_cp(dimension_semantics=("parallel",)),
    )(u, w_pool, pool_scale, *tokens)


def _rms_bwd(dy_g, xn, r):
    return r * (dy_g - xn * jnp.mean(dy_g * xn, axis=-1, keepdims=True))


def _mlp_fwd_bwd(x, attn, pool, target, w_out, w_up4, w_down4, g2, gf, tm):
    t_tok = x.shape[0]

    def body(x_ref, attn_ref, pool_ref, tgt_ref, wo_ref, wu_ref, wd_ref, g2_ref, gf_ref,
             h2_ref, a_ref, da_ref, dx2b_ref, dx1_ref, dx1b_ref, loss_ref, dgf_ref, dg2_ref):
        @pl.when(pl.program_id(0) == 0)
        def _():
            loss_ref[...] = jnp.zeros_like(loss_ref)
            dgf_ref[...] = jnp.zeros_like(dgf_ref)
            dg2_ref[...] = jnp.zeros_like(dg2_ref)

        x1 = x_ref[...] + (_mm(attn_ref[...], wo_ref[:ATTN_W]) + _mm(pool_ref[...], wo_ref[ATTN_W:]))
        r2 = lax.rsqrt(jnp.mean(x1 * x1, axis=-1, keepdims=True) + EPS)
        xn1 = x1 * r2
        g2 = g2_ref[...]
        h2 = (xn1 * g2).astype(MXU_DTYPE)
        h2_ref[...] = h2
        acc = jnp.zeros((tm, D_MODEL), F32)
        for j in range(N_CHIPS):
            a = _mm(h2, wu_ref[j])
            a_ref[:, j * FF_SHARD:(j + 1) * FF_SHARD] = a.astype(a_ref.dtype)
            acc = acc + _mm(jnp.square(jnp.maximum(a, 0.0)), wd_ref[j])
        x2 = x1 + acc
        r3 = lax.rsqrt(jnp.mean(x2 * x2, axis=-1, keepdims=True) + EPS)
        xn2 = x2 * r3
        gf_v = gf_ref[...]
        err = xn2 * gf_v - tgt_ref[...]
        part = jnp.sum(err * err) * (0.5 / D_MODEL)
        first = (lax.broadcasted_iota(jnp.int32, loss_ref.shape, 0) == 0) & (lax.broadcasted_iota(jnp.int32, loss_ref.shape, 1) == 0)
        loss_ref[...] += jnp.where(first, part, 0.0)
        dy = err * (1.0 / D_MODEL)
        dgf_ref[...] += jnp.sum(dy * xn2, axis=0, keepdims=True)
        dx2 = _rms_bwd(dy * gf_v, xn2, r3)
        dx2b = dx2.astype(MXU_DTYPE)
        dx2b_ref[...] = dx2b
        dh2 = jnp.zeros((tm, D_MODEL), F32)
        for j in range(N_CHIPS):
            sl = slice(j * FF_SHARD, (j + 1) * FF_SHARD)
            dhid = _mm_nt(dx2b, wd_ref[j])
            da = (dhid * (2.0 * jnp.maximum(a_ref[:, sl].astype(F32), 0.0))).astype(MXU_DTYPE)
            da_ref[:, sl] = da
            dh2 = dh2 + _mm_nt(da, wu_ref[j])
        dg2_ref[...] += jnp.sum(dh2 * xn1, axis=0, keepdims=True)
        dx1 = dx2 + _rms_bwd(dh2 * g2, xn1, r2)
        dx1_ref[...] = dx1
        dx1b_ref[...] = dx1.astype(dx1b_ref.dtype)

    row = lambda w: pl.BlockSpec((tm, w), lambda i: (i, 0))
    vec = jax.ShapeDtypeStruct((1, D_MODEL), F32)
    return pl.pallas_call(
        body, name="mlp_fwd_bwd", grid=(t_tok // tm,),
        in_specs=[row(D_MODEL), row(ATTN_W), row(POOL_W), row(D_MODEL), _resident((D_MODEL, D_MODEL)),
                  _resident((N_CHIPS, D_MODEL, FF_SHARD)), _resident((N_CHIPS, FF_SHARD, D_MODEL)),
                  _const((1, D_MODEL)), _const((1, D_MODEL))],
        out_specs=(row(D_MODEL), row(D_FF), row(D_FF), row(D_MODEL), row(D_MODEL), row(D_MODEL),
                   _const((8, LANES)), _const((1, D_MODEL)), _const((1, D_MODEL))),
        out_shape=(jax.ShapeDtypeStruct((t_tok, D_MODEL), MXU_DTYPE), jax.ShapeDtypeStruct((t_tok, D_FF), MXU_DTYPE),
                   jax.ShapeDtypeStruct((t_tok, D_FF), MXU_DTYPE), jax.ShapeDtypeStruct((t_tok, D_MODEL), MXU_DTYPE),
                   jax.ShapeDtypeStruct((t_tok, D_MODEL), F32), jax.ShapeDtypeStruct((t_tok, D_MODEL), MXU_DTYPE),
                   jax.ShapeDtypeStruct((8, LANES), F32), vec, vec),
        compiler_params=_cp(dimension_semantics=("arbitrary",)),
    )(x, attn, pool, target, w_out, w_up4, w_down4, g2, gf)


def _bwd_mlp_wgrads(h2, da, a, dx2b, tk):
    t_tok = h2.shape[0]

    def body(h2_ref, da_ref, a_ref, dx2b_ref, gup_ref, gdn_ref):
        @pl.when(pl.program_id(1) == 0)
        def _():
            gup_ref[...] = jnp.zeros_like(gup_ref)
            gdn_ref[...] = jnp.zeros_like(gdn_ref)

        gup_ref[0] += _mm_tn(h2_ref[...], da_ref[...])
        hid = jnp.square(jnp.maximum(a_ref[...].astype(F32), 0.0))
        gdn_ref[0] += _mm_tn(hid, dx2b_ref[...])

    tok = pl.BlockSpec((tk, D_MODEL), lambda j, t: (t, 0))
    ffb = pl.BlockSpec((tk, FF_SHARD), lambda j, t: (t, j))
    wblk = pl.BlockSpec((1, D_MODEL, D_MODEL), lambda j, t: (j, 0, 0))
    return pl.pallas_call(
        body, name="bwd_mlp_wgrads", grid=(N_CHIPS, t_tok // tk),
        in_specs=[tok, ffb, ffb, tok], out_specs=(wblk, wblk),
        out_shape=(jax.ShapeDtypeStruct((N_CHIPS, D_MODEL, FF_SHARD), F32), jax.ShapeDtypeStruct((N_CHIPS, FF_SHARD, D_MODEL), F32)),
        compiler_params=_cp(dimension_semantics=("parallel", "arbitrary")),
    )(h2, da, a, dx2b)


def _head_selector():
    ch = lax.broadcasted_iota(jnp.int32, (ATTN_W, LANES), 0)
    col = lax.broadcasted_iota(jnp.int32, (ATTN_W, LANES), 1)
    return (ch // HEAD_DIM == col).astype(MXU_DTYPE)


def _bwd_outproj(dx1b, attn, pool, w_out, head_sel, tm, after=()):
    t_tok = dx1b.shape[0]
    tokens, token_specs = _after(after)

    def body(dx_ref, attn_ref, pool_ref, wo_ref, sel_ref, *rest):
        dattn_ref, dpool_ref, delta_ref, gwo_ref = rest[len(tokens):]

        @pl.when(pl.program_id(0) == 0)
        def _():
            gwo_ref[...] = jnp.zeros_like(gwo_ref)

        dx = dx_ref[...]
        attn = attn_ref[...]
        dattn = _mm_nt(dx, wo_ref[:ATTN_W])
        dattn_ref[...] = dattn.astype(dattn_ref.dtype)
        dpool_ref[...] = _mm_nt(dx, wo_ref[ATTN_W:])
        prod = dattn * attn.astype(F32)
        hi = prod.astype(MXU_DTYPE)
        lo = prod - hi.astype(F32)
        delta = _mm(hi, sel_ref[...]) + _mm(lo, sel_ref[...])
        delta_ref[...] = delta.T[:N_Q_HEADS]
        gwo_ref[:ATTN_W] += _mm_tn(attn, dx)
        gwo_ref[ATTN_W:] += _mm_tn(pool_ref[...], dx)

    row = lambda w: pl.BlockSpec((tm, w), lambda i: (i, 0))
    return pl.pallas_call(
        body, name="bwd_outproj", grid=(t_tok // tm,),
        in_specs=[row(D_MODEL), row(ATTN_W), row(POOL_W), _resident((D_MODEL, D_MODEL)), _const((ATTN_W, LANES))] + token_specs,
        out_specs=(row(ATTN_W), row(POOL_W), pl.BlockSpec((N_Q_HEADS, tm), lambda i: (0, i)), _const((D_MODEL, D_MODEL))),
        out_shape=(jax.ShapeDtypeStruct((t_tok, ATTN_W), MXU_DTYPE), jax.ShapeDtypeStruct((t_tok, POOL_W), F32),
                   jax.ShapeDtypeStruct((N_Q_HEADS, t_tok), F32), jax.ShapeDtypeStruct((D_MODEL, D_MODEL), F32)),
        compiler_params=_cp(dimension_semantics=("arbitrary",)),
    )(dx1b, attn, pool, w_out, head_sel, *tokens)


def _bwd_attn(q, k, v, dattn, lse, delta, sink_rows, bias, cos_t, sin_t, seq, after=()):
    t_tok = q.shape[0]
    nblk = seq // BLK
    qkv_w = ATTN_W + 2 * KV_W
    tokens, token_specs = _after(after)

    def unrope(d, cos, sin):
        return d * cos - _swap_halves(d) * sin

    def body(q_ref, k_ref, v_ref, do_ref, lse_ref, delta_ref, sink_ref, bias_ref, cos_ref, sin_ref, *rest):
        dqkv_ref, dsink_ref, dk_acc, dv_acc = rest[len(tokens):]

        @pl.when(pl.program_id(0) == 0)
        def _():
            dsink_ref[...] = jnp.zeros_like(dsink_ref)

        dk_acc[...] = jnp.zeros_like(dk_acc)
        dv_acc[...] = jnp.zeros_like(dv_acc)

        def blk(i, dsink):
            r0 = pl.multiple_of(i * BLK, BLK)
            p0 = pl.multiple_of(jnp.maximum(i - 1, 0) * BLK, BLK)
            bias = bias_ref[jnp.minimum(i, 1)]
            cos, sin = cos_ref[pl.ds(r0, BLK), :], sin_ref[pl.ds(r0, BLK), :]
            new = []
            for g in range(N_KV_HEADS):
                st, kk, qs = _scores_t(k_ref, q_ref, bias, p0, r0, g)
                lse_g, delta_g = _head_rows(lse_ref, r0, g), _head_rows(delta_ref, r0, g)
                pn = jnp.exp(st - lse_g)
                dos = _stack_heads(do_ref, r0, g)
                dst = pn * (_mm_nt(_kv_window(v_ref, p0, r0, g), dos) - delta_g)
                new.append(dsink[g] - jnp.exp(sink_ref[g] - lse_g) * delta_g)
                dqt = _mm_tn(kk, dst) * (HEAD_DIM ** -0.5)
                for t, tile in enumerate(_pairs_to_rows(dqt)):
                    c0 = (Q_PER_KV * g + 2 * t) * HEAD_DIM
                    dqkv_ref[pl.ds(r0, BLK), c0:c0 + LANES] = unrope(tile, cos, sin).astype(dqkv_ref.dtype)
                dkk = _mm(dst, qs)
                dvv = _mm(pn, dos)
                dk_acc[g, pl.ds(p0, BLK), :] += dkk[:BLK]
                dk_acc[g, pl.ds(r0, BLK), :] += dkk[BLK:]
                dv_acc[g, pl.ds(p0, BLK), :] += dvv[:BLK]
                dv_acc[g, pl.ds(r0, BLK), :] += dvv[BLK:]
            return tuple(new)

        zero = jnp.zeros((1, Q_PER_KV * BLK), F32)
        dsink = lax.fori_loop(0, nblk // 2, lambda i2, acc: blk(2 * i2 + 1, blk(2 * i2, acc)), (zero,) * N_KV_HEADS)
        lane = lax.broadcasted_iota(jnp.int32, dsink_ref.shape, 1)
        row = lax.broadcasted_iota(jnp.int32, dsink_ref.shape, 0)
        tile = jnp.zeros(dsink_ref.shape, F32)
        for g in range(N_KV_HEADS):
            for h in range(Q_PER_KV):
                tot = jnp.sum(dsink[g][:, h * BLK:(h + 1) * BLK])
                tile = tile + jnp.where((row == 0) & (lane == Q_PER_KV * g + h), tot, 0.0)
        dsink_ref[...] += tile
        dk = jnp.concatenate([dk_acc[g] for g in range(N_KV_HEADS)], axis=1)
        dqkv_ref[:, ATTN_W:ATTN_W + KV_W] = unrope(dk, cos_ref[...], sin_ref[...]).astype(dqkv_ref.dtype)
        dqkv_ref[:, ATTN_W + KV_W:] = jnp.concatenate([dv_acc[g] for g in range(N_KV_HEADS)], axis=1).astype(dqkv_ref.dtype)

    row = lambda w: pl.BlockSpec((seq, w), lambda i: (i, 0))
    kv = row(KV_W)
    per_head = pl.BlockSpec((N_Q_HEADS, seq), lambda i: (0, i))
    return pl.pallas_call(
        body, name="bwd_attn", grid=(t_tok // seq,),
        in_specs=[row(ATTN_W), kv, kv, row(ATTN_W), per_head, per_head, _const((N_KV_HEADS, 1, Q_PER_KV * BLK)),
                  _const((2, 2 * BLK, BLK)), _resident((seq, LANES)), _resident((seq, LANES))] + token_specs,
        out_specs=(row(qkv_w), _const((8, LANES))),
        out_shape=(jax.ShapeDtypeStruct((t_tok, qkv_w), MXU_DTYPE), jax.ShapeDtypeStruct((8, LANES), F32)),
        scratch_shapes=[pltpu.VMEM((N_KV_HEADS, seq, HEAD_DIM), F32), pltpu.VMEM((N_KV_HEADS, seq, HEAD_DIM), F32)],
        compiler_params=_cp(dimension_semantics=("arbitrary",)),
    )(q, k, v, dattn, lse, delta, sink_rows, bias, cos_t, sin_t, *tokens)


def _bwd_pool(u, dpool, w_pool, pool_scale, seq):
    t_tok = u.shape[0]

    def body(u_ref, dp_ref, wp_ref, sc_ref, du_ref, dwp_ref, dsc_ref):
        @pl.when(pl.program_id(0) == 0)
        def _():
            dwp_ref[...] = jnp.zeros_like(dwp_ref)
            dsc_ref[...] = jnp.zeros_like(dsc_ref)

        for gi, w in enumerate(POOL_WINDOWS):
            sl = slice(gi * POOL_G, (gi + 1) * POOL_G)
            ug = u_ref[:, sl]
            inv = _inv_count(seq, w)
            d = (_window_sum(ug, w, seq, False) * inv - ug).astype(MXU_DTYPE)
            y = _mm(d, wp_ref[gi])
            dpool = dp_ref[:, sl]
            dsc_ref[:, sl] += jnp.sum(y * dpool, axis=0, keepdims=True)
            dy = (dpool * sc_ref[:, sl]).astype(MXU_DTYPE)
            dwp_ref[gi] += _mm_tn(d, dy)
            dd = _mm_nt(dy, wp_ref[gi])
            du_ref[:, sl] = (_window_sum(dd * inv, w, seq, True) - dd).astype(du_ref.dtype)

    row = pl.BlockSpec((seq, POOL_W), lambda i: (i, 0))
    return pl.pallas_call(
        body, name="bwd_pool", grid=(t_tok // seq,),
        in_specs=[row, row, _const((N_POOL, POOL_G, POOL_G)), _const((1, POOL_W))],
        out_specs=(row, _const((N_POOL, POOL_G, POOL_G)), _const((1, POOL_W))),
        out_shape=(jax.ShapeDtypeStruct((t_tok, POOL_W), MXU_DTYPE), jax.ShapeDtypeStruct((N_POOL, POOL_G, POOL_G), F32),
                   jax.ShapeDtypeStruct((1, POOL_W), F32)),
        compiler_params=_cp(dimension_semantics=("arbitrary",)),
    )(u, dpool, w_pool, pool_scale)


def _bwd_inproj(dqkv, du, x, dx1, w_in, g1, tm):
    t_tok = x.shape[0]
    nsteps = t_tok // tm

    qkv_w = ATTN_W + 2 * KV_W

    def body(dqkv_ref, du_ref, x_ref, dx1_ref, w_ref, g_ref, gx_ref, gw_ref, dg_ref):
        @pl.when(pl.program_id(0) == 0)
        def _():
            gw_ref[...] = jnp.zeros_like(gw_ref)
            dg_ref[...] = jnp.zeros_like(dg_ref)

        dqkv, du = dqkv_ref[...], du_ref[...]
        xv = x_ref[...]
        r = lax.rsqrt(jnp.mean(xv * xv, axis=-1, keepdims=True) + EPS)
        xn = xv * r
        g = g_ref[...]
        dh = _mm(dqkv, w_ref[:qkv_w]) + _mm(du, w_ref[qkv_w:])
        dg_ref[...] += jnp.sum(dh * xn, axis=0, keepdims=True)
        gx_ref[...] = dx1_ref[...] + _rms_bwd(dh * g, xn, r)
        h = (xn * g).astype(MXU_DTYPE)
        gw_ref[:qkv_w] += _mm_tn(dqkv, h)
        gw_ref[qkv_w:] += _mm_tn(du, h)

    row = lambda w: pl.BlockSpec((tm, w), lambda i: (i, 0))
    return pl.pallas_call(
        body, name="bwd_inproj", grid=(nsteps,),
        in_specs=[row(qkv_w), row(POOL_W), row(D_MODEL), row(D_MODEL), _resident((IN_W, D_MODEL)), _const((1, D_MODEL))],
        out_specs=(row(D_MODEL), _const((IN_W, D_MODEL)), _const((1, D_MODEL))),
        out_shape=(jax.ShapeDtypeStruct((t_tok, D_MODEL), F32), jax.ShapeDtypeStruct((IN_W, D_MODEL), F32),
                   jax.ShapeDtypeStruct((1, D_MODEL), F32)),
        compiler_params=_cp(dimension_semantics=("arbitrary",)),
    )(dqkv, du, x, dx1, w_in, g1)


class _NoComm:
    def start(self):
        return ()

    def attn_done(self, attn):
        return ()

    def rest_of_weights(self, pool):
        raise NotImplementedError

    def mlp_grads_ready(self, gw_up4, gw_down4):
        return ()

    def outproj_done(self, gw_out):
        return ()


def _local_step(x, target, w_in, comm, g1, sinks, w_pool, pool_scale, g2, gf, seq):
    tm = min(512, seq)
    tm_big = min(1024, seq)
    tm_mlp = min(256, seq)
    cos_t, sin_t = _rope_tables(seq)
    sink_rows, bias = _sink_rows(sinks), _attn_bias()
    q, k, v, u = _fwd_inproj(x, g1, w_in, cos_t, sin_t, tm_big, comm.start())
    attn, lse = _fwd_attn(q, k, v, sink_rows, bias, seq)
    pool = _fwd_pool(u, w_pool, pool_scale, seq, comm.attn_done(attn))
    w_out, w_up4, w_down4 = comm.rest_of_weights(pool)
    h2, a, da, dx2b, dx1, dx1b, loss, dgf, dg2 = _mlp_fwd_bwd(x, attn, pool, target, w_out, w_up4, w_down4, g2, gf, tm_mlp)
    gw_up4, gw_down4 = _bwd_mlp_wgrads(h2, da, a, dx2b, min(1024, x.shape[0]))
    dattn, dpool, delta, gw_out = _bwd_outproj(dx1b, attn, pool, w_out, _head_selector(), tm_big,
                                               comm.mlp_grads_ready(gw_up4, gw_down4))
    dqkv, dsinks = _bwd_attn(q, k, v, dattn, lse, delta, sink_rows, bias, cos_t, sin_t, seq, comm.outproj_done(gw_out))
    du, dwp, dsc = _bwd_pool(u, dpool, w_pool, pool_scale, seq)
    gx, gw_in_t, dg1 = _bwd_inproj(dqkv, du, x, dx1, w_in, g1, tm)
    big = (gw_in_t.reshape(N_CHIPS, IN_SHARD, D_MODEL), gw_out.reshape(N_CHIPS, OUT_SHARD, D_MODEL), gw_up4, gw_down4)
    small = (dg1, dsinks, dwp.reshape(N_POOL * POOL_G, POOL_G), dsc, dg2, dgf, loss)
    return gx, big, small


def _sibling_exchange(arrs, pick_half, name, after=()):
    n = len(arrs)
    tokens, token_specs = _after(after)

    def out_shape(a):
        if pick_half:
            return (a.shape[0], a.shape[1] // 2) + a.shape[2:]
        return a.shape

    def body(*refs):
        srcs, dsts = refs[:n], refs[n + len(tokens):2 * n + len(tokens)]
        send, recv = refs[2 * n + len(tokens):]
        x, y, c = lax.axis_index("x"), lax.axis_index("y"), lax.axis_index("c")
        cps = []
        for i in range(n):
            src = srcs[i]
            if pick_half:
                h = src.shape[1] // 2
                src = src.at[:, pl.ds((1 - c) * h, h)]
            cp = pltpu.make_async_remote_copy(src_ref=src, dst_ref=dsts[i], send_sem=send.at[i], recv_sem=recv.at[i],
                                              device_id=(x, y, 1 - c), device_id_type=MESH)
            cp.start()
            cps.append(cp)
        for cp in cps:
            cp.wait()

    hbm = pl.BlockSpec(memory_space=pltpu.HBM)
    return pl.pallas_call(
        body, name=name, out_shape=tuple(jax.ShapeDtypeStruct(out_shape(a), a.dtype) for a in arrs),
        in_specs=[hbm] * n + token_specs, out_specs=tuple([hbm] * n),
        scratch_shapes=[pltpu.SemaphoreType.DMA((n,)), pltpu.SemaphoreType.DMA((n,))],
        compiler_params=_cp(),
    )(*arrs, *tokens)


def _sibling_copies(srcs, lands, send, recv):
    x, y, c = lax.axis_index("x"), lax.axis_index("y"), lax.axis_index("c")
    cps = []
    for i in range(len(srcs)):
        h = srcs[i].shape[1] // 2
        cps.append(pltpu.make_async_remote_copy(
            src_ref=srcs[i].at[:, pl.ds((1 - c) * h, h)], dst_ref=lands[i], send_sem=send.at[i], recv_sem=recv.at[i],
            device_id=(x, y, 1 - c), device_id_type=MESH))
    return cps


def _sibling_start(arrs, name):
    n = len(arrs)

    def body(*refs):
        send, recv, token = refs[3 * n:3 * n + 3]
        for cp in _sibling_copies(refs[:n], refs[2 * n:3 * n], send, recv):
            cp.start()
        token[...] = jnp.zeros_like(token)

    half = lambda a: (a.shape[0], a.shape[1] // 2) + a.shape[2:]
    res = pl.pallas_call(
        body, name=name + "_start",
        out_shape=tuple(pltpu.HBM(a.shape, a.dtype) for a in arrs) + tuple(pltpu.HBM(half(a), a.dtype) for a in arrs) + (
            pltpu.SemaphoreType.DMA((n,)), pltpu.SemaphoreType.DMA((n,)), TOKEN),
        in_specs=[_HBM] * n, out_specs=tuple([_HBM] * (2 * n)) + (_SEM, _SEM, pl.BlockSpec(memory_space=pltpu.VMEM)),
        input_output_aliases={i: i for i in range(n)},
        compiler_params=_cp(has_side_effects=_EFFECT),
    )(*[pltpu.with_memory_space_constraint(a, pltpu.HBM) for a in arrs])
    return res[:n], res[n:2 * n], res[2 * n], res[2 * n + 1], res[2 * n + 2]


def _sibling_wait(arrs, lands, send, recv, after, name):
    n = len(arrs)

    def body(*refs):
        for cp in _sibling_copies(refs[:n], refs[n:2 * n], refs[2 * n], refs[2 * n + 1]):
            cp.wait_send()
            cp.wait_recv()

    res = pl.pallas_call(
        body, name=name + "_wait", out_shape=tuple(pltpu.HBM(a.shape, a.dtype) for a in list(arrs) + list(lands)),
        in_specs=[_HBM] * (2 * n) + [_SEM, _SEM, _ANY], out_specs=tuple([_HBM] * (2 * n)),
        input_output_aliases={i: i for i in range(2 * n)},
        compiler_params=_cp(has_side_effects=_EFFECT),
    )(*arrs, *lands, send, recv, after)
    return res[:n], res[n:]


def _row_block(rows):
    if rows <= 256:
        return rows
    for cand in (256, 128, 64, 32, 16, 8):
        if rows % cand == 0:
            return cand
    raise ValueError(rows)


def _chip_partial(g4, r4, c_arr, name):
    _, rows, cols = r4.shape
    rb = _row_block(rows)
    nb = rows // rb

    def body(c_ref, g_ref, r_ref, o_ref):
        o_ref[...] = (g_ref[...] + r_ref[...]).astype(o_ref.dtype)

    return pl.pallas_call(
        body, name=name,
        grid_spec=pltpu.PrefetchScalarGridSpec(
            num_scalar_prefetch=1, grid=(N_CHIPS, nb),
            in_specs=[pl.BlockSpec((1, rb, cols), lambda s, i, c: (s, c[0] * nb + i, 0)),
                      pl.BlockSpec((1, rb, cols), lambda s, i, c: (s, i, 0))],
            out_specs=pl.BlockSpec((1, rb, cols), lambda s, i, c: (s, i, 0))),
        out_shape=jax.ShapeDtypeStruct(r4.shape, BF16),
        compiler_params=_cp(dimension_semantics=("parallel", "parallel")),
    )(c_arr, g4, r4)


def _send_start(parts, name):
    n = len(parts)

    def body(*refs):
        srcs, lands = refs[:n], refs[2 * n:3 * n]
        send, recv, token = refs[3 * n:3 * n + 3]
        x, y, c = lax.axis_index("x"), lax.axis_index("y"), lax.axis_index("c")
        for i in range(n):
            for m, chip in enumerate(_other_chips(x, y)):
                pltpu.make_async_remote_copy(
                    src_ref=srcs[i].at[2 * chip[0] + chip[1]], dst_ref=lands[i].at[m],
                    send_sem=send.at[3 * i + m], recv_sem=recv.at[3 * i + m],
                    device_id=(chip[0], chip[1], c), device_id_type=MESH).start()
        token[...] = jnp.zeros_like(token)

    res = pl.pallas_call(
        body, name=name,
        out_shape=tuple(pltpu.HBM(p.shape, p.dtype) for p in parts) + tuple(pltpu.HBM((3,) + p.shape[1:], p.dtype) for p in parts) + (
            pltpu.SemaphoreType.DMA((3 * n,)), pltpu.SemaphoreType.DMA((3 * n,)), TOKEN),
        in_specs=[_HBM] * n, out_specs=tuple([_HBM] * (2 * n)) + (_SEM, _SEM, pl.BlockSpec(memory_space=pltpu.VMEM)),
        input_output_aliases={i: i for i in range(n)},
        compiler_params=_cp(has_side_effects=_EFFECT),
    )(*[pltpu.with_memory_space_constraint(p, pltpu.HBM) for p in parts])
    return res[:n], res[n:2 * n], res[2 * n], res[2 * n + 1], res[2 * n + 2]


def _send_wait(parts, lands, send, recv, after, name):
    n = len(parts)

    def body(*refs):
        srcs, ins = refs[:n], refs[n:2 * n]
        send_ref, recv_ref = refs[2 * n], refs[2 * n + 1]
        x, y, c = lax.axis_index("x"), lax.axis_index("y"), lax.axis_index("c")
        for i in range(n):
            for m, chip in enumerate(_other_chips(x, y)):
                cp = pltpu.make_async_remote_copy(
                    src_ref=srcs[i].at[2 * chip[0] + chip[1]], dst_ref=ins[i].at[m],
                    send_sem=send_ref.at[3 * i + m], recv_sem=recv_ref.at[3 * i + m],
                    device_id=(chip[0], chip[1], c), device_id_type=MESH)
                cp.wait_send()
                cp.wait_recv()

    res = pl.pallas_call(
        body, name=name, out_shape=tuple(pltpu.HBM(a.shape, a.dtype) for a in list(parts) + list(lands)),
        in_specs=[_HBM] * (2 * n) + [_SEM, _SEM, _ANY], out_specs=tuple([_HBM] * (2 * n)),
        input_output_aliases={i: i for i in range(2 * n)},
        compiler_params=_cp(has_side_effects=_EFFECT),
    )(*parts, *lands, send, recv, after)
    return res[n:]


def _final_half(g4, r4, got3, jc_arr, name):
    _, rows, cols = r4.shape
    rb = _row_block(rows)
    nb = rows // rb

    def body(jc_ref, g_ref, r_ref, p_ref, o_ref):
        own = g_ref[0] + r_ref[0]
        o_ref[...] = ((own + p_ref[0].astype(F32)) + p_ref[1].astype(F32)) + p_ref[2].astype(F32)

    return pl.pallas_call(
        body, name=name,
        grid_spec=pltpu.PrefetchScalarGridSpec(
            num_scalar_prefetch=1, grid=(nb,),
            in_specs=[pl.BlockSpec((1, rb, cols), lambda i, jc: (jc[0], jc[1] * nb + i, 0)),
                      pl.BlockSpec((1, rb, cols), lambda i, jc: (jc[0], i, 0)),
                      pl.BlockSpec((3, rb, cols), lambda i, jc: (0, i, 0))],
            out_specs=pl.BlockSpec((rb, cols), lambda i, jc: (i, 0))),
        out_shape=jax.ShapeDtypeStruct((rows, cols), F32),
        compiler_params=_cp(dimension_semantics=("parallel",)),
    )(jc_arr, g4, r4, got3)


def _adamw_math(w, g, m, v):
    m2 = ADAM_B1 * m + (1.0 - ADAM_B1) * g
    v2 = ADAM_B2 * v + (1.0 - ADAM_B2) * (g * g)
    m_hat = m2 / (1.0 - ADAM_B1 ** ADAM_STEP)
    v_hat = v2 / (1.0 - ADAM_B2 ** ADAM_STEP)
    delta = -ADAM_LR * (m_hat / (jnp.sqrt(v_hat) + ADAM_EPS) + ADAM_WD * w)
    return delta, m2, v2


def _adamw_shard(mine, other, w, m, v, c_arr, name):
    rows, cols = w.shape
    half = rows // 2
    rb = _row_block(half)
    nb = half // rb

    def body(c_ref, a_ref, b_ref, w_ref, m_ref, v_ref, g_ref, d_ref, m2_ref, v2_ref):
        g = jnp.where(pl.program_id(0) == c_ref[0], a_ref[...], b_ref[...])
        delta, m2, v2 = _adamw_math(w_ref[...], g, m_ref[...], v_ref[...])
        g_ref[...] = g
        d_ref[...] = delta
        m2_ref[...] = m2
        v2_ref[...] = v2

    hb = pl.BlockSpec((rb, cols), lambda h, i, c: (i, 0))
    fb = pl.BlockSpec((rb, cols), lambda h, i, c: (h * nb + i, 0))
    shp = jax.ShapeDtypeStruct((rows, cols), F32)
    return pl.pallas_call(
        body, name=name,
        grid_spec=pltpu.PrefetchScalarGridSpec(num_scalar_prefetch=1, grid=(2, nb), in_specs=[hb, hb, fb, fb, fb],
                                               out_specs=(fb, fb, fb, fb)),
        out_shape=(shp, shp, shp, shp),
        compiler_params=_cp(dimension_semantics=("parallel", "parallel")),
    )(c_arr, mine, other, w, m, v)


def _small_allreduce(parts):
    n = len(parts)

    def body(*refs):
        p_refs, accs = refs[:n], refs[n:2 * n]
        bufs = refs[2 * n:3 * n]
        send, recv = refs[3 * n:]
        x, y, c = lax.axis_index("x"), lax.axis_index("y"), lax.axis_index("c")
        partners = [(x, y, 1 - c), (1 - x, y, c), (x, 1 - y, c)]
        for i in range(n):
            accs[i][...] = p_refs[i][...]
        for s, partner in enumerate(partners):
            cps = []
            for i in range(n):
                cp = pltpu.make_async_remote_copy(src_ref=accs[i], dst_ref=bufs[i].at[s], send_sem=send.at[3 * i + s],
                                                  recv_sem=recv.at[3 * i + s], device_id=partner, device_id_type=MESH)
                cp.start()
                cps.append(cp)
            for cp in cps:
                cp.wait()
            for i in range(n):
                accs[i][...] = accs[i][...] + bufs[i][s]

    vmem = pl.BlockSpec(memory_space=pltpu.VMEM)
    return pl.pallas_call(
        body, name="small_allreduce", out_shape=tuple(jax.ShapeDtypeStruct(p.shape, F32) for p in parts),
        in_specs=[vmem] * n, out_specs=tuple([vmem] * n),
        scratch_shapes=[pltpu.VMEM((3,) + p.shape, F32) for p in parts] + [
            pltpu.SemaphoreType.DMA((3 * n,)), pltpu.SemaphoreType.DMA((3 * n,))],
        compiler_params=_cp(),
    )(*parts)


def _small_adamw(reduced, params):
    n = len(reduced)
    n_w = len(params)

    def body(*refs):
        r_refs = refs[:n]
        wmv = refs[n:n + 3 * n_w]
        outs = refs[n + 3 * n_w:]
        outs[0][...] = r_refs[n - 1][0:1, 0:1]
        grads = [r_refs[0][...], r_refs[1][0:1, 0:N_Q_HEADS]] + [r_refs[i][...] for i in range(2, n_w)]
        for i in range(n_w):
            w_ref, m_ref, v_ref = wmv[3 * i:3 * i + 3]
            g_ref, d_ref, m2_ref, v2_ref = outs[1 + 4 * i:5 + 4 * i]
            delta, m2, v2 = _adamw_math(w_ref[...], grads[i], m_ref[...], v_ref[...])
            g_ref[...] = grads[i]
            d_ref[...] = delta
            m2_ref[...] = m2
            v2_ref[...] = v2

    flat = [a for p in params for a in p]
    vmem = pl.BlockSpec(memory_space=pltpu.VMEM)
    out_shape = [jax.ShapeDtypeStruct((1, 1), F32)]
    for p in params:
        out_shape += [jax.ShapeDtypeStruct(p[0].shape, F32)] * 4
    res = pl.pallas_call(
        body, name="small_adamw", out_shape=tuple(out_shape),
        in_specs=[vmem] * (n + len(flat)), out_specs=tuple([vmem] * len(out_shape)),
        compiler_params=_cp(),
    )(*reduced, *flat)
    return res[0], [res[1 + 4 * i:5 + 4 * i] for i in range(n_w)]


def kernel(x, attn_norm_g, w_in, attn_sinks, w_pool, pool_scale, w_out, mlp_norm_g, w_up, w_down, final_norm_g, loss_target, m_attn_norm_g, m_w_in, m_attn_sinks, m_w_pool, m_pool_scale, m_w_out, m_mlp_norm_g, m_w_up, m_w_down, m_final_norm_g, v_attn_norm_g, v_w_in, v_attn_sinks, v_w_pool, v_pool_scale, v_w_out, v_mlp_norm_g, v_w_up, v_w_down, v_final_norm_g):
    nseq, seq, d = x.shape
    c_idx = lax.axis_index("c").astype(jnp.int32)
    j_idx = (2 * lax.axis_index("x") + lax.axis_index("y")).astype(jnp.int32)
    c_arr = jnp.reshape(c_idx, (1,))
    jc_arr = jnp.stack([j_idx, c_idx])

    big_w = (w_in[0].T, w_out[0], w_up[0], w_down[0])
    big_m = (m_w_in[0].T, m_w_out[0], m_w_up[0], m_w_down[0])
    big_v = (v_w_in[0].T, v_w_out[0], v_w_up[0], v_w_down[0])
    (w_in_lands, w_in_send, w_in_recv), rest = _gather_start(big_w[:1], big_w[1:])
    (w_in4,) = _forward_now(_gather_wait(w_in_lands, w_in_send, w_in_recv, "gather_wait_w_in"), "forward_w_in")
    w_in_full = w_in4.reshape(IN_W, D_MODEL)
    class Comm(_NoComm):
        def attn_done(self, attn):
            arrived = _gather_wait(*rest, "gather_wait_rest", (attn,))
            self.lands, self.send, self.recv, token = _forward_start(arrived)
            return (token,)

        def rest_of_weights(self, pool):
            w_out4, w_up4, w_down4 = _forward_wait(self.lands, self.send, self.recv, pool)
            return w_out4.reshape(D_MODEL, D_MODEL), w_up4, w_down4

        def mlp_grads_ready(self, gw_up4, gw_down4):
            self.mlp = _sibling_start((gw_up4, gw_down4), "mlp_grads_to_sibling")
            return (self.mlp[4],)

        def outproj_done(self, gw_out):
            grads, lands, send, recv, _ = self.mlp
            gw_out4 = gw_out.reshape(N_CHIPS, OUT_SHARD, D_MODEL)
            mlp_grads, mlp_from_sib = _sibling_wait(grads, lands, send, recv, gw_out, "mlp_grads_to_sibling")
            self.grads = (gw_out4,) + tuple(mlp_grads)
            self.from_sib = tuple(_sibling_exchange((gw_out4,), True, "w_out_grad_to_sibling")) + tuple(mlp_from_sib)
            partials = [_chip_partial(g, r, c_arr, "chip_partial_" + nm)
                        for g, r, nm in zip(self.grads, self.from_sib, names[1:])]
            self.parts, self.part_lands, self.part_send, self.part_recv, token = _send_start(partials, "send_start")
            return (token,)

    names = ("w_in", "w_out", "w_up", "w_down")
    comm = Comm()
    gx, big_g, small_g = _local_step(
        x.reshape(nseq * seq, d), loss_target.reshape(nseq * seq, d), w_in_full, comm,
        attn_norm_g, attn_sinks.reshape(N_Q_HEADS), w_pool[0], pool_scale, mlp_norm_g, final_norm_g.reshape(1, d), seq)
    big_g = tuple(big_g[:1]) + comm.grads

    adamw = lambda i, a, b: _adamw_shard(a, b, big_w[i], big_m[i], big_v[i], c_arr, "adamw_" + names[i])
    sib_in = _sibling_start(big_g[:1], "w_in_grad_to_sibling")
    received = _send_wait(comm.parts, comm.part_lands, comm.part_send, comm.part_recv, sib_in[4], "send_wait")
    mine = [_final_half(g, r, p, jc_arr, "final_half_" + nm)
            for g, r, p, nm in zip(big_g[1:], comm.from_sib, received, names[1:])]
    (g_in,), (r_in,) = _sibling_wait(*sib_in[:4], mine[0], "w_in_grad_to_sibling")
    send_in = _send_start([_chip_partial(g_in, r_in, c_arr, "chip_partial_w_in")], "send_start_w_in")
    other = _sibling_exchange(mine, False, "halves_to_sibling", (send_in[4],))
    big_out = [None, adamw(1, mine[0], other[0]), adamw(2, mine[1], other[1]), None]
    (p_in,) = _send_wait(*send_in[:4], big_out[2][0], "send_wait_w_in")
    mine_in = _final_half(g_in, r_in, p_in, jc_arr, "final_half_w_in")
    (other_in,) = _sibling_exchange([mine_in], False, "w_in_half_to_sibling")
    big_out[3] = adamw(3, mine[2], other[2])
    big_out[0] = [t.T for t in adamw(0, mine_in, other_in)]

    wp_flat = lambda a: a.reshape(N_POOL * POOL_G, POOL_G)
    small_params = [
        (attn_norm_g, m_attn_norm_g, v_attn_norm_g),
        (attn_sinks, m_attn_sinks, v_attn_sinks),
        (wp_flat(w_pool), wp_flat(m_w_pool), wp_flat(v_w_pool)),
        (pool_scale, m_pool_scale, v_pool_scale),
        (mlp_norm_g, m_mlp_norm_g, v_mlp_norm_g),
        (final_norm_g.reshape(1, d), m_final_norm_g.reshape(1, d), v_final_norm_g.reshape(1, d)),
    ]
    loss, small_out = _small_adamw(_small_allreduce(small_g), small_params)

    def shaped(i, arr):
        return {1: w_in, 5: w_out, 7: w_up, 8: w_down, 0: attn_norm_g, 2: attn_sinks, 3: w_pool, 4: pool_scale,
                6: mlp_norm_g, 9: final_norm_g}[i].shape

    order = [small_out[0], big_out[0], small_out[1], small_out[2], small_out[3], big_out[1], small_out[4], big_out[2],
             big_out[3], small_out[5]]
    outs = [loss.reshape(()), gx.reshape(nseq, seq, d)]
    for kind in range(4):
        outs += [order[i][kind].reshape(shaped(i, None)) for i in range(10)]
    return tuple(outs)
```

```python
import jax
import jax.numpy as jnp
import numpy as np
from jax import lax
from jax.experimental import pallas as pl
from jax.experimental.pallas import tpu as pltpu

F32 = jnp.float32
BF16 = jnp.bfloat16
MXU_DTYPE = jnp.bfloat16

D_MODEL = 1024
HEAD_DIM = 64
N_Q_HEADS = 8
N_KV_HEADS = 2
Q_PER_KV = N_Q_HEADS // N_KV_HEADS
ATTN_W = N_Q_HEADS * HEAD_DIM
KV_W = N_KV_HEADS * HEAD_DIM
BLK = 128
POOL_WINDOWS = (2, 4, 8, 16)
N_POOL = len(POOL_WINDOWS)
POOL_W = D_MODEL - ATTN_W
POOL_G = POOL_W // N_POOL
IN_W = ATTN_W + 2 * KV_W + POOL_W
D_FF = 4 * D_MODEL
EPS = 1e-6
ROPE_THETA = 10000.0
N_CHIPS = 4
IN_SHARD = IN_W // N_CHIPS
OUT_SHARD = D_MODEL // N_CHIPS
FF_SHARD = D_FF // N_CHIPS
LANES = 128

ADAM_LR = 0.001
ADAM_B1 = 0.9
ADAM_B2 = 0.999
ADAM_EPS = 1e-08
ADAM_WD = 0.01
ADAM_STEP = 10

VMEM_LIMIT = 56 * 1024 * 1024
MESH = pl.DeviceIdType.MESH


def _cp(**kw):
    return pltpu.CompilerParams(vmem_limit_bytes=VMEM_LIMIT, **kw)


def _mm(a, b):
    return jnp.dot(a.astype(MXU_DTYPE), b.astype(MXU_DTYPE), preferred_element_type=F32)


def _mm_nt(a, b):
    return lax.dot_general(a.astype(MXU_DTYPE), b.astype(MXU_DTYPE), (((1,), (1,)), ((), ())),
                           preferred_element_type=F32)


def _mm_tn(a, b):
    return lax.dot_general(a.astype(MXU_DTYPE), b.astype(MXU_DTYPE), (((0,), (0,)), ((), ())),
                           preferred_element_type=F32)


def _resident(shape):
    nd = len(shape)
    return pl.BlockSpec(shape, lambda *_: (0,) * nd, pipeline_mode=pl.Buffered(1))


def _const(shape):
    nd = len(shape)
    return pl.BlockSpec(shape, lambda *_: (0,) * nd)


def _rope_tables(seq):
    half = HEAD_DIM // 2
    inv_freq = (ROPE_THETA ** (-np.arange(half, dtype=np.float32) / half)).astype(np.float32)
    ang = np.arange(seq, dtype=np.float32)[:, None] * inv_freq[None, :]
    cos, sin = np.cos(ang).astype(np.float32), np.sin(ang).astype(np.float32)
    cos_t = np.concatenate([cos, cos, cos, cos], axis=1)
    sin_t = np.concatenate([-sin, sin, -sin, sin], axis=1)
    return jnp.asarray(cos_t), jnp.asarray(sin_t)


def _swap_halves(xc):
    lane = lax.broadcasted_iota(jnp.int32, xc.shape, 1)
    return jnp.where((lane & 63) < 32, pltpu.roll(xc, 96, 1), pltpu.roll(xc, 32, 1))


_HBM = pl.BlockSpec(memory_space=pltpu.HBM)
_SEM = pl.BlockSpec(memory_space=pltpu.SEMAPHORE)
_ANY = pl.BlockSpec(memory_space=pl.ANY)
_EFFECT = pltpu.SideEffectType.DATAFLOW_SIDE_EFFECTING
TOKEN = jax.ShapeDtypeStruct((8, LANES), F32)


def _other_chips(x, y):
    return [(1 - x, y), (x, 1 - y), (1 - x, 1 - y)]


def _gather_start(first, rest):
    shards = tuple(first) + tuple(rest)
    nw, nf = len(shards), len(first)

    def body(*refs):
        srcs, lands = refs[:nw], refs[nw:2 * nw]
        sems = refs[2 * nw:2 * nw + 4]
        stages = refs[2 * nw + 4:3 * nw + 4]
        lsem = refs[3 * nw + 4]
        x, y, c = lax.axis_index("x"), lax.axis_index("y"), lax.axis_index("c")
        j = 2 * x + y
        for k in range(nw):
            send, recv = sems[0:2] if k < nf else sems[2:4]
            slot = k if k < nf else k - nf
            stages[k][...] = srcs[k][...].astype(BF16)
            local = pltpu.make_async_copy(stages[k], lands[k].at[j], lsem.at[k])
            local.start()
            local.wait()
            rows = shards[k].shape[0] // 2
            mine = lands[k].at[j, pl.ds(c * rows, rows)]
            for n, chip in enumerate(_other_chips(x, y)):
                pltpu.make_async_remote_copy(
                    src_ref=mine, dst_ref=mine, send_sem=send.at[slot * 3 + n], recv_sem=recv.at[slot * 3 + n],
                    device_id=(chip[0], chip[1], c), device_id_type=MESH).start()

    vmem = pl.BlockSpec(memory_space=pltpu.VMEM)
    nr = nw - nf
    res = pl.pallas_call(
        body, name="gather_start",
        out_shape=tuple(pltpu.HBM((N_CHIPS,) + s.shape, BF16) for s in shards) + (
            pltpu.SemaphoreType.DMA((3 * nf,)), pltpu.SemaphoreType.DMA((3 * nf,)),
            pltpu.SemaphoreType.DMA((3 * nr,)), pltpu.SemaphoreType.DMA((3 * nr,))),
        in_specs=[vmem] * nw, out_specs=tuple([_HBM] * nw) + (_SEM, _SEM, _SEM, _SEM),
        scratch_shapes=[pltpu.VMEM(s.shape, BF16) for s in shards] + [pltpu.SemaphoreType.DMA((nw,))],
        compiler_params=_cp(has_side_effects=_EFFECT),
    )(*shards)
    return (res[:nf], res[nw], res[nw + 1]), (res[nf:nw], res[nw + 2], res[nw + 3])


def _gather_wait(lands, send, recv, name, after=()):
    nw = len(lands)
    tokens, token_specs = _after(after)

    def body(*refs):
        ins = refs[:nw]
        send_ref, recv_ref = refs[nw], refs[nw + 1]
        x, y, c = lax.axis_index("x"), lax.axis_index("y"), lax.axis_index("c")
        j = 2 * x + y
        for k in range(nw):
            rows = lands[k].shape[1] // 2
            mine = ins[k].at[j, pl.ds(c * rows, rows)]
            for n, chip in enumerate(_other_chips(x, y)):
                got = ins[k].at[2 * chip[0] + chip[1], pl.ds(c * rows, rows)]
                cp = pltpu.make_async_remote_copy(
                    src_ref=mine, dst_ref=got, send_sem=send_ref.at[k * 3 + n], recv_sem=recv_ref.at[k * 3 + n],
                    device_id=(chip[0], chip[1], c), device_id_type=MESH)
                cp.wait_send()
                cp.wait_recv()

    return pl.pallas_call(
        body, name=name, out_shape=tuple(pltpu.HBM(a.shape, a.dtype) for a in lands),
        in_specs=[_HBM] * nw + [_SEM, _SEM] + token_specs, out_specs=tuple([_HBM] * nw),
        input_output_aliases={k: k for k in range(nw)},
        compiler_params=_cp(has_side_effects=_EFFECT),
    )(*lands, send, recv, *tokens)


def _forward_now(lands, name):
    nw = len(lands)
    rows_of = [a.shape[1] // 2 for a in lands]

    def body(*refs):
        mine, theirs = _forward_copies(refs[nw:2 * nw], nw, rows_of, refs[2 * nw], refs[2 * nw + 1])
        for cp in mine:
            cp.start()
        for cp in theirs:
            cp.wait_recv()
        for cp in mine:
            cp.wait_send()

    return pl.pallas_call(
        body, name=name, out_shape=tuple(jax.ShapeDtypeStruct(a.shape, a.dtype) for a in lands),
        in_specs=[_HBM] * nw, out_specs=tuple([_HBM] * nw), input_output_aliases={k: k for k in range(nw)},
        scratch_shapes=[pltpu.SemaphoreType.DMA((3 * nw,)), pltpu.SemaphoreType.DMA((3 * nw,))],
        compiler_params=_cp(),
    )(*lands)


def _forward_copies(refs, nw, rows_of, send, recv):
    x, y, c = lax.axis_index("x"), lax.axis_index("y"), lax.axis_index("c")
    out = []
    for k in range(nw):
        rows = rows_of[k]
        for n, chip in enumerate(_other_chips(x, y)):
            got = refs[k].at[2 * chip[0] + chip[1], pl.ds(c * rows, rows)]
            theirs = refs[k].at[2 * chip[0] + chip[1], pl.ds((1 - c) * rows, rows)]
            out.append(pltpu.make_async_remote_copy(
                src_ref=got, dst_ref=got, send_sem=send.at[k * 3 + n], recv_sem=recv.at[k * 3 + n],
                device_id=(x, y, 1 - c), device_id_type=MESH))
            out.append(pltpu.make_async_remote_copy(
                src_ref=theirs, dst_ref=theirs, send_sem=send.at[k * 3 + n], recv_sem=recv.at[k * 3 + n],
                device_id=(x, y, 1 - c), device_id_type=MESH))
    return out[0::2], out[1::2]


def _forward_start(lands):
    nw = len(lands)
    rows_of = [a.shape[1] // 2 for a in lands]

    def body(*refs):
        send, recv, token = refs[2 * nw:2 * nw + 3]
        mine, _ = _forward_copies(refs[:nw], nw, rows_of, send, recv)
        for cp in mine:
            cp.start()
        token[...] = jnp.zeros_like(token)

    res = pl.pallas_call(
        body, name="forward_start",
        out_shape=tuple(pltpu.HBM(a.shape, a.dtype) for a in lands) + (
            pltpu.SemaphoreType.DMA((3 * nw,)), pltpu.SemaphoreType.DMA((3 * nw,)), TOKEN),
        in_specs=[_HBM] * nw, out_specs=tuple([_HBM] * nw) + (_SEM, _SEM, pl.BlockSpec(memory_space=pltpu.VMEM)),
        input_output_aliases={k: k for k in range(nw)},
        compiler_params=_cp(has_side_effects=_EFFECT),
    )(*lands)
    return res[:nw], res[nw], res[nw + 1], res[nw + 2]


def _forward_wait(lands, send, recv, after):
    nw = len(lands)
    rows_of = [a.shape[1] // 2 for a in lands]

    def body(*refs):
        mine, theirs = _forward_copies(refs[:nw], nw, rows_of, refs[nw], refs[nw + 1])
        for cp in mine:
            cp.wait_send()
        for cp in theirs:
            cp.wait_recv()

    return pl.pallas_call(
        body, name="forward_wait", out_shape=tuple(pltpu.HBM(a.shape, a.dtype) for a in lands),
        in_specs=[_HBM] * nw + [_SEM, _SEM, _ANY], out_specs=tuple([_HBM] * nw),
        input_output_aliases={k: k for k in range(nw)},
        compiler_params=_cp(has_side_effects=_EFFECT),
    )(*lands, send, recv, after)


def _after(tokens):
    tokens = [t for t in tokens if t is not None]
    return tokens, [_ANY] * len(tokens)


def _fwd_inproj(x, g1, w_in, cos_t, sin_t, tm, after=()):
    t_tok = x.shape[0]
    seq = cos_t.shape[0]
    per_seq = seq // tm
    tokens, token_specs = _after(after)

    def body(x_ref, g_ref, w_ref, cos_ref, sin_ref, *rest):
        q_ref, k_ref, v_ref, u_ref = rest[len(tokens):]
        xv = x_ref[...]
        r = lax.rsqrt(jnp.mean(xv * xv, axis=-1, keepdims=True) + EPS)
        h = (xv * r) * g_ref[...]
        proj = _mm_nt(h, w_ref[...])
        cos, sin = cos_ref[...], sin_ref[...]
        for cidx in range((ATTN_W + KV_W) // LANES):
            xc = proj[:, cidx * LANES:(cidx + 1) * LANES]
            rot = xc * cos + _swap_halves(xc) * sin
            if cidx < ATTN_W // LANES:
                q_ref[:, cidx * LANES:(cidx + 1) * LANES] = (rot * (HEAD_DIM ** -0.5)).astype(q_ref.dtype)
            else:
                k_ref[...] = rot.astype(k_ref.dtype)
        v_ref[...] = proj[:, ATTN_W + KV_W:ATTN_W + 2 * KV_W].astype(v_ref.dtype)
        u_ref[...] = proj[:, ATTN_W + 2 * KV_W:]

    row = lambda w: pl.BlockSpec((tm, w), lambda i: (i, 0))
    kv = row(KV_W)
    tab = pl.BlockSpec((tm, LANES), lambda i: (i % per_seq, 0))
    kv_shape = jax.ShapeDtypeStruct((t_tok, KV_W), MXU_DTYPE)
    return pl.pallas_call(
        body, name="fwd_inproj", grid=(t_tok // tm,),
        in_specs=[row(D_MODEL), _const((1, D_MODEL)), _resident((IN_W, D_MODEL)), tab, tab] + token_specs,
        out_specs=(row(ATTN_W), kv, kv, row(POOL_W)),
        out_shape=(jax.ShapeDtypeStruct((t_tok, ATTN_W), MXU_DTYPE), kv_shape, kv_shape,
                   jax.ShapeDtypeStruct((t_tok, POOL_W), F32)),
        compiler_params=_cp(dimension_semantics=("parallel",)),
    )(x, g1, w_in, cos_t, sin_t, *tokens)


MASKED = -1e30


def _attn_bias():
    b = lax.broadcasted_iota(jnp.int32, (2 * BLK, BLK), 0)
    a = lax.broadcasted_iota(jnp.int32, (2 * BLK, BLK), 1)
    own = (b >= BLK) & (b - BLK <= a)
    prev = (b < BLK) & (b > a)
    return jnp.stack([jnp.where(own, 0.0, MASKED), jnp.where(own | prev, 0.0, MASKED)]).astype(F32)


def _sink_rows(sinks):
    return jnp.repeat(sinks.reshape(N_KV_HEADS, Q_PER_KV), BLK, axis=1).reshape(N_KV_HEADS, 1, Q_PER_KV * BLK)


def _stack_heads(ref, r0, g):
    return jnp.concatenate(
        [ref[pl.ds(r0, BLK), (Q_PER_KV * g + h) * HEAD_DIM:(Q_PER_KV * g + h + 1) * HEAD_DIM] for h in range(Q_PER_KV)],
        axis=0)


def _kv_window(ref, p0, r0, g):
    sl = slice(g * HEAD_DIM, (g + 1) * HEAD_DIM)
    return jnp.concatenate([ref[pl.ds(p0, BLK), sl], ref[pl.ds(r0, BLK), sl]], axis=0)


def _scores_t(k_ref, q_ref, bias, p0, r0, g):
    kk = _kv_window(k_ref, p0, r0, g)
    qs = _stack_heads(q_ref, r0, g)
    st = _mm_nt(kk, qs) + jnp.concatenate([bias] * Q_PER_KV, axis=1)
    return st, kk, qs


def _head_rows(ref, r0, g):
    return jnp.concatenate([ref[pl.ds(Q_PER_KV * g + h, 1), pl.ds(r0, BLK)] for h in range(Q_PER_KV)], axis=1)


def _pairs_to_rows(xt):
    out = []
    for t in range(Q_PER_KV // 2):
        pair = jnp.concatenate([xt[:, (2 * t) * BLK:(2 * t + 1) * BLK], xt[:, (2 * t + 1) * BLK:(2 * t + 2) * BLK]], axis=0)
        out.append(pair.T)
    return out


def _shift_rows(x, k, seq):
    row = lax.broadcasted_iota(jnp.int32, x.shape, 0)
    if k > 0:
        return jnp.where(row >= k, pltpu.roll(x, k, 0), 0.0)
    return jnp.where(row < seq + k, pltpu.roll(x, seq + k, 0), 0.0)


def _window_sum(x, w, seq, forward):
    s, k = x, 1
    while k < w:
        s = s + _shift_rows(s, -k if forward else k, seq)
        k *= 2
    return s


def _inv_count(seq, w):
    pos = lax.broadcasted_iota(jnp.int32, (seq, 1), 0)
    return 1.0 / jnp.minimum(pos + 1, w).astype(F32)


def _fwd_attn(q, k, v, sink_rows, bias, seq):
    t_tok = q.shape[0]
    nblk = seq // BLK

    def body(q_ref, k_ref, v_ref, sink_ref, bias_ref, o_ref, lse_ref):
        def blk(i, carry):
            r0 = pl.multiple_of(i * BLK, BLK)
            p0 = pl.multiple_of(jnp.maximum(i - 1, 0) * BLK, BLK)
            bias = bias_ref[jnp.minimum(i, 1)]
            for g in range(N_KV_HEADS):
                st, _, _ = _scores_t(k_ref, q_ref, bias, p0, r0, g)
                sink = sink_ref[g]
                m = jnp.maximum(jnp.max(st, axis=0, keepdims=True), sink)
                p = jnp.exp(st - m).astype(MXU_DTYPE)
                vv = _kv_window(v_ref, p0, r0, g)
                ot = _mm_tn(jnp.concatenate([vv, jnp.ones_like(vv)], axis=1), p)
                den = ot[HEAD_DIM:HEAD_DIM + 1] + jnp.exp(sink - m)
                lse = m + jnp.log(den)
                for h in range(Q_PER_KV):
                    lse_ref[pl.ds(Q_PER_KV * g + h, 1), pl.ds(r0, BLK)] = lse[:, h * BLK:(h + 1) * BLK]
                ot = ot[:HEAD_DIM] * (1.0 / den)
                for t, tile in enumerate(_pairs_to_rows(ot)):
                    c0 = (Q_PER_KV * g + 2 * t) * HEAD_DIM
                    o_ref[pl.ds(r0, BLK), c0:c0 + LANES] = tile.astype(o_ref.dtype)
            return carry

        lax.fori_loop(0, nblk, blk, 0, unroll=4)

    row = lambda w: pl.BlockSpec((seq, w), lambda i: (i, 0))
    kv = row(KV_W)
    return pl.pallas_call(
        body, name="fwd_attn", grid=(t_tok // seq,),
        in_specs=[row(ATTN_W), kv, kv, _const((N_KV_HEADS, 1, Q_PER_KV * BLK)), _const((2, 2 * BLK, BLK))],
        out_specs=(row(ATTN_W), pl.BlockSpec((N_Q_HEADS, seq), lambda i: (0, i))),
        out_shape=(jax.ShapeDtypeStruct((t_tok, ATTN_W), MXU_DTYPE), jax.ShapeDtypeStruct((N_Q_HEADS, t_tok), F32)),
        compiler_params=_cp(dimension_semantics=("parallel",)),
    )(q, k, v, sink_rows, bias)


def _fwd_pool(u, w_pool, pool_scale, seq, after=()):
    t_tok = u.shape[0]
    tokens, token_specs = _after(after)

    def body(u_ref, wp_ref, sc_ref, *rest):
        o_ref = rest[-1]
        for gi, w in enumerate(POOL_WINDOWS):
            sl = slice(gi * POOL_G, (gi + 1) * POOL_G)
            ug = u_ref[:, sl]
            d = _window_sum(ug, w, seq, False) * _inv_count(seq, w) - ug
            o_ref[:, sl] = (_mm(d, wp_ref[gi]) * sc_ref[:, sl]).astype(o_ref.dtype)

    row = pl.BlockSpec((seq, POOL_W), lambda i: (i, 0))
    return pl.pallas_call(
        body, name="fwd_pool", grid=(t_tok // seq,),
        in_specs=[row, _const((N_POOL, POOL_G, POOL_G)), _const((1, POOL_W))] + token_specs, out_specs=row,
        out_shape=jax.ShapeDtypeStruct((t_tok, POOL_W), MXU_DTYPE),
        compiler_params=_cp(dimension_semantics=("parallel",)),
    )(u, w_pool, pool_scale, *tokens)


def _rms_bwd(dy_g, xn, r):
    return r * (dy_g - xn * jnp.mean(dy_g * xn, axis=-1, keepdims=True))


def _mlp_fwd_bwd(x, attn, pool, target, w_out, w_up4, w_down4, g2, gf, tm):
    t_tok = x.shape[0]

    def body(x_ref, attn_ref, pool_ref, tgt_ref, wo_ref, wu_ref, wd_ref, g2_ref, gf_ref,
             h2_ref, a_ref, da_ref, dx2b_ref, dx1_ref, dx1b_ref, loss_ref, dgf_ref, dg2_ref):
        @pl.when(pl.program_id(0) == 0)
        def _():
            loss_ref[...] = jnp.zeros_like(loss_ref)
            dgf_ref[...] = jnp.zeros_like(dgf_ref)
            dg2_ref[...] = jnp.zeros_like(dg2_ref)

        x1 = x_ref[...] + (_mm(attn_ref[...], wo_ref[:ATTN_W]) + _mm(pool_ref[...], wo_ref[ATTN_W:]))
        r2 = lax.rsqrt(jnp.mean(x1 * x1, axis=-1, keepdims=True) + EPS)
        xn1 = x1 * r2
        g2 = g2_ref[...]
        h2 = (xn1 * g2).astype(MXU_DTYPE)
        h2_ref[...] = h2
        acc = jnp.zeros((tm, D_MODEL), F32)
        for j in range(N_CHIPS):
            a = _mm(h2, wu_ref[j])
            a_ref[:, j * FF_SHARD:(j + 1) * FF_SHARD] = a.astype(a_ref.dtype)
            acc = acc + _mm(jnp.square(jnp.maximum(a, 0.0)), wd_ref[j])
        x2 = x1 + acc
        r3 = lax.rsqrt(jnp.mean(x2 * x2, axis=-1, keepdims=True) + EPS)
        xn2 = x2 * r3
        gf_v = gf_ref[...]
        err = xn2 * gf_v - tgt_ref[...]
        part = jnp.sum(err * err) * (0.5 / D_MODEL)
        first = (lax.broadcasted_iota(jnp.int32, loss_ref.shape, 0) == 0) & (lax.broadcasted_iota(jnp.int32, loss_ref.shape, 1) == 0)
        loss_ref[...] += jnp.where(first, part, 0.0)
        dy = err * (1.0 / D_MODEL)
        dgf_ref[...] += jnp.sum(dy * xn2, axis=0, keepdims=True)
        dx2 = _rms_bwd(dy * gf_v, xn2, r3)
        dx2b = dx2.astype(MXU_DTYPE)
        dx2b_ref[...] = dx2b
        dh2 = jnp.zeros((tm, D_MODEL), F32)
        for j in range(N_CHIPS):
            sl = slice(j * FF_SHARD, (j + 1) * FF_SHARD)
            dhid = _mm_nt(dx2b, wd_ref[j])
            da = (dhid * (2.0 * jnp.maximum(a_ref[:, sl].astype(F32), 0.0))).astype(MXU_DTYPE)
            da_ref[:, sl] = da
            dh2 = dh2 + _mm_nt(da, wu_ref[j])
        dg2_ref[...] += jnp.sum(dh2 * xn1, axis=0, keepdims=True)
        dx1 = dx2 + _rms_bwd(dh2 * g2, xn1, r2)
        dx1_ref[...] = dx1
        dx1b_ref[...] = dx1.astype(dx1b_ref.dtype)

    row = lambda w: pl.BlockSpec((tm, w), lambda i: (i, 0))
    vec = jax.ShapeDtypeStruct((1, D_MODEL), F32)
    return pl.pallas_call(
        body, name="mlp_fwd_bwd", grid=(t_tok // tm,),
        in_specs=[row(D_MODEL), row(ATTN_W), row(POOL_W), row(D_MODEL), _resident((D_MODEL, D_MODEL)),
                  _resident((N_CHIPS, D_MODEL, FF_SHARD)), _resident((N_CHIPS, FF_SHARD, D_MODEL)),
                  _const((1, D_MODEL)), _const((1, D_MODEL))],
        out_specs=(row(D_MODEL), row(D_FF), row(D_FF), row(D_MODEL), row(D_MODEL), row(D_MODEL),
                   _const((8, LANES)), _const((1, D_MODEL)), _const((1, D_MODEL))),
        out_shape=(jax.ShapeDtypeStruct((t_tok, D_MODEL), MXU_DTYPE), jax.ShapeDtypeStruct((t_tok, D_FF), MXU_DTYPE),
                   jax.ShapeDtypeStruct((t_tok, D_FF), MXU_DTYPE), jax.ShapeDtypeStruct((t_tok, D_MODEL), MXU_DTYPE),
                   jax.ShapeDtypeStruct((t_tok, D_MODEL), F32), jax.ShapeDtypeStruct((t_tok, D_MODEL), MXU_DTYPE),
                   jax.ShapeDtypeStruct((8, LANES), F32), vec, vec),
        compiler_params=_cp(dimension_semantics=("arbitrary",)),
    )(x, attn, pool, target, w_out, w_up4, w_down4, g2, gf)


def _bwd_mlp_wgrads(h2, da, a, dx2b, tk):
    t_tok = h2.shape[0]

    def body(h2_ref, da_ref, a_ref, dx2b_ref, gup_ref, gdn_ref):
        @pl.when(pl.program_id(1) == 0)
        def _():
            gup_ref[...] = jnp.zeros_like(gup_ref)
            gdn_ref[...] = jnp.zeros_like(gdn_ref)

        gup_ref[0] += _mm_tn(h2_ref[...], da_ref[...])
        hid = jnp.square(jnp.maximum(a_ref[...].astype(F32), 0.0))
        gdn_ref[0] += _mm_tn(hid, dx2b_ref[...])

    tok = pl.BlockSpec((tk, D_MODEL), lambda j, t: (t, 0))
    ffb = pl.BlockSpec((tk, FF_SHARD), lambda j, t: (t, j))
    wblk = pl.BlockSpec((1, D_MODEL, D_MODEL), lambda j, t: (j, 0, 0))
    return pl.pallas_call(
        body, name="bwd_mlp_wgrads", grid=(N_CHIPS, t_tok // tk),
        in_specs=[tok, ffb, ffb, tok], out_specs=(wblk, wblk),
        out_shape=(jax.ShapeDtypeStruct((N_CHIPS, D_MODEL, FF_SHARD), F32), jax.ShapeDtypeStruct((N_CHIPS, FF_SHARD, D_MODEL), F32)),
        compiler_params=_cp(dimension_semantics=("parallel", "arbitrary")),
    )(h2, da, a, dx2b)


def _head_selector():
    ch = lax.broadcasted_iota(jnp.int32, (ATTN_W, LANES), 0)
    col = lax.broadcasted_iota(jnp.int32, (ATTN_W, LANES), 1)
    return (ch // HEAD_DIM == col).astype(MXU_DTYPE)


def _bwd_outproj(dx1b, attn, pool, w_out, head_sel, tm, after=()):
    t_tok = dx1b.shape[0]
    tokens, token_specs = _after(after)

    def body(dx_ref, attn_ref, pool_ref, wo_ref, sel_ref, *rest):
        dattn_ref, dpool_ref, delta_ref, gwo_ref = rest[len(tokens):]

        @pl.when(pl.program_id(0) == 0)
        def _():
            gwo_ref[...] = jnp.zeros_like(gwo_ref)

        dx = dx_ref[...]
        attn = attn_ref[...]
        dattn = _mm_nt(dx, wo_ref[:ATTN_W])
        dattn_ref[...] = dattn.astype(dattn_ref.dtype)
        dpool_ref[...] = _mm_nt(dx, wo_ref[ATTN_W:])
        prod = dattn * attn.astype(F32)
        hi = prod.astype(MXU_DTYPE)
        lo = prod - hi.astype(F32)
        delta = _mm(hi, sel_ref[...]) + _mm(lo, sel_ref[...])
        delta_ref[...] = delta.T[:N_Q_HEADS]
        gwo_ref[:ATTN_W] += _mm_tn(attn, dx)
        gwo_ref[ATTN_W:] += _mm_tn(pool_ref[...], dx)

    row = lambda w: pl.BlockSpec((tm, w), lambda i: (i, 0))
    return pl.pallas_call(
        body, name="bwd_outproj", grid=(t_tok // tm,),
        in_specs=[row(D_MODEL), row(ATTN_W), row(POOL_W), _resident((D_MODEL, D_MODEL)), _const((ATTN_W, LANES))] + token_specs,
        out_specs=(row(ATTN_W), row(POOL_W), pl.BlockSpec((N_Q_HEADS, tm), lambda i: (0, i)), _const((D_MODEL, D_MODEL))),
        out_shape=(jax.ShapeDtypeStruct((t_tok, ATTN_W), MXU_DTYPE), jax.ShapeDtypeStruct((t_tok, POOL_W), F32),
                   jax.ShapeDtypeStruct((N_Q_HEADS, t_tok), F32), jax.ShapeDtypeStruct((D_MODEL, D_MODEL), F32)),
        compiler_params=_cp(dimension_semantics=("arbitrary",)),
    )(dx1b, attn, pool, w_out, head_sel, *tokens)


def _bwd_attn(q, k, v, dattn, lse, delta, sink_rows, bias, cos_t, sin_t, seq, after=()):
    t_tok = q.shape[0]
    nblk = seq // BLK
    qkv_w = ATTN_W + 2 * KV_W
    tokens, token_specs = _after(after)

    def unrope(d, cos, sin):
        return d * cos - _swap_halves(d) * sin

    def body(q_ref, k_ref, v_ref, do_ref, lse_ref, delta_ref, sink_ref, bias_ref, cos_ref, sin_ref, *rest):
        dqkv_ref, dsink_ref, dk_acc, dv_acc = rest[len(tokens):]

        @pl.when(pl.program_id(0) == 0)
        def _():
            dsink_ref[...] = jnp.zeros_like(dsink_ref)

        dk_acc[...] = jnp.zeros_like(dk_acc)
        dv_acc[...] = jnp.zeros_like(dv_acc)

        def blk(i, dsink):
            r0 = pl.multiple_of(i * BLK, BLK)
            p0 = pl.multiple_of(jnp.maximum(i - 1, 0) * BLK, BLK)
            bias = bias_ref[jnp.minimum(i, 1)]
            cos, sin = cos_ref[pl.ds(r0, BLK), :], sin_ref[pl.ds(r0, BLK), :]
            new = []
            for g in range(N_KV_HEADS):
                st, kk, qs = _scores_t(k_ref, q_ref, bias, p0, r0, g)
                lse_g, delta_g = _head_rows(lse_ref, r0, g), _head_rows(delta_ref, r0, g)
                pn = jnp.exp(st - lse_g)
                dos = _stack_heads(do_ref, r0, g)
                dst = pn * (_mm_nt(_kv_window(v_ref, p0, r0, g), dos) - delta_g)
                new.append(dsink[g] - jnp.exp(sink_ref[g] - lse_g) * delta_g)
                dqt = _mm_tn(kk, dst) * (HEAD_DIM ** -0.5)
                for t, tile in enumerate(_pairs_to_rows(dqt)):
                    c0 = (Q_PER_KV * g + 2 * t) * HEAD_DIM
                    dqkv_ref[pl.ds(r0, BLK), c0:c0 + LANES] = unrope(tile, cos, sin).astype(dqkv_ref.dtype)
                dkk = _mm(dst, qs)
                dvv = _mm(pn, dos)
                dk_acc[g, pl.ds(p0, BLK), :] += dkk[:BLK]
                dk_acc[g, pl.ds(r0, BLK), :] += dkk[BLK:]
                dv_acc[g, pl.ds(p0, BLK), :] += dvv[:BLK]
                dv_acc[g, pl.ds(r0, BLK), :] += dvv[BLK:]
            return tuple(new)

        zero = jnp.zeros((1, Q_PER_KV * BLK), F32)
        dsink = lax.fori_loop(0, nblk // 2, lambda i2, acc: blk(2 * i2 + 1, blk(2 * i2, acc)), (zero,) * N_KV_HEADS)
        lane = lax.broadcasted_iota(jnp.int32, dsink_ref.shape, 1)
        row = lax.broadcasted_iota(jnp.int32, dsink_ref.shape, 0)
        tile = jnp.zeros(dsink_ref.shape, F32)
        for g in range(N_KV_HEADS):
            for h in range(Q_PER_KV):
                tot = jnp.sum(dsink[g][:, h * BLK:(h + 1) * BLK])
                tile = tile + jnp.where((row == 0) & (lane == Q_PER_KV * g + h), tot, 0.0)
        dsink_ref[...] += tile
        dk = jnp.concatenate([dk_acc[g] for g in range(N_KV_HEADS)], axis=1)
        dqkv_ref[:, ATTN_W:ATTN_W + KV_W] = unrope(dk, cos_ref[...], sin_ref[...]).astype(dqkv_ref.dtype)
        dqkv_ref[:, ATTN_W + KV_W:] = jnp.concatenate([dv_acc[g] for g in range(N_KV_HEADS)], axis=1).astype(dqkv_ref.dtype)

    row = lambda w: pl.BlockSpec((seq, w), lambda i: (i, 0))
    kv = row(KV_W)
    per_head = pl.BlockSpec((N_Q_HEADS, seq), lambda i: (0, i))
    return pl.pallas_call(
        body, name="bwd_attn", grid=(t_tok // seq,),
        in_specs=[row(ATTN_W), kv, kv, row(ATTN_W), per_head, per_head, _const((N_KV_HEADS, 1, Q_PER_KV * BLK)),
                  _const((2, 2 * BLK, BLK)), _resident((seq, LANES)), _resident((seq, LANES))] + token_specs,
        out_specs=(row(qkv_w), _const((8, LANES))),
        out_shape=(jax.ShapeDtypeStruct((t_tok, qkv_w), MXU_DTYPE), jax.ShapeDtypeStruct((8, LANES), F32)),
        scratch_shapes=[pltpu.VMEM((N_KV_HEADS, seq, HEAD_DIM), F32), pltpu.VMEM((N_KV_HEADS, seq, HEAD_DIM), F32)],
        compiler_params=_cp(dimension_semantics=("arbitrary",)),
    )(q, k, v, dattn, lse, delta, sink_rows, bias, cos_t, sin_t, *tokens)


def _bwd_pool(u, dpool, w_pool, pool_scale, seq):
    t_tok = u.shape[0]

    def body(u_ref, dp_ref, wp_ref, sc_ref, du_ref, dwp_ref, dsc_ref):
        @pl.when(pl.program_id(0) == 0)
        def _():
            dwp_ref[...] = jnp.zeros_like(dwp_ref)
            dsc_ref[...] = jnp.zeros_like(dsc_ref)

        for gi, w in enumerate(POOL_WINDOWS):
            sl = slice(gi * POOL_G, (gi + 1) * POOL_G)
            ug = u_ref[:, sl]
            inv = _inv_count(seq, w)
            d = (_window_sum(ug, w, seq, False) * inv - ug).astype(MXU_DTYPE)
            y = _mm(d, wp_ref[gi])
            dpool = dp_ref[:, sl]
            dsc_ref[:, sl] += jnp.sum(y * dpool, axis=0, keepdims=True)
            dy = (dpool * sc_ref[:, sl]).astype(MXU_DTYPE)
            dwp_ref[gi] += _mm_tn(d, dy)
            dd = _mm_nt(dy, wp_ref[gi])
            du_ref[:, sl] = (_window_sum(dd * inv, w, seq, True) - dd).astype(du_ref.dtype)

    row = pl.BlockSpec((seq, POOL_W), lambda i: (i, 0))
    return pl.pallas_call(
        body, name="bwd_pool", grid=(t_tok // seq,),
        in_specs=[row, row, _const((N_POOL, POOL_G, POOL_G)), _const((1, POOL_W))],
        out_specs=(row, _const((N_POOL, POOL_G, POOL_G)), _const((1, POOL_W))),
        out_shape=(jax.ShapeDtypeStruct((t_tok, POOL_W), MXU_DTYPE), jax.ShapeDtypeStruct((N_POOL, POOL_G, POOL_G), F32),
                   jax.ShapeDtypeStruct((1, POOL_W), F32)),
        compiler_params=_cp(dimension_semantics=("arbitrary",)),
    )(u, dpool, w_pool, pool_scale)


def _bwd_inproj(dqkv, du, x, dx1, w_in, g1, tm):
    t_tok = x.shape[0]
    nsteps = t_tok // tm

    qkv_w = ATTN_W + 2 * KV_W

    def body(dqkv_ref, du_ref, x_ref, dx1_ref, w_ref, g_ref, gx_ref, gw_ref, dg_ref):
        @pl.when(pl.program_id(0) == 0)
        def _():
            gw_ref[...] = jnp.zeros_like(gw_ref)
            dg_ref[...] = jnp.zeros_like(dg_ref)

        dqkv, du = dqkv_ref[...], du_ref[...]
        xv = x_ref[...]
        r = lax.rsqrt(jnp.mean(xv * xv, axis=-1, keepdims=True) + EPS)
        xn = xv * r
        g = g_ref[...]
        dh = _mm(dqkv, w_ref[:qkv_w]) + _mm(du, w_ref[qkv_w:])
        dg_ref[...] += jnp.sum(dh * xn, axis=0, keepdims=True)
        gx_ref[...] = dx1_ref[...] + _rms_bwd(dh * g, xn, r)
        h = (xn * g).astype(MXU_DTYPE)
        gw_ref[:qkv_w] += _mm_tn(dqkv, h)
        gw_ref[qkv_w:] += _mm_tn(du, h)

    row = lambda w: pl.BlockSpec((tm, w), lambda i: (i, 0))
    return pl.pallas_call(
        body, name="bwd_inproj", grid=(nsteps,),
        in_specs=[row(qkv_w), row(POOL_W), row(D_MODEL), row(D_MODEL), _resident((IN_W, D_MODEL)), _const((1, D_MODEL))],
        out_specs=(row(D_MODEL), _const((IN_W, D_MODEL)), _const((1, D_MODEL))),
        out_shape=(jax.ShapeDtypeStruct((t_tok, D_MODEL), F32), jax.ShapeDtypeStruct((IN_W, D_MODEL), F32),
                   jax.ShapeDtypeStruct((1, D_MODEL), F32)),
        compiler_params=_cp(dimension_semantics=("arbitrary",)),
    )(dqkv, du, x, dx1, w_in, g1)


class _NoComm:
    def start(self):
        return ()

    def attn_done(self, attn):
        return ()

    def rest_of_weights(self, pool):
        raise NotImplementedError

    def mlp_grads_ready(self, gw_up4, gw_down4):
        return ()

    def outproj_done(self, gw_out):
        return ()


def _local_step(x, target, w_in, comm, g1, sinks, w_pool, pool_scale, g2, gf, seq):
    tm = min(512, seq)
    tm_big = min(1024, seq)
    tm_mlp = min(256, seq)
    cos_t, sin_t = _rope_tables(seq)
    sink_rows, bias = _sink_rows(sinks), _attn_bias()
    q, k, v, u = _fwd_inproj(x, g1, w_in, cos_t, sin_t, tm_big, comm.start())
    attn, lse = _fwd_attn(q, k, v, sink_rows, bias, seq)
    pool = _fwd_pool(u, w_pool, pool_scale, seq, comm.attn_done(attn))
    w_out, w_up4, w_down4 = comm.rest_of_weights(pool)
    h2, a, da, dx2b, dx1, dx1b, loss, dgf, dg2 = _mlp_fwd_bwd(x, attn, pool, target, w_out, w_up4, w_down4, g2, gf, tm_mlp)
    gw_up4, gw_down4 = _bwd_mlp_wgrads(h2, da, a, dx2b, min(1024, x.shape[0]))
    dattn, dpool, delta, gw_out = _bwd_outproj(dx1b, attn, pool, w_out, _head_selector(), tm_big,
                                               comm.mlp_grads_ready(gw_up4, gw_down4))
    dqkv, dsinks = _bwd_attn(q, k, v, dattn, lse, delta, sink_rows, bias, cos_t, sin_t, seq, comm.outproj_done(gw_out))
    du, dwp, dsc = _bwd_pool(u, dpool, w_pool, pool_scale, seq)
    gx, gw_in_t, dg1 = _bwd_inproj(dqkv, du, x, dx1, w_in, g1, tm)
    big = (gw_in_t.reshape(N_CHIPS, IN_SHARD, D_MODEL), gw_out.reshape(N_CHIPS, OUT_SHARD, D_MODEL), gw_up4, gw_down4)
    small = (dg1, dsinks, dwp.reshape(N_POOL * POOL_G, POOL_G), dsc, dg2, dgf, loss)
    return gx, big, small


def _sibling_exchange(arrs, pick_half, name, after=()):
    n = len(arrs)
    tokens, token_specs = _after(after)

    def out_shape(a):
        if pick_half:
            return (a.shape[0], a.shape[1] // 2) + a.shape[2:]
        return a.shape

    def body(*refs):
        srcs, dsts = refs[:n], refs[n + len(tokens):2 * n + len(tokens)]
        send, recv = refs[2 * n + len(tokens):]
        x, y, c = lax.axis_index("x"), lax.axis_index("y"), lax.axis_index("c")
        cps = []
        for i in range(n):
            src = srcs[i]
            if pick_half:
                h = src.shape[1] // 2
                src = src.at[:, pl.ds((1 - c) * h, h)]
            cp = pltpu.make_async_remote_copy(src_ref=src, dst_ref=dsts[i], send_sem=send.at[i], recv_sem=recv.at[i],
                                              device_id=(x, y, 1 - c), device_id_type=MESH)
            cp.start()
            cps.append(cp)
        for cp in cps:
            cp.wait()

    hbm = pl.BlockSpec(memory_space=pltpu.HBM)
    return pl.pallas_call(
        body, name=name, out_shape=tuple(jax.ShapeDtypeStruct(out_shape(a), a.dtype) for a in arrs),
        in_specs=[hbm] * n + token_specs, out_specs=tuple([hbm] * n),
        scratch_shapes=[pltpu.SemaphoreType.DMA((n,)), pltpu.SemaphoreType.DMA((n,))],
        compiler_params=_cp(),
    )(*arrs, *tokens)


def _sibling_copies(srcs, lands, send, recv, pick_half):
    x, y, c = lax.axis_index("x"), lax.axis_index("y"), lax.axis_index("c")
    cps = []
    for i in range(len(srcs)):
        src = srcs[i]
        if pick_half:
            h = src.shape[1] // 2
            src = src.at[:, pl.ds((1 - c) * h, h)]
        cps.append(pltpu.make_async_remote_copy(
            src_ref=src, dst_ref=lands[i], send_sem=send.at[i], recv_sem=recv.at[i],
            device_id=(x, y, 1 - c), device_id_type=MESH))
    return cps


def _sibling_start(arrs, name, pick_half=True):
    n = len(arrs)

    def body(*refs):
        send, recv, token = refs[3 * n:3 * n + 3]
        for cp in _sibling_copies(refs[:n], refs[2 * n:3 * n], send, recv, pick_half):
            cp.start()
        token[...] = jnp.zeros_like(token)

    half = lambda a: ((a.shape[0], a.shape[1] // 2) + a.shape[2:]) if pick_half else a.shape
    res = pl.pallas_call(
        body, name=name + "_start",
        out_shape=tuple(pltpu.HBM(a.shape, a.dtype) for a in arrs) + tuple(pltpu.HBM(half(a), a.dtype) for a in arrs) + (
            pltpu.SemaphoreType.DMA((n,)), pltpu.SemaphoreType.DMA((n,)), TOKEN),
        in_specs=[_HBM] * n, out_specs=tuple([_HBM] * (2 * n)) + (_SEM, _SEM, pl.BlockSpec(memory_space=pltpu.VMEM)),
        input_output_aliases={i: i for i in range(n)},
        compiler_params=_cp(has_side_effects=_EFFECT),
    )(*[pltpu.with_memory_space_constraint(a, pltpu.HBM) for a in arrs])
    return res[:n], res[n:2 * n], res[2 * n], res[2 * n + 1], res[2 * n + 2]


def _sibling_wait(arrs, lands, send, recv, after, name, pick_half=True):
    n = len(arrs)

    def body(*refs):
        for cp in _sibling_copies(refs[:n], refs[n:2 * n], refs[2 * n], refs[2 * n + 1], pick_half):
            cp.wait_send()
            cp.wait_recv()

    res = pl.pallas_call(
        body, name=name + "_wait", out_shape=tuple(pltpu.HBM(a.shape, a.dtype) for a in list(arrs) + list(lands)),
        in_specs=[_HBM] * (2 * n) + [_SEM, _SEM, _ANY], out_specs=tuple([_HBM] * (2 * n)),
        input_output_aliases={i: i for i in range(2 * n)},
        compiler_params=_cp(has_side_effects=_EFFECT),
    )(*arrs, *lands, send, recv, after)
    return res[:n], res[n:]


def _row_block(rows):
    if rows <= 256:
        return rows
    for cand in (256, 128, 64, 32, 16, 8):
        if rows % cand == 0:
            return cand
    raise ValueError(rows)


def _chip_partial(g4, r4, c_arr, name):
    _, rows, cols = r4.shape
    rb = _row_block(rows)
    nb = rows // rb

    def body(c_ref, g_ref, r_ref, o_ref):
        o_ref[...] = (g_ref[...] + r_ref[...]).astype(o_ref.dtype)

    return pl.pallas_call(
        body, name=name,
        grid_spec=pltpu.PrefetchScalarGridSpec(
            num_scalar_prefetch=1, grid=(N_CHIPS, nb),
            in_specs=[pl.BlockSpec((1, rb, cols), lambda s, i, c: (s, c[0] * nb + i, 0)),
                      pl.BlockSpec((1, rb, cols), lambda s, i, c: (s, i, 0))],
            out_specs=pl.BlockSpec((1, rb, cols), lambda s, i, c: (s, i, 0))),
        out_shape=jax.ShapeDtypeStruct(r4.shape, BF16),
        compiler_params=_cp(dimension_semantics=("parallel", "parallel")),
    )(c_arr, g4, r4)


def _send_start(parts, name):
    n = len(parts)

    def body(*refs):
        srcs, lands = refs[:n], refs[2 * n:3 * n]
        send, recv, token = refs[3 * n:3 * n + 3]
        x, y, c = lax.axis_index("x"), lax.axis_index("y"), lax.axis_index("c")
        for i in range(n):
            for m, chip in enumerate(_other_chips(x, y)):
                pltpu.make_async_remote_copy(
                    src_ref=srcs[i].at[2 * chip[0] + chip[1]], dst_ref=lands[i].at[m],
                    send_sem=send.at[3 * i + m], recv_sem=recv.at[3 * i + m],
                    device_id=(chip[0], chip[1], c), device_id_type=MESH).start()
        token[...] = jnp.zeros_like(token)

    res = pl.pallas_call(
        body, name=name,
        out_shape=tuple(pltpu.HBM(p.shape, p.dtype) for p in parts) + tuple(pltpu.HBM((3,) + p.shape[1:], p.dtype) for p in parts) + (
            pltpu.SemaphoreType.DMA((3 * n,)), pltpu.SemaphoreType.DMA((3 * n,)), TOKEN),
        in_specs=[_HBM] * n, out_specs=tuple([_HBM] * (2 * n)) + (_SEM, _SEM, pl.BlockSpec(memory_space=pltpu.VMEM)),
        input_output_aliases={i: i for i in range(n)},
        compiler_params=_cp(has_side_effects=_EFFECT),
    )(*[pltpu.with_memory_space_constraint(p, pltpu.HBM) for p in parts])
    return res[:n], res[n:2 * n], res[2 * n], res[2 * n + 1], res[2 * n + 2]


def _send_wait(parts, lands, send, recv, after, name):
    n = len(parts)

    def body(*refs):
        srcs, ins = refs[:n], refs[n:2 * n]
        send_ref, recv_ref = refs[2 * n], refs[2 * n + 1]
        x, y, c = lax.axis_index("x"), lax.axis_index("y"), lax.axis_index("c")
        for i in range(n):
            for m, chip in enumerate(_other_chips(x, y)):
                cp = pltpu.make_async_remote_copy(
                    src_ref=srcs[i].at[2 * chip[0] + chip[1]], dst_ref=ins[i].at[m],
                    send_sem=send_ref.at[3 * i + m], recv_sem=recv_ref.at[3 * i + m],
                    device_id=(chip[0], chip[1], c), device_id_type=MESH)
                cp.wait_send()
                cp.wait_recv()

    res = pl.pallas_call(
        body, name=name, out_shape=tuple(pltpu.HBM(a.shape, a.dtype) for a in list(parts) + list(lands)),
        in_specs=[_HBM] * (2 * n) + [_SEM, _SEM, _ANY], out_specs=tuple([_HBM] * (2 * n)),
        input_output_aliases={i: i for i in range(2 * n)},
        compiler_params=_cp(has_side_effects=_EFFECT),
    )(*parts, *lands, send, recv, after)
    return res[n:]


def _final_half(g4, r4, got3, jc_arr, name):
    _, rows, cols = r4.shape
    rb = _row_block(rows)
    nb = rows // rb

    def body(jc_ref, g_ref, r_ref, p_ref, o_ref):
        own = g_ref[0] + r_ref[0]
        o_ref[...] = ((own + p_ref[0].astype(F32)) + p_ref[1].astype(F32)) + p_ref[2].astype(F32)

    return pl.pallas_call(
        body, name=name,
        grid_spec=pltpu.PrefetchScalarGridSpec(
            num_scalar_prefetch=1, grid=(nb,),
            in_specs=[pl.BlockSpec((1, rb, cols), lambda i, jc: (jc[0], jc[1] * nb + i, 0)),
                      pl.BlockSpec((1, rb, cols), lambda i, jc: (jc[0], i, 0)),
                      pl.BlockSpec((3, rb, cols), lambda i, jc: (0, i, 0))],
            out_specs=pl.BlockSpec((rb, cols), lambda i, jc: (i, 0))),
        out_shape=jax.ShapeDtypeStruct((rows, cols), F32),
        compiler_params=_cp(dimension_semantics=("parallel",)),
    )(jc_arr, g4, r4, got3)


def _adamw_math(w, g, m, v):
    m2 = ADAM_B1 * m + (1.0 - ADAM_B1) * g
    v2 = ADAM_B2 * v + (1.0 - ADAM_B2) * (g * g)
    m_hat = m2 / (1.0 - ADAM_B1 ** ADAM_STEP)
    v_hat = v2 / (1.0 - ADAM_B2 ** ADAM_STEP)
    delta = -ADAM_LR * (m_hat / (jnp.sqrt(v_hat) + ADAM_EPS) + ADAM_WD * w)
    return delta, m2, v2


def _adamw_shard(mine, other, w, m, v, c_arr, name):
    rows, cols = w.shape
    half = rows // 2
    rb = _row_block(half)
    nb = half // rb

    def body(c_ref, a_ref, b_ref, w_ref, m_ref, v_ref, g_ref, d_ref, m2_ref, v2_ref):
        g = jnp.where(pl.program_id(0) == c_ref[0], a_ref[...], b_ref[...])
        delta, m2, v2 = _adamw_math(w_ref[...], g, m_ref[...], v_ref[...])
        g_ref[...] = g
        d_ref[...] = delta
        m2_ref[...] = m2
        v2_ref[...] = v2

    hb = pl.BlockSpec((rb, cols), lambda h, i, c: (i, 0))
    fb = pl.BlockSpec((rb, cols), lambda h, i, c: (h * nb + i, 0))
    shp = jax.ShapeDtypeStruct((rows, cols), F32)
    return pl.pallas_call(
        body, name=name,
        grid_spec=pltpu.PrefetchScalarGridSpec(num_scalar_prefetch=1, grid=(2, nb), in_specs=[hb, hb, fb, fb, fb],
                                               out_specs=(fb, fb, fb, fb)),
        out_shape=(shp, shp, shp, shp),
        compiler_params=_cp(dimension_semantics=("parallel", "parallel")),
    )(c_arr, mine, other, w, m, v)


def _small_allreduce(parts):
    n = len(parts)

    def body(*refs):
        p_refs, accs = refs[:n], refs[n:2 * n]
        bufs = refs[2 * n:3 * n]
        send, recv = refs[3 * n:]
        x, y, c = lax.axis_index("x"), lax.axis_index("y"), lax.axis_index("c")
        partners = [(x, y, 1 - c), (1 - x, y, c), (x, 1 - y, c)]
        other_way = [partners[0], partners[2], partners[1]]
        for i in range(n):
            accs[i][...] = p_refs[i][...]
        for s in range(3):
            cps = []
            for i in range(n):
                rows = parts[i].shape[0]
                pieces = [(slice(None), partners[s])] if rows < 2 * BLK else [
                    (pl.ds(0, rows // 2), partners[s]), (pl.ds(rows // 2, rows // 2), other_way[s])]
                for h, (sl, partner) in enumerate(pieces):
                    cp = pltpu.make_async_remote_copy(
                        src_ref=accs[i].at[sl], dst_ref=bufs[i].at[s, sl], send_sem=send.at[6 * i + 2 * s + h],
                        recv_sem=recv.at[6 * i + 2 * s + h], device_id=partner, device_id_type=MESH)
                    cp.start()
                    cps.append(cp)
            for cp in cps:
                cp.wait()
            for i in range(n):
                accs[i][...] = accs[i][...] + bufs[i][s]

    vmem = pl.BlockSpec(memory_space=pltpu.VMEM)
    return pl.pallas_call(
        body, name="small_allreduce", out_shape=tuple(jax.ShapeDtypeStruct(p.shape, F32) for p in parts),
        in_specs=[vmem] * n, out_specs=tuple([vmem] * n),
        scratch_shapes=[pltpu.VMEM((3,) + p.shape, F32) for p in parts] + [
            pltpu.SemaphoreType.DMA((6 * n,)), pltpu.SemaphoreType.DMA((6 * n,))],
        compiler_params=_cp(),
    )(*parts)


def _small_adamw(reduced, params):
    n = len(reduced)
    n_w = len(params)

    def body(*refs):
        r_refs = refs[:n]
        wmv = refs[n:n + 3 * n_w]
        outs = refs[n + 3 * n_w:]
        outs[0][...] = r_refs[n - 1][0:1, 0:1]
        grads = [r_refs[0][...], r_refs[1][0:1, 0:N_Q_HEADS]] + [r_refs[i][...] for i in range(2, n_w)]
        for i in range(n_w):
            w_ref, m_ref, v_ref = wmv[3 * i:3 * i + 3]
            g_ref, d_ref, m2_ref, v2_ref = outs[1 + 4 * i:5 + 4 * i]
            delta, m2, v2 = _adamw_math(w_ref[...], grads[i], m_ref[...], v_ref[...])
            g_ref[...] = grads[i]
            d_ref[...] = delta
            m2_ref[...] = m2
            v2_ref[...] = v2

    flat = [a for p in params for a in p]
    vmem = pl.BlockSpec(memory_space=pltpu.VMEM)
    out_shape = [jax.ShapeDtypeStruct((1, 1), F32)]
    for p in params:
        out_shape += [jax.ShapeDtypeStruct(p[0].shape, F32)] * 4
    res = pl.pallas_call(
        body, name="small_adamw", out_shape=tuple(out_shape),
        in_specs=[vmem] * (n + len(flat)), out_specs=tuple([vmem] * len(out_shape)),
        compiler_params=_cp(),
    )(*reduced, *flat)
    return res[0], [res[1 + 4 * i:5 + 4 * i] for i in range(n_w)]


def kernel(x, attn_norm_g, w_in, attn_sinks, w_pool, pool_scale, w_out, mlp_norm_g, w_up, w_down, final_norm_g, loss_target, m_attn_norm_g, m_w_in, m_attn_sinks, m_w_pool, m_pool_scale, m_w_out, m_mlp_norm_g, m_w_up, m_w_down, m_final_norm_g, v_attn_norm_g, v_w_in, v_attn_sinks, v_w_pool, v_pool_scale, v_w_out, v_mlp_norm_g, v_w_up, v_w_down, v_final_norm_g):
    nseq, seq, d = x.shape
    c_idx = lax.axis_index("c").astype(jnp.int32)
    j_idx = (2 * lax.axis_index("x") + lax.axis_index("y")).astype(jnp.int32)
    c_arr = jnp.reshape(c_idx, (1,))
    jc_arr = jnp.stack([j_idx, c_idx])

    big_w = (w_in[0].T, w_out[0], w_up[0], w_down[0])
    big_m = (m_w_in[0].T, m_w_out[0], m_w_up[0], m_w_down[0])
    big_v = (v_w_in[0].T, v_w_out[0], v_w_up[0], v_w_down[0])
    (w_in_lands, w_in_send, w_in_recv), rest = _gather_start(big_w[:1], big_w[1:])
    (w_in4,) = _forward_now(_gather_wait(w_in_lands, w_in_send, w_in_recv, "gather_wait_w_in"), "forward_w_in")
    w_in_full = w_in4.reshape(IN_W, D_MODEL)
    class Comm(_NoComm):
        def attn_done(self, attn):
            arrived = _gather_wait(*rest, "gather_wait_rest", (attn,))
            self.lands, self.send, self.recv, token = _forward_start(arrived)
            return (token,)

        def rest_of_weights(self, pool):
            w_out4, w_up4, w_down4 = _forward_wait(self.lands, self.send, self.recv, pool)
            return w_out4.reshape(D_MODEL, D_MODEL), w_up4, w_down4

        def mlp_grads_ready(self, gw_up4, gw_down4):
            self.mlp = _sibling_start((gw_up4, gw_down4), "mlp_grads_to_sibling")
            return (self.mlp[4],)

        def outproj_done(self, gw_out):
            grads, lands, send, recv, _ = self.mlp
            sib_out = _sibling_start((gw_out.reshape(N_CHIPS, OUT_SHARD, D_MODEL),), "w_out_grad_to_sibling")
            mlp_grads, mlp_from_sib = _sibling_wait(grads, lands, send, recv, sib_out[4], "mlp_grads_to_sibling")
            partials = [_chip_partial(g, r, c_arr, "chip_partial_" + nm)
                        for g, r, nm in zip(mlp_grads, mlp_from_sib, names[2:])]
            out_grad, out_from_sib = _sibling_wait(*sib_out[:4], partials[1], "w_out_grad_to_sibling")
            partials = [_chip_partial(out_grad[0], out_from_sib[0], c_arr, "chip_partial_w_out")] + partials
            self.grads = tuple(out_grad) + tuple(mlp_grads)
            self.from_sib = tuple(out_from_sib) + tuple(mlp_from_sib)
            self.parts, self.part_lands, self.part_send, self.part_recv, token = _send_start(partials, "send_start")
            return (token,)

    names = ("w_in", "w_out", "w_up", "w_down")
    comm = Comm()
    gx, big_g, small_g = _local_step(
        x.reshape(nseq * seq, d), loss_target.reshape(nseq * seq, d), w_in_full, comm,
        attn_norm_g, attn_sinks.reshape(N_Q_HEADS), w_pool[0], pool_scale, mlp_norm_g, final_norm_g.reshape(1, d), seq)
    big_g = tuple(big_g[:1]) + comm.grads

    adamw = lambda i, a, b: _adamw_shard(a, b, big_w[i], big_m[i], big_v[i], c_arr, "adamw_" + names[i])
    sib_in = _sibling_start(big_g[:1], "w_in_grad_to_sibling")
    received = _send_wait(comm.parts, comm.part_lands, comm.part_send, comm.part_recv, sib_in[4], "send_wait")
    mine = [_final_half(g, r, p, jc_arr, "final_half_" + nm)
            for g, r, p, nm in zip(big_g[1:], comm.from_sib, received, names[1:])]
    halves = _sibling_start(mine, "halves_to_sibling", False)
    (g_in,), (r_in,) = _sibling_wait(*sib_in[:4], halves[4], "w_in_grad_to_sibling")
    send_in = _send_start([_chip_partial(g_in, r_in, c_arr, "chip_partial_w_in")], "send_start_w_in")
    mine, other = _sibling_wait(*halves[:4], send_in[4], "halves_to_sibling", False)
    big_out = [None, adamw(1, mine[0], other[0]), adamw(2, mine[1], other[1]), None]
    (p_in,) = _send_wait(*send_in[:4], big_out[2][0], "send_wait_w_in")
    mine_in = _final_half(g_in, r_in, p_in, jc_arr, "final_half_w_in")
    (other_in,) = _sibling_exchange([mine_in], False, "w_in_half_to_sibling")
    big_out[3] = adamw(3, mine[2], other[2])
    big_out[0] = [t.T for t in adamw(0, mine_in, other_in)]

    wp_flat = lambda a: a.reshape(N_POOL * POOL_G, POOL_G)
    small_params = [
        (attn_norm_g, m_attn_norm_g, v_attn_norm_g),
        (attn_sinks, m_attn_sinks, v_attn_sinks),
        (wp_flat(w_pool), wp_flat(m_w_pool), wp_flat(v_w_pool)),
        (pool_scale, m_pool_scale, v_pool_scale),
        (mlp_norm_g, m_mlp_norm_g, v_mlp_norm_g),
        (final_norm_g.reshape(1, d), m_final_norm_g.reshape(1, d), v_final_norm_g.reshape(1, d)),
    ]
    loss, small_out = _small_adamw(_small_allreduce(small_g), small_params)

    def shaped(i, arr):
        return {1: w_in, 5: w_out, 7: w_up, 8: w_down, 0: attn_norm_g, 2: attn_sinks, 3: w_pool, 4: pool_scale,
                6: mlp_norm_g, 9: final_norm_g}[i].shape

    order = [small_out[0], big_out[0], small_out[1], small_out[2], small_out[3], big_out[1], small_out[4], big_out[2],
             big_out[3], small_out[5]]
    outs = [loss.reshape(()), gx.reshape(nseq, seq, d)]
    for kind in range(4):
        outs += [order[i][kind].reshape(shaped(i, None)) for i in range(10)]
    return tuple(outs)
```

```python
import jax
import jax.numpy as jnp
import numpy as np
from jax import lax
from jax.experimental import pallas as pl
from jax.experimental.pallas import tpu as pltpu

F32 = jnp.float32
BF16 = jnp.bfloat16
MXU_DTYPE = jnp.bfloat16

D_MODEL = 1024
HEAD_DIM = 64
N_Q_HEADS = 8
N_KV_HEADS = 2
Q_PER_KV = N_Q_HEADS // N_KV_HEADS
ATTN_W = N_Q_HEADS * HEAD_DIM
KV_W = N_KV_HEADS * HEAD_DIM
BLK = 128
POOL_WINDOWS = (2, 4, 8, 16)
N_POOL = len(POOL_WINDOWS)
POOL_W = D_MODEL - ATTN_W
POOL_G = POOL_W // N_POOL
IN_W = ATTN_W + 2 * KV_W + POOL_W
D_FF = 4 * D_MODEL
EPS = 1e-6
ROPE_THETA = 10000.0
N_CHIPS = 4
IN_SHARD = IN_W // N_CHIPS
OUT_SHARD = D_MODEL // N_CHIPS
FF_SHARD = D_FF // N_CHIPS
LANES = 128

ADAM_LR = 0.001
ADAM_B1 = 0.9
ADAM_B2 = 0.999
ADAM_EPS = 1e-08
ADAM_WD = 0.01
ADAM_STEP = 10

VMEM_LIMIT = 56 * 1024 * 1024
MESH = pl.DeviceIdType.MESH


def _cp(**kw):
    return pltpu.CompilerParams(vmem_limit_bytes=VMEM_LIMIT, **kw)


def _mm(a, b):
    return jnp.dot(a.astype(MXU_DTYPE), b.astype(MXU_DTYPE), preferred_element_type=F32)


def _mm_nt(a, b):
    return lax.dot_general(a.astype(MXU_DTYPE), b.astype(MXU_DTYPE), (((1,), (1,)), ((), ())),
                           preferred_element_type=F32)


def _mm_tn(a, b):
    return lax.dot_general(a.astype(MXU_DTYPE), b.astype(MXU_DTYPE), (((0,), (0,)), ((), ())),
                           preferred_element_type=F32)


def _resident(shape):
    nd = len(shape)
    return pl.BlockSpec(shape, lambda *_: (0,) * nd, pipeline_mode=pl.Buffered(1))


def _const(shape):
    nd = len(shape)
    return pl.BlockSpec(shape, lambda *_: (0,) * nd)


def _rope_tables(seq):
    half = HEAD_DIM // 2
    inv_freq = (ROPE_THETA ** (-np.arange(half, dtype=np.float32) / half)).astype(np.float32)
    ang = np.arange(seq, dtype=np.float32)[:, None] * inv_freq[None, :]
    cos, sin = np.cos(ang).astype(np.float32), np.sin(ang).astype(np.float32)
    cos_t = np.concatenate([cos, cos, cos, cos], axis=1)
    sin_t = np.concatenate([-sin, sin, -sin, sin], axis=1)
    return jnp.asarray(cos_t), jnp.asarray(sin_t)


def _swap_halves(xc):
    lane = lax.broadcasted_iota(jnp.int32, xc.shape, 1)
    return jnp.where((lane & 63) < 32, pltpu.roll(xc, 96, 1), pltpu.roll(xc, 32, 1))


_HBM = pl.BlockSpec(memory_space=pltpu.HBM)
_SEM = pl.BlockSpec(memory_space=pltpu.SEMAPHORE)
_ANY = pl.BlockSpec(memory_space=pl.ANY)
_EFFECT = pltpu.SideEffectType.DATAFLOW_SIDE_EFFECTING
TOKEN = jax.ShapeDtypeStruct((8, LANES), F32)


def _other_chips(x, y):
    return [(1 - x, y), (x, 1 - y), (1 - x, 1 - y)]


def _gather_start(first, rest):
    shards = tuple(first) + tuple(rest)
    nw, nf = len(shards), len(first)

    def body(*refs):
        srcs, lands = refs[:nw], refs[nw:2 * nw]
        sems = refs[2 * nw:2 * nw + 4]
        stages = refs[2 * nw + 4:3 * nw + 4]
        lsem = refs[3 * nw + 4]
        x, y, c = lax.axis_index("x"), lax.axis_index("y"), lax.axis_index("c")
        j = 2 * x + y
        for k in range(nw):
            send, recv = sems[0:2] if k < nf else sems[2:4]
            slot = k if k < nf else k - nf
            stages[k][...] = srcs[k][...].astype(BF16)
            local = pltpu.make_async_copy(stages[k], lands[k].at[j], lsem.at[k])
            local.start()
            local.wait()
            rows = shards[k].shape[0] // 2
            mine = lands[k].at[j, pl.ds(c * rows, rows)]
            for n, chip in enumerate(_other_chips(x, y)):
                pltpu.make_async_remote_copy(
                    src_ref=mine, dst_ref=mine, send_sem=send.at[slot * 3 + n], recv_sem=recv.at[slot * 3 + n],
                    device_id=(chip[0], chip[1], c), device_id_type=MESH).start()

    vmem = pl.BlockSpec(memory_space=pltpu.VMEM)
    nr = nw - nf
    res = pl.pallas_call(
        body, name="gather_start",
        out_shape=tuple(pltpu.HBM((N_CHIPS,) + s.shape, BF16) for s in shards) + (
            pltpu.SemaphoreType.DMA((3 * nf,)), pltpu.SemaphoreType.DMA((3 * nf,)),
            pltpu.SemaphoreType.DMA((3 * nr,)), pltpu.SemaphoreType.DMA((3 * nr,))),
        in_specs=[vmem] * nw, out_specs=tuple([_HBM] * nw) + (_SEM, _SEM, _SEM, _SEM),
        scratch_shapes=[pltpu.VMEM(s.shape, BF16) for s in shards] + [pltpu.SemaphoreType.DMA((nw,))],
        compiler_params=_cp(has_side_effects=_EFFECT),
    )(*shards)
    return (res[:nf], res[nw], res[nw + 1]), (res[nf:nw], res[nw + 2], res[nw + 3])


def _gather_wait(lands, send, recv, name, after=()):
    nw = len(lands)
    tokens, token_specs = _after(after)

    def body(*refs):
        ins = refs[:nw]
        send_ref, recv_ref = refs[nw], refs[nw + 1]
        x, y, c = lax.axis_index("x"), lax.axis_index("y"), lax.axis_index("c")
        j = 2 * x + y
        for k in range(nw):
            rows = lands[k].shape[1] // 2
            mine = ins[k].at[j, pl.ds(c * rows, rows)]
            for n, chip in enumerate(_other_chips(x, y)):
                got = ins[k].at[2 * chip[0] + chip[1], pl.ds(c * rows, rows)]
                cp = pltpu.make_async_remote_copy(
                    src_ref=mine, dst_ref=got, send_sem=send_ref.at[k * 3 + n], recv_sem=recv_ref.at[k * 3 + n],
                    device_id=(chip[0], chip[1], c), device_id_type=MESH)
                cp.wait_send()
                cp.wait_recv()

    return pl.pallas_call(
        body, name=name, out_shape=tuple(pltpu.HBM(a.shape, a.dtype) for a in lands),
        in_specs=[_HBM] * nw + [_SEM, _SEM] + token_specs, out_specs=tuple([_HBM] * nw),
        input_output_aliases={k: k for k in range(nw)},
        compiler_params=_cp(has_side_effects=_EFFECT),
    )(*lands, send, recv, *tokens)


def _forward_now(lands, name):
    nw = len(lands)
    rows_of = [a.shape[1] // 2 for a in lands]

    def body(*refs):
        mine, theirs = _forward_copies(refs[nw:2 * nw], nw, rows_of, refs[2 * nw], refs[2 * nw + 1])
        for cp in mine:
            cp.start()
        for cp in theirs:
            cp.wait_recv()
        for cp in mine:
            cp.wait_send()

    return pl.pallas_call(
        body, name=name, out_shape=tuple(jax.ShapeDtypeStruct(a.shape, a.dtype) for a in lands),
        in_specs=[_HBM] * nw, out_specs=tuple([_HBM] * nw), input_output_aliases={k: k for k in range(nw)},
        scratch_shapes=[pltpu.SemaphoreType.DMA((3 * nw,)), pltpu.SemaphoreType.DMA((3 * nw,))],
        compiler_params=_cp(),
    )(*lands)


def _forward_copies(refs, nw, rows_of, send, recv):
    x, y, c = lax.axis_index("x"), lax.axis_index("y"), lax.axis_index("c")
    out = []
    for k in range(nw):
        rows = rows_of[k]
        for n, chip in enumerate(_other_chips(x, y)):
            got = refs[k].at[2 * chip[0] + chip[1], pl.ds(c * rows, rows)]
            theirs = refs[k].at[2 * chip[0] + chip[1], pl.ds((1 - c) * rows, rows)]
            out.append(pltpu.make_async_remote_copy(
                src_ref=got, dst_ref=got, send_sem=send.at[k * 3 + n], recv_sem=recv.at[k * 3 + n],
                device_id=(x, y, 1 - c), device_id_type=MESH))
            out.append(pltpu.make_async_remote_copy(
                src_ref=theirs, dst_ref=theirs, send_sem=send.at[k * 3 + n], recv_sem=recv.at[k * 3 + n],
                device_id=(x, y, 1 - c), device_id_type=MESH))
    return out[0::2], out[1::2]


def _forward_start(lands):
    nw = len(lands)
    rows_of = [a.shape[1] // 2 for a in lands]

    def body(*refs):
        send, recv, token = refs[2 * nw:2 * nw + 3]
        mine, _ = _forward_copies(refs[:nw], nw, rows_of, send, recv)
        for cp in mine:
            cp.start()
        token[...] = jnp.zeros_like(token)

    res = pl.pallas_call(
        body, name="forward_start",
        out_shape=tuple(pltpu.HBM(a.shape, a.dtype) for a in lands) + (
            pltpu.SemaphoreType.DMA((3 * nw,)), pltpu.SemaphoreType.DMA((3 * nw,)), TOKEN),
        in_specs=[_HBM] * nw, out_specs=tuple([_HBM] * nw) + (_SEM, _SEM, pl.BlockSpec(memory_space=pltpu.VMEM)),
        input_output_aliases={k: k for k in range(nw)},
        compiler_params=_cp(has_side_effects=_EFFECT),
    )(*lands)
    return res[:nw], res[nw], res[nw + 1], res[nw + 2]


def _forward_wait(lands, send, recv, after):
    nw = len(lands)
    rows_of = [a.shape[1] // 2 for a in lands]

    def body(*refs):
        mine, theirs = _forward_copies(refs[:nw], nw, rows_of, refs[nw], refs[nw + 1])
        for cp in mine:
            cp.wait_send()
        for cp in theirs:
            cp.wait_recv()

    return pl.pallas_call(
        body, name="forward_wait", out_shape=tuple(pltpu.HBM(a.shape, a.dtype) for a in lands),
        in_specs=[_HBM] * nw + [_SEM, _SEM, _ANY], out_specs=tuple([_HBM] * nw),
        input_output_aliases={k: k for k in range(nw)},
        compiler_params=_cp(has_side_effects=_EFFECT),
    )(*lands, send, recv, after)


def _after(tokens):
    tokens = [t for t in tokens if t is not None]
    return tokens, [_ANY] * len(tokens)


def _fwd_inproj(x, g1, w_in, cos_t, sin_t, tm, after=()):
    t_tok = x.shape[0]
    seq = cos_t.shape[0]
    per_seq = seq // tm
    tokens, token_specs = _after(after)

    def body(x_ref, g_ref, w_ref, cos_ref, sin_ref, *rest):
        q_ref, k_ref, v_ref, u_ref = rest[len(tokens):]
        xv = x_ref[...]
        r = lax.rsqrt(jnp.mean(xv * xv, axis=-1, keepdims=True) + EPS)
        h = (xv * r) * g_ref[...]
        proj = _mm_nt(h, w_ref[...])
        cos, sin = cos_ref[...], sin_ref[...]
        for cidx in range((ATTN_W + KV_W) // LANES):
            xc = proj[:, cidx * LANES:(cidx + 1) * LANES]
            rot = xc * cos + _swap_halves(xc) * sin
            if cidx < ATTN_W // LANES:
                q_ref[:, cidx * LANES:(cidx + 1) * LANES] = (rot * (HEAD_DIM ** -0.5)).astype(q_ref.dtype)
            else:
                k_ref[...] = rot.astype(k_ref.dtype)
        v_ref[...] = proj[:, ATTN_W + KV_W:ATTN_W + 2 * KV_W].astype(v_ref.dtype)
        u_ref[...] = proj[:, ATTN_W + 2 * KV_W:].astype(u_ref.dtype)

    row = lambda w: pl.BlockSpec((tm, w), lambda i: (i, 0))
    kv = row(KV_W)
    tab = pl.BlockSpec((tm, LANES), lambda i: (i % per_seq, 0))
    kv_shape = jax.ShapeDtypeStruct((t_tok, KV_W), MXU_DTYPE)
    return pl.pallas_call(
        body, name="fwd_inproj", grid=(t_tok // tm,),
        in_specs=[row(D_MODEL), _const((1, D_MODEL)), _resident((IN_W, D_MODEL)), tab, tab] + token_specs,
        out_specs=(row(ATTN_W), kv, kv, row(POOL_W)),
        out_shape=(jax.ShapeDtypeStruct((t_tok, ATTN_W), MXU_DTYPE), kv_shape, kv_shape,
                   jax.ShapeDtypeStruct((t_tok, POOL_W), MXU_DTYPE)),
        compiler_params=_cp(dimension_semantics=("parallel",)),
    )(x, g1, w_in, cos_t, sin_t, *tokens)


MASKED = -1e30


def _attn_bias():
    b = lax.broadcasted_iota(jnp.int32, (2 * BLK, BLK), 0)
    a = lax.broadcasted_iota(jnp.int32, (2 * BLK, BLK), 1)
    own = (b >= BLK) & (b - BLK <= a)
    prev = (b < BLK) & (b > a)
    return jnp.stack([jnp.where(own, 0.0, MASKED), jnp.where(own | prev, 0.0, MASKED)]).astype(F32)


def _sink_rows(sinks):
    return jnp.repeat(sinks.reshape(N_KV_HEADS, Q_PER_KV), BLK, axis=1).reshape(N_KV_HEADS, 1, Q_PER_KV * BLK)


def _stack_heads(ref, r0, g):
    return jnp.concatenate(
        [ref[pl.ds(r0, BLK), (Q_PER_KV * g + h) * HEAD_DIM:(Q_PER_KV * g + h + 1) * HEAD_DIM] for h in range(Q_PER_KV)],
        axis=0)


def _kv_window(ref, p0, r0, g):
    sl = slice(g * HEAD_DIM, (g + 1) * HEAD_DIM)
    return jnp.concatenate([ref[pl.ds(p0, BLK), sl], ref[pl.ds(r0, BLK), sl]], axis=0)


def _scores_t(k_ref, q_ref, bias, p0, r0, g):
    kk = _kv_window(k_ref, p0, r0, g)
    qs = _stack_heads(q_ref, r0, g)
    st = _mm_nt(kk, qs) + jnp.concatenate([bias] * Q_PER_KV, axis=1)
    return st, kk, qs


def _head_rows(ref, r0, g):
    return jnp.concatenate([ref[pl.ds(Q_PER_KV * g + h, 1), pl.ds(r0, BLK)] for h in range(Q_PER_KV)], axis=1)


def _pairs_to_rows(xt):
    out = []
    for t in range(Q_PER_KV // 2):
        pair = jnp.concatenate([xt[:, (2 * t) * BLK:(2 * t + 1) * BLK], xt[:, (2 * t + 1) * BLK:(2 * t + 2) * BLK]], axis=0)
        out.append(pair.T)
    return out


def _shift_rows(x, k, seq):
    row = lax.broadcasted_iota(jnp.int32, x.shape, 0)
    if k > 0:
        return jnp.where(row >= k, pltpu.roll(x, k, 0), 0.0)
    return jnp.where(row < seq + k, pltpu.roll(x, seq + k, 0), 0.0)


def _window_sum(x, w, seq, forward):
    s, k = x, 1
    while k < w:
        s = s + _shift_rows(s, -k if forward else k, seq)
        k *= 2
    return s


def _inv_count(seq, w):
    pos = lax.broadcasted_iota(jnp.int32, (seq, 1), 0)
    return 1.0 / jnp.minimum(pos + 1, w).astype(F32)


def _fwd_attn(q, k, v, sink_rows, bias, seq):
    t_tok = q.shape[0]
    nblk = seq // BLK

    def body(q_ref, k_ref, v_ref, sink_ref, bias_ref, o_ref, lse_ref):
        def blk(i, carry):
            r0 = pl.multiple_of(i * BLK, BLK)
            p0 = pl.multiple_of(jnp.maximum(i - 1, 0) * BLK, BLK)
            bias = bias_ref[jnp.minimum(i, 1)]
            for g in range(N_KV_HEADS):
                st, _, _ = _scores_t(k_ref, q_ref, bias, p0, r0, g)
                sink = sink_ref[g]
                m = jnp.maximum(jnp.max(st, axis=0, keepdims=True), sink)
                p = jnp.exp(st - m).astype(MXU_DTYPE)
                vv = _kv_window(v_ref, p0, r0, g)
                ot = _mm_tn(jnp.concatenate([vv, jnp.ones_like(vv)], axis=1), p)
                den = ot[HEAD_DIM:HEAD_DIM + 1] + jnp.exp(sink - m)
                lse = m + jnp.log(den)
                for h in range(Q_PER_KV):
                    lse_ref[pl.ds(Q_PER_KV * g + h, 1), pl.ds(r0, BLK)] = lse[:, h * BLK:(h + 1) * BLK]
                ot = ot[:HEAD_DIM] * (1.0 / den)
                for t, tile in enumerate(_pairs_to_rows(ot)):
                    c0 = (Q_PER_KV * g + 2 * t) * HEAD_DIM
                    o_ref[pl.ds(r0, BLK), c0:c0 + LANES] = tile.astype(o_ref.dtype)
            return carry

        lax.fori_loop(0, nblk, blk, 0, unroll=4)

    row = lambda w: pl.BlockSpec((seq, w), lambda i: (i, 0))
    kv = row(KV_W)
    return pl.pallas_call(
        body, name="fwd_attn", grid=(t_tok // seq,),
        in_specs=[row(ATTN_W), kv, kv, _const((N_KV_HEADS, 1, Q_PER_KV * BLK)), _const((2, 2 * BLK, BLK))],
        out_specs=(row(ATTN_W), pl.BlockSpec((N_Q_HEADS, seq), lambda i: (0, i))),
        out_shape=(jax.ShapeDtypeStruct((t_tok, ATTN_W), MXU_DTYPE), jax.ShapeDtypeStruct((N_Q_HEADS, t_tok), F32)),
        compiler_params=_cp(dimension_semantics=("parallel",)),
    )(q, k, v, sink_rows, bias)


def _fwd_pool(u, w_pool, pool_scale, seq, after=()):
    t_tok = u.shape[0]
    tokens, token_specs = _after(after)

    def body(u_ref, wp_ref, sc_ref, *rest):
        o_ref = rest[-1]
        for gi, w in enumerate(POOL_WINDOWS):
            sl = slice(gi * POOL_G, (gi + 1) * POOL_G)
            ug = u_ref[:, sl].astype(F32)
            d = _window_sum(ug, w, seq, False) * _inv_count(seq, w) - ug
            o_ref[:, sl] = (_mm(d, wp_ref[gi]) * sc_ref[:, sl]).astype(o_ref.dtype)

    row = pl.BlockSpec((seq, POOL_W), lambda i: (i, 0))
    return pl.pallas_call(
        body, name="fwd_pool", grid=(t_tok // seq,),
        in_specs=[row, _const((N_POOL, POOL_G, POOL_G)), _const((1, POOL_W))] + token_specs, out_specs=row,
        out_shape=jax.ShapeDtypeStruct((t_tok, POOL_W), MXU_DTYPE),
        compiler_params=_cp(dimension_semantics=("parallel",)),
    )(u, w_pool, pool_scale, *tokens)


def _rms_bwd(dy_g, xn, r):
    return r * (dy_g - xn * jnp.mean(dy_g * xn, axis=-1, keepdims=True))


def _mlp_fwd_bwd(x, attn, pool, target, w_out, w_up4, w_down4, g2, gf, tm):
    t_tok = x.shape[0]

    def body(x_ref, attn_ref, pool_ref, tgt_ref, wo_ref, wu_ref, wd_ref, g2_ref, gf_ref,
             h2_ref, a_ref, da_ref, dx2b_ref, dx1_ref, dx1b_ref, loss_ref, dgf_ref, dg2_ref):
        @pl.when(pl.program_id(0) == 0)
        def _():
            loss_ref[...] = jnp.zeros_like(loss_ref)
            dgf_ref[...] = jnp.zeros_like(dgf_ref)
            dg2_ref[...] = jnp.zeros_like(dg2_ref)

        x1 = x_ref[...] + (_mm(attn_ref[...], wo_ref[:ATTN_W]) + _mm(pool_ref[...], wo_ref[ATTN_W:]))
        r2 = lax.rsqrt(jnp.mean(x1 * x1, axis=-1, keepdims=True) + EPS)
        xn1 = x1 * r2
        g2 = g2_ref[...]
        h2 = (xn1 * g2).astype(MXU_DTYPE)
        h2_ref[...] = h2
        acc = jnp.zeros((tm, D_MODEL), F32)
        for j in range(N_CHIPS):
            a = _mm(h2, wu_ref[j])
            a_ref[:, j * FF_SHARD:(j + 1) * FF_SHARD] = a.astype(a_ref.dtype)
            acc = acc + _mm(jnp.square(jnp.maximum(a, 0.0)), wd_ref[j])
        x2 = x1 + acc
        r3 = lax.rsqrt(jnp.mean(x2 * x2, axis=-1, keepdims=True) + EPS)
        xn2 = x2 * r3
        gf_v = gf_ref[...]
        err = xn2 * gf_v - tgt_ref[...]
        part = jnp.sum(err * err) * (0.5 / D_MODEL)
        first = (lax.broadcasted_iota(jnp.int32, loss_ref.shape, 0) == 0) & (lax.broadcasted_iota(jnp.int32, loss_ref.shape, 1) == 0)
        loss_ref[...] += jnp.where(first, part, 0.0)
        dy = err * (1.0 / D_MODEL)
        dgf_ref[...] += jnp.sum(dy * xn2, axis=0, keepdims=True)
        dx2 = _rms_bwd(dy * gf_v, xn2, r3)
        dx2b = dx2.astype(MXU_DTYPE)
        dx2b_ref[...] = dx2b
        dh2 = jnp.zeros((tm, D_MODEL), F32)
        for j in range(N_CHIPS):
            sl = slice(j * FF_SHARD, (j + 1) * FF_SHARD)
            dhid = _mm_nt(dx2b, wd_ref[j])
            da = (dhid * (2.0 * jnp.maximum(a_ref[:, sl].astype(F32), 0.0))).astype(MXU_DTYPE)
            da_ref[:, sl] = da
            dh2 = dh2 + _mm_nt(da, wu_ref[j])
        dg2_ref[...] += jnp.sum(dh2 * xn1, axis=0, keepdims=True)
        dx1 = dx2 + _rms_bwd(dh2 * g2, xn1, r2)
        dx1_ref[...] = dx1
        dx1b_ref[...] = dx1.astype(dx1b_ref.dtype)

    row = lambda w: pl.BlockSpec((tm, w), lambda i: (i, 0))
    vec = jax.ShapeDtypeStruct((1, D_MODEL), F32)
    return pl.pallas_call(
        body, name="mlp_fwd_bwd", grid=(t_tok // tm,),
        in_specs=[row(D_MODEL), row(ATTN_W), row(POOL_W), row(D_MODEL), _resident((D_MODEL, D_MODEL)),
                  _resident((N_CHIPS, D_MODEL, FF_SHARD)), _resident((N_CHIPS, FF_SHARD, D_MODEL)),
                  _const((1, D_MODEL)), _const((1, D_MODEL))],
        out_specs=(row(D_MODEL), row(D_FF), row(D_FF), row(D_MODEL), row(D_MODEL), row(D_MODEL),
                   _const((8, LANES)), _const((1, D_MODEL)), _const((1, D_MODEL))),
        out_shape=(jax.ShapeDtypeStruct((t_tok, D_MODEL), MXU_DTYPE), jax.ShapeDtypeStruct((t_tok, D_FF), MXU_DTYPE),
                   jax.ShapeDtypeStruct((t_tok, D_FF), MXU_DTYPE), jax.ShapeDtypeStruct((t_tok, D_MODEL), MXU_DTYPE),
                   jax.ShapeDtypeStruct((t_tok, D_MODEL), F32), jax.ShapeDtypeStruct((t_tok, D_MODEL), MXU_DTYPE),
                   jax.ShapeDtypeStruct((8, LANES), F32), vec, vec),
        compiler_params=_cp(dimension_semantics=("arbitrary",)),
    )(x, attn, pool, target, w_out, w_up4, w_down4, g2, gf)


def _bwd_mlp_wgrads(h2, da, a, dx2b, tk):
    t_tok = h2.shape[0]

    def body(h2_ref, da_ref, a_ref, dx2b_ref, gup_ref, gdn_ref):
        @pl.when(pl.program_id(1) == 0)
        def _():
            gup_ref[...] = jnp.zeros_like(gup_ref)
            gdn_ref[...] = jnp.zeros_like(gdn_ref)

        gup_ref[0] += _mm_tn(h2_ref[...], da_ref[...])
        hid = jnp.square(jnp.maximum(a_ref[...].astype(F32), 0.0))
        gdn_ref[0] += _mm_tn(hid, dx2b_ref[...])

    tok = pl.BlockSpec((tk, D_MODEL), lambda j, t: (t, 0))
    ffb = pl.BlockSpec((tk, FF_SHARD), lambda j, t: (t, j))
    wblk = pl.BlockSpec((1, D_MODEL, D_MODEL), lambda j, t: (j, 0, 0))
    return pl.pallas_call(
        body, name="bwd_mlp_wgrads", grid=(N_CHIPS, t_tok // tk),
        in_specs=[tok, ffb, ffb, tok], out_specs=(wblk, wblk),
        out_shape=(jax.ShapeDtypeStruct((N_CHIPS, D_MODEL, FF_SHARD), F32), jax.ShapeDtypeStruct((N_CHIPS, FF_SHARD, D_MODEL), F32)),
        compiler_params=_cp(dimension_semantics=("parallel", "arbitrary")),
    )(h2, da, a, dx2b)


def _head_selector():
    ch = lax.broadcasted_iota(jnp.int32, (ATTN_W, LANES), 0)
    col = lax.broadcasted_iota(jnp.int32, (ATTN_W, LANES), 1)
    return (ch // HEAD_DIM == col).astype(MXU_DTYPE)


def _bwd_outproj(dx1b, attn, pool, w_out, head_sel, tm, after=()):
    t_tok = dx1b.shape[0]
    tokens, token_specs = _after(after)

    def body(dx_ref, attn_ref, pool_ref, wo_ref, sel_ref, *rest):
        dattn_ref, dpool_ref, delta_ref, gwo_ref = rest[len(tokens):]

        @pl.when(pl.program_id(0) == 0)
        def _():
            gwo_ref[...] = jnp.zeros_like(gwo_ref)

        dx = dx_ref[...]
        attn = attn_ref[...]
        dattn = _mm_nt(dx, wo_ref[:ATTN_W])
        dattn_ref[...] = dattn.astype(dattn_ref.dtype)
        dpool_ref[...] = _mm_nt(dx, wo_ref[ATTN_W:]).astype(dpool_ref.dtype)
        prod = dattn * attn.astype(F32)
        hi = prod.astype(MXU_DTYPE)
        lo = prod - hi.astype(F32)
        delta = _mm(hi, sel_ref[...]) + _mm(lo, sel_ref[...])
        delta_ref[...] = delta.T[:N_Q_HEADS]
        gwo_ref[:ATTN_W] += _mm_tn(attn, dx)
        gwo_ref[ATTN_W:] += _mm_tn(pool_ref[...], dx)

    row = lambda w: pl.BlockSpec((tm, w), lambda i: (i, 0))
    return pl.pallas_call(
        body, name="bwd_outproj", grid=(t_tok // tm,),
        in_specs=[row(D_MODEL), row(ATTN_W), row(POOL_W), _resident((D_MODEL, D_MODEL)), _const((ATTN_W, LANES))] + token_specs,
        out_specs=(row(ATTN_W), row(POOL_W), pl.BlockSpec((N_Q_HEADS, tm), lambda i: (0, i)), _const((D_MODEL, D_MODEL))),
        out_shape=(jax.ShapeDtypeStruct((t_tok, ATTN_W), MXU_DTYPE), jax.ShapeDtypeStruct((t_tok, POOL_W), MXU_DTYPE),
                   jax.ShapeDtypeStruct((N_Q_HEADS, t_tok), F32), jax.ShapeDtypeStruct((D_MODEL, D_MODEL), F32)),
        compiler_params=_cp(dimension_semantics=("arbitrary",)),
    )(dx1b, attn, pool, w_out, head_sel, *tokens)


def _bwd_attn(q, k, v, dattn, lse, delta, sink_rows, bias, cos_t, sin_t, seq, after=()):
    t_tok = q.shape[0]
    nblk = seq // BLK
    unroll = max(f for f in (4, 2, 1) if nblk % f == 0)
    qkv_w = ATTN_W + 2 * KV_W
    tokens, token_specs = _after(after)

    def unrope(d, cos, sin):
        return d * cos - _swap_halves(d) * sin

    def body(q_ref, k_ref, v_ref, do_ref, lse_ref, delta_ref, sink_ref, bias_ref, cos_ref, sin_ref, *rest):
        dqkv_ref, dsink_ref, dk_acc, dv_acc = rest[len(tokens):]

        @pl.when(pl.program_id(0) == 0)
        def _():
            dsink_ref[...] = jnp.zeros_like(dsink_ref)

        dk_acc[...] = jnp.zeros_like(dk_acc)
        dv_acc[...] = jnp.zeros_like(dv_acc)

        def blk(i, dsink):
            r0 = pl.multiple_of(i * BLK, BLK)
            p0 = pl.multiple_of(jnp.maximum(i - 1, 0) * BLK, BLK)
            bias = bias_ref[jnp.minimum(i, 1)]
            cos, sin = cos_ref[pl.ds(r0, BLK), :], sin_ref[pl.ds(r0, BLK), :]
            new = []
            for g in range(N_KV_HEADS):
                st, kk, qs = _scores_t(k_ref, q_ref, bias, p0, r0, g)
                lse_g, delta_g = _head_rows(lse_ref, r0, g), _head_rows(delta_ref, r0, g)
                pn = jnp.exp(st - lse_g)
                dos = _stack_heads(do_ref, r0, g)
                dst = pn * (_mm_nt(_kv_window(v_ref, p0, r0, g), dos) - delta_g)
                new.append(dsink[g] - jnp.exp(sink_ref[g] - lse_g) * delta_g)
                dqt = _mm_tn(kk, dst) * (HEAD_DIM ** -0.5)
                for t, tile in enumerate(_pairs_to_rows(dqt)):
                    c0 = (Q_PER_KV * g + 2 * t) * HEAD_DIM
                    dqkv_ref[pl.ds(r0, BLK), c0:c0 + LANES] = unrope(tile, cos, sin).astype(dqkv_ref.dtype)
                dkk = _mm(dst, qs)
                dvv = _mm(pn, dos)
                dk_acc[g, pl.ds(p0, BLK), :] += dkk[:BLK]
                dk_acc[g, pl.ds(r0, BLK), :] += dkk[BLK:]
                dv_acc[g, pl.ds(p0, BLK), :] += dvv[:BLK]
                dv_acc[g, pl.ds(r0, BLK), :] += dvv[BLK:]
            return tuple(new)

        zero = jnp.zeros((1, Q_PER_KV * BLK), F32)
        def several(ik, acc):
            for s in range(unroll):
                acc = blk(unroll * ik + s, acc)
            return acc

        dsink = lax.fori_loop(0, nblk // unroll, several, (zero,) * N_KV_HEADS)
        lane = lax.broadcasted_iota(jnp.int32, dsink_ref.shape, 1)
        row = lax.broadcasted_iota(jnp.int32, dsink_ref.shape, 0)
        tile = jnp.zeros(dsink_ref.shape, F32)
        for g in range(N_KV_HEADS):
            for h in range(Q_PER_KV):
                tot = jnp.sum(dsink[g][:, h * BLK:(h + 1) * BLK])
                tile = tile + jnp.where((row == 0) & (lane == Q_PER_KV * g + h), tot, 0.0)
        dsink_ref[...] += tile
        dk = jnp.concatenate([dk_acc[g] for g in range(N_KV_HEADS)], axis=1)
        dqkv_ref[:, ATTN_W:ATTN_W + KV_W] = unrope(dk, cos_ref[...], sin_ref[...]).astype(dqkv_ref.dtype)
        dqkv_ref[:, ATTN_W + KV_W:] = jnp.concatenate([dv_acc[g] for g in range(N_KV_HEADS)], axis=1).astype(dqkv_ref.dtype)

    row = lambda w: pl.BlockSpec((seq, w), lambda i: (i, 0))
    kv = row(KV_W)
    per_head = pl.BlockSpec((N_Q_HEADS, seq), lambda i: (0, i))
    return pl.pallas_call(
        body, name="bwd_attn", grid=(t_tok // seq,),
        in_specs=[row(ATTN_W), kv, kv, row(ATTN_W), per_head, per_head, _const((N_KV_HEADS, 1, Q_PER_KV * BLK)),
                  _const((2, 2 * BLK, BLK)), _resident((seq, LANES)), _resident((seq, LANES))] + token_specs,
        out_specs=(row(qkv_w), _const((8, LANES))),
        out_shape=(jax.ShapeDtypeStruct((t_tok, qkv_w), MXU_DTYPE), jax.ShapeDtypeStruct((8, LANES), F32)),
        scratch_shapes=[pltpu.VMEM((N_KV_HEADS, seq, HEAD_DIM), F32), pltpu.VMEM((N_KV_HEADS, seq, HEAD_DIM), F32)],
        compiler_params=_cp(dimension_semantics=("arbitrary",)),
    )(q, k, v, dattn, lse, delta, sink_rows, bias, cos_t, sin_t, *tokens)


def _bwd_pool(u, dpool, w_pool, pool_scale, seq):
    t_tok = u.shape[0]

    def body(u_ref, dp_ref, wp_ref, sc_ref, du_ref, dwp_ref, dsc_ref):
        @pl.when(pl.program_id(0) == 0)
        def _():
            dwp_ref[...] = jnp.zeros_like(dwp_ref)
            dsc_ref[...] = jnp.zeros_like(dsc_ref)

        for gi, w in enumerate(POOL_WINDOWS):
            sl = slice(gi * POOL_G, (gi + 1) * POOL_G)
            ug = u_ref[:, sl].astype(F32)
            inv = _inv_count(seq, w)
            d = (_window_sum(ug, w, seq, False) * inv - ug).astype(MXU_DTYPE)
            y = _mm(d, wp_ref[gi])
            dpool = dp_ref[:, sl].astype(F32)
            dsc_ref[:, sl] += jnp.sum(y * dpool, axis=0, keepdims=True)
            dy = (dpool * sc_ref[:, sl]).astype(MXU_DTYPE)
            dwp_ref[gi] += _mm_tn(d, dy)
            dd = _mm_nt(dy, wp_ref[gi])
            du_ref[:, sl] = (_window_sum(dd * inv, w, seq, True) - dd).astype(du_ref.dtype)

    row = pl.BlockSpec((seq, POOL_W), lambda i: (i, 0))
    return pl.pallas_call(
        body, name="bwd_pool", grid=(t_tok // seq,),
        in_specs=[row, row, _const((N_POOL, POOL_G, POOL_G)), _const((1, POOL_W))],
        out_specs=(row, _const((N_POOL, POOL_G, POOL_G)), _const((1, POOL_W))),
        out_shape=(jax.ShapeDtypeStruct((t_tok, POOL_W), MXU_DTYPE), jax.ShapeDtypeStruct((N_POOL, POOL_G, POOL_G), F32),
                   jax.ShapeDtypeStruct((1, POOL_W), F32)),
        compiler_params=_cp(dimension_semantics=("arbitrary",)),
    )(u, dpool, w_pool, pool_scale)


def _bwd_inproj(dqkv, du, x, dx1, w_in, g1, tm):
    t_tok = x.shape[0]
    nsteps = t_tok // tm

    qkv_w = ATTN_W + 2 * KV_W

    def body(dqkv_ref, du_ref, x_ref, dx1_ref, w_ref, g_ref, gx_ref, gw_ref, dg_ref):
        @pl.when(pl.program_id(0) == 0)
        def _():
            gw_ref[...] = jnp.zeros_like(gw_ref)
            dg_ref[...] = jnp.zeros_like(dg_ref)

        dqkv, du = dqkv_ref[...], du_ref[...]
        xv = x_ref[...]
        r = lax.rsqrt(jnp.mean(xv * xv, axis=-1, keepdims=True) + EPS)
        xn = xv * r
        g = g_ref[...]
        dh = _mm(dqkv, w_ref[:qkv_w]) + _mm(du, w_ref[qkv_w:])
        dg_ref[...] += jnp.sum(dh * xn, axis=0, keepdims=True)
        gx_ref[...] = dx1_ref[...] + _rms_bwd(dh * g, xn, r)
        h = (xn * g).astype(MXU_DTYPE)
        gw_ref[:qkv_w] += _mm_tn(dqkv, h)
        gw_ref[qkv_w:] += _mm_tn(du, h)

    row = lambda w: pl.BlockSpec((tm, w), lambda i: (i, 0))
    return pl.pallas_call(
        body, name="bwd_inproj", grid=(nsteps,),
        in_specs=[row(qkv_w), row(POOL_W), row(D_MODEL), row(D_MODEL), _resident((IN_W, D_MODEL)), _const((1, D_MODEL))],
        out_specs=(row(D_MODEL), _const((IN_W, D_MODEL)), _const((1, D_MODEL))),
        out_shape=(jax.ShapeDtypeStruct((t_tok, D_MODEL), F32), jax.ShapeDtypeStruct((IN_W, D_MODEL), F32),
                   jax.ShapeDtypeStruct((1, D_MODEL), F32)),
        compiler_params=_cp(dimension_semantics=("arbitrary",)),
    )(dqkv, du, x, dx1, w_in, g1)


class _NoComm:
    def start(self):
        return ()

    def attn_done(self, attn):
        return ()

    def rest_of_weights(self, pool):
        raise NotImplementedError

    def mlp_grads_ready(self, gw_up4, gw_down4):
        return ()

    def outproj_done(self, gw_out):
        return ()


def _local_step(x, target, w_in, comm, g1, sinks, w_pool, pool_scale, g2, gf, seq):
    tm = min(512, seq)
    tm_big = min(1024, seq)
    tm_mlp = min(256, seq)
    cos_t, sin_t = _rope_tables(seq)
    sink_rows, bias = _sink_rows(sinks), _attn_bias()
    q, k, v, u = _fwd_inproj(x, g1, w_in, cos_t, sin_t, tm_big, comm.start())
    attn, lse = _fwd_attn(q, k, v, sink_rows, bias, seq)
    pool = _fwd_pool(u, w_pool, pool_scale, seq, comm.attn_done(attn))
    w_out, w_up4, w_down4 = comm.rest_of_weights(pool)
    h2, a, da, dx2b, dx1, dx1b, loss, dgf, dg2 = _mlp_fwd_bwd(x, attn, pool, target, w_out, w_up4, w_down4, g2, gf, tm_mlp)
    gw_up4, gw_down4 = _bwd_mlp_wgrads(h2, da, a, dx2b, min(1024, x.shape[0]))
    dattn, dpool, delta, gw_out = _bwd_outproj(dx1b, attn, pool, w_out, _head_selector(), tm_big,
                                               comm.mlp_grads_ready(gw_up4, gw_down4))
    dqkv, dsinks = _bwd_attn(q, k, v, dattn, lse, delta, sink_rows, bias, cos_t, sin_t, seq, comm.outproj_done(gw_out))
    du, dwp, dsc = _bwd_pool(u, dpool, w_pool, pool_scale, seq)
    gx, gw_in_t, dg1 = _bwd_inproj(dqkv, du, x, dx1, w_in, g1, tm_big)
    big = (gw_in_t.reshape(N_CHIPS, IN_SHARD, D_MODEL), gw_out.reshape(N_CHIPS, OUT_SHARD, D_MODEL), gw_up4, gw_down4)
    small = (dg1, dsinks, dwp.reshape(N_POOL * POOL_G, POOL_G), dsc, dg2, dgf, loss)
    return gx, big, small


def _sibling_exchange(arrs, pick_half, name, after=()):
    n = len(arrs)
    tokens, token_specs = _after(after)

    def out_shape(a):
        if pick_half:
            return (a.shape[0], a.shape[1] // 2) + a.shape[2:]
        return a.shape

    def body(*refs):
        srcs, dsts = refs[:n], refs[n + len(tokens):2 * n + len(tokens)]
        send, recv = refs[2 * n + len(tokens):]
        x, y, c = lax.axis_index("x"), lax.axis_index("y"), lax.axis_index("c")
        cps = []
        for i in range(n):
            src = srcs[i]
            if pick_half:
                h = src.shape[1] // 2
                src = src.at[:, pl.ds((1 - c) * h, h)]
            cp = pltpu.make_async_remote_copy(src_ref=src, dst_ref=dsts[i], send_sem=send.at[i], recv_sem=recv.at[i],
                                              device_id=(x, y, 1 - c), device_id_type=MESH)
            cp.start()
            cps.append(cp)
        for cp in cps:
            cp.wait()

    hbm = pl.BlockSpec(memory_space=pltpu.HBM)
    return pl.pallas_call(
        body, name=name, out_shape=tuple(jax.ShapeDtypeStruct(out_shape(a), a.dtype) for a in arrs),
        in_specs=[hbm] * n + token_specs, out_specs=tuple([hbm] * n),
        scratch_shapes=[pltpu.SemaphoreType.DMA((n,)), pltpu.SemaphoreType.DMA((n,))],
        compiler_params=_cp(),
    )(*arrs, *tokens)


def _sibling_copies(srcs, lands, send, recv, pick_half):
    x, y, c = lax.axis_index("x"), lax.axis_index("y"), lax.axis_index("c")
    cps = []
    for i in range(len(srcs)):
        src = srcs[i]
        if pick_half:
            h = src.shape[1] // 2
            src = src.at[:, pl.ds((1 - c) * h, h)]
        cps.append(pltpu.make_async_remote_copy(
            src_ref=src, dst_ref=lands[i], send_sem=send.at[i], recv_sem=recv.at[i],
            device_id=(x, y, 1 - c), device_id_type=MESH))
    return cps


def _sibling_start(arrs, name, pick_half=True):
    n = len(arrs)

    def body(*refs):
        send, recv, token = refs[3 * n:3 * n + 3]
        for cp in _sibling_copies(refs[:n], refs[2 * n:3 * n], send, recv, pick_half):
            cp.start()
        token[...] = jnp.zeros_like(token)

    half = lambda a: ((a.shape[0], a.shape[1] // 2) + a.shape[2:]) if pick_half else a.shape
    res = pl.pallas_call(
        body, name=name + "_start",
        out_shape=tuple(pltpu.HBM(a.shape, a.dtype) for a in arrs) + tuple(pltpu.HBM(half(a), a.dtype) for a in arrs) + (
            pltpu.SemaphoreType.DMA((n,)), pltpu.SemaphoreType.DMA((n,)), TOKEN),
        in_specs=[_HBM] * n, out_specs=tuple([_HBM] * (2 * n)) + (_SEM, _SEM, pl.BlockSpec(memory_space=pltpu.VMEM)),
        input_output_aliases={i: i for i in range(n)},
        compiler_params=_cp(has_side_effects=_EFFECT),
    )(*[pltpu.with_memory_space_constraint(a, pltpu.HBM) for a in arrs])
    return res[:n], res[n:2 * n], res[2 * n], res[2 * n + 1], res[2 * n + 2]


def _sibling_wait(arrs, lands, send, recv, after, name, pick_half=True):
    n = len(arrs)

    def body(*refs):
        for cp in _sibling_copies(refs[:n], refs[n:2 * n], refs[2 * n], refs[2 * n + 1], pick_half):
            cp.wait_send()
            cp.wait_recv()

    res = pl.pallas_call(
        body, name=name + "_wait", out_shape=tuple(pltpu.HBM(a.shape, a.dtype) for a in list(arrs) + list(lands)),
        in_specs=[_HBM] * (2 * n) + [_SEM, _SEM, _ANY], out_specs=tuple([_HBM] * (2 * n)),
        input_output_aliases={i: i for i in range(2 * n)},
        compiler_params=_cp(has_side_effects=_EFFECT),
    )(*arrs, *lands, send, recv, after)
    return res[:n], res[n:]


def _row_block(rows):
    if rows <= 256:
        return rows
    for cand in (256, 128, 64, 32, 16, 8):
        if rows % cand == 0:
            return cand
    raise ValueError(rows)


def _chip_partial(g4, r4, c_arr, name):
    _, rows, cols = r4.shape
    rb = _row_block(rows)
    nb = rows // rb

    def body(c_ref, g_ref, r_ref, o_ref):
        o_ref[...] = (g_ref[...] + r_ref[...]).astype(o_ref.dtype)

    return pl.pallas_call(
        body, name=name,
        grid_spec=pltpu.PrefetchScalarGridSpec(
            num_scalar_prefetch=1, grid=(N_CHIPS, nb),
            in_specs=[pl.BlockSpec((1, rb, cols), lambda s, i, c: (s, c[0] * nb + i, 0)),
                      pl.BlockSpec((1, rb, cols), lambda s, i, c: (s, i, 0))],
            out_specs=pl.BlockSpec((1, rb, cols), lambda s, i, c: (s, i, 0))),
        out_shape=jax.ShapeDtypeStruct(r4.shape, BF16),
        compiler_params=_cp(dimension_semantics=("parallel", "parallel")),
    )(c_arr, g4, r4)


def _send_start(parts, name):
    n = len(parts)

    def body(*refs):
        srcs, lands = refs[:n], refs[2 * n:3 * n]
        send, recv, token = refs[3 * n:3 * n + 3]
        x, y, c = lax.axis_index("x"), lax.axis_index("y"), lax.axis_index("c")
        for i in range(n):
            for m, chip in enumerate(_other_chips(x, y)):
                pltpu.make_async_remote_copy(
                    src_ref=srcs[i].at[2 * chip[0] + chip[1]], dst_ref=lands[i].at[m],
                    send_sem=send.at[3 * i + m], recv_sem=recv.at[3 * i + m],
                    device_id=(chip[0], chip[1], c), device_id_type=MESH).start()
        token[...] = jnp.zeros_like(token)

    res = pl.pallas_call(
        body, name=name,
        out_shape=tuple(pltpu.HBM(p.shape, p.dtype) for p in parts) + tuple(pltpu.HBM((3,) + p.shape[1:], p.dtype) for p in parts) + (
            pltpu.SemaphoreType.DMA((3 * n,)), pltpu.SemaphoreType.DMA((3 * n,)), TOKEN),
        in_specs=[_HBM] * n, out_specs=tuple([_HBM] * (2 * n)) + (_SEM, _SEM, pl.BlockSpec(memory_space=pltpu.VMEM)),
        input_output_aliases={i: i for i in range(n)},
        compiler_params=_cp(has_side_effects=_EFFECT),
    )(*[pltpu.with_memory_space_constraint(p, pltpu.HBM) for p in parts])
    return res[:n], res[n:2 * n], res[2 * n], res[2 * n + 1], res[2 * n + 2]


def _send_wait(parts, lands, send, recv, after, name):
    n = len(parts)

    def body(*refs):
        srcs, ins = refs[:n], refs[n:2 * n]
        send_ref, recv_ref = refs[2 * n], refs[2 * n + 1]
        x, y, c = lax.axis_index("x"), lax.axis_index("y"), lax.axis_index("c")
        for i in range(n):
            for m, chip in enumerate(_other_chips(x, y)):
                cp = pltpu.make_async_remote_copy(
                    src_ref=srcs[i].at[2 * chip[0] + chip[1]], dst_ref=ins[i].at[m],
                    send_sem=send_ref.at[3 * i + m], recv_sem=recv_ref.at[3 * i + m],
                    device_id=(chip[0], chip[1], c), device_id_type=MESH)
                cp.wait_send()
                cp.wait_recv()

    res = pl.pallas_call(
        body, name=name, out_shape=tuple(pltpu.HBM(a.shape, a.dtype) for a in list(parts) + list(lands)),
        in_specs=[_HBM] * (2 * n) + [_SEM, _SEM, _ANY], out_specs=tuple([_HBM] * (2 * n)),
        input_output_aliases={i: i for i in range(2 * n)},
        compiler_params=_cp(has_side_effects=_EFFECT),
    )(*parts, *lands, send, recv, after)
    return res[n:]


def _final_half(g4, r4, got3, jc_arr, name):
    _, rows, cols = r4.shape
    rb = _row_block(rows)
    nb = rows // rb

    def body(jc_ref, g_ref, r_ref, p_ref, o_ref):
        own = g_ref[0] + r_ref[0]
        o_ref[...] = ((own + p_ref[0].astype(F32)) + p_ref[1].astype(F32)) + p_ref[2].astype(F32)

    return pl.pallas_call(
        body, name=name,
        grid_spec=pltpu.PrefetchScalarGridSpec(
            num_scalar_prefetch=1, grid=(nb,),
            in_specs=[pl.BlockSpec((1, rb, cols), lambda i, jc: (jc[0], jc[1] * nb + i, 0)),
                      pl.BlockSpec((1, rb, cols), lambda i, jc: (jc[0], i, 0)),
                      pl.BlockSpec((3, rb, cols), lambda i, jc: (0, i, 0))],
            out_specs=pl.BlockSpec((rb, cols), lambda i, jc: (i, 0))),
        out_shape=jax.ShapeDtypeStruct((rows, cols), F32),
        compiler_params=_cp(dimension_semantics=("parallel",)),
    )(jc_arr, g4, r4, got3)


def _adamw_math(w, g, m, v):
    m2 = ADAM_B1 * m + (1.0 - ADAM_B1) * g
    v2 = ADAM_B2 * v + (1.0 - ADAM_B2) * (g * g)
    m_hat = m2 / (1.0 - ADAM_B1 ** ADAM_STEP)
    v_hat = v2 / (1.0 - ADAM_B2 ** ADAM_STEP)
    delta = -ADAM_LR * (m_hat / (jnp.sqrt(v_hat) + ADAM_EPS) + ADAM_WD * w)
    return delta, m2, v2


def _adamw_shard(mine, other, w, m, v, c_arr, name):
    rows, cols = w.shape
    half = rows // 2
    rb = _row_block(half)
    nb = half // rb

    def body(c_ref, a_ref, b_ref, w_ref, m_ref, v_ref, g_ref, d_ref, m2_ref, v2_ref):
        g = jnp.where(pl.program_id(0) == c_ref[0], a_ref[...], b_ref[...])
        delta, m2, v2 = _adamw_math(w_ref[...], g, m_ref[...], v_ref[...])
        g_ref[...] = g
        d_ref[...] = delta
        m2_ref[...] = m2
        v2_ref[...] = v2

    hb = pl.BlockSpec((rb, cols), lambda h, i, c: (i, 0))
    fb = pl.BlockSpec((rb, cols), lambda h, i, c: (h * nb + i, 0))
    shp = jax.ShapeDtypeStruct((rows, cols), F32)
    return pl.pallas_call(
        body, name=name,
        grid_spec=pltpu.PrefetchScalarGridSpec(num_scalar_prefetch=1, grid=(2, nb), in_specs=[hb, hb, fb, fb, fb],
                                               out_specs=(fb, fb, fb, fb)),
        out_shape=(shp, shp, shp, shp),
        compiler_params=_cp(dimension_semantics=("parallel", "parallel")),
    )(c_arr, mine, other, w, m, v)


def _small_allreduce(parts):
    n = len(parts)

    def body(*refs):
        p_refs, accs = refs[:n], refs[n:2 * n]
        bufs = refs[2 * n:3 * n]
        send, recv = refs[3 * n:]
        x, y, c = lax.axis_index("x"), lax.axis_index("y"), lax.axis_index("c")
        partners = [(x, y, 1 - c), (1 - x, y, c), (x, 1 - y, c)]
        other_way = [partners[0], partners[2], partners[1]]
        for i in range(n):
            accs[i][...] = p_refs[i][...]
        for s in range(3):
            cps = []
            for i in range(n):
                rows = parts[i].shape[0]
                pieces = [(slice(None), partners[s])] if rows < 2 * BLK else [
                    (pl.ds(0, rows // 2), partners[s]), (pl.ds(rows // 2, rows // 2), other_way[s])]
                for h, (sl, partner) in enumerate(pieces):
                    cp = pltpu.make_async_remote_copy(
                        src_ref=accs[i].at[sl], dst_ref=bufs[i].at[s, sl], send_sem=send.at[6 * i + 2 * s + h],
                        recv_sem=recv.at[6 * i + 2 * s + h], device_id=partner, device_id_type=MESH)
                    cp.start()
                    cps.append(cp)
            for cp in cps:
                cp.wait()
            for i in range(n):
                accs[i][...] = accs[i][...] + bufs[i][s]

    vmem = pl.BlockSpec(memory_space=pltpu.VMEM)
    return pl.pallas_call(
        body, name="small_allreduce", out_shape=tuple(jax.ShapeDtypeStruct(p.shape, F32) for p in parts),
        in_specs=[vmem] * n, out_specs=tuple([vmem] * n),
        scratch_shapes=[pltpu.VMEM((3,) + p.shape, F32) for p in parts] + [
            pltpu.SemaphoreType.DMA((6 * n,)), pltpu.SemaphoreType.DMA((6 * n,))],
        compiler_params=_cp(),
    )(*parts)


def _small_adamw(reduced, params):
    n = len(reduced)
    n_w = len(params)

    def body(*refs):
        r_refs = refs[:n]
        wmv = refs[n:n + 3 * n_w]
        outs = refs[n + 3 * n_w:]
        outs[0][...] = r_refs[n - 1][0:1, 0:1]
        grads = [r_refs[0][...], r_refs[1][0:1, 0:N_Q_HEADS]] + [r_refs[i][...] for i in range(2, n_w)]
        for i in range(n_w):
            w_ref, m_ref, v_ref = wmv[3 * i:3 * i + 3]
            g_ref, d_ref, m2_ref, v2_ref = outs[1 + 4 * i:5 + 4 * i]
            delta, m2, v2 = _adamw_math(w_ref[...], grads[i], m_ref[...], v_ref[...])
            g_ref[...] = grads[i]
            d_ref[...] = delta
            m2_ref[...] = m2
            v2_ref[...] = v2

    flat = [a for p in params for a in p]
    vmem = pl.BlockSpec(memory_space=pltpu.VMEM)
    out_shape = [jax.ShapeDtypeStruct((1, 1), F32)]
    for p in params:
        out_shape += [jax.ShapeDtypeStruct(p[0].shape, F32)] * 4
    res = pl.pallas_call(
        body, name="small_adamw", out_shape=tuple(out_shape),
        in_specs=[vmem] * (n + len(flat)), out_specs=tuple([vmem] * len(out_shape)),
        compiler_params=_cp(),
    )(*reduced, *flat)
    return res[0], [res[1 + 4 * i:5 + 4 * i] for i in range(n_w)]


def kernel(x, attn_norm_g, w_in, attn_sinks, w_pool, pool_scale, w_out, mlp_norm_g, w_up, w_down, final_norm_g, loss_target, m_attn_norm_g, m_w_in, m_attn_sinks, m_w_pool, m_pool_scale, m_w_out, m_mlp_norm_g, m_w_up, m_w_down, m_final_norm_g, v_attn_norm_g, v_w_in, v_attn_sinks, v_w_pool, v_pool_scale, v_w_out, v_mlp_norm_g, v_w_up, v_w_down, v_final_norm_g):
    nseq, seq, d = x.shape
    c_idx = lax.axis_index("c").astype(jnp.int32)
    j_idx = (2 * lax.axis_index("x") + lax.axis_index("y")).astype(jnp.int32)
    c_arr = jnp.reshape(c_idx, (1,))
    jc_arr = jnp.stack([j_idx, c_idx])

    big_w = (w_in[0].T, w_out[0], w_up[0], w_down[0])
    big_m = (m_w_in[0].T, m_w_out[0], m_w_up[0], m_w_down[0])
    big_v = (v_w_in[0].T, v_w_out[0], v_w_up[0], v_w_down[0])
    (w_in_lands, w_in_send, w_in_recv), rest = _gather_start(big_w[:1], big_w[1:])
    (w_in4,) = _forward_now(_gather_wait(w_in_lands, w_in_send, w_in_recv, "gather_wait_w_in"), "forward_w_in")
    w_in_full = w_in4.reshape(IN_W, D_MODEL)
    class Comm(_NoComm):
        def attn_done(self, attn):
            arrived = _gather_wait(*rest, "gather_wait_rest", (attn,))
            self.lands, self.send, self.recv, token = _forward_start(arrived)
            return (token,)

        def rest_of_weights(self, pool):
            w_out4, w_up4, w_down4 = _forward_wait(self.lands, self.send, self.recv, pool)
            return w_out4.reshape(D_MODEL, D_MODEL), w_up4, w_down4

        def mlp_grads_ready(self, gw_up4, gw_down4):
            self.mlp = _sibling_start((gw_up4, gw_down4), "mlp_grads_to_sibling")
            return (self.mlp[4],)

        def outproj_done(self, gw_out):
            grads, lands, send, recv, _ = self.mlp
            sib_out = _sibling_start((gw_out.reshape(N_CHIPS, OUT_SHARD, D_MODEL),), "w_out_grad_to_sibling")
            mlp_grads, mlp_from_sib = _sibling_wait(grads, lands, send, recv, sib_out[4], "mlp_grads_to_sibling")
            partials = [_chip_partial(g, r, c_arr, "chip_partial_" + nm)
                        for g, r, nm in zip(mlp_grads, mlp_from_sib, names[2:])]
            out_grad, out_from_sib = _sibling_wait(*sib_out[:4], partials[1], "w_out_grad_to_sibling")
            partials = [_chip_partial(out_grad[0], out_from_sib[0], c_arr, "chip_partial_w_out")] + partials
            self.grads = tuple(out_grad) + tuple(mlp_grads)
            self.from_sib = tuple(out_from_sib) + tuple(mlp_from_sib)
            self.parts, self.part_lands, self.part_send, self.part_recv, token = _send_start(partials, "send_start")
            return (token,)

    names = ("w_in", "w_out", "w_up", "w_down")
    comm = Comm()
    gx, big_g, small_g = _local_step(
        x.reshape(nseq * seq, d), loss_target.reshape(nseq * seq, d), w_in_full, comm,
        attn_norm_g, attn_sinks.reshape(N_Q_HEADS), w_pool[0], pool_scale, mlp_norm_g, final_norm_g.reshape(1, d), seq)
    big_g = tuple(big_g[:1]) + comm.grads

    adamw = lambda i, a, b: _adamw_shard(a, b, big_w[i], big_m[i], big_v[i], c_arr, "adamw_" + names[i])
    sib_in = _sibling_start(big_g[:1], "w_in_grad_to_sibling")
    received = _send_wait(comm.parts, comm.part_lands, comm.part_send, comm.part_recv, sib_in[4], "send_wait")
    mine = [_final_half(g, r, p, jc_arr, "final_half_" + nm)
            for g, r, p, nm in zip(big_g[1:], comm.from_sib, received, names[1:])]
    halves = _sibling_start(mine, "halves_to_sibling", False)
    (g_in,), (r_in,) = _sibling_wait(*sib_in[:4], halves[4], "w_in_grad_to_sibling")
    send_in = _send_start([_chip_partial(g_in, r_in, c_arr, "chip_partial_w_in")], "send_start_w_in")
    mine, other = _sibling_wait(*halves[:4], send_in[4], "halves_to_sibling", False)
    big_out = [None, adamw(1, mine[0], other[0]), adamw(2, mine[1], other[1]), None]
    (p_in,) = _send_wait(*send_in[:4], big_out[2][0], "send_wait_w_in")
    mine_in = _final_half(g_in, r_in, p_in, jc_arr, "final_half_w_in")
    (other_in,) = _sibling_exchange([mine_in], False, "w_in_half_to_sibling")
    big_out[3] = adamw(3, mine[2], other[2])
    big_out[0] = [t.T for t in adamw(0, mine_in, other_in)]

    wp_flat = lambda a: a.reshape(N_POOL * POOL_G, POOL_G)
    small_params = [
        (attn_norm_g, m_attn_norm_g, v_attn_norm_g),
        (attn_sinks, m_attn_sinks, v_attn_sinks),
        (wp_flat(w_pool), wp_flat(m_w_pool), wp_flat(v_w_pool)),
        (pool_scale, m_pool_scale, v_pool_scale),
        (mlp_norm_g, m_mlp_norm_g, v_mlp_norm_g),
        (final_norm_g.reshape(1, d), m_final_norm_g.reshape(1, d), v_final_norm_g.reshape(1, d)),
    ]
    loss, small_out = _small_adamw(_small_allreduce(small_g), small_params)

    def shaped(i, arr):
        return {1: w_in, 5: w_out, 7: w_up, 8: w_down, 0: attn_norm_g, 2: attn_sinks, 3: w_pool, 4: pool_scale,
                6: mlp_norm_g, 9: final_norm_g}[i].shape

    order = [small_out[0], big_out[0], small_out[1], small_out[2], small_out[3], big_out[1], small_out[4], big_out[2],
             big_out[3], small_out[5]]
    outs = [loss.reshape(()), gx.reshape(nseq, seq, d)]
    for kind in range(4):
        outs += [order[i][kind].reshape(shaped(i, None)) for i in range(10)]
    return tuple(outs)
```

```python
import jax
import jax.numpy as jnp
import numpy as np
from jax import lax
from jax.experimental import pallas as pl
from jax.experimental.pallas import tpu as pltpu

F32 = jnp.float32
BF16 = jnp.bfloat16
MXU_DTYPE = jnp.bfloat16

D_MODEL = 1024
HEAD_DIM = 64
N_Q_HEADS = 8
N_KV_HEADS = 2
Q_PER_KV = N_Q_HEADS // N_KV_HEADS
ATTN_W = N_Q_HEADS * HEAD_DIM
KV_W = N_KV_HEADS * HEAD_DIM
BLK = 128
POOL_WINDOWS = (2, 4, 8, 16)
N_POOL = len(POOL_WINDOWS)
POOL_W = D_MODEL - ATTN_W
POOL_G = POOL_W // N_POOL
IN_W = ATTN_W + 2 * KV_W + POOL_W
D_FF = 4 * D_MODEL
EPS = 1e-6
ROPE_THETA = 10000.0
N_CHIPS = 4
IN_SHARD = IN_W // N_CHIPS
OUT_SHARD = D_MODEL // N_CHIPS
FF_SHARD = D_FF // N_CHIPS
LANES = 128

ADAM_LR = 0.001
ADAM_B1 = 0.9
ADAM_B2 = 0.999
ADAM_EPS = 1e-08
ADAM_WD = 0.01
ADAM_STEP = 10

VMEM_LIMIT = 56 * 1024 * 1024
TILE_BIG = 1024
TILE_MLP = 256
MESH = pl.DeviceIdType.MESH


def _cp(**kw):
    return pltpu.CompilerParams(vmem_limit_bytes=VMEM_LIMIT, **kw)


def _mm(a, b):
    return jnp.dot(a.astype(MXU_DTYPE), b.astype(MXU_DTYPE), preferred_element_type=F32)


def _mm_nt(a, b):
    return lax.dot_general(a.astype(MXU_DTYPE), b.astype(MXU_DTYPE), (((1,), (1,)), ((), ())),
                           preferred_element_type=F32)


def _mm_tn(a, b):
    return lax.dot_general(a.astype(MXU_DTYPE), b.astype(MXU_DTYPE), (((0,), (0,)), ((), ())),
                           preferred_element_type=F32)


def _resident(shape):
    nd = len(shape)
    return pl.BlockSpec(shape, lambda *_: (0,) * nd, pipeline_mode=pl.Buffered(1))


def _const(shape):
    nd = len(shape)
    return pl.BlockSpec(shape, lambda *_: (0,) * nd)


def _rope_tables(seq):
    half = HEAD_DIM // 2
    inv_freq = (ROPE_THETA ** (-np.arange(half, dtype=np.float32) / half)).astype(np.float32)
    ang = np.arange(seq, dtype=np.float32)[:, None] * inv_freq[None, :]
    cos, sin = np.cos(ang).astype(np.float32), np.sin(ang).astype(np.float32)
    cos_t = np.concatenate([cos, cos, cos, cos], axis=1)
    sin_t = np.concatenate([-sin, sin, -sin, sin], axis=1)
    return jnp.asarray(cos_t), jnp.asarray(sin_t)


def _swap_halves(xc):
    lane = lax.broadcasted_iota(jnp.int32, xc.shape, 1)
    return jnp.where((lane & 63) < 32, pltpu.roll(xc, 96, 1), pltpu.roll(xc, 32, 1))


_HBM = pl.BlockSpec(memory_space=pltpu.HBM)
_SEM = pl.BlockSpec(memory_space=pltpu.SEMAPHORE)
_ANY = pl.BlockSpec(memory_space=pl.ANY)
_EFFECT = pltpu.SideEffectType.DATAFLOW_SIDE_EFFECTING
TOKEN = jax.ShapeDtypeStruct((8, LANES), F32)


def _other_chips(x, y):
    return [(1 - x, y), (x, 1 - y), (1 - x, 1 - y)]


def _after(tokens):
    tokens = list(tokens)
    return tokens, [_ANY] * len(tokens)


def _gather_start(first, rest):
    shards = tuple(first) + tuple(rest)
    nw, nf = len(shards), len(first)

    def body(*refs):
        srcs, lands = refs[:nw], refs[nw:2 * nw]
        sems = refs[2 * nw:2 * nw + 4]
        stages = refs[2 * nw + 4:3 * nw + 4]
        lsem = refs[3 * nw + 4]
        x, y, c = lax.axis_index("x"), lax.axis_index("y"), lax.axis_index("c")
        j = 2 * x + y
        for k in range(nw):
            send, recv = sems[0:2] if k < nf else sems[2:4]
            slot = k if k < nf else k - nf
            stages[k][...] = srcs[k][...].astype(BF16)
            local = pltpu.make_async_copy(stages[k], lands[k].at[j], lsem.at[k])
            local.start()
            local.wait()
            rows = shards[k].shape[0] // 2
            mine = lands[k].at[j, pl.ds(c * rows, rows)]
            for n, chip in enumerate(_other_chips(x, y)):
                pltpu.make_async_remote_copy(
                    src_ref=mine, dst_ref=mine, send_sem=send.at[slot * 3 + n], recv_sem=recv.at[slot * 3 + n],
                    device_id=(chip[0], chip[1], c), device_id_type=MESH).start()

    vmem = pl.BlockSpec(memory_space=pltpu.VMEM)
    nr = nw - nf
    res = pl.pallas_call(
        body, name="gather_start",
        out_shape=tuple(pltpu.HBM((N_CHIPS,) + s.shape, BF16) for s in shards) + (
            pltpu.SemaphoreType.DMA((3 * nf,)), pltpu.SemaphoreType.DMA((3 * nf,)),
            pltpu.SemaphoreType.DMA((3 * nr,)), pltpu.SemaphoreType.DMA((3 * nr,))),
        in_specs=[vmem] * nw, out_specs=tuple([_HBM] * nw) + (_SEM, _SEM, _SEM, _SEM),
        scratch_shapes=[pltpu.VMEM(s.shape, BF16) for s in shards] + [pltpu.SemaphoreType.DMA((nw,))],
        compiler_params=_cp(has_side_effects=_EFFECT),
    )(*shards)
    return (res[:nf], res[nw], res[nw + 1]), (res[nf:nw], res[nw + 2], res[nw + 3])


def _gather_wait(lands, send, recv, name, after=()):
    nw = len(lands)
    tokens, token_specs = _after(after)

    def body(*refs):
        ins = refs[:nw]
        send_ref, recv_ref = refs[nw], refs[nw + 1]
        x, y, c = lax.axis_index("x"), lax.axis_index("y"), lax.axis_index("c")
        j = 2 * x + y
        for k in range(nw):
            rows = lands[k].shape[1] // 2
            mine = ins[k].at[j, pl.ds(c * rows, rows)]
            for n, chip in enumerate(_other_chips(x, y)):
                got = ins[k].at[2 * chip[0] + chip[1], pl.ds(c * rows, rows)]
                cp = pltpu.make_async_remote_copy(
                    src_ref=mine, dst_ref=got, send_sem=send_ref.at[k * 3 + n], recv_sem=recv_ref.at[k * 3 + n],
                    device_id=(chip[0], chip[1], c), device_id_type=MESH)
                cp.wait_send()
                cp.wait_recv()

    return pl.pallas_call(
        body, name=name, out_shape=tuple(pltpu.HBM(a.shape, a.dtype) for a in lands),
        in_specs=[_HBM] * nw + [_SEM, _SEM] + token_specs, out_specs=tuple([_HBM] * nw),
        input_output_aliases={k: k for k in range(nw)},
        compiler_params=_cp(has_side_effects=_EFFECT),
    )(*lands, send, recv, *tokens)


def _forward_now(lands, name):
    nw = len(lands)
    rows_of = [a.shape[1] // 2 for a in lands]

    def body(*refs):
        mine, theirs = _forward_copies(refs[nw:2 * nw], nw, rows_of, refs[2 * nw], refs[2 * nw + 1])
        for cp in mine:
            cp.start()
        for cp in theirs:
            cp.wait_recv()
        for cp in mine:
            cp.wait_send()

    return pl.pallas_call(
        body, name=name, out_shape=tuple(jax.ShapeDtypeStruct(a.shape, a.dtype) for a in lands),
        in_specs=[_HBM] * nw, out_specs=tuple([_HBM] * nw), input_output_aliases={k: k for k in range(nw)},
        scratch_shapes=[pltpu.SemaphoreType.DMA((3 * nw,)), pltpu.SemaphoreType.DMA((3 * nw,))],
        compiler_params=_cp(),
    )(*lands)


def _forward_copies(refs, nw, rows_of, send, recv):
    x, y, c = lax.axis_index("x"), lax.axis_index("y"), lax.axis_index("c")
    out = []
    for k in range(nw):
        rows = rows_of[k]
        for n, chip in enumerate(_other_chips(x, y)):
            got = refs[k].at[2 * chip[0] + chip[1], pl.ds(c * rows, rows)]
            theirs = refs[k].at[2 * chip[0] + chip[1], pl.ds((1 - c) * rows, rows)]
            out.append(pltpu.make_async_remote_copy(
                src_ref=got, dst_ref=got, send_sem=send.at[k * 3 + n], recv_sem=recv.at[k * 3 + n],
                device_id=(x, y, 1 - c), device_id_type=MESH))
            out.append(pltpu.make_async_remote_copy(
                src_ref=theirs, dst_ref=theirs, send_sem=send.at[k * 3 + n], recv_sem=recv.at[k * 3 + n],
                device_id=(x, y, 1 - c), device_id_type=MESH))
    return out[0::2], out[1::2]


def _forward_start(lands):
    nw = len(lands)
    rows_of = [a.shape[1] // 2 for a in lands]

    def body(*refs):
        send, recv, token = refs[2 * nw:2 * nw + 3]
        mine, _ = _forward_copies(refs[:nw], nw, rows_of, send, recv)
        for cp in mine:
            cp.start()
        token[...] = jnp.zeros_like(token)

    res = pl.pallas_call(
        body, name="forward_start",
        out_shape=tuple(pltpu.HBM(a.shape, a.dtype) for a in lands) + (
            pltpu.SemaphoreType.DMA((3 * nw,)), pltpu.SemaphoreType.DMA((3 * nw,)), TOKEN),
        in_specs=[_HBM] * nw, out_specs=tuple([_HBM] * nw) + (_SEM, _SEM, pl.BlockSpec(memory_space=pltpu.VMEM)),
        input_output_aliases={k: k for k in range(nw)},
        compiler_params=_cp(has_side_effects=_EFFECT),
    )(*lands)
    return res[:nw], res[nw], res[nw + 1], res[nw + 2]


def _forward_wait(lands, send, recv, after):
    nw = len(lands)
    rows_of = [a.shape[1] // 2 for a in lands]

    def body(*refs):
        mine, theirs = _forward_copies(refs[:nw], nw, rows_of, refs[nw], refs[nw + 1])
        for cp in mine:
            cp.wait_send()
        for cp in theirs:
            cp.wait_recv()

    return pl.pallas_call(
        body, name="forward_wait", out_shape=tuple(pltpu.HBM(a.shape, a.dtype) for a in lands),
        in_specs=[_HBM] * nw + [_SEM, _SEM, _ANY], out_specs=tuple([_HBM] * nw),
        input_output_aliases={k: k for k in range(nw)},
        compiler_params=_cp(has_side_effects=_EFFECT),
    )(*lands, send, recv, after)


def _fwd_inproj(x, g1, w_in, cos_t, sin_t, tm):
    t_tok = x.shape[0]
    seq = cos_t.shape[0]
    per_seq = seq // tm

    def body(x_ref, g_ref, w_ref, cos_ref, sin_ref, q_ref, k_ref, v_ref, u_ref):
        xv = x_ref[...]
        r = lax.rsqrt(jnp.mean(xv * xv, axis=-1, keepdims=True) + EPS)
        h = (xv * r) * g_ref[...]
        proj = _mm_nt(h, w_ref[...])
        cos, sin = cos_ref[...], sin_ref[...]
        for cidx in range((ATTN_W + KV_W) // LANES):
            xc = proj[:, cidx * LANES:(cidx + 1) * LANES]
            rot = xc * cos + _swap_halves(xc) * sin
            if cidx < ATTN_W // LANES:
                q_ref[:, cidx * LANES:(cidx + 1) * LANES] = (rot * (HEAD_DIM ** -0.5)).astype(q_ref.dtype)
            else:
                k_ref[...] = rot.astype(k_ref.dtype)
        v_ref[...] = proj[:, ATTN_W + KV_W:ATTN_W + 2 * KV_W].astype(v_ref.dtype)
        u_ref[...] = proj[:, ATTN_W + 2 * KV_W:].astype(u_ref.dtype)

    row = lambda w: pl.BlockSpec((tm, w), lambda i: (i, 0))
    kv = row(KV_W)
    tab = pl.BlockSpec((tm, LANES), lambda i: (i % per_seq, 0))
    kv_shape = jax.ShapeDtypeStruct((t_tok, KV_W), MXU_DTYPE)
    return pl.pallas_call(
        body, name="fwd_inproj", grid=(t_tok // tm,),
        in_specs=[row(D_MODEL), _const((1, D_MODEL)), _resident((IN_W, D_MODEL)), tab, tab],
        out_specs=(row(ATTN_W), kv, kv, row(POOL_W)),
        out_shape=(jax.ShapeDtypeStruct((t_tok, ATTN_W), MXU_DTYPE), kv_shape, kv_shape,
                   jax.ShapeDtypeStruct((t_tok, POOL_W), MXU_DTYPE)),
        compiler_params=_cp(dimension_semantics=("parallel",)),
    )(x, g1, w_in, cos_t, sin_t)


MASKED = -1e30


def _attn_bias():
    b = lax.broadcasted_iota(jnp.int32, (2 * BLK, BLK), 0)
    a = lax.broadcasted_iota(jnp.int32, (2 * BLK, BLK), 1)
    own = (b >= BLK) & (b - BLK <= a)
    prev = (b < BLK) & (b > a)
    return jnp.stack([jnp.where(own, 0.0, MASKED), jnp.where(own | prev, 0.0, MASKED)]).astype(F32)


def _sink_rows(sinks):
    return jnp.repeat(sinks.reshape(N_KV_HEADS, Q_PER_KV), BLK, axis=1).reshape(N_KV_HEADS, 1, Q_PER_KV * BLK)


def _stack_heads(ref, r0, g):
    return jnp.concatenate(
        [ref[pl.ds(r0, BLK), (Q_PER_KV * g + h) * HEAD_DIM:(Q_PER_KV * g + h + 1) * HEAD_DIM] for h in range(Q_PER_KV)],
        axis=0)


def _kv_window(ref, p0, r0, g):
    sl = slice(g * HEAD_DIM, (g + 1) * HEAD_DIM)
    return jnp.concatenate([ref[pl.ds(p0, BLK), sl], ref[pl.ds(r0, BLK), sl]], axis=0)


def _scores_t(k_ref, q_ref, bias, p0, r0, g):
    kk = _kv_window(k_ref, p0, r0, g)
    qs = _stack_heads(q_ref, r0, g)
    st = _mm_nt(kk, qs) + jnp.concatenate([bias] * Q_PER_KV, axis=1)
    return st, kk, qs


def _head_rows(ref, r0, g):
    return jnp.concatenate([ref[pl.ds(Q_PER_KV * g + h, 1), pl.ds(r0, BLK)] for h in range(Q_PER_KV)], axis=1)


def _pairs_to_rows(xt):
    out = []
    for t in range(Q_PER_KV // 2):
        pair = jnp.concatenate([xt[:, (2 * t) * BLK:(2 * t + 1) * BLK], xt[:, (2 * t + 1) * BLK:(2 * t + 2) * BLK]], axis=0)
        out.append(pair.T)
    return out


def _shift_rows(x, k, seq):
    row = lax.broadcasted_iota(jnp.int32, x.shape, 0)
    if k > 0:
        return jnp.where(row >= k, pltpu.roll(x, k, 0), 0.0)
    return jnp.where(row < seq + k, pltpu.roll(x, seq + k, 0), 0.0)


def _window_sum(x, w, seq, forward):
    s, k = x, 1
    while k < w:
        s = s + _shift_rows(s, -k if forward else k, seq)
        k *= 2
    return s


def _inv_count(seq, w):
    pos = lax.broadcasted_iota(jnp.int32, (seq, 1), 0)
    return 1.0 / jnp.minimum(pos + 1, w).astype(F32)


def _fwd_attn(q, k, v, sink_rows, bias, seq):
    t_tok = q.shape[0]
    nblk = seq // BLK

    def body(q_ref, k_ref, v_ref, sink_ref, bias_ref, o_ref, lse_ref):
        def blk(i, carry):
            r0 = pl.multiple_of(i * BLK, BLK)
            p0 = pl.multiple_of(jnp.maximum(i - 1, 0) * BLK, BLK)
            bias = bias_ref[jnp.minimum(i, 1)]
            for g in range(N_KV_HEADS):
                st, _, _ = _scores_t(k_ref, q_ref, bias, p0, r0, g)
                sink = sink_ref[g]
                m = jnp.maximum(jnp.max(st, axis=0, keepdims=True), sink)
                p = jnp.exp(st - m).astype(MXU_DTYPE)
                vv = _kv_window(v_ref, p0, r0, g)
                ot = _mm_tn(jnp.concatenate([vv, jnp.ones_like(vv)], axis=1), p)
                den = ot[HEAD_DIM:HEAD_DIM + 1] + jnp.exp(sink - m)
                lse = m + jnp.log(den)
                for h in range(Q_PER_KV):
                    lse_ref[pl.ds(Q_PER_KV * g + h, 1), pl.ds(r0, BLK)] = lse[:, h * BLK:(h + 1) * BLK]
                ot = ot[:HEAD_DIM] * (1.0 / den)
                for t, tile in enumerate(_pairs_to_rows(ot)):
                    c0 = (Q_PER_KV * g + 2 * t) * HEAD_DIM
                    o_ref[pl.ds(r0, BLK), c0:c0 + LANES] = tile.astype(o_ref.dtype)
            return carry

        lax.fori_loop(0, nblk, blk, 0, unroll=4)

    row = lambda w: pl.BlockSpec((seq, w), lambda i: (i, 0))
    kv = row(KV_W)
    return pl.pallas_call(
        body, name="fwd_attn", grid=(t_tok // seq,),
        in_specs=[row(ATTN_W), kv, kv, _const((N_KV_HEADS, 1, Q_PER_KV * BLK)), _const((2, 2 * BLK, BLK))],
        out_specs=(row(ATTN_W), pl.BlockSpec((N_Q_HEADS, seq), lambda i: (0, i))),
        out_shape=(jax.ShapeDtypeStruct((t_tok, ATTN_W), MXU_DTYPE), jax.ShapeDtypeStruct((N_Q_HEADS, t_tok), F32)),
        compiler_params=_cp(dimension_semantics=("parallel",)),
    )(q, k, v, sink_rows, bias)


def _fwd_pool(u, w_pool, pool_scale, seq, after=()):
    t_tok = u.shape[0]
    tokens, token_specs = _after(after)

    def body(u_ref, wp_ref, sc_ref, *rest):
        o_ref = rest[-1]
        for gi, w in enumerate(POOL_WINDOWS):
            sl = slice(gi * POOL_G, (gi + 1) * POOL_G)
            ug = u_ref[:, sl].astype(F32)
            d = _window_sum(ug, w, seq, False) * _inv_count(seq, w) - ug
            o_ref[:, sl] = (_mm(d, wp_ref[gi]) * sc_ref[:, sl]).astype(o_ref.dtype)

    row = pl.BlockSpec((seq, POOL_W), lambda i: (i, 0))
    return pl.pallas_call(
        body, name="fwd_pool", grid=(t_tok // seq,),
        in_specs=[row, _const((N_POOL, POOL_G, POOL_G)), _const((1, POOL_W))] + token_specs, out_specs=row,
        out_shape=jax.ShapeDtypeStruct((t_tok, POOL_W), MXU_DTYPE),
        compiler_params=_cp(dimension_semantics=("parallel",)),
    )(u, w_pool, pool_scale, *tokens)


def _rms_bwd(dy_g, xn, r):
    return r * (dy_g - xn * jnp.mean(dy_g * xn, axis=-1, keepdims=True))


def _mlp_fwd_bwd(x, attn, pool, target, w_out, w_up4, w_down4, g2, gf, tm):
    t_tok = x.shape[0]

    def body(x_ref, attn_ref, pool_ref, tgt_ref, wo_ref, wu_ref, wd_ref, g2_ref, gf_ref,
             h2_ref, a_ref, da_ref, dx2b_ref, dx1_ref, dx1b_ref, loss_ref, dgf_ref, dg2_ref):
        @pl.when(pl.program_id(0) == 0)
        def _():
            loss_ref[...] = jnp.zeros_like(loss_ref)
            dgf_ref[...] = jnp.zeros_like(dgf_ref)
            dg2_ref[...] = jnp.zeros_like(dg2_ref)

        x1 = x_ref[...] + (_mm(attn_ref[...], wo_ref[:ATTN_W]) + _mm(pool_ref[...], wo_ref[ATTN_W:]))
        r2 = lax.rsqrt(jnp.mean(x1 * x1, axis=-1, keepdims=True) + EPS)
        xn1 = x1 * r2
        g2 = g2_ref[...]
        h2 = (xn1 * g2).astype(MXU_DTYPE)
        h2_ref[...] = h2
        acc = jnp.zeros((tm, D_MODEL), F32)
        for j in range(N_CHIPS):
            a = _mm(h2, wu_ref[j])
            a_ref[:, j * FF_SHARD:(j + 1) * FF_SHARD] = a.astype(a_ref.dtype)
            acc = acc + _mm(jnp.square(jnp.maximum(a, 0.0)), wd_ref[j])
        x2 = x1 + acc
        r3 = lax.rsqrt(jnp.mean(x2 * x2, axis=-1, keepdims=True) + EPS)
        xn2 = x2 * r3
        gf_v = gf_ref[...]
        err = xn2 * gf_v - tgt_ref[...]
        part = jnp.sum(err * err) * (0.5 / D_MODEL)
        first = (lax.broadcasted_iota(jnp.int32, loss_ref.shape, 0) == 0) & (lax.broadcasted_iota(jnp.int32, loss_ref.shape, 1) == 0)
        loss_ref[...] += jnp.where(first, part, 0.0)
        dy = err * (1.0 / D_MODEL)
        dgf_ref[...] += jnp.sum(dy * xn2, axis=0, keepdims=True)
        dx2 = _rms_bwd(dy * gf_v, xn2, r3)
        dx2b = dx2.astype(MXU_DTYPE)
        dx2b_ref[...] = dx2b
        dh2 = jnp.zeros((tm, D_MODEL), F32)
        for j in range(N_CHIPS):
            sl = slice(j * FF_SHARD, (j + 1) * FF_SHARD)
            dhid = _mm_nt(dx2b, wd_ref[j])
            da = (dhid * (2.0 * jnp.maximum(a_ref[:, sl].astype(F32), 0.0))).astype(MXU_DTYPE)
            da_ref[:, sl] = da
            dh2 = dh2 + _mm_nt(da, wu_ref[j])
        dg2_ref[...] += jnp.sum(dh2 * xn1, axis=0, keepdims=True)
        dx1 = dx2 + _rms_bwd(dh2 * g2, xn1, r2)
        dx1_ref[...] = dx1
        dx1b_ref[...] = dx1.astype(dx1b_ref.dtype)

    row = lambda w: pl.BlockSpec((tm, w), lambda i: (i, 0))
    vec = jax.ShapeDtypeStruct((1, D_MODEL), F32)
    return pl.pallas_call(
        body, name="mlp_fwd_bwd", grid=(t_tok // tm,),
        in_specs=[row(D_MODEL), row(ATTN_W), row(POOL_W), row(D_MODEL), _resident((D_MODEL, D_MODEL)),
                  _resident((N_CHIPS, D_MODEL, FF_SHARD)), _resident((N_CHIPS, FF_SHARD, D_MODEL)),
                  _const((1, D_MODEL)), _const((1, D_MODEL))],
        out_specs=(row(D_MODEL), row(D_FF), row(D_FF), row(D_MODEL), row(D_MODEL), row(D_MODEL),
                   _const((8, LANES)), _const((1, D_MODEL)), _const((1, D_MODEL))),
        out_shape=(jax.ShapeDtypeStruct((t_tok, D_MODEL), MXU_DTYPE), jax.ShapeDtypeStruct((t_tok, D_FF), MXU_DTYPE),
                   jax.ShapeDtypeStruct((t_tok, D_FF), MXU_DTYPE), jax.ShapeDtypeStruct((t_tok, D_MODEL), MXU_DTYPE),
                   jax.ShapeDtypeStruct((t_tok, D_MODEL), F32), jax.ShapeDtypeStruct((t_tok, D_MODEL), MXU_DTYPE),
                   jax.ShapeDtypeStruct((8, LANES), F32), vec, vec),
        compiler_params=_cp(dimension_semantics=("arbitrary",)),
    )(x, attn, pool, target, w_out, w_up4, w_down4, g2, gf)


def _bwd_mlp_wgrads(h2, da, a, dx2b, tk):
    t_tok = h2.shape[0]

    def body(h2_ref, da_ref, a_ref, dx2b_ref, gup_ref, gdn_ref):
        @pl.when(pl.program_id(1) == 0)
        def _():
            gup_ref[...] = jnp.zeros_like(gup_ref)
            gdn_ref[...] = jnp.zeros_like(gdn_ref)

        gup_ref[0] += _mm_tn(h2_ref[...], da_ref[...])
        hid = jnp.square(jnp.maximum(a_ref[...].astype(F32), 0.0))
        gdn_ref[0] += _mm_tn(hid, dx2b_ref[...])

    tok = pl.BlockSpec((tk, D_MODEL), lambda j, t: (t, 0))
    ffb = pl.BlockSpec((tk, FF_SHARD), lambda j, t: (t, j))
    wblk = pl.BlockSpec((1, D_MODEL, D_MODEL), lambda j, t: (j, 0, 0))
    return pl.pallas_call(
        body, name="bwd_mlp_wgrads", grid=(N_CHIPS, t_tok // tk),
        in_specs=[tok, ffb, ffb, tok], out_specs=(wblk, wblk),
        out_shape=(jax.ShapeDtypeStruct((N_CHIPS, D_MODEL, FF_SHARD), F32), jax.ShapeDtypeStruct((N_CHIPS, FF_SHARD, D_MODEL), F32)),
        compiler_params=_cp(dimension_semantics=("parallel", "arbitrary")),
    )(h2, da, a, dx2b)


def _head_selector():
    ch = lax.broadcasted_iota(jnp.int32, (ATTN_W, LANES), 0)
    col = lax.broadcasted_iota(jnp.int32, (ATTN_W, LANES), 1)
    return (ch // HEAD_DIM == col).astype(MXU_DTYPE)


def _bwd_outproj(dx1b, attn, pool, w_out, head_sel, tm, after=()):
    t_tok = dx1b.shape[0]
    tokens, token_specs = _after(after)

    def body(dx_ref, attn_ref, pool_ref, wo_ref, sel_ref, *rest):
        dattn_ref, dpool_ref, delta_ref, gwo_ref = rest[len(tokens):]

        @pl.when(pl.program_id(0) == 0)
        def _():
            gwo_ref[...] = jnp.zeros_like(gwo_ref)

        dx = dx_ref[...]
        attn = attn_ref[...]
        dattn = _mm_nt(dx, wo_ref[:ATTN_W])
        dattn_ref[...] = dattn.astype(dattn_ref.dtype)
        dpool_ref[...] = _mm_nt(dx, wo_ref[ATTN_W:]).astype(dpool_ref.dtype)
        prod = dattn * attn.astype(F32)
        hi = prod.astype(MXU_DTYPE)
        lo = prod - hi.astype(F32)
        delta = _mm(hi, sel_ref[...]) + _mm(lo, sel_ref[...])
        delta_ref[...] = delta.T[:N_Q_HEADS]
        gwo_ref[:ATTN_W] += _mm_tn(attn, dx)
        gwo_ref[ATTN_W:] += _mm_tn(pool_ref[...], dx)

    row = lambda w: pl.BlockSpec((tm, w), lambda i: (i, 0))
    return pl.pallas_call(
        body, name="bwd_outproj", grid=(t_tok // tm,),
        in_specs=[row(D_MODEL), row(ATTN_W), row(POOL_W), _resident((D_MODEL, D_MODEL)), _const((ATTN_W, LANES))] + token_specs,
        out_specs=(row(ATTN_W), row(POOL_W), pl.BlockSpec((N_Q_HEADS, tm), lambda i: (0, i)), _const((D_MODEL, D_MODEL))),
        out_shape=(jax.ShapeDtypeStruct((t_tok, ATTN_W), MXU_DTYPE), jax.ShapeDtypeStruct((t_tok, POOL_W), MXU_DTYPE),
                   jax.ShapeDtypeStruct((N_Q_HEADS, t_tok), F32), jax.ShapeDtypeStruct((D_MODEL, D_MODEL), F32)),
        compiler_params=_cp(dimension_semantics=("arbitrary",)),
    )(dx1b, attn, pool, w_out, head_sel, *tokens)


def _bwd_attn(q, k, v, dattn, lse, delta, sink_rows, bias, cos_t, sin_t, seq, after=()):
    t_tok = q.shape[0]
    nblk = seq // BLK
    unroll = max(f for f in (4, 2, 1) if nblk % f == 0)
    qkv_w = ATTN_W + 2 * KV_W
    tokens, token_specs = _after(after)

    def unrope(d, cos, sin):
        return d * cos - _swap_halves(d) * sin

    def body(q_ref, k_ref, v_ref, do_ref, lse_ref, delta_ref, sink_ref, bias_ref, cos_ref, sin_ref, *rest):
        dqkv_ref, dsink_ref, dk_acc, dv_acc = rest[len(tokens):]

        @pl.when(pl.program_id(0) == 0)
        def _():
            dsink_ref[...] = jnp.zeros_like(dsink_ref)

        dk_acc[...] = jnp.zeros_like(dk_acc)
        dv_acc[...] = jnp.zeros_like(dv_acc)

        def blk(i, dsink):
            r0 = pl.multiple_of(i * BLK, BLK)
            p0 = pl.multiple_of(jnp.maximum(i - 1, 0) * BLK, BLK)
            bias = bias_ref[jnp.minimum(i, 1)]
            cos, sin = cos_ref[pl.ds(r0, BLK), :], sin_ref[pl.ds(r0, BLK), :]
            new = []
            for g in range(N_KV_HEADS):
                st, kk, qs = _scores_t(k_ref, q_ref, bias, p0, r0, g)
                lse_g, delta_g = _head_rows(lse_ref, r0, g), _head_rows(delta_ref, r0, g)
                pn = jnp.exp(st - lse_g)
                dos = _stack_heads(do_ref, r0, g)
                dst = pn * (_mm_nt(_kv_window(v_ref, p0, r0, g), dos) - delta_g)
                new.append(dsink[g] - jnp.exp(sink_ref[g] - lse_g) * delta_g)
                dqt = _mm_tn(kk, dst) * (HEAD_DIM ** -0.5)
                for t, tile in enumerate(_pairs_to_rows(dqt)):
                    c0 = (Q_PER_KV * g + 2 * t) * HEAD_DIM
                    dqkv_ref[pl.ds(r0, BLK), c0:c0 + LANES] = unrope(tile, cos, sin).astype(dqkv_ref.dtype)
                dkk = _mm(dst, qs)
                dvv = _mm(pn, dos)
                dk_acc[g, pl.ds(p0, BLK), :] += dkk[:BLK]
                dk_acc[g, pl.ds(r0, BLK), :] += dkk[BLK:]
                dv_acc[g, pl.ds(p0, BLK), :] += dvv[:BLK]
                dv_acc[g, pl.ds(r0, BLK), :] += dvv[BLK:]
            return tuple(new)

        zero = jnp.zeros((1, Q_PER_KV * BLK), F32)
        def several(ik, acc):
            for s in range(unroll):
                acc = blk(unroll * ik + s, acc)
            return acc

        dsink = lax.fori_loop(0, nblk // unroll, several, (zero,) * N_KV_HEADS)
        lane = lax.broadcasted_iota(jnp.int32, dsink_ref.shape, 1)
        row = lax.broadcasted_iota(jnp.int32, dsink_ref.shape, 0)
        tile = jnp.zeros(dsink_ref.shape, F32)
        for g in range(N_KV_HEADS):
            for h in range(Q_PER_KV):
                tot = jnp.sum(dsink[g][:, h * BLK:(h + 1) * BLK])
                tile = tile + jnp.where((row == 0) & (lane == Q_PER_KV * g + h), tot, 0.0)
        dsink_ref[...] += tile
        dk = jnp.concatenate([dk_acc[g] for g in range(N_KV_HEADS)], axis=1)
        dqkv_ref[:, ATTN_W:ATTN_W + KV_W] = unrope(dk, cos_ref[...], sin_ref[...]).astype(dqkv_ref.dtype)
        dqkv_ref[:, ATTN_W + KV_W:] = jnp.concatenate([dv_acc[g] for g in range(N_KV_HEADS)], axis=1).astype(dqkv_ref.dtype)

    row = lambda w: pl.BlockSpec((seq, w), lambda i: (i, 0))
    kv = row(KV_W)
    per_head = pl.BlockSpec((N_Q_HEADS, seq), lambda i: (0, i))
    return pl.pallas_call(
        body, name="bwd_attn", grid=(t_tok // seq,),
        in_specs=[row(ATTN_W), kv, kv, row(ATTN_W), per_head, per_head, _const((N_KV_HEADS, 1, Q_PER_KV * BLK)),
                  _const((2, 2 * BLK, BLK)), _resident((seq, LANES)), _resident((seq, LANES))] + token_specs,
        out_specs=(row(qkv_w), _const((8, LANES))),
        out_shape=(jax.ShapeDtypeStruct((t_tok, qkv_w), MXU_DTYPE), jax.ShapeDtypeStruct((8, LANES), F32)),
        scratch_shapes=[pltpu.VMEM((N_KV_HEADS, seq, HEAD_DIM), F32), pltpu.VMEM((N_KV_HEADS, seq, HEAD_DIM), F32)],
        compiler_params=_cp(dimension_semantics=("arbitrary",)),
    )(q, k, v, dattn, lse, delta, sink_rows, bias, cos_t, sin_t, *tokens)


def _bwd_pool(u, dpool, w_pool, pool_scale, seq):
    t_tok = u.shape[0]

    def body(u_ref, dp_ref, wp_ref, sc_ref, du_ref, dwp_ref, dsc_ref):
        @pl.when(pl.program_id(0) == 0)
        def _():
            dwp_ref[...] = jnp.zeros_like(dwp_ref)
            dsc_ref[...] = jnp.zeros_like(dsc_ref)

        for gi, w in enumerate(POOL_WINDOWS):
            sl = slice(gi * POOL_G, (gi + 1) * POOL_G)
            ug = u_ref[:, sl].astype(F32)
            inv = _inv_count(seq, w)
            d = (_window_sum(ug, w, seq, False) * inv - ug).astype(MXU_DTYPE)
            y = _mm(d, wp_ref[gi])
            dpool = dp_ref[:, sl].astype(F32)
            dsc_ref[:, sl] += jnp.sum(y * dpool, axis=0, keepdims=True)
            dy = (dpool * sc_ref[:, sl]).astype(MXU_DTYPE)
            dwp_ref[gi] += _mm_tn(d, dy)
            dd = _mm_nt(dy, wp_ref[gi])
            du_ref[:, sl] = (_window_sum(dd * inv, w, seq, True) - dd).astype(du_ref.dtype)

    row = pl.BlockSpec((seq, POOL_W), lambda i: (i, 0))
    return pl.pallas_call(
        body, name="bwd_pool", grid=(t_tok // seq,),
        in_specs=[row, row, _const((N_POOL, POOL_G, POOL_G)), _const((1, POOL_W))],
        out_specs=(row, _const((N_POOL, POOL_G, POOL_G)), _const((1, POOL_W))),
        out_shape=(jax.ShapeDtypeStruct((t_tok, POOL_W), MXU_DTYPE), jax.ShapeDtypeStruct((N_POOL, POOL_G, POOL_G), F32),
                   jax.ShapeDtypeStruct((1, POOL_W), F32)),
        compiler_params=_cp(dimension_semantics=("arbitrary",)),
    )(u, dpool, w_pool, pool_scale)


def _bwd_inproj(dqkv, du, x, dx1, w_in, g1, tm):
    t_tok = x.shape[0]
    nsteps = t_tok // tm

    qkv_w = ATTN_W + 2 * KV_W

    def body(dqkv_ref, du_ref, x_ref, dx1_ref, w_ref, g_ref, gx_ref, gw_ref, dg_ref):
        @pl.when(pl.program_id(0) == 0)
        def _():
            gw_ref[...] = jnp.zeros_like(gw_ref)
            dg_ref[...] = jnp.zeros_like(dg_ref)

        dqkv, du = dqkv_ref[...], du_ref[...]
        xv = x_ref[...]
        r = lax.rsqrt(jnp.mean(xv * xv, axis=-1, keepdims=True) + EPS)
        xn = xv * r
        g = g_ref[...]
        dh = _mm(dqkv, w_ref[:qkv_w]) + _mm(du, w_ref[qkv_w:])
        dg_ref[...] += jnp.sum(dh * xn, axis=0, keepdims=True)
        gx_ref[...] = dx1_ref[...] + _rms_bwd(dh * g, xn, r)
        h = (xn * g).astype(MXU_DTYPE)
        gw_ref[:qkv_w] += _mm_tn(dqkv, h)
        gw_ref[qkv_w:] += _mm_tn(du, h)

    row = lambda w: pl.BlockSpec((tm, w), lambda i: (i, 0))
    return pl.pallas_call(
        body, name="bwd_inproj", grid=(nsteps,),
        in_specs=[row(qkv_w), row(POOL_W), row(D_MODEL), row(D_MODEL), _resident((IN_W, D_MODEL)), _const((1, D_MODEL))],
        out_specs=(row(D_MODEL), _const((IN_W, D_MODEL)), _const((1, D_MODEL))),
        out_shape=(jax.ShapeDtypeStruct((t_tok, D_MODEL), F32), jax.ShapeDtypeStruct((IN_W, D_MODEL), F32),
                   jax.ShapeDtypeStruct((1, D_MODEL), F32)),
        compiler_params=_cp(dimension_semantics=("arbitrary",)),
    )(dqkv, du, x, dx1, w_in, g1)


class _NoComm:
    def attn_done(self, attn):
        return ()

    def rest_of_weights(self, pool):
        raise NotImplementedError

    def mlp_grads_ready(self, gw_up4, gw_down4):
        return ()

    def outproj_done(self, gw_out):
        return ()


def _local_step(x, target, w_in, comm, g1, sinks, w_pool, pool_scale, g2, gf, seq):
    tm_big, tm_mlp = min(TILE_BIG, seq), min(TILE_MLP, seq)
    cos_t, sin_t = _rope_tables(seq)
    sink_rows, bias = _sink_rows(sinks), _attn_bias()
    q, k, v, u = _fwd_inproj(x, g1, w_in, cos_t, sin_t, tm_big)
    attn, lse = _fwd_attn(q, k, v, sink_rows, bias, seq)
    pool = _fwd_pool(u, w_pool, pool_scale, seq, comm.attn_done(attn))
    w_out, w_up4, w_down4 = comm.rest_of_weights(pool)
    h2, a, da, dx2b, dx1, dx1b, loss, dgf, dg2 = _mlp_fwd_bwd(x, attn, pool, target, w_out, w_up4, w_down4, g2, gf, tm_mlp)
    gw_up4, gw_down4 = _bwd_mlp_wgrads(h2, da, a, dx2b, min(TILE_BIG, x.shape[0]))
    dattn, dpool, delta, gw_out = _bwd_outproj(dx1b, attn, pool, w_out, _head_selector(), tm_big,
                                               comm.mlp_grads_ready(gw_up4, gw_down4))
    dqkv, dsinks = _bwd_attn(q, k, v, dattn, lse, delta, sink_rows, bias, cos_t, sin_t, seq, comm.outproj_done(gw_out))
    du, dwp, dsc = _bwd_pool(u, dpool, w_pool, pool_scale, seq)
    gx, gw_in_t, dg1 = _bwd_inproj(dqkv, du, x, dx1, w_in, g1, tm_big)
    big = (gw_in_t.reshape(N_CHIPS, IN_SHARD, D_MODEL), gw_out.reshape(N_CHIPS, OUT_SHARD, D_MODEL), gw_up4, gw_down4)
    small = (dg1, dsinks, dwp.reshape(N_POOL * POOL_G, POOL_G), dsc, dg2, dgf, loss)
    return gx, big, small


def _sibling_swap(arrs, name):
    n = len(arrs)

    def body(*refs):
        cps = _sibling_copies(refs[:n], refs[n:2 * n], refs[2 * n], refs[2 * n + 1], False)
        for cp in cps:
            cp.start()
        for cp in cps:
            cp.wait()

    return pl.pallas_call(
        body, name=name, out_shape=tuple(jax.ShapeDtypeStruct(a.shape, a.dtype) for a in arrs),
        in_specs=[_HBM] * n, out_specs=tuple([_HBM] * n),
        scratch_shapes=[pltpu.SemaphoreType.DMA((n,)), pltpu.SemaphoreType.DMA((n,))],
        compiler_params=_cp(),
    )(*arrs)


def _sibling_copies(srcs, lands, send, recv, pick_half):
    x, y, c = lax.axis_index("x"), lax.axis_index("y"), lax.axis_index("c")
    cps = []
    for i in range(len(srcs)):
        src = srcs[i]
        if pick_half:
            h = src.shape[1] // 2
            src = src.at[:, pl.ds((1 - c) * h, h)]
        cps.append(pltpu.make_async_remote_copy(
            src_ref=src, dst_ref=lands[i], send_sem=send.at[i], recv_sem=recv.at[i],
            device_id=(x, y, 1 - c), device_id_type=MESH))
    return cps


def _sibling_start(arrs, name, pick_half=True):
    n = len(arrs)

    def body(*refs):
        send, recv, token = refs[3 * n:3 * n + 3]
        for cp in _sibling_copies(refs[:n], refs[2 * n:3 * n], send, recv, pick_half):
            cp.start()
        token[...] = jnp.zeros_like(token)

    half = lambda a: ((a.shape[0], a.shape[1] // 2) + a.shape[2:]) if pick_half else a.shape
    res = pl.pallas_call(
        body, name=name + "_start",
        out_shape=tuple(pltpu.HBM(a.shape, a.dtype) for a in arrs) + tuple(pltpu.HBM(half(a), a.dtype) for a in arrs) + (
            pltpu.SemaphoreType.DMA((n,)), pltpu.SemaphoreType.DMA((n,)), TOKEN),
        in_specs=[_HBM] * n, out_specs=tuple([_HBM] * (2 * n)) + (_SEM, _SEM, pl.BlockSpec(memory_space=pltpu.VMEM)),
        input_output_aliases={i: i for i in range(n)},
        compiler_params=_cp(has_side_effects=_EFFECT),
    )(*[pltpu.with_memory_space_constraint(a, pltpu.HBM) for a in arrs])
    return res[:n], res[n:2 * n], res[2 * n], res[2 * n + 1], res[2 * n + 2]


def _sibling_wait(arrs, lands, send, recv, after, name, pick_half=True):
    n = len(arrs)

    def body(*refs):
        for cp in _sibling_copies(refs[:n], refs[n:2 * n], refs[2 * n], refs[2 * n + 1], pick_half):
            cp.wait_send()
            cp.wait_recv()

    res = pl.pallas_call(
        body, name=name + "_wait", out_shape=tuple(pltpu.HBM(a.shape, a.dtype) for a in list(arrs) + list(lands)),
        in_specs=[_HBM] * (2 * n) + [_SEM, _SEM, _ANY], out_specs=tuple([_HBM] * (2 * n)),
        input_output_aliases={i: i for i in range(2 * n)},
        compiler_params=_cp(has_side_effects=_EFFECT),
    )(*arrs, *lands, send, recv, after)
    return res[:n], res[n:]


def _row_block(rows):
    if rows <= 256:
        return rows
    for cand in (256, 128, 64, 32, 16, 8):
        if rows % cand == 0:
            return cand
    raise ValueError(rows)


def _chip_partial(g4s, r4s, c_arr, name):
    n = len(g4s)
    _, rows, cols = r4s[0].shape
    rb = _row_block(rows)
    nb = rows // rb

    def body(c_ref, *refs):
        for k in range(n):
            refs[2 * n + k][...] = (refs[k][...] + refs[n + k][...]).astype(BF16)

    own = pl.BlockSpec((1, rb, cols), lambda s, i, c: (s, c[0] * nb + i, 0))
    blk = pl.BlockSpec((1, rb, cols), lambda s, i, c: (s, i, 0))
    return pl.pallas_call(
        body, name=name,
        grid_spec=pltpu.PrefetchScalarGridSpec(num_scalar_prefetch=1, grid=(N_CHIPS, nb),
                                               in_specs=[own] * n + [blk] * n, out_specs=tuple([blk] * n)),
        out_shape=tuple(jax.ShapeDtypeStruct(r.shape, BF16) for r in r4s),
        compiler_params=_cp(dimension_semantics=("parallel", "parallel")),
    )(c_arr, *g4s, *r4s)


def _send_start(parts, name):
    n = len(parts)

    def body(*refs):
        srcs, lands = refs[:n], refs[2 * n:3 * n]
        send, recv, token = refs[3 * n:3 * n + 3]
        x, y, c = lax.axis_index("x"), lax.axis_index("y"), lax.axis_index("c")
        for i in range(n):
            for m, chip in enumerate(_other_chips(x, y)):
                pltpu.make_async_remote_copy(
                    src_ref=srcs[i].at[2 * chip[0] + chip[1]], dst_ref=lands[i].at[m],
                    send_sem=send.at[3 * i + m], recv_sem=recv.at[3 * i + m],
                    device_id=(chip[0], chip[1], c), device_id_type=MESH).start()
        token[...] = jnp.zeros_like(token)

    res = pl.pallas_call(
        body, name=name,
        out_shape=tuple(pltpu.HBM(p.shape, p.dtype) for p in parts) + tuple(pltpu.HBM((3,) + p.shape[1:], p.dtype) for p in parts) + (
            pltpu.SemaphoreType.DMA((3 * n,)), pltpu.SemaphoreType.DMA((3 * n,)), TOKEN),
        in_specs=[_HBM] * n, out_specs=tuple([_HBM] * (2 * n)) + (_SEM, _SEM, pl.BlockSpec(memory_space=pltpu.VMEM)),
        input_output_aliases={i: i for i in range(n)},
        compiler_params=_cp(has_side_effects=_EFFECT),
    )(*[pltpu.with_memory_space_constraint(p, pltpu.HBM) for p in parts])
    return res[:n], res[n:2 * n], res[2 * n], res[2 * n + 1], res[2 * n + 2]


def _send_wait(parts, lands, send, recv, after, name):
    n = len(parts)

    def body(*refs):
        srcs, ins = refs[:n], refs[n:2 * n]
        send_ref, recv_ref = refs[2 * n], refs[2 * n + 1]
        x, y, c = lax.axis_index("x"), lax.axis_index("y"), lax.axis_index("c")
        for i in range(n):
            for m, chip in enumerate(_other_chips(x, y)):
                cp = pltpu.make_async_remote_copy(
                    src_ref=srcs[i].at[2 * chip[0] + chip[1]], dst_ref=ins[i].at[m],
                    send_sem=send_ref.at[3 * i + m], recv_sem=recv_ref.at[3 * i + m],
                    device_id=(chip[0], chip[1], c), device_id_type=MESH)
                cp.wait_send()
                cp.wait_recv()

    res = pl.pallas_call(
        body, name=name, out_shape=tuple(pltpu.HBM(a.shape, a.dtype) for a in list(parts) + list(lands)),
        in_specs=[_HBM] * (2 * n) + [_SEM, _SEM, _ANY], out_specs=tuple([_HBM] * (2 * n)),
        input_output_aliases={i: i for i in range(2 * n)},
        compiler_params=_cp(has_side_effects=_EFFECT),
    )(*parts, *lands, send, recv, after)
    return res[n:]


def _final_half(g4s, r4s, got3s, jc_arr, name):
    n = len(g4s)
    _, rows, cols = r4s[0].shape
    rb = _row_block(rows)
    nb = rows // rb

    def body(jc_ref, *refs):
        for k in range(n):
            g_ref, r_ref, p_ref, o_ref = refs[k], refs[n + k], refs[2 * n + k], refs[3 * n + k]
            own = g_ref[0] + r_ref[0]
            o_ref[...] = ((own + p_ref[0].astype(F32)) + p_ref[1].astype(F32)) + p_ref[2].astype(F32)

    own = pl.BlockSpec((1, rb, cols), lambda i, jc: (jc[0], jc[1] * nb + i, 0))
    sib = pl.BlockSpec((1, rb, cols), lambda i, jc: (jc[0], i, 0))
    got = pl.BlockSpec((3, rb, cols), lambda i, jc: (0, i, 0))
    out = pl.BlockSpec((rb, cols), lambda i, jc: (i, 0))
    return pl.pallas_call(
        body, name=name,
        grid_spec=pltpu.PrefetchScalarGridSpec(num_scalar_prefetch=1, grid=(nb,),
                                               in_specs=[own] * n + [sib] * n + [got] * n, out_specs=tuple([out] * n)),
        out_shape=tuple(jax.ShapeDtypeStruct((rows, cols), F32) for _ in range(n)),
        compiler_params=_cp(dimension_semantics=("parallel",)),
    )(jc_arr, *g4s, *r4s, *got3s)


def _adamw_math(w, g, m, v):
    m2 = ADAM_B1 * m + (1.0 - ADAM_B1) * g
    v2 = ADAM_B2 * v + (1.0 - ADAM_B2) * (g * g)
    m_hat = m2 / (1.0 - ADAM_B1 ** ADAM_STEP)
    v_hat = v2 / (1.0 - ADAM_B2 ** ADAM_STEP)
    delta = -ADAM_LR * (m_hat / (jnp.sqrt(v_hat) + ADAM_EPS) + ADAM_WD * w)
    return delta, m2, v2


def _adamw_shard(mines, others, ws, ms, vs, c_arr, name):
    n = len(ws)
    rows, cols = ws[0].shape
    half = rows // 2
    rb = _row_block(half)
    nb = half // rb

    def body(c_ref, *refs):
        for k in range(n):
            a_ref, b_ref, w_ref, m_ref, v_ref = (refs[i * n + k] for i in range(5))
            g_ref, d_ref, m2_ref, v2_ref = refs[5 * n + 4 * k:5 * n + 4 * k + 4]
            g = jnp.where(pl.program_id(0) == c_ref[0], a_ref[...], b_ref[...])
            delta, m2, v2 = _adamw_math(w_ref[...], g, m_ref[...], v_ref[...])
            g_ref[...] = g
            d_ref[...] = delta
            m2_ref[...] = m2
            v2_ref[...] = v2

    hb = pl.BlockSpec((rb, cols), lambda h, i, c: (i, 0))
    fb = pl.BlockSpec((rb, cols), lambda h, i, c: (h * nb + i, 0))
    shp = jax.ShapeDtypeStruct((rows, cols), F32)
    res = pl.pallas_call(
        body, name=name,
        grid_spec=pltpu.PrefetchScalarGridSpec(num_scalar_prefetch=1, grid=(2, nb),
                                               in_specs=[hb] * (2 * n) + [fb] * (3 * n), out_specs=tuple([fb] * (4 * n))),
        out_shape=tuple([shp] * (4 * n)),
        compiler_params=_cp(dimension_semantics=("parallel", "parallel")),
    )(c_arr, *mines, *others, *ws, *ms, *vs)
    return [res[4 * k:4 * k + 4] for k in range(n)]


def _small_allreduce(parts):
    n = len(parts)

    def body(*refs):
        p_refs, accs = refs[:n], refs[n:2 * n]
        bufs = refs[2 * n:3 * n]
        send, recv = refs[3 * n:]
        x, y, c = lax.axis_index("x"), lax.axis_index("y"), lax.axis_index("c")
        partners = [(x, y, 1 - c), (1 - x, y, c), (x, 1 - y, c)]
        other_way = [partners[0], partners[2], partners[1]]
        for i in range(n):
            accs[i][...] = p_refs[i][...]
        for s in range(3):
            cps = []
            for i in range(n):
                rows = parts[i].shape[0]
                pieces = [(slice(None), partners[s])] if rows < 2 * BLK else [
                    (pl.ds(0, rows // 2), partners[s]), (pl.ds(rows // 2, rows // 2), other_way[s])]
                for h, (sl, partner) in enumerate(pieces):
                    cp = pltpu.make_async_remote_copy(
                        src_ref=accs[i].at[sl], dst_ref=bufs[i].at[s, sl], send_sem=send.at[6 * i + 2 * s + h],
                        recv_sem=recv.at[6 * i + 2 * s + h], device_id=partner, device_id_type=MESH)
                    cp.start()
                    cps.append(cp)
            for cp in cps:
                cp.wait()
            for i in range(n):
                accs[i][...] = accs[i][...] + bufs[i][s]

    vmem = pl.BlockSpec(memory_space=pltpu.VMEM)
    return pl.pallas_call(
        body, name="small_allreduce", out_shape=tuple(jax.ShapeDtypeStruct(p.shape, F32) for p in parts),
        in_specs=[vmem] * n, out_specs=tuple([vmem] * n),
        scratch_shapes=[pltpu.VMEM((3,) + p.shape, F32) for p in parts] + [
            pltpu.SemaphoreType.DMA((6 * n,)), pltpu.SemaphoreType.DMA((6 * n,))],
        compiler_params=_cp(),
    )(*parts)


def _small_adamw(reduced, params):
    n = len(reduced)
    n_w = len(params)

    def body(*refs):
        r_refs = refs[:n]
        wmv = refs[n:n + 3 * n_w]
        outs = refs[n + 3 * n_w:]
        outs[0][...] = r_refs[n - 1][0:1, 0:1]
        grads = [r_refs[0][...], r_refs[1][0:1, 0:N_Q_HEADS]] + [r_refs[i][...] for i in range(2, n_w)]
        for i in range(n_w):
            w_ref, m_ref, v_ref = wmv[3 * i:3 * i + 3]
            g_ref, d_ref, m2_ref, v2_ref = outs[1 + 4 * i:5 + 4 * i]
            delta, m2, v2 = _adamw_math(w_ref[...], grads[i], m_ref[...], v_ref[...])
            g_ref[...] = grads[i]
            d_ref[...] = delta
            m2_ref[...] = m2
            v2_ref[...] = v2

    flat = [a for p in params for a in p]
    vmem = pl.BlockSpec(memory_space=pltpu.VMEM)
    out_shape = [jax.ShapeDtypeStruct((1, 1), F32)]
    for p in params:
        out_shape += [jax.ShapeDtypeStruct(p[0].shape, F32)] * 4
    res = pl.pallas_call(
        body, name="small_adamw", out_shape=tuple(out_shape),
        in_specs=[vmem] * (n + len(flat)), out_specs=tuple([vmem] * len(out_shape)),
        compiler_params=_cp(),
    )(*reduced, *flat)
    return res[0], [res[1 + 4 * i:5 + 4 * i] for i in range(n_w)]


def kernel(x, attn_norm_g, w_in, attn_sinks, w_pool, pool_scale, w_out, mlp_norm_g, w_up, w_down, final_norm_g, loss_target, m_attn_norm_g, m_w_in, m_attn_sinks, m_w_pool, m_pool_scale, m_w_out, m_mlp_norm_g, m_w_up, m_w_down, m_final_norm_g, v_attn_norm_g, v_w_in, v_attn_sinks, v_w_pool, v_pool_scale, v_w_out, v_mlp_norm_g, v_w_up, v_w_down, v_final_norm_g):
    nseq, seq, d = x.shape
    c_idx = lax.axis_index("c").astype(jnp.int32)
    j_idx = (2 * lax.axis_index("x") + lax.axis_index("y")).astype(jnp.int32)
    c_arr = jnp.reshape(c_idx, (1,))
    jc_arr = jnp.stack([j_idx, c_idx])

    big_w = (w_in[0].T, w_out[0], w_up[0], w_down[0])
    big_m = (m_w_in[0].T, m_w_out[0], m_w_up[0], m_w_down[0])
    big_v = (v_w_in[0].T, v_w_out[0], v_w_up[0], v_w_down[0])
    (w_in_lands, w_in_send, w_in_recv), rest = _gather_start(big_w[:1], big_w[1:])
    (w_in4,) = _forward_now(_gather_wait(w_in_lands, w_in_send, w_in_recv, "gather_wait_w_in"), "forward_w_in")
    w_in_full = w_in4.reshape(IN_W, D_MODEL)
    class Comm(_NoComm):
        def attn_done(self, attn):
            arrived = _gather_wait(*rest, "gather_wait_rest", (attn,))
            self.lands, self.send, self.recv, token = _forward_start(arrived)
            return (token,)

        def rest_of_weights(self, pool):
            w_out4, w_up4, w_down4 = _forward_wait(self.lands, self.send, self.recv, pool)
            return w_out4.reshape(D_MODEL, D_MODEL), w_up4, w_down4

        def mlp_grads_ready(self, gw_up4, gw_down4):
            self.mlp = _sibling_start((gw_up4, gw_down4), "mlp_grads_to_sibling")
            return (self.mlp[4],)

        def outproj_done(self, gw_out):
            grads, lands, send, recv, _ = self.mlp
            sib_out = _sibling_start((gw_out.reshape(N_CHIPS, OUT_SHARD, D_MODEL),), "w_out_grad_to_sibling")
            mlp_grads, mlp_from_sib = _sibling_wait(grads, lands, send, recv, sib_out[4], "mlp_grads_to_sibling")
            partials = _chip_partial(mlp_grads, mlp_from_sib, c_arr, "chip_partial_mlp")
            out_grad, out_from_sib = _sibling_wait(*sib_out[:4], partials[1], "w_out_grad_to_sibling")
            partials = tuple(_chip_partial(out_grad, out_from_sib, c_arr, "chip_partial_w_out")) + tuple(partials)
            self.grads = tuple(out_grad) + tuple(mlp_grads)
            self.from_sib = tuple(out_from_sib) + tuple(mlp_from_sib)
            self.parts, self.part_lands, self.part_send, self.part_recv, token = _send_start(partials, "send_start")
            return (token,)

    comm = Comm()
    gx, big_g, small_g = _local_step(
        x.reshape(nseq * seq, d), loss_target.reshape(nseq * seq, d), w_in_full, comm,
        attn_norm_g, attn_sinks.reshape(N_Q_HEADS), w_pool[0], pool_scale, mlp_norm_g, final_norm_g.reshape(1, d), seq)
    big_g = tuple(big_g[:1]) + comm.grads

    def adamw(idx, mines, others, name):
        pick = lambda group: [group[i] for i in idx]
        return _adamw_shard(mines, others, pick(big_w), pick(big_m), pick(big_v), c_arr, name)

    sib_in = _sibling_start(big_g[:1], "w_in_grad_to_sibling")
    received = _send_wait(comm.parts, comm.part_lands, comm.part_send, comm.part_recv, sib_in[4], "send_wait")
    mine = list(_final_half(big_g[1:2], comm.from_sib[:1], received[:1], jc_arr, "final_half_w_out"))
    mine += list(_final_half(big_g[2:], comm.from_sib[1:], received[1:], jc_arr, "final_half_mlp"))
    halves = _sibling_start(mine, "halves_to_sibling", False)
    g_in, r_in = _sibling_wait(*sib_in[:4], halves[4], "w_in_grad_to_sibling")
    send_in = _send_start(_chip_partial(g_in, r_in, c_arr, "chip_partial_w_in"), "send_start_w_in")
    mine, other = _sibling_wait(*halves[:4], send_in[4], "halves_to_sibling", False)
    (out_res,) = adamw((1,), mine[:1], other[:1], "adamw_w_out")
    up_res, down_res = adamw((2, 3), mine[1:], other[1:], "adamw_mlp")
    p_in = _send_wait(*send_in[:4], up_res[0], "send_wait_w_in")
    mine_in = _final_half(g_in, r_in, p_in, jc_arr, "final_half_w_in")
    other_in = _sibling_swap(mine_in, "w_in_half_to_sibling")
    (in_res,) = adamw((0,), mine_in, other_in, "adamw_w_in")
    big_out = [[t.T for t in in_res], out_res, up_res, down_res]

    wp_flat = lambda a: a.reshape(N_POOL * POOL_G, POOL_G)
    small_params = [
        (attn_norm_g, m_attn_norm_g, v_attn_norm_g),
        (attn_sinks, m_attn_sinks, v_attn_sinks),
        (wp_flat(w_pool), wp_flat(m_w_pool), wp_flat(v_w_pool)),
        (pool_scale, m_pool_scale, v_pool_scale),
        (mlp_norm_g, m_mlp_norm_g, v_mlp_norm_g),
        (final_norm_g.reshape(1, d), m_final_norm_g.reshape(1, d), v_final_norm_g.reshape(1, d)),
    ]
    loss, small_out = _small_adamw(_small_allreduce(small_g), small_params)

    weights = (attn_norm_g, w_in, attn_sinks, w_pool, pool_scale, w_out, mlp_norm_g, w_up, w_down, final_norm_g)
    order = [small_out[0], big_out[0], small_out[1], small_out[2], small_out[3], big_out[1], small_out[4], big_out[2],
             big_out[3], small_out[5]]
    outs = [loss.reshape(()), gx.reshape(nseq, seq, d)]
    for kind in range(4):
        outs += [res[kind].reshape(w.shape) for res, w in zip(order, weights)]
    return tuple(outs)
```

```python
import jax
import jax.numpy as jnp
import numpy as np
from jax import lax
from jax.experimental import pallas as pl
from jax.experimental.pallas import tpu as pltpu

F32 = jnp.float32
BF16 = jnp.bfloat16
MXU_DTYPE = jnp.bfloat16

D_MODEL = 1024
HEAD_DIM = 64
N_Q_HEADS = 8
N_KV_HEADS = 2
Q_PER_KV = N_Q_HEADS // N_KV_HEADS
ATTN_W = N_Q_HEADS * HEAD_DIM
KV_W = N_KV_HEADS * HEAD_DIM
BLK = 128
POOL_WINDOWS = (2, 4, 8, 16)
N_POOL = len(POOL_WINDOWS)
POOL_W = D_MODEL - ATTN_W
POOL_G = POOL_W // N_POOL
IN_W = ATTN_W + 2 * KV_W + POOL_W
D_FF = 4 * D_MODEL
EPS = 1e-6
ROPE_THETA = 10000.0
N_CHIPS = 4
IN_SHARD = IN_W // N_CHIPS
OUT_SHARD = D_MODEL // N_CHIPS
FF_SHARD = D_FF // N_CHIPS
LANES = 128

ADAM_LR = 0.001
ADAM_B1 = 0.9
ADAM_B2 = 0.999
ADAM_EPS = 1e-08
ADAM_WD = 0.01
ADAM_STEP = 10

VMEM_LIMIT = 56 * 1024 * 1024
TILE_BIG = 1024
TILE_MLP = 256
MESH = pl.DeviceIdType.MESH


def _cp(**kw):
    return pltpu.CompilerParams(vmem_limit_bytes=VMEM_LIMIT, **kw)


def _mm(a, b):
    return jnp.dot(a.astype(MXU_DTYPE), b.astype(MXU_DTYPE), preferred_element_type=F32)


def _mm_nt(a, b):
    return lax.dot_general(a.astype(MXU_DTYPE), b.astype(MXU_DTYPE), (((1,), (1,)), ((), ())),
                           preferred_element_type=F32)


def _mm_tn(a, b):
    return lax.dot_general(a.astype(MXU_DTYPE), b.astype(MXU_DTYPE), (((0,), (0,)), ((), ())),
                           preferred_element_type=F32)


def _resident(shape):
    nd = len(shape)
    return pl.BlockSpec(shape, lambda *_: (0,) * nd, pipeline_mode=pl.Buffered(1))


def _const(shape):
    nd = len(shape)
    return pl.BlockSpec(shape, lambda *_: (0,) * nd)


def _rope_tables(seq):
    half = HEAD_DIM // 2
    inv_freq = (ROPE_THETA ** (-np.arange(half, dtype=np.float32) / half)).astype(np.float32)
    ang = np.arange(seq, dtype=np.float32)[:, None] * inv_freq[None, :]
    cos, sin = np.cos(ang).astype(np.float32), np.sin(ang).astype(np.float32)
    cos_t = np.concatenate([cos, cos, cos, cos], axis=1)
    sin_t = np.concatenate([-sin, sin, -sin, sin], axis=1)
    return jnp.asarray(cos_t), jnp.asarray(sin_t)


def _swap_halves(xc):
    lane = lax.broadcasted_iota(jnp.int32, xc.shape, 1)
    return jnp.where((lane & 63) < 32, pltpu.roll(xc, 96, 1), pltpu.roll(xc, 32, 1))


_HBM = pl.BlockSpec(memory_space=pltpu.HBM)
_SEM = pl.BlockSpec(memory_space=pltpu.SEMAPHORE)
_ANY = pl.BlockSpec(memory_space=pl.ANY)
_EFFECT = pltpu.SideEffectType.DATAFLOW_SIDE_EFFECTING
TOKEN = jax.ShapeDtypeStruct((8, LANES), F32)


def _other_chips(x, y):
    return [(1 - x, y), (x, 1 - y), (1 - x, 1 - y)]


def _after(tokens):
    tokens = list(tokens)
    return tokens, [_ANY] * len(tokens)


def _gather_start(first, rest):
    shards = tuple(first) + tuple(rest)
    nw, nf = len(shards), len(first)

    def body(*refs):
        srcs, lands = refs[:nw], refs[nw:2 * nw]
        sems = refs[2 * nw:2 * nw + 4]
        stages = refs[2 * nw + 4:3 * nw + 4]
        lsem = refs[3 * nw + 4]
        x, y, c = lax.axis_index("x"), lax.axis_index("y"), lax.axis_index("c")
        j = 2 * x + y
        for k in range(nw):
            send, recv = sems[0:2] if k < nf else sems[2:4]
            slot = k if k < nf else k - nf
            stages[k][...] = srcs[k][...].astype(BF16)
            local = pltpu.make_async_copy(stages[k], lands[k].at[j], lsem.at[k])
            local.start()
            local.wait()
            rows = shards[k].shape[0] // 2
            mine = lands[k].at[j, pl.ds(c * rows, rows)]
            for n, chip in enumerate(_other_chips(x, y)):
                pltpu.make_async_remote_copy(
                    src_ref=mine, dst_ref=mine, send_sem=send.at[slot * 3 + n], recv_sem=recv.at[slot * 3 + n],
                    device_id=(chip[0], chip[1], c), device_id_type=MESH).start()

    vmem = pl.BlockSpec(memory_space=pltpu.VMEM)
    nr = nw - nf
    res = pl.pallas_call(
        body, name="gather_start",
        out_shape=tuple(pltpu.HBM((N_CHIPS,) + s.shape, BF16) for s in shards) + (
            pltpu.SemaphoreType.DMA((3 * nf,)), pltpu.SemaphoreType.DMA((3 * nf,)),
            pltpu.SemaphoreType.DMA((3 * nr,)), pltpu.SemaphoreType.DMA((3 * nr,))),
        in_specs=[vmem] * nw, out_specs=tuple([_HBM] * nw) + (_SEM, _SEM, _SEM, _SEM),
        scratch_shapes=[pltpu.VMEM(s.shape, BF16) for s in shards] + [pltpu.SemaphoreType.DMA((nw,))],
        compiler_params=_cp(has_side_effects=_EFFECT),
    )(*shards)
    return (res[:nf], res[nw], res[nw + 1]), (res[nf:nw], res[nw + 2], res[nw + 3])


def _gather_wait(lands, send, recv, name, after=()):
    nw = len(lands)
    tokens, token_specs = _after(after)

    def body(*refs):
        ins = refs[:nw]
        send_ref, recv_ref = refs[nw], refs[nw + 1]
        x, y, c = lax.axis_index("x"), lax.axis_index("y"), lax.axis_index("c")
        j = 2 * x + y
        for k in range(nw):
            rows = lands[k].shape[1] // 2
            mine = ins[k].at[j, pl.ds(c * rows, rows)]
            for n, chip in enumerate(_other_chips(x, y)):
                got = ins[k].at[2 * chip[0] + chip[1], pl.ds(c * rows, rows)]
                cp = pltpu.make_async_remote_copy(
                    src_ref=mine, dst_ref=got, send_sem=send_ref.at[k * 3 + n], recv_sem=recv_ref.at[k * 3 + n],
                    device_id=(chip[0], chip[1], c), device_id_type=MESH)
                cp.wait_send()
                cp.wait_recv()

    return pl.pallas_call(
        body, name=name, out_shape=tuple(pltpu.HBM(a.shape, a.dtype) for a in lands),
        in_specs=[_HBM] * nw + [_SEM, _SEM] + token_specs, out_specs=tuple([_HBM] * nw),
        input_output_aliases={k: k for k in range(nw)},
        compiler_params=_cp(has_side_effects=_EFFECT),
    )(*lands, send, recv, *tokens)


def _forward_now(lands, name):
    nw = len(lands)
    rows_of = [a.shape[1] // 2 for a in lands]

    def body(*refs):
        mine, theirs = _forward_copies(refs[nw:2 * nw], nw, rows_of, refs[2 * nw], refs[2 * nw + 1])
        for cp in mine:
            cp.start()
        for cp in theirs:
            cp.wait_recv()
        for cp in mine:
            cp.wait_send()

    return pl.pallas_call(
        body, name=name, out_shape=tuple(jax.ShapeDtypeStruct(a.shape, a.dtype) for a in lands),
        in_specs=[_HBM] * nw, out_specs=tuple([_HBM] * nw), input_output_aliases={k: k for k in range(nw)},
        scratch_shapes=[pltpu.SemaphoreType.DMA((3 * nw,)), pltpu.SemaphoreType.DMA((3 * nw,))],
        compiler_params=_cp(),
    )(*lands)


def _forward_copies(refs, nw, rows_of, send, recv):
    x, y, c = lax.axis_index("x"), lax.axis_index("y"), lax.axis_index("c")
    out = []
    for k in range(nw):
        rows = rows_of[k]
        for n, chip in enumerate(_other_chips(x, y)):
            got = refs[k].at[2 * chip[0] + chip[1], pl.ds(c * rows, rows)]
            theirs = refs[k].at[2 * chip[0] + chip[1], pl.ds((1 - c) * rows, rows)]
            out.append(pltpu.make_async_remote_copy(
                src_ref=got, dst_ref=got, send_sem=send.at[k * 3 + n], recv_sem=recv.at[k * 3 + n],
                device_id=(x, y, 1 - c), device_id_type=MESH))
            out.append(pltpu.make_async_remote_copy(
                src_ref=theirs, dst_ref=theirs, send_sem=send.at[k * 3 + n], recv_sem=recv.at[k * 3 + n],
                device_id=(x, y, 1 - c), device_id_type=MESH))
    return out[0::2], out[1::2]


def _forward_start(lands):
    nw = len(lands)
    rows_of = [a.shape[1] // 2 for a in lands]

    def body(*refs):
        send, recv, token = refs[2 * nw:2 * nw + 3]
        mine, _ = _forward_copies(refs[:nw], nw, rows_of, send, recv)
        for cp in mine:
            cp.start()
        token[...] = jnp.zeros_like(token)

    res = pl.pallas_call(
        body, name="forward_start",
        out_shape=tuple(pltpu.HBM(a.shape, a.dtype) for a in lands) + (
            pltpu.SemaphoreType.DMA((3 * nw,)), pltpu.SemaphoreType.DMA((3 * nw,)), TOKEN),
        in_specs=[_HBM] * nw, out_specs=tuple([_HBM] * nw) + (_SEM, _SEM, pl.BlockSpec(memory_space=pltpu.VMEM)),
        input_output_aliases={k: k for k in range(nw)},
        compiler_params=_cp(has_side_effects=_EFFECT),
    )(*lands)
    return res[:nw], res[nw], res[nw + 1], res[nw + 2]


def _forward_wait(lands, send, recv, after):
    nw = len(lands)
    rows_of = [a.shape[1] // 2 for a in lands]

    def body(*refs):
        mine, theirs = _forward_copies(refs[:nw], nw, rows_of, refs[nw], refs[nw + 1])
        for cp in mine:
            cp.wait_send()
        for cp in theirs:
            cp.wait_recv()

    return pl.pallas_call(
        body, name="forward_wait", out_shape=tuple(pltpu.HBM(a.shape, a.dtype) for a in lands),
        in_specs=[_HBM] * nw + [_SEM, _SEM, _ANY], out_specs=tuple([_HBM] * nw),
        input_output_aliases={k: k for k in range(nw)},
        compiler_params=_cp(has_side_effects=_EFFECT),
    )(*lands, send, recv, after)


def _fwd_inproj(x, g1, w_in, cos_t, sin_t, tm):
    t_tok = x.shape[0]
    seq = cos_t.shape[0]
    per_seq = seq // tm

    def body(x_ref, g_ref, w_ref, cos_ref, sin_ref, q_ref, k_ref, v_ref, u_ref):
        xv = x_ref[...]
        r = lax.rsqrt(jnp.mean(xv * xv, axis=-1, keepdims=True) + EPS)
        h = (xv * r) * g_ref[...]
        proj = _mm_nt(h, w_ref[...])
        cos, sin = cos_ref[...], sin_ref[...]
        for cidx in range((ATTN_W + KV_W) // LANES):
            xc = proj[:, cidx * LANES:(cidx + 1) * LANES]
            rot = xc * cos + _swap_halves(xc) * sin
            if cidx < ATTN_W // LANES:
                q_ref[:, cidx * LANES:(cidx + 1) * LANES] = (rot * (HEAD_DIM ** -0.5)).astype(q_ref.dtype)
            else:
                k_ref[...] = rot.astype(k_ref.dtype)
        v_ref[...] = proj[:, ATTN_W + KV_W:ATTN_W + 2 * KV_W].astype(v_ref.dtype)
        u_ref[...] = proj[:, ATTN_W + 2 * KV_W:].astype(u_ref.dtype)

    row = lambda w: pl.BlockSpec((tm, w), lambda i: (i, 0))
    kv = row(KV_W)
    tab = pl.BlockSpec((tm, LANES), lambda i: (i % per_seq, 0))
    kv_shape = jax.ShapeDtypeStruct((t_tok, KV_W), MXU_DTYPE)
    return pl.pallas_call(
        body, name="fwd_inproj", grid=(t_tok // tm,),
        in_specs=[row(D_MODEL), _const((1, D_MODEL)), _resident((IN_W, D_MODEL)), tab, tab],
        out_specs=(row(ATTN_W), kv, kv, row(POOL_W)),
        out_shape=(jax.ShapeDtypeStruct((t_tok, ATTN_W), MXU_DTYPE), kv_shape, kv_shape,
                   jax.ShapeDtypeStruct((t_tok, POOL_W), MXU_DTYPE)),
        compiler_params=_cp(dimension_semantics=("parallel",)),
    )(x, g1, w_in, cos_t, sin_t)


MASKED = -1e30


def _attn_bias():
    b = lax.broadcasted_iota(jnp.int32, (2 * BLK, BLK), 0)
    a = lax.broadcasted_iota(jnp.int32, (2 * BLK, BLK), 1)
    own = (b >= BLK) & (b - BLK <= a)
    prev = (b < BLK) & (b > a)
    return jnp.stack([jnp.where(own, 0.0, MASKED), jnp.where(own | prev, 0.0, MASKED)]).astype(F32)


def _sink_rows(sinks):
    return jnp.repeat(sinks.reshape(N_KV_HEADS, Q_PER_KV), BLK, axis=1).reshape(N_KV_HEADS, 1, Q_PER_KV * BLK)


def _stack_heads(ref, r0, g):
    return jnp.concatenate(
        [ref[pl.ds(r0, BLK), (Q_PER_KV * g + h) * HEAD_DIM:(Q_PER_KV * g + h + 1) * HEAD_DIM] for h in range(Q_PER_KV)],
        axis=0)


def _kv_window(ref, p0, r0, g):
    sl = slice(g * HEAD_DIM, (g + 1) * HEAD_DIM)
    return jnp.concatenate([ref[pl.ds(p0, BLK), sl], ref[pl.ds(r0, BLK), sl]], axis=0)


def _scores_t(k_ref, q_ref, bias, p0, r0, g):
    kk = _kv_window(k_ref, p0, r0, g)
    qs = _stack_heads(q_ref, r0, g)
    st = _mm_nt(kk, qs) + jnp.concatenate([bias] * Q_PER_KV, axis=1)
    return st, kk, qs


def _head_rows(ref, r0, g):
    return jnp.concatenate([ref[pl.ds(Q_PER_KV * g + h, 1), pl.ds(r0, BLK)] for h in range(Q_PER_KV)], axis=1)


def _pairs_to_rows(xt):
    out = []
    for t in range(Q_PER_KV // 2):
        pair = jnp.concatenate([xt[:, (2 * t) * BLK:(2 * t + 1) * BLK], xt[:, (2 * t + 1) * BLK:(2 * t + 2) * BLK]], axis=0)
        out.append(pair.T)
    return out


def _shift_rows(x, k, seq):
    row = lax.broadcasted_iota(jnp.int32, x.shape, 0)
    if k > 0:
        return jnp.where(row >= k, pltpu.roll(x, k, 0), 0.0)
    return jnp.where(row < seq + k, pltpu.roll(x, seq + k, 0), 0.0)


def _window_sum(x, w, seq, forward):
    s, k = x, 1
    while k < w:
        s = s + _shift_rows(s, -k if forward else k, seq)
        k *= 2
    return s


def _inv_count(seq, w):
    pos = lax.broadcasted_iota(jnp.int32, (seq, 1), 0)
    return 1.0 / jnp.minimum(pos + 1, w).astype(F32)


def _fwd_attn(q, k, v, sink_rows, bias, seq):
    t_tok = q.shape[0]
    nblk = seq // BLK
    together = max(f for f in (4, 2, 1) if nblk % f == 0)

    def body(q_ref, k_ref, v_ref, sink_ref, bias_ref, o_ref, lse_ref):
        def blocks(ik, carry):
            units = []
            for s in range(together):
                i = together * ik + s
                r0 = pl.multiple_of(i * BLK, BLK)
                p0 = pl.multiple_of(jnp.maximum(i - 1, 0) * BLK, BLK)
                units += [(g, r0, p0, bias_ref[jnp.minimum(i, 1)]) for g in range(N_KV_HEADS)]
            sts = [_scores_t(k_ref, q_ref, bias, p0, r0, g)[0] for g, r0, p0, bias in units]
            ms = [jnp.maximum(jnp.max(st, axis=0, keepdims=True), sink_ref[u[0]]) for st, u in zip(sts, units)]
            ps = [jnp.exp(st - m).astype(MXU_DTYPE) for st, m in zip(sts, ms)]
            vvs = [_kv_window(v_ref, p0, r0, g) for g, r0, p0, _ in units]
            ots = [_mm_tn(jnp.concatenate([vv, jnp.ones_like(vv)], axis=1), p) for vv, p in zip(vvs, ps)]
            for (g, r0, _, _), m, ot in zip(units, ms, ots):
                den = ot[HEAD_DIM:HEAD_DIM + 1] + jnp.exp(sink_ref[g] - m)
                lse = m + jnp.log(den)
                for h in range(Q_PER_KV):
                    lse_ref[pl.ds(Q_PER_KV * g + h, 1), pl.ds(r0, BLK)] = lse[:, h * BLK:(h + 1) * BLK]
                for t, tile in enumerate(_pairs_to_rows(ot[:HEAD_DIM] * (1.0 / den))):
                    c0 = (Q_PER_KV * g + 2 * t) * HEAD_DIM
                    o_ref[pl.ds(r0, BLK), c0:c0 + LANES] = tile.astype(o_ref.dtype)
            return carry

        lax.fori_loop(0, nblk // together, blocks, 0)

    row = lambda w: pl.BlockSpec((seq, w), lambda i: (i, 0))
    kv = row(KV_W)
    return pl.pallas_call(
        body, name="fwd_attn", grid=(t_tok // seq,),
        in_specs=[row(ATTN_W), kv, kv, _const((N_KV_HEADS, 1, Q_PER_KV * BLK)), _const((2, 2 * BLK, BLK))],
        out_specs=(row(ATTN_W), pl.BlockSpec((N_Q_HEADS, seq), lambda i: (0, i))),
        out_shape=(jax.ShapeDtypeStruct((t_tok, ATTN_W), MXU_DTYPE), jax.ShapeDtypeStruct((N_Q_HEADS, t_tok), F32)),
        compiler_params=_cp(dimension_semantics=("parallel",)),
    )(q, k, v, sink_rows, bias)


def _fwd_pool(u, w_pool, pool_scale, seq, after=()):
    t_tok = u.shape[0]
    tokens, token_specs = _after(after)

    def body(u_ref, wp_ref, sc_ref, *rest):
        o_ref = rest[-1]
        for gi, w in enumerate(POOL_WINDOWS):
            sl = slice(gi * POOL_G, (gi + 1) * POOL_G)
            ug = u_ref[:, sl].astype(F32)
            d = _window_sum(ug, w, seq, False) * _inv_count(seq, w) - ug
            o_ref[:, sl] = (_mm(d, wp_ref[gi]) * sc_ref[:, sl]).astype(o_ref.dtype)

    row = pl.BlockSpec((seq, POOL_W), lambda i: (i, 0))
    return pl.pallas_call(
        body, name="fwd_pool", grid=(t_tok // seq,),
        in_specs=[row, _const((N_POOL, POOL_G, POOL_G)), _const((1, POOL_W))] + token_specs, out_specs=row,
        out_shape=jax.ShapeDtypeStruct((t_tok, POOL_W), MXU_DTYPE),
        compiler_params=_cp(dimension_semantics=("parallel",)),
    )(u, w_pool, pool_scale, *tokens)


def _rms_bwd(dy_g, xn, r):
    return r * (dy_g - xn * jnp.mean(dy_g * xn, axis=-1, keepdims=True))


def _mlp_fwd_bwd(x, attn, pool, target, w_out, w_up4, w_down4, g2, gf, tm):
    t_tok = x.shape[0]

    def body(x_ref, attn_ref, pool_ref, tgt_ref, wo_ref, wu_ref, wd_ref, g2_ref, gf_ref,
             h2_ref, a_ref, da_ref, dx2b_ref, dx1_ref, dx1b_ref, loss_ref, dgf_ref, dg2_ref):
        @pl.when(pl.program_id(0) == 0)
        def _():
            loss_ref[...] = jnp.zeros_like(loss_ref)
            dgf_ref[...] = jnp.zeros_like(dgf_ref)
            dg2_ref[...] = jnp.zeros_like(dg2_ref)

        x1 = x_ref[...] + (_mm(attn_ref[...], wo_ref[:ATTN_W]) + _mm(pool_ref[...], wo_ref[ATTN_W:]))
        r2 = lax.rsqrt(jnp.mean(x1 * x1, axis=-1, keepdims=True) + EPS)
        xn1 = x1 * r2
        g2 = g2_ref[...]
        h2 = (xn1 * g2).astype(MXU_DTYPE)
        h2_ref[...] = h2
        acc = jnp.zeros((tm, D_MODEL), F32)
        for j in range(N_CHIPS):
            a = _mm(h2, wu_ref[j])
            a_ref[:, j * FF_SHARD:(j + 1) * FF_SHARD] = a.astype(a_ref.dtype)
            acc = acc + _mm(jnp.square(jnp.maximum(a, 0.0)), wd_ref[j])
        x2 = x1 + acc
        r3 = lax.rsqrt(jnp.mean(x2 * x2, axis=-1, keepdims=True) + EPS)
        xn2 = x2 * r3
        gf_v = gf_ref[...]
        err = xn2 * gf_v - tgt_ref[...]
        part = jnp.sum(err * err) * (0.5 / D_MODEL)
        first = (lax.broadcasted_iota(jnp.int32, loss_ref.shape, 0) == 0) & (lax.broadcasted_iota(jnp.int32, loss_ref.shape, 1) == 0)
        loss_ref[...] += jnp.where(first, part, 0.0)
        dy = err * (1.0 / D_MODEL)
        dgf_ref[...] += jnp.sum(dy * xn2, axis=0, keepdims=True)
        dx2 = _rms_bwd(dy * gf_v, xn2, r3)
        dx2b = dx2.astype(MXU_DTYPE)
        dx2b_ref[...] = dx2b
        dh2 = jnp.zeros((tm, D_MODEL), F32)
        for j in range(N_CHIPS):
            sl = slice(j * FF_SHARD, (j + 1) * FF_SHARD)
            dhid = _mm_nt(dx2b, wd_ref[j])
            da = (dhid * (2.0 * jnp.maximum(a_ref[:, sl].astype(F32), 0.0))).astype(MXU_DTYPE)
            da_ref[:, sl] = da
            dh2 = dh2 + _mm_nt(da, wu_ref[j])
        dg2_ref[...] += jnp.sum(dh2 * xn1, axis=0, keepdims=True)
        dx1 = dx2 + _rms_bwd(dh2 * g2, xn1, r2)
        dx1_ref[...] = dx1
        dx1b_ref[...] = dx1.astype(dx1b_ref.dtype)

    row = lambda w: pl.BlockSpec((tm, w), lambda i: (i, 0))
    vec = jax.ShapeDtypeStruct((1, D_MODEL), F32)
    return pl.pallas_call(
        body, name="mlp_fwd_bwd", grid=(t_tok // tm,),
        in_specs=[row(D_MODEL), row(ATTN_W), row(POOL_W), row(D_MODEL), _resident((D_MODEL, D_MODEL)),
                  _resident((N_CHIPS, D_MODEL, FF_SHARD)), _resident((N_CHIPS, FF_SHARD, D_MODEL)),
                  _const((1, D_MODEL)), _const((1, D_MODEL))],
        out_specs=(row(D_MODEL), row(D_FF), row(D_FF), row(D_MODEL), row(D_MODEL), row(D_MODEL),
                   _const((8, LANES)), _const((1, D_MODEL)), _const((1, D_MODEL))),
        out_shape=(jax.ShapeDtypeStruct((t_tok, D_MODEL), MXU_DTYPE), jax.ShapeDtypeStruct((t_tok, D_FF), MXU_DTYPE),
                   jax.ShapeDtypeStruct((t_tok, D_FF), MXU_DTYPE), jax.ShapeDtypeStruct((t_tok, D_MODEL), MXU_DTYPE),
                   jax.ShapeDtypeStruct((t_tok, D_MODEL), F32), jax.ShapeDtypeStruct((t_tok, D_MODEL), MXU_DTYPE),
                   jax.ShapeDtypeStruct((8, LANES), F32), vec, vec),
        compiler_params=_cp(dimension_semantics=("arbitrary",)),
    )(x, attn, pool, target, w_out, w_up4, w_down4, g2, gf)


def _bwd_mlp_wgrads(h2, da, a, dx2b, tk):
    t_tok = h2.shape[0]

    def body(h2_ref, da_ref, a_ref, dx2b_ref, gup_ref, gdn_ref):
        @pl.when(pl.program_id(1) == 0)
        def _():
            gup_ref[...] = jnp.zeros_like(gup_ref)
            gdn_ref[...] = jnp.zeros_like(gdn_ref)

        gup_ref[0] += _mm_tn(h2_ref[...], da_ref[...])
        hid = jnp.square(jnp.maximum(a_ref[...].astype(F32), 0.0))
        gdn_ref[0] += _mm_tn(hid, dx2b_ref[...])

    tok = pl.BlockSpec((tk, D_MODEL), lambda j, t: (t, 0))
    ffb = pl.BlockSpec((tk, FF_SHARD), lambda j, t: (t, j))
    wblk = pl.BlockSpec((1, D_MODEL, D_MODEL), lambda j, t: (j, 0, 0))
    return pl.pallas_call(
        body, name="bwd_mlp_wgrads", grid=(N_CHIPS, t_tok // tk),
        in_specs=[tok, ffb, ffb, tok], out_specs=(wblk, wblk),
        out_shape=(jax.ShapeDtypeStruct((N_CHIPS, D_MODEL, FF_SHARD), F32), jax.ShapeDtypeStruct((N_CHIPS, FF_SHARD, D_MODEL), F32)),
        compiler_params=_cp(dimension_semantics=("parallel", "arbitrary")),
    )(h2, da, a, dx2b)


def _head_selector():
    ch = lax.broadcasted_iota(jnp.int32, (ATTN_W, LANES), 0)
    col = lax.broadcasted_iota(jnp.int32, (ATTN_W, LANES), 1)
    return (ch // HEAD_DIM == col).astype(MXU_DTYPE)


def _bwd_outproj(dx1b, attn, pool, w_out, head_sel, tm, after=()):
    t_tok = dx1b.shape[0]
    tokens, token_specs = _after(after)

    def body(dx_ref, attn_ref, pool_ref, wo_ref, sel_ref, *rest):
        dattn_ref, dpool_ref, delta_ref, gwo_ref = rest[len(tokens):]

        @pl.when(pl.program_id(0) == 0)
        def _():
            gwo_ref[...] = jnp.zeros_like(gwo_ref)

        dx = dx_ref[...]
        attn = attn_ref[...]
        dattn = _mm_nt(dx, wo_ref[:ATTN_W])
        dattn_ref[...] = dattn.astype(dattn_ref.dtype)
        dpool_ref[...] = _mm_nt(dx, wo_ref[ATTN_W:]).astype(dpool_ref.dtype)
        prod = dattn * attn.astype(F32)
        hi = prod.astype(MXU_DTYPE)
        lo = prod - hi.astype(F32)
        delta = _mm(hi, sel_ref[...]) + _mm(lo, sel_ref[...])
        delta_ref[...] = delta.T[:N_Q_HEADS]
        gwo_ref[:ATTN_W] += _mm_tn(attn, dx)
        gwo_ref[ATTN_W:] += _mm_tn(pool_ref[...], dx)

    row = lambda w: pl.BlockSpec((tm, w), lambda i: (i, 0))
    return pl.pallas_call(
        body, name="bwd_outproj", grid=(t_tok // tm,),
        in_specs=[row(D_MODEL), row(ATTN_W), row(POOL_W), _resident((D_MODEL, D_MODEL)), _const((ATTN_W, LANES))] + token_specs,
        out_specs=(row(ATTN_W), row(POOL_W), pl.BlockSpec((N_Q_HEADS, tm), lambda i: (0, i)), _const((D_MODEL, D_MODEL))),
        out_shape=(jax.ShapeDtypeStruct((t_tok, ATTN_W), MXU_DTYPE), jax.ShapeDtypeStruct((t_tok, POOL_W), MXU_DTYPE),
                   jax.ShapeDtypeStruct((N_Q_HEADS, t_tok), F32), jax.ShapeDtypeStruct((D_MODEL, D_MODEL), F32)),
        compiler_params=_cp(dimension_semantics=("arbitrary",)),
    )(dx1b, attn, pool, w_out, head_sel, *tokens)


def _bwd_attn(q, k, v, dattn, lse, delta, sink_rows, bias, cos_t, sin_t, seq, after=()):
    t_tok = q.shape[0]
    nblk = seq // BLK
    together = max(f for f in (4, 2, 1) if nblk % f == 0)
    qkv_w = ATTN_W + 2 * KV_W
    tokens, token_specs = _after(after)

    def unrope(d, cos, sin):
        return d * cos - _swap_halves(d) * sin

    def body(q_ref, k_ref, v_ref, do_ref, lse_ref, delta_ref, sink_ref, bias_ref, cos_ref, sin_ref, *rest):
        dqkv_ref, dsink_ref, dk_acc, dv_acc = rest[len(tokens):]

        @pl.when(pl.program_id(0) == 0)
        def _():
            dsink_ref[...] = jnp.zeros_like(dsink_ref)

        dk_acc[...] = jnp.zeros_like(dk_acc)
        dv_acc[...] = jnp.zeros_like(dv_acc)

        def blocks(ik, dsink):
            units = []
            for s in range(together):
                i = together * ik + s
                r0 = pl.multiple_of(i * BLK, BLK)
                p0 = pl.multiple_of(jnp.maximum(i - 1, 0) * BLK, BLK)
                units += [(g, r0, p0, bias_ref[jnp.minimum(i, 1)]) for g in range(N_KV_HEADS)]
            scores = [_scores_t(k_ref, q_ref, bias, p0, r0, g) for g, r0, p0, bias in units]
            doss = [_stack_heads(do_ref, r0, g) for g, r0, _, _ in units]
            dpts = [_mm_nt(_kv_window(v_ref, p0, r0, g), dos) for (g, r0, p0, _), dos in zip(units, doss)]
            lses = [_head_rows(lse_ref, r0, g) for g, r0, _, _ in units]
            deltas = [_head_rows(delta_ref, r0, g) for g, r0, _, _ in units]
            pns = [jnp.exp(sc[0] - lse) for sc, lse in zip(scores, lses)]
            dsts = [pn * (dpt - delta) for pn, dpt, delta in zip(pns, dpts, deltas)]
            dsink = list(dsink)
            for (g, _, _, _), lse, delta in zip(units, lses, deltas):
                dsink[g] = dsink[g] - jnp.exp(sink_ref[g] - lse) * delta
            dqts = [_mm_tn(sc[1], dst) * (HEAD_DIM ** -0.5) for sc, dst in zip(scores, dsts)]
            dkks = [_mm(dst, sc[2]) for sc, dst in zip(scores, dsts)]
            dvvs = [_mm(pn, dos) for pn, dos in zip(pns, doss)]
            for (g, r0, p0, _), dqt, dkk, dvv in zip(units, dqts, dkks, dvvs):
                cos, sin = cos_ref[pl.ds(r0, BLK), :], sin_ref[pl.ds(r0, BLK), :]
                for t, tile in enumerate(_pairs_to_rows(dqt)):
                    c0 = (Q_PER_KV * g + 2 * t) * HEAD_DIM
                    dqkv_ref[pl.ds(r0, BLK), c0:c0 + LANES] = unrope(tile, cos, sin).astype(dqkv_ref.dtype)
                dk_acc[g, pl.ds(p0, BLK), :] += dkk[:BLK]
                dk_acc[g, pl.ds(r0, BLK), :] += dkk[BLK:]
                dv_acc[g, pl.ds(p0, BLK), :] += dvv[:BLK]
                dv_acc[g, pl.ds(r0, BLK), :] += dvv[BLK:]
            return tuple(dsink)

        zero = jnp.zeros((1, Q_PER_KV * BLK), F32)
        dsink = lax.fori_loop(0, nblk // together, blocks, (zero,) * N_KV_HEADS)
        lane = lax.broadcasted_iota(jnp.int32, dsink_ref.shape, 1)
        row = lax.broadcasted_iota(jnp.int32, dsink_ref.shape, 0)
        tile = jnp.zeros(dsink_ref.shape, F32)
        for g in range(N_KV_HEADS):
            for h in range(Q_PER_KV):
                tot = jnp.sum(dsink[g][:, h * BLK:(h + 1) * BLK])
                tile = tile + jnp.where((row == 0) & (lane == Q_PER_KV * g + h), tot, 0.0)
        dsink_ref[...] += tile
        dk = jnp.concatenate([dk_acc[g] for g in range(N_KV_HEADS)], axis=1)
        dqkv_ref[:, ATTN_W:ATTN_W + KV_W] = unrope(dk, cos_ref[...], sin_ref[...]).astype(dqkv_ref.dtype)
        dqkv_ref[:, ATTN_W + KV_W:] = jnp.concatenate([dv_acc[g] for g in range(N_KV_HEADS)], axis=1).astype(dqkv_ref.dtype)

    row = lambda w: pl.BlockSpec((seq, w), lambda i: (i, 0))
    kv = row(KV_W)
    per_head = pl.BlockSpec((N_Q_HEADS, seq), lambda i: (0, i))
    return pl.pallas_call(
        body, name="bwd_attn", grid=(t_tok // seq,),
        in_specs=[row(ATTN_W), kv, kv, row(ATTN_W), per_head, per_head, _const((N_KV_HEADS, 1, Q_PER_KV * BLK)),
                  _const((2, 2 * BLK, BLK)), _resident((seq, LANES)), _resident((seq, LANES))] + token_specs,
        out_specs=(row(qkv_w), _const((8, LANES))),
        out_shape=(jax.ShapeDtypeStruct((t_tok, qkv_w), MXU_DTYPE), jax.ShapeDtypeStruct((8, LANES), F32)),
        scratch_shapes=[pltpu.VMEM((N_KV_HEADS, seq, HEAD_DIM), F32), pltpu.VMEM((N_KV_HEADS, seq, HEAD_DIM), F32)],
        compiler_params=_cp(dimension_semantics=("arbitrary",)),
    )(q, k, v, dattn, lse, delta, sink_rows, bias, cos_t, sin_t, *tokens)


def _bwd_pool(u, dpool, w_pool, pool_scale, seq):
    t_tok = u.shape[0]

    def body(u_ref, dp_ref, wp_ref, sc_ref, du_ref, dwp_ref, dsc_ref):
        @pl.when(pl.program_id(0) == 0)
        def _():
            dwp_ref[...] = jnp.zeros_like(dwp_ref)
            dsc_ref[...] = jnp.zeros_like(dsc_ref)

        for gi, w in enumerate(POOL_WINDOWS):
            sl = slice(gi * POOL_G, (gi + 1) * POOL_G)
            ug = u_ref[:, sl].astype(F32)
            inv = _inv_count(seq, w)
            d = (_window_sum(ug, w, seq, False) * inv - ug).astype(MXU_DTYPE)
            y = _mm(d, wp_ref[gi])
            dpool = dp_ref[:, sl].astype(F32)
            dsc_ref[:, sl] += jnp.sum(y * dpool, axis=0, keepdims=True)
            dy = (dpool * sc_ref[:, sl]).astype(MXU_DTYPE)
            dwp_ref[gi] += _mm_tn(d, dy)
            dd = _mm_nt(dy, wp_ref[gi])
            du_ref[:, sl] = (_window_sum(dd * inv, w, seq, True) - dd).astype(du_ref.dtype)

    row = pl.BlockSpec((seq, POOL_W), lambda i: (i, 0))
    return pl.pallas_call(
        body, name="bwd_pool", grid=(t_tok // seq,),
        in_specs=[row, row, _const((N_POOL, POOL_G, POOL_G)), _const((1, POOL_W))],
        out_specs=(row, _const((N_POOL, POOL_G, POOL_G)), _const((1, POOL_W))),
        out_shape=(jax.ShapeDtypeStruct((t_tok, POOL_W), MXU_DTYPE), jax.ShapeDtypeStruct((N_POOL, POOL_G, POOL_G), F32),
                   jax.ShapeDtypeStruct((1, POOL_W), F32)),
        compiler_params=_cp(dimension_semantics=("arbitrary",)),
    )(u, dpool, w_pool, pool_scale)


def _bwd_inproj(dqkv, du, x, dx1, w_in, g1, tm):
    t_tok = x.shape[0]
    nsteps = t_tok // tm

    qkv_w = ATTN_W + 2 * KV_W

    def body(dqkv_ref, du_ref, x_ref, dx1_ref, w_ref, g_ref, gx_ref, gw_ref, dg_ref):
        @pl.when(pl.program_id(0) == 0)
        def _():
            gw_ref[...] = jnp.zeros_like(gw_ref)
            dg_ref[...] = jnp.zeros_like(dg_ref)

        dqkv, du = dqkv_ref[...], du_ref[...]
        xv = x_ref[...]
        r = lax.rsqrt(jnp.mean(xv * xv, axis=-1, keepdims=True) + EPS)
        xn = xv * r
        g = g_ref[...]
        dh = _mm(dqkv, w_ref[:qkv_w]) + _mm(du, w_ref[qkv_w:])
        dg_ref[...] += jnp.sum(dh * xn, axis=0, keepdims=True)
        gx_ref[...] = dx1_ref[...] + _rms_bwd(dh * g, xn, r)
        h = (xn * g).astype(MXU_DTYPE)
        gw_ref[:qkv_w] += _mm_tn(dqkv, h)
        gw_ref[qkv_w:] += _mm_tn(du, h)

    row = lambda w: pl.BlockSpec((tm, w), lambda i: (i, 0))
    return pl.pallas_call(
        body, name="bwd_inproj", grid=(nsteps,),
        in_specs=[row(qkv_w), row(POOL_W), row(D_MODEL), row(D_MODEL), _resident((IN_W, D_MODEL)), _const((1, D_MODEL))],
        out_specs=(row(D_MODEL), _const((IN_W, D_MODEL)), _const((1, D_MODEL))),
        out_shape=(jax.ShapeDtypeStruct((t_tok, D_MODEL), F32), jax.ShapeDtypeStruct((IN_W, D_MODEL), F32),
                   jax.ShapeDtypeStruct((1, D_MODEL), F32)),
        compiler_params=_cp(dimension_semantics=("arbitrary",)),
    )(dqkv, du, x, dx1, w_in, g1)


class _NoComm:
    def attn_done(self, attn):
        return ()

    def rest_of_weights(self, pool):
        raise NotImplementedError

    def mlp_grads_ready(self, gw_up4, gw_down4):
        return ()

    def outproj_done(self, gw_out):
        return ()


def _local_step(x, target, w_in, comm, g1, sinks, w_pool, pool_scale, g2, gf, seq):
    tm_big, tm_mlp = min(TILE_BIG, seq), min(TILE_MLP, seq)
    cos_t, sin_t = _rope_tables(seq)
    sink_rows, bias = _sink_rows(sinks), _attn_bias()
    q, k, v, u = _fwd_inproj(x, g1, w_in, cos_t, sin_t, tm_big)
    attn, lse = _fwd_attn(q, k, v, sink_rows, bias, seq)
    pool = _fwd_pool(u, w_pool, pool_scale, seq, comm.attn_done(attn))
    w_out, w_up4, w_down4 = comm.rest_of_weights(pool)
    h2, a, da, dx2b, dx1, dx1b, loss, dgf, dg2 = _mlp_fwd_bwd(x, attn, pool, target, w_out, w_up4, w_down4, g2, gf, tm_mlp)
    gw_up4, gw_down4 = _bwd_mlp_wgrads(h2, da, a, dx2b, min(TILE_BIG, x.shape[0]))
    dattn, dpool, delta, gw_out = _bwd_outproj(dx1b, attn, pool, w_out, _head_selector(), tm_big,
                                               comm.mlp_grads_ready(gw_up4, gw_down4))
    dqkv, dsinks = _bwd_attn(q, k, v, dattn, lse, delta, sink_rows, bias, cos_t, sin_t, seq, comm.outproj_done(gw_out))
    du, dwp, dsc = _bwd_pool(u, dpool, w_pool, pool_scale, seq)
    gx, gw_in_t, dg1 = _bwd_inproj(dqkv, du, x, dx1, w_in, g1, tm_big)
    big = (gw_in_t.reshape(N_CHIPS, IN_SHARD, D_MODEL), gw_out.reshape(N_CHIPS, OUT_SHARD, D_MODEL), gw_up4, gw_down4)
    small = (dg1, dsinks, dwp.reshape(N_POOL * POOL_G, POOL_G), dsc, dg2, dgf, loss)
    return gx, big, small


def _sibling_swap(arrs, name):
    n = len(arrs)

    def body(*refs):
        cps = _sibling_copies(refs[:n], refs[n:2 * n], refs[2 * n], refs[2 * n + 1], False)
        for cp in cps:
            cp.start()
        for cp in cps:
            cp.wait()

    return pl.pallas_call(
        body, name=name, out_shape=tuple(jax.ShapeDtypeStruct(a.shape, a.dtype) for a in arrs),
        in_specs=[_HBM] * n, out_specs=tuple([_HBM] * n),
        scratch_shapes=[pltpu.SemaphoreType.DMA((n,)), pltpu.SemaphoreType.DMA((n,))],
        compiler_params=_cp(),
    )(*arrs)


def _sibling_copies(srcs, lands, send, recv, pick_half):
    x, y, c = lax.axis_index("x"), lax.axis_index("y"), lax.axis_index("c")
    cps = []
    for i in range(len(srcs)):
        src = srcs[i]
        if pick_half:
            h = src.shape[1] // 2
            src = src.at[:, pl.ds((1 - c) * h, h)]
        cps.append(pltpu.make_async_remote_copy(
            src_ref=src, dst_ref=lands[i], send_sem=send.at[i], recv_sem=recv.at[i],
            device_id=(x, y, 1 - c), device_id_type=MESH))
    return cps


def _sibling_start(arrs, name, pick_half=True):
    n = len(arrs)

    def body(*refs):
        send, recv, token = refs[3 * n:3 * n + 3]
        for cp in _sibling_copies(refs[:n], refs[2 * n:3 * n], send, recv, pick_half):
            cp.start()
        token[...] = jnp.zeros_like(token)

    half = lambda a: ((a.shape[0], a.shape[1] // 2) + a.shape[2:]) if pick_half else a.shape
    res = pl.pallas_call(
        body, name=name + "_start",
        out_shape=tuple(pltpu.HBM(a.shape, a.dtype) for a in arrs) + tuple(pltpu.HBM(half(a), a.dtype) for a in arrs) + (
            pltpu.SemaphoreType.DMA((n,)), pltpu.SemaphoreType.DMA((n,)), TOKEN),
        in_specs=[_HBM] * n, out_specs=tuple([_HBM] * (2 * n)) + (_SEM, _SEM, pl.BlockSpec(memory_space=pltpu.VMEM)),
        input_output_aliases={i: i for i in range(n)},
        compiler_params=_cp(has_side_effects=_EFFECT),
    )(*[pltpu.with_memory_space_constraint(a, pltpu.HBM) for a in arrs])
    return res[:n], res[n:2 * n], res[2 * n], res[2 * n + 1], res[2 * n + 2]


def _sibling_wait(arrs, lands, send, recv, after, name, pick_half=True):
    n = len(arrs)

    def body(*refs):
        for cp in _sibling_copies(refs[:n], refs[n:2 * n], refs[2 * n], refs[2 * n + 1], pick_half):
            cp.wait_send()
            cp.wait_recv()

    res = pl.pallas_call(
        body, name=name + "_wait", out_shape=tuple(pltpu.HBM(a.shape, a.dtype) for a in list(arrs) + list(lands)),
        in_specs=[_HBM] * (2 * n) + [_SEM, _SEM, _ANY], out_specs=tuple([_HBM] * (2 * n)),
        input_output_aliases={i: i for i in range(2 * n)},
        compiler_params=_cp(has_side_effects=_EFFECT),
    )(*arrs, *lands, send, recv, after)
    return res[:n], res[n:]


def _row_block(rows):
    if rows <= 256:
        return rows
    for cand in (256, 128, 64, 32, 16, 8):
        if rows % cand == 0:
            return cand
    raise ValueError(rows)


def _chip_partial(g4s, r4s, c_arr, name):
    n = len(g4s)
    _, rows, cols = r4s[0].shape
    rb = _row_block(rows)
    nb = rows // rb

    def body(c_ref, *refs):
        for k in range(n):
            refs[2 * n + k][...] = (refs[k][...] + refs[n + k][...]).astype(BF16)

    own = pl.BlockSpec((1, rb, cols), lambda s, i, c: (s, c[0] * nb + i, 0))
    blk = pl.BlockSpec((1, rb, cols), lambda s, i, c: (s, i, 0))
    return pl.pallas_call(
        body, name=name,
        grid_spec=pltpu.PrefetchScalarGridSpec(num_scalar_prefetch=1, grid=(N_CHIPS, nb),
                                               in_specs=[own] * n + [blk] * n, out_specs=tuple([blk] * n)),
        out_shape=tuple(jax.ShapeDtypeStruct(r.shape, BF16) for r in r4s),
        compiler_params=_cp(dimension_semantics=("parallel", "parallel")),
    )(c_arr, *g4s, *r4s)


def _send_start(parts, name):
    n = len(parts)

    def body(*refs):
        srcs, lands = refs[:n], refs[2 * n:3 * n]
        send, recv, token = refs[3 * n:3 * n + 3]
        x, y, c = lax.axis_index("x"), lax.axis_index("y"), lax.axis_index("c")
        for i in range(n):
            for m, chip in enumerate(_other_chips(x, y)):
                pltpu.make_async_remote_copy(
                    src_ref=srcs[i].at[2 * chip[0] + chip[1]], dst_ref=lands[i].at[m],
                    send_sem=send.at[3 * i + m], recv_sem=recv.at[3 * i + m],
                    device_id=(chip[0], chip[1], c), device_id_type=MESH).start()
        token[...] = jnp.zeros_like(token)

    res = pl.pallas_call(
        body, name=name,
        out_shape=tuple(pltpu.HBM(p.shape, p.dtype) for p in parts) + tuple(pltpu.HBM((3,) + p.shape[1:], p.dtype) for p in parts) + (
            pltpu.SemaphoreType.DMA((3 * n,)), pltpu.SemaphoreType.DMA((3 * n,)), TOKEN),
        in_specs=[_HBM] * n, out_specs=tuple([_HBM] * (2 * n)) + (_SEM, _SEM, pl.BlockSpec(memory_space=pltpu.VMEM)),
        input_output_aliases={i: i for i in range(n)},
        compiler_params=_cp(has_side_effects=_EFFECT),
    )(*[pltpu.with_memory_space_constraint(p, pltpu.HBM) for p in parts])
    return res[:n], res[n:2 * n], res[2 * n], res[2 * n + 1], res[2 * n + 2]


def _send_wait(parts, lands, send, recv, after, name):
    n = len(parts)

    def body(*refs):
        srcs, ins = refs[:n], refs[n:2 * n]
        send_ref, recv_ref = refs[2 * n], refs[2 * n + 1]
        x, y, c = lax.axis_index("x"), lax.axis_index("y"), lax.axis_index("c")
        for i in range(n):
            for m, chip in enumerate(_other_chips(x, y)):
                cp = pltpu.make_async_remote_copy(
                    src_ref=srcs[i].at[2 * chip[0] + chip[1]], dst_ref=ins[i].at[m],
                    send_sem=send_ref.at[3 * i + m], recv_sem=recv_ref.at[3 * i + m],
                    device_id=(chip[0], chip[1], c), device_id_type=MESH)
                cp.wait_send()
                cp.wait_recv()

    res = pl.pallas_call(
        body, name=name, out_shape=tuple(pltpu.HBM(a.shape, a.dtype) for a in list(parts) + list(lands)),
        in_specs=[_HBM] * (2 * n) + [_SEM, _SEM, _ANY], out_specs=tuple([_HBM] * (2 * n)),
        input_output_aliases={i: i for i in range(2 * n)},
        compiler_params=_cp(has_side_effects=_EFFECT),
    )(*parts, *lands, send, recv, after)
    return res[n:]


def _final_half(g4s, r4s, got3s, jc_arr, name):
    n = len(g4s)
    _, rows, cols = r4s[0].shape
    rb = _row_block(rows)
    nb = rows // rb

    def body(jc_ref, *refs):
        for k in range(n):
            g_ref, r_ref, p_ref, o_ref = refs[k], refs[n + k], refs[2 * n + k], refs[3 * n + k]
            own = g_ref[0] + r_ref[0]
            o_ref[...] = ((own + p_ref[0].astype(F32)) + p_ref[1].astype(F32)) + p_ref[2].astype(F32)

    own = pl.BlockSpec((1, rb, cols), lambda i, jc: (jc[0], jc[1] * nb + i, 0))
    sib = pl.BlockSpec((1, rb, cols), lambda i, jc: (jc[0], i, 0))
    got = pl.BlockSpec((3, rb, cols), lambda i, jc: (0, i, 0))
    out = pl.BlockSpec((rb, cols), lambda i, jc: (i, 0))
    return pl.pallas_call(
        body, name=name,
        grid_spec=pltpu.PrefetchScalarGridSpec(num_scalar_prefetch=1, grid=(nb,),
                                               in_specs=[own] * n + [sib] * n + [got] * n, out_specs=tuple([out] * n)),
        out_shape=tuple(jax.ShapeDtypeStruct((rows, cols), F32) for _ in range(n)),
        compiler_params=_cp(dimension_semantics=("parallel",)),
    )(jc_arr, *g4s, *r4s, *got3s)


def _adamw_math(w, g, m, v):
    m2 = ADAM_B1 * m + (1.0 - ADAM_B1) * g
    v2 = ADAM_B2 * v + (1.0 - ADAM_B2) * (g * g)
    m_hat = m2 / (1.0 - ADAM_B1 ** ADAM_STEP)
    v_hat = v2 / (1.0 - ADAM_B2 ** ADAM_STEP)
    delta = -ADAM_LR * (m_hat / (jnp.sqrt(v_hat) + ADAM_EPS) + ADAM_WD * w)
    return delta, m2, v2


def _adamw_shard(mines, others, ws, ms, vs, c_arr, name):
    n = len(ws)
    rows, cols = ws[0].shape
    half = rows // 2
    rb = _row_block(half)
    nb = half // rb

    def body(c_ref, *refs):
        for k in range(n):
            a_ref, b_ref, w_ref, m_ref, v_ref = (refs[i * n + k] for i in range(5))
            g_ref, d_ref, m2_ref, v2_ref = refs[5 * n + 4 * k:5 * n + 4 * k + 4]
            g = jnp.where(pl.program_id(0) == c_ref[0], a_ref[...], b_ref[...])
            delta, m2, v2 = _adamw_math(w_ref[...], g, m_ref[...], v_ref[...])
            g_ref[...] = g
            d_ref[...] = delta
            m2_ref[...] = m2
            v2_ref[...] = v2

    hb = pl.BlockSpec((rb, cols), lambda h, i, c: (i, 0))
    fb = pl.BlockSpec((rb, cols), lambda h, i, c: (h * nb + i, 0))
    shp = jax.ShapeDtypeStruct((rows, cols), F32)
    res = pl.pallas_call(
        body, name=name,
        grid_spec=pltpu.PrefetchScalarGridSpec(num_scalar_prefetch=1, grid=(2, nb),
                                               in_specs=[hb] * (2 * n) + [fb] * (3 * n), out_specs=tuple([fb] * (4 * n))),
        out_shape=tuple([shp] * (4 * n)),
        compiler_params=_cp(dimension_semantics=("parallel", "parallel")),
    )(c_arr, *mines, *others, *ws, *ms, *vs)
    return [res[4 * k:4 * k + 4] for k in range(n)]


def _small_allreduce(parts):
    n = len(parts)

    def body(*refs):
        p_refs, accs = refs[:n], refs[n:2 * n]
        bufs = refs[2 * n:3 * n]
        send, recv = refs[3 * n:]
        x, y, c = lax.axis_index("x"), lax.axis_index("y"), lax.axis_index("c")
        partners = [(x, y, 1 - c), (1 - x, y, c), (x, 1 - y, c)]
        other_way = [partners[0], partners[2], partners[1]]
        for i in range(n):
            accs[i][...] = p_refs[i][...]
        for s in range(3):
            cps = []
            for i in range(n):
                rows = parts[i].shape[0]
                pieces = [(slice(None), partners[s])] if rows < 2 * BLK else [
                    (pl.ds(0, rows // 2), partners[s]), (pl.ds(rows // 2, rows // 2), other_way[s])]
                for h, (sl, partner) in enumerate(pieces):
                    cp = pltpu.make_async_remote_copy(
                        src_ref=accs[i].at[sl], dst_ref=bufs[i].at[s, sl], send_sem=send.at[6 * i + 2 * s + h],
                        recv_sem=recv.at[6 * i + 2 * s + h], device_id=partner, device_id_type=MESH)
                    cp.start()
                    cps.append(cp)
            for cp in cps:
                cp.wait()
            for i in range(n):
                accs[i][...] = accs[i][...] + bufs[i][s]

    vmem = pl.BlockSpec(memory_space=pltpu.VMEM)
    return pl.pallas_call(
        body, name="small_allreduce", out_shape=tuple(jax.ShapeDtypeStruct(p.shape, F32) for p in parts),
        in_specs=[vmem] * n, out_specs=tuple([vmem] * n),
        scratch_shapes=[pltpu.VMEM((3,) + p.shape, F32) for p in parts] + [
            pltpu.SemaphoreType.DMA((6 * n,)), pltpu.SemaphoreType.DMA((6 * n,))],
        compiler_params=_cp(),
    )(*parts)


def _small_adamw(reduced, params):
    n = len(reduced)
    n_w = len(params)

    def body(*refs):
        r_refs = refs[:n]
        wmv = refs[n:n + 3 * n_w]
        outs = refs[n + 3 * n_w:]
        outs[0][...] = r_refs[n - 1][0:1, 0:1]
        grads = [r_refs[0][...], r_refs[1][0:1, 0:N_Q_HEADS]] + [r_refs[i][...] for i in range(2, n_w)]
        for i in range(n_w):
            w_ref, m_ref, v_ref = wmv[3 * i:3 * i + 3]
            g_ref, d_ref, m2_ref, v2_ref = outs[1 + 4 * i:5 + 4 * i]
            delta, m2, v2 = _adamw_math(w_ref[...], grads[i], m_ref[...], v_ref[...])
            g_ref[...] = grads[i]
            d_ref[...] = delta
            m2_ref[...] = m2
            v2_ref[...] = v2

    flat = [a for p in params for a in p]
    vmem = pl.BlockSpec(memory_space=pltpu.VMEM)
    out_shape = [jax.ShapeDtypeStruct((1, 1), F32)]
    for p in params:
        out_shape += [jax.ShapeDtypeStruct(p[0].shape, F32)] * 4
    res = pl.pallas_call(
        body, name="small_adamw", out_shape=tuple(out_shape),
        in_specs=[vmem] * (n + len(flat)), out_specs=tuple([vmem] * len(out_shape)),
        compiler_params=_cp(),
    )(*reduced, *flat)
    return res[0], [res[1 + 4 * i:5 + 4 * i] for i in range(n_w)]


def kernel(x, attn_norm_g, w_in, attn_sinks, w_pool, pool_scale, w_out, mlp_norm_g, w_up, w_down, final_norm_g, loss_target, m_attn_norm_g, m_w_in, m_attn_sinks, m_w_pool, m_pool_scale, m_w_out, m_mlp_norm_g, m_w_up, m_w_down, m_final_norm_g, v_attn_norm_g, v_w_in, v_attn_sinks, v_w_pool, v_pool_scale, v_w_out, v_mlp_norm_g, v_w_up, v_w_down, v_final_norm_g):
    nseq, seq, d = x.shape
    c_idx = lax.axis_index("c").astype(jnp.int32)
    j_idx = (2 * lax.axis_index("x") + lax.axis_index("y")).astype(jnp.int32)
    c_arr = jnp.reshape(c_idx, (1,))
    jc_arr = jnp.stack([j_idx, c_idx])

    big_w = (w_in[0].T, w_out[0], w_up[0], w_down[0])
    big_m = (m_w_in[0].T, m_w_out[0], m_w_up[0], m_w_down[0])
    big_v = (v_w_in[0].T, v_w_out[0], v_w_up[0], v_w_down[0])
    (w_in_lands, w_in_send, w_in_recv), rest = _gather_start(big_w[:1], big_w[1:])
    (w_in4,) = _forward_now(_gather_wait(w_in_lands, w_in_send, w_in_recv, "gather_wait_w_in"), "forward_w_in")
    w_in_full = w_in4.reshape(IN_W, D_MODEL)
    class Comm(_NoComm):
        def attn_done(self, attn):
            arrived = _gather_wait(*rest, "gather_wait_rest", (attn,))
            self.lands, self.send, self.recv, token = _forward_start(arrived)
            return (token,)

        def rest_of_weights(self, pool):
            w_out4, w_up4, w_down4 = _forward_wait(self.lands, self.send, self.recv, pool)
            return w_out4.reshape(D_MODEL, D_MODEL), w_up4, w_down4

        def mlp_grads_ready(self, gw_up4, gw_down4):
            self.mlp = _sibling_start((gw_up4, gw_down4), "mlp_grads_to_sibling")
            return (self.mlp[4],)

        def outproj_done(self, gw_out):
            grads, lands, send, recv, _ = self.mlp
            sib_out = _sibling_start((gw_out.reshape(N_CHIPS, OUT_SHARD, D_MODEL),), "w_out_grad_to_sibling")
            mlp_grads, mlp_from_sib = _sibling_wait(grads, lands, send, recv, sib_out[4], "mlp_grads_to_sibling")
            partials = _chip_partial(mlp_grads, mlp_from_sib, c_arr, "chip_partial_mlp")
            out_grad, out_from_sib = _sibling_wait(*sib_out[:4], partials[1], "w_out_grad_to_sibling")
            partials = tuple(_chip_partial(out_grad, out_from_sib, c_arr, "chip_partial_w_out")) + tuple(partials)
            self.grads = tuple(out_grad) + tuple(mlp_grads)
            self.from_sib = tuple(out_from_sib) + tuple(mlp_from_sib)
            self.parts, self.part_lands, self.part_send, self.part_recv, token = _send_start(partials, "send_start")
            return (token,)

    comm = Comm()
    gx, big_g, small_g = _local_step(
        x.reshape(nseq * seq, d), loss_target.reshape(nseq * seq, d), w_in_full, comm,
        attn_norm_g, attn_sinks.reshape(N_Q_HEADS), w_pool[0], pool_scale, mlp_norm_g, final_norm_g.reshape(1, d), seq)
    big_g = tuple(big_g[:1]) + comm.grads

    def adamw(idx, mines, others, name):
        pick = lambda group: [group[i] for i in idx]
        return _adamw_shard(mines, others, pick(big_w), pick(big_m), pick(big_v), c_arr, name)

    sib_in = _sibling_start(big_g[:1], "w_in_grad_to_sibling")
    received = _send_wait(comm.parts, comm.part_lands, comm.part_send, comm.part_recv, sib_in[4], "send_wait")
    mine = list(_final_half(big_g[1:2], comm.from_sib[:1], received[:1], jc_arr, "final_half_w_out"))
    mine += list(_final_half(big_g[2:], comm.from_sib[1:], received[1:], jc_arr, "final_half_mlp"))
    halves = _sibling_start(mine, "halves_to_sibling", False)
    g_in, r_in = _sibling_wait(*sib_in[:4], halves[4], "w_in_grad_to_sibling")
    send_in = _send_start(_chip_partial(g_in, r_in, c_arr, "chip_partial_w_in"), "send_start_w_in")
    mine, other = _sibling_wait(*halves[:4], send_in[4], "halves_to_sibling", False)
    (out_res,) = adamw((1,), mine[:1], other[:1], "adamw_w_out")
    up_res, down_res = adamw((2, 3), mine[1:], other[1:], "adamw_mlp")
    p_in = _send_wait(*send_in[:4], up_res[0], "send_wait_w_in")
    mine_in = _final_half(g_in, r_in, p_in, jc_arr, "final_half_w_in")
    other_in = _sibling_swap(mine_in, "w_in_half_to_sibling")
    (in_res,) = adamw((0,), mine_in, other_in, "adamw_w_in")
    big_out = [[t.T for t in in_res], out_res, up_res, down_res]

    wp_flat = lambda a: a.reshape(N_POOL * POOL_G, POOL_G)
    small_params = [
        (attn_norm_g, m_attn_norm_g, v_attn_norm_g),
        (attn_sinks, m_attn_sinks, v_attn_sinks),
        (wp_flat(w_pool), wp_flat(m_w_pool), wp_flat(v_w_pool)),
        (pool_scale, m_pool_scale, v_pool_scale),
        (mlp_norm_g, m_mlp_norm_g, v_mlp_norm_g),
        (final_norm_g.reshape(1, d), m_final_norm_g.reshape(1, d), v_final_norm_g.reshape(1, d)),
    ]
    loss, small_out = _small_adamw(_small_allreduce(small_g), small_params)

    weights = (attn_norm_g, w_in, attn_sinks, w_pool, pool_scale, w_out, mlp_norm_g, w_up, w_down, final_norm_g)
    order = [small_out[0], big_out[0], small_out[1], small_out[2], small_out[3], big_out[1], small_out[4], big_out[2],
             big_out[3], small_out[5]]
    outs = [loss.reshape(()), gx.reshape(nseq, seq, d)]
    for kind in range(4):
        outs += [res[kind].reshape(w.shape) for res, w in zip(order, weights)]
    return tuple(outs)
```

```python
import jax
import jax.numpy as jnp
import numpy as np
from jax import lax
from jax.experimental import pallas as pl
from jax.experimental.pallas import tpu as pltpu

F32 = jnp.float32
BF16 = jnp.bfloat16
MXU_DTYPE = jnp.bfloat16

D_MODEL = 1024
HEAD_DIM = 64
N_Q_HEADS = 8
N_KV_HEADS = 2
Q_PER_KV = N_Q_HEADS // N_KV_HEADS
ATTN_W = N_Q_HEADS * HEAD_DIM
KV_W = N_KV_HEADS * HEAD_DIM
BLK = 128
POOL_WINDOWS = (2, 4, 8, 16)
N_POOL = len(POOL_WINDOWS)
POOL_W = D_MODEL - ATTN_W
POOL_G = POOL_W // N_POOL
IN_W = ATTN_W + 2 * KV_W + POOL_W
D_FF = 4 * D_MODEL
EPS = 1e-6
ROPE_THETA = 10000.0
N_CHIPS = 4
IN_SHARD = IN_W // N_CHIPS
OUT_SHARD = D_MODEL // N_CHIPS
FF_SHARD = D_FF // N_CHIPS
LANES = 128

ADAM_LR = 0.001
ADAM_B1 = 0.9
ADAM_B2 = 0.999
ADAM_EPS = 1e-08
ADAM_WD = 0.01
ADAM_STEP = 10

VMEM_LIMIT = 56 * 1024 * 1024
TILE_BIG = 1024
TILE_MLP = 256
MESH = pl.DeviceIdType.MESH


def _cp(**kw):
    return pltpu.CompilerParams(vmem_limit_bytes=VMEM_LIMIT, **kw)


def _mm(a, b):
    return jnp.dot(a.astype(MXU_DTYPE), b.astype(MXU_DTYPE), preferred_element_type=F32)


def _mm_nt(a, b):
    return lax.dot_general(a.astype(MXU_DTYPE), b.astype(MXU_DTYPE), (((1,), (1,)), ((), ())),
                           preferred_element_type=F32)


def _mm_tn(a, b):
    return lax.dot_general(a.astype(MXU_DTYPE), b.astype(MXU_DTYPE), (((0,), (0,)), ((), ())),
                           preferred_element_type=F32)


def _resident(shape):
    nd = len(shape)
    return pl.BlockSpec(shape, lambda *_: (0,) * nd, pipeline_mode=pl.Buffered(1))


def _const(shape):
    nd = len(shape)
    return pl.BlockSpec(shape, lambda *_: (0,) * nd)


def _rope_tables(seq):
    half = HEAD_DIM // 2
    inv_freq = (ROPE_THETA ** (-np.arange(half, dtype=np.float32) / half)).astype(np.float32)
    ang = np.arange(seq, dtype=np.float32)[:, None] * inv_freq[None, :]
    cos, sin = np.cos(ang).astype(np.float32), np.sin(ang).astype(np.float32)
    cos_t = np.concatenate([cos, cos, cos, cos], axis=1)
    sin_t = np.concatenate([-sin, sin, -sin, sin], axis=1)
    return jnp.asarray(cos_t), jnp.asarray(sin_t)


def _swap_halves(xc):
    lane = lax.broadcasted_iota(jnp.int32, xc.shape, 1)
    return jnp.where((lane & 63) < 32, pltpu.roll(xc, 96, 1), pltpu.roll(xc, 32, 1))


_HBM = pl.BlockSpec(memory_space=pltpu.HBM)
_SEM = pl.BlockSpec(memory_space=pltpu.SEMAPHORE)
_ANY = pl.BlockSpec(memory_space=pl.ANY)
_EFFECT = pltpu.SideEffectType.DATAFLOW_SIDE_EFFECTING
TOKEN = jax.ShapeDtypeStruct((8, LANES), F32)


def _other_chips(x, y):
    return [(1 - x, y), (x, 1 - y), (1 - x, 1 - y)]


def _after(tokens):
    tokens = list(tokens)
    return tokens, [_ANY] * len(tokens)


def _gather_start(first, rest):
    shards = tuple(first) + tuple(rest)
    nw, nf = len(shards), len(first)

    def body(*refs):
        srcs, lands = refs[:nw], refs[nw:2 * nw]
        sems = refs[2 * nw:2 * nw + 4]
        stages = refs[2 * nw + 4:3 * nw + 4]
        lsem = refs[3 * nw + 4]
        x, y, c = lax.axis_index("x"), lax.axis_index("y"), lax.axis_index("c")
        j = 2 * x + y
        for k in range(nw):
            send, recv = sems[0:2] if k < nf else sems[2:4]
            chips = _other_chips(x, y) if k < nf else _other_chips(x, y)[:2]
            base = k * 3 if k < nf else (k - nf) * 2
            stages[k][...] = srcs[k][...].astype(BF16)
            local = pltpu.make_async_copy(stages[k], lands[k].at[j], lsem.at[k])
            local.start()
            local.wait()
            rows = shards[k].shape[0] // 2
            mine = lands[k].at[j, pl.ds(c * rows, rows)]
            for n, chip in enumerate(chips):
                pltpu.make_async_remote_copy(
                    src_ref=mine, dst_ref=mine, send_sem=send.at[base + n], recv_sem=recv.at[base + n],
                    device_id=(chip[0], chip[1], c), device_id_type=MESH).start()

    vmem = pl.BlockSpec(memory_space=pltpu.VMEM)
    nr = nw - nf
    res = pl.pallas_call(
        body, name="gather_start",
        out_shape=tuple(pltpu.HBM((N_CHIPS,) + s.shape, BF16) for s in shards) + (
            pltpu.SemaphoreType.DMA((3 * nf,)), pltpu.SemaphoreType.DMA((3 * nf,)),
            pltpu.SemaphoreType.DMA((2 * nr,)), pltpu.SemaphoreType.DMA((2 * nr,))),
        in_specs=[vmem] * nw, out_specs=tuple([_HBM] * nw) + (_SEM, _SEM, _SEM, _SEM),
        scratch_shapes=[pltpu.VMEM(s.shape, BF16) for s in shards] + [pltpu.SemaphoreType.DMA((nw,))],
        compiler_params=_cp(has_side_effects=_EFFECT),
    )(*shards)
    return (res[:nf], res[nw], res[nw + 1]), (res[nf:nw], res[nw + 2], res[nw + 3])


def _gather_wait(lands, send, recv, name, after=()):
    nw = len(lands)
    tokens, token_specs = _after(after)

    def body(*refs):
        ins = refs[:nw]
        send_ref, recv_ref = refs[nw], refs[nw + 1]
        x, y, c = lax.axis_index("x"), lax.axis_index("y"), lax.axis_index("c")
        j = 2 * x + y
        for k in range(nw):
            rows = lands[k].shape[1] // 2
            mine = ins[k].at[j, pl.ds(c * rows, rows)]
            for n, chip in enumerate(_other_chips(x, y)):
                got = ins[k].at[2 * chip[0] + chip[1], pl.ds(c * rows, rows)]
                cp = pltpu.make_async_remote_copy(
                    src_ref=mine, dst_ref=got, send_sem=send_ref.at[k * 3 + n], recv_sem=recv_ref.at[k * 3 + n],
                    device_id=(chip[0], chip[1], c), device_id_type=MESH)
                cp.wait_send()
                cp.wait_recv()

    return pl.pallas_call(
        body, name=name, out_shape=tuple(pltpu.HBM(a.shape, a.dtype) for a in lands),
        in_specs=[_HBM] * nw + [_SEM, _SEM] + token_specs, out_specs=tuple([_HBM] * nw),
        input_output_aliases={k: k for k in range(nw)},
        compiler_params=_cp(has_side_effects=_EFFECT),
    )(*lands, send, recv, *tokens)


def _relay_copies(refs, nw, rows_of, send, recv):
    x, y, c = lax.axis_index("x"), lax.axis_index("y"), lax.axis_index("c")
    xn, yn, dg = _other_chips(x, y)
    mine, theirs = [], []
    for k in range(nw):
        rows = rows_of[k]
        for n, (origin, target) in enumerate(((xn, yn), (yn, xn))):
            part = pl.ds(c * rows + n * (rows // 2), rows // 2)
            sent, got = refs[k].at[2 * origin[0] + origin[1], part], refs[k].at[2 * dg[0] + dg[1], part]
            mine.append(pltpu.make_async_remote_copy(
                src_ref=sent, dst_ref=sent, send_sem=send.at[k * 2 + n], recv_sem=recv.at[k * 2 + n],
                device_id=(target[0], target[1], c), device_id_type=MESH))
            theirs.append(pltpu.make_async_remote_copy(
                src_ref=got, dst_ref=got, send_sem=send.at[k * 2 + n], recv_sem=recv.at[k * 2 + n],
                device_id=(target[0], target[1], c), device_id_type=MESH))
    return mine, theirs


def _gather_relay(lands, send, recv, after):
    nw = len(lands)
    rows_of = [a.shape[1] // 2 for a in lands]

    def body(*refs):
        ins, send1, recv1 = refs[:nw], refs[nw], refs[nw + 1]
        send2, recv2, token = refs[2 * nw + 3:2 * nw + 6]
        x, y, c = lax.axis_index("x"), lax.axis_index("y"), lax.axis_index("c")
        j = 2 * x + y
        for k in range(nw):
            mine = ins[k].at[j, pl.ds(c * rows_of[k], rows_of[k])]
            for n, chip in enumerate(_other_chips(x, y)[:2]):
                got = ins[k].at[2 * chip[0] + chip[1], pl.ds(c * rows_of[k], rows_of[k])]
                cp = pltpu.make_async_remote_copy(
                    src_ref=mine, dst_ref=got, send_sem=send1.at[k * 2 + n], recv_sem=recv1.at[k * 2 + n],
                    device_id=(chip[0], chip[1], c), device_id_type=MESH)
                cp.wait_send()
                cp.wait_recv()
        for cp in _relay_copies(ins, nw, rows_of, send2, recv2)[0]:
            cp.start()
        token[...] = jnp.zeros_like(token)

    res = pl.pallas_call(
        body, name="gather_relay",
        out_shape=tuple(pltpu.HBM(a.shape, a.dtype) for a in lands) + (
            pltpu.SemaphoreType.DMA((2 * nw,)), pltpu.SemaphoreType.DMA((2 * nw,)), TOKEN),
        in_specs=[_HBM] * nw + [_SEM, _SEM, _ANY],
        out_specs=tuple([_HBM] * nw) + (_SEM, _SEM, pl.BlockSpec(memory_space=pltpu.VMEM)),
        input_output_aliases={k: k for k in range(nw)},
        compiler_params=_cp(has_side_effects=_EFFECT),
    )(*lands, send, recv, after)
    return res[:nw], res[nw], res[nw + 1], res[nw + 2]


def _gather_wait_relayed(lands, send, recv, after):
    nw = len(lands)
    rows_of = [a.shape[1] // 2 for a in lands]

    def body(*refs):
        mine, theirs = _relay_copies(refs[:nw], nw, rows_of, refs[nw], refs[nw + 1])
        for cp in mine:
            cp.wait_send()
        for cp in theirs:
            cp.wait_recv()

    return pl.pallas_call(
        body, name="gather_wait_relayed", out_shape=tuple(pltpu.HBM(a.shape, a.dtype) for a in lands),
        in_specs=[_HBM] * nw + [_SEM, _SEM, _ANY], out_specs=tuple([_HBM] * nw),
        input_output_aliases={k: k for k in range(nw)},
        compiler_params=_cp(has_side_effects=_EFFECT),
    )(*lands, send, recv, after)


def _forward_now(lands, name):
    nw = len(lands)
    rows_of = [a.shape[1] // 2 for a in lands]

    def body(*refs):
        mine, theirs = _forward_copies(refs[nw:2 * nw], nw, rows_of, refs[2 * nw], refs[2 * nw + 1])
        for cp in mine:
            cp.start()
        for cp in theirs:
            cp.wait_recv()
        for cp in mine:
            cp.wait_send()

    return pl.pallas_call(
        body, name=name, out_shape=tuple(jax.ShapeDtypeStruct(a.shape, a.dtype) for a in lands),
        in_specs=[_HBM] * nw, out_specs=tuple([_HBM] * nw), input_output_aliases={k: k for k in range(nw)},
        scratch_shapes=[pltpu.SemaphoreType.DMA((3 * nw,)), pltpu.SemaphoreType.DMA((3 * nw,))],
        compiler_params=_cp(),
    )(*lands)


def _forward_copies(refs, nw, rows_of, send, recv):
    x, y, c = lax.axis_index("x"), lax.axis_index("y"), lax.axis_index("c")
    out = []
    for k in range(nw):
        rows = rows_of[k]
        for n, chip in enumerate(_other_chips(x, y)):
            got = refs[k].at[2 * chip[0] + chip[1], pl.ds(c * rows, rows)]
            theirs = refs[k].at[2 * chip[0] + chip[1], pl.ds((1 - c) * rows, rows)]
            out.append(pltpu.make_async_remote_copy(
                src_ref=got, dst_ref=got, send_sem=send.at[k * 3 + n], recv_sem=recv.at[k * 3 + n],
                device_id=(x, y, 1 - c), device_id_type=MESH))
            out.append(pltpu.make_async_remote_copy(
                src_ref=theirs, dst_ref=theirs, send_sem=send.at[k * 3 + n], recv_sem=recv.at[k * 3 + n],
                device_id=(x, y, 1 - c), device_id_type=MESH))
    return out[0::2], out[1::2]


def _forward_start(lands):
    nw = len(lands)
    rows_of = [a.shape[1] // 2 for a in lands]

    def body(*refs):
        send, recv, token = refs[2 * nw:2 * nw + 3]
        mine, _ = _forward_copies(refs[:nw], nw, rows_of, send, recv)
        for cp in mine:
            cp.start()
        token[...] = jnp.zeros_like(token)

    res = pl.pallas_call(
        body, name="forward_start",
        out_shape=tuple(pltpu.HBM(a.shape, a.dtype) for a in lands) + (
            pltpu.SemaphoreType.DMA((3 * nw,)), pltpu.SemaphoreType.DMA((3 * nw,)), TOKEN),
        in_specs=[_HBM] * nw, out_specs=tuple([_HBM] * nw) + (_SEM, _SEM, pl.BlockSpec(memory_space=pltpu.VMEM)),
        input_output_aliases={k: k for k in range(nw)},
        compiler_params=_cp(has_side_effects=_EFFECT),
    )(*lands)
    return res[:nw], res[nw], res[nw + 1], res[nw + 2]


def _forward_wait(lands, send, recv, after):
    nw = len(lands)
    rows_of = [a.shape[1] // 2 for a in lands]

    def body(*refs):
        mine, theirs = _forward_copies(refs[:nw], nw, rows_of, refs[nw], refs[nw + 1])
        for cp in mine:
            cp.wait_send()
        for cp in theirs:
            cp.wait_recv()

    return pl.pallas_call(
        body, name="forward_wait", out_shape=tuple(pltpu.HBM(a.shape, a.dtype) for a in lands),
        in_specs=[_HBM] * nw + [_SEM, _SEM, _ANY], out_specs=tuple([_HBM] * nw),
        input_output_aliases={k: k for k in range(nw)},
        compiler_params=_cp(has_side_effects=_EFFECT),
    )(*lands, send, recv, after)


def _fwd_inproj(x, g1, w_in, cos_t, sin_t, tm):
    t_tok = x.shape[0]
    seq = cos_t.shape[0]
    per_seq = seq // tm

    def body(x_ref, g_ref, w_ref, cos_ref, sin_ref, q_ref, k_ref, v_ref, u_ref):
        xv = x_ref[...]
        r = lax.rsqrt(jnp.mean(xv * xv, axis=-1, keepdims=True) + EPS)
        h = (xv * r) * g_ref[...]
        proj = _mm_nt(h, w_ref[...])
        cos, sin = cos_ref[...], sin_ref[...]
        for cidx in range((ATTN_W + KV_W) // LANES):
            xc = proj[:, cidx * LANES:(cidx + 1) * LANES]
            rot = xc * cos + _swap_halves(xc) * sin
            if cidx < ATTN_W // LANES:
                q_ref[:, cidx * LANES:(cidx + 1) * LANES] = (rot * (HEAD_DIM ** -0.5)).astype(q_ref.dtype)
            else:
                k_ref[...] = rot.astype(k_ref.dtype)
        v_ref[...] = proj[:, ATTN_W + KV_W:ATTN_W + 2 * KV_W].astype(v_ref.dtype)
        u_ref[...] = proj[:, ATTN_W + 2 * KV_W:].astype(u_ref.dtype)

    row = lambda w: pl.BlockSpec((tm, w), lambda i: (i, 0))
    kv = row(KV_W)
    tab = pl.BlockSpec((tm, LANES), lambda i: (i % per_seq, 0))
    kv_shape = jax.ShapeDtypeStruct((t_tok, KV_W), MXU_DTYPE)
    return pl.pallas_call(
        body, name="fwd_inproj", grid=(t_tok // tm,),
        in_specs=[row(D_MODEL), _const((1, D_MODEL)), _resident((IN_W, D_MODEL)), tab, tab],
        out_specs=(row(ATTN_W), kv, kv, row(POOL_W)),
        out_shape=(jax.ShapeDtypeStruct((t_tok, ATTN_W), MXU_DTYPE), kv_shape, kv_shape,
                   jax.ShapeDtypeStruct((t_tok, POOL_W), MXU_DTYPE)),
        compiler_params=_cp(dimension_semantics=("parallel",)),
    )(x, g1, w_in, cos_t, sin_t)


MASKED = -1e30


def _attn_bias():
    b = lax.broadcasted_iota(jnp.int32, (2 * BLK, BLK), 0)
    a = lax.broadcasted_iota(jnp.int32, (2 * BLK, BLK), 1)
    own = (b >= BLK) & (b - BLK <= a)
    prev = (b < BLK) & (b > a)
    return jnp.stack([jnp.where(own, 0.0, MASKED), jnp.where(own | prev, 0.0, MASKED)]).astype(F32)


def _sink_rows(sinks):
    return jnp.repeat(sinks.reshape(N_KV_HEADS, Q_PER_KV), BLK, axis=1).reshape(N_KV_HEADS, 1, Q_PER_KV * BLK)


def _stack_heads(ref, r0, g):
    return jnp.concatenate(
        [ref[pl.ds(r0, BLK), (Q_PER_KV * g + h) * HEAD_DIM:(Q_PER_KV * g + h + 1) * HEAD_DIM] for h in range(Q_PER_KV)],
        axis=0)


def _kv_window(ref, p0, r0, g):
    sl = slice(g * HEAD_DIM, (g + 1) * HEAD_DIM)
    return jnp.concatenate([ref[pl.ds(p0, BLK), sl], ref[pl.ds(r0, BLK), sl]], axis=0)


def _scores_t(k_ref, q_ref, bias, p0, r0, g):
    kk = _kv_window(k_ref, p0, r0, g)
    qs = _stack_heads(q_ref, r0, g)
    st = _mm_nt(kk, qs) + jnp.concatenate([bias] * Q_PER_KV, axis=1)
    return st, kk, qs


def _head_rows(ref, r0, g):
    return jnp.concatenate([ref[pl.ds(Q_PER_KV * g + h, 1), pl.ds(r0, BLK)] for h in range(Q_PER_KV)], axis=1)


def _pairs_to_rows(xt):
    out = []
    for t in range(Q_PER_KV // 2):
        pair = jnp.concatenate([xt[:, (2 * t) * BLK:(2 * t + 1) * BLK], xt[:, (2 * t + 1) * BLK:(2 * t + 2) * BLK]], axis=0)
        out.append(pair.T)
    return out


def _shift_rows(x, k, seq):
    row = lax.broadcasted_iota(jnp.int32, x.shape, 0)
    if k > 0:
        return jnp.where(row >= k, pltpu.roll(x, k, 0), 0.0)
    return jnp.where(row < seq + k, pltpu.roll(x, seq + k, 0), 0.0)


def _window_sum(x, w, seq, forward):
    s, k = x, 1
    while k < w:
        s = s + _shift_rows(s, -k if forward else k, seq)
        k *= 2
    return s


def _inv_count(seq, w):
    pos = lax.broadcasted_iota(jnp.int32, (seq, 1), 0)
    return 1.0 / jnp.minimum(pos + 1, w).astype(F32)


def _fwd_attn(q, k, v, sink_rows, bias, seq, after=()):
    t_tok = q.shape[0]
    nblk = seq // BLK
    together = max(f for f in (4, 2, 1) if nblk % f == 0)
    tokens, token_specs = _after(after)

    def body(q_ref, k_ref, v_ref, sink_ref, bias_ref, *rest):
        o_ref, lse_ref = rest[len(tokens):]

        def blocks(ik, carry):
            units = []
            for s in range(together):
                i = together * ik + s
                r0 = pl.multiple_of(i * BLK, BLK)
                p0 = pl.multiple_of(jnp.maximum(i - 1, 0) * BLK, BLK)
                units += [(g, r0, p0, bias_ref[jnp.minimum(i, 1)]) for g in range(N_KV_HEADS)]
            sts = [_scores_t(k_ref, q_ref, bias, p0, r0, g)[0] for g, r0, p0, bias in units]
            ms = [jnp.maximum(jnp.max(st, axis=0, keepdims=True), sink_ref[u[0]]) for st, u in zip(sts, units)]
            ps = [jnp.exp(st - m).astype(MXU_DTYPE) for st, m in zip(sts, ms)]
            vvs = [_kv_window(v_ref, p0, r0, g) for g, r0, p0, _ in units]
            ots = [_mm_tn(jnp.concatenate([vv, jnp.ones_like(vv)], axis=1), p) for vv, p in zip(vvs, ps)]
            for (g, r0, _, _), m, ot in zip(units, ms, ots):
                den = ot[HEAD_DIM:HEAD_DIM + 1] + jnp.exp(sink_ref[g] - m)
                lse = m + jnp.log(den)
                for h in range(Q_PER_KV):
                    lse_ref[pl.ds(Q_PER_KV * g + h, 1), pl.ds(r0, BLK)] = lse[:, h * BLK:(h + 1) * BLK]
                for t, tile in enumerate(_pairs_to_rows(ot[:HEAD_DIM] * (1.0 / den))):
                    c0 = (Q_PER_KV * g + 2 * t) * HEAD_DIM
                    o_ref[pl.ds(r0, BLK), c0:c0 + LANES] = tile.astype(o_ref.dtype)
            return carry

        lax.fori_loop(0, nblk // together, blocks, 0)

    row = lambda w: pl.BlockSpec((seq, w), lambda i: (i, 0))
    kv = row(KV_W)
    return pl.pallas_call(
        body, name="fwd_attn", grid=(t_tok // seq,),
        in_specs=[row(ATTN_W), kv, kv, _const((N_KV_HEADS, 1, Q_PER_KV * BLK)), _const((2, 2 * BLK, BLK))] + token_specs,
        out_specs=(row(ATTN_W), pl.BlockSpec((N_Q_HEADS, seq), lambda i: (0, i))),
        out_shape=(jax.ShapeDtypeStruct((t_tok, ATTN_W), MXU_DTYPE), jax.ShapeDtypeStruct((N_Q_HEADS, t_tok), F32)),
        compiler_params=_cp(dimension_semantics=("parallel",)),
    )(q, k, v, sink_rows, bias, *tokens)


def _fwd_pool(u, w_pool, pool_scale, seq, after=()):
    t_tok = u.shape[0]
    tokens, token_specs = _after(after)

    def body(u_ref, wp_ref, sc_ref, *rest):
        o_ref = rest[-1]
        for gi, w in enumerate(POOL_WINDOWS):
            sl = slice(gi * POOL_G, (gi + 1) * POOL_G)
            ug = u_ref[:, sl].astype(F32)
            d = _window_sum(ug, w, seq, False) * _inv_count(seq, w) - ug
            o_ref[:, sl] = (_mm(d, wp_ref[gi]) * sc_ref[:, sl]).astype(o_ref.dtype)

    row = pl.BlockSpec((seq, POOL_W), lambda i: (i, 0))
    return pl.pallas_call(
        body, name="fwd_pool", grid=(t_tok // seq,),
        in_specs=[row, _const((N_POOL, POOL_G, POOL_G)), _const((1, POOL_W))] + token_specs, out_specs=row,
        out_shape=jax.ShapeDtypeStruct((t_tok, POOL_W), MXU_DTYPE),
        compiler_params=_cp(dimension_semantics=("parallel",)),
    )(u, w_pool, pool_scale, *tokens)


def _rms_bwd(dy_g, xn, r):
    return r * (dy_g - xn * jnp.mean(dy_g * xn, axis=-1, keepdims=True))


def _mlp_fwd_bwd(x, attn, pool, target, w_out, w_up4, w_down4, g2, gf, tm):
    t_tok = x.shape[0]

    def body(x_ref, attn_ref, pool_ref, tgt_ref, wo_ref, wu_ref, wd_ref, g2_ref, gf_ref,
             h2_ref, a_ref, da_ref, dx2b_ref, dx1_ref, dx1b_ref, loss_ref, dgf_ref, dg2_ref):
        @pl.when(pl.program_id(0) == 0)
        def _():
            loss_ref[...] = jnp.zeros_like(loss_ref)
            dgf_ref[...] = jnp.zeros_like(dgf_ref)
            dg2_ref[...] = jnp.zeros_like(dg2_ref)

        x1 = x_ref[...] + (_mm(attn_ref[...], wo_ref[:ATTN_W]) + _mm(pool_ref[...], wo_ref[ATTN_W:]))
        r2 = lax.rsqrt(jnp.mean(x1 * x1, axis=-1, keepdims=True) + EPS)
        xn1 = x1 * r2
        g2 = g2_ref[...]
        h2 = (xn1 * g2).astype(MXU_DTYPE)
        h2_ref[...] = h2
        acc = jnp.zeros((tm, D_MODEL), F32)
        for j in range(N_CHIPS):
            a = _mm(h2, wu_ref[j])
            a_ref[:, j * FF_SHARD:(j + 1) * FF_SHARD] = a.astype(a_ref.dtype)
            acc = acc + _mm(jnp.square(jnp.maximum(a, 0.0)), wd_ref[j])
        x2 = x1 + acc
        r3 = lax.rsqrt(jnp.mean(x2 * x2, axis=-1, keepdims=True) + EPS)
        xn2 = x2 * r3
        gf_v = gf_ref[...]
        err = xn2 * gf_v - tgt_ref[...]
        part = jnp.sum(err * err) * (0.5 / D_MODEL)
        first = (lax.broadcasted_iota(jnp.int32, loss_ref.shape, 0) == 0) & (lax.broadcasted_iota(jnp.int32, loss_ref.shape, 1) == 0)
        loss_ref[...] += jnp.where(first, part, 0.0)
        dy = err * (1.0 / D_MODEL)
        dgf_ref[...] += jnp.sum(dy * xn2, axis=0, keepdims=True)
        dx2 = _rms_bwd(dy * gf_v, xn2, r3)
        dx2b = dx2.astype(MXU_DTYPE)
        dx2b_ref[...] = dx2b
        dh2 = jnp.zeros((tm, D_MODEL), F32)
        for j in range(N_CHIPS):
            sl = slice(j * FF_SHARD, (j + 1) * FF_SHARD)
            dhid = _mm_nt(dx2b, wd_ref[j])
            da = (dhid * (2.0 * jnp.maximum(a_ref[:, sl].astype(F32), 0.0))).astype(MXU_DTYPE)
            da_ref[:, sl] = da
            dh2 = dh2 + _mm_nt(da, wu_ref[j])
        dg2_ref[...] += jnp.sum(dh2 * xn1, axis=0, keepdims=True)
        dx1 = dx2 + _rms_bwd(dh2 * g2, xn1, r2)
        dx1_ref[...] = dx1
        dx1b_ref[...] = dx1.astype(dx1b_ref.dtype)

    row = lambda w: pl.BlockSpec((tm, w), lambda i: (i, 0))
    vec = jax.ShapeDtypeStruct((1, D_MODEL), F32)
    return pl.pallas_call(
        body, name="mlp_fwd_bwd", grid=(t_tok // tm,),
        in_specs=[row(D_MODEL), row(ATTN_W), row(POOL_W), row(D_MODEL), _resident((D_MODEL, D_MODEL)),
                  _resident((N_CHIPS, D_MODEL, FF_SHARD)), _resident((N_CHIPS, FF_SHARD, D_MODEL)),
                  _const((1, D_MODEL)), _const((1, D_MODEL))],
        out_specs=(row(D_MODEL), row(D_FF), row(D_FF), row(D_MODEL), row(D_MODEL), row(D_MODEL),
                   _const((8, LANES)), _const((1, D_MODEL)), _const((1, D_MODEL))),
        out_shape=(jax.ShapeDtypeStruct((t_tok, D_MODEL), MXU_DTYPE), jax.ShapeDtypeStruct((t_tok, D_FF), MXU_DTYPE),
                   jax.ShapeDtypeStruct((t_tok, D_FF), MXU_DTYPE), jax.ShapeDtypeStruct((t_tok, D_MODEL), MXU_DTYPE),
                   jax.ShapeDtypeStruct((t_tok, D_MODEL), F32), jax.ShapeDtypeStruct((t_tok, D_MODEL), MXU_DTYPE),
                   jax.ShapeDtypeStruct((8, LANES), F32), vec, vec),
        compiler_params=_cp(dimension_semantics=("arbitrary",)),
    )(x, attn, pool, target, w_out, w_up4, w_down4, g2, gf)


def _bwd_mlp_wgrads(h2, da, a, dx2b, tk):
    t_tok = h2.shape[0]

    def body(h2_ref, da_ref, a_ref, dx2b_ref, gup_ref, gdn_ref):
        @pl.when(pl.program_id(1) == 0)
        def _():
            gup_ref[...] = jnp.zeros_like(gup_ref)
            gdn_ref[...] = jnp.zeros_like(gdn_ref)

        gup_ref[0] += _mm_tn(h2_ref[...], da_ref[...])
        hid = jnp.square(jnp.maximum(a_ref[...].astype(F32), 0.0))
        gdn_ref[0] += _mm_tn(hid, dx2b_ref[...])

    tok = pl.BlockSpec((tk, D_MODEL), lambda j, t: (t, 0))
    ffb = pl.BlockSpec((tk, FF_SHARD), lambda j, t: (t, j))
    wblk = pl.BlockSpec((1, D_MODEL, D_MODEL), lambda j, t: (j, 0, 0))
    return pl.pallas_call(
        body, name="bwd_mlp_wgrads", grid=(N_CHIPS, t_tok // tk),
        in_specs=[tok, ffb, ffb, tok], out_specs=(wblk, wblk),
        out_shape=(jax.ShapeDtypeStruct((N_CHIPS, D_MODEL, FF_SHARD), F32), jax.ShapeDtypeStruct((N_CHIPS, FF_SHARD, D_MODEL), F32)),
        compiler_params=_cp(dimension_semantics=("parallel", "arbitrary")),
    )(h2, da, a, dx2b)


def _head_selector():
    ch = lax.broadcasted_iota(jnp.int32, (ATTN_W, LANES), 0)
    col = lax.broadcasted_iota(jnp.int32, (ATTN_W, LANES), 1)
    return (ch // HEAD_DIM == col).astype(MXU_DTYPE)


def _bwd_outproj(dx1b, attn, pool, w_out, head_sel, tm, after=()):
    t_tok = dx1b.shape[0]
    tokens, token_specs = _after(after)

    def body(dx_ref, attn_ref, pool_ref, wo_ref, sel_ref, *rest):
        dattn_ref, dpool_ref, delta_ref, gwo_ref = rest[len(tokens):]

        @pl.when(pl.program_id(0) == 0)
        def _():
            gwo_ref[...] = jnp.zeros_like(gwo_ref)

        dx = dx_ref[...]
        attn = attn_ref[...]
        dattn = _mm_nt(dx, wo_ref[:ATTN_W])
        dattn_ref[...] = dattn.astype(dattn_ref.dtype)
        dpool_ref[...] = _mm_nt(dx, wo_ref[ATTN_W:]).astype(dpool_ref.dtype)
        prod = dattn * attn.astype(F32)
        hi = prod.astype(MXU_DTYPE)
        lo = prod - hi.astype(F32)
        delta = _mm(hi, sel_ref[...]) + _mm(lo, sel_ref[...])
        delta_ref[...] = delta.T[:N_Q_HEADS]
        gwo_ref[:ATTN_W] += _mm_tn(attn, dx)
        gwo_ref[ATTN_W:] += _mm_tn(pool_ref[...], dx)

    row = lambda w: pl.BlockSpec((tm, w), lambda i: (i, 0))
    return pl.pallas_call(
        body, name="bwd_outproj", grid=(t_tok // tm,),
        in_specs=[row(D_MODEL), row(ATTN_W), row(POOL_W), _resident((D_MODEL, D_MODEL)), _const((ATTN_W, LANES))] + token_specs,
        out_specs=(row(ATTN_W), row(POOL_W), pl.BlockSpec((N_Q_HEADS, tm), lambda i: (0, i)), _const((D_MODEL, D_MODEL))),
        out_shape=(jax.ShapeDtypeStruct((t_tok, ATTN_W), MXU_DTYPE), jax.ShapeDtypeStruct((t_tok, POOL_W), MXU_DTYPE),
                   jax.ShapeDtypeStruct((N_Q_HEADS, t_tok), F32), jax.ShapeDtypeStruct((D_MODEL, D_MODEL), F32)),
        compiler_params=_cp(dimension_semantics=("arbitrary",)),
    )(dx1b, attn, pool, w_out, head_sel, *tokens)


def _bwd_attn(q, k, v, dattn, lse, delta, sink_rows, bias, cos_t, sin_t, seq, after=()):
    t_tok = q.shape[0]
    nblk = seq // BLK
    together = max(f for f in (4, 2, 1) if nblk % f == 0)
    qkv_w = ATTN_W + 2 * KV_W
    tokens, token_specs = _after(after)

    def unrope(d, cos, sin):
        return d * cos - _swap_halves(d) * sin

    def body(q_ref, k_ref, v_ref, do_ref, lse_ref, delta_ref, sink_ref, bias_ref, cos_ref, sin_ref, *rest):
        dqkv_ref, dsink_ref, dk_acc, dv_acc = rest[len(tokens):]

        @pl.when(pl.program_id(0) == 0)
        def _():
            dsink_ref[...] = jnp.zeros_like(dsink_ref)

        dk_acc[...] = jnp.zeros_like(dk_acc)
        dv_acc[...] = jnp.zeros_like(dv_acc)

        def blocks(ik, dsink):
            units = []
            for s in range(together):
                i = together * ik + s
                r0 = pl.multiple_of(i * BLK, BLK)
                p0 = pl.multiple_of(jnp.maximum(i - 1, 0) * BLK, BLK)
                units += [(g, r0, p0, bias_ref[jnp.minimum(i, 1)]) for g in range(N_KV_HEADS)]
            scores = [_scores_t(k_ref, q_ref, bias, p0, r0, g) for g, r0, p0, bias in units]
            doss = [_stack_heads(do_ref, r0, g) for g, r0, _, _ in units]
            dpts = [_mm_nt(_kv_window(v_ref, p0, r0, g), dos) for (g, r0, p0, _), dos in zip(units, doss)]
            lses = [_head_rows(lse_ref, r0, g) for g, r0, _, _ in units]
            deltas = [_head_rows(delta_ref, r0, g) for g, r0, _, _ in units]
            pns = [jnp.exp(sc[0] - lse) for sc, lse in zip(scores, lses)]
            dsts = [pn * (dpt - delta) for pn, dpt, delta in zip(pns, dpts, deltas)]
            dsink = list(dsink)
            for (g, _, _, _), lse, delta in zip(units, lses, deltas):
                dsink[g] = dsink[g] - jnp.exp(sink_ref[g] - lse) * delta
            dqts = [_mm_tn(sc[1], dst) * (HEAD_DIM ** -0.5) for sc, dst in zip(scores, dsts)]
            dkks = [_mm(dst, sc[2]) for sc, dst in zip(scores, dsts)]
            dvvs = [_mm(pn, dos) for pn, dos in zip(pns, doss)]
            for (g, r0, p0, _), dqt, dkk, dvv in zip(units, dqts, dkks, dvvs):
                cos, sin = cos_ref[pl.ds(r0, BLK), :], sin_ref[pl.ds(r0, BLK), :]
                for t, tile in enumerate(_pairs_to_rows(dqt)):
                    c0 = (Q_PER_KV * g + 2 * t) * HEAD_DIM
                    dqkv_ref[pl.ds(r0, BLK), c0:c0 + LANES] = unrope(tile, cos, sin).astype(dqkv_ref.dtype)
                dk_acc[g, pl.ds(p0, BLK), :] += dkk[:BLK]
                dk_acc[g, pl.ds(r0, BLK), :] += dkk[BLK:]
                dv_acc[g, pl.ds(p0, BLK), :] += dvv[:BLK]
                dv_acc[g, pl.ds(r0, BLK), :] += dvv[BLK:]
            return tuple(dsink)

        zero = jnp.zeros((1, Q_PER_KV * BLK), F32)
        dsink = lax.fori_loop(0, nblk // together, blocks, (zero,) * N_KV_HEADS)
        lane = lax.broadcasted_iota(jnp.int32, dsink_ref.shape, 1)
        row = lax.broadcasted_iota(jnp.int32, dsink_ref.shape, 0)
        tile = jnp.zeros(dsink_ref.shape, F32)
        for g in range(N_KV_HEADS):
            for h in range(Q_PER_KV):
                tot = jnp.sum(dsink[g][:, h * BLK:(h + 1) * BLK])
                tile = tile + jnp.where((row == 0) & (lane == Q_PER_KV * g + h), tot, 0.0)
        dsink_ref[...] += tile
        dk = jnp.concatenate([dk_acc[g] for g in range(N_KV_HEADS)], axis=1)
        dqkv_ref[:, ATTN_W:ATTN_W + KV_W] = unrope(dk, cos_ref[...], sin_ref[...]).astype(dqkv_ref.dtype)
        dqkv_ref[:, ATTN_W + KV_W:] = jnp.concatenate([dv_acc[g] for g in range(N_KV_HEADS)], axis=1).astype(dqkv_ref.dtype)

    row = lambda w: pl.BlockSpec((seq, w), lambda i: (i, 0))
    kv = row(KV_W)
    per_head = pl.BlockSpec((N_Q_HEADS, seq), lambda i: (0, i))
    return pl.pallas_call(
        body, name="bwd_attn", grid=(t_tok // seq,),
        in_specs=[row(ATTN_W), kv, kv, row(ATTN_W), per_head, per_head, _const((N_KV_HEADS, 1, Q_PER_KV * BLK)),
                  _const((2, 2 * BLK, BLK)), _resident((seq, LANES)), _resident((seq, LANES))] + token_specs,
        out_specs=(row(qkv_w), _const((8, LANES))),
        out_shape=(jax.ShapeDtypeStruct((t_tok, qkv_w), MXU_DTYPE), jax.ShapeDtypeStruct((8, LANES), F32)),
        scratch_shapes=[pltpu.VMEM((N_KV_HEADS, seq, HEAD_DIM), F32), pltpu.VMEM((N_KV_HEADS, seq, HEAD_DIM), F32)],
        compiler_params=_cp(dimension_semantics=("arbitrary",)),
    )(q, k, v, dattn, lse, delta, sink_rows, bias, cos_t, sin_t, *tokens)


def _bwd_pool(u, dpool, w_pool, pool_scale, seq):
    t_tok = u.shape[0]

    def body(u_ref, dp_ref, wp_ref, sc_ref, du_ref, dwp_ref, dsc_ref):
        @pl.when(pl.program_id(0) == 0)
        def _():
            dwp_ref[...] = jnp.zeros_like(dwp_ref)
            dsc_ref[...] = jnp.zeros_like(dsc_ref)

        for gi, w in enumerate(POOL_WINDOWS):
            sl = slice(gi * POOL_G, (gi + 1) * POOL_G)
            ug = u_ref[:, sl].astype(F32)
            inv = _inv_count(seq, w)
            d = (_window_sum(ug, w, seq, False) * inv - ug).astype(MXU_DTYPE)
            y = _mm(d, wp_ref[gi])
            dpool = dp_ref[:, sl].astype(F32)
            dsc_ref[:, sl] += jnp.sum(y * dpool, axis=0, keepdims=True)
            dy = (dpool * sc_ref[:, sl]).astype(MXU_DTYPE)
            dwp_ref[gi] += _mm_tn(d, dy)
            dd = _mm_nt(dy, wp_ref[gi])
            du_ref[:, sl] = (_window_sum(dd * inv, w, seq, True) - dd).astype(du_ref.dtype)

    row = pl.BlockSpec((seq, POOL_W), lambda i: (i, 0))
    return pl.pallas_call(
        body, name="bwd_pool", grid=(t_tok // seq,),
        in_specs=[row, row, _const((N_POOL, POOL_G, POOL_G)), _const((1, POOL_W))],
        out_specs=(row, _const((N_POOL, POOL_G, POOL_G)), _const((1, POOL_W))),
        out_shape=(jax.ShapeDtypeStruct((t_tok, POOL_W), MXU_DTYPE), jax.ShapeDtypeStruct((N_POOL, POOL_G, POOL_G), F32),
                   jax.ShapeDtypeStruct((1, POOL_W), F32)),
        compiler_params=_cp(dimension_semantics=("arbitrary",)),
    )(u, dpool, w_pool, pool_scale)


def _bwd_inproj(dqkv, du, x, dx1, w_in, g1, tm):
    t_tok = x.shape[0]
    nsteps = t_tok // tm

    qkv_w = ATTN_W + 2 * KV_W

    def body(dqkv_ref, du_ref, x_ref, dx1_ref, w_ref, g_ref, gx_ref, gw_ref, dg_ref):
        @pl.when(pl.program_id(0) == 0)
        def _():
            gw_ref[...] = jnp.zeros_like(gw_ref)
            dg_ref[...] = jnp.zeros_like(dg_ref)

        dqkv, du = dqkv_ref[...], du_ref[...]
        xv = x_ref[...]
        r = lax.rsqrt(jnp.mean(xv * xv, axis=-1, keepdims=True) + EPS)
        xn = xv * r
        g = g_ref[...]
        dh = _mm(dqkv, w_ref[:qkv_w]) + _mm(du, w_ref[qkv_w:])
        dg_ref[...] += jnp.sum(dh * xn, axis=0, keepdims=True)
        gx_ref[...] = dx1_ref[...] + _rms_bwd(dh * g, xn, r)
        h = (xn * g).astype(MXU_DTYPE)
        gw_ref[:qkv_w] += _mm_tn(dqkv, h)
        gw_ref[qkv_w:] += _mm_tn(du, h)

    row = lambda w: pl.BlockSpec((tm, w), lambda i: (i, 0))
    return pl.pallas_call(
        body, name="bwd_inproj", grid=(nsteps,),
        in_specs=[row(qkv_w), row(POOL_W), row(D_MODEL), row(D_MODEL), _resident((IN_W, D_MODEL)), _const((1, D_MODEL))],
        out_specs=(row(D_MODEL), _const((IN_W, D_MODEL)), _const((1, D_MODEL))),
        out_shape=(jax.ShapeDtypeStruct((t_tok, D_MODEL), F32), jax.ShapeDtypeStruct((IN_W, D_MODEL), F32),
                   jax.ShapeDtypeStruct((1, D_MODEL), F32)),
        compiler_params=_cp(dimension_semantics=("arbitrary",)),
    )(dqkv, du, x, dx1, w_in, g1)


class _NoComm:
    def inproj_done(self, q):
        return ()

    def attn_done(self, attn):
        return ()

    def rest_of_weights(self, pool):
        raise NotImplementedError

    def mlp_grads_ready(self, gw_up4, gw_down4):
        return ()

    def outproj_done(self, gw_out):
        return ()


def _local_step(x, target, w_in, comm, g1, sinks, w_pool, pool_scale, g2, gf, seq):
    tm_big, tm_mlp = min(TILE_BIG, seq), min(TILE_MLP, seq)
    cos_t, sin_t = _rope_tables(seq)
    sink_rows, bias = _sink_rows(sinks), _attn_bias()
    q, k, v, u = _fwd_inproj(x, g1, w_in, cos_t, sin_t, tm_big)
    attn, lse = _fwd_attn(q, k, v, sink_rows, bias, seq, comm.inproj_done(q))
    pool = _fwd_pool(u, w_pool, pool_scale, seq, comm.attn_done(attn))
    w_out, w_up4, w_down4 = comm.rest_of_weights(pool)
    h2, a, da, dx2b, dx1, dx1b, loss, dgf, dg2 = _mlp_fwd_bwd(x, attn, pool, target, w_out, w_up4, w_down4, g2, gf, tm_mlp)
    gw_up4, gw_down4 = _bwd_mlp_wgrads(h2, da, a, dx2b, min(TILE_BIG, x.shape[0]))
    dattn, dpool, delta, gw_out = _bwd_outproj(dx1b, attn, pool, w_out, _head_selector(), tm_big,
                                               comm.mlp_grads_ready(gw_up4, gw_down4))
    dqkv, dsinks = _bwd_attn(q, k, v, dattn, lse, delta, sink_rows, bias, cos_t, sin_t, seq, comm.outproj_done(gw_out))
    du, dwp, dsc = _bwd_pool(u, dpool, w_pool, pool_scale, seq)
    gx, gw_in_t, dg1 = _bwd_inproj(dqkv, du, x, dx1, w_in, g1, tm_big)
    big = (gw_in_t.reshape(N_CHIPS, IN_SHARD, D_MODEL), gw_out.reshape(N_CHIPS, OUT_SHARD, D_MODEL), gw_up4, gw_down4)
    small = (dg1, dsinks, dwp.reshape(N_POOL * POOL_G, POOL_G), dsc, dg2, dgf, loss)
    return gx, big, small


def _sibling_swap(arrs, name):
    n = len(arrs)

    def body(*refs):
        cps = _sibling_copies(refs[:n], refs[n:2 * n], refs[2 * n], refs[2 * n + 1], False)
        for cp in cps:
            cp.start()
        for cp in cps:
            cp.wait()

    return pl.pallas_call(
        body, name=name, out_shape=tuple(jax.ShapeDtypeStruct(a.shape, a.dtype) for a in arrs),
        in_specs=[_HBM] * n, out_specs=tuple([_HBM] * n),
        scratch_shapes=[pltpu.SemaphoreType.DMA((n,)), pltpu.SemaphoreType.DMA((n,))],
        compiler_params=_cp(),
    )(*arrs)


def _sibling_copies(srcs, lands, send, recv, pick_half):
    x, y, c = lax.axis_index("x"), lax.axis_index("y"), lax.axis_index("c")
    cps = []
    for i in range(len(srcs)):
        src = srcs[i]
        if pick_half:
            h = src.shape[1] // 2
            src = src.at[:, pl.ds((1 - c) * h, h)]
        cps.append(pltpu.make_async_remote_copy(
            src_ref=src, dst_ref=lands[i], send_sem=send.at[i], recv_sem=recv.at[i],
            device_id=(x, y, 1 - c), device_id_type=MESH))
    return cps


def _sibling_start(arrs, name, pick_half=True):
    n = len(arrs)

    def body(*refs):
        send, recv, token = refs[3 * n:3 * n + 3]
        for cp in _sibling_copies(refs[:n], refs[2 * n:3 * n], send, recv, pick_half):
            cp.start()
        token[...] = jnp.zeros_like(token)

    half = lambda a: ((a.shape[0], a.shape[1] // 2) + a.shape[2:]) if pick_half else a.shape
    res = pl.pallas_call(
        body, name=name + "_start",
        out_shape=tuple(pltpu.HBM(a.shape, a.dtype) for a in arrs) + tuple(pltpu.HBM(half(a), a.dtype) for a in arrs) + (
            pltpu.SemaphoreType.DMA((n,)), pltpu.SemaphoreType.DMA((n,)), TOKEN),
        in_specs=[_HBM] * n, out_specs=tuple([_HBM] * (2 * n)) + (_SEM, _SEM, pl.BlockSpec(memory_space=pltpu.VMEM)),
        input_output_aliases={i: i for i in range(n)},
        compiler_params=_cp(has_side_effects=_EFFECT),
    )(*[pltpu.with_memory_space_constraint(a, pltpu.HBM) for a in arrs])
    return res[:n], res[n:2 * n], res[2 * n], res[2 * n + 1], res[2 * n + 2]


def _sibling_wait(arrs, lands, send, recv, after, name, pick_half=True):
    n = len(arrs)

    def body(*refs):
        for cp in _sibling_copies(refs[:n], refs[n:2 * n], refs[2 * n], refs[2 * n + 1], pick_half):
            cp.wait_send()
            cp.wait_recv()

    res = pl.pallas_call(
        body, name=name + "_wait", out_shape=tuple(pltpu.HBM(a.shape, a.dtype) for a in list(arrs) + list(lands)),
        in_specs=[_HBM] * (2 * n) + [_SEM, _SEM, _ANY], out_specs=tuple([_HBM] * (2 * n)),
        input_output_aliases={i: i for i in range(2 * n)},
        compiler_params=_cp(has_side_effects=_EFFECT),
    )(*arrs, *lands, send, recv, after)
    return res[:n], res[n:]


def _row_block(rows):
    if rows <= 256:
        return rows
    for cand in (256, 128, 64, 32, 16, 8):
        if rows % cand == 0:
            return cand
    raise ValueError(rows)


def _chip_partial(g4s, r4s, c_arr, name):
    n = len(g4s)
    _, rows, cols = r4s[0].shape
    rb = _row_block(rows)
    nb = rows // rb

    def body(c_ref, *refs):
        for k in range(n):
            refs[2 * n + k][...] = (refs[k][...] + refs[n + k][...]).astype(BF16)

    own = pl.BlockSpec((1, rb, cols), lambda s, i, c: (s, c[0] * nb + i, 0))
    blk = pl.BlockSpec((1, rb, cols), lambda s, i, c: (s, i, 0))
    return pl.pallas_call(
        body, name=name,
        grid_spec=pltpu.PrefetchScalarGridSpec(num_scalar_prefetch=1, grid=(N_CHIPS, nb),
                                               in_specs=[own] * n + [blk] * n, out_specs=tuple([blk] * n)),
        out_shape=tuple(jax.ShapeDtypeStruct(r.shape, BF16) for r in r4s),
        compiler_params=_cp(dimension_semantics=("parallel", "parallel")),
    )(c_arr, *g4s, *r4s)


def _send_start(parts, name):
    n = len(parts)

    def body(*refs):
        srcs, lands = refs[:n], refs[2 * n:3 * n]
        send, recv, token = refs[3 * n:3 * n + 3]
        x, y, c = lax.axis_index("x"), lax.axis_index("y"), lax.axis_index("c")
        for i in range(n):
            for m, chip in enumerate(_other_chips(x, y)):
                pltpu.make_async_remote_copy(
                    src_ref=srcs[i].at[2 * chip[0] + chip[1]], dst_ref=lands[i].at[m],
                    send_sem=send.at[3 * i + m], recv_sem=recv.at[3 * i + m],
                    device_id=(chip[0], chip[1], c), device_id_type=MESH).start()
        token[...] = jnp.zeros_like(token)

    res = pl.pallas_call(
        body, name=name,
        out_shape=tuple(pltpu.HBM(p.shape, p.dtype) for p in parts) + tuple(pltpu.HBM((3,) + p.shape[1:], p.dtype) for p in parts) + (
            pltpu.SemaphoreType.DMA((3 * n,)), pltpu.SemaphoreType.DMA((3 * n,)), TOKEN),
        in_specs=[_HBM] * n, out_specs=tuple([_HBM] * (2 * n)) + (_SEM, _SEM, pl.BlockSpec(memory_space=pltpu.VMEM)),
        input_output_aliases={i: i for i in range(n)},
        compiler_params=_cp(has_side_effects=_EFFECT),
    )(*[pltpu.with_memory_space_constraint(p, pltpu.HBM) for p in parts])
    return res[:n], res[n:2 * n], res[2 * n], res[2 * n + 1], res[2 * n + 2]


def _send_wait(parts, lands, send, recv, after, name):
    n = len(parts)

    def body(*refs):
        srcs, ins = refs[:n], refs[n:2 * n]
        send_ref, recv_ref = refs[2 * n], refs[2 * n + 1]
        x, y, c = lax.axis_index("x"), lax.axis_index("y"), lax.axis_index("c")
        for i in range(n):
            for m, chip in enumerate(_other_chips(x, y)):
                cp = pltpu.make_async_remote_copy(
                    src_ref=srcs[i].at[2 * chip[0] + chip[1]], dst_ref=ins[i].at[m],
                    send_sem=send_ref.at[3 * i + m], recv_sem=recv_ref.at[3 * i + m],
                    device_id=(chip[0], chip[1], c), device_id_type=MESH)
                cp.wait_send()
                cp.wait_recv()

    res = pl.pallas_call(
        body, name=name, out_shape=tuple(pltpu.HBM(a.shape, a.dtype) for a in list(parts) + list(lands)),
        in_specs=[_HBM] * (2 * n) + [_SEM, _SEM, _ANY], out_specs=tuple([_HBM] * (2 * n)),
        input_output_aliases={i: i for i in range(2 * n)},
        compiler_params=_cp(has_side_effects=_EFFECT),
    )(*parts, *lands, send, recv, after)
    return res[n:]


def _final_half(g4s, r4s, got3s, jc_arr, name):
    n = len(g4s)
    _, rows, cols = r4s[0].shape
    rb = _row_block(rows)
    nb = rows // rb

    def body(jc_ref, *refs):
        for k in range(n):
            g_ref, r_ref, p_ref, o_ref = refs[k], refs[n + k], refs[2 * n + k], refs[3 * n + k]
            own = g_ref[0] + r_ref[0]
            o_ref[...] = ((own + p_ref[0].astype(F32)) + p_ref[1].astype(F32)) + p_ref[2].astype(F32)

    own = pl.BlockSpec((1, rb, cols), lambda i, jc: (jc[0], jc[1] * nb + i, 0))
    sib = pl.BlockSpec((1, rb, cols), lambda i, jc: (jc[0], i, 0))
    got = pl.BlockSpec((3, rb, cols), lambda i, jc: (0, i, 0))
    out = pl.BlockSpec((rb, cols), lambda i, jc: (i, 0))
    return pl.pallas_call(
        body, name=name,
        grid_spec=pltpu.PrefetchScalarGridSpec(num_scalar_prefetch=1, grid=(nb,),
                                               in_specs=[own] * n + [sib] * n + [got] * n, out_specs=tuple([out] * n)),
        out_shape=tuple(jax.ShapeDtypeStruct((rows, cols), F32) for _ in range(n)),
        compiler_params=_cp(dimension_semantics=("parallel",)),
    )(jc_arr, *g4s, *r4s, *got3s)


def _adamw_math(w, g, m, v):
    m2 = ADAM_B1 * m + (1.0 - ADAM_B1) * g
    v2 = ADAM_B2 * v + (1.0 - ADAM_B2) * (g * g)
    m_hat = m2 / (1.0 - ADAM_B1 ** ADAM_STEP)
    v_hat = v2 / (1.0 - ADAM_B2 ** ADAM_STEP)
    delta = -ADAM_LR * (m_hat / (jnp.sqrt(v_hat) + ADAM_EPS) + ADAM_WD * w)
    return delta, m2, v2


def _adamw_shard(mines, others, ws, ms, vs, c_arr, name):
    n = len(ws)
    rows, cols = ws[0].shape
    half = rows // 2
    rb = _row_block(half)
    nb = half // rb

    def body(c_ref, *refs):
        for k in range(n):
            a_ref, b_ref, w_ref, m_ref, v_ref = (refs[i * n + k] for i in range(5))
            g_ref, d_ref, m2_ref, v2_ref = refs[5 * n + 4 * k:5 * n + 4 * k + 4]
            g = jnp.where(pl.program_id(0) == c_ref[0], a_ref[...], b_ref[...])
            delta, m2, v2 = _adamw_math(w_ref[...], g, m_ref[...], v_ref[...])
            g_ref[...] = g
            d_ref[...] = delta
            m2_ref[...] = m2
            v2_ref[...] = v2

    hb = pl.BlockSpec((rb, cols), lambda h, i, c: (i, 0))
    fb = pl.BlockSpec((rb, cols), lambda h, i, c: (h * nb + i, 0))
    shp = jax.ShapeDtypeStruct((rows, cols), F32)
    res = pl.pallas_call(
        body, name=name,
        grid_spec=pltpu.PrefetchScalarGridSpec(num_scalar_prefetch=1, grid=(2, nb),
                                               in_specs=[hb] * (2 * n) + [fb] * (3 * n), out_specs=tuple([fb] * (4 * n))),
        out_shape=tuple([shp] * (4 * n)),
        compiler_params=_cp(dimension_semantics=("parallel", "parallel")),
    )(c_arr, *mines, *others, *ws, *ms, *vs)
    return [res[4 * k:4 * k + 4] for k in range(n)]


def _small_allreduce(parts):
    n = len(parts)

    def body(*refs):
        p_refs, accs = refs[:n], refs[n:2 * n]
        bufs = refs[2 * n:3 * n]
        send, recv = refs[3 * n:]
        x, y, c = lax.axis_index("x"), lax.axis_index("y"), lax.axis_index("c")
        partners = [(x, y, 1 - c), (1 - x, y, c), (x, 1 - y, c)]
        other_way = [partners[0], partners[2], partners[1]]
        for i in range(n):
            accs[i][...] = p_refs[i][...]
        for s in range(3):
            cps = []
            for i in range(n):
                rows = parts[i].shape[0]
                pieces = [(slice(None), partners[s])] if rows < 2 * BLK else [
                    (pl.ds(0, rows // 2), partners[s]), (pl.ds(rows // 2, rows // 2), other_way[s])]
                for h, (sl, partner) in enumerate(pieces):
                    cp = pltpu.make_async_remote_copy(
                        src_ref=accs[i].at[sl], dst_ref=bufs[i].at[s, sl], send_sem=send.at[6 * i + 2 * s + h],
                        recv_sem=recv.at[6 * i + 2 * s + h], device_id=partner, device_id_type=MESH)
                    cp.start()
                    cps.append(cp)
            for cp in cps:
                cp.wait()
            for i in range(n):
                accs[i][...] = accs[i][...] + bufs[i][s]

    vmem = pl.BlockSpec(memory_space=pltpu.VMEM)
    return pl.pallas_call(
        body, name="small_allreduce", out_shape=tuple(jax.ShapeDtypeStruct(p.shape, F32) for p in parts),
        in_specs=[vmem] * n, out_specs=tuple([vmem] * n),
        scratch_shapes=[pltpu.VMEM((3,) + p.shape, F32) for p in parts] + [
            pltpu.SemaphoreType.DMA((6 * n,)), pltpu.SemaphoreType.DMA((6 * n,))],
        compiler_params=_cp(),
    )(*parts)


def _small_adamw(reduced, params):
    n = len(reduced)
    n_w = len(params)

    def body(*refs):
        r_refs = refs[:n]
        wmv = refs[n:n + 3 * n_w]
        outs = refs[n + 3 * n_w:]
        outs[0][...] = r_refs[n - 1][0:1, 0:1]
        grads = [r_refs[0][...], r_refs[1][0:1, 0:N_Q_HEADS]] + [r_refs[i][...] for i in range(2, n_w)]
        for i in range(n_w):
            w_ref, m_ref, v_ref = wmv[3 * i:3 * i + 3]
            g_ref, d_ref, m2_ref, v2_ref = outs[1 + 4 * i:5 + 4 * i]
            delta, m2, v2 = _adamw_math(w_ref[...], grads[i], m_ref[...], v_ref[...])
            g_ref[...] = grads[i]
            d_ref[...] = delta
            m2_ref[...] = m2
            v2_ref[...] = v2

    flat = [a for p in params for a in p]
    vmem = pl.BlockSpec(memory_space=pltpu.VMEM)
    out_shape = [jax.ShapeDtypeStruct((1, 1), F32)]
    for p in params:
        out_shape += [jax.ShapeDtypeStruct(p[0].shape, F32)] * 4
    res = pl.pallas_call(
        body, name="small_adamw", out_shape=tuple(out_shape),
        in_specs=[vmem] * (n + len(flat)), out_specs=tuple([vmem] * len(out_shape)),
        compiler_params=_cp(),
    )(*reduced, *flat)
    return res[0], [res[1 + 4 * i:5 + 4 * i] for i in range(n_w)]


def kernel(x, attn_norm_g, w_in, attn_sinks, w_pool, pool_scale, w_out, mlp_norm_g, w_up, w_down, final_norm_g, loss_target, m_attn_norm_g, m_w_in, m_attn_sinks, m_w_pool, m_pool_scale, m_w_out, m_mlp_norm_g, m_w_up, m_w_down, m_final_norm_g, v_attn_norm_g, v_w_in, v_attn_sinks, v_w_pool, v_pool_scale, v_w_out, v_mlp_norm_g, v_w_up, v_w_down, v_final_norm_g):
    nseq, seq, d = x.shape
    c_idx = lax.axis_index("c").astype(jnp.int32)
    j_idx = (2 * lax.axis_index("x") + lax.axis_index("y")).astype(jnp.int32)
    c_arr = jnp.reshape(c_idx, (1,))
    jc_arr = jnp.stack([j_idx, c_idx])

    big_w = (w_in[0].T, w_out[0], w_up[0], w_down[0])
    big_m = (m_w_in[0].T, m_w_out[0], m_w_up[0], m_w_down[0])
    big_v = (v_w_in[0].T, v_w_out[0], v_w_up[0], v_w_down[0])
    (w_in_lands, w_in_send, w_in_recv), rest = _gather_start(big_w[:1], big_w[1:])
    (w_in4,) = _forward_now(_gather_wait(w_in_lands, w_in_send, w_in_recv, "gather_wait_w_in"), "forward_w_in")
    w_in_full = w_in4.reshape(IN_W, D_MODEL)
    class Comm(_NoComm):
        def inproj_done(self, q):
            self.lands, self.send, self.recv, token = _gather_relay(*rest, q)
            return (token,)

        def attn_done(self, attn):
            arrived = _gather_wait_relayed(self.lands, self.send, self.recv, attn)
            self.lands, self.send, self.recv, token = _forward_start(arrived)
            return (token,)

        def rest_of_weights(self, pool):
            w_out4, w_up4, w_down4 = _forward_wait(self.lands, self.send, self.recv, pool)
            return w_out4.reshape(D_MODEL, D_MODEL), w_up4, w_down4

        def mlp_grads_ready(self, gw_up4, gw_down4):
            self.mlp = _sibling_start((gw_up4, gw_down4), "mlp_grads_to_sibling")
            return (self.mlp[4],)

        def outproj_done(self, gw_out):
            grads, lands, send, recv, _ = self.mlp
            sib_out = _sibling_start((gw_out.reshape(N_CHIPS, OUT_SHARD, D_MODEL),), "w_out_grad_to_sibling")
            mlp_grads, mlp_from_sib = _sibling_wait(grads, lands, send, recv, sib_out[4], "mlp_grads_to_sibling")
            partials = _chip_partial(mlp_grads, mlp_from_sib, c_arr, "chip_partial_mlp")
            out_grad, out_from_sib = _sibling_wait(*sib_out[:4], partials[1], "w_out_grad_to_sibling")
            partials = tuple(_chip_partial(out_grad, out_from_sib, c_arr, "chip_partial_w_out")) + tuple(partials)
            self.grads = tuple(out_grad) + tuple(mlp_grads)
            self.from_sib = tuple(out_from_sib) + tuple(mlp_from_sib)
            self.parts, self.part_lands, self.part_send, self.part_recv, token = _send_start(partials, "send_start")
            return (token,)

    comm = Comm()
    gx, big_g, small_g = _local_step(
        x.reshape(nseq * seq, d), loss_target.reshape(nseq * seq, d), w_in_full, comm,
        attn_norm_g, attn_sinks.reshape(N_Q_HEADS), w_pool[0], pool_scale, mlp_norm_g, final_norm_g.reshape(1, d), seq)
    big_g = tuple(big_g[:1]) + comm.grads

    def adamw(idx, mines, others, name):
        pick = lambda group: [group[i] for i in idx]
        return _adamw_shard(mines, others, pick(big_w), pick(big_m), pick(big_v), c_arr, name)

    sib_in = _sibling_start(big_g[:1], "w_in_grad_to_sibling")
    received = _send_wait(comm.parts, comm.part_lands, comm.part_send, comm.part_recv, sib_in[4], "send_wait")
    mine = list(_final_half(big_g[1:2], comm.from_sib[:1], received[:1], jc_arr, "final_half_w_out"))
    mine += list(_final_half(big_g[2:], comm.from_sib[1:], received[1:], jc_arr, "final_half_mlp"))
    halves = _sibling_start(mine, "halves_to_sibling", False)
    g_in, r_in = _sibling_wait(*sib_in[:4], halves[4], "w_in_grad_to_sibling")
    send_in = _send_start(_chip_partial(g_in, r_in, c_arr, "chip_partial_w_in"), "send_start_w_in")
    mine, other = _sibling_wait(*halves[:4], send_in[4], "halves_to_sibling", False)
    (out_res,) = adamw((1,), mine[:1], other[:1], "adamw_w_out")
    up_res, down_res = adamw((2, 3), mine[1:], other[1:], "adamw_mlp")
    p_in = _send_wait(*send_in[:4], up_res[0], "send_wait_w_in")
    mine_in = _final_half(g_in, r_in, p_in, jc_arr, "final_half_w_in")
    other_in = _sibling_swap(mine_in, "w_in_half_to_sibling")
    (in_res,) = adamw((0,), mine_in, other_in, "adamw_w_in")
    big_out = [[t.T for t in in_res], out_res, up_res, down_res]

    wp_flat = lambda a: a.reshape(N_POOL * POOL_G, POOL_G)
    small_params = [
        (attn_norm_g, m_attn_norm_g, v_attn_norm_g),
        (attn_sinks, m_attn_sinks, v_attn_sinks),
        (wp_flat(w_pool), wp_flat(m_w_pool), wp_flat(v_w_pool)),
        (pool_scale, m_pool_scale, v_pool_scale),
        (mlp_norm_g, m_mlp_norm_g, v_mlp_norm_g),
        (final_norm_g.reshape(1, d), m_final_norm_g.reshape(1, d), v_final_norm_g.reshape(1, d)),
    ]
    loss, small_out = _small_adamw(_small_allreduce(small_g), small_params)

    weights = (attn_norm_g, w_in, attn_sinks, w_pool, pool_scale, w_out, mlp_norm_g, w_up, w_down, final_norm_g)
    order = [small_out[0], big_out[0], small_out[1], small_out[2], small_out[3], big_out[1], small_out[4], big_out[2],
             big_out[3], small_out[5]]
    outs = [loss.reshape(()), gx.reshape(nseq, seq, d)]
    for kind in range(4):
        outs += [res[kind].reshape(w.shape) for res, w in zip(order, weights)]
    return tuple(outs)
```

```python
import jax
import jax.numpy as jnp
import numpy as np
from jax import lax
from jax.experimental import pallas as pl
from jax.experimental.pallas import tpu as pltpu

F32 = jnp.float32
BF16 = jnp.bfloat16
MXU_DTYPE = jnp.bfloat16

D_MODEL = 1024
HEAD_DIM = 64
N_Q_HEADS = 8
N_KV_HEADS = 2
Q_PER_KV = N_Q_HEADS // N_KV_HEADS
ATTN_W = N_Q_HEADS * HEAD_DIM
KV_W = N_KV_HEADS * HEAD_DIM
BLK = 128
POOL_WINDOWS = (2, 4, 8, 16)
N_POOL = len(POOL_WINDOWS)
POOL_W = D_MODEL - ATTN_W
POOL_G = POOL_W // N_POOL
IN_W = ATTN_W + 2 * KV_W + POOL_W
D_FF = 4 * D_MODEL
EPS = 1e-6
ROPE_THETA = 10000.0
N_CHIPS = 4
IN_SHARD = IN_W // N_CHIPS
OUT_SHARD = D_MODEL // N_CHIPS
FF_SHARD = D_FF // N_CHIPS
LANES = 128

ADAM_LR = 0.001
ADAM_B1 = 0.9
ADAM_B2 = 0.999
ADAM_EPS = 1e-08
ADAM_WD = 0.01
ADAM_STEP = 10

VMEM_LIMIT = 56 * 1024 * 1024
TILE_BIG = 1024
TILE_MLP = 256
MESH = pl.DeviceIdType.MESH


def _cp(**kw):
    return pltpu.CompilerParams(vmem_limit_bytes=VMEM_LIMIT, **kw)


def _mm(a, b):
    return jnp.dot(a.astype(MXU_DTYPE), b.astype(MXU_DTYPE), preferred_element_type=F32)


def _mm_nt(a, b):
    return lax.dot_general(a.astype(MXU_DTYPE), b.astype(MXU_DTYPE), (((1,), (1,)), ((), ())),
                           preferred_element_type=F32)


def _mm_tn(a, b):
    return lax.dot_general(a.astype(MXU_DTYPE), b.astype(MXU_DTYPE), (((0,), (0,)), ((), ())),
                           preferred_element_type=F32)


def _resident(shape):
    nd = len(shape)
    return pl.BlockSpec(shape, lambda *_: (0,) * nd, pipeline_mode=pl.Buffered(1))


def _const(shape):
    nd = len(shape)
    return pl.BlockSpec(shape, lambda *_: (0,) * nd)


def _rope_tables(seq):
    half = HEAD_DIM // 2
    inv_freq = (ROPE_THETA ** (-np.arange(half, dtype=np.float32) / half)).astype(np.float32)
    ang = np.arange(seq, dtype=np.float32)[:, None] * inv_freq[None, :]
    cos, sin = np.cos(ang).astype(np.float32), np.sin(ang).astype(np.float32)
    cos_t = np.concatenate([cos, cos, cos, cos], axis=1)
    sin_t = np.concatenate([-sin, sin, -sin, sin], axis=1)
    return jnp.asarray(cos_t), jnp.asarray(sin_t)


def _swap_halves(xc):
    lane = lax.broadcasted_iota(jnp.int32, xc.shape, 1)
    return jnp.where((lane & 63) < 32, pltpu.roll(xc, 96, 1), pltpu.roll(xc, 32, 1))


_HBM = pl.BlockSpec(memory_space=pltpu.HBM)
_SEM = pl.BlockSpec(memory_space=pltpu.SEMAPHORE)
_ANY = pl.BlockSpec(memory_space=pl.ANY)
_EFFECT = pltpu.SideEffectType.DATAFLOW_SIDE_EFFECTING
TOKEN = jax.ShapeDtypeStruct((8, LANES), F32)


def _other_chips(x, y):
    return [(1 - x, y), (x, 1 - y), (1 - x, 1 - y)]


def _after(tokens):
    tokens = list(tokens)
    return tokens, [_ANY] * len(tokens)


def _gather_start(first, rest):
    shards = tuple(first) + tuple(rest)
    nw, nf = len(shards), len(first)

    def body(*refs):
        srcs, lands = refs[:nw], refs[nw:2 * nw]
        sems = refs[2 * nw:2 * nw + 4]
        stages = refs[2 * nw + 4:3 * nw + 4]
        lsem = refs[3 * nw + 4]
        x, y, c = lax.axis_index("x"), lax.axis_index("y"), lax.axis_index("c")
        j = 2 * x + y
        for k in range(nw):
            send, recv = sems[0:2] if k < nf else sems[2:4]
            chips = _other_chips(x, y) if k < nf else _other_chips(x, y)[:2]
            base = k * 3 if k < nf else (k - nf) * 2
            stages[k][...] = srcs[k][...].astype(BF16)
            local = pltpu.make_async_copy(stages[k], lands[k].at[j], lsem.at[k])
            local.start()
            local.wait()
            rows = shards[k].shape[0] // 2
            mine = lands[k].at[j, pl.ds(c * rows, rows)]
            for n, chip in enumerate(chips):
                pltpu.make_async_remote_copy(
                    src_ref=mine, dst_ref=mine, send_sem=send.at[base + n], recv_sem=recv.at[base + n],
                    device_id=(chip[0], chip[1], c), device_id_type=MESH).start()

    vmem = pl.BlockSpec(memory_space=pltpu.VMEM)
    nr = nw - nf
    res = pl.pallas_call(
        body, name="gather_start",
        out_shape=tuple(pltpu.HBM((N_CHIPS,) + s.shape, BF16) for s in shards) + (
            pltpu.SemaphoreType.DMA((3 * nf,)), pltpu.SemaphoreType.DMA((3 * nf,)),
            pltpu.SemaphoreType.DMA((2 * nr,)), pltpu.SemaphoreType.DMA((2 * nr,))),
        in_specs=[vmem] * nw, out_specs=tuple([_HBM] * nw) + (_SEM, _SEM, _SEM, _SEM),
        scratch_shapes=[pltpu.VMEM(s.shape, BF16) for s in shards] + [pltpu.SemaphoreType.DMA((nw,))],
        compiler_params=_cp(has_side_effects=_EFFECT),
    )(*shards)
    return (res[:nf], res[nw], res[nw + 1]), (res[nf:nw], res[nw + 2], res[nw + 3])


def _gather_wait(lands, send, recv, name, after=()):
    nw = len(lands)
    tokens, token_specs = _after(after)

    def body(*refs):
        ins = refs[:nw]
        send_ref, recv_ref = refs[nw], refs[nw + 1]
        x, y, c = lax.axis_index("x"), lax.axis_index("y"), lax.axis_index("c")
        j = 2 * x + y
        for k in range(nw):
            rows = lands[k].shape[1] // 2
            mine = ins[k].at[j, pl.ds(c * rows, rows)]
            for n, chip in enumerate(_other_chips(x, y)):
                got = ins[k].at[2 * chip[0] + chip[1], pl.ds(c * rows, rows)]
                cp = pltpu.make_async_remote_copy(
                    src_ref=mine, dst_ref=got, send_sem=send_ref.at[k * 3 + n], recv_sem=recv_ref.at[k * 3 + n],
                    device_id=(chip[0], chip[1], c), device_id_type=MESH)
                cp.wait_send()
                cp.wait_recv()

    return pl.pallas_call(
        body, name=name, out_shape=tuple(pltpu.HBM(a.shape, a.dtype) for a in lands),
        in_specs=[_HBM] * nw + [_SEM, _SEM] + token_specs, out_specs=tuple([_HBM] * nw),
        input_output_aliases={k: k for k in range(nw)},
        compiler_params=_cp(has_side_effects=_EFFECT),
    )(*lands, send, recv, *tokens)


def _relay_copies(refs, nw, rows_of, send, recv):
    x, y, c = lax.axis_index("x"), lax.axis_index("y"), lax.axis_index("c")
    xn, yn, dg = _other_chips(x, y)
    mine, theirs = [], []
    for k in range(nw):
        rows = rows_of[k]
        for n, (origin, target) in enumerate(((xn, yn), (yn, xn))):
            part = pl.ds(c * rows + n * (rows // 2), rows // 2)
            sent, got = refs[k].at[2 * origin[0] + origin[1], part], refs[k].at[2 * dg[0] + dg[1], part]
            mine.append(pltpu.make_async_remote_copy(
                src_ref=sent, dst_ref=sent, send_sem=send.at[k * 2 + n], recv_sem=recv.at[k * 2 + n],
                device_id=(target[0], target[1], c), device_id_type=MESH))
            theirs.append(pltpu.make_async_remote_copy(
                src_ref=got, dst_ref=got, send_sem=send.at[k * 2 + n], recv_sem=recv.at[k * 2 + n],
                device_id=(target[0], target[1], c), device_id_type=MESH))
    return mine, theirs


def _gather_relay(lands, send, recv, after):
    nw = len(lands)
    rows_of = [a.shape[1] // 2 for a in lands]

    def body(*refs):
        ins, send1, recv1 = refs[:nw], refs[nw], refs[nw + 1]
        send2, recv2, send3, recv3, token = refs[2 * nw + 3:2 * nw + 8]
        x, y, c = lax.axis_index("x"), lax.axis_index("y"), lax.axis_index("c")
        j = 2 * x + y
        for k in range(nw):
            mine = ins[k].at[j, pl.ds(c * rows_of[k], rows_of[k])]
            for n, chip in enumerate(_other_chips(x, y)[:2]):
                got = ins[k].at[2 * chip[0] + chip[1], pl.ds(c * rows_of[k], rows_of[k])]
                cp = pltpu.make_async_remote_copy(
                    src_ref=mine, dst_ref=got, send_sem=send1.at[k * 2 + n], recv_sem=recv1.at[k * 2 + n],
                    device_id=(chip[0], chip[1], c), device_id_type=MESH)
                cp.wait_send()
                cp.wait_recv()
        for cp in _relay_copies(ins, nw, rows_of, send2, recv2)[0] + _forward_copies(ins, nw, rows_of, send3, recv3, (0, 1))[0]:
            cp.start()
        token[...] = jnp.zeros_like(token)

    sems = pltpu.SemaphoreType.DMA((2 * nw,))
    res = pl.pallas_call(
        body, name="gather_relay",
        out_shape=tuple(pltpu.HBM(a.shape, a.dtype) for a in lands) + (sems, sems, sems, sems, TOKEN),
        in_specs=[_HBM] * nw + [_SEM, _SEM, _ANY],
        out_specs=tuple([_HBM] * nw) + (_SEM, _SEM, _SEM, _SEM, pl.BlockSpec(memory_space=pltpu.VMEM)),
        input_output_aliases={k: k for k in range(nw)},
        compiler_params=_cp(has_side_effects=_EFFECT),
    )(*lands, send, recv, after)
    return res[:nw], res[nw:nw + 2], res[nw + 2:nw + 4], res[nw + 4]


def _gather_relayed(lands, relay_sems, after):
    nw = len(lands)
    rows_of = [a.shape[1] // 2 for a in lands]

    def body(*refs):
        ins = refs[:nw]
        send3, recv3 = refs[2 * nw + 3:2 * nw + 5]
        mine, theirs = _relay_copies(ins, nw, rows_of, refs[nw], refs[nw + 1])
        for cp in mine:
            cp.wait_send()
        for cp in theirs:
            cp.wait_recv()
        for cp in _forward_copies(ins, nw, rows_of, send3, recv3, (2,))[0]:
            cp.start()

    sems = pltpu.SemaphoreType.DMA((nw,))
    res = pl.pallas_call(
        body, name="gather_relayed", out_shape=tuple(pltpu.HBM(a.shape, a.dtype) for a in lands) + (sems, sems),
        in_specs=[_HBM] * nw + [_SEM, _SEM, _ANY], out_specs=tuple([_HBM] * nw) + (_SEM, _SEM),
        input_output_aliases={k: k for k in range(nw)},
        compiler_params=_cp(has_side_effects=_EFFECT),
    )(*lands, *relay_sems, after)
    return res[:nw], res[nw:nw + 2]


def _gather_done(lands, near_sems, far_sems):
    nw = len(lands)
    rows_of = [a.shape[1] // 2 for a in lands]

    def body(*refs):
        ins = refs[:nw]
        for sems, which in ((refs[nw:nw + 2], (0, 1)), (refs[nw + 2:nw + 4], (2,))):
            mine, theirs = _forward_copies(ins, nw, rows_of, sems[0], sems[1], which)
            for cp in mine:
                cp.wait_send()
            for cp in theirs:
                cp.wait_recv()

    return pl.pallas_call(
        body, name="gather_done", out_shape=tuple(pltpu.HBM(a.shape, a.dtype) for a in lands),
        in_specs=[_HBM] * nw + [_SEM] * 4, out_specs=tuple([_HBM] * nw),
        input_output_aliases={k: k for k in range(nw)},
        compiler_params=_cp(has_side_effects=_EFFECT),
    )(*lands, *near_sems, *far_sems)


def _forward_now(lands, name):
    nw = len(lands)
    rows_of = [a.shape[1] // 2 for a in lands]

    def body(*refs):
        mine, theirs = _forward_copies(refs[nw:2 * nw], nw, rows_of, refs[2 * nw], refs[2 * nw + 1])
        for cp in mine:
            cp.start()
        for cp in theirs:
            cp.wait_recv()
        for cp in mine:
            cp.wait_send()

    return pl.pallas_call(
        body, name=name, out_shape=tuple(jax.ShapeDtypeStruct(a.shape, a.dtype) for a in lands),
        in_specs=[_HBM] * nw, out_specs=tuple([_HBM] * nw), input_output_aliases={k: k for k in range(nw)},
        scratch_shapes=[pltpu.SemaphoreType.DMA((3 * nw,)), pltpu.SemaphoreType.DMA((3 * nw,))],
        compiler_params=_cp(),
    )(*lands)


def _forward_copies(refs, nw, rows_of, send, recv, which=(0, 1, 2)):
    x, y, c = lax.axis_index("x"), lax.axis_index("y"), lax.axis_index("c")
    out = []
    for k in range(nw):
        rows = rows_of[k]
        for n, slot in enumerate(which):
            chip = _other_chips(x, y)[slot]
            got = refs[k].at[2 * chip[0] + chip[1], pl.ds(c * rows, rows)]
            theirs = refs[k].at[2 * chip[0] + chip[1], pl.ds((1 - c) * rows, rows)]
            sem = k * len(which) + n
            out.append(pltpu.make_async_remote_copy(
                src_ref=got, dst_ref=got, send_sem=send.at[sem], recv_sem=recv.at[sem],
                device_id=(x, y, 1 - c), device_id_type=MESH))
            out.append(pltpu.make_async_remote_copy(
                src_ref=theirs, dst_ref=theirs, send_sem=send.at[sem], recv_sem=recv.at[sem],
                device_id=(x, y, 1 - c), device_id_type=MESH))
    return out[0::2], out[1::2]


def _fwd_inproj(x, g1, w_in, cos_t, sin_t, tm):
    t_tok = x.shape[0]
    seq = cos_t.shape[0]
    per_seq = seq // tm

    def body(x_ref, g_ref, w_ref, cos_ref, sin_ref, q_ref, k_ref, v_ref, u_ref):
        xv = x_ref[...]
        r = lax.rsqrt(jnp.mean(xv * xv, axis=-1, keepdims=True) + EPS)
        h = (xv * r) * g_ref[...]
        proj = _mm_nt(h, w_ref[...])
        cos, sin = cos_ref[...], sin_ref[...]
        for cidx in range((ATTN_W + KV_W) // LANES):
            xc = proj[:, cidx * LANES:(cidx + 1) * LANES]
            rot = xc * cos + _swap_halves(xc) * sin
            if cidx < ATTN_W // LANES:
                q_ref[:, cidx * LANES:(cidx + 1) * LANES] = (rot * (HEAD_DIM ** -0.5)).astype(q_ref.dtype)
            else:
                k_ref[...] = rot.astype(k_ref.dtype)
        v_ref[...] = proj[:, ATTN_W + KV_W:ATTN_W + 2 * KV_W].astype(v_ref.dtype)
        u_ref[...] = proj[:, ATTN_W + 2 * KV_W:].astype(u_ref.dtype)

    row = lambda w: pl.BlockSpec((tm, w), lambda i: (i, 0))
    kv = row(KV_W)
    tab = pl.BlockSpec((tm, LANES), lambda i: (i % per_seq, 0))
    kv_shape = jax.ShapeDtypeStruct((t_tok, KV_W), MXU_DTYPE)
    return pl.pallas_call(
        body, name="fwd_inproj", grid=(t_tok // tm,),
        in_specs=[row(D_MODEL), _const((1, D_MODEL)), _resident((IN_W, D_MODEL)), tab, tab],
        out_specs=(row(ATTN_W), kv, kv, row(POOL_W)),
        out_shape=(jax.ShapeDtypeStruct((t_tok, ATTN_W), MXU_DTYPE), kv_shape, kv_shape,
                   jax.ShapeDtypeStruct((t_tok, POOL_W), MXU_DTYPE)),
        compiler_params=_cp(dimension_semantics=("parallel",)),
    )(x, g1, w_in, cos_t, sin_t)


MASKED = -1e30


def _attn_bias():
    b = lax.broadcasted_iota(jnp.int32, (2 * BLK, BLK), 0)
    a = lax.broadcasted_iota(jnp.int32, (2 * BLK, BLK), 1)
    own = (b >= BLK) & (b - BLK <= a)
    prev = (b < BLK) & (b > a)
    return jnp.stack([jnp.where(own, 0.0, MASKED), jnp.where(own | prev, 0.0, MASKED)]).astype(F32)


def _sink_rows(sinks):
    return jnp.repeat(sinks.reshape(N_KV_HEADS, Q_PER_KV), BLK, axis=1).reshape(N_KV_HEADS, 1, Q_PER_KV * BLK)


def _stack_heads(ref, r0, g):
    return jnp.concatenate(
        [ref[pl.ds(r0, BLK), (Q_PER_KV * g + h) * HEAD_DIM:(Q_PER_KV * g + h + 1) * HEAD_DIM] for h in range(Q_PER_KV)],
        axis=0)


def _kv_window(ref, p0, r0, g):
    sl = slice(g * HEAD_DIM, (g + 1) * HEAD_DIM)
    return jnp.concatenate([ref[pl.ds(p0, BLK), sl], ref[pl.ds(r0, BLK), sl]], axis=0)


def _scores_t(k_ref, q_ref, bias, p0, r0, g):
    kk = _kv_window(k_ref, p0, r0, g)
    qs = _stack_heads(q_ref, r0, g)
    st = _mm_nt(kk, qs) + jnp.concatenate([bias] * Q_PER_KV, axis=1)
    return st, kk, qs


def _head_rows(ref, r0, g):
    return jnp.concatenate([ref[pl.ds(Q_PER_KV * g + h, 1), pl.ds(r0, BLK)] for h in range(Q_PER_KV)], axis=1)


def _pairs_to_rows(xt):
    out = []
    for t in range(Q_PER_KV // 2):
        pair = jnp.concatenate([xt[:, (2 * t) * BLK:(2 * t + 1) * BLK], xt[:, (2 * t + 1) * BLK:(2 * t + 2) * BLK]], axis=0)
        out.append(pair.T)
    return out


def _shift_rows(x, k, seq):
    row = lax.broadcasted_iota(jnp.int32, x.shape, 0)
    if k > 0:
        return jnp.where(row >= k, pltpu.roll(x, k, 0), 0.0)
    return jnp.where(row < seq + k, pltpu.roll(x, seq + k, 0), 0.0)


def _window_sum(x, w, seq, forward):
    s, k = x, 1
    while k < w:
        s = s + _shift_rows(s, -k if forward else k, seq)
        k *= 2
    return s


def _inv_count(seq, w):
    pos = lax.broadcasted_iota(jnp.int32, (seq, 1), 0)
    return 1.0 / jnp.minimum(pos + 1, w).astype(F32)


def _fwd_attn(q, k, v, sink_rows, bias, seq, after=()):
    t_tok = q.shape[0]
    nblk = seq // BLK
    together = max(f for f in (4, 2, 1) if nblk % f == 0)
    tokens, token_specs = _after(after)

    def body(q_ref, k_ref, v_ref, sink_ref, bias_ref, *rest):
        o_ref, lse_ref = rest[len(tokens):]

        def blocks(ik, carry):
            units = []
            for s in range(together):
                i = together * ik + s
                r0 = pl.multiple_of(i * BLK, BLK)
                p0 = pl.multiple_of(jnp.maximum(i - 1, 0) * BLK, BLK)
                units += [(g, r0, p0, bias_ref[jnp.minimum(i, 1)]) for g in range(N_KV_HEADS)]
            sts = [_scores_t(k_ref, q_ref, bias, p0, r0, g)[0] for g, r0, p0, bias in units]
            ms = [jnp.maximum(jnp.max(st, axis=0, keepdims=True), sink_ref[u[0]]) for st, u in zip(sts, units)]
            ps = [jnp.exp(st - m).astype(MXU_DTYPE) for st, m in zip(sts, ms)]
            vvs = [_kv_window(v_ref, p0, r0, g) for g, r0, p0, _ in units]
            ots = [_mm_tn(jnp.concatenate([vv, jnp.ones_like(vv)], axis=1), p) for vv, p in zip(vvs, ps)]
            for (g, r0, _, _), m, ot in zip(units, ms, ots):
                den = ot[HEAD_DIM:HEAD_DIM + 1] + jnp.exp(sink_ref[g] - m)
                lse = m + jnp.log(den)
                for h in range(Q_PER_KV):
                    lse_ref[pl.ds(Q_PER_KV * g + h, 1), pl.ds(r0, BLK)] = lse[:, h * BLK:(h + 1) * BLK]
                for t, tile in enumerate(_pairs_to_rows(ot[:HEAD_DIM] * (1.0 / den))):
                    c0 = (Q_PER_KV * g + 2 * t) * HEAD_DIM
                    o_ref[pl.ds(r0, BLK), c0:c0 + LANES] = tile.astype(o_ref.dtype)
            return carry

        lax.fori_loop(0, nblk // together, blocks, 0)

    row = lambda w: pl.BlockSpec((seq, w), lambda i: (i, 0))
    kv = row(KV_W)
    return pl.pallas_call(
        body, name="fwd_attn", grid=(t_tok // seq,),
        in_specs=[row(ATTN_W), kv, kv, _const((N_KV_HEADS, 1, Q_PER_KV * BLK)), _const((2, 2 * BLK, BLK))] + token_specs,
        out_specs=(row(ATTN_W), pl.BlockSpec((N_Q_HEADS, seq), lambda i: (0, i))),
        out_shape=(jax.ShapeDtypeStruct((t_tok, ATTN_W), MXU_DTYPE), jax.ShapeDtypeStruct((N_Q_HEADS, t_tok), F32)),
        compiler_params=_cp(dimension_semantics=("parallel",)),
    )(q, k, v, sink_rows, bias, *tokens)


def _fwd_pool(u, w_pool, pool_scale, seq):
    t_tok = u.shape[0]

    def body(u_ref, wp_ref, sc_ref, o_ref):
        for gi, w in enumerate(POOL_WINDOWS):
            sl = slice(gi * POOL_G, (gi + 1) * POOL_G)
            ug = u_ref[:, sl].astype(F32)
            d = _window_sum(ug, w, seq, False) * _inv_count(seq, w) - ug
            o_ref[:, sl] = (_mm(d, wp_ref[gi]) * sc_ref[:, sl]).astype(o_ref.dtype)

    row = pl.BlockSpec((seq, POOL_W), lambda i: (i, 0))
    return pl.pallas_call(
        body, name="fwd_pool", grid=(t_tok // seq,),
        in_specs=[row, _const((N_POOL, POOL_G, POOL_G)), _const((1, POOL_W))], out_specs=row,
        out_shape=jax.ShapeDtypeStruct((t_tok, POOL_W), MXU_DTYPE),
        compiler_params=_cp(dimension_semantics=("parallel",)),
    )(u, w_pool, pool_scale)


def _rms_bwd(dy_g, xn, r):
    return r * (dy_g - xn * jnp.mean(dy_g * xn, axis=-1, keepdims=True))


def _mlp_fwd_bwd(x, attn, pool, target, w_out, w_up4, w_down4, g2, gf, tm):
    t_tok = x.shape[0]

    def body(x_ref, attn_ref, pool_ref, tgt_ref, wo_ref, wu_ref, wd_ref, g2_ref, gf_ref,
             h2_ref, a_ref, da_ref, dx2b_ref, dx1_ref, dx1b_ref, loss_ref, dgf_ref, dg2_ref):
        @pl.when(pl.program_id(0) == 0)
        def _():
            loss_ref[...] = jnp.zeros_like(loss_ref)
            dgf_ref[...] = jnp.zeros_like(dgf_ref)
            dg2_ref[...] = jnp.zeros_like(dg2_ref)

        x1 = x_ref[...] + (_mm(attn_ref[...], wo_ref[:ATTN_W]) + _mm(pool_ref[...], wo_ref[ATTN_W:]))
        r2 = lax.rsqrt(jnp.mean(x1 * x1, axis=-1, keepdims=True) + EPS)
        xn1 = x1 * r2
        g2 = g2_ref[...]
        h2 = (xn1 * g2).astype(MXU_DTYPE)
        h2_ref[...] = h2
        acc = jnp.zeros((tm, D_MODEL), F32)
        for j in range(N_CHIPS):
            a = _mm(h2, wu_ref[j])
            a_ref[:, j * FF_SHARD:(j + 1) * FF_SHARD] = a.astype(a_ref.dtype)
            acc = acc + _mm(jnp.square(jnp.maximum(a, 0.0)), wd_ref[j])
        x2 = x1 + acc
        r3 = lax.rsqrt(jnp.mean(x2 * x2, axis=-1, keepdims=True) + EPS)
        xn2 = x2 * r3
        gf_v = gf_ref[...]
        err = xn2 * gf_v - tgt_ref[...]
        part = jnp.sum(err * err) * (0.5 / D_MODEL)
        first = (lax.broadcasted_iota(jnp.int32, loss_ref.shape, 0) == 0) & (lax.broadcasted_iota(jnp.int32, loss_ref.shape, 1) == 0)
        loss_ref[...] += jnp.where(first, part, 0.0)
        dy = err * (1.0 / D_MODEL)
        dgf_ref[...] += jnp.sum(dy * xn2, axis=0, keepdims=True)
        dx2 = _rms_bwd(dy * gf_v, xn2, r3)
        dx2b = dx2.astype(MXU_DTYPE)
        dx2b_ref[...] = dx2b
        dh2 = jnp.zeros((tm, D_MODEL), F32)
        for j in range(N_CHIPS):
            sl = slice(j * FF_SHARD, (j + 1) * FF_SHARD)
            dhid = _mm_nt(dx2b, wd_ref[j])
            da = (dhid * (2.0 * jnp.maximum(a_ref[:, sl].astype(F32), 0.0))).astype(MXU_DTYPE)
            da_ref[:, sl] = da
            dh2 = dh2 + _mm_nt(da, wu_ref[j])
        dg2_ref[...] += jnp.sum(dh2 * xn1, axis=0, keepdims=True)
        dx1 = dx2 + _rms_bwd(dh2 * g2, xn1, r2)
        dx1_ref[...] = dx1
        dx1b_ref[...] = dx1.astype(dx1b_ref.dtype)

    row = lambda w: pl.BlockSpec((tm, w), lambda i: (i, 0))
    vec = jax.ShapeDtypeStruct((1, D_MODEL), F32)
    return pl.pallas_call(
        body, name="mlp_fwd_bwd", grid=(t_tok // tm,),
        in_specs=[row(D_MODEL), row(ATTN_W), row(POOL_W), row(D_MODEL), _resident((D_MODEL, D_MODEL)),
                  _resident((N_CHIPS, D_MODEL, FF_SHARD)), _resident((N_CHIPS, FF_SHARD, D_MODEL)),
                  _const((1, D_MODEL)), _const((1, D_MODEL))],
        out_specs=(row(D_MODEL), row(D_FF), row(D_FF), row(D_MODEL), row(D_MODEL), row(D_MODEL),
                   _const((8, LANES)), _const((1, D_MODEL)), _const((1, D_MODEL))),
        out_shape=(jax.ShapeDtypeStruct((t_tok, D_MODEL), MXU_DTYPE), jax.ShapeDtypeStruct((t_tok, D_FF), MXU_DTYPE),
                   jax.ShapeDtypeStruct((t_tok, D_FF), MXU_DTYPE), jax.ShapeDtypeStruct((t_tok, D_MODEL), MXU_DTYPE),
                   jax.ShapeDtypeStruct((t_tok, D_MODEL), F32), jax.ShapeDtypeStruct((t_tok, D_MODEL), MXU_DTYPE),
                   jax.ShapeDtypeStruct((8, LANES), F32), vec, vec),
        compiler_params=_cp(dimension_semantics=("arbitrary",)),
    )(x, attn, pool, target, w_out, w_up4, w_down4, g2, gf)


def _bwd_mlp_wgrads(h2, da, a, dx2b, tk):
    t_tok = h2.shape[0]

    def body(h2_ref, da_ref, a_ref, dx2b_ref, gup_ref, gdn_ref):
        @pl.when(pl.program_id(1) == 0)
        def _():
            gup_ref[...] = jnp.zeros_like(gup_ref)
            gdn_ref[...] = jnp.zeros_like(gdn_ref)

        gup_ref[0] += _mm_tn(h2_ref[...], da_ref[...])
        hid = jnp.square(jnp.maximum(a_ref[...].astype(F32), 0.0))
        gdn_ref[0] += _mm_tn(hid, dx2b_ref[...])

    tok = pl.BlockSpec((tk, D_MODEL), lambda j, t: (t, 0))
    ffb = pl.BlockSpec((tk, FF_SHARD), lambda j, t: (t, j))
    wblk = pl.BlockSpec((1, D_MODEL, D_MODEL), lambda j, t: (j, 0, 0))
    return pl.pallas_call(
        body, name="bwd_mlp_wgrads", grid=(N_CHIPS, t_tok // tk),
        in_specs=[tok, ffb, ffb, tok], out_specs=(wblk, wblk),
        out_shape=(jax.ShapeDtypeStruct((N_CHIPS, D_MODEL, FF_SHARD), F32), jax.ShapeDtypeStruct((N_CHIPS, FF_SHARD, D_MODEL), F32)),
        compiler_params=_cp(dimension_semantics=("parallel", "arbitrary")),
    )(h2, da, a, dx2b)


def _head_selector():
    ch = lax.broadcasted_iota(jnp.int32, (ATTN_W, LANES), 0)
    col = lax.broadcasted_iota(jnp.int32, (ATTN_W, LANES), 1)
    return (ch // HEAD_DIM == col).astype(MXU_DTYPE)


def _bwd_outproj(dx1b, attn, pool, w_out, head_sel, tm, after=()):
    t_tok = dx1b.shape[0]
    tokens, token_specs = _after(after)

    def body(dx_ref, attn_ref, pool_ref, wo_ref, sel_ref, *rest):
        dattn_ref, dpool_ref, delta_ref, gwo_ref = rest[len(tokens):]

        @pl.when(pl.program_id(0) == 0)
        def _():
            gwo_ref[...] = jnp.zeros_like(gwo_ref)

        dx = dx_ref[...]
        attn = attn_ref[...]
        dattn = _mm_nt(dx, wo_ref[:ATTN_W])
        dattn_ref[...] = dattn.astype(dattn_ref.dtype)
        dpool_ref[...] = _mm_nt(dx, wo_ref[ATTN_W:]).astype(dpool_ref.dtype)
        prod = dattn * attn.astype(F32)
        hi = prod.astype(MXU_DTYPE)
        lo = prod - hi.astype(F32)
        delta = _mm(hi, sel_ref[...]) + _mm(lo, sel_ref[...])
        delta_ref[...] = delta.T[:N_Q_HEADS]
        gwo_ref[:ATTN_W] += _mm_tn(attn, dx)
        gwo_ref[ATTN_W:] += _mm_tn(pool_ref[...], dx)

    row = lambda w: pl.BlockSpec((tm, w), lambda i: (i, 0))
    return pl.pallas_call(
        body, name="bwd_outproj", grid=(t_tok // tm,),
        in_specs=[row(D_MODEL), row(ATTN_W), row(POOL_W), _resident((D_MODEL, D_MODEL)), _const((ATTN_W, LANES))] + token_specs,
        out_specs=(row(ATTN_W), row(POOL_W), pl.BlockSpec((N_Q_HEADS, tm), lambda i: (0, i)), _const((D_MODEL, D_MODEL))),
        out_shape=(jax.ShapeDtypeStruct((t_tok, ATTN_W), MXU_DTYPE), jax.ShapeDtypeStruct((t_tok, POOL_W), MXU_DTYPE),
                   jax.ShapeDtypeStruct((N_Q_HEADS, t_tok), F32), jax.ShapeDtypeStruct((D_MODEL, D_MODEL), F32)),
        compiler_params=_cp(dimension_semantics=("arbitrary",)),
    )(dx1b, attn, pool, w_out, head_sel, *tokens)


def _bwd_attn(q, k, v, dattn, lse, delta, sink_rows, bias, cos_t, sin_t, seq, after=()):
    t_tok = q.shape[0]
    nblk = seq // BLK
    together = max(f for f in (4, 2, 1) if nblk % f == 0)
    qkv_w = ATTN_W + 2 * KV_W
    tokens, token_specs = _after(after)

    def unrope(d, cos, sin):
        return d * cos - _swap_halves(d) * sin

    def body(q_ref, k_ref, v_ref, do_ref, lse_ref, delta_ref, sink_ref, bias_ref, cos_ref, sin_ref, *rest):
        dqkv_ref, dsink_ref, dk_acc, dv_acc = rest[len(tokens):]

        @pl.when(pl.program_id(0) == 0)
        def _():
            dsink_ref[...] = jnp.zeros_like(dsink_ref)

        dk_acc[...] = jnp.zeros_like(dk_acc)
        dv_acc[...] = jnp.zeros_like(dv_acc)

        def blocks(ik, dsink):
            units = []
            for s in range(together):
                i = together * ik + s
                r0 = pl.multiple_of(i * BLK, BLK)
                p0 = pl.multiple_of(jnp.maximum(i - 1, 0) * BLK, BLK)
                units += [(g, r0, p0, bias_ref[jnp.minimum(i, 1)]) for g in range(N_KV_HEADS)]
            scores = [_scores_t(k_ref, q_ref, bias, p0, r0, g) for g, r0, p0, bias in units]
            doss = [_stack_heads(do_ref, r0, g) for g, r0, _, _ in units]
            dpts = [_mm_nt(_kv_window(v_ref, p0, r0, g), dos) for (g, r0, p0, _), dos in zip(units, doss)]
            lses = [_head_rows(lse_ref, r0, g) for g, r0, _, _ in units]
            deltas = [_head_rows(delta_ref, r0, g) for g, r0, _, _ in units]
            pns = [jnp.exp(sc[0] - lse) for sc, lse in zip(scores, lses)]
            dsts = [pn * (dpt - delta) for pn, dpt, delta in zip(pns, dpts, deltas)]
            dsink = list(dsink)
            for (g, _, _, _), lse, delta in zip(units, lses, deltas):
                dsink[g] = dsink[g] - jnp.exp(sink_ref[g] - lse) * delta
            dqts = [_mm_tn(sc[1], dst) * (HEAD_DIM ** -0.5) for sc, dst in zip(scores, dsts)]
            dkks = [_mm(dst, sc[2]) for sc, dst in zip(scores, dsts)]
            dvvs = [_mm(pn, dos) for pn, dos in zip(pns, doss)]
            for (g, r0, p0, _), dqt, dkk, dvv in zip(units, dqts, dkks, dvvs):
                cos, sin = cos_ref[pl.ds(r0, BLK), :], sin_ref[pl.ds(r0, BLK), :]
                for t, tile in enumerate(_pairs_to_rows(dqt)):
                    c0 = (Q_PER_KV * g + 2 * t) * HEAD_DIM
                    dqkv_ref[pl.ds(r0, BLK), c0:c0 + LANES] = unrope(tile, cos, sin).astype(dqkv_ref.dtype)
                dk_acc[g, pl.ds(p0, BLK), :] += dkk[:BLK]
                dk_acc[g, pl.ds(r0, BLK), :] += dkk[BLK:]
                dv_acc[g, pl.ds(p0, BLK), :] += dvv[:BLK]
                dv_acc[g, pl.ds(r0, BLK), :] += dvv[BLK:]
            return tuple(dsink)

        zero = jnp.zeros((1, Q_PER_KV * BLK), F32)
        dsink = lax.fori_loop(0, nblk // together, blocks, (zero,) * N_KV_HEADS)
        lane = lax.broadcasted_iota(jnp.int32, dsink_ref.shape, 1)
        row = lax.broadcasted_iota(jnp.int32, dsink_ref.shape, 0)
        tile = jnp.zeros(dsink_ref.shape, F32)
        for g in range(N_KV_HEADS):
            for h in range(Q_PER_KV):
                tot = jnp.sum(dsink[g][:, h * BLK:(h + 1) * BLK])
                tile = tile + jnp.where((row == 0) & (lane == Q_PER_KV * g + h), tot, 0.0)
        dsink_ref[...] += tile
        dk = jnp.concatenate([dk_acc[g] for g in range(N_KV_HEADS)], axis=1)
        dqkv_ref[:, ATTN_W:ATTN_W + KV_W] = unrope(dk, cos_ref[...], sin_ref[...]).astype(dqkv_ref.dtype)
        dqkv_ref[:, ATTN_W + KV_W:] = jnp.concatenate([dv_acc[g] for g in range(N_KV_HEADS)], axis=1).astype(dqkv_ref.dtype)

    row = lambda w: pl.BlockSpec((seq, w), lambda i: (i, 0))
    kv = row(KV_W)
    per_head = pl.BlockSpec((N_Q_HEADS, seq), lambda i: (0, i))
    return pl.pallas_call(
        body, name="bwd_attn", grid=(t_tok // seq,),
        in_specs=[row(ATTN_W), kv, kv, row(ATTN_W), per_head, per_head, _const((N_KV_HEADS, 1, Q_PER_KV * BLK)),
                  _const((2, 2 * BLK, BLK)), _resident((seq, LANES)), _resident((seq, LANES))] + token_specs,
        out_specs=(row(qkv_w), _const((8, LANES))),
        out_shape=(jax.ShapeDtypeStruct((t_tok, qkv_w), MXU_DTYPE), jax.ShapeDtypeStruct((8, LANES), F32)),
        scratch_shapes=[pltpu.VMEM((N_KV_HEADS, seq, HEAD_DIM), F32), pltpu.VMEM((N_KV_HEADS, seq, HEAD_DIM), F32)],
        compiler_params=_cp(dimension_semantics=("arbitrary",)),
    )(q, k, v, dattn, lse, delta, sink_rows, bias, cos_t, sin_t, *tokens)


def _bwd_pool(u, dpool, w_pool, pool_scale, seq):
    t_tok = u.shape[0]

    def body(u_ref, dp_ref, wp_ref, sc_ref, du_ref, dwp_ref, dsc_ref):
        @pl.when(pl.program_id(0) == 0)
        def _():
            dwp_ref[...] = jnp.zeros_like(dwp_ref)
            dsc_ref[...] = jnp.zeros_like(dsc_ref)

        for gi, w in enumerate(POOL_WINDOWS):
            sl = slice(gi * POOL_G, (gi + 1) * POOL_G)
            ug = u_ref[:, sl].astype(F32)
            inv = _inv_count(seq, w)
            d = (_window_sum(ug, w, seq, False) * inv - ug).astype(MXU_DTYPE)
            y = _mm(d, wp_ref[gi])
            dpool = dp_ref[:, sl].astype(F32)
            dsc_ref[:, sl] += jnp.sum(y * dpool, axis=0, keepdims=True)
            dy = (dpool * sc_ref[:, sl]).astype(MXU_DTYPE)
            dwp_ref[gi] += _mm_tn(d, dy)
            dd = _mm_nt(dy, wp_ref[gi])
            du_ref[:, sl] = (_window_sum(dd * inv, w, seq, True) - dd).astype(du_ref.dtype)

    row = pl.BlockSpec((seq, POOL_W), lambda i: (i, 0))
    return pl.pallas_call(
        body, name="bwd_pool", grid=(t_tok // seq,),
        in_specs=[row, row, _const((N_POOL, POOL_G, POOL_G)), _const((1, POOL_W))],
        out_specs=(row, _const((N_POOL, POOL_G, POOL_G)), _const((1, POOL_W))),
        out_shape=(jax.ShapeDtypeStruct((t_tok, POOL_W), MXU_DTYPE), jax.ShapeDtypeStruct((N_POOL, POOL_G, POOL_G), F32),
                   jax.ShapeDtypeStruct((1, POOL_W), F32)),
        compiler_params=_cp(dimension_semantics=("arbitrary",)),
    )(u, dpool, w_pool, pool_scale)


def _bwd_inproj(dqkv, du, x, dx1, w_in, g1, tm):
    t_tok = x.shape[0]
    nsteps = t_tok // tm

    qkv_w = ATTN_W + 2 * KV_W

    def body(dqkv_ref, du_ref, x_ref, dx1_ref, w_ref, g_ref, gx_ref, gw_ref, dg_ref):
        @pl.when(pl.program_id(0) == 0)
        def _():
            gw_ref[...] = jnp.zeros_like(gw_ref)
            dg_ref[...] = jnp.zeros_like(dg_ref)

        dqkv, du = dqkv_ref[...], du_ref[...]
        xv = x_ref[...]
        r = lax.rsqrt(jnp.mean(xv * xv, axis=-1, keepdims=True) + EPS)
        xn = xv * r
        g = g_ref[...]
        dh = _mm(dqkv, w_ref[:qkv_w]) + _mm(du, w_ref[qkv_w:])
        dg_ref[...] += jnp.sum(dh * xn, axis=0, keepdims=True)
        gx_ref[...] = dx1_ref[...] + _rms_bwd(dh * g, xn, r)
        h = (xn * g).astype(MXU_DTYPE)
        gw_ref[:qkv_w] += _mm_tn(dqkv, h)
        gw_ref[qkv_w:] += _mm_tn(du, h)

    row = lambda w: pl.BlockSpec((tm, w), lambda i: (i, 0))
    return pl.pallas_call(
        body, name="bwd_inproj", grid=(nsteps,),
        in_specs=[row(qkv_w), row(POOL_W), row(D_MODEL), row(D_MODEL), _resident((IN_W, D_MODEL)), _const((1, D_MODEL))],
        out_specs=(row(D_MODEL), _const((IN_W, D_MODEL)), _const((1, D_MODEL))),
        out_shape=(jax.ShapeDtypeStruct((t_tok, D_MODEL), F32), jax.ShapeDtypeStruct((IN_W, D_MODEL), F32),
                   jax.ShapeDtypeStruct((1, D_MODEL), F32)),
        compiler_params=_cp(dimension_semantics=("arbitrary",)),
    )(dqkv, du, x, dx1, w_in, g1)


class _NoComm:
    def pool_done(self, pool):
        return ()

    def rest_of_weights(self, attn):
        raise NotImplementedError

    def mlp_grads_ready(self, gw_up4, gw_down4):
        return ()

    def outproj_done(self, gw_out):
        return ()


def _local_step(x, target, w_in, comm, g1, sinks, w_pool, pool_scale, g2, gf, seq):
    tm_big, tm_mlp = min(TILE_BIG, seq), min(TILE_MLP, seq)
    cos_t, sin_t = _rope_tables(seq)
    sink_rows, bias = _sink_rows(sinks), _attn_bias()
    q, k, v, u = _fwd_inproj(x, g1, w_in, cos_t, sin_t, tm_big)
    pool = _fwd_pool(u, w_pool, pool_scale, seq)
    attn, lse = _fwd_attn(q, k, v, sink_rows, bias, seq, comm.pool_done(pool))
    w_out, w_up4, w_down4 = comm.rest_of_weights(attn)
    h2, a, da, dx2b, dx1, dx1b, loss, dgf, dg2 = _mlp_fwd_bwd(x, attn, pool, target, w_out, w_up4, w_down4, g2, gf, tm_mlp)
    gw_up4, gw_down4 = _bwd_mlp_wgrads(h2, da, a, dx2b, min(TILE_BIG, x.shape[0]))
    dattn, dpool, delta, gw_out = _bwd_outproj(dx1b, attn, pool, w_out, _head_selector(), tm_big,
                                               comm.mlp_grads_ready(gw_up4, gw_down4))
    dqkv, dsinks = _bwd_attn(q, k, v, dattn, lse, delta, sink_rows, bias, cos_t, sin_t, seq, comm.outproj_done(gw_out))
    du, dwp, dsc = _bwd_pool(u, dpool, w_pool, pool_scale, seq)
    gx, gw_in_t, dg1 = _bwd_inproj(dqkv, du, x, dx1, w_in, g1, tm_big)
    big = (gw_in_t.reshape(N_CHIPS, IN_SHARD, D_MODEL), gw_out.reshape(N_CHIPS, OUT_SHARD, D_MODEL), gw_up4, gw_down4)
    small = (dg1, dsinks, dwp.reshape(N_POOL * POOL_G, POOL_G), dsc, dg2, dgf, loss)
    return gx, big, small


def _sibling_swap(arrs, name):
    n = len(arrs)

    def body(*refs):
        cps = _sibling_copies(refs[:n], refs[n:2 * n], refs[2 * n], refs[2 * n + 1], False)
        for cp in cps:
            cp.start()
        for cp in cps:
            cp.wait()

    return pl.pallas_call(
        body, name=name, out_shape=tuple(jax.ShapeDtypeStruct(a.shape, a.dtype) for a in arrs),
        in_specs=[_HBM] * n, out_specs=tuple([_HBM] * n),
        scratch_shapes=[pltpu.SemaphoreType.DMA((n,)), pltpu.SemaphoreType.DMA((n,))],
        compiler_params=_cp(),
    )(*arrs)


def _sibling_copies(srcs, lands, send, recv, pick_half):
    x, y, c = lax.axis_index("x"), lax.axis_index("y"), lax.axis_index("c")
    cps = []
    for i in range(len(srcs)):
        src = srcs[i]
        if pick_half:
            h = src.shape[1] // 2
            src = src.at[:, pl.ds((1 - c) * h, h)]
        cps.append(pltpu.make_async_remote_copy(
            src_ref=src, dst_ref=lands[i], send_sem=send.at[i], recv_sem=recv.at[i],
            device_id=(x, y, 1 - c), device_id_type=MESH))
    return cps


def _sibling_start(arrs, name, pick_half=True):
    n = len(arrs)

    def body(*refs):
        send, recv, token = refs[3 * n:3 * n + 3]
        for cp in _sibling_copies(refs[:n], refs[2 * n:3 * n], send, recv, pick_half):
            cp.start()
        token[...] = jnp.zeros_like(token)

    half = lambda a: ((a.shape[0], a.shape[1] // 2) + a.shape[2:]) if pick_half else a.shape
    res = pl.pallas_call(
        body, name=name + "_start",
        out_shape=tuple(pltpu.HBM(a.shape, a.dtype) for a in arrs) + tuple(pltpu.HBM(half(a), a.dtype) for a in arrs) + (
            pltpu.SemaphoreType.DMA((n,)), pltpu.SemaphoreType.DMA((n,)), TOKEN),
        in_specs=[_HBM] * n, out_specs=tuple([_HBM] * (2 * n)) + (_SEM, _SEM, pl.BlockSpec(memory_space=pltpu.VMEM)),
        input_output_aliases={i: i for i in range(n)},
        compiler_params=_cp(has_side_effects=_EFFECT),
    )(*[pltpu.with_memory_space_constraint(a, pltpu.HBM) for a in arrs])
    return res[:n], res[n:2 * n], res[2 * n], res[2 * n + 1], res[2 * n + 2]


def _sibling_wait(arrs, lands, send, recv, after, name, pick_half=True):
    n = len(arrs)

    def body(*refs):
        for cp in _sibling_copies(refs[:n], refs[n:2 * n], refs[2 * n], refs[2 * n + 1], pick_half):
            cp.wait_send()
            cp.wait_recv()

    res = pl.pallas_call(
        body, name=name + "_wait", out_shape=tuple(pltpu.HBM(a.shape, a.dtype) for a in list(arrs) + list(lands)),
        in_specs=[_HBM] * (2 * n) + [_SEM, _SEM, _ANY], out_specs=tuple([_HBM] * (2 * n)),
        input_output_aliases={i: i for i in range(2 * n)},
        compiler_params=_cp(has_side_effects=_EFFECT),
    )(*arrs, *lands, send, recv, after)
    return res[:n], res[n:]


def _row_block(rows):
    if rows <= 256:
        return rows
    for cand in (256, 128, 64, 32, 16, 8):
        if rows % cand == 0:
            return cand
    raise ValueError(rows)


def _chip_partial(g4s, r4s, c_arr, name):
    n = len(g4s)
    _, rows, cols = r4s[0].shape
    rb = _row_block(rows)
    nb = rows // rb

    def body(c_ref, *refs):
        for k in range(n):
            refs[2 * n + k][...] = (refs[k][...] + refs[n + k][...]).astype(BF16)

    own = pl.BlockSpec((1, rb, cols), lambda s, i, c: (s, c[0] * nb + i, 0))
    blk = pl.BlockSpec((1, rb, cols), lambda s, i, c: (s, i, 0))
    return pl.pallas_call(
        body, name=name,
        grid_spec=pltpu.PrefetchScalarGridSpec(num_scalar_prefetch=1, grid=(N_CHIPS, nb),
                                               in_specs=[own] * n + [blk] * n, out_specs=tuple([blk] * n)),
        out_shape=tuple(jax.ShapeDtypeStruct(r.shape, BF16) for r in r4s),
        compiler_params=_cp(dimension_semantics=("parallel", "parallel")),
    )(c_arr, *g4s, *r4s)


def _send_start(parts, name):
    n = len(parts)

    def body(*refs):
        srcs, lands = refs[:n], refs[2 * n:3 * n]
        send, recv, token = refs[3 * n:3 * n + 3]
        x, y, c = lax.axis_index("x"), lax.axis_index("y"), lax.axis_index("c")
        for i in range(n):
            for m, chip in enumerate(_other_chips(x, y)):
                pltpu.make_async_remote_copy(
                    src_ref=srcs[i].at[2 * chip[0] + chip[1]], dst_ref=lands[i].at[m],
                    send_sem=send.at[3 * i + m], recv_sem=recv.at[3 * i + m],
                    device_id=(chip[0], chip[1], c), device_id_type=MESH).start()
        token[...] = jnp.zeros_like(token)

    res = pl.pallas_call(
        body, name=name,
        out_shape=tuple(pltpu.HBM(p.shape, p.dtype) for p in parts) + tuple(pltpu.HBM((3,) + p.shape[1:], p.dtype) for p in parts) + (
            pltpu.SemaphoreType.DMA((3 * n,)), pltpu.SemaphoreType.DMA((3 * n,)), TOKEN),
        in_specs=[_HBM] * n, out_specs=tuple([_HBM] * (2 * n)) + (_SEM, _SEM, pl.BlockSpec(memory_space=pltpu.VMEM)),
        input_output_aliases={i: i for i in range(n)},
        compiler_params=_cp(has_side_effects=_EFFECT),
    )(*[pltpu.with_memory_space_constraint(p, pltpu.HBM) for p in parts])
    return res[:n], res[n:2 * n], res[2 * n], res[2 * n + 1], res[2 * n + 2]


def _send_wait(parts, lands, send, recv, after, name):
    n = len(parts)

    def body(*refs):
        srcs, ins = refs[:n], refs[n:2 * n]
        send_ref, recv_ref = refs[2 * n], refs[2 * n + 1]
        x, y, c = lax.axis_index("x"), lax.axis_index("y"), lax.axis_index("c")
        for i in range(n):
            for m, chip in enumerate(_other_chips(x, y)):
                cp = pltpu.make_async_remote_copy(
                    src_ref=srcs[i].at[2 * chip[0] + chip[1]], dst_ref=ins[i].at[m],
                    send_sem=send_ref.at[3 * i + m], recv_sem=recv_ref.at[3 * i + m],
                    device_id=(chip[0], chip[1], c), device_id_type=MESH)
                cp.wait_send()
                cp.wait_recv()

    res = pl.pallas_call(
        body, name=name, out_shape=tuple(pltpu.HBM(a.shape, a.dtype) for a in list(parts) + list(lands)),
        in_specs=[_HBM] * (2 * n) + [_SEM, _SEM, _ANY], out_specs=tuple([_HBM] * (2 * n)),
        input_output_aliases={i: i for i in range(2 * n)},
        compiler_params=_cp(has_side_effects=_EFFECT),
    )(*parts, *lands, send, recv, after)
    return res[n:]


def _final_half(g4s, r4s, got3s, jc_arr, name):
    n = len(g4s)
    _, rows, cols = r4s[0].shape
    rb = _row_block(rows)
    nb = rows // rb

    def body(jc_ref, *refs):
        for k in range(n):
            g_ref, r_ref, p_ref, o_ref = refs[k], refs[n + k], refs[2 * n + k], refs[3 * n + k]
            own = g_ref[0] + r_ref[0]
            o_ref[...] = ((own + p_ref[0].astype(F32)) + p_ref[1].astype(F32)) + p_ref[2].astype(F32)

    own = pl.BlockSpec((1, rb, cols), lambda i, jc: (jc[0], jc[1] * nb + i, 0))
    sib = pl.BlockSpec((1, rb, cols), lambda i, jc: (jc[0], i, 0))
    got = pl.BlockSpec((3, rb, cols), lambda i, jc: (0, i, 0))
    out = pl.BlockSpec((rb, cols), lambda i, jc: (i, 0))
    return pl.pallas_call(
        body, name=name,
        grid_spec=pltpu.PrefetchScalarGridSpec(num_scalar_prefetch=1, grid=(nb,),
                                               in_specs=[own] * n + [sib] * n + [got] * n, out_specs=tuple([out] * n)),
        out_shape=tuple(jax.ShapeDtypeStruct((rows, cols), F32) for _ in range(n)),
        compiler_params=_cp(dimension_semantics=("parallel",)),
    )(jc_arr, *g4s, *r4s, *got3s)


def _adamw_math(w, g, m, v):
    m2 = ADAM_B1 * m + (1.0 - ADAM_B1) * g
    v2 = ADAM_B2 * v + (1.0 - ADAM_B2) * (g * g)
    m_hat = m2 / (1.0 - ADAM_B1 ** ADAM_STEP)
    v_hat = v2 / (1.0 - ADAM_B2 ** ADAM_STEP)
    delta = -ADAM_LR * (m_hat / (jnp.sqrt(v_hat) + ADAM_EPS) + ADAM_WD * w)
    return delta, m2, v2


def _adamw_shard(mines, others, ws, ms, vs, c_arr, name):
    n = len(ws)
    rows, cols = ws[0].shape
    half = rows // 2
    rb = _row_block(half)
    nb = half // rb

    def body(c_ref, *refs):
        for k in range(n):
            a_ref, b_ref, w_ref, m_ref, v_ref = (refs[i * n + k] for i in range(5))
            g_ref, d_ref, m2_ref, v2_ref = refs[5 * n + 4 * k:5 * n + 4 * k + 4]
            g = jnp.where(pl.program_id(0) == c_ref[0], a_ref[...], b_ref[...])
            delta, m2, v2 = _adamw_math(w_ref[...], g, m_ref[...], v_ref[...])
            g_ref[...] = g
            d_ref[...] = delta
            m2_ref[...] = m2
            v2_ref[...] = v2

    hb = pl.BlockSpec((rb, cols), lambda h, i, c: (i, 0))
    fb = pl.BlockSpec((rb, cols), lambda h, i, c: (h * nb + i, 0))
    shp = jax.ShapeDtypeStruct((rows, cols), F32)
    res = pl.pallas_call(
        body, name=name,
        grid_spec=pltpu.PrefetchScalarGridSpec(num_scalar_prefetch=1, grid=(2, nb),
                                               in_specs=[hb] * (2 * n) + [fb] * (3 * n), out_specs=tuple([fb] * (4 * n))),
        out_shape=tuple([shp] * (4 * n)),
        compiler_params=_cp(dimension_semantics=("parallel", "parallel")),
    )(c_arr, *mines, *others, *ws, *ms, *vs)
    return [res[4 * k:4 * k + 4] for k in range(n)]


def _small_allreduce(parts):
    n = len(parts)

    def body(*refs):
        p_refs, accs = refs[:n], refs[n:2 * n]
        bufs = refs[2 * n:3 * n]
        send, recv = refs[3 * n:]
        x, y, c = lax.axis_index("x"), lax.axis_index("y"), lax.axis_index("c")
        partners = [(x, y, 1 - c), (1 - x, y, c), (x, 1 - y, c)]
        other_way = [partners[0], partners[2], partners[1]]
        for i in range(n):
            accs[i][...] = p_refs[i][...]
        for s in range(3):
            cps = []
            for i in range(n):
                rows = parts[i].shape[0]
                pieces = [(slice(None), partners[s])] if rows < 2 * BLK else [
                    (pl.ds(0, rows // 2), partners[s]), (pl.ds(rows // 2, rows // 2), other_way[s])]
                for h, (sl, partner) in enumerate(pieces):
                    cp = pltpu.make_async_remote_copy(
                        src_ref=accs[i].at[sl], dst_ref=bufs[i].at[s, sl], send_sem=send.at[6 * i + 2 * s + h],
                        recv_sem=recv.at[6 * i + 2 * s + h], device_id=partner, device_id_type=MESH)
                    cp.start()
                    cps.append(cp)
            for cp in cps:
                cp.wait()
            for i in range(n):
                accs[i][...] = accs[i][...] + bufs[i][s]

    vmem = pl.BlockSpec(memory_space=pltpu.VMEM)
    return pl.pallas_call(
        body, name="small_allreduce", out_shape=tuple(jax.ShapeDtypeStruct(p.shape, F32) for p in parts),
        in_specs=[vmem] * n, out_specs=tuple([vmem] * n),
        scratch_shapes=[pltpu.VMEM((3,) + p.shape, F32) for p in parts] + [
            pltpu.SemaphoreType.DMA((6 * n,)), pltpu.SemaphoreType.DMA((6 * n,))],
        compiler_params=_cp(),
    )(*parts)


def _small_adamw(reduced, params):
    n = len(reduced)
    n_w = len(params)

    def body(*refs):
        r_refs = refs[:n]
        wmv = refs[n:n + 3 * n_w]
        outs = refs[n + 3 * n_w:]
        outs[0][...] = r_refs[n - 1][0:1, 0:1]
        grads = [r_refs[0][...], r_refs[1][0:1, 0:N_Q_HEADS]] + [r_refs[i][...] for i in range(2, n_w)]
        for i in range(n_w):
            w_ref, m_ref, v_ref = wmv[3 * i:3 * i + 3]
            g_ref, d_ref, m2_ref, v2_ref = outs[1 + 4 * i:5 + 4 * i]
            delta, m2, v2 = _adamw_math(w_ref[...], grads[i], m_ref[...], v_ref[...])
            g_ref[...] = grads[i]
            d_ref[...] = delta
            m2_ref[...] = m2
            v2_ref[...] = v2

    flat = [a for p in params for a in p]
    vmem = pl.BlockSpec(memory_space=pltpu.VMEM)
    out_shape = [jax.ShapeDtypeStruct((1, 1), F32)]
    for p in params:
        out_shape += [jax.ShapeDtypeStruct(p[0].shape, F32)] * 4
    res = pl.pallas_call(
        body, name="small_adamw", out_shape=tuple(out_shape),
        in_specs=[vmem] * (n + len(flat)), out_specs=tuple([vmem] * len(out_shape)),
        compiler_params=_cp(),
    )(*reduced, *flat)
    return res[0], [res[1 + 4 * i:5 + 4 * i] for i in range(n_w)]


def kernel(x, attn_norm_g, w_in, attn_sinks, w_pool, pool_scale, w_out, mlp_norm_g, w_up, w_down, final_norm_g, loss_target, m_attn_norm_g, m_w_in, m_attn_sinks, m_w_pool, m_pool_scale, m_w_out, m_mlp_norm_g, m_w_up, m_w_down, m_final_norm_g, v_attn_norm_g, v_w_in, v_attn_sinks, v_w_pool, v_pool_scale, v_w_out, v_mlp_norm_g, v_w_up, v_w_down, v_final_norm_g):
    nseq, seq, d = x.shape
    c_idx = lax.axis_index("c").astype(jnp.int32)
    j_idx = (2 * lax.axis_index("x") + lax.axis_index("y")).astype(jnp.int32)
    c_arr = jnp.reshape(c_idx, (1,))
    jc_arr = jnp.stack([j_idx, c_idx])

    big_w = (w_in[0].T, w_out[0], w_up[0], w_down[0])
    big_m = (m_w_in[0].T, m_w_out[0], m_w_up[0], m_w_down[0])
    big_v = (v_w_in[0].T, v_w_out[0], v_w_up[0], v_w_down[0])
    (w_in_lands, w_in_send, w_in_recv), rest = _gather_start(big_w[:1], big_w[1:])
    (w_in4,) = _forward_now(_gather_wait(w_in_lands, w_in_send, w_in_recv, "gather_wait_w_in"), "forward_w_in")
    w_in_full = w_in4.reshape(IN_W, D_MODEL)
    class Comm(_NoComm):
        def pool_done(self, pool):
            self.lands, self.relay_sems, self.near_sems, token = _gather_relay(*rest, pool)
            return (token,)

        def rest_of_weights(self, attn):
            lands, far_sems = _gather_relayed(self.lands, self.relay_sems, attn)
            w_out4, w_up4, w_down4 = _gather_done(lands, self.near_sems, far_sems)
            return w_out4.reshape(D_MODEL, D_MODEL), w_up4, w_down4

        def mlp_grads_ready(self, gw_up4, gw_down4):
            self.mlp = _sibling_start((gw_up4, gw_down4), "mlp_grads_to_sibling")
            return (self.mlp[4],)

        def outproj_done(self, gw_out):
            grads, lands, send, recv, _ = self.mlp
            sib_out = _sibling_start((gw_out.reshape(N_CHIPS, OUT_SHARD, D_MODEL),), "w_out_grad_to_sibling")
            mlp_grads, mlp_from_sib = _sibling_wait(grads, lands, send, recv, sib_out[4], "mlp_grads_to_sibling")
            partials = _chip_partial(mlp_grads, mlp_from_sib, c_arr, "chip_partial_mlp")
            out_grad, out_from_sib = _sibling_wait(*sib_out[:4], partials[1], "w_out_grad_to_sibling")
            partials = tuple(_chip_partial(out_grad, out_from_sib, c_arr, "chip_partial_w_out")) + tuple(partials)
            self.grads = tuple(out_grad) + tuple(mlp_grads)
            self.from_sib = tuple(out_from_sib) + tuple(mlp_from_sib)
            self.parts, self.part_lands, self.part_send, self.part_recv, token = _send_start(partials, "send_start")
            return (token,)

    comm = Comm()
    gx, big_g, small_g = _local_step(
        x.reshape(nseq * seq, d), loss_target.reshape(nseq * seq, d), w_in_full, comm,
        attn_norm_g, attn_sinks.reshape(N_Q_HEADS), w_pool[0], pool_scale, mlp_norm_g, final_norm_g.reshape(1, d), seq)
    big_g = tuple(big_g[:1]) + comm.grads

    def adamw(idx, mines, others, name):
        pick = lambda group: [group[i] for i in idx]
        return _adamw_shard(mines, others, pick(big_w), pick(big_m), pick(big_v), c_arr, name)

    sib_in = _sibling_start(big_g[:1], "w_in_grad_to_sibling")
    received = _send_wait(comm.parts, comm.part_lands, comm.part_send, comm.part_recv, sib_in[4], "send_wait")
    mine = list(_final_half(big_g[1:2], comm.from_sib[:1], received[:1], jc_arr, "final_half_w_out"))
    mine += list(_final_half(big_g[2:], comm.from_sib[1:], received[1:], jc_arr, "final_half_mlp"))
    halves = _sibling_start(mine, "halves_to_sibling", False)
    g_in, r_in = _sibling_wait(*sib_in[:4], halves[4], "w_in_grad_to_sibling")
    send_in = _send_start(_chip_partial(g_in, r_in, c_arr, "chip_partial_w_in"), "send_start_w_in")
    mine, other = _sibling_wait(*halves[:4], send_in[4], "halves_to_sibling", False)
    (out_res,) = adamw((1,), mine[:1], other[:1], "adamw_w_out")
    up_res, down_res = adamw((2, 3), mine[1:], other[1:], "adamw_mlp")
    p_in = _send_wait(*send_in[:4], up_res[0], "send_wait_w_in")
    mine_in = _final_half(g_in, r_in, p_in, jc_arr, "final_half_w_in")
    other_in = _sibling_swap(mine_in, "w_in_half_to_sibling")
    (in_res,) = adamw((0,), mine_in, other_in, "adamw_w_in")
    big_out = [[t.T for t in in_res], out_res, up_res, down_res]

    wp_flat = lambda a: a.reshape(N_POOL * POOL_G, POOL_G)
    small_params = [
        (attn_norm_g, m_attn_norm_g, v_attn_norm_g),
        (attn_sinks, m_attn_sinks, v_attn_sinks),
        (wp_flat(w_pool), wp_flat(m_w_pool), wp_flat(v_w_pool)),
        (pool_scale, m_pool_scale, v_pool_scale),
        (mlp_norm_g, m_mlp_norm_g, v_mlp_norm_g),
        (final_norm_g.reshape(1, d), m_final_norm_g.reshape(1, d), v_final_norm_g.reshape(1, d)),
    ]
    loss, small_out = _small_adamw(_small_allreduce(small_g), small_params)

    weights = (attn_norm_g, w_in, attn_sinks, w_pool, pool_scale, w_out, mlp_norm_g, w_up, w_down, final_norm_g)
    order = [small_out[0], big_out[0], small_out[1], small_out[2], small_out[3], big_out[1], small_out[4], big_out[2],
             big_out[3], small_out[5]]
    outs = [loss.reshape(()), gx.reshape(nseq, seq, d)]
    for kind in range(4):
        outs += [res[kind].reshape(w.shape) for res, w in zip(order, weights)]
    return tuple(outs)
```

```python
import jax
import jax.numpy as jnp
import numpy as np
from jax import lax
from jax.experimental import pallas as pl
from jax.experimental.pallas import tpu as pltpu

F32 = jnp.float32
BF16 = jnp.bfloat16
MXU_DTYPE = jnp.bfloat16

D_MODEL = 1024
HEAD_DIM = 64
N_Q_HEADS = 8
N_KV_HEADS = 2
Q_PER_KV = N_Q_HEADS // N_KV_HEADS
ATTN_W = N_Q_HEADS * HEAD_DIM
KV_W = N_KV_HEADS * HEAD_DIM
BLK = 128
POOL_WINDOWS = (2, 4, 8, 16)
N_POOL = len(POOL_WINDOWS)
POOL_W = D_MODEL - ATTN_W
POOL_G = POOL_W // N_POOL
IN_W = ATTN_W + 2 * KV_W + POOL_W
D_FF = 4 * D_MODEL
EPS = 1e-6
ROPE_THETA = 10000.0
N_CHIPS = 4
IN_SHARD = IN_W // N_CHIPS
OUT_SHARD = D_MODEL // N_CHIPS
FF_SHARD = D_FF // N_CHIPS
LANES = 128

ADAM_LR = 0.001
ADAM_B1 = 0.9
ADAM_B2 = 0.999
ADAM_EPS = 1e-08
ADAM_WD = 0.01
ADAM_STEP = 10

VMEM_LIMIT = 56 * 1024 * 1024
TILE_BIG = 1024
TILE_MLP = 256
MESH = pl.DeviceIdType.MESH


def _cp(**kw):
    return pltpu.CompilerParams(vmem_limit_bytes=VMEM_LIMIT, **kw)


def _mm(a, b):
    return jnp.dot(a.astype(MXU_DTYPE), b.astype(MXU_DTYPE), preferred_element_type=F32)


def _mm_nt(a, b):
    return lax.dot_general(a.astype(MXU_DTYPE), b.astype(MXU_DTYPE), (((1,), (1,)), ((), ())),
                           preferred_element_type=F32)


def _mm_tn(a, b):
    return lax.dot_general(a.astype(MXU_DTYPE), b.astype(MXU_DTYPE), (((0,), (0,)), ((), ())),
                           preferred_element_type=F32)


def _resident(shape):
    nd = len(shape)
    return pl.BlockSpec(shape, lambda *_: (0,) * nd, pipeline_mode=pl.Buffered(1))


def _const(shape):
    nd = len(shape)
    return pl.BlockSpec(shape, lambda *_: (0,) * nd)


def _rope_tables(seq):
    half = HEAD_DIM // 2
    inv_freq = (ROPE_THETA ** (-np.arange(half, dtype=np.float32) / half)).astype(np.float32)
    ang = np.arange(seq, dtype=np.float32)[:, None] * inv_freq[None, :]
    cos, sin = np.cos(ang).astype(np.float32), np.sin(ang).astype(np.float32)
    cos_t = np.concatenate([cos, cos, cos, cos], axis=1)
    sin_t = np.concatenate([-sin, sin, -sin, sin], axis=1)
    return jnp.asarray(cos_t), jnp.asarray(sin_t)


def _swap_halves(xc):
    lane = lax.broadcasted_iota(jnp.int32, xc.shape, 1)
    return jnp.where((lane & 63) < 32, pltpu.roll(xc, 96, 1), pltpu.roll(xc, 32, 1))


_HBM = pl.BlockSpec(memory_space=pltpu.HBM)
_SEM = pl.BlockSpec(memory_space=pltpu.SEMAPHORE)
_ANY = pl.BlockSpec(memory_space=pl.ANY)
_EFFECT = pltpu.SideEffectType.DATAFLOW_SIDE_EFFECTING
TOKEN = jax.ShapeDtypeStruct((8, LANES), F32)


def _other_chips(x, y):
    return [(1 - x, y), (x, 1 - y), (1 - x, 1 - y)]


def _after(tokens):
    tokens = list(tokens)
    return tokens, [_ANY] * len(tokens)


def _gather_start(first, rest):
    shards = tuple(first) + tuple(rest)
    nw, nf = len(shards), len(first)

    def body(*refs):
        srcs, lands = refs[:nw], refs[nw:2 * nw]
        sems = refs[2 * nw:2 * nw + 4]
        stages = refs[2 * nw + 4:3 * nw + 4]
        lsem = refs[3 * nw + 4]
        loads = refs[3 * nw + 5:]
        x, y, c = lax.axis_index("x"), lax.axis_index("y"), lax.axis_index("c")
        j = 2 * x + y
        for k in range(nw):
            send, recv = sems[0:2] if k < nf else sems[2:4]
            chips = _other_chips(x, y) if k < nf else _other_chips(x, y)[:2]
            base = k * 3 if k < nf else (k - nf) * 2
            if k < nf:
                shard = srcs[k][...]
            else:
                pltpu.sync_copy(srcs[k], loads[k - nf])
                shard = loads[k - nf][...]
            stages[k][...] = shard.astype(BF16)
            local = pltpu.make_async_copy(stages[k], lands[k].at[j], lsem.at[k])
            local.start()
            local.wait()
            rows = shards[k].shape[0] // 2
            mine = lands[k].at[j, pl.ds(c * rows, rows)]
            for n, chip in enumerate(chips):
                pltpu.make_async_remote_copy(
                    src_ref=mine, dst_ref=mine, send_sem=send.at[base + n], recv_sem=recv.at[base + n],
                    device_id=(chip[0], chip[1], c), device_id_type=MESH).start()

    vmem = pl.BlockSpec(memory_space=pltpu.VMEM)
    nr = nw - nf
    res = pl.pallas_call(
        body, name="gather_start",
        out_shape=tuple(pltpu.HBM((N_CHIPS,) + s.shape, BF16) for s in shards) + (
            pltpu.SemaphoreType.DMA((3 * nf,)), pltpu.SemaphoreType.DMA((3 * nf,)),
            pltpu.SemaphoreType.DMA((2 * nr,)), pltpu.SemaphoreType.DMA((2 * nr,))),
        in_specs=[vmem] * nf + [_ANY] * nr, out_specs=tuple([_HBM] * nw) + (_SEM, _SEM, _SEM, _SEM),
        scratch_shapes=[pltpu.VMEM(s.shape, BF16) for s in shards] + [pltpu.SemaphoreType.DMA((nw,))] + [
            pltpu.VMEM(s.shape, F32) for s in rest],
        compiler_params=_cp(has_side_effects=_EFFECT),
    )(*shards)
    return (res[:nf], res[nw], res[nw + 1]), (res[nf:nw], res[nw + 2], res[nw + 3])


def _gather_wait(lands, send, recv, name, after=()):
    nw = len(lands)
    tokens, token_specs = _after(after)

    def body(*refs):
        ins = refs[:nw]
        send_ref, recv_ref = refs[nw], refs[nw + 1]
        x, y, c = lax.axis_index("x"), lax.axis_index("y"), lax.axis_index("c")
        j = 2 * x + y
        for k in range(nw):
            rows = lands[k].shape[1] // 2
            mine = ins[k].at[j, pl.ds(c * rows, rows)]
            for n, chip in enumerate(_other_chips(x, y)):
                got = ins[k].at[2 * chip[0] + chip[1], pl.ds(c * rows, rows)]
                cp = pltpu.make_async_remote_copy(
                    src_ref=mine, dst_ref=got, send_sem=send_ref.at[k * 3 + n], recv_sem=recv_ref.at[k * 3 + n],
                    device_id=(chip[0], chip[1], c), device_id_type=MESH)
                cp.wait_send()
                cp.wait_recv()

    return pl.pallas_call(
        body, name=name, out_shape=tuple(pltpu.HBM(a.shape, a.dtype) for a in lands),
        in_specs=[_HBM] * nw + [_SEM, _SEM] + token_specs, out_specs=tuple([_HBM] * nw),
        input_output_aliases={k: k for k in range(nw)},
        compiler_params=_cp(has_side_effects=_EFFECT),
    )(*lands, send, recv, *tokens)


def _relay_copies(refs, nw, rows_of, send, recv):
    x, y, c = lax.axis_index("x"), lax.axis_index("y"), lax.axis_index("c")
    xn, yn, dg = _other_chips(x, y)
    mine, theirs = [], []
    for k in range(nw):
        rows = rows_of[k]
        for n, (origin, target) in enumerate(((xn, yn), (yn, xn))):
            part = pl.ds(c * rows + n * (rows // 2), rows // 2)
            sent, got = refs[k].at[2 * origin[0] + origin[1], part], refs[k].at[2 * dg[0] + dg[1], part]
            mine.append(pltpu.make_async_remote_copy(
                src_ref=sent, dst_ref=sent, send_sem=send.at[k * 2 + n], recv_sem=recv.at[k * 2 + n],
                device_id=(target[0], target[1], c), device_id_type=MESH))
            theirs.append(pltpu.make_async_remote_copy(
                src_ref=got, dst_ref=got, send_sem=send.at[k * 2 + n], recv_sem=recv.at[k * 2 + n],
                device_id=(target[0], target[1], c), device_id_type=MESH))
    return mine, theirs


def _gather_relay(lands, send, recv, after):
    nw = len(lands)
    rows_of = [a.shape[1] // 2 for a in lands]

    def body(*refs):
        ins, send1, recv1 = refs[:nw], refs[nw], refs[nw + 1]
        send2, recv2, send3, recv3, token = refs[2 * nw + 3:2 * nw + 8]
        x, y, c = lax.axis_index("x"), lax.axis_index("y"), lax.axis_index("c")
        j = 2 * x + y
        for k in range(nw):
            mine = ins[k].at[j, pl.ds(c * rows_of[k], rows_of[k])]
            for n, chip in enumerate(_other_chips(x, y)[:2]):
                got = ins[k].at[2 * chip[0] + chip[1], pl.ds(c * rows_of[k], rows_of[k])]
                cp = pltpu.make_async_remote_copy(
                    src_ref=mine, dst_ref=got, send_sem=send1.at[k * 2 + n], recv_sem=recv1.at[k * 2 + n],
                    device_id=(chip[0], chip[1], c), device_id_type=MESH)
                cp.wait_send()
                cp.wait_recv()
        for cp in _relay_copies(ins, nw, rows_of, send2, recv2)[0] + _forward_copies(ins, nw, rows_of, send3, recv3, (0, 1))[0]:
            cp.start()
        token[...] = jnp.zeros_like(token)

    sems = pltpu.SemaphoreType.DMA((2 * nw,))
    res = pl.pallas_call(
        body, name="gather_relay",
        out_shape=tuple(pltpu.HBM(a.shape, a.dtype) for a in lands) + (sems, sems, sems, sems, TOKEN),
        in_specs=[_HBM] * nw + [_SEM, _SEM, _ANY],
        out_specs=tuple([_HBM] * nw) + (_SEM, _SEM, _SEM, _SEM, pl.BlockSpec(memory_space=pltpu.VMEM)),
        input_output_aliases={k: k for k in range(nw)},
        compiler_params=_cp(has_side_effects=_EFFECT),
    )(*lands, send, recv, after)
    return res[:nw], res[nw:nw + 2], res[nw + 2:nw + 4], res[nw + 4]


def _gather_relayed(lands, relay_sems, after):
    nw = len(lands)
    rows_of = [a.shape[1] // 2 for a in lands]

    def body(*refs):
        ins = refs[:nw]
        send3, recv3 = refs[2 * nw + 3:2 * nw + 5]
        mine, theirs = _relay_copies(ins, nw, rows_of, refs[nw], refs[nw + 1])
        for cp in mine:
            cp.wait_send()
        for cp in theirs:
            cp.wait_recv()
        for cp in _forward_copies(ins, nw, rows_of, send3, recv3, (2,))[0]:
            cp.start()

    sems = pltpu.SemaphoreType.DMA((nw,))
    res = pl.pallas_call(
        body, name="gather_relayed", out_shape=tuple(pltpu.HBM(a.shape, a.dtype) for a in lands) + (sems, sems),
        in_specs=[_HBM] * nw + [_SEM, _SEM, _ANY], out_specs=tuple([_HBM] * nw) + (_SEM, _SEM),
        input_output_aliases={k: k for k in range(nw)},
        compiler_params=_cp(has_side_effects=_EFFECT),
    )(*lands, *relay_sems, after)
    return res[:nw], res[nw:nw + 2]


def _gather_done(lands, near_sems, far_sems):
    nw = len(lands)
    rows_of = [a.shape[1] // 2 for a in lands]

    def body(*refs):
        ins = refs[:nw]
        for sems, which in ((refs[nw:nw + 2], (0, 1)), (refs[nw + 2:nw + 4], (2,))):
            mine, theirs = _forward_copies(ins, nw, rows_of, sems[0], sems[1], which)
            for cp in mine:
                cp.wait_send()
            for cp in theirs:
                cp.wait_recv()

    return pl.pallas_call(
        body, name="gather_done", out_shape=tuple(pltpu.HBM(a.shape, a.dtype) for a in lands),
        in_specs=[_HBM] * nw + [_SEM] * 4, out_specs=tuple([_HBM] * nw),
        input_output_aliases={k: k for k in range(nw)},
        compiler_params=_cp(has_side_effects=_EFFECT),
    )(*lands, *near_sems, *far_sems)


def _forward_now(lands, name):
    nw = len(lands)
    rows_of = [a.shape[1] // 2 for a in lands]

    def body(*refs):
        mine, theirs = _forward_copies(refs[nw:2 * nw], nw, rows_of, refs[2 * nw], refs[2 * nw + 1])
        for cp in mine:
            cp.start()
        for cp in theirs:
            cp.wait_recv()
        for cp in mine:
            cp.wait_send()

    return pl.pallas_call(
        body, name=name, out_shape=tuple(jax.ShapeDtypeStruct(a.shape, a.dtype) for a in lands),
        in_specs=[_HBM] * nw, out_specs=tuple([_HBM] * nw), input_output_aliases={k: k for k in range(nw)},
        scratch_shapes=[pltpu.SemaphoreType.DMA((3 * nw,)), pltpu.SemaphoreType.DMA((3 * nw,))],
        compiler_params=_cp(),
    )(*lands)


def _forward_copies(refs, nw, rows_of, send, recv, which=(0, 1, 2)):
    x, y, c = lax.axis_index("x"), lax.axis_index("y"), lax.axis_index("c")
    out = []
    for k in range(nw):
        rows = rows_of[k]
        for n, slot in enumerate(which):
            chip = _other_chips(x, y)[slot]
            got = refs[k].at[2 * chip[0] + chip[1], pl.ds(c * rows, rows)]
            theirs = refs[k].at[2 * chip[0] + chip[1], pl.ds((1 - c) * rows, rows)]
            sem = k * len(which) + n
            out.append(pltpu.make_async_remote_copy(
                src_ref=got, dst_ref=got, send_sem=send.at[sem], recv_sem=recv.at[sem],
                device_id=(x, y, 1 - c), device_id_type=MESH))
            out.append(pltpu.make_async_remote_copy(
                src_ref=theirs, dst_ref=theirs, send_sem=send.at[sem], recv_sem=recv.at[sem],
                device_id=(x, y, 1 - c), device_id_type=MESH))
    return out[0::2], out[1::2]


def _fwd_inproj(x, g1, w_in, cos_t, sin_t, tm):
    t_tok = x.shape[0]
    seq = cos_t.shape[0]
    per_seq = seq // tm

    def body(x_ref, g_ref, w_ref, cos_ref, sin_ref, q_ref, k_ref, v_ref, u_ref):
        xv = x_ref[...]
        r = lax.rsqrt(jnp.mean(xv * xv, axis=-1, keepdims=True) + EPS)
        h = (xv * r) * g_ref[...]
        proj = _mm_nt(h, w_ref[...])
        cos, sin = cos_ref[...], sin_ref[...]
        for cidx in range((ATTN_W + KV_W) // LANES):
            xc = proj[:, cidx * LANES:(cidx + 1) * LANES]
            rot = xc * cos + _swap_halves(xc) * sin
            if cidx < ATTN_W // LANES:
                q_ref[:, cidx * LANES:(cidx + 1) * LANES] = (rot * (HEAD_DIM ** -0.5)).astype(q_ref.dtype)
            else:
                k_ref[...] = rot.astype(k_ref.dtype)
        v_ref[...] = proj[:, ATTN_W + KV_W:ATTN_W + 2 * KV_W].astype(v_ref.dtype)
        u_ref[...] = proj[:, ATTN_W + 2 * KV_W:].astype(u_ref.dtype)

    row = lambda w: pl.BlockSpec((tm, w), lambda i: (i, 0))
    kv = row(KV_W)
    tab = pl.BlockSpec((tm, LANES), lambda i: (i % per_seq, 0))
    kv_shape = jax.ShapeDtypeStruct((t_tok, KV_W), MXU_DTYPE)
    return pl.pallas_call(
        body, name="fwd_inproj", grid=(t_tok // tm,),
        in_specs=[row(D_MODEL), _const((1, D_MODEL)), _resident((IN_W, D_MODEL)), tab, tab],
        out_specs=(row(ATTN_W), kv, kv, row(POOL_W)),
        out_shape=(jax.ShapeDtypeStruct((t_tok, ATTN_W), MXU_DTYPE), kv_shape, kv_shape,
                   jax.ShapeDtypeStruct((t_tok, POOL_W), MXU_DTYPE)),
        compiler_params=_cp(dimension_semantics=("parallel",)),
    )(x, g1, w_in, cos_t, sin_t)


MASKED = -1e30


def _attn_bias():
    b = lax.broadcasted_iota(jnp.int32, (2 * BLK, BLK), 0)
    a = lax.broadcasted_iota(jnp.int32, (2 * BLK, BLK), 1)
    own = (b >= BLK) & (b - BLK <= a)
    prev = (b < BLK) & (b > a)
    return jnp.stack([jnp.where(own, 0.0, MASKED), jnp.where(own | prev, 0.0, MASKED)]).astype(F32)


def _sink_rows(sinks):
    return jnp.repeat(sinks.reshape(N_KV_HEADS, Q_PER_KV), BLK, axis=1).reshape(N_KV_HEADS, 1, Q_PER_KV * BLK)


def _stack_heads(ref, r0, g):
    return jnp.concatenate(
        [ref[pl.ds(r0, BLK), (Q_PER_KV * g + h) * HEAD_DIM:(Q_PER_KV * g + h + 1) * HEAD_DIM] for h in range(Q_PER_KV)],
        axis=0)


def _kv_window(ref, p0, r0, g):
    sl = slice(g * HEAD_DIM, (g + 1) * HEAD_DIM)
    return jnp.concatenate([ref[pl.ds(p0, BLK), sl], ref[pl.ds(r0, BLK), sl]], axis=0)


def _scores_t(k_ref, q_ref, bias, p0, r0, g):
    kk = _kv_window(k_ref, p0, r0, g)
    qs = _stack_heads(q_ref, r0, g)
    st = _mm_nt(kk, qs) + jnp.concatenate([bias] * Q_PER_KV, axis=1)
    return st, kk, qs


def _head_rows(ref, r0, g):
    return jnp.concatenate([ref[pl.ds(Q_PER_KV * g + h, 1), pl.ds(r0, BLK)] for h in range(Q_PER_KV)], axis=1)


def _pairs_to_rows(xt):
    out = []
    for t in range(Q_PER_KV // 2):
        pair = jnp.concatenate([xt[:, (2 * t) * BLK:(2 * t + 1) * BLK], xt[:, (2 * t + 1) * BLK:(2 * t + 2) * BLK]], axis=0)
        out.append(pair.T)
    return out


def _shift_rows(x, k, seq):
    row = lax.broadcasted_iota(jnp.int32, x.shape, 0)
    if k > 0:
        return jnp.where(row >= k, pltpu.roll(x, k, 0), 0.0)
    return jnp.where(row < seq + k, pltpu.roll(x, seq + k, 0), 0.0)


def _window_sum(x, w, seq, forward):
    s, k = x, 1
    while k < w:
        s = s + _shift_rows(s, -k if forward else k, seq)
        k *= 2
    return s


def _inv_count(seq, w):
    pos = lax.broadcasted_iota(jnp.int32, (seq, 1), 0)
    return 1.0 / jnp.minimum(pos + 1, w).astype(F32)


def _fwd_attn(q, k, v, sink_rows, bias, seq, after=()):
    t_tok = q.shape[0]
    nblk = seq // BLK
    together = max(f for f in (4, 2, 1) if nblk % f == 0)
    tokens, token_specs = _after(after)

    def body(q_ref, k_ref, v_ref, sink_ref, bias_ref, *rest):
        o_ref, lse_ref = rest[len(tokens):]

        def blocks(ik, carry):
            units = []
            for s in range(together):
                i = together * ik + s
                r0 = pl.multiple_of(i * BLK, BLK)
                p0 = pl.multiple_of(jnp.maximum(i - 1, 0) * BLK, BLK)
                units += [(g, r0, p0, bias_ref[jnp.minimum(i, 1)]) for g in range(N_KV_HEADS)]
            sts = [_scores_t(k_ref, q_ref, bias, p0, r0, g)[0] for g, r0, p0, bias in units]
            ms = [jnp.maximum(jnp.max(st, axis=0, keepdims=True), sink_ref[u[0]]) for st, u in zip(sts, units)]
            ps = [jnp.exp(st - m).astype(MXU_DTYPE) for st, m in zip(sts, ms)]
            vvs = [_kv_window(v_ref, p0, r0, g) for g, r0, p0, _ in units]
            ots = [_mm_tn(jnp.concatenate([vv, jnp.ones_like(vv)], axis=1), p) for vv, p in zip(vvs, ps)]
            for (g, r0, _, _), m, ot in zip(units, ms, ots):
                den = ot[HEAD_DIM:HEAD_DIM + 1] + jnp.exp(sink_ref[g] - m)
                lse = m + jnp.log(den)
                for h in range(Q_PER_KV):
                    lse_ref[pl.ds(Q_PER_KV * g + h, 1), pl.ds(r0, BLK)] = lse[:, h * BLK:(h + 1) * BLK]
                for t, tile in enumerate(_pairs_to_rows(ot[:HEAD_DIM] * (1.0 / den))):
                    c0 = (Q_PER_KV * g + 2 * t) * HEAD_DIM
                    o_ref[pl.ds(r0, BLK), c0:c0 + LANES] = tile.astype(o_ref.dtype)
            return carry

        lax.fori_loop(0, nblk // together, blocks, 0)

    row = lambda w: pl.BlockSpec((seq, w), lambda i: (i, 0))
    kv = row(KV_W)
    return pl.pallas_call(
        body, name="fwd_attn", grid=(t_tok // seq,),
        in_specs=[row(ATTN_W), kv, kv, _const((N_KV_HEADS, 1, Q_PER_KV * BLK)), _const((2, 2 * BLK, BLK))] + token_specs,
        out_specs=(row(ATTN_W), pl.BlockSpec((N_Q_HEADS, seq), lambda i: (0, i))),
        out_shape=(jax.ShapeDtypeStruct((t_tok, ATTN_W), MXU_DTYPE), jax.ShapeDtypeStruct((N_Q_HEADS, t_tok), F32)),
        compiler_params=_cp(dimension_semantics=("parallel",)),
    )(q, k, v, sink_rows, bias, *tokens)


def _fwd_pool(u, w_pool, pool_scale, seq):
    t_tok = u.shape[0]

    def body(u_ref, wp_ref, sc_ref, o_ref):
        for gi, w in enumerate(POOL_WINDOWS):
            sl = slice(gi * POOL_G, (gi + 1) * POOL_G)
            ug = u_ref[:, sl].astype(F32)
            d = _window_sum(ug, w, seq, False) * _inv_count(seq, w) - ug
            o_ref[:, sl] = (_mm(d, wp_ref[gi]) * sc_ref[:, sl]).astype(o_ref.dtype)

    row = pl.BlockSpec((seq, POOL_W), lambda i: (i, 0))
    return pl.pallas_call(
        body, name="fwd_pool", grid=(t_tok // seq,),
        in_specs=[row, _const((N_POOL, POOL_G, POOL_G)), _const((1, POOL_W))], out_specs=row,
        out_shape=jax.ShapeDtypeStruct((t_tok, POOL_W), MXU_DTYPE),
        compiler_params=_cp(dimension_semantics=("parallel",)),
    )(u, w_pool, pool_scale)


def _rms_bwd(dy_g, xn, r):
    return r * (dy_g - xn * jnp.mean(dy_g * xn, axis=-1, keepdims=True))


def _mlp_fwd_bwd(x, attn, pool, target, w_out, w_up4, w_down4, g2, gf, tm):
    t_tok = x.shape[0]

    def body(x_ref, attn_ref, pool_ref, tgt_ref, wo_ref, wu_ref, wd_ref, g2_ref, gf_ref,
             h2_ref, a_ref, da_ref, dx2b_ref, dx1_ref, dx1b_ref, loss_ref, dgf_ref, dg2_ref):
        @pl.when(pl.program_id(0) == 0)
        def _():
            loss_ref[...] = jnp.zeros_like(loss_ref)
            dgf_ref[...] = jnp.zeros_like(dgf_ref)
            dg2_ref[...] = jnp.zeros_like(dg2_ref)

        x1 = x_ref[...] + (_mm(attn_ref[...], wo_ref[:ATTN_W]) + _mm(pool_ref[...], wo_ref[ATTN_W:]))
        r2 = lax.rsqrt(jnp.mean(x1 * x1, axis=-1, keepdims=True) + EPS)
        xn1 = x1 * r2
        g2 = g2_ref[...]
        h2 = (xn1 * g2).astype(MXU_DTYPE)
        h2_ref[...] = h2
        acc = jnp.zeros((tm, D_MODEL), F32)
        for j in range(N_CHIPS):
            a = _mm(h2, wu_ref[j])
            a_ref[:, j * FF_SHARD:(j + 1) * FF_SHARD] = a.astype(a_ref.dtype)
            acc = acc + _mm(jnp.square(jnp.maximum(a, 0.0)), wd_ref[j])
        x2 = x1 + acc
        r3 = lax.rsqrt(jnp.mean(x2 * x2, axis=-1, keepdims=True) + EPS)
        xn2 = x2 * r3
        gf_v = gf_ref[...]
        err = xn2 * gf_v - tgt_ref[...]
        part = jnp.sum(err * err) * (0.5 / D_MODEL)
        first = (lax.broadcasted_iota(jnp.int32, loss_ref.shape, 0) == 0) & (lax.broadcasted_iota(jnp.int32, loss_ref.shape, 1) == 0)
        loss_ref[...] += jnp.where(first, part, 0.0)
        dy = err * (1.0 / D_MODEL)
        dgf_ref[...] += jnp.sum(dy * xn2, axis=0, keepdims=True)
        dx2 = _rms_bwd(dy * gf_v, xn2, r3)
        dx2b = dx2.astype(MXU_DTYPE)
        dx2b_ref[...] = dx2b
        dh2 = jnp.zeros((tm, D_MODEL), F32)
        for j in range(N_CHIPS):
            sl = slice(j * FF_SHARD, (j + 1) * FF_SHARD)
            dhid = _mm_nt(dx2b, wd_ref[j])
            da = (dhid * (2.0 * jnp.maximum(a_ref[:, sl].astype(F32), 0.0))).astype(MXU_DTYPE)
            da_ref[:, sl] = da
            dh2 = dh2 + _mm_nt(da, wu_ref[j])
        dg2_ref[...] += jnp.sum(dh2 * xn1, axis=0, keepdims=True)
        dx1 = dx2 + _rms_bwd(dh2 * g2, xn1, r2)
        dx1_ref[...] = dx1
        dx1b_ref[...] = dx1.astype(dx1b_ref.dtype)

    row = lambda w: pl.BlockSpec((tm, w), lambda i: (i, 0))
    vec = jax.ShapeDtypeStruct((1, D_MODEL), F32)
    return pl.pallas_call(
        body, name="mlp_fwd_bwd", grid=(t_tok // tm,),
        in_specs=[row(D_MODEL), row(ATTN_W), row(POOL_W), row(D_MODEL), _resident((D_MODEL, D_MODEL)),
                  _resident((N_CHIPS, D_MODEL, FF_SHARD)), _resident((N_CHIPS, FF_SHARD, D_MODEL)),
                  _const((1, D_MODEL)), _const((1, D_MODEL))],
        out_specs=(row(D_MODEL), row(D_FF), row(D_FF), row(D_MODEL), row(D_MODEL), row(D_MODEL),
                   _const((8, LANES)), _const((1, D_MODEL)), _const((1, D_MODEL))),
        out_shape=(jax.ShapeDtypeStruct((t_tok, D_MODEL), MXU_DTYPE), jax.ShapeDtypeStruct((t_tok, D_FF), MXU_DTYPE),
                   jax.ShapeDtypeStruct((t_tok, D_FF), MXU_DTYPE), jax.ShapeDtypeStruct((t_tok, D_MODEL), MXU_DTYPE),
                   jax.ShapeDtypeStruct((t_tok, D_MODEL), F32), jax.ShapeDtypeStruct((t_tok, D_MODEL), MXU_DTYPE),
                   jax.ShapeDtypeStruct((8, LANES), F32), vec, vec),
        compiler_params=_cp(dimension_semantics=("arbitrary",)),
    )(x, attn, pool, target, w_out, w_up4, w_down4, g2, gf)


def _bwd_mlp_wgrads(h2, da, a, dx2b, tk):
    t_tok = h2.shape[0]

    def body(h2_ref, da_ref, a_ref, dx2b_ref, gup_ref, gdn_ref):
        @pl.when(pl.program_id(1) == 0)
        def _():
            gup_ref[...] = jnp.zeros_like(gup_ref)
            gdn_ref[...] = jnp.zeros_like(gdn_ref)

        gup_ref[0] += _mm_tn(h2_ref[...], da_ref[...])
        hid = jnp.square(jnp.maximum(a_ref[...].astype(F32), 0.0))
        gdn_ref[0] += _mm_tn(hid, dx2b_ref[...])

    tok = pl.BlockSpec((tk, D_MODEL), lambda j, t: (t, 0))
    ffb = pl.BlockSpec((tk, FF_SHARD), lambda j, t: (t, j))
    wblk = pl.BlockSpec((1, D_MODEL, D_MODEL), lambda j, t: (j, 0, 0))
    return pl.pallas_call(
        body, name="bwd_mlp_wgrads", grid=(N_CHIPS, t_tok // tk),
        in_specs=[tok, ffb, ffb, tok], out_specs=(wblk, wblk),
        out_shape=(jax.ShapeDtypeStruct((N_CHIPS, D_MODEL, FF_SHARD), F32), jax.ShapeDtypeStruct((N_CHIPS, FF_SHARD, D_MODEL), F32)),
        compiler_params=_cp(dimension_semantics=("parallel", "arbitrary")),
    )(h2, da, a, dx2b)


def _head_selector():
    ch = lax.broadcasted_iota(jnp.int32, (ATTN_W, LANES), 0)
    col = lax.broadcasted_iota(jnp.int32, (ATTN_W, LANES), 1)
    return (ch // HEAD_DIM == col).astype(MXU_DTYPE)


def _bwd_outproj(dx1b, attn, pool, w_out, head_sel, tm, after=()):
    t_tok = dx1b.shape[0]
    tokens, token_specs = _after(after)

    def body(dx_ref, attn_ref, pool_ref, wo_ref, sel_ref, *rest):
        dattn_ref, dpool_ref, delta_ref, gwo_ref = rest[len(tokens):]

        @pl.when(pl.program_id(0) == 0)
        def _():
            gwo_ref[...] = jnp.zeros_like(gwo_ref)

        dx = dx_ref[...]
        attn = attn_ref[...]
        dattn = _mm_nt(dx, wo_ref[:ATTN_W])
        dattn_ref[...] = dattn.astype(dattn_ref.dtype)
        dpool_ref[...] = _mm_nt(dx, wo_ref[ATTN_W:]).astype(dpool_ref.dtype)
        prod = dattn * attn.astype(F32)
        hi = prod.astype(MXU_DTYPE)
        lo = prod - hi.astype(F32)
        delta = _mm(hi, sel_ref[...]) + _mm(lo, sel_ref[...])
        delta_ref[...] = delta.T[:N_Q_HEADS]
        gwo_ref[:ATTN_W] += _mm_tn(attn, dx)
        gwo_ref[ATTN_W:] += _mm_tn(pool_ref[...], dx)

    row = lambda w: pl.BlockSpec((tm, w), lambda i: (i, 0))
    return pl.pallas_call(
        body, name="bwd_outproj", grid=(t_tok // tm,),
        in_specs=[row(D_MODEL), row(ATTN_W), row(POOL_W), _resident((D_MODEL, D_MODEL)), _const((ATTN_W, LANES))] + token_specs,
        out_specs=(row(ATTN_W), row(POOL_W), pl.BlockSpec((N_Q_HEADS, tm), lambda i: (0, i)), _const((D_MODEL, D_MODEL))),
        out_shape=(jax.ShapeDtypeStruct((t_tok, ATTN_W), MXU_DTYPE), jax.ShapeDtypeStruct((t_tok, POOL_W), MXU_DTYPE),
                   jax.ShapeDtypeStruct((N_Q_HEADS, t_tok), F32), jax.ShapeDtypeStruct((D_MODEL, D_MODEL), F32)),
        compiler_params=_cp(dimension_semantics=("arbitrary",)),
    )(dx1b, attn, pool, w_out, head_sel, *tokens)


def _bwd_attn(q, k, v, dattn, lse, delta, sink_rows, bias, cos_t, sin_t, seq, after=()):
    t_tok = q.shape[0]
    nblk = seq // BLK
    together = max(f for f in (4, 2, 1) if nblk % f == 0)
    qkv_w = ATTN_W + 2 * KV_W
    tokens, token_specs = _after(after)

    def unrope(d, cos, sin):
        return d * cos - _swap_halves(d) * sin

    def body(q_ref, k_ref, v_ref, do_ref, lse_ref, delta_ref, sink_ref, bias_ref, cos_ref, sin_ref, *rest):
        dqkv_ref, dsink_ref, dk_acc, dv_acc = rest[len(tokens):]

        @pl.when(pl.program_id(0) == 0)
        def _():
            dsink_ref[...] = jnp.zeros_like(dsink_ref)

        dk_acc[...] = jnp.zeros_like(dk_acc)
        dv_acc[...] = jnp.zeros_like(dv_acc)

        def blocks(ik, dsink):
            units = []
            for s in range(together):
                i = together * ik + s
                r0 = pl.multiple_of(i * BLK, BLK)
                p0 = pl.multiple_of(jnp.maximum(i - 1, 0) * BLK, BLK)
                units += [(g, r0, p0, bias_ref[jnp.minimum(i, 1)]) for g in range(N_KV_HEADS)]
            scores = [_scores_t(k_ref, q_ref, bias, p0, r0, g) for g, r0, p0, bias in units]
            doss = [_stack_heads(do_ref, r0, g) for g, r0, _, _ in units]
            dpts = [_mm_nt(_kv_window(v_ref, p0, r0, g), dos) for (g, r0, p0, _), dos in zip(units, doss)]
            lses = [_head_rows(lse_ref, r0, g) for g, r0, _, _ in units]
            deltas = [_head_rows(delta_ref, r0, g) for g, r0, _, _ in units]
            pns = [jnp.exp(sc[0] - lse) for sc, lse in zip(scores, lses)]
            dsts = [pn * (dpt - delta) for pn, dpt, delta in zip(pns, dpts, deltas)]
            dsink = list(dsink)
            for (g, _, _, _), lse, delta in zip(units, lses, deltas):
                dsink[g] = dsink[g] - jnp.exp(sink_ref[g] - lse) * delta
            dqts = [_mm_tn(sc[1], dst) * (HEAD_DIM ** -0.5) for sc, dst in zip(scores, dsts)]
            dkks = [_mm(dst, sc[2]) for sc, dst in zip(scores, dsts)]
            dvvs = [_mm(pn, dos) for pn, dos in zip(pns, doss)]
            for (g, r0, p0, _), dqt, dkk, dvv in zip(units, dqts, dkks, dvvs):
                cos, sin = cos_ref[pl.ds(r0, BLK), :], sin_ref[pl.ds(r0, BLK), :]
                for t, tile in enumerate(_pairs_to_rows(dqt)):
                    c0 = (Q_PER_KV * g + 2 * t) * HEAD_DIM
                    dqkv_ref[pl.ds(r0, BLK), c0:c0 + LANES] = unrope(tile, cos, sin).astype(dqkv_ref.dtype)
                dk_acc[g, pl.ds(p0, BLK), :] += dkk[:BLK]
                dk_acc[g, pl.ds(r0, BLK), :] += dkk[BLK:]
                dv_acc[g, pl.ds(p0, BLK), :] += dvv[:BLK]
                dv_acc[g, pl.ds(r0, BLK), :] += dvv[BLK:]
            return tuple(dsink)

        zero = jnp.zeros((1, Q_PER_KV * BLK), F32)
        dsink = lax.fori_loop(0, nblk // together, blocks, (zero,) * N_KV_HEADS)
        lane = lax.broadcasted_iota(jnp.int32, dsink_ref.shape, 1)
        row = lax.broadcasted_iota(jnp.int32, dsink_ref.shape, 0)
        tile = jnp.zeros(dsink_ref.shape, F32)
        for g in range(N_KV_HEADS):
            for h in range(Q_PER_KV):
                tot = jnp.sum(dsink[g][:, h * BLK:(h + 1) * BLK])
                tile = tile + jnp.where((row == 0) & (lane == Q_PER_KV * g + h), tot, 0.0)
        dsink_ref[...] += tile
        dk = jnp.concatenate([dk_acc[g] for g in range(N_KV_HEADS)], axis=1)
        dqkv_ref[:, ATTN_W:ATTN_W + KV_W] = unrope(dk, cos_ref[...], sin_ref[...]).astype(dqkv_ref.dtype)
        dqkv_ref[:, ATTN_W + KV_W:] = jnp.concatenate([dv_acc[g] for g in range(N_KV_HEADS)], axis=1).astype(dqkv_ref.dtype)

    row = lambda w: pl.BlockSpec((seq, w), lambda i: (i, 0))
    kv = row(KV_W)
    per_head = pl.BlockSpec((N_Q_HEADS, seq), lambda i: (0, i))
    return pl.pallas_call(
        body, name="bwd_attn", grid=(t_tok // seq,),
        in_specs=[row(ATTN_W), kv, kv, row(ATTN_W), per_head, per_head, _const((N_KV_HEADS, 1, Q_PER_KV * BLK)),
                  _const((2, 2 * BLK, BLK)), _resident((seq, LANES)), _resident((seq, LANES))] + token_specs,
        out_specs=(row(qkv_w), _const((8, LANES))),
        out_shape=(jax.ShapeDtypeStruct((t_tok, qkv_w), MXU_DTYPE), jax.ShapeDtypeStruct((8, LANES), F32)),
        scratch_shapes=[pltpu.VMEM((N_KV_HEADS, seq, HEAD_DIM), F32), pltpu.VMEM((N_KV_HEADS, seq, HEAD_DIM), F32)],
        compiler_params=_cp(dimension_semantics=("arbitrary",)),
    )(q, k, v, dattn, lse, delta, sink_rows, bias, cos_t, sin_t, *tokens)


def _bwd_pool(u, dpool, w_pool, pool_scale, seq):
    t_tok = u.shape[0]

    def body(u_ref, dp_ref, wp_ref, sc_ref, du_ref, dwp_ref, dsc_ref):
        @pl.when(pl.program_id(0) == 0)
        def _():
            dwp_ref[...] = jnp.zeros_like(dwp_ref)
            dsc_ref[...] = jnp.zeros_like(dsc_ref)

        for gi, w in enumerate(POOL_WINDOWS):
            sl = slice(gi * POOL_G, (gi + 1) * POOL_G)
            ug = u_ref[:, sl].astype(F32)
            inv = _inv_count(seq, w)
            d = (_window_sum(ug, w, seq, False) * inv - ug).astype(MXU_DTYPE)
            y = _mm(d, wp_ref[gi])
            dpool = dp_ref[:, sl].astype(F32)
            dsc_ref[:, sl] += jnp.sum(y * dpool, axis=0, keepdims=True)
            dy = (dpool * sc_ref[:, sl]).astype(MXU_DTYPE)
            dwp_ref[gi] += _mm_tn(d, dy)
            dd = _mm_nt(dy, wp_ref[gi])
            du_ref[:, sl] = (_window_sum(dd * inv, w, seq, True) - dd).astype(du_ref.dtype)

    row = pl.BlockSpec((seq, POOL_W), lambda i: (i, 0))
    return pl.pallas_call(
        body, name="bwd_pool", grid=(t_tok // seq,),
        in_specs=[row, row, _const((N_POOL, POOL_G, POOL_G)), _const((1, POOL_W))],
        out_specs=(row, _const((N_POOL, POOL_G, POOL_G)), _const((1, POOL_W))),
        out_shape=(jax.ShapeDtypeStruct((t_tok, POOL_W), MXU_DTYPE), jax.ShapeDtypeStruct((N_POOL, POOL_G, POOL_G), F32),
                   jax.ShapeDtypeStruct((1, POOL_W), F32)),
        compiler_params=_cp(dimension_semantics=("arbitrary",)),
    )(u, dpool, w_pool, pool_scale)


def _bwd_inproj(dqkv, du, x, dx1, w_in, g1, tm):
    t_tok = x.shape[0]
    nsteps = t_tok // tm

    qkv_w = ATTN_W + 2 * KV_W

    def body(dqkv_ref, du_ref, x_ref, dx1_ref, w_ref, g_ref, gx_ref, gw_ref, dg_ref):
        @pl.when(pl.program_id(0) == 0)
        def _():
            gw_ref[...] = jnp.zeros_like(gw_ref)
            dg_ref[...] = jnp.zeros_like(dg_ref)

        dqkv, du = dqkv_ref[...], du_ref[...]
        xv = x_ref[...]
        r = lax.rsqrt(jnp.mean(xv * xv, axis=-1, keepdims=True) + EPS)
        xn = xv * r
        g = g_ref[...]
        dh = _mm(dqkv, w_ref[:qkv_w]) + _mm(du, w_ref[qkv_w:])
        dg_ref[...] += jnp.sum(dh * xn, axis=0, keepdims=True)
        gx_ref[...] = dx1_ref[...] + _rms_bwd(dh * g, xn, r)
        h = (xn * g).astype(MXU_DTYPE)
        gw_ref[:qkv_w] += _mm_tn(dqkv, h)
        gw_ref[qkv_w:] += _mm_tn(du, h)

    row = lambda w: pl.BlockSpec((tm, w), lambda i: (i, 0))
    return pl.pallas_call(
        body, name="bwd_inproj", grid=(nsteps,),
        in_specs=[row(qkv_w), row(POOL_W), row(D_MODEL), row(D_MODEL), _resident((IN_W, D_MODEL)), _const((1, D_MODEL))],
        out_specs=(row(D_MODEL), _const((IN_W, D_MODEL)), _const((1, D_MODEL))),
        out_shape=(jax.ShapeDtypeStruct((t_tok, D_MODEL), F32), jax.ShapeDtypeStruct((IN_W, D_MODEL), F32),
                   jax.ShapeDtypeStruct((1, D_MODEL), F32)),
        compiler_params=_cp(dimension_semantics=("arbitrary",)),
    )(dqkv, du, x, dx1, w_in, g1)


class _NoComm:
    def pool_done(self, pool):
        return ()

    def rest_of_weights(self, attn):
        raise NotImplementedError

    def mlp_grads_ready(self, gw_up4, gw_down4):
        return ()

    def outproj_done(self, gw_out):
        return ()


def _local_step(x, target, w_in, comm, g1, sinks, w_pool, pool_scale, g2, gf, seq):
    tm_big, tm_mlp = min(TILE_BIG, seq), min(TILE_MLP, seq)
    cos_t, sin_t = _rope_tables(seq)
    sink_rows, bias = _sink_rows(sinks), _attn_bias()
    q, k, v, u = _fwd_inproj(x, g1, w_in, cos_t, sin_t, tm_big)
    pool = _fwd_pool(u, w_pool, pool_scale, seq)
    attn, lse = _fwd_attn(q, k, v, sink_rows, bias, seq, comm.pool_done(pool))
    w_out, w_up4, w_down4 = comm.rest_of_weights(attn)
    h2, a, da, dx2b, dx1, dx1b, loss, dgf, dg2 = _mlp_fwd_bwd(x, attn, pool, target, w_out, w_up4, w_down4, g2, gf, tm_mlp)
    gw_up4, gw_down4 = _bwd_mlp_wgrads(h2, da, a, dx2b, min(TILE_BIG, x.shape[0]))
    dattn, dpool, delta, gw_out = _bwd_outproj(dx1b, attn, pool, w_out, _head_selector(), tm_big,
                                               comm.mlp_grads_ready(gw_up4, gw_down4))
    dqkv, dsinks = _bwd_attn(q, k, v, dattn, lse, delta, sink_rows, bias, cos_t, sin_t, seq, comm.outproj_done(gw_out))
    du, dwp, dsc = _bwd_pool(u, dpool, w_pool, pool_scale, seq)
    gx, gw_in_t, dg1 = _bwd_inproj(dqkv, du, x, dx1, w_in, g1, tm_big)
    big = (gw_in_t.reshape(N_CHIPS, IN_SHARD, D_MODEL), gw_out.reshape(N_CHIPS, OUT_SHARD, D_MODEL), gw_up4, gw_down4)
    small = (dg1, dsinks, dwp.reshape(N_POOL * POOL_G, POOL_G), dsc, dg2, dgf, loss)
    return gx, big, small


def _sibling_swap(arrs, name):
    n = len(arrs)

    def body(*refs):
        cps = _sibling_copies(refs[:n], refs[n:2 * n], refs[2 * n], refs[2 * n + 1], False)
        for cp in cps:
            cp.start()
        for cp in cps:
            cp.wait()

    return pl.pallas_call(
        body, name=name, out_shape=tuple(jax.ShapeDtypeStruct(a.shape, a.dtype) for a in arrs),
        in_specs=[_HBM] * n, out_specs=tuple([_HBM] * n),
        scratch_shapes=[pltpu.SemaphoreType.DMA((n,)), pltpu.SemaphoreType.DMA((n,))],
        compiler_params=_cp(),
    )(*arrs)


def _sibling_copies(srcs, lands, send, recv, pick_half):
    x, y, c = lax.axis_index("x"), lax.axis_index("y"), lax.axis_index("c")
    cps = []
    for i in range(len(srcs)):
        src = srcs[i]
        if pick_half:
            h = src.shape[1] // 2
            src = src.at[:, pl.ds((1 - c) * h, h)]
        cps.append(pltpu.make_async_remote_copy(
            src_ref=src, dst_ref=lands[i], send_sem=send.at[i], recv_sem=recv.at[i],
            device_id=(x, y, 1 - c), device_id_type=MESH))
    return cps


def _sibling_start(arrs, name, pick_half=True):
    n = len(arrs)

    def body(*refs):
        send, recv, token = refs[3 * n:3 * n + 3]
        for cp in _sibling_copies(refs[:n], refs[2 * n:3 * n], send, recv, pick_half):
            cp.start()
        token[...] = jnp.zeros_like(token)

    half = lambda a: ((a.shape[0], a.shape[1] // 2) + a.shape[2:]) if pick_half else a.shape
    res = pl.pallas_call(
        body, name=name + "_start",
        out_shape=tuple(pltpu.HBM(a.shape, a.dtype) for a in arrs) + tuple(pltpu.HBM(half(a), a.dtype) for a in arrs) + (
            pltpu.SemaphoreType.DMA((n,)), pltpu.SemaphoreType.DMA((n,)), TOKEN),
        in_specs=[_HBM] * n, out_specs=tuple([_HBM] * (2 * n)) + (_SEM, _SEM, pl.BlockSpec(memory_space=pltpu.VMEM)),
        input_output_aliases={i: i for i in range(n)},
        compiler_params=_cp(has_side_effects=_EFFECT),
    )(*[pltpu.with_memory_space_constraint(a, pltpu.HBM) for a in arrs])
    return res[:n], res[n:2 * n], res[2 * n], res[2 * n + 1], res[2 * n + 2]


def _sibling_wait(arrs, lands, send, recv, after, name, pick_half=True):
    n = len(arrs)

    def body(*refs):
        for cp in _sibling_copies(refs[:n], refs[n:2 * n], refs[2 * n], refs[2 * n + 1], pick_half):
            cp.wait_send()
            cp.wait_recv()

    res = pl.pallas_call(
        body, name=name + "_wait", out_shape=tuple(pltpu.HBM(a.shape, a.dtype) for a in list(arrs) + list(lands)),
        in_specs=[_HBM] * (2 * n) + [_SEM, _SEM, _ANY], out_specs=tuple([_HBM] * (2 * n)),
        input_output_aliases={i: i for i in range(2 * n)},
        compiler_params=_cp(has_side_effects=_EFFECT),
    )(*arrs, *lands, send, recv, after)
    return res[:n], res[n:]


def _row_block(rows):
    if rows <= 256:
        return rows
    for cand in (256, 128, 64, 32, 16, 8):
        if rows % cand == 0:
            return cand
    raise ValueError(rows)


def _chip_partial(g4s, r4s, c_arr, name):
    n = len(g4s)
    _, rows, cols = r4s[0].shape
    rb = _row_block(rows)
    nb = rows // rb

    def body(c_ref, *refs):
        for k in range(n):
            refs[2 * n + k][...] = (refs[k][...] + refs[n + k][...]).astype(BF16)

    own = pl.BlockSpec((1, rb, cols), lambda s, i, c: (s, c[0] * nb + i, 0))
    blk = pl.BlockSpec((1, rb, cols), lambda s, i, c: (s, i, 0))
    return pl.pallas_call(
        body, name=name,
        grid_spec=pltpu.PrefetchScalarGridSpec(num_scalar_prefetch=1, grid=(N_CHIPS, nb),
                                               in_specs=[own] * n + [blk] * n, out_specs=tuple([blk] * n)),
        out_shape=tuple(jax.ShapeDtypeStruct(r.shape, BF16) for r in r4s),
        compiler_params=_cp(dimension_semantics=("parallel", "parallel")),
    )(c_arr, *g4s, *r4s)


def _send_start(parts, name):
    n = len(parts)

    def body(*refs):
        srcs, lands = refs[:n], refs[2 * n:3 * n]
        send, recv, token = refs[3 * n:3 * n + 3]
        x, y, c = lax.axis_index("x"), lax.axis_index("y"), lax.axis_index("c")
        for i in range(n):
            for m, chip in enumerate(_other_chips(x, y)):
                pltpu.make_async_remote_copy(
                    src_ref=srcs[i].at[2 * chip[0] + chip[1]], dst_ref=lands[i].at[m],
                    send_sem=send.at[3 * i + m], recv_sem=recv.at[3 * i + m],
                    device_id=(chip[0], chip[1], c), device_id_type=MESH).start()
        token[...] = jnp.zeros_like(token)

    res = pl.pallas_call(
        body, name=name,
        out_shape=tuple(pltpu.HBM(p.shape, p.dtype) for p in parts) + tuple(pltpu.HBM((3,) + p.shape[1:], p.dtype) for p in parts) + (
            pltpu.SemaphoreType.DMA((3 * n,)), pltpu.SemaphoreType.DMA((3 * n,)), TOKEN),
        in_specs=[_HBM] * n, out_specs=tuple([_HBM] * (2 * n)) + (_SEM, _SEM, pl.BlockSpec(memory_space=pltpu.VMEM)),
        input_output_aliases={i: i for i in range(n)},
        compiler_params=_cp(has_side_effects=_EFFECT),
    )(*[pltpu.with_memory_space_constraint(p, pltpu.HBM) for p in parts])
    return res[:n], res[n:2 * n], res[2 * n], res[2 * n + 1], res[2 * n + 2]


def _send_wait(parts, lands, send, recv, after, name):
    n = len(parts)

    def body(*refs):
        srcs, ins = refs[:n], refs[n:2 * n]
        send_ref, recv_ref = refs[2 * n], refs[2 * n + 1]
        x, y, c = lax.axis_index("x"), lax.axis_index("y"), lax.axis_index("c")
        for i in range(n):
            for m, chip in enumerate(_other_chips(x, y)):
                cp = pltpu.make_async_remote_copy(
                    src_ref=srcs[i].at[2 * chip[0] + chip[1]], dst_ref=ins[i].at[m],
                    send_sem=send_ref.at[3 * i + m], recv_sem=recv_ref.at[3 * i + m],
                    device_id=(chip[0], chip[1], c), device_id_type=MESH)
                cp.wait_send()
                cp.wait_recv()

    res = pl.pallas_call(
        body, name=name, out_shape=tuple(pltpu.HBM(a.shape, a.dtype) for a in list(parts) + list(lands)),
        in_specs=[_HBM] * (2 * n) + [_SEM, _SEM, _ANY], out_specs=tuple([_HBM] * (2 * n)),
        input_output_aliases={i: i for i in range(2 * n)},
        compiler_params=_cp(has_side_effects=_EFFECT),
    )(*parts, *lands, send, recv, after)
    return res[n:]


def _final_half(g4s, r4s, got3s, jc_arr, name):
    n = len(g4s)
    _, rows, cols = r4s[0].shape
    rb = _row_block(rows)
    nb = rows // rb

    def body(jc_ref, *refs):
        for k in range(n):
            g_ref, r_ref, p_ref, o_ref = refs[k], refs[n + k], refs[2 * n + k], refs[3 * n + k]
            own = g_ref[0] + r_ref[0]
            o_ref[...] = ((own + p_ref[0].astype(F32)) + p_ref[1].astype(F32)) + p_ref[2].astype(F32)

    own = pl.BlockSpec((1, rb, cols), lambda i, jc: (jc[0], jc[1] * nb + i, 0))
    sib = pl.BlockSpec((1, rb, cols), lambda i, jc: (jc[0], i, 0))
    got = pl.BlockSpec((3, rb, cols), lambda i, jc: (0, i, 0))
    out = pl.BlockSpec((rb, cols), lambda i, jc: (i, 0))
    return pl.pallas_call(
        body, name=name,
        grid_spec=pltpu.PrefetchScalarGridSpec(num_scalar_prefetch=1, grid=(nb,),
                                               in_specs=[own] * n + [sib] * n + [got] * n, out_specs=tuple([out] * n)),
        out_shape=tuple(jax.ShapeDtypeStruct((rows, cols), F32) for _ in range(n)),
        compiler_params=_cp(dimension_semantics=("parallel",)),
    )(jc_arr, *g4s, *r4s, *got3s)


def _adamw_math(w, g, m, v):
    m2 = ADAM_B1 * m + (1.0 - ADAM_B1) * g
    v2 = ADAM_B2 * v + (1.0 - ADAM_B2) * (g * g)
    m_hat = m2 / (1.0 - ADAM_B1 ** ADAM_STEP)
    v_hat = v2 / (1.0 - ADAM_B2 ** ADAM_STEP)
    delta = -ADAM_LR * (m_hat / (jnp.sqrt(v_hat) + ADAM_EPS) + ADAM_WD * w)
    return delta, m2, v2


def _adamw_shard(mines, others, ws, ms, vs, c_arr, name):
    n = len(ws)
    rows, cols = ws[0].shape
    half = rows // 2
    rb = _row_block(half)
    nb = half // rb

    def body(c_ref, *refs):
        for k in range(n):
            a_ref, b_ref, w_ref, m_ref, v_ref = (refs[i * n + k] for i in range(5))
            g_ref, d_ref, m2_ref, v2_ref = refs[5 * n + 4 * k:5 * n + 4 * k + 4]
            g = jnp.where(pl.program_id(0) == c_ref[0], a_ref[...], b_ref[...])
            delta, m2, v2 = _adamw_math(w_ref[...], g, m_ref[...], v_ref[...])
            g_ref[...] = g
            d_ref[...] = delta
            m2_ref[...] = m2
            v2_ref[...] = v2

    hb = pl.BlockSpec((rb, cols), lambda h, i, c: (i, 0))
    fb = pl.BlockSpec((rb, cols), lambda h, i, c: (h * nb + i, 0))
    shp = jax.ShapeDtypeStruct((rows, cols), F32)
    res = pl.pallas_call(
        body, name=name,
        grid_spec=pltpu.PrefetchScalarGridSpec(num_scalar_prefetch=1, grid=(2, nb),
                                               in_specs=[hb] * (2 * n) + [fb] * (3 * n), out_specs=tuple([fb] * (4 * n))),
        out_shape=tuple([shp] * (4 * n)),
        compiler_params=_cp(dimension_semantics=("parallel", "parallel")),
    )(c_arr, *mines, *others, *ws, *ms, *vs)
    return [res[4 * k:4 * k + 4] for k in range(n)]


def _small_allreduce(parts):
    n = len(parts)

    def body(*refs):
        p_refs, accs = refs[:n], refs[n:2 * n]
        bufs = refs[2 * n:3 * n]
        send, recv = refs[3 * n:]
        x, y, c = lax.axis_index("x"), lax.axis_index("y"), lax.axis_index("c")
        partners = [(x, y, 1 - c), (1 - x, y, c), (x, 1 - y, c)]
        other_way = [partners[0], partners[2], partners[1]]
        for i in range(n):
            accs[i][...] = p_refs[i][...]
        for s in range(3):
            cps = []
            for i in range(n):
                rows = parts[i].shape[0]
                pieces = [(slice(None), partners[s])] if rows < 2 * BLK else [
                    (pl.ds(0, rows // 2), partners[s]), (pl.ds(rows // 2, rows // 2), other_way[s])]
                for h, (sl, partner) in enumerate(pieces):
                    cp = pltpu.make_async_remote_copy(
                        src_ref=accs[i].at[sl], dst_ref=bufs[i].at[s, sl], send_sem=send.at[6 * i + 2 * s + h],
                        recv_sem=recv.at[6 * i + 2 * s + h], device_id=partner, device_id_type=MESH)
                    cp.start()
                    cps.append(cp)
            for cp in cps:
                cp.wait()
            for i in range(n):
                accs[i][...] = accs[i][...] + bufs[i][s]

    vmem = pl.BlockSpec(memory_space=pltpu.VMEM)
    return pl.pallas_call(
        body, name="small_allreduce", out_shape=tuple(jax.ShapeDtypeStruct(p.shape, F32) for p in parts),
        in_specs=[vmem] * n, out_specs=tuple([vmem] * n),
        scratch_shapes=[pltpu.VMEM((3,) + p.shape, F32) for p in parts] + [
            pltpu.SemaphoreType.DMA((6 * n,)), pltpu.SemaphoreType.DMA((6 * n,))],
        compiler_params=_cp(),
    )(*parts)


def _small_adamw(reduced, params):
    n = len(reduced)
    n_w = len(params)

    def body(*refs):
        r_refs = refs[:n]
        wmv = refs[n:n + 3 * n_w]
        outs = refs[n + 3 * n_w:]
        outs[0][...] = r_refs[n - 1][0:1, 0:1]
        grads = [r_refs[0][...], r_refs[1][0:1, 0:N_Q_HEADS]] + [r_refs[i][...] for i in range(2, n_w)]
        for i in range(n_w):
            w_ref, m_ref, v_ref = wmv[3 * i:3 * i + 3]
            g_ref, d_ref, m2_ref, v2_ref = outs[1 + 4 * i:5 + 4 * i]
            delta, m2, v2 = _adamw_math(w_ref[...], grads[i], m_ref[...], v_ref[...])
            g_ref[...] = grads[i]
            d_ref[...] = delta
            m2_ref[...] = m2
            v2_ref[...] = v2

    flat = [a for p in params for a in p]
    vmem = pl.BlockSpec(memory_space=pltpu.VMEM)
    out_shape = [jax.ShapeDtypeStruct((1, 1), F32)]
    for p in params:
        out_shape += [jax.ShapeDtypeStruct(p[0].shape, F32)] * 4
    res = pl.pallas_call(
        body, name="small_adamw", out_shape=tuple(out_shape),
        in_specs=[vmem] * (n + len(flat)), out_specs=tuple([vmem] * len(out_shape)),
        compiler_params=_cp(),
    )(*reduced, *flat)
    return res[0], [res[1 + 4 * i:5 + 4 * i] for i in range(n_w)]


def kernel(x, attn_norm_g, w_in, attn_sinks, w_pool, pool_scale, w_out, mlp_norm_g, w_up, w_down, final_norm_g, loss_target, m_attn_norm_g, m_w_in, m_attn_sinks, m_w_pool, m_pool_scale, m_w_out, m_mlp_norm_g, m_w_up, m_w_down, m_final_norm_g, v_attn_norm_g, v_w_in, v_attn_sinks, v_w_pool, v_pool_scale, v_w_out, v_mlp_norm_g, v_w_up, v_w_down, v_final_norm_g):
    nseq, seq, d = x.shape
    c_idx = lax.axis_index("c").astype(jnp.int32)
    j_idx = (2 * lax.axis_index("x") + lax.axis_index("y")).astype(jnp.int32)
    c_arr = jnp.reshape(c_idx, (1,))
    jc_arr = jnp.stack([j_idx, c_idx])

    big_w = (w_in[0].T, w_out[0], w_up[0], w_down[0])
    big_m = (m_w_in[0].T, m_w_out[0], m_w_up[0], m_w_down[0])
    big_v = (v_w_in[0].T, v_w_out[0], v_w_up[0], v_w_down[0])
    (w_in_lands, w_in_send, w_in_recv), rest = _gather_start(big_w[:1], big_w[1:])
    (w_in4,) = _forward_now(_gather_wait(w_in_lands, w_in_send, w_in_recv, "gather_wait_w_in"), "forward_w_in")
    w_in_full = w_in4.reshape(IN_W, D_MODEL)
    class Comm(_NoComm):
        def pool_done(self, pool):
            self.lands, self.relay_sems, self.near_sems, token = _gather_relay(*rest, pool)
            return (token,)

        def rest_of_weights(self, attn):
            lands, far_sems = _gather_relayed(self.lands, self.relay_sems, attn)
            w_out4, w_up4, w_down4 = _gather_done(lands, self.near_sems, far_sems)
            return w_out4.reshape(D_MODEL, D_MODEL), w_up4, w_down4

        def mlp_grads_ready(self, gw_up4, gw_down4):
            self.mlp = _sibling_start((gw_up4, gw_down4), "mlp_grads_to_sibling")
            return (self.mlp[4],)

        def outproj_done(self, gw_out):
            grads, lands, send, recv, _ = self.mlp
            sib_out = _sibling_start((gw_out.reshape(N_CHIPS, OUT_SHARD, D_MODEL),), "w_out_grad_to_sibling")
            mlp_grads, mlp_from_sib = _sibling_wait(grads, lands, send, recv, sib_out[4], "mlp_grads_to_sibling")
            partials = _chip_partial(mlp_grads, mlp_from_sib, c_arr, "chip_partial_mlp")
            out_grad, out_from_sib = _sibling_wait(*sib_out[:4], partials[1], "w_out_grad_to_sibling")
            partials = tuple(_chip_partial(out_grad, out_from_sib, c_arr, "chip_partial_w_out")) + tuple(partials)
            self.grads = tuple(out_grad) + tuple(mlp_grads)
            self.from_sib = tuple(out_from_sib) + tuple(mlp_from_sib)
            self.parts, self.part_lands, self.part_send, self.part_recv, token = _send_start(partials, "send_start")
            return (token,)

    comm = Comm()
    gx, big_g, small_g = _local_step(
        x.reshape(nseq * seq, d), loss_target.reshape(nseq * seq, d), w_in_full, comm,
        attn_norm_g, attn_sinks.reshape(N_Q_HEADS), w_pool[0], pool_scale, mlp_norm_g, final_norm_g.reshape(1, d), seq)
    big_g = tuple(big_g[:1]) + comm.grads

    def adamw(idx, mines, others, name):
        pick = lambda group: [group[i] for i in idx]
        return _adamw_shard(mines, others, pick(big_w), pick(big_m), pick(big_v), c_arr, name)

    sib_in = _sibling_start(big_g[:1], "w_in_grad_to_sibling")
    received = _send_wait(comm.parts, comm.part_lands, comm.part_send, comm.part_recv, sib_in[4], "send_wait")
    mine = list(_final_half(big_g[1:2], comm.from_sib[:1], received[:1], jc_arr, "final_half_w_out"))
    mine += list(_final_half(big_g[2:], comm.from_sib[1:], received[1:], jc_arr, "final_half_mlp"))
    halves = _sibling_start(mine, "halves_to_sibling", False)
    g_in, r_in = _sibling_wait(*sib_in[:4], halves[4], "w_in_grad_to_sibling")
    send_in = _send_start(_chip_partial(g_in, r_in, c_arr, "chip_partial_w_in"), "send_start_w_in")
    mine, other = _sibling_wait(*halves[:4], send_in[4], "halves_to_sibling", False)
    (out_res,) = adamw((1,), mine[:1], other[:1], "adamw_w_out")
    up_res, down_res = adamw((2, 3), mine[1:], other[1:], "adamw_mlp")
    p_in = _send_wait(*send_in[:4], up_res[0], "send_wait_w_in")
    mine_in = _final_half(g_in, r_in, p_in, jc_arr, "final_half_w_in")
    other_in = _sibling_swap(mine_in, "w_in_half_to_sibling")
    (in_res,) = adamw((0,), mine_in, other_in, "adamw_w_in")
    big_out = [[t.T for t in in_res], out_res, up_res, down_res]

    wp_flat = lambda a: a.reshape(N_POOL * POOL_G, POOL_G)
    small_params = [
        (attn_norm_g, m_attn_norm_g, v_attn_norm_g),
        (attn_sinks, m_attn_sinks, v_attn_sinks),
        (wp_flat(w_pool), wp_flat(m_w_pool), wp_flat(v_w_pool)),
        (pool_scale, m_pool_scale, v_pool_scale),
        (mlp_norm_g, m_mlp_norm_g, v_mlp_norm_g),
        (final_norm_g.reshape(1, d), m_final_norm_g.reshape(1, d), v_final_norm_g.reshape(1, d)),
    ]
    loss, small_out = _small_adamw(_small_allreduce(small_g), small_params)

    weights = (attn_norm_g, w_in, attn_sinks, w_pool, pool_scale, w_out, mlp_norm_g, w_up, w_down, final_norm_g)
    order = [small_out[0], big_out[0], small_out[1], small_out[2], small_out[3], big_out[1], small_out[4], big_out[2],
             big_out[3], small_out[5]]
    outs = [loss.reshape(()), gx.reshape(nseq, seq, d)]
    for kind in range(4):
        outs += [res[kind].reshape(w.shape) for res, w in zip(order, weights)]
    return tuple(outs)
```
